```python
import jax, jax.numpy as jnp
from jax import lax
import numpy as np

D_MODEL = 1024
BATCH = 8
SEQ = 8192
DEPTH = 2

EPS = 1e-6
NEG_INF = -1e30
CHUNK = 128
A_GROUPS = 4
A_WIDTH = D_MODEL
A_GROUP_DIM = A_WIDTH // A_GROUPS
B_WIDTH = D_MODEL
POOL_WINDOWS = (2, 4, 8, 16)
B_GROUPS = len(POOL_WINDOWS)
B_GROUP_DIM = B_WIDTH // B_GROUPS
MIX0_IN = 3 * A_WIDTH + 2 * B_WIDTH
MIX0_OUT = A_WIDTH + B_WIDTH
N_HEADS = 16
N_KV_HEADS = 4
HEAD_DIM = 64
GQA_GROUP = N_HEADS // N_KV_HEADS
C_WIDTH = N_HEADS * HEAD_DIM
KV_WIDTH = N_KV_HEADS * HEAD_DIM
MIX1_IN = 2 * C_WIDTH + 2 * KV_WIDTH
WINDOW = 128
ATTN_BLOCK = 128
ROPE_THETA = 500000.0
ROT_DIM = HEAD_DIM // 4

kernel_name = "hybrid_gmlp_pool_swa_encoder"


def rms_norm(x, g):
    xf = x.astype(jnp.float32)
    y = xf * lax.rsqrt(jnp.mean(xf * xf, axis=-1, keepdims=True) + EPS)
    return (y * g.astype(jnp.float32)).astype(x.dtype)


def spatial_gating_mixer(u, v, w_s, b_s, g_v):
    bn, s, _ = u.shape
    u = jax.nn.gelu(u)
    v = rms_norm(jax.nn.gelu(v), g_v)
    v = v.reshape(bn, s // CHUNK, CHUNK, A_GROUPS, A_GROUP_DIM)
    mixed = jnp.einsum('hij,bcjhd->bcihd', w_s, v) + b_s.T[None, None, :, :, None]
    return u * mixed.reshape(bn, s, A_WIDTH)


def multiscale_pool_mixer(xb, w_g, scale):
    bn, s, _ = xb.shape
    xf = xb.astype(jnp.float32)
    cs = jnp.concatenate([jnp.zeros((bn, 1, B_WIDTH), jnp.float32), lax.cumsum(xf, axis=1)], axis=1)
    t = jnp.arange(s)
    outs = []
    for gi, w in enumerate(POOL_WINDOWS):
        lo = jnp.clip(t - w // 2, 0, s)
        hi = jnp.clip(t + w // 2, 0, s)
        sl = slice(gi * B_GROUP_DIM, (gi + 1) * B_GROUP_DIM)
        csg = cs[:, :, sl]
        win_sum = jnp.take(csg, hi, axis=1) - jnp.take(csg, lo, axis=1)
        mean = win_sum / (hi - lo).astype(jnp.float32)[None, :, None]
        outs.append(mean - xf[:, :, sl])
    p = jnp.stack(outs, axis=2).astype(xb.dtype)
    y = jnp.einsum('bsgd,gde->bsge', p, w_g).reshape(bn, s, B_WIDTH)
    return y * scale


def partial_rope(x, pos):
    inv = ROPE_THETA ** (-jnp.arange(0, ROT_DIM, 2, dtype=jnp.float32) / ROT_DIM)
    ang = pos.astype(jnp.float32)[:, None] * inv[None, :]
    cos = jnp.cos(ang)[None, :, None, :]
    sin = jnp.sin(ang)[None, :, None, :]
    xr = x[..., :ROT_DIM].astype(jnp.float32)
    x1, x2 = xr[..., :ROT_DIM // 2], xr[..., ROT_DIM // 2:]
    rot = jnp.concatenate([x1 * cos - x2 * sin, x2 * cos + x1 * sin], axis=-1)
    return jnp.concatenate([rot.astype(x.dtype), x[..., ROT_DIM:]], axis=-1)


def windowed_gqa(q, k, v, sink):
    bn, s = q.shape[:2]
    nb = s // ATTN_BLOCK
    qb = q.reshape(bn, nb, ATTN_BLOCK, N_KV_HEADS, GQA_GROUP, HEAD_DIM)

    def band(t):
        tp = jnp.pad(t, ((0, 0), (ATTN_BLOCK, ATTN_BLOCK), (0, 0), (0, 0)))
        tp = tp.reshape(bn, nb + 2, ATTN_BLOCK, N_KV_HEADS, HEAD_DIM)
        return jnp.concatenate([tp[:, :nb], tp[:, 1:nb + 1], tp[:, 2:]], axis=2)

    kb, vb = band(k), band(v)
    scores = jnp.einsum('bnqkgd,bnskd->bnkgqs', qb, kb).astype(jnp.float32) * (HEAD_DIM ** -0.5)
    qi = jnp.arange(nb)[:, None, None] * ATTN_BLOCK + jnp.arange(ATTN_BLOCK)[None, :, None]
    kj = (jnp.arange(nb)[:, None, None] - 1) * ATTN_BLOCK + jnp.arange(3 * ATTN_BLOCK)[None, None, :]
    allowed = (jnp.abs(qi - kj) <= WINDOW) & (kj >= 0) & (kj < s)
    scores = jnp.where(allowed[None, :, None, None], scores, NEG_INF)
    sink_col = jnp.broadcast_to(sink.astype(jnp.float32).reshape(1, 1, N_KV_HEADS, GQA_GROUP, 1, 1),
                                scores.shape[:-1] + (1,))
    probs = jax.nn.softmax(jnp.concatenate([scores, sink_col], axis=-1), axis=-1)[..., :-1]
    out = jnp.einsum('bnkgqs,bnskd->bnqkgd', probs.astype(v.dtype), vb)
    return out.reshape(bn, s, C_WIDTH)


def _fwd_setup_inputs(seed: int = 0) -> dict:
    key = jax.random.key(seed)
    ks = jax.random.split(key, 16)
    f32 = jnp.float32
    nrm = lambda k, shp, sc: jax.random.normal(k, shp, f32) * sc
    return {
        "x": nrm(ks[0], (BATCH, SEQ, D_MODEL), 1.0),
        "norm_0": 1.0 + nrm(ks[1], (D_MODEL,), 0.05),
        "w_in_0": nrm(ks[2], (D_MODEL, MIX0_IN), D_MODEL ** -0.5),
        "a_v_norm_0": 1.0 + nrm(ks[3], (A_WIDTH,), 0.05),
        "a_spatial_w_0": nrm(ks[4], (A_GROUPS, CHUNK, CHUNK), CHUNK ** -0.5),
        "a_spatial_b_0": 1.0 + nrm(ks[5], (A_GROUPS, CHUNK), 0.1),
        "b_group_w_0": nrm(ks[6], (B_GROUPS, B_GROUP_DIM, B_GROUP_DIM), B_GROUP_DIM ** -0.5),
        "b_scale_0": 1.0 + nrm(ks[7], (B_WIDTH,), 0.05),
        "w_out_0": nrm(ks[8], (MIX0_OUT, D_MODEL), MIX0_OUT ** -0.5),
        "norm_1": 1.0 + nrm(ks[9], (D_MODEL,), 0.05),
        "w_in_1": nrm(ks[10], (D_MODEL, MIX1_IN), D_MODEL ** -0.5),
        "sink_1": nrm(ks[11], (N_HEADS,), 0.5),
        "w_out_1": nrm(ks[12], (C_WIDTH, D_MODEL), C_WIDTH ** -0.5),
        "final_norm": 1.0 + nrm(ks[13], (D_MODEL,), 0.05),
    }


def _fwd_reference(x, norm_0, w_in_0, a_v_norm_0, a_spatial_w_0, a_spatial_b_0, b_group_w_0, b_scale_0,
              w_out_0, norm_1, w_in_1, sink_1, w_out_1, final_norm):
    bn, s, _ = x.shape
    pos = jnp.arange(s)
    for layer in range(DEPTH):
        if layer % 2 == 0:
            h = rms_norm(x, norm_0)
            z = h @ w_in_0
            a_u, a_v, a_gate, b_x, b_gate = jnp.split(
                z, [A_WIDTH, 2 * A_WIDTH, 3 * A_WIDTH, 3 * A_WIDTH + B_WIDTH], axis=-1)
            ya = spatial_gating_mixer(a_u, a_v, a_spatial_w_0, a_spatial_b_0, a_v_norm_0) * jax.nn.silu(a_gate)
            yb = multiscale_pool_mixer(b_x, b_group_w_0, b_scale_0) * jax.nn.silu(b_gate)
            x = x + jnp.concatenate([ya, yb], axis=-1) @ w_out_0
        else:
            h = rms_norm(x, norm_1)
            z = h @ w_in_1
            q, k, v, gate = jnp.split(z, [C_WIDTH, C_WIDTH + KV_WIDTH, C_WIDTH + 2 * KV_WIDTH], axis=-1)
            q = partial_rope(q.reshape(bn, s, N_HEADS, HEAD_DIM), pos)
            k = partial_rope(k.reshape(bn, s, N_KV_HEADS, HEAD_DIM), pos)
            v = v.reshape(bn, s, N_KV_HEADS, HEAD_DIM)
            y = windowed_gqa(q, k, v, sink_1) * jax.nn.silu(gate)
            x = x + y @ w_out_1
    return rms_norm(x, final_norm)


import jax as _jax
import jax.numpy as _jnp

TWIN_FORMAT = 'train_step'
FWD_PARAMS = ['x', 'norm_0', 'w_in_0', 'a_v_norm_0', 'a_spatial_w_0', 'a_spatial_b_0', 'b_group_w_0', 'b_scale_0', 'w_out_0', 'norm_1', 'w_in_1', 'sink_1', 'w_out_1', 'final_norm']
TWIN_WEIGHTS = ['norm_0', 'w_in_0', 'a_v_norm_0', 'a_spatial_w_0', 'a_spatial_b_0', 'b_group_w_0', 'b_scale_0', 'w_out_0', 'norm_1', 'w_in_1', 'sink_1', 'w_out_1', 'final_norm']
TWIN_DIFF_INPUT = 'x'
TWIN_INPUTS = ['x', 'norm_0', 'w_in_0', 'a_v_norm_0', 'a_spatial_w_0', 'a_spatial_b_0', 'b_group_w_0', 'b_scale_0', 'w_out_0', 'norm_1', 'w_in_1', 'sink_1', 'w_out_1', 'final_norm', 'loss_target', 'm_norm_0', 'm_w_in_0', 'm_a_v_norm_0', 'm_a_spatial_w_0', 'm_a_spatial_b_0', 'm_b_group_w_0', 'm_b_scale_0', 'm_w_out_0', 'm_norm_1', 'm_w_in_1', 'm_sink_1', 'm_w_out_1', 'm_final_norm', 'v_norm_0', 'v_w_in_0', 'v_a_v_norm_0', 'v_a_spatial_w_0', 'v_a_spatial_b_0', 'v_b_group_w_0', 'v_b_scale_0', 'v_w_out_0', 'v_norm_1', 'v_w_in_1', 'v_sink_1', 'v_w_out_1', 'v_final_norm']
TWIN_OUTPUTS = ['loss', 'grad_x', 'grad_norm_0', 'grad_w_in_0', 'grad_a_v_norm_0', 'grad_a_spatial_w_0', 'grad_a_spatial_b_0', 'grad_b_group_w_0', 'grad_b_scale_0', 'grad_w_out_0', 'grad_norm_1', 'grad_w_in_1', 'grad_sink_1', 'grad_w_out_1', 'grad_final_norm', 'delta_norm_0', 'delta_w_in_0', 'delta_a_v_norm_0', 'delta_a_spatial_w_0', 'delta_a_spatial_b_0', 'delta_b_group_w_0', 'delta_b_scale_0', 'delta_w_out_0', 'delta_norm_1', 'delta_w_in_1', 'delta_sink_1', 'delta_w_out_1', 'delta_final_norm', 'new_m_norm_0', 'new_m_w_in_0', 'new_m_a_v_norm_0', 'new_m_a_spatial_w_0', 'new_m_a_spatial_b_0', 'new_m_b_group_w_0', 'new_m_b_scale_0', 'new_m_w_out_0', 'new_m_norm_1', 'new_m_w_in_1', 'new_m_sink_1', 'new_m_w_out_1', 'new_m_final_norm', 'new_v_norm_0', 'new_v_w_in_0', 'new_v_a_v_norm_0', 'new_v_a_spatial_w_0', 'new_v_a_spatial_b_0', 'new_v_b_group_w_0', 'new_v_b_scale_0', 'new_v_w_out_0', 'new_v_norm_1', 'new_v_w_in_1', 'new_v_sink_1', 'new_v_w_out_1', 'new_v_final_norm']
TWIN_LEAF_KINDS = {'loss': 'loss', 'grad_x': 'grad_x', 'grad_norm_0': 'grad_w', 'grad_w_in_0': 'grad_w', 'grad_a_v_norm_0': 'grad_w', 'grad_a_spatial_w_0': 'grad_w', 'grad_a_spatial_b_0': 'grad_w', 'grad_b_group_w_0': 'grad_w', 'grad_b_scale_0': 'grad_w', 'grad_w_out_0': 'grad_w', 'grad_norm_1': 'grad_w', 'grad_w_in_1': 'grad_w', 'grad_sink_1': 'grad_w', 'grad_w_out_1': 'grad_w', 'grad_final_norm': 'grad_w', 'delta_norm_0': 'delta_w', 'delta_w_in_0': 'delta_w', 'delta_a_v_norm_0': 'delta_w', 'delta_a_spatial_w_0': 'delta_w', 'delta_a_spatial_b_0': 'delta_w', 'delta_b_group_w_0': 'delta_w', 'delta_b_scale_0': 'delta_w', 'delta_w_out_0': 'delta_w', 'delta_norm_1': 'delta_w', 'delta_w_in_1': 'delta_w', 'delta_sink_1': 'delta_w', 'delta_w_out_1': 'delta_w', 'delta_final_norm': 'delta_w', 'new_m_norm_0': 'new_m', 'new_m_w_in_0': 'new_m', 'new_m_a_v_norm_0': 'new_m', 'new_m_a_spatial_w_0': 'new_m', 'new_m_a_spatial_b_0': 'new_m', 'new_m_b_group_w_0': 'new_m', 'new_m_b_scale_0': 'new_m', 'new_m_w_out_0': 'new_m', 'new_m_norm_1': 'new_m', 'new_m_w_in_1': 'new_m', 'new_m_sink_1': 'new_m', 'new_m_w_out_1': 'new_m', 'new_m_final_norm': 'new_m', 'new_v_norm_0': 'new_v', 'new_v_w_in_0': 'new_v', 'new_v_a_v_norm_0': 'new_v', 'new_v_a_spatial_w_0': 'new_v', 'new_v_a_spatial_b_0': 'new_v', 'new_v_b_group_w_0': 'new_v', 'new_v_b_scale_0': 'new_v', 'new_v_w_out_0': 'new_v', 'new_v_norm_1': 'new_v', 'new_v_w_in_1': 'new_v', 'new_v_sink_1': 'new_v', 'new_v_w_out_1': 'new_v', 'new_v_final_norm': 'new_v'}


def _forward(args):
    return _fwd_reference(*[args[k] for k in FWD_PARAMS])


def _output_shape():
    def fwd():
        inp = _fwd_setup_inputs(0)
        return _fwd_reference(*[inp[k] for k in FWD_PARAMS])
    out = _jax.eval_shape(fwd)
    return out.shape, out.dtype

N_MICROBATCH = 1
ADAM_LR = 0.001
ADAM_B1 = 0.9
ADAM_B2 = 0.999
ADAM_EPS = 1e-08
ADAM_WD = 0.01
ADAM_STEP = 10
PER_EXAMPLE_BATCH_AXIS = {'x': 0, 'loss_target': 0}
SHARED_INPUTS = []
_WEIGHT_DTYPES = {'norm_0': _jnp.float32, 'w_in_0': _jnp.float32, 'a_v_norm_0': _jnp.float32, 'a_spatial_w_0': _jnp.float32, 'a_spatial_b_0': _jnp.float32, 'b_group_w_0': _jnp.float32, 'b_scale_0': _jnp.float32, 'w_out_0': _jnp.float32, 'norm_1': _jnp.float32, 'w_in_1': _jnp.float32, 'sink_1': _jnp.float32, 'w_out_1': _jnp.float32, 'final_norm': _jnp.float32}
MOMENT_SCALE = {'norm_0': 1.843142e-01, 'w_in_0': 8.295204e-02, 'a_v_norm_0': 7.013180e-02, 'a_spatial_w_0': 8.711906e-02, 'a_spatial_b_0': 8.933377e-02, 'b_group_w_0': 8.317184e-02, 'b_scale_0': 8.969663e-02, 'w_out_0': 1.217709e-01, 'norm_1': 3.287364e-02, 'w_in_1': 2.083655e-02, 'sink_1': 6.293616e-04, 'w_out_1': 1.684219e-02, 'final_norm': 6.415082e+01}


def _to_microbatches(a, axis):
    t = _jnp.moveaxis(a, axis, 0)
    t = t.reshape((N_MICROBATCH, t.shape[0] // N_MICROBATCH) + t.shape[1:])
    return _jnp.moveaxis(t, 1, axis + 1)


def setup_inputs(seed: int = 0) -> dict:
    inp = _fwd_setup_inputs(seed)
    key = _jax.random.fold_in(_jax.random.key(seed), 7919)
    shape, _ = _output_shape()
    out = dict(inp)
    out["loss_target"] = _jax.random.normal(_jax.random.fold_in(key, 0), shape, _jnp.float32)
    for i, name in enumerate(TWIN_WEIGHTS):
        w = inp[name].astype(_jnp.float32)
        if MOMENT_SCALE is None:
            s = _jnp.sqrt(_jnp.mean(_jnp.square(w)) + 1e-30)
        else:
            s = MOMENT_SCALE[name]
        km, kv = _jax.random.split(_jax.random.fold_in(key, i + 1))
        out[name] = w
        out["m_" + name] = s * _jax.random.normal(km, w.shape, _jnp.float32)
        out["v_" + name] = (s * s) * _jax.random.uniform(kv, w.shape, _jnp.float32, 0.5, 1.5)
    if N_MICROBATCH > 1:
        for name, axis in PER_EXAMPLE_BATCH_AXIS.items():
            out[name] = _to_microbatches(out[name], axis)
    return {'x': out['x'], 'norm_0': out['norm_0'], 'w_in_0': out['w_in_0'], 'a_v_norm_0': out['a_v_norm_0'], 'a_spatial_w_0': out['a_spatial_w_0'], 'a_spatial_b_0': out['a_spatial_b_0'], 'b_group_w_0': out['b_group_w_0'], 'b_scale_0': out['b_scale_0'], 'w_out_0': out['w_out_0'], 'norm_1': out['norm_1'], 'w_in_1': out['w_in_1'], 'sink_1': out['sink_1'], 'w_out_1': out['w_out_1'], 'final_norm': out['final_norm'], 'loss_target': out['loss_target'], 'm_norm_0': out['m_norm_0'], 'm_w_in_0': out['m_w_in_0'], 'm_a_v_norm_0': out['m_a_v_norm_0'], 'm_a_spatial_w_0': out['m_a_spatial_w_0'], 'm_a_spatial_b_0': out['m_a_spatial_b_0'], 'm_b_group_w_0': out['m_b_group_w_0'], 'm_b_scale_0': out['m_b_scale_0'], 'm_w_out_0': out['m_w_out_0'], 'm_norm_1': out['m_norm_1'], 'm_w_in_1': out['m_w_in_1'], 'm_sink_1': out['m_sink_1'], 'm_w_out_1': out['m_w_out_1'], 'm_final_norm': out['m_final_norm'], 'v_norm_0': out['v_norm_0'], 'v_w_in_0': out['v_w_in_0'], 'v_a_v_norm_0': out['v_a_v_norm_0'], 'v_a_spatial_w_0': out['v_a_spatial_w_0'], 'v_a_spatial_b_0': out['v_a_spatial_b_0'], 'v_b_group_w_0': out['v_b_group_w_0'], 'v_b_scale_0': out['v_b_scale_0'], 'v_w_out_0': out['v_w_out_0'], 'v_norm_1': out['v_norm_1'], 'v_w_in_1': out['v_w_in_1'], 'v_sink_1': out['v_sink_1'], 'v_w_out_1': out['v_w_out_1'], 'v_final_norm': out['v_final_norm']}


def _loss(weights, diff, rest, loss_target):
    with _jax.named_scope("forward"):
        args = {**rest, TWIN_DIFF_INPUT: diff, **{k: w.astype(_WEIGHT_DTYPES[k]) for k, w in weights.items()}}
        y = _forward(args)
    with _jax.named_scope("loss_head"):
        err = _jnp.square(y.astype(_jnp.float32) - loss_target)
        return 0.5 * _jnp.sum(_jnp.mean(err, axis=-1)) if err.ndim else 0.5 * err


def _adamw(w, g, m, v):
    m = ADAM_B1 * m + (1.0 - ADAM_B1) * g
    v = ADAM_B2 * v + (1.0 - ADAM_B2) * _jnp.square(g)
    m_hat = m / (1.0 - ADAM_B1 ** ADAM_STEP)
    v_hat = v / (1.0 - ADAM_B2 ** ADAM_STEP)
    delta = -ADAM_LR * (m_hat / (_jnp.sqrt(v_hat) + ADAM_EPS) + ADAM_WD * w)
    return delta, m, v


def reference(x, norm_0, w_in_0, a_v_norm_0, a_spatial_w_0, a_spatial_b_0, b_group_w_0, b_scale_0, w_out_0, norm_1, w_in_1, sink_1, w_out_1, final_norm, loss_target, m_norm_0, m_w_in_0, m_a_v_norm_0, m_a_spatial_w_0, m_a_spatial_b_0, m_b_group_w_0, m_b_scale_0, m_w_out_0, m_norm_1, m_w_in_1, m_sink_1, m_w_out_1, m_final_norm, v_norm_0, v_w_in_0, v_a_v_norm_0, v_a_spatial_w_0, v_a_spatial_b_0, v_b_group_w_0, v_b_scale_0, v_w_out_0, v_norm_1, v_w_in_1, v_sink_1, v_w_out_1, v_final_norm):
    given = dict(x=x, norm_0=norm_0, w_in_0=w_in_0, a_v_norm_0=a_v_norm_0, a_spatial_w_0=a_spatial_w_0, a_spatial_b_0=a_spatial_b_0, b_group_w_0=b_group_w_0, b_scale_0=b_scale_0, w_out_0=w_out_0, norm_1=norm_1, w_in_1=w_in_1, sink_1=sink_1, w_out_1=w_out_1, final_norm=final_norm, loss_target=loss_target, m_norm_0=m_norm_0, m_w_in_0=m_w_in_0, m_a_v_norm_0=m_a_v_norm_0, m_a_spatial_w_0=m_a_spatial_w_0, m_a_spatial_b_0=m_a_spatial_b_0, m_b_group_w_0=m_b_group_w_0, m_b_scale_0=m_b_scale_0, m_w_out_0=m_w_out_0, m_norm_1=m_norm_1, m_w_in_1=m_w_in_1, m_sink_1=m_sink_1, m_w_out_1=m_w_out_1, m_final_norm=m_final_norm, v_norm_0=v_norm_0, v_w_in_0=v_w_in_0, v_a_v_norm_0=v_a_v_norm_0, v_a_spatial_w_0=v_a_spatial_w_0, v_a_spatial_b_0=v_a_spatial_b_0, v_b_group_w_0=v_b_group_w_0, v_b_scale_0=v_b_scale_0, v_w_out_0=v_w_out_0, v_norm_1=v_norm_1, v_w_in_1=v_w_in_1, v_sink_1=v_sink_1, v_w_out_1=v_w_out_1, v_final_norm=v_final_norm)
    weights = {n: given[n] for n in TWIN_WEIGHTS}
    shared = {n: given[n] for n in SHARED_INPUTS}
    per_example = {n: given[n] for n in ['x']}
    grad_fn = _jax.value_and_grad(_loss, argnums=(0, 1))

    def one_microbatch(ex, loss_target):
        ex = dict(ex)
        diff = ex.pop(TWIN_DIFF_INPUT)
        return grad_fn(weights, diff, {**shared, **ex}, loss_target)

    if N_MICROBATCH == 1:
        loss, (grad_w, grad_x) = one_microbatch(per_example, given["loss_target"])
    else:
        def body(carry, xs):
            loss_sum, grad_sum = carry
            l_k, (gw_k, gx_k) = one_microbatch(xs[0], xs[1])
            with _jax.named_scope("update"):
                return (loss_sum + l_k, _jax.tree.map(_jnp.add, grad_sum, gw_k)), gx_k

        init = (_jnp.zeros((), _jnp.float32), _jax.tree.map(_jnp.zeros_like, weights))
        (loss, grad_w), grad_x = _jax.lax.scan(body, init, (per_example, given["loss_target"]))
    with _jax.named_scope("update"):
        delta_w, new_m, new_v = {}, {}, {}
        for n in TWIN_WEIGHTS:
            delta_w[n], new_m[n], new_v[n] = _adamw(weights[n], grad_w[n], given["m_" + n], given["v_" + n])
    return (loss, grad_x, *[grad_w[n] for n in TWIN_WEIGHTS], *[delta_w[n] for n in TWIN_WEIGHTS],
            *[new_m[n] for n in TWIN_WEIGHTS], *[new_v[n] for n in TWIN_WEIGHTS])
```

```python
import functools

import numpy as np
import jax
import jax.numpy as jnp
from jax import lax
from jax.experimental import pallas as pl
from jax.experimental.pallas import tpu as pltpu

F32 = jnp.float32
BF16 = jnp.bfloat16
MESH = pl.DeviceIdType.MESH

D_MODEL = 1024
EPS = 1e-6
NEG_INF = -1e30
CHUNK = 128
POOL_WINDOWS = (2, 4, 8, 16)
HALO = 16
N_HEADS = 16
HEAD_DIM = 64
ATTN_WINDOW = 128
ROPE_THETA = 500000.0
ROT_DIM = 16
ADAM_LR = 0.001
ADAM_B1 = 0.9
ADAM_B2 = 0.999
ADAM_EPS = 1e-08
ADAM_WD = 0.01
ADAM_STEP = 10

TOK = 256
VMEM_LIMIT = 56 * 1024 * 1024


def _params(**kw):
    return pltpu.CompilerParams(vmem_limit_bytes=VMEM_LIMIT, **kw)


def _whole(shape):
    nd = len(shape)
    return pl.BlockSpec(shape, lambda *_: (0,) * nd)


def _rows(t, n):
    return pl.BlockSpec((t, n), lambda i: (i, 0))


ANY = pl.BlockSpec(memory_space=pl.ANY)

_G0 = 0.7978845608028654
_G1 = 0.044715


def _gelu(x):
    return 0.5 * x * (1.0 + jnp.tanh(_G0 * (x + _G1 * x * x * x)))


def _dgelu(x):
    t = jnp.tanh(_G0 * (x + _G1 * x * x * x))
    return 0.5 * (1.0 + t) + 0.5 * x * (1.0 - t * t) * (_G0 * (1.0 + 3.0 * _G1 * x * x))


def _sigmoid(x):
    return 1.0 / (1.0 + jnp.exp(-x))


def _dot(a, b):
    return jnp.dot(a, b, preferred_element_type=F32)


def _dot_nt(a, b):
    return lax.dot_general(a, b, (((1,), (1,)), ((), ())), preferred_element_type=F32)


def _dot_tn(a, b):
    return lax.dot_general(a, b, (((0,), (0,)), ((), ())), preferred_element_type=F32)


def _rms_fwd(x, g):
    r = lax.rsqrt(jnp.mean(x * x, axis=-1, keepdims=True) + EPS)
    xh = x * r
    return r, xh, xh * g


def _rms_bwd(dy, g, r, xh):
    dxh = dy * g
    return r * (dxh - xh * jnp.mean(dxh * xh, axis=-1, keepdims=True))


def _band_matrices(t):
    r = np.arange(t)[:, None]
    j = np.arange(t + 2 * HALO)[None, :]
    fwd, bwd = [], []
    for w in POOL_WINDOWS:
        d = j - r - HALO
        fwd.append((d >= -(w // 2)) & (d < w // 2))
        bwd.append((d >= -(w // 2) + 1) & (d <= w // 2))
    return (jnp.asarray(np.stack(fwd), BF16), jnp.asarray(np.stack(bwd), BF16))


def _window_counts(i, t, s):
    tok = i * t + lax.broadcasted_iota(jnp.int32, (t, 1), 0)
    out = []
    for w in POOL_WINDOWS:
        cnt = jnp.minimum(tok + w // 2, s) - jnp.maximum(tok - w // 2, 0)
        out.append(cnt.astype(F32))
    return out


def _rope_tables(s):
    inv = ROPE_THETA ** (-jnp.arange(0, ROT_DIM, 2, dtype=F32) / ROT_DIM)
    ang = jnp.arange(s).astype(F32)[:, None] * inv[None, :]
    cos, sin = jnp.cos(ang), jnp.sin(ang)
    z8 = jnp.zeros((s, 8), F32)
    z48 = jnp.zeros((s, HEAD_DIM - ROT_DIM), F32)
    c = jnp.concatenate([cos, cos, jnp.ones((s, HEAD_DIM - ROT_DIM), F32)], axis=1)
    s_lo = jnp.concatenate([z8, sin, z48], axis=1)
    s_hi = jnp.concatenate([-sin, z8, z48], axis=1)
    return tuple(jnp.concatenate([a, a], axis=1) for a in (c, s_lo, s_hi))


def _rope(x, c, s_lo, s_hi):
    n = x.shape[1]
    reps = n // 128
    c, s_lo, s_hi = (jnp.tile(a, (1, reps)) for a in (c, s_lo, s_hi))
    return x * c + pltpu.roll(x, 8, 1) * s_lo + pltpu.roll(x, n - 8, 1) * s_hi


def _rope_t(dx, c, s_lo, s_hi):
    n = dx.shape[1]
    reps = n // 128
    c, s_lo, s_hi = (jnp.tile(a, (1, reps)) for a in (c, s_lo, s_hi))
    return dx * c + pltpu.roll(dx * s_lo, n - 8, 1) + pltpu.roll(dx * s_hi, 8, 1)


def _in_proj0(x, g0, w_in0):
    s = x.shape[0]
    n = w_in0.shape[1]

    def body(x_ref, g_ref, w_ref, h_ref, z_ref):
        _, _, h = _rms_fwd(x_ref[...], g_ref[...])
        h = h.astype(BF16)
        h_ref[...] = h
        for j in range(n // 1024):
            z_ref[:, j * 1024:(j + 1) * 1024] = _dot(h, w_ref[:, j * 1024:(j + 1) * 1024]).astype(BF16)

    return pl.pallas_call(
        body, name="in_proj0", grid=(s // TOK,),
        in_specs=[_rows(TOK, D_MODEL), _whole((1, D_MODEL)), _whole(w_in0.shape)],
        out_specs=[_rows(TOK, D_MODEL), _rows(TOK, n)],
        out_shape=[jax.ShapeDtypeStruct((s, D_MODEL), BF16), jax.ShapeDtypeStruct((s, n), BF16)],
        compiler_params=_params(),
    )(x, g0, w_in0)


def _halo_specs(s, col_block):
    per = TOK // HALO
    last = s // HALO - 1
    prev = pl.BlockSpec((HALO, 1024), lambda i: (jnp.maximum(i * per - 1, 0), col_block))
    nxt = pl.BlockSpec((HALO, 1024), lambda i: (jnp.minimum((i + 1) * per, last), col_block))
    return prev, nxt


def _with_halo(i, n_tiles, prev_ref, cur, next_ref):
    prev = prev_ref[...]
    nxt = next_ref[...]
    prev = jnp.where(i > 0, prev, jnp.zeros_like(prev))
    nxt = jnp.where(i < n_tiles - 1, nxt, jnp.zeros_like(nxt))
    return jnp.concatenate([prev, cur, nxt], axis=0)


def _mixer_a(au, av, gv, ws_ref, bsx):
    u = _gelu(au)
    v1 = _gelu(av)
    rv, vh, v2 = _rms_fwd(v1, gv)
    v2 = v2.astype(BF16)
    rows = []
    for c in range(au.shape[0] // CHUNK):
        cols = [_dot(ws_ref[h], v2[c * CHUNK:(c + 1) * CHUNK, h * 256:(h + 1) * 256]) for h in range(4)]
        rows.append(jnp.concatenate(cols, axis=1) + bsx)
    return u, rv, vh, v2, jnp.concatenate(rows, axis=0)


def _mixer_b_pooled(bx, halo, band_ref, counts):
    out = []
    for g in range(4):
        win = _dot(band_ref[g], halo[:, g * 256:(g + 1) * 256])
        out.append(win / counts[g] - bx[:, g * 256:(g + 1) * 256])
    return out


def _mix0_fwd(x, z0, w_out0, gv, ws, bsx, wg, scale, band):
    s = x.shape[0]
    n_tiles = s // TOK

    def body(z_ref, zp_ref, zn_ref, x_ref, wout_ref, gv_ref, ws_ref, bsx_ref, wg_ref, sc_ref, band_ref,
             cat_ref, x1_ref):
        i = pl.program_id(0)
        au = z_ref[:, 0:1024].astype(F32)
        av = z_ref[:, 1024:2048].astype(F32)
        ag = z_ref[:, 2048:3072].astype(F32)
        u, _, _, _, mixed = _mixer_a(au, av, gv_ref[...], ws_ref, bsx_ref[...])
        cat_ref[:, 0:1024] = (u * mixed * (ag * _sigmoid(ag))).astype(BF16)

        bx16 = z_ref[:, 3072:4096]
        bg = z_ref[:, 4096:5120].astype(F32)
        halo = _with_halo(i, n_tiles, zp_ref, bx16, zn_ref)
        ps = _mixer_b_pooled(bx16.astype(F32), halo, band_ref, _window_counts(i, TOK, s))
        pw = jnp.concatenate([_dot(ps[g].astype(BF16), wg_ref[g]) for g in range(4)], axis=1)
        cat_ref[:, 1024:2048] = (pw * sc_ref[...] * (bg * _sigmoid(bg))).astype(BF16)

        x1_ref[...] = x_ref[...] + _dot(cat_ref[...], wout_ref[...])

    prev, nxt = _halo_specs(s, 3)
    return pl.pallas_call(
        body, name="mix0_fwd", grid=(n_tiles,),
        in_specs=[_rows(TOK, 5120), prev, nxt, _rows(TOK, D_MODEL), _whole(w_out0.shape), _whole((1, 1024)),
                  _whole(ws.shape), _whole(bsx.shape), _whole(wg.shape), _whole((1, 1024)), _whole(band.shape)],
        out_specs=[_rows(TOK, 2048), _rows(TOK, D_MODEL)],
        out_shape=[jax.ShapeDtypeStruct((s, 2048), BF16), jax.ShapeDtypeStruct((s, D_MODEL), F32)],
        compiler_params=_params(),
    )(z0, z0, z0, x, w_out0, gv, ws, bsx, wg, scale, band)


def _in_proj1(x1, g1, w_in1, rope):
    s = x1.shape[0]

    def body(x_ref, g_ref, w_ref, c_ref, lo_ref, hi_ref, h_ref, q_ref, k_ref, v_ref, gate_ref):
        _, _, h = _rms_fwd(x_ref[...], g_ref[...])
        h = h.astype(BF16)
        h_ref[...] = h
        tabs = (c_ref[...], lo_ref[...], hi_ref[...])
        q_ref[...] = _rope(_dot(h, w_ref[:, 0:1024]), *tabs).astype(BF16)
        kv = _dot(h, w_ref[:, 1024:1536])
        k_ref[...] = _rope(kv[:, 0:256], *tabs).astype(BF16)
        v_ref[...] = kv[:, 256:512].astype(BF16)
        gate_ref[...] = _dot(h, w_ref[:, 1536:2560]).astype(BF16)

    tab = _rows(TOK, 128)
    return pl.pallas_call(
        body, name="in_proj1", grid=(s // TOK,),
        in_specs=[_rows(TOK, D_MODEL), _whole((1, D_MODEL)), _whole(w_in1.shape), tab, tab, tab],
        out_specs=[_rows(TOK, 1024), _rows(TOK, 1024), _rows(TOK, 256), _rows(TOK, 256), _rows(TOK, 1024)],
        out_shape=[jax.ShapeDtypeStruct((s, 1024), BF16), jax.ShapeDtypeStruct((s, 1024), BF16),
                   jax.ShapeDtypeStruct((s, 256), BF16), jax.ShapeDtypeStruct((s, 256), BF16),
                   jax.ShapeDtypeStruct((s, 1024), BF16)],
        compiler_params=_params(),
    )(x1, g1, w_in1, *rope)


def _band_mask(i, s):
    tk = TOK + 2 * ATTN_WINDOW
    qi = i * TOK + lax.broadcasted_iota(jnp.int32, (TOK, tk), 0)
    kj = i * TOK - ATTN_WINDOW + lax.broadcasted_iota(jnp.int32, (TOK, tk), 1)
    return (jnp.abs(qi - kj) <= ATTN_WINDOW) & (kj >= 0) & (kj < s)


def _softmax_with_sink(sc, sink):
    m = jnp.maximum(jnp.max(sc, axis=-1, keepdims=True), sink)
    e = jnp.exp(sc - m)
    e_sink = jnp.exp(sink - m)
    den = jnp.sum(e, axis=-1, keepdims=True) + e_sink
    return e / den, e_sink / den


def _attn_fwd(q, kpad, vpad, sink):
    s = q.shape[0]
    tk = TOK + 2 * ATTN_WINDOW

    def body(sink_ref, q_ref, k_ref, v_ref, o_ref):
        i = pl.program_id(0)
        start = pl.multiple_of(i * TOK, TOK)
        kb = k_ref[pl.ds(start, tk), :]
        vb = v_ref[pl.ds(start, tk), :]
        allowed = _band_mask(i, s)
        outs = []
        for h in range(N_HEADS):
            g = h // 4
            qh = q_ref[:, h * HEAD_DIM:(h + 1) * HEAD_DIM]
            sc = _dot_nt(qh, kb[:, g * HEAD_DIM:(g + 1) * HEAD_DIM]) * (HEAD_DIM ** -0.5)
            sc = jnp.where(allowed, sc, NEG_INF)
            p, _ = _softmax_with_sink(sc, sink_ref[h])
            outs.append(_dot(p.astype(BF16), vb[:, g * HEAD_DIM:(g + 1) * HEAD_DIM]))
        o_ref[...] = jnp.concatenate(outs, axis=1).astype(BF16)

    return pl.pallas_call(
        body, name="attn_fwd", grid=(s // TOK,),
        in_specs=[pl.BlockSpec(memory_space=pltpu.SMEM), _rows(TOK, 1024), _whole(kpad.shape), _whole(vpad.shape)],
        out_specs=_rows(TOK, 1024),
        out_shape=jax.ShapeDtypeStruct((s, 1024), BF16),
        compiler_params=_params(),
    )(sink, q, kpad, vpad)


def _tail(x1, o, gate, target, w_out1, gf):
    s = x1.shape[0]

    def body(x1_ref, o_ref, gate_ref, t_ref, w_ref, gf_ref, y1_ref, dx2_ref, do_ref, dgate_ref, loss_ref, gfn_ref):
        i = pl.program_id(0)

        @pl.when(i == 0)
        def _():
            loss_ref[...] = jnp.zeros_like(loss_ref)
            gfn_ref[...] = jnp.zeros_like(gfn_ref)

        g = gate_ref[...].astype(F32)
        sg = _sigmoid(g)
        sil = g * sg
        o = o_ref[...].astype(F32)
        y1 = (o * sil).astype(BF16)
        y1_ref[...] = y1
        x2 = x1_ref[...] + _dot(y1, w_ref[...])
        gf = gf_ref[...]
        r, xh, out = _rms_fwd(x2, gf)
        diff = out - t_ref[...]
        loss_ref[...] += jnp.sum(diff * diff, axis=0, keepdims=True) * (0.5 / D_MODEL)
        dout = diff * (1.0 / D_MODEL)
        gfn_ref[...] += jnp.sum(dout * xh, axis=0, keepdims=True)
        dx2 = _rms_bwd(dout, gf, r, xh)
        dx2_ref[...] = dx2
        dy1 = _dot_nt(dx2.astype(BF16), w_ref[...])
        do_ref[...] = (dy1 * sil).astype(BF16)
        dgate_ref[...] = (dy1 * o * (sg * (1.0 + g * (1.0 - sg)))).astype(BF16)

    row = _rows(TOK, 1024)
    acc = _whole((1, 1024))
    return pl.pallas_call(
        body, name="tail", grid=(s // TOK,),
        in_specs=[row, row, row, row, _whole(w_out1.shape), acc],
        out_specs=[row, row, row, row, acc, acc],
        out_shape=[jax.ShapeDtypeStruct((s, 1024), BF16), jax.ShapeDtypeStruct((s, 1024), F32),
                   jax.ShapeDtypeStruct((s, 1024), BF16), jax.ShapeDtypeStruct((s, 1024), BF16),
                   jax.ShapeDtypeStruct((1, 1024), F32), jax.ShapeDtypeStruct((1, 1024), F32)],
        compiler_params=_params(),
    )(x1, o, gate, target, w_out1, gf)


def _attn_bwd(q, kpad, vpad, sink, o, do, rope):
    s = q.shape[0]
    tk = TOK + 2 * ATTN_WINDOW

    def body(sink_ref, q_ref, k_ref, v_ref, o_ref, do_ref, c_ref, lo_ref, hi_ref, dq_ref, dk_ref, dv_ref, ds_ref):
        i = pl.program_id(0)

        @pl.when(i == 0)
        def _():
            dk_ref[...] = jnp.zeros_like(dk_ref)
            dv_ref[...] = jnp.zeros_like(dv_ref)
            ds_ref[...] = jnp.zeros_like(ds_ref)

        start = pl.multiple_of(i * TOK, TOK)
        kb = k_ref[pl.ds(start, tk), :]
        vb = v_ref[pl.ds(start, tk), :]
        allowed = _band_mask(i, s)
        lane = lax.broadcasted_iota(jnp.int32, (1, 128), 1)
        dsink = jnp.zeros((1, 128), F32)
        dqs, dks, dvs = [], [], []
        for g in range(4):
            kg = kb[:, g * HEAD_DIM:(g + 1) * HEAD_DIM]
            vg = vb[:, g * HEAD_DIM:(g + 1) * HEAD_DIM]
            dkg = jnp.zeros((tk, HEAD_DIM), F32)
            dvg = jnp.zeros((tk, HEAD_DIM), F32)
            for j in range(4):
                h = 4 * g + j
                cols = slice(h * HEAD_DIM, (h + 1) * HEAD_DIM)
                qh = q_ref[:, cols]
                doh = do_ref[:, cols]
                sc = _dot_nt(qh, kg) * (HEAD_DIM ** -0.5)
                sc = jnp.where(allowed, sc, NEG_INF)
                p, p_sink = _softmax_with_sink(sc, sink_ref[h])
                delta = jnp.sum(doh.astype(F32) * o_ref[:, cols].astype(F32), axis=-1, keepdims=True)
                dsink = dsink + jnp.where(lane == h, -jnp.sum(p_sink * delta, axis=0, keepdims=True), 0.0)
                dvg = dvg + _dot_tn(p.astype(BF16), doh)
                dsc = (p * (_dot_nt(doh, vg) - delta) * (HEAD_DIM ** -0.5)).astype(BF16)
                dqs.append(_dot(dsc, kg))
                dkg = dkg + _dot_tn(dsc, qh)
            dks.append(dkg)
            dvs.append(dvg)
        dq = jnp.concatenate(dqs, axis=1)
        dq_ref[...] = _rope_t(dq, c_ref[...], lo_ref[...], hi_ref[...]).astype(BF16)
        dk_ref[pl.ds(start, tk), :] += jnp.concatenate(dks, axis=1)
        dv_ref[pl.ds(start, tk), :] += jnp.concatenate(dvs, axis=1)
        ds_ref[...] += dsink

    row = _rows(TOK, 1024)
    tab = _rows(TOK, 128)
    pad = _whole(kpad.shape)
    return pl.pallas_call(
        body, name="attn_bwd", grid=(s // TOK,),
        in_specs=[pl.BlockSpec(memory_space=pltpu.SMEM), row, pad, pad, row, row, tab, tab, tab],
        out_specs=[row, pad, pad, _whole((1, 128))],
        out_shape=[jax.ShapeDtypeStruct((s, 1024), BF16), jax.ShapeDtypeStruct(kpad.shape, F32),
                   jax.ShapeDtypeStruct(kpad.shape, F32), jax.ShapeDtypeStruct((1, 128), F32)],
        compiler_params=_params(),
    )(sink, q, kpad, vpad, o, do, *rope)


def _in_proj1_bwd(dq, dk, dv, dgate, x1, dx2, g1, w_in1, rope):
    s = x1.shape[0]

    def body(dq_ref, dk_ref, dv_ref, dgate_ref, x1_ref, dx2_ref, g_ref, w_ref, c_ref, lo_ref, hi_ref,
             dz_ref, dx1_ref, gn_ref):
        i = pl.program_id(0)

        @pl.when(i == 0)
        def _():
            gn_ref[...] = jnp.zeros_like(gn_ref)

        dz_ref[:, 0:1024] = dq_ref[...]
        dz_ref[:, 1024:1280] = _rope_t(dk_ref[...], c_ref[...], lo_ref[...], hi_ref[...]).astype(BF16)
        dz_ref[:, 1280:1536] = dv_ref[...].astype(BF16)
        dz_ref[:, 1536:2560] = dgate_ref[...]
        dh = _dot_nt(dz_ref[...], w_ref[...])
        g = g_ref[...]
        r, xh, _ = _rms_fwd(x1_ref[...], g)
        gn_ref[...] += jnp.sum(dh * xh, axis=0, keepdims=True)
        dx1_ref[...] = dx2_ref[...] + _rms_bwd(dh, g, r, xh)

    row = _rows(TOK, 1024)
    nar = _rows(TOK, 256)
    tab = _rows(TOK, 128)
    acc = _whole((1, 1024))
    return pl.pallas_call(
        body, name="in_proj1_bwd", grid=(s // TOK,),
        in_specs=[row, nar, nar, row, row, row, acc, _whole(w_in1.shape), tab, tab, tab],
        out_specs=[_rows(TOK, 2560), row, acc],
        out_shape=[jax.ShapeDtypeStruct((s, 2560), BF16), jax.ShapeDtypeStruct((s, 1024), F32),
                   jax.ShapeDtypeStruct((1, 1024), F32)],
        compiler_params=_params(),
    )(dq, dk, dv, dgate, x1, dx2, g1, w_in1, *rope)


def _mix0_bwd(dx1, z0, w_out0, gv, ws, bsx, wg, scale, band):
    s = dx1.shape[0]
    n_tiles = s // TOK

    def body(dx1_ref, z_ref, zp_ref, zn_ref, wout_ref, gv_ref, ws_ref, bsx_ref, wg_ref, sc_ref, band_ref,
             dza_ref, dzg_ref, dpn_ref, dws_ref, dbs_ref, dgv_ref, dsc_ref, dwg_ref):
        i = pl.program_id(0)

        @pl.when(i == 0)
        def _():
            for ref in (dws_ref, dbs_ref, dgv_ref, dsc_ref, dwg_ref):
                ref[...] = jnp.zeros_like(ref)

        dcat = _dot_nt(dx1_ref[...].astype(BF16), wout_ref[...])
        dya = dcat[:, 0:1024]
        dyb = dcat[:, 1024:2048]

        au = z_ref[:, 0:1024].astype(F32)
        av = z_ref[:, 1024:2048].astype(F32)
        ag = z_ref[:, 2048:3072].astype(F32)
        gv = gv_ref[...]
        u, rv, vh, v2, mixed = _mixer_a(au, av, gv, ws_ref, bsx_ref[...])
        sg = _sigmoid(ag)
        sil = ag * sg
        dza_ref[:, 2048:3072] = (dya * u * mixed * (sg * (1.0 + ag * (1.0 - sg)))).astype(BF16)
        dza_ref[:, 0:1024] = (dya * mixed * sil * _dgelu(au)).astype(BF16)
        dmixed = dya * u * sil
        lane = lax.broadcasted_iota(jnp.int32, (1, 128), 1)
        dm16 = dmixed.astype(BF16)
        dv2_rows = []
        for c in range(TOK // CHUNK):
            rows = slice(c * CHUNK, (c + 1) * CHUNK)
            cols_out = []
            for h in range(4):
                cols = slice(h * 256, (h + 1) * 256)
                dws_ref[h] += _dot_nt(dm16[rows, cols], v2[rows, cols])
                dbs_ref[...] += jnp.where(lane == h, jnp.sum(dmixed[rows, cols], axis=-1, keepdims=True), 0.0)
                cols_out.append(_dot_tn(ws_ref[h], dm16[rows, cols]))
            dv2_rows.append(jnp.concatenate(cols_out, axis=1))
        dv2 = jnp.concatenate(dv2_rows, axis=0)
        dgv_ref[...] += jnp.sum(dv2 * vh, axis=0, keepdims=True)
        dza_ref[:, 1024:2048] = (_rms_bwd(dv2, gv, rv, vh) * _dgelu(av)).astype(BF16)

        bx16 = z_ref[:, 3072:4096]
        bg = z_ref[:, 4096:5120].astype(F32)
        counts = _window_counts(i, TOK, s)
        halo = _with_halo(i, n_tiles, zp_ref, bx16, zn_ref)
        ps = [p.astype(BF16) for p in _mixer_b_pooled(bx16.astype(F32), halo, band_ref, counts)]
        pw = jnp.concatenate([_dot(ps[g], wg_ref[g]) for g in range(4)], axis=1)
        sgb = _sigmoid(bg)
        sc = sc_ref[...]
        dzg_ref[...] = (dyb * pw * sc * (sgb * (1.0 + bg * (1.0 - sgb)))).astype(BF16)
        dys = dyb * (bg * sgb)
        dsc_ref[...] += jnp.sum(dys * pw, axis=0, keepdims=True)
        dpw = (dys * sc).astype(BF16)
        for g in range(4):
            cols = slice(g * 256, (g + 1) * 256)
            dwg_ref[g] += _dot_tn(ps[g], dpw[:, cols])
            dpn_ref[:, cols] = (_dot_nt(dpw[:, cols], wg_ref[g]) / counts[g]).astype(BF16)

    prev, nxt = _halo_specs(s, 3)
    row = _rows(TOK, 1024)
    vec = _whole((1, 1024))
    return pl.pallas_call(
        body, name="mix0_bwd", grid=(n_tiles,),
        in_specs=[row, _rows(TOK, 5120), prev, nxt, _whole(w_out0.shape), vec, _whole(ws.shape),
                  _whole(bsx.shape), _whole(wg.shape), vec, _whole(band.shape)],
        out_specs=[_rows(TOK, 3072), row, row, _whole((4, 128, 128)), _whole((128, 128)), vec, vec,
                   _whole((4, 256, 256))],
        out_shape=[jax.ShapeDtypeStruct((s, 3072), BF16), jax.ShapeDtypeStruct((s, 1024), BF16),
                   jax.ShapeDtypeStruct((s, 1024), BF16), jax.ShapeDtypeStruct((4, 128, 128), F32),
                   jax.ShapeDtypeStruct((128, 128), F32), jax.ShapeDtypeStruct((1, 1024), F32),
                   jax.ShapeDtypeStruct((1, 1024), F32), jax.ShapeDtypeStruct((4, 256, 256), F32)],
        compiler_params=_params(),
    )(dx1, z0, z0, z0, w_out0, gv, ws, bsx, wg, scale, band)


def _in_proj0_bwd(dza, dzg, dpn, x, dx1, g0, w_in0, band_t):
    s = x.shape[0]
    n_tiles = s // TOK

    def body(dza_ref, dzg_ref, dpn_ref, dpp_ref, dpx_ref, x_ref, dx1_ref, g_ref, w_ref, band_ref,
             dz_ref, dx_ref, gn_ref):
        i = pl.program_id(0)

        @pl.when(i == 0)
        def _():
            gn_ref[...] = jnp.zeros_like(gn_ref)

        dz_ref[:, 0:3072] = dza_ref[...]
        dz_ref[:, 4096:5120] = dzg_ref[...]
        dpn = dpn_ref[...]
        halo = _with_halo(i, n_tiles, dpp_ref, dpn, dpx_ref)
        counts = _window_counts(i, TOK, s)
        for g in range(4):
            cols = slice(g * 256, (g + 1) * 256)
            dbx = _dot(band_ref[g], halo[:, cols]) - dpn[:, cols].astype(F32) * counts[g]
            dz_ref[:, 3072 + g * 256:3072 + (g + 1) * 256] = dbx.astype(BF16)
        dh = _dot_nt(dz_ref[...], w_ref[...])
        g0v = g_ref[...]
        r, xh, _ = _rms_fwd(x_ref[...], g0v)
        gn_ref[...] += jnp.sum(dh * xh, axis=0, keepdims=True)
        dx_ref[...] = dx1_ref[...] + _rms_bwd(dh, g0v, r, xh)

    prev, nxt = _halo_specs(s, 0)
    row = _rows(TOK, 1024)
    vec = _whole((1, 1024))
    return pl.pallas_call(
        body, name="in_proj0_bwd", grid=(n_tiles,),
        in_specs=[_rows(TOK, 3072), row, row, prev, nxt, row, row, vec, _whole(w_in0.shape), _whole(band_t.shape)],
        out_specs=[_rows(TOK, 5120), row, vec],
        out_shape=[jax.ShapeDtypeStruct((s, 5120), BF16), jax.ShapeDtypeStruct((s, 1024), F32),
                   jax.ShapeDtypeStruct((1, 1024), F32)],
        compiler_params=_params(),
    )(dza, dzg, dpn, dpn, dpn, x, dx1, g0, w_in0, band_t)


def _weight_grad(a, b, n_blocks, name):
    s, k = a.shape
    n = b.shape[1]
    tn = n // n_blocks
    ts = 512

    def body(a_ref, b_ref, o_ref):
        @pl.when(pl.program_id(1) == 0)
        def _():
            o_ref[...] = jnp.zeros_like(o_ref)

        o_ref[...] += _dot_tn(a_ref[...], b_ref[...])

    return pl.pallas_call(
        body, name=name, grid=(n_blocks, s // ts),
        in_specs=[pl.BlockSpec((ts, k), lambda j, t: (t, 0)), pl.BlockSpec((ts, tn), lambda j, t: (t, j))],
        out_specs=pl.BlockSpec((None, k, tn), lambda j, t: (j, 0, 0)),
        out_shape=jax.ShapeDtypeStruct((n_blocks, k, tn), F32),
        compiler_params=_params(),
    )(a, b)


def _row_tile(rows, cols):
    t = rows
    while t * cols * 4 > (1 << 20) and t % 16 == 0:
        t //= 2
    return t


def _add2(a, b, name):
    rows, cols = a.shape
    t = _row_tile(rows, cols)

    def body(a_ref, b_ref, o_ref):
        o_ref[...] = a_ref[...] + b_ref[...]

    spec = pl.BlockSpec((t, cols), lambda i: (i, 0))
    return pl.pallas_call(body, name=name, grid=(rows // t,), in_specs=[spec, spec], out_specs=spec,
                          out_shape=jax.ShapeDtypeStruct(a.shape, F32), compiler_params=_params())(a, b)


def _sum_slots(parts, name):
    n, rows, cols = parts.shape
    t = _row_tile(rows, cols)

    def body(p_ref, o_ref):
        acc = p_ref[0]
        for k in range(1, n):
            acc = acc + p_ref[k]
        o_ref[...] = acc

    return pl.pallas_call(
        body, name=name, grid=(rows // t,),
        in_specs=[pl.BlockSpec((n, t, cols), lambda i: (0, i, 0))],
        out_specs=pl.BlockSpec((t, cols), lambda i: (i, 0)),
        out_shape=jax.ShapeDtypeStruct((rows, cols), F32), compiler_params=_params())(parts)


def _adamw(w, g, m, v, name):
    rows, cols = w.shape
    t = _row_tile(rows, cols)

    def body(w_ref, g_ref, m_ref, v_ref, d_ref, nm_ref, nv_ref):
        g = g_ref[...]
        m2 = ADAM_B1 * m_ref[...] + (1.0 - ADAM_B1) * g
        v2 = ADAM_B2 * v_ref[...] + (1.0 - ADAM_B2) * (g * g)
        m_hat = m2 / (1.0 - ADAM_B1 ** ADAM_STEP)
        v_hat = v2 / (1.0 - ADAM_B2 ** ADAM_STEP)
        d_ref[...] = -ADAM_LR * (m_hat / (jnp.sqrt(v_hat) + ADAM_EPS) + ADAM_WD * w_ref[...])
        nm_ref[...] = m2
        nv_ref[...] = v2

    spec = pl.BlockSpec((t, cols), lambda i: (i, 0))
    shp = jax.ShapeDtypeStruct(w.shape, F32)
    return pl.pallas_call(body, name=name, grid=(rows // t,), in_specs=[spec] * 4, out_specs=[spec] * 3,
                          out_shape=[shp] * 3, compiler_params=_params())(w, g, m, v)


def _place():
    x, y, c = lax.axis_index("x"), lax.axis_index("y"), lax.axis_index("c")
    chips = [(1 - x, y), (x, 1 - y), (1 - x, 1 - y)]
    return x, y, c, chips


class _Sharded:
    def __init__(self, kind, full_shape):
        self.kind = kind
        self.full_shape = full_shape

    def in_full(self, ref, s, h):
        if self.kind == "cols":
            r, n = self.full_shape
            return ref.at[pl.ds(h * (r // 2), r // 2), pl.ds(pl.multiple_of(s * (n // 4), 128), n // 4)]
        if self.kind == "rows":
            r, _ = self.full_shape
            return ref.at[pl.ds(pl.multiple_of(s * (r // 4) + h * (r // 8), 8), r // 8), :]
        g, r, _ = self.full_shape
        return ref.at[pl.ds(h * (g // 2), g // 2), pl.ds(pl.multiple_of(s * (r // 4), 16), r // 4), :]

    def in_shard(self, ref, h):
        if self.kind == "cols":
            r = self.full_shape[0]
            return ref.at[pl.ds(h * (r // 2), r // 2), :]
        if self.kind == "rows":
            r = self.full_shape[0]
            return ref.at[pl.ds(h * (r // 8), r // 8), :]
        g = self.full_shape[0]
        return ref.at[pl.ds(h * (g // 2), g // 2)]


def _gather_weights(shards, cuts):
    n = len(shards)

    def body(*refs):
        src = refs[:n]
        out = refs[n:2 * n]
        send_sems, recv_sems, local_sems = refs[2 * n:]
        x, y, c, chips = _place()
        me = 2 * x + y
        sibling = (x, y, 1 - c)

        def remote(k, w, s, h, to, from_shard):
            dst = cuts[w].in_full(out[w], s, h)
            return pltpu.make_async_remote_copy(
                src_ref=cuts[w].in_shard(src[w], h) if from_shard else dst, dst_ref=dst,
                send_sem=send_sems.at[k], recv_sem=recv_sems.at[k], device_id=to, device_id_type=MESH)

        local = []
        for w in range(n):
            for h in range(2):
                cp = pltpu.make_async_copy(cuts[w].in_shard(src[w], h), cuts[w].in_full(out[w], me, h),
                                           local_sems.at[2 * w + h])
                cp.start()
                local.append(cp)
        first = []
        for w in range(n):
            for j, chip in enumerate(chips):
                cp = remote(3 * w + j, w, me, c, (*chip, c), True)
                cp.start()
                first.append(cp)
        passed = []
        for w in range(n):
            for j, chip in enumerate(chips):
                s = 2 * chip[0] + chip[1]
                remote(3 * w + j, w, s, c, (x, y, c), False).wait_recv()
                cp = remote(3 * n + 3 * w + j, w, s, c, sibling, False)
                cp.start()
                passed.append(cp)
        for w in range(n):
            for j, chip in enumerate(chips):
                s = 2 * chip[0] + chip[1]
                remote(3 * n + 3 * w + j, w, s, 1 - c, (x, y, c), False).wait_recv()
        for cp in first + passed:
            cp.wait_send()
        for cp in local:
            cp.wait()

    return pl.pallas_call(
        body, name="gather_weights",
        in_specs=[ANY] * n, out_specs=[ANY] * n,
        out_shape=[jax.ShapeDtypeStruct(cuts[w].full_shape, BF16) for w in range(n)],
        scratch_shapes=[pltpu.SemaphoreType.DMA((6 * n,)), pltpu.SemaphoreType.DMA((6 * n,)),
                        pltpu.SemaphoreType.DMA((2 * n,))],
        compiler_params=pltpu.CompilerParams(has_side_effects=True),
    )(*shards)


def _exchange_halves(grads, small):
    n = len(grads)
    flips = [(fx, fy, fc) for fx in range(2) for fy in range(2) for fc in range(2)][1:]

    def body(*refs):
        g = refs[:n]
        small_ref = refs[n]
        mine = refs[n + 1:2 * n + 1]
        theirs = refs[2 * n + 1:3 * n + 1]
        gathered = refs[3 * n + 1]
        send_sems, recv_sems, local_sems = refs[3 * n + 2:]
        x, y, c, _ = _place()
        me = 4 * x + 2 * y + c
        local = [pltpu.make_async_copy(g[w].at[:, c], mine[w], local_sems.at[w]) for w in range(n)]
        local.append(pltpu.make_async_copy(small_ref, gathered.at[me], local_sems.at[n]))
        for cp in local:
            cp.start()
        sends = []
        for w in range(n):
            sends.append(pltpu.make_async_remote_copy(
                src_ref=g[w].at[:, 1 - c], dst_ref=theirs[w], send_sem=send_sems.at[w], recv_sem=recv_sems.at[w],
                device_id=(x, y, 1 - c), device_id_type=MESH))
        for k, (fx, fy, fc) in enumerate(flips):
            sends.append(pltpu.make_async_remote_copy(
                src_ref=small_ref, dst_ref=gathered.at[me], send_sem=send_sems.at[n + k],
                recv_sem=recv_sems.at[n + k],
                device_id=(x + fx - 2 * x * fx, y + fy - 2 * y * fy, c + fc - 2 * c * fc), device_id_type=MESH))
        for cp in sends:
            cp.start()
        for w in range(n):
            pltpu.make_async_remote_copy(
                src_ref=g[w].at[:, 1 - c], dst_ref=theirs[w], send_sem=send_sems.at[w], recv_sem=recv_sems.at[w],
                device_id=(x, y, 1 - c), device_id_type=MESH).wait_recv()
        for k, (fx, fy, fc) in enumerate(flips):
            peer = 4 * (x + fx - 2 * x * fx) + 2 * (y + fy - 2 * y * fy) + (c + fc - 2 * c * fc)
            pltpu.make_async_remote_copy(
                src_ref=small_ref, dst_ref=gathered.at[peer], send_sem=send_sems.at[n + k],
                recv_sem=recv_sems.at[n + k], device_id=(x, y, c), device_id_type=MESH).wait_recv()
        for cp in sends:
            cp.wait_send()
        for cp in local:
            cp.wait()

    halves = [jax.ShapeDtypeStruct((4,) + g.shape[2:], F32) for g in grads]
    return pl.pallas_call(
        body, name="exchange_halves",
        in_specs=[ANY] * (n + 1), out_specs=[ANY] * (2 * n + 1),
        out_shape=halves + halves + [jax.ShapeDtypeStruct((8,) + small.shape, F32)],
        scratch_shapes=[pltpu.SemaphoreType.DMA((n + 7,)), pltpu.SemaphoreType.DMA((n + 7,)),
                        pltpu.SemaphoreType.DMA((n + 1,))],
        compiler_params=pltpu.CompilerParams(has_side_effects=True),
    )(*grads, small)


def _scatter_to_chips(parts):
    n = len(parts)

    def body(*refs):
        p = refs[:n]
        out = refs[n:2 * n]
        send_sems, recv_sems, local_sems = refs[2 * n:]
        x, y, c, chips = _place()
        me = 2 * x + y
        local = [pltpu.make_async_copy(p[w].at[me], out[w].at[me], local_sems.at[w]) for w in range(n)]
        for cp in local:
            cp.start()
        sends = []
        for w in range(n):
            for j, chip in enumerate(chips):
                s = 2 * chip[0] + chip[1]
                sends.append(pltpu.make_async_remote_copy(
                    src_ref=p[w].at[s], dst_ref=out[w].at[me], send_sem=send_sems.at[3 * w + j],
                    recv_sem=recv_sems.at[3 * w + j], device_id=(*chip, c), device_id_type=MESH))
        for cp in sends:
            cp.start()
        for w in range(n):
            for j, chip in enumerate(chips):
                s = 2 * chip[0] + chip[1]
                pltpu.make_async_remote_copy(
                    src_ref=p[w].at[me], dst_ref=out[w].at[s], send_sem=send_sems.at[3 * w + j],
                    recv_sem=recv_sems.at[3 * w + j], device_id=(x, y, c), device_id_type=MESH).wait_recv()
        for cp in sends:
            cp.wait_send()
        for cp in local:
            cp.wait()

    return pl.pallas_call(
        body, name="scatter_to_chips",
        in_specs=[ANY] * n, out_specs=[ANY] * n,
        out_shape=[jax.ShapeDtypeStruct(a.shape, F32) for a in parts],
        scratch_shapes=[pltpu.SemaphoreType.DMA((3 * n,)), pltpu.SemaphoreType.DMA((3 * n,)),
                        pltpu.SemaphoreType.DMA((n,))],
        compiler_params=pltpu.CompilerParams(has_side_effects=True),
    )(*parts)


def _share_halves(halves):
    n = len(halves)

    def body(*refs):
        hv = refs[:n]
        out = refs[n:2 * n]
        send_sems, recv_sems, local_sems = refs[2 * n:]
        x, y, c, _ = _place()
        local = [pltpu.make_async_copy(hv[w], out[w].at[c], local_sems.at[w]) for w in range(n)]
        for cp in local:
            cp.start()
        sends = [pltpu.make_async_remote_copy(
            src_ref=hv[w], dst_ref=out[w].at[c], send_sem=send_sems.at[w], recv_sem=recv_sems.at[w],
            device_id=(x, y, 1 - c), device_id_type=MESH) for w in range(n)]
        for cp in sends:
            cp.start()
        for w in range(n):
            pltpu.make_async_remote_copy(
                src_ref=hv[w], dst_ref=out[w].at[1 - c], send_sem=send_sems.at[w], recv_sem=recv_sems.at[w],
                device_id=(x, y, c), device_id_type=MESH).wait_recv()
        for cp in sends:
            cp.wait_send()
        for cp in local:
            cp.wait()

    return pl.pallas_call(
        body, name="share_halves",
        in_specs=[ANY] * n, out_specs=[ANY] * n,
        out_shape=[jax.ShapeDtypeStruct((2,) + a.shape, F32) for a in halves],
        scratch_shapes=[pltpu.SemaphoreType.DMA((n,)), pltpu.SemaphoreType.DMA((n,)),
                        pltpu.SemaphoreType.DMA((n,))],
        compiler_params=pltpu.CompilerParams(has_side_effects=True),
    )(*halves)


SMALL_ROWS = 80


def _pack_small(vecs, ws, bs, sink):
    top = jnp.concatenate(
        [v.reshape(1, 1024) for v in vecs]
        + [jnp.pad(bs.reshape(1, 512), ((0, 0), (0, 512))), jnp.pad(sink.reshape(1, 16), ((0, 0), (0, 1008))),
           jnp.zeros((1, 1024), F32)], axis=0)
    return jnp.concatenate([top, ws.reshape(64, 1024), jnp.zeros((8, 1024), F32)], axis=0)


def _unpack_small(p):
    vecs = [p[k] for k in range(5)]
    return vecs, p[8:72].reshape(4, 128, 128), p[5, :512].reshape(4, 128), p[6, :16]


def kernel(x, norm_0, w_in_0, a_v_norm_0, a_spatial_w_0, a_spatial_b_0, b_group_w_0, b_scale_0, w_out_0, norm_1, w_in_1, sink_1, w_out_1, final_norm, loss_target, m_norm_0, m_w_in_0, m_a_v_norm_0, m_a_spatial_w_0, m_a_spatial_b_0, m_b_group_w_0, m_b_scale_0, m_w_out_0, m_norm_1, m_w_in_1, m_sink_1, m_w_out_1, m_final_norm, v_norm_0, v_w_in_0, v_a_v_norm_0, v_a_spatial_w_0, v_a_spatial_b_0, v_b_group_w_0, v_b_scale_0, v_w_out_0, v_norm_1, v_w_in_1, v_sink_1, v_w_out_1, v_final_norm):
    s = x.shape[1]
    xs = x.reshape(s, D_MODEL)
    target = loss_target.reshape(s, D_MODEL)

    cuts = [_Sharded("cols", (1024, 5120)), _Sharded("rows", (2048, 1024)), _Sharded("cols", (1024, 2560)),
            _Sharded("rows", (1024, 1024)), _Sharded("mid", (4, 256, 256))]
    big_w = [w_in_0, w_out_0, w_in_1, w_out_1, b_group_w_0]
    big_m = [m_w_in_0, m_w_out_0, m_w_in_1, m_w_out_1, m_b_group_w_0]
    big_v = [v_w_in_0, v_w_out_0, v_w_in_1, v_w_out_1, v_b_group_w_0]
    w_in0, w_out0, w_in1, w_out1, wg = _gather_weights([w.astype(BF16) for w in big_w], cuts)

    row = lambda v: v.reshape(1, 1024)
    ws16 = a_spatial_w_0.astype(BF16)
    bsx = jnp.repeat(a_spatial_b_0.T, 256, axis=1)
    band, band_t = _band_matrices(TOK)
    rope = _rope_tables(s)

    h0, z0 = _in_proj0(xs, row(norm_0), w_in0)
    cat, x1 = _mix0_fwd(xs, z0, w_out0, row(a_v_norm_0), ws16, bsx, wg, row(b_scale_0), band)
    h1, q, k, v, gate = _in_proj1(x1, row(norm_1), w_in1, rope)
    kpad = jnp.pad(k, ((ATTN_WINDOW, ATTN_WINDOW), (0, 0)))
    vpad = jnp.pad(v, ((ATTN_WINDOW, ATTN_WINDOW), (0, 0)))
    o = _attn_fwd(q, kpad, vpad, sink_1)
    y1, dx2, do, dgate, loss_lanes, g_final = _tail(x1, o, gate, target, w_out1, row(final_norm))

    dq, dkpad, dvpad, dsink = _attn_bwd(q, kpad, vpad, sink_1, o, do, rope)
    dk = dkpad[ATTN_WINDOW:ATTN_WINDOW + s]
    dv = dvpad[ATTN_WINDOW:ATTN_WINDOW + s]
    dz1, dx1, g_norm1 = _in_proj1_bwd(dq, dk, dv, dgate, x1, dx2, row(norm_1), w_in1, rope)
    dza, dzg, dpn, d_ws, d_bs, d_gv, d_scale, d_wg = _mix0_bwd(
        dx1, z0, w_out0, row(a_v_norm_0), ws16, bsx, wg, row(b_scale_0), band)
    dz0, grad_x, g_norm0 = _in_proj0_bwd(dza, dzg, dpn, xs, dx1, row(norm_0), w_in0, band_t)

    g_w_in0 = _weight_grad(h0, dz0, 4, "grad_w_in0")
    g_w_out0 = _weight_grad(cat, dx1.astype(BF16), 1, "grad_w_out0")
    g_w_in1 = _weight_grad(h1, dz1, 4, "grad_w_in1")
    g_w_out1 = _weight_grad(y1, dx2.astype(BF16), 1, "grad_w_out1")

    grads = [g_w_in0.reshape(4, 2, 512, 1280), g_w_out0.reshape(4, 2, 256, 1024), g_w_in1.reshape(4, 2, 512, 640),
             g_w_out1.reshape(4, 2, 128, 1024),
             d_wg.reshape(2, 2, 4, 64, 256).transpose(2, 0, 1, 3, 4).reshape(4, 2, 128, 256)]
    small = _pack_small([g_norm0, d_gv, d_scale, g_norm1, g_final], d_ws, d_bs[:, :4].T, dsink[0, :16])
    *halves, small_all = _exchange_halves(grads, small)
    n = len(grads)
    parts = []
    for w in range(n):
        mine, theirs = halves[w], halves[n + w]
        r, c_ = mine.shape[1:]
        parts.append(_add2(mine.reshape(4 * r, c_), theirs.reshape(4 * r, c_), f"add_sibling{w}").reshape(4, r, c_))
    slots = _scatter_to_chips(parts)
    reduced = [_sum_slots(slots[w], f"sum_chips{w}") for w in range(n)]
    shards = _share_halves(reduced)

    out_g, out_d, out_m, out_v = {}, {}, {}, {}
    names = ["w_in_0", "w_out_0", "w_in_1", "w_out_1", "b_group_w_0"]
    for w in range(n):
        shape = big_w[w].shape
        if cuts[w].kind == "mid":
            g = shards[w].reshape(4, 64, 256)
        else:
            g = shards[w].reshape(shape)
        two_d = (-1, shape[-1])
        d, nm, nv = _adamw(big_w[w].reshape(two_d), g.reshape(two_d), big_m[w].reshape(two_d),
                           big_v[w].reshape(two_d), f"adamw{w}")
        out_g[names[w]], out_d[names[w]], out_m[names[w]], out_v[names[w]] = (
            g, d.reshape(shape), nm.reshape(shape), nv.reshape(shape))

    g_small = _sum_slots(small_all, "sum_small")
    small_names = ["norm_0", "a_v_norm_0", "b_scale_0", "norm_1", "final_norm"]
    pack = lambda vecs, ws_, bs_, sk: _pack_small(vecs, ws_, bs_, sk)
    w_small = pack([norm_0, a_v_norm_0, b_scale_0, norm_1, final_norm], a_spatial_w_0, a_spatial_b_0, sink_1)
    m_small = pack([m_norm_0, m_a_v_norm_0, m_b_scale_0, m_norm_1, m_final_norm], m_a_spatial_w_0,
                   m_a_spatial_b_0, m_sink_1)
    v_small = pack([v_norm_0, v_a_v_norm_0, v_b_scale_0, v_norm_1, v_final_norm], v_a_spatial_w_0,
                   v_a_spatial_b_0, v_sink_1)
    d_small, nm_small, nv_small = _adamw(w_small, g_small, m_small, v_small, "adamw_small")
    for store, packed in ((out_g, g_small), (out_d, d_small), (out_m, nm_small), (out_v, nv_small)):
        vecs, ws_, bs_, sk = _unpack_small(packed)
        for name, vec in zip(small_names, vecs):
            store[name] = vec
        store["a_spatial_w_0"], store["a_spatial_b_0"], store["sink_1"] = ws_, bs_, sk

    loss = lax.psum(jnp.sum(loss_lanes), ("x", "y", "c"))
    order = ["norm_0", "w_in_0", "a_v_norm_0", "a_spatial_w_0", "a_spatial_b_0", "b_group_w_0", "b_scale_0",
             "w_out_0", "norm_1", "w_in_1", "sink_1", "w_out_1", "final_norm"]
    return (loss, grad_x.reshape(1, s, D_MODEL), *[out_g[k] for k in order], *[out_d[k] for k in order],
            *[out_m[k] for k in order], *[out_v[k] for k in order])
```

```python
import functools

import numpy as np
import jax
import jax.numpy as jnp
from jax import lax
from jax.experimental import pallas as pl
from jax.experimental.pallas import tpu as pltpu

F32 = jnp.float32
BF16 = jnp.bfloat16
MESH = pl.DeviceIdType.MESH

D_MODEL = 1024
EPS = 1e-6
NEG_INF = -1e30
CHUNK = 128
POOL_WINDOWS = (2, 4, 8, 16)
HALO = 16
N_HEADS = 16
HEAD_DIM = 64
ATTN_WINDOW = 128
ROPE_THETA = 500000.0
ROT_DIM = 16
ADAM_LR = 0.001
ADAM_B1 = 0.9
ADAM_B2 = 0.999
ADAM_EPS = 1e-08
ADAM_WD = 0.01
ADAM_STEP = 10

TOK = 256
VMEM_LIMIT = 56 * 1024 * 1024


def _params(**kw):
    return pltpu.CompilerParams(vmem_limit_bytes=VMEM_LIMIT, **kw)


def _whole(shape):
    nd = len(shape)
    return pl.BlockSpec(shape, lambda *_: (0,) * nd)


def _rows(t, n):
    return pl.BlockSpec((t, n), lambda i: (i, 0))


ANY = pl.BlockSpec(memory_space=pl.ANY)

_G0 = 0.7978845608028654
_G1 = 0.044715


def _gelu(x):
    return 0.5 * x * (1.0 + jnp.tanh(_G0 * (x + _G1 * x * x * x)))


def _dgelu(x):
    t = jnp.tanh(_G0 * (x + _G1 * x * x * x))
    return 0.5 * (1.0 + t) + 0.5 * x * (1.0 - t * t) * (_G0 * (1.0 + 3.0 * _G1 * x * x))


def _sigmoid(x):
    return 1.0 / (1.0 + jnp.exp(-x))


def _dot(a, b):
    return jnp.dot(a, b, preferred_element_type=F32)


def _dot_nt(a, b):
    return lax.dot_general(a, b, (((1,), (1,)), ((), ())), preferred_element_type=F32)


def _dot_tn(a, b):
    return lax.dot_general(a, b, (((0,), (0,)), ((), ())), preferred_element_type=F32)


def _rms_fwd(x, g):
    r = lax.rsqrt(jnp.mean(x * x, axis=-1, keepdims=True) + EPS)
    xh = x * r
    return r, xh, xh * g


def _rms_bwd(dy, g, r, xh):
    dxh = dy * g
    return r * (dxh - xh * jnp.mean(dxh * xh, axis=-1, keepdims=True))


def _band_matrices(t):
    r = np.arange(t)[:, None]
    j = np.arange(t + 2 * HALO)[None, :]
    fwd, bwd = [], []
    for w in POOL_WINDOWS:
        d = j - r - HALO
        fwd.append((d >= -(w // 2)) & (d < w // 2))
        bwd.append((d >= -(w // 2) + 1) & (d <= w // 2))
    return (jnp.asarray(np.stack(fwd), BF16), jnp.asarray(np.stack(bwd), BF16))


def _window_counts(i, t, s):
    tok = i * t + lax.broadcasted_iota(jnp.int32, (t, 1), 0)
    out = []
    for w in POOL_WINDOWS:
        cnt = jnp.minimum(tok + w // 2, s) - jnp.maximum(tok - w // 2, 0)
        out.append(cnt.astype(F32))
    return out


def _rope_tables(s):
    inv = np.float32(ROPE_THETA) ** (-np.arange(0, ROT_DIM, 2, dtype=np.float32) / np.float32(ROT_DIM))
    ang = np.arange(s, dtype=np.float32)[:, None] * inv.astype(np.float32)[None, :]
    cos, sin = np.cos(ang).astype(np.float32), np.sin(ang).astype(np.float32)
    z8 = np.zeros((s, 8), np.float32)
    z48 = np.zeros((s, HEAD_DIM - ROT_DIM), np.float32)
    c = np.concatenate([cos, cos, np.ones((s, HEAD_DIM - ROT_DIM), np.float32)], axis=1)
    s_lo = np.concatenate([z8, sin, z48], axis=1)
    s_hi = np.concatenate([-sin, z8, z48], axis=1)
    return tuple(jnp.asarray(np.concatenate([a, a], axis=1)) for a in (c, s_lo, s_hi))


def _rope(x, c, s_lo, s_hi):
    n = x.shape[1]
    reps = n // 128
    c, s_lo, s_hi = (jnp.tile(a, (1, reps)) for a in (c, s_lo, s_hi))
    return x * c + pltpu.roll(x, 8, 1) * s_lo + pltpu.roll(x, n - 8, 1) * s_hi


def _rope_t(dx, c, s_lo, s_hi):
    n = dx.shape[1]
    reps = n // 128
    c, s_lo, s_hi = (jnp.tile(a, (1, reps)) for a in (c, s_lo, s_hi))
    return dx * c + pltpu.roll(dx * s_lo, n - 8, 1) + pltpu.roll(dx * s_hi, 8, 1)


def _in_proj0(x, g0, w_in0):
    s = x.shape[0]
    n = w_in0.shape[1]

    def body(x_ref, g_ref, w_ref, h_ref, z_ref):
        _, _, h = _rms_fwd(x_ref[...], g_ref[...])
        h = h.astype(BF16)
        h_ref[...] = h
        for j in range(n // 1024):
            z_ref[:, j * 1024:(j + 1) * 1024] = _dot(h, w_ref[:, j * 1024:(j + 1) * 1024]).astype(BF16)

    return pl.pallas_call(
        body, name="in_proj0", grid=(s // TOK,),
        in_specs=[_rows(TOK, D_MODEL), _whole((1, D_MODEL)), _whole(w_in0.shape)],
        out_specs=[_rows(TOK, D_MODEL), _rows(TOK, n)],
        out_shape=[jax.ShapeDtypeStruct((s, D_MODEL), BF16), jax.ShapeDtypeStruct((s, n), BF16)],
        compiler_params=_params(),
    )(x, g0, w_in0)


def _halo_specs(s, col_block):
    per = TOK // HALO
    last = s // HALO - 1
    prev = pl.BlockSpec((HALO, 1024), lambda i: (jnp.maximum(i * per - 1, 0), col_block))
    nxt = pl.BlockSpec((HALO, 1024), lambda i: (jnp.minimum((i + 1) * per, last), col_block))
    return prev, nxt


def _with_halo(i, n_tiles, prev_ref, cur, next_ref):
    prev = prev_ref[...]
    nxt = next_ref[...]
    prev = jnp.where(i > 0, prev, jnp.zeros_like(prev))
    nxt = jnp.where(i < n_tiles - 1, nxt, jnp.zeros_like(nxt))
    return jnp.concatenate([prev, cur, nxt], axis=0)


def _mixer_a(au, av, gv, ws_ref, bsx):
    u = _gelu(au)
    v1 = _gelu(av)
    rv, vh, v2 = _rms_fwd(v1, gv)
    v2 = v2.astype(BF16)
    rows = []
    for c in range(au.shape[0] // CHUNK):
        cols = [_dot(ws_ref[h], v2[c * CHUNK:(c + 1) * CHUNK, h * 256:(h + 1) * 256]) for h in range(4)]
        rows.append(jnp.concatenate(cols, axis=1) + bsx)
    return u, rv, vh, v2, jnp.concatenate(rows, axis=0)


def _mixer_b_pooled(bx, halo, band_ref, counts):
    out = []
    for g in range(4):
        win = _dot(band_ref[g], halo[:, g * 256:(g + 1) * 256])
        out.append(win / counts[g] - bx[:, g * 256:(g + 1) * 256])
    return out


def _mix0_fwd(x, z0, w_out0, gv, ws, bsx, wg, scale, band):
    s = x.shape[0]
    n_tiles = s // TOK

    def body(z_ref, zp_ref, zn_ref, x_ref, wout_ref, gv_ref, ws_ref, bsx_ref, wg_ref, sc_ref, band_ref,
             cat_ref, x1_ref):
        i = pl.program_id(0)
        au = z_ref[:, 0:1024].astype(F32)
        av = z_ref[:, 1024:2048].astype(F32)
        ag = z_ref[:, 2048:3072].astype(F32)
        u, _, _, _, mixed = _mixer_a(au, av, gv_ref[...], ws_ref, bsx_ref[...])
        cat_ref[:, 0:1024] = (u * mixed * (ag * _sigmoid(ag))).astype(BF16)

        bx16 = z_ref[:, 3072:4096]
        bg = z_ref[:, 4096:5120].astype(F32)
        halo = _with_halo(i, n_tiles, zp_ref, bx16, zn_ref)
        ps = _mixer_b_pooled(bx16.astype(F32), halo, band_ref, _window_counts(i, TOK, s))
        pw = jnp.concatenate([_dot(ps[g].astype(BF16), wg_ref[g]) for g in range(4)], axis=1)
        cat_ref[:, 1024:2048] = (pw * sc_ref[...] * (bg * _sigmoid(bg))).astype(BF16)

        x1_ref[...] = x_ref[...] + _dot(cat_ref[...], wout_ref[...])

    prev, nxt = _halo_specs(s, 3)
    return pl.pallas_call(
        body, name="mix0_fwd", grid=(n_tiles,),
        in_specs=[_rows(TOK, 5120), prev, nxt, _rows(TOK, D_MODEL), _whole(w_out0.shape), _whole((1, 1024)),
                  _whole(ws.shape), _whole(bsx.shape), _whole(wg.shape), _whole((1, 1024)), _whole(band.shape)],
        out_specs=[_rows(TOK, 2048), _rows(TOK, D_MODEL)],
        out_shape=[jax.ShapeDtypeStruct((s, 2048), BF16), jax.ShapeDtypeStruct((s, D_MODEL), F32)],
        compiler_params=_params(),
    )(z0, z0, z0, x, w_out0, gv, ws, bsx, wg, scale, band)


def _in_proj1(x1, g1, w_in1, rope):
    s = x1.shape[0]

    def body(x_ref, g_ref, w_ref, c_ref, lo_ref, hi_ref, h_ref, q_ref, k_ref, v_ref, gate_ref):
        _, _, h = _rms_fwd(x_ref[...], g_ref[...])
        h = h.astype(BF16)
        h_ref[...] = h
        tabs = (c_ref[...], lo_ref[...], hi_ref[...])
        q_ref[...] = _rope(_dot(h, w_ref[:, 0:1024]), *tabs).astype(BF16)
        kv = _dot(h, w_ref[:, 1024:1536])
        k_ref[...] = _rope(kv[:, 0:256], *tabs).astype(BF16)
        v_ref[...] = kv[:, 256:512].astype(BF16)
        gate_ref[...] = _dot(h, w_ref[:, 1536:2560]).astype(BF16)

    tab = _rows(TOK, 128)
    return pl.pallas_call(
        body, name="in_proj1", grid=(s // TOK,),
        in_specs=[_rows(TOK, D_MODEL), _whole((1, D_MODEL)), _whole(w_in1.shape), tab, tab, tab],
        out_specs=[_rows(TOK, 1024), _rows(TOK, 1024), _rows(TOK, 256), _rows(TOK, 256), _rows(TOK, 1024)],
        out_shape=[jax.ShapeDtypeStruct((s, 1024), BF16), jax.ShapeDtypeStruct((s, 1024), BF16),
                   jax.ShapeDtypeStruct((s, 256), BF16), jax.ShapeDtypeStruct((s, 256), BF16),
                   jax.ShapeDtypeStruct((s, 1024), BF16)],
        compiler_params=_params(),
    )(x1, g1, w_in1, *rope)


def _band_mask(i, s):
    tk = TOK + 2 * ATTN_WINDOW
    qi = i * TOK + lax.broadcasted_iota(jnp.int32, (TOK, tk), 0)
    kj = i * TOK - ATTN_WINDOW + lax.broadcasted_iota(jnp.int32, (TOK, tk), 1)
    return (jnp.abs(qi - kj) <= ATTN_WINDOW) & (kj >= 0) & (kj < s)


def _softmax_with_sink(sc, sink):
    m = jnp.maximum(jnp.max(sc, axis=-1, keepdims=True), sink)
    e = jnp.exp(sc - m)
    e_sink = jnp.exp(sink - m)
    den = jnp.sum(e, axis=-1, keepdims=True) + e_sink
    return e / den, e_sink / den


def _attn_fwd(q, kpad, vpad, sink):
    s = q.shape[0]
    tk = TOK + 2 * ATTN_WINDOW

    def body(sink_ref, q_ref, k_ref, v_ref, o_ref):
        i = pl.program_id(0)
        start = pl.multiple_of(i * TOK, TOK)
        kb = k_ref[pl.ds(start, tk), :]
        vb = v_ref[pl.ds(start, tk), :]
        allowed = _band_mask(i, s)
        outs = []
        for h in range(N_HEADS):
            g = h // 4
            qh = q_ref[:, h * HEAD_DIM:(h + 1) * HEAD_DIM]
            sc = _dot_nt(qh, kb[:, g * HEAD_DIM:(g + 1) * HEAD_DIM]) * (HEAD_DIM ** -0.5)
            sc = jnp.where(allowed, sc, NEG_INF)
            p, _ = _softmax_with_sink(sc, sink_ref[h])
            outs.append(_dot(p.astype(BF16), vb[:, g * HEAD_DIM:(g + 1) * HEAD_DIM]))
        o_ref[...] = jnp.concatenate(outs, axis=1).astype(BF16)

    return pl.pallas_call(
        body, name="attn_fwd", grid=(s // TOK,),
        in_specs=[pl.BlockSpec(memory_space=pltpu.SMEM), _rows(TOK, 1024), _whole(kpad.shape), _whole(vpad.shape)],
        out_specs=_rows(TOK, 1024),
        out_shape=jax.ShapeDtypeStruct((s, 1024), BF16),
        compiler_params=_params(),
    )(sink, q, kpad, vpad)


def _tail(x1, o, gate, target, w_out1, gf):
    s = x1.shape[0]

    def body(x1_ref, o_ref, gate_ref, t_ref, w_ref, gf_ref, y1_ref, dx2_ref, do_ref, dgate_ref, loss_ref, gfn_ref,
             dx2h_ref):
        i = pl.program_id(0)

        @pl.when(i == 0)
        def _():
            loss_ref[...] = jnp.zeros_like(loss_ref)
            gfn_ref[...] = jnp.zeros_like(gfn_ref)

        g = gate_ref[...].astype(F32)
        sg = _sigmoid(g)
        sil = g * sg
        o = o_ref[...].astype(F32)
        y1 = (o * sil).astype(BF16)
        y1_ref[...] = y1
        x2 = x1_ref[...] + _dot(y1, w_ref[...])
        gf = gf_ref[...]
        r, xh, out = _rms_fwd(x2, gf)
        diff = out - t_ref[...]
        loss_ref[...] += jnp.sum(diff * diff, axis=0, keepdims=True) * (0.5 / D_MODEL)
        dout = diff * (1.0 / D_MODEL)
        gfn_ref[...] += jnp.sum(dout * xh, axis=0, keepdims=True)
        dx2 = _rms_bwd(dout, gf, r, xh)
        dx2_ref[...] = dx2
        dx2h = dx2.astype(BF16)
        dx2h_ref[...] = dx2h
        dy1 = _dot_nt(dx2h, w_ref[...])
        do_ref[...] = (dy1 * sil).astype(BF16)
        dgate_ref[...] = (dy1 * o * (sg * (1.0 + g * (1.0 - sg)))).astype(BF16)

    row = _rows(TOK, 1024)
    acc = _whole((1, 1024))
    return pl.pallas_call(
        body, name="tail", grid=(s // TOK,),
        in_specs=[row, row, row, row, _whole(w_out1.shape), acc],
        out_specs=[row, row, row, row, acc, acc, row],
        out_shape=[jax.ShapeDtypeStruct((s, 1024), BF16), jax.ShapeDtypeStruct((s, 1024), F32),
                   jax.ShapeDtypeStruct((s, 1024), BF16), jax.ShapeDtypeStruct((s, 1024), BF16),
                   jax.ShapeDtypeStruct((1, 1024), F32), jax.ShapeDtypeStruct((1, 1024), F32),
                   jax.ShapeDtypeStruct((s, 1024), BF16)],
        compiler_params=_params(),
    )(x1, o, gate, target, w_out1, gf)


def _attn_bwd(q, kpad, vpad, sink, o, do, rope):
    s = q.shape[0]
    tk = TOK + 2 * ATTN_WINDOW

    def body(sink_ref, q_ref, k_ref, v_ref, o_ref, do_ref, c_ref, lo_ref, hi_ref, dq_ref, dk_ref, dv_ref, ds_ref):
        i = pl.program_id(0)

        @pl.when(i == 0)
        def _():
            dk_ref[...] = jnp.zeros_like(dk_ref)
            dv_ref[...] = jnp.zeros_like(dv_ref)
            ds_ref[...] = jnp.zeros_like(ds_ref)

        start = pl.multiple_of(i * TOK, TOK)
        kb = k_ref[pl.ds(start, tk), :]
        vb = v_ref[pl.ds(start, tk), :]
        allowed = _band_mask(i, s)
        lane = lax.broadcasted_iota(jnp.int32, (1, 128), 1)
        dsink = jnp.zeros((1, 128), F32)
        dqs, dks, dvs = [], [], []
        for g in range(4):
            kg = kb[:, g * HEAD_DIM:(g + 1) * HEAD_DIM]
            vg = vb[:, g * HEAD_DIM:(g + 1) * HEAD_DIM]
            dkg = jnp.zeros((tk, HEAD_DIM), F32)
            dvg = jnp.zeros((tk, HEAD_DIM), F32)
            for j in range(4):
                h = 4 * g + j
                cols = slice(h * HEAD_DIM, (h + 1) * HEAD_DIM)
                qh = q_ref[:, cols]
                doh = do_ref[:, cols]
                sc = _dot_nt(qh, kg) * (HEAD_DIM ** -0.5)
                sc = jnp.where(allowed, sc, NEG_INF)
                p, p_sink = _softmax_with_sink(sc, sink_ref[h])
                delta = jnp.sum(doh.astype(F32) * o_ref[:, cols].astype(F32), axis=-1, keepdims=True)
                dsink = dsink + jnp.where(lane == h, -jnp.sum(p_sink * delta, axis=0, keepdims=True), 0.0)
                dvg = dvg + _dot_tn(p.astype(BF16), doh)
                dsc = (p * (_dot_nt(doh, vg) - delta) * (HEAD_DIM ** -0.5)).astype(BF16)
                dqs.append(_dot(dsc, kg))
                dkg = dkg + _dot_tn(dsc, qh)
            dks.append(dkg)
            dvs.append(dvg)
        dq = jnp.concatenate(dqs, axis=1)
        dq_ref[...] = _rope_t(dq, c_ref[...], lo_ref[...], hi_ref[...]).astype(BF16)
        dk_ref[pl.ds(start, tk), :] += jnp.concatenate(dks, axis=1)
        dv_ref[pl.ds(start, tk), :] += jnp.concatenate(dvs, axis=1)
        ds_ref[...] += dsink

    row = _rows(TOK, 1024)
    tab = _rows(TOK, 128)
    pad = _whole(kpad.shape)
    return pl.pallas_call(
        body, name="attn_bwd", grid=(s // TOK,),
        in_specs=[pl.BlockSpec(memory_space=pltpu.SMEM), row, pad, pad, row, row, tab, tab, tab],
        out_specs=[row, pad, pad, _whole((1, 128))],
        out_shape=[jax.ShapeDtypeStruct((s, 1024), BF16), jax.ShapeDtypeStruct(kpad.shape, F32),
                   jax.ShapeDtypeStruct(kpad.shape, F32), jax.ShapeDtypeStruct((1, 128), F32)],
        compiler_params=_params(),
    )(sink, q, kpad, vpad, o, do, *rope)


def _in_proj1_bwd(dq, dk, dv, dgate, x1, dx2, g1, w_in1, rope):
    s = x1.shape[0]

    def body(dq_ref, dk_ref, dv_ref, dgate_ref, x1_ref, dx2_ref, g_ref, w_ref, c_ref, lo_ref, hi_ref,
             dz_ref, dx1_ref, gn_ref, dx1h_ref):
        i = pl.program_id(0)

        @pl.when(i == 0)
        def _():
            gn_ref[...] = jnp.zeros_like(gn_ref)

        dz_ref[:, 0:1024] = dq_ref[...]
        dz_ref[:, 1024:1280] = _rope_t(dk_ref[...], c_ref[...], lo_ref[...], hi_ref[...]).astype(BF16)
        dz_ref[:, 1280:1536] = dv_ref[...].astype(BF16)
        dz_ref[:, 1536:2560] = dgate_ref[...]
        dh = _dot_nt(dz_ref[...], w_ref[...])
        g = g_ref[...]
        r, xh, _ = _rms_fwd(x1_ref[...], g)
        gn_ref[...] += jnp.sum(dh * xh, axis=0, keepdims=True)
        dx1 = dx2_ref[...] + _rms_bwd(dh, g, r, xh)
        dx1_ref[...] = dx1
        dx1h_ref[...] = dx1.astype(BF16)

    row = _rows(TOK, 1024)
    nar = _rows(TOK, 256)
    tab = _rows(TOK, 128)
    acc = _whole((1, 1024))
    return pl.pallas_call(
        body, name="in_proj1_bwd", grid=(s // TOK,),
        in_specs=[row, nar, nar, row, row, row, acc, _whole(w_in1.shape), tab, tab, tab],
        out_specs=[_rows(TOK, 2560), row, acc, row],
        out_shape=[jax.ShapeDtypeStruct((s, 2560), BF16), jax.ShapeDtypeStruct((s, 1024), F32),
                   jax.ShapeDtypeStruct((1, 1024), F32), jax.ShapeDtypeStruct((s, 1024), BF16)],
        compiler_params=_params(),
    )(dq, dk, dv, dgate, x1, dx2, g1, w_in1, *rope)


def _mix0_bwd(dx1, z0, w_out0, gv, ws, bsx, wg, scale, band):
    s = dx1.shape[0]
    n_tiles = s // TOK

    def body(dx1h_ref, z_ref, zp_ref, zn_ref, wout_ref, gv_ref, ws_ref, bsx_ref, wg_ref, sc_ref, band_ref,
             dza_ref, dzg_ref, dpn_ref, dws_ref, dbs_ref, dgv_ref, dsc_ref, dwg_ref):
        i = pl.program_id(0)

        @pl.when(i == 0)
        def _():
            for ref in (dws_ref, dbs_ref, dgv_ref, dsc_ref, dwg_ref):
                ref[...] = jnp.zeros_like(ref)

        dcat = _dot_nt(dx1h_ref[...], wout_ref[...])
        dya = dcat[:, 0:1024]
        dyb = dcat[:, 1024:2048]

        au = z_ref[:, 0:1024].astype(F32)
        av = z_ref[:, 1024:2048].astype(F32)
        ag = z_ref[:, 2048:3072].astype(F32)
        gv = gv_ref[...]
        u, rv, vh, v2, mixed = _mixer_a(au, av, gv, ws_ref, bsx_ref[...])
        sg = _sigmoid(ag)
        sil = ag * sg
        dza_ref[:, 2048:3072] = (dya * u * mixed * (sg * (1.0 + ag * (1.0 - sg)))).astype(BF16)
        dza_ref[:, 0:1024] = (dya * mixed * sil * _dgelu(au)).astype(BF16)
        dmixed = dya * u * sil
        lane = lax.broadcasted_iota(jnp.int32, (1, 128), 1)
        dm16 = dmixed.astype(BF16)
        dv2_rows = []
        for c in range(TOK // CHUNK):
            rows = slice(c * CHUNK, (c + 1) * CHUNK)
            cols_out = []
            for h in range(4):
                cols = slice(h * 256, (h + 1) * 256)
                dws_ref[h] += _dot_nt(dm16[rows, cols], v2[rows, cols])
                dbs_ref[...] += jnp.where(lane == h, jnp.sum(dmixed[rows, cols], axis=-1, keepdims=True), 0.0)
                cols_out.append(_dot_tn(ws_ref[h], dm16[rows, cols]))
            dv2_rows.append(jnp.concatenate(cols_out, axis=1))
        dv2 = jnp.concatenate(dv2_rows, axis=0)
        dgv_ref[...] += jnp.sum(dv2 * vh, axis=0, keepdims=True)
        dza_ref[:, 1024:2048] = (_rms_bwd(dv2, gv, rv, vh) * _dgelu(av)).astype(BF16)

        bx16 = z_ref[:, 3072:4096]
        bg = z_ref[:, 4096:5120].astype(F32)
        counts = _window_counts(i, TOK, s)
        halo = _with_halo(i, n_tiles, zp_ref, bx16, zn_ref)
        ps = [p.astype(BF16) for p in _mixer_b_pooled(bx16.astype(F32), halo, band_ref, counts)]
        pw = jnp.concatenate([_dot(ps[g], wg_ref[g]) for g in range(4)], axis=1)
        sgb = _sigmoid(bg)
        sc = sc_ref[...]
        dzg_ref[...] = (dyb * pw * sc * (sgb * (1.0 + bg * (1.0 - sgb)))).astype(BF16)
        dys = dyb * (bg * sgb)
        dsc_ref[...] += jnp.sum(dys * pw, axis=0, keepdims=True)
        dpw = (dys * sc).astype(BF16)
        for g in range(4):
            cols = slice(g * 256, (g + 1) * 256)
            dwg_ref[g] += _dot_tn(ps[g], dpw[:, cols])
            dpn_ref[:, cols] = (_dot_nt(dpw[:, cols], wg_ref[g]) / counts[g]).astype(BF16)

    prev, nxt = _halo_specs(s, 3)
    row = _rows(TOK, 1024)
    vec = _whole((1, 1024))
    return pl.pallas_call(
        body, name="mix0_bwd", grid=(n_tiles,),
        in_specs=[row, _rows(TOK, 5120), prev, nxt, _whole(w_out0.shape), vec, _whole(ws.shape),
                  _whole(bsx.shape), _whole(wg.shape), vec, _whole(band.shape)],
        out_specs=[_rows(TOK, 3072), row, row, _whole((4, 128, 128)), _whole((128, 128)), vec, vec,
                   _whole((4, 256, 256))],
        out_shape=[jax.ShapeDtypeStruct((s, 3072), BF16), jax.ShapeDtypeStruct((s, 1024), BF16),
                   jax.ShapeDtypeStruct((s, 1024), BF16), jax.ShapeDtypeStruct((4, 128, 128), F32),
                   jax.ShapeDtypeStruct((128, 128), F32), jax.ShapeDtypeStruct((1, 1024), F32),
                   jax.ShapeDtypeStruct((1, 1024), F32), jax.ShapeDtypeStruct((4, 256, 256), F32)],
        compiler_params=_params(),
    )(dx1, z0, z0, z0, w_out0, gv, ws, bsx, wg, scale, band)


def _in_proj0_bwd(dza, dzg, dpn, x, dx1, g0, w_in0, band_t):
    s = x.shape[0]
    n_tiles = s // TOK

    def body(dza_ref, dzg_ref, dpn_ref, dpp_ref, dpx_ref, x_ref, dx1_ref, g_ref, w_ref, band_ref,
             dz_ref, dx_ref, gn_ref):
        i = pl.program_id(0)

        @pl.when(i == 0)
        def _():
            gn_ref[...] = jnp.zeros_like(gn_ref)

        dz_ref[:, 0:3072] = dza_ref[...]
        dz_ref[:, 4096:5120] = dzg_ref[...]
        dpn = dpn_ref[...]
        halo = _with_halo(i, n_tiles, dpp_ref, dpn, dpx_ref)
        counts = _window_counts(i, TOK, s)
        for g in range(4):
            cols = slice(g * 256, (g + 1) * 256)
            dbx = _dot(band_ref[g], halo[:, cols]) - dpn[:, cols].astype(F32) * counts[g]
            dz_ref[:, 3072 + g * 256:3072 + (g + 1) * 256] = dbx.astype(BF16)
        dh = _dot_nt(dz_ref[...], w_ref[...])
        g0v = g_ref[...]
        r, xh, _ = _rms_fwd(x_ref[...], g0v)
        gn_ref[...] += jnp.sum(dh * xh, axis=0, keepdims=True)
        dx_ref[...] = dx1_ref[...] + _rms_bwd(dh, g0v, r, xh)

    prev, nxt = _halo_specs(s, 0)
    row = _rows(TOK, 1024)
    vec = _whole((1, 1024))
    return pl.pallas_call(
        body, name="in_proj0_bwd", grid=(n_tiles,),
        in_specs=[_rows(TOK, 3072), row, row, prev, nxt, row, row, vec, _whole(w_in0.shape), _whole(band_t.shape)],
        out_specs=[_rows(TOK, 5120), row, vec],
        out_shape=[jax.ShapeDtypeStruct((s, 5120), BF16), jax.ShapeDtypeStruct((s, 1024), F32),
                   jax.ShapeDtypeStruct((1, 1024), F32)],
        compiler_params=_params(),
    )(dza, dzg, dpn, dpn, dpn, x, dx1, g0, w_in0, band_t)


def _weight_grad(a, b, n_blocks, name):
    s, k = a.shape
    n = b.shape[1]
    tn = n // n_blocks
    ts = 512

    def body(a_ref, b_ref, o_ref):
        @pl.when(pl.program_id(1) == 0)
        def _():
            o_ref[...] = jnp.zeros_like(o_ref)

        o_ref[...] += _dot_tn(a_ref[...], b_ref[...])

    return pl.pallas_call(
        body, name=name, grid=(n_blocks, s // ts),
        in_specs=[pl.BlockSpec((ts, k), lambda j, t: (t, 0)), pl.BlockSpec((ts, tn), lambda j, t: (t, j))],
        out_specs=pl.BlockSpec((None, k, tn), lambda j, t: (j, 0, 0)),
        out_shape=jax.ShapeDtypeStruct((n_blocks, k, tn), F32),
        compiler_params=_params(),
    )(a, b)


def _row_tile(rows, cols):
    t = rows
    while t * cols * 4 > (1 << 20) and t % 16 == 0:
        t //= 2
    return t


def _add2(a, b, name):
    rows, cols = a.shape
    t = _row_tile(rows, cols)

    def body(a_ref, b_ref, o_ref):
        o_ref[...] = a_ref[...] + b_ref[...]

    spec = pl.BlockSpec((t, cols), lambda i: (i, 0))
    return pl.pallas_call(body, name=name, grid=(rows // t,), in_specs=[spec, spec], out_specs=spec,
                          out_shape=jax.ShapeDtypeStruct(a.shape, F32), compiler_params=_params())(a, b)


def _sum_slots(parts, name):
    n, rows, cols = parts.shape
    t = _row_tile(rows, cols)

    def body(p_ref, o_ref):
        acc = p_ref[0]
        for k in range(1, n):
            acc = acc + p_ref[k]
        o_ref[...] = acc

    return pl.pallas_call(
        body, name=name, grid=(rows // t,),
        in_specs=[pl.BlockSpec((n, t, cols), lambda i: (0, i, 0))],
        out_specs=pl.BlockSpec((t, cols), lambda i: (i, 0)),
        out_shape=jax.ShapeDtypeStruct((rows, cols), F32), compiler_params=_params())(parts)


def _adamw(w, g, m, v, name):
    rows, cols = w.shape
    t = _row_tile(rows, cols)

    def body(w_ref, g_ref, m_ref, v_ref, d_ref, nm_ref, nv_ref):
        g = g_ref[...]
        m2 = ADAM_B1 * m_ref[...] + (1.0 - ADAM_B1) * g
        v2 = ADAM_B2 * v_ref[...] + (1.0 - ADAM_B2) * (g * g)
        m_hat = m2 / (1.0 - ADAM_B1 ** ADAM_STEP)
        v_hat = v2 / (1.0 - ADAM_B2 ** ADAM_STEP)
        d_ref[...] = -ADAM_LR * (m_hat / (jnp.sqrt(v_hat) + ADAM_EPS) + ADAM_WD * w_ref[...])
        nm_ref[...] = m2
        nv_ref[...] = v2

    spec = pl.BlockSpec((t, cols), lambda i: (i, 0))
    shp = jax.ShapeDtypeStruct(w.shape, F32)
    return pl.pallas_call(body, name=name, grid=(rows // t,), in_specs=[spec] * 4, out_specs=[spec] * 3,
                          out_shape=[shp] * 3, compiler_params=_params())(w, g, m, v)


def _place():
    x, y, c = lax.axis_index("x"), lax.axis_index("y"), lax.axis_index("c")
    chips = [(1 - x, y), (x, 1 - y), (1 - x, 1 - y)]
    return x, y, c, chips


class _Sharded:
    def __init__(self, kind, full_shape):
        self.kind = kind
        self.full_shape = full_shape
        self.chunk_axis = 1 if kind == "mid" else 0

    def in_full(self, ref, s, h):
        if self.kind == "cols":
            r, n = self.full_shape
            return ref.at[pl.ds(h * (r // 2), r // 2), pl.ds(pl.multiple_of(s * (n // 4), 128), n // 4)]
        if self.kind == "rows":
            r, _ = self.full_shape
            return ref.at[pl.ds(pl.multiple_of(s * (r // 4) + h * (r // 8), 8), r // 8), :]
        g, r, _ = self.full_shape
        return ref.at[pl.ds(h * (g // 2), g // 2), pl.ds(pl.multiple_of(s * (r // 4), 16), r // 4), :]

    def in_shard(self, ref, h):
        if self.kind == "cols":
            r = self.full_shape[0]
            return ref.at[pl.ds(h * (r // 2), r // 2), :]
        if self.kind == "rows":
            r = self.full_shape[0]
            return ref.at[pl.ds(h * (r // 8), r // 8), :]
        g = self.full_shape[0]
        return ref.at[pl.ds(h * (g // 2), g // 2)]


CHUNK_BYTES = 256 * 1024
MAX_CHUNKS = 32


def _cut(ref, axis, n):
    step = ref.shape[axis] // n
    lead = (slice(None),) * axis
    return [ref.at[lead + (pl.ds(k * step, step),)] for k in range(n)]


def _n_chunks(ref, axis):
    nbytes = int(np.prod(ref.shape)) * jnp.dtype(ref.dtype).itemsize
    n = 1
    while n < MAX_CHUNKS and nbytes // (2 * n) >= CHUNK_BYTES and ref.shape[axis] % (32 * n) == 0:
        n *= 2
    return n


def _remote(src, dst, send_sem, recv_sem, to):
    return pltpu.make_async_remote_copy(src_ref=src, dst_ref=dst, send_sem=send_sem, recv_sem=recv_sem,
                                        device_id=to, device_id_type=MESH)


def _start_remote(src, dst, axis, send_sem, recv_sem, to):
    n = _n_chunks(dst, axis)
    for s_k, d_k in zip(_cut(src, axis, n), _cut(dst, axis, n)):
        _remote(s_k, d_k, send_sem, recv_sem, to).start()
    return _remote(src, dst, send_sem, recv_sem, to)


def _start_local(src, dst, axis, sem):
    n = _n_chunks(dst, axis)
    for s_k, d_k in zip(_cut(src, axis, n), _cut(dst, axis, n)):
        pltpu.make_async_copy(s_k, d_k, sem).start()
    return pltpu.make_async_copy(src, dst, sem)


def _gather_weights(shards, cuts):
    n = len(shards)

    def body(*refs):
        src = refs[:n]
        out = refs[n:2 * n]
        send_sems, recv_sems, local_sems = refs[2 * n:]
        x, y, c, chips = _place()
        me = 2 * x + y
        sibling = (x, y, 1 - c)

        def ends(w, s, h, from_shard):
            dst = cuts[w].in_full(out[w], s, h)
            return (cuts[w].in_shard(src[w], h) if from_shard else dst), dst

        def remote(k, w, s, h, to, from_shard):
            return _remote(*ends(w, s, h, from_shard), send_sems.at[k], recv_sems.at[k], to)

        def start(k, w, s, h, to, from_shard):
            return _start_remote(*ends(w, s, h, from_shard), cuts[w].chunk_axis, send_sems.at[k], recv_sems.at[k], to)

        local = []
        for w in range(n):
            for h in range(2):
                local.append(_start_local(cuts[w].in_shard(src[w], h), cuts[w].in_full(out[w], me, h),
                                          cuts[w].chunk_axis, local_sems.at[2 * w + h]))
        first = []
        for w in range(n):
            for j, chip in enumerate(chips):
                first.append(start(3 * w + j, w, me, c, (*chip, c), True))
        passed = []
        for w in range(n):
            for j, chip in enumerate(chips):
                s = 2 * chip[0] + chip[1]
                remote(3 * w + j, w, s, c, (x, y, c), False).wait_recv()
                passed.append(start(3 * n + 3 * w + j, w, s, c, sibling, False))
        for w in range(n):
            for j, chip in enumerate(chips):
                s = 2 * chip[0] + chip[1]
                remote(3 * n + 3 * w + j, w, s, 1 - c, (x, y, c), False).wait_recv()
        for cp in first + passed:
            cp.wait_send()
        for cp in local:
            cp.wait()

    return pl.pallas_call(
        body, name="gather_weights",
        in_specs=[ANY] * n, out_specs=[ANY] * n,
        out_shape=[jax.ShapeDtypeStruct(cuts[w].full_shape, BF16) for w in range(n)],
        scratch_shapes=[pltpu.SemaphoreType.DMA((6 * n,)), pltpu.SemaphoreType.DMA((6 * n,)),
                        pltpu.SemaphoreType.DMA((2 * n,))],
        compiler_params=pltpu.CompilerParams(has_side_effects=True),
    )(*shards)


def _exchange_halves(grads, small):
    n = len(grads)
    flips = [(fx, fy, fc) for fx in range(2) for fy in range(2) for fc in range(2)][1:]

    def body(*refs):
        g = refs[:n]
        small_ref = refs[n]
        mine = refs[n + 1:2 * n + 1]
        theirs = refs[2 * n + 1:3 * n + 1]
        gathered = refs[3 * n + 1]
        send_sems, recv_sems, local_sems = refs[3 * n + 2:]
        x, y, c, _ = _place()
        me = 4 * x + 2 * y + c
        local = [_start_local(g[w].at[:, c], mine[w], 1, local_sems.at[w]) for w in range(n)]
        local.append(_start_local(small_ref, gathered.at[me], 0, local_sems.at[n]))
        sends = []
        for w in range(n):
            sends.append(_start_remote(g[w].at[:, 1 - c], theirs[w], 1, send_sems.at[w], recv_sems.at[w],
                                       (x, y, 1 - c)))
        for k, (fx, fy, fc) in enumerate(flips):
            sends.append(_start_remote(
                small_ref, gathered.at[me], 0, send_sems.at[n + k], recv_sems.at[n + k],
                (x + fx - 2 * x * fx, y + fy - 2 * y * fy, c + fc - 2 * c * fc)))
        for w in range(n):
            _remote(g[w].at[:, 1 - c], theirs[w], send_sems.at[w], recv_sems.at[w], (x, y, c)).wait_recv()
        for k, (fx, fy, fc) in enumerate(flips):
            peer = 4 * (x + fx - 2 * x * fx) + 2 * (y + fy - 2 * y * fy) + (c + fc - 2 * c * fc)
            _remote(small_ref, gathered.at[peer], send_sems.at[n + k], recv_sems.at[n + k], (x, y, c)).wait_recv()
        for cp in sends:
            cp.wait_send()
        for cp in local:
            cp.wait()

    halves = [jax.ShapeDtypeStruct((4,) + g.shape[2:], F32) for g in grads]
    return pl.pallas_call(
        body, name="exchange_halves",
        in_specs=[ANY] * (n + 1), out_specs=[ANY] * (2 * n + 1),
        out_shape=halves + halves + [jax.ShapeDtypeStruct((8,) + small.shape, F32)],
        scratch_shapes=[pltpu.SemaphoreType.DMA((n + 7,)), pltpu.SemaphoreType.DMA((n + 7,)),
                        pltpu.SemaphoreType.DMA((n + 1,))],
        compiler_params=pltpu.CompilerParams(has_side_effects=True),
    )(*grads, small)


def _scatter_to_chips(parts):
    n = len(parts)

    def body(*refs):
        p = refs[:n]
        out = refs[n:2 * n]
        send_sems, recv_sems, local_sems = refs[2 * n:]
        x, y, c, chips = _place()
        me = 2 * x + y
        local = [_start_local(p[w].at[me], out[w].at[me], 0, local_sems.at[w]) for w in range(n)]
        sends = []
        for w in range(n):
            for j, chip in enumerate(chips):
                s = 2 * chip[0] + chip[1]
                sends.append(_start_remote(p[w].at[s], out[w].at[me], 0, send_sems.at[3 * w + j],
                                           recv_sems.at[3 * w + j], (*chip, c)))
        for w in range(n):
            for j, chip in enumerate(chips):
                s = 2 * chip[0] + chip[1]
                _remote(p[w].at[me], out[w].at[s], send_sems.at[3 * w + j], recv_sems.at[3 * w + j],
                        (x, y, c)).wait_recv()
        for cp in sends:
            cp.wait_send()
        for cp in local:
            cp.wait()

    return pl.pallas_call(
        body, name="scatter_to_chips",
        in_specs=[ANY] * n, out_specs=[ANY] * n,
        out_shape=[jax.ShapeDtypeStruct(a.shape, F32) for a in parts],
        scratch_shapes=[pltpu.SemaphoreType.DMA((3 * n,)), pltpu.SemaphoreType.DMA((3 * n,)),
                        pltpu.SemaphoreType.DMA((n,))],
        compiler_params=pltpu.CompilerParams(has_side_effects=True),
    )(*parts)


def _share_halves(halves):
    n = len(halves)

    def body(*refs):
        hv = refs[:n]
        out = refs[n:2 * n]
        send_sems, recv_sems, local_sems = refs[2 * n:]
        x, y, c, _ = _place()
        local = [_start_local(hv[w], out[w].at[c], 0, local_sems.at[w]) for w in range(n)]
        sends = [_start_remote(hv[w], out[w].at[c], 0, send_sems.at[w], recv_sems.at[w], (x, y, 1 - c))
                 for w in range(n)]
        for w in range(n):
            _remote(hv[w], out[w].at[1 - c], send_sems.at[w], recv_sems.at[w], (x, y, c)).wait_recv()
        for cp in sends:
            cp.wait_send()
        for cp in local:
            cp.wait()

    return pl.pallas_call(
        body, name="share_halves",
        in_specs=[ANY] * n, out_specs=[ANY] * n,
        out_shape=[jax.ShapeDtypeStruct((2,) + a.shape, F32) for a in halves],
        scratch_shapes=[pltpu.SemaphoreType.DMA((n,)), pltpu.SemaphoreType.DMA((n,)),
                        pltpu.SemaphoreType.DMA((n,))],
        compiler_params=pltpu.CompilerParams(has_side_effects=True),
    )(*halves)


SMALL_ROWS = 80


def _pack_small(vecs, ws, bs, sink):
    top = jnp.concatenate(
        [v.reshape(1, 1024) for v in vecs]
        + [jnp.pad(bs.reshape(1, 512), ((0, 0), (0, 512))), jnp.pad(sink.reshape(1, 16), ((0, 0), (0, 1008))),
           jnp.zeros((1, 1024), F32)], axis=0)
    return jnp.concatenate([top, ws.reshape(64, 1024), jnp.zeros((8, 1024), F32)], axis=0)


def _unpack_small(p):
    vecs = [p[k] for k in range(5)]
    return vecs, p[8:72].reshape(4, 128, 128), p[5, :512].reshape(4, 128), p[6, :16]


def kernel(x, norm_0, w_in_0, a_v_norm_0, a_spatial_w_0, a_spatial_b_0, b_group_w_0, b_scale_0, w_out_0, norm_1, w_in_1, sink_1, w_out_1, final_norm, loss_target, m_norm_0, m_w_in_0, m_a_v_norm_0, m_a_spatial_w_0, m_a_spatial_b_0, m_b_group_w_0, m_b_scale_0, m_w_out_0, m_norm_1, m_w_in_1, m_sink_1, m_w_out_1, m_final_norm, v_norm_0, v_w_in_0, v_a_v_norm_0, v_a_spatial_w_0, v_a_spatial_b_0, v_b_group_w_0, v_b_scale_0, v_w_out_0, v_norm_1, v_w_in_1, v_sink_1, v_w_out_1, v_final_norm):
    s = x.shape[1]
    xs = x.reshape(s, D_MODEL)
    target = loss_target.reshape(s, D_MODEL)

    cuts = [_Sharded("cols", (1024, 5120)), _Sharded("rows", (2048, 1024)), _Sharded("cols", (1024, 2560)),
            _Sharded("rows", (1024, 1024)), _Sharded("mid", (4, 256, 256))]
    big_w = [w_in_0, w_out_0, w_in_1, w_out_1, b_group_w_0]
    big_m = [m_w_in_0, m_w_out_0, m_w_in_1, m_w_out_1, m_b_group_w_0]
    big_v = [v_w_in_0, v_w_out_0, v_w_in_1, v_w_out_1, v_b_group_w_0]
    w_in0, w_out0, w_in1, w_out1, wg = _gather_weights([w.astype(BF16) for w in big_w], cuts)

    row = lambda v: v.reshape(1, 1024)
    ws16 = a_spatial_w_0.astype(BF16)
    bsx = jnp.repeat(a_spatial_b_0.T, 256, axis=1)
    band, band_t = _band_matrices(TOK)
    rope = _rope_tables(s)

    h0, z0 = _in_proj0(xs, row(norm_0), w_in0)
    cat, x1 = _mix0_fwd(xs, z0, w_out0, row(a_v_norm_0), ws16, bsx, wg, row(b_scale_0), band)
    h1, q, k, v, gate = _in_proj1(x1, row(norm_1), w_in1, rope)
    kpad = jnp.pad(k, ((ATTN_WINDOW, ATTN_WINDOW), (0, 0)))
    vpad = jnp.pad(v, ((ATTN_WINDOW, ATTN_WINDOW), (0, 0)))
    o = _attn_fwd(q, kpad, vpad, sink_1)
    y1, dx2, do, dgate, loss_lanes, g_final, dx2h = _tail(x1, o, gate, target, w_out1, row(final_norm))

    dq, dkpad, dvpad, dsink = _attn_bwd(q, kpad, vpad, sink_1, o, do, rope)
    dk = dkpad[ATTN_WINDOW:ATTN_WINDOW + s]
    dv = dvpad[ATTN_WINDOW:ATTN_WINDOW + s]
    dz1, dx1, g_norm1, dx1h = _in_proj1_bwd(dq, dk, dv, dgate, x1, dx2, row(norm_1), w_in1, rope)
    dza, dzg, dpn, d_ws, d_bs, d_gv, d_scale, d_wg = _mix0_bwd(
        dx1h, z0, w_out0, row(a_v_norm_0), ws16, bsx, wg, row(b_scale_0), band)
    dz0, grad_x, g_norm0 = _in_proj0_bwd(dza, dzg, dpn, xs, dx1, row(norm_0), w_in0, band_t)

    g_w_in0 = _weight_grad(h0, dz0, 4, "grad_w_in0")
    g_w_out0 = _weight_grad(cat, dx1h, 1, "grad_w_out0")
    g_w_in1 = _weight_grad(h1, dz1, 4, "grad_w_in1")
    g_w_out1 = _weight_grad(y1, dx2h, 1, "grad_w_out1")

    grads = [g_w_in0.reshape(4, 2, 512, 1280), g_w_out0.reshape(4, 2, 256, 1024), g_w_in1.reshape(4, 2, 512, 640),
             g_w_out1.reshape(4, 2, 128, 1024),
             d_wg.reshape(2, 2, 4, 64, 256).transpose(2, 0, 1, 3, 4).reshape(4, 2, 128, 256)]
    small = _pack_small([g_norm0, d_gv, d_scale, g_norm1, g_final], d_ws, d_bs[:, :4].T, dsink[0, :16])
    *halves, small_all = _exchange_halves(grads, small)
    n = len(grads)
    parts = []
    for w in range(n):
        mine, theirs = halves[w], halves[n + w]
        r, c_ = mine.shape[1:]
        parts.append(_add2(mine.reshape(4 * r, c_), theirs.reshape(4 * r, c_), f"add_sibling{w}").reshape(4, r, c_))
    slots = _scatter_to_chips(parts)
    reduced = [_sum_slots(slots[w], f"sum_chips{w}") for w in range(n)]
    shards = _share_halves(reduced)

    out_g, out_d, out_m, out_v = {}, {}, {}, {}
    names = ["w_in_0", "w_out_0", "w_in_1", "w_out_1", "b_group_w_0"]
    for w in range(n):
        shape = big_w[w].shape
        if cuts[w].kind == "mid":
            g = shards[w].reshape(4, 64, 256)
        else:
            g = shards[w].reshape(shape)
        two_d = (-1, shape[-1])
        d, nm, nv = _adamw(big_w[w].reshape(two_d), g.reshape(two_d), big_m[w].reshape(two_d),
                           big_v[w].reshape(two_d), f"adamw{w}")
        out_g[names[w]], out_d[names[w]], out_m[names[w]], out_v[names[w]] = (
            g, d.reshape(shape), nm.reshape(shape), nv.reshape(shape))

    g_small = _sum_slots(small_all, "sum_small")
    small_names = ["norm_0", "a_v_norm_0", "b_scale_0", "norm_1", "final_norm"]
    pack = lambda vecs, ws_, bs_, sk: _pack_small(vecs, ws_, bs_, sk)
    w_small = pack([norm_0, a_v_norm_0, b_scale_0, norm_1, final_norm], a_spatial_w_0, a_spatial_b_0, sink_1)
    m_small = pack([m_norm_0, m_a_v_norm_0, m_b_scale_0, m_norm_1, m_final_norm], m_a_spatial_w_0,
                   m_a_spatial_b_0, m_sink_1)
    v_small = pack([v_norm_0, v_a_v_norm_0, v_b_scale_0, v_norm_1, v_final_norm], v_a_spatial_w_0,
                   v_a_spatial_b_0, v_sink_1)
    d_small, nm_small, nv_small = _adamw(w_small, g_small, m_small, v_small, "adamw_small")
    for store, packed in ((out_g, g_small), (out_d, d_small), (out_m, nm_small), (out_v, nv_small)):
        vecs, ws_, bs_, sk = _unpack_small(packed)
        for name, vec in zip(small_names, vecs):
            store[name] = vec
        store["a_spatial_w_0"], store["a_spatial_b_0"], store["sink_1"] = ws_, bs_, sk

    loss = lax.psum(jnp.sum(loss_lanes), ("x", "y", "c"))
    order = ["norm_0", "w_in_0", "a_v_norm_0", "a_spatial_w_0", "a_spatial_b_0", "b_group_w_0", "b_scale_0",
             "w_out_0", "norm_1", "w_in_1", "sink_1", "w_out_1", "final_norm"]
    return (loss, grad_x.reshape(1, s, D_MODEL), *[out_g[k] for k in order], *[out_d[k] for k in order],
            *[out_m[k] for k in order], *[out_v[k] for k in order])
```

```python
import functools

import numpy as np
import jax
import jax.numpy as jnp
from jax import lax
from jax.experimental import pallas as pl
from jax.experimental.pallas import tpu as pltpu

F32 = jnp.float32
BF16 = jnp.bfloat16
MESH = pl.DeviceIdType.MESH

D_MODEL = 1024
EPS = 1e-6
NEG_INF = -1e30
CHUNK = 128
POOL_WINDOWS = (2, 4, 8, 16)
HALO = 16
N_HEADS = 16
HEAD_DIM = 64
ATTN_WINDOW = 128
ROPE_THETA = 500000.0
ROT_DIM = 16
ADAM_LR = 0.001
ADAM_B1 = 0.9
ADAM_B2 = 0.999
ADAM_EPS = 1e-08
ADAM_WD = 0.01
ADAM_STEP = 10

TOK = 256
VMEM_LIMIT = 56 * 1024 * 1024


def _params(**kw):
    return pltpu.CompilerParams(vmem_limit_bytes=VMEM_LIMIT, **kw)


def _whole(shape):
    nd = len(shape)
    return pl.BlockSpec(shape, lambda *_: (0,) * nd)


def _rows(t, n):
    return pl.BlockSpec((t, n), lambda i: (i, 0))


ANY = pl.BlockSpec(memory_space=pl.ANY)

_G0 = 0.7978845608028654
_G1 = 0.044715


def _gelu(x):
    return 0.5 * x * (1.0 + jnp.tanh(_G0 * (x + _G1 * x * x * x)))


def _dgelu(x):
    t = jnp.tanh(_G0 * (x + _G1 * x * x * x))
    return 0.5 * (1.0 + t) + 0.5 * x * (1.0 - t * t) * (_G0 * (1.0 + 3.0 * _G1 * x * x))


def _sigmoid(x):
    return 1.0 / (1.0 + jnp.exp(-x))


def _dot(a, b):
    return jnp.dot(a, b, preferred_element_type=F32)


def _dot_nt(a, b):
    return lax.dot_general(a, b, (((1,), (1,)), ((), ())), preferred_element_type=F32)


def _dot_tn(a, b):
    return lax.dot_general(a, b, (((0,), (0,)), ((), ())), preferred_element_type=F32)


def _rms_fwd(x, g):
    r = lax.rsqrt(jnp.mean(x * x, axis=-1, keepdims=True) + EPS)
    xh = x * r
    return r, xh, xh * g


def _rms_bwd(dy, g, r, xh):
    dxh = dy * g
    return r * (dxh - xh * jnp.mean(dxh * xh, axis=-1, keepdims=True))


def _band_matrices(t):
    r = np.arange(t)[:, None]
    j = np.arange(t + 2 * HALO)[None, :]
    fwd, bwd = [], []
    for w in POOL_WINDOWS:
        d = j - r - HALO
        fwd.append((d >= -(w // 2)) & (d < w // 2))
        bwd.append((d >= -(w // 2) + 1) & (d <= w // 2))
    return (jnp.asarray(np.stack(fwd), BF16), jnp.asarray(np.stack(bwd), BF16))


def _window_counts(i, t, s):
    tok = i * t + lax.broadcasted_iota(jnp.int32, (t, 1), 0)
    out = []
    for w in POOL_WINDOWS:
        cnt = jnp.minimum(tok + w // 2, s) - jnp.maximum(tok - w // 2, 0)
        out.append(cnt.astype(F32))
    return out


def _rope_tables(s):
    inv = np.float32(ROPE_THETA) ** (-np.arange(0, ROT_DIM, 2, dtype=np.float32) / np.float32(ROT_DIM))
    ang = np.arange(s, dtype=np.float32)[:, None] * inv.astype(np.float32)[None, :]
    cos, sin = np.cos(ang).astype(np.float32), np.sin(ang).astype(np.float32)
    z8 = np.zeros((s, 8), np.float32)
    z48 = np.zeros((s, HEAD_DIM - ROT_DIM), np.float32)
    c = np.concatenate([cos, cos, np.ones((s, HEAD_DIM - ROT_DIM), np.float32)], axis=1)
    s_lo = np.concatenate([z8, sin, z48], axis=1)
    s_hi = np.concatenate([-sin, z8, z48], axis=1)
    return tuple(jnp.asarray(np.concatenate([a, a], axis=1)) for a in (c, s_lo, s_hi))


def _rope(x, c, s_lo, s_hi):
    n = x.shape[1]
    reps = n // 128
    c, s_lo, s_hi = (jnp.tile(a, (1, reps)) for a in (c, s_lo, s_hi))
    return x * c + pltpu.roll(x, 8, 1) * s_lo + pltpu.roll(x, n - 8, 1) * s_hi


def _rope_t(dx, c, s_lo, s_hi):
    n = dx.shape[1]
    reps = n // 128
    c, s_lo, s_hi = (jnp.tile(a, (1, reps)) for a in (c, s_lo, s_hi))
    return dx * c + pltpu.roll(dx * s_lo, n - 8, 1) + pltpu.roll(dx * s_hi, 8, 1)


def _in_proj0(x, g0, w_in0):
    s = x.shape[0]
    n = w_in0.shape[1]

    def body(x_ref, g_ref, w_ref, h_ref, z_ref):
        _, _, h = _rms_fwd(x_ref[...], g_ref[...])
        h = h.astype(BF16)
        h_ref[...] = h
        for j in range(n // 1024):
            z_ref[:, j * 1024:(j + 1) * 1024] = _dot(h, w_ref[:, j * 1024:(j + 1) * 1024]).astype(BF16)

    return pl.pallas_call(
        body, name="in_proj0", grid=(s // TOK,),
        in_specs=[_rows(TOK, D_MODEL), _whole((1, D_MODEL)), _whole(w_in0.shape)],
        out_specs=[_rows(TOK, D_MODEL), _rows(TOK, n)],
        out_shape=[jax.ShapeDtypeStruct((s, D_MODEL), BF16), jax.ShapeDtypeStruct((s, n), BF16)],
        compiler_params=_params(),
    )(x, g0, w_in0)


def _halo_specs(s, col_block):
    per = TOK // HALO
    last = s // HALO - 1
    prev = pl.BlockSpec((HALO, 1024), lambda i: (jnp.maximum(i * per - 1, 0), col_block))
    nxt = pl.BlockSpec((HALO, 1024), lambda i: (jnp.minimum((i + 1) * per, last), col_block))
    return prev, nxt


def _with_halo(i, n_tiles, prev_ref, cur, next_ref):
    prev = prev_ref[...]
    nxt = next_ref[...]
    prev = jnp.where(i > 0, prev, jnp.zeros_like(prev))
    nxt = jnp.where(i < n_tiles - 1, nxt, jnp.zeros_like(nxt))
    return jnp.concatenate([prev, cur, nxt], axis=0)


def _mixer_a(au, av, gv, ws_ref, bsx):
    u = _gelu(au)
    v1 = _gelu(av)
    rv, vh, v2 = _rms_fwd(v1, gv)
    v2 = v2.astype(BF16)
    rows = []
    for c in range(au.shape[0] // CHUNK):
        cols = [_dot(ws_ref[h], v2[c * CHUNK:(c + 1) * CHUNK, h * 256:(h + 1) * 256]) for h in range(4)]
        rows.append(jnp.concatenate(cols, axis=1) + bsx)
    return u, rv, vh, v2, jnp.concatenate(rows, axis=0)


def _mixer_b_pooled(bx, halo, band_ref, counts):
    out = []
    for g in range(4):
        win = _dot(band_ref[g], halo[:, g * 256:(g + 1) * 256])
        out.append(win / counts[g] - bx[:, g * 256:(g + 1) * 256])
    return out


def _mix0_fwd(x, z0, w_out0, gv, ws, bsx, wg, scale, band):
    s = x.shape[0]
    n_tiles = s // TOK

    def body(z_ref, zp_ref, zn_ref, x_ref, wout_ref, gv_ref, ws_ref, bsx_ref, wg_ref, sc_ref, band_ref,
             cat_ref, x1_ref):
        i = pl.program_id(0)
        au = z_ref[:, 0:1024].astype(F32)
        av = z_ref[:, 1024:2048].astype(F32)
        ag = z_ref[:, 2048:3072].astype(F32)
        u, _, _, _, mixed = _mixer_a(au, av, gv_ref[...], ws_ref, bsx_ref[...])
        cat_ref[:, 0:1024] = (u * mixed * (ag * _sigmoid(ag))).astype(BF16)

        bx16 = z_ref[:, 3072:4096]
        bg = z_ref[:, 4096:5120].astype(F32)
        halo = _with_halo(i, n_tiles, zp_ref, bx16, zn_ref)
        ps = _mixer_b_pooled(bx16.astype(F32), halo, band_ref, _window_counts(i, TOK, s))
        pw = jnp.concatenate([_dot(ps[g].astype(BF16), wg_ref[g]) for g in range(4)], axis=1)
        cat_ref[:, 1024:2048] = (pw * sc_ref[...] * (bg * _sigmoid(bg))).astype(BF16)

        x1_ref[...] = x_ref[...] + _dot(cat_ref[...], wout_ref[...])

    prev, nxt = _halo_specs(s, 3)
    return pl.pallas_call(
        body, name="mix0_fwd", grid=(n_tiles,),
        in_specs=[_rows(TOK, 5120), prev, nxt, _rows(TOK, D_MODEL), _whole(w_out0.shape), _whole((1, 1024)),
                  _whole(ws.shape), _whole(bsx.shape), _whole(wg.shape), _whole((1, 1024)), _whole(band.shape)],
        out_specs=[_rows(TOK, 2048), _rows(TOK, D_MODEL)],
        out_shape=[jax.ShapeDtypeStruct((s, 2048), BF16), jax.ShapeDtypeStruct((s, D_MODEL), F32)],
        compiler_params=_params(),
    )(z0, z0, z0, x, w_out0, gv, ws, bsx, wg, scale, band)


def _in_proj1(x1, g1, w_in1, rope):
    s = x1.shape[0]

    def body(x_ref, g_ref, w_ref, c_ref, lo_ref, hi_ref, h_ref, q_ref, k_ref, v_ref, gate_ref):
        _, _, h = _rms_fwd(x_ref[...], g_ref[...])
        h = h.astype(BF16)
        h_ref[...] = h
        tabs = (c_ref[...], lo_ref[...], hi_ref[...])
        q_ref[...] = (_rope(_dot(h, w_ref[:, 0:1024]), *tabs) * Q_SCALE).astype(BF16)
        kv = _dot(h, w_ref[:, 1024:1536])
        k_ref[...] = _rope(kv[:, 0:256], *tabs).astype(BF16)
        v_ref[...] = kv[:, 256:512].astype(BF16)
        gate_ref[...] = _dot(h, w_ref[:, 1536:2560]).astype(BF16)

    tab = _rows(TOK, 128)
    return pl.pallas_call(
        body, name="in_proj1", grid=(s // TOK,),
        in_specs=[_rows(TOK, D_MODEL), _whole((1, D_MODEL)), _whole(w_in1.shape), tab, tab, tab],
        out_specs=[_rows(TOK, 1024), _rows(TOK, 1024), _rows(TOK, 256), _rows(TOK, 256), _rows(TOK, 1024)],
        out_shape=[jax.ShapeDtypeStruct((s, 1024), BF16), jax.ShapeDtypeStruct((s, 1024), BF16),
                   jax.ShapeDtypeStruct((s, 256), BF16), jax.ShapeDtypeStruct((s, 256), BF16),
                   jax.ShapeDtypeStruct((s, 1024), BF16)],
        compiler_params=_params(),
    )(x1, g1, w_in1, *rope)


QBLK = 128
KBLK = QBLK + 2 * ATTN_WINDOW
Q_SCALE = HEAD_DIM ** -0.5


def _block_bias(q0, s):
    r = lax.broadcasted_iota(jnp.int32, (QBLK, KBLK), 0)
    c = lax.broadcasted_iota(jnp.int32, (QBLK, KBLK), 1)
    kj = q0 - ATTN_WINDOW + c
    ok = (c >= r) & (c <= r + 2 * ATTN_WINDOW) & (kj >= 0) & (kj < s)
    return jnp.where(ok, 0.0, NEG_INF)


def _exp_scores(qh, kg, bias, sink):
    sc = _dot_nt(qh, kg) + bias
    m = jnp.maximum(jnp.max(sc, axis=-1, keepdims=True), sink)
    e = jnp.exp(sc - m)
    e_sink = jnp.exp(sink - m)
    return e, e_sink, 1.0 / (jnp.sum(e, axis=-1, keepdims=True) + e_sink)


def _attn_fwd(q, kpad, vpad, sink):
    s = q.shape[0]

    def body(sink_ref, q_ref, k_ref, v_ref, o_ref):
        i = pl.program_id(0)
        for b in range(TOK // QBLK):
            rows = slice(b * QBLK, (b + 1) * QBLK)
            start = pl.multiple_of(i * TOK + b * QBLK, QBLK)
            kb = k_ref[pl.ds(start, KBLK), :]
            vb = v_ref[pl.ds(start, KBLK), :]
            bias = _block_bias(i * TOK + b * QBLK, s)
            outs = []
            for h in range(N_HEADS):
                g = h // 4
                cols = slice(g * HEAD_DIM, (g + 1) * HEAD_DIM)
                e, _, rden = _exp_scores(q_ref[rows, h * HEAD_DIM:(h + 1) * HEAD_DIM], kb[:, cols], bias, sink_ref[h])
                outs.append(_dot(e.astype(BF16), vb[:, cols]) * rden)
            o_ref[rows, :] = jnp.concatenate(outs, axis=1).astype(BF16)

    return pl.pallas_call(
        body, name="attn_fwd", grid=(s // TOK,),
        in_specs=[pl.BlockSpec(memory_space=pltpu.SMEM), _rows(TOK, 1024), _whole(kpad.shape), _whole(vpad.shape)],
        out_specs=_rows(TOK, 1024),
        out_shape=jax.ShapeDtypeStruct((s, 1024), BF16),
        compiler_params=_params(),
    )(sink, q, kpad, vpad)


def _tail(x1, o, gate, target, w_out1, gf):
    s = x1.shape[0]

    def body(x1_ref, o_ref, gate_ref, t_ref, w_ref, gf_ref, y1_ref, dx2_ref, do_ref, dgate_ref, loss_ref, gfn_ref,
             dx2h_ref):
        i = pl.program_id(0)

        @pl.when(i == 0)
        def _():
            loss_ref[...] = jnp.zeros_like(loss_ref)
            gfn_ref[...] = jnp.zeros_like(gfn_ref)

        g = gate_ref[...].astype(F32)
        sg = _sigmoid(g)
        sil = g * sg
        o = o_ref[...].astype(F32)
        y1 = (o * sil).astype(BF16)
        y1_ref[...] = y1
        x2 = x1_ref[...] + _dot(y1, w_ref[...])
        gf = gf_ref[...]
        r, xh, out = _rms_fwd(x2, gf)
        diff = out - t_ref[...]
        loss_ref[...] += jnp.sum(diff * diff, axis=0, keepdims=True) * (0.5 / D_MODEL)
        dout = diff * (1.0 / D_MODEL)
        gfn_ref[...] += jnp.sum(dout * xh, axis=0, keepdims=True)
        dx2 = _rms_bwd(dout, gf, r, xh)
        dx2_ref[...] = dx2
        dx2h = dx2.astype(BF16)
        dx2h_ref[...] = dx2h
        dy1 = _dot_nt(dx2h, w_ref[...])
        do_ref[...] = (dy1 * sil).astype(BF16)
        dgate_ref[...] = (dy1 * o * (sg * (1.0 + g * (1.0 - sg)))).astype(BF16)

    row = _rows(TOK, 1024)
    acc = _whole((1, 1024))
    return pl.pallas_call(
        body, name="tail", grid=(s // TOK,),
        in_specs=[row, row, row, row, _whole(w_out1.shape), acc],
        out_specs=[row, row, row, row, acc, acc, row],
        out_shape=[jax.ShapeDtypeStruct((s, 1024), BF16), jax.ShapeDtypeStruct((s, 1024), F32),
                   jax.ShapeDtypeStruct((s, 1024), BF16), jax.ShapeDtypeStruct((s, 1024), BF16),
                   jax.ShapeDtypeStruct((1, 1024), F32), jax.ShapeDtypeStruct((1, 1024), F32),
                   jax.ShapeDtypeStruct((s, 1024), BF16)],
        compiler_params=_params(),
    )(x1, o, gate, target, w_out1, gf)


def _attn_bwd(q, kpad, vpad, sink, o, do, rope):
    s = q.shape[0]

    def body(sink_ref, q_ref, k_ref, v_ref, o_ref, do_ref, c_ref, lo_ref, hi_ref, dq_ref, dk_ref, dv_ref, ds_ref):
        i = pl.program_id(0)

        @pl.when(i == 0)
        def _():
            dk_ref[...] = jnp.zeros_like(dk_ref)
            dv_ref[...] = jnp.zeros_like(dv_ref)
            ds_ref[...] = jnp.zeros_like(ds_ref)

        lane = lax.broadcasted_iota(jnp.int32, (1, 128), 1)
        dsink = jnp.zeros((1, 128), F32)
        for b in range(TOK // QBLK):
            rows = slice(b * QBLK, (b + 1) * QBLK)
            start = pl.multiple_of(i * TOK + b * QBLK, QBLK)
            kb = k_ref[pl.ds(start, KBLK), :]
            vb = v_ref[pl.ds(start, KBLK), :]
            bias = _block_bias(i * TOK + b * QBLK, s)
            dqs, dks, dvs = [], [], []
            for g in range(4):
                kg = kb[:, g * HEAD_DIM:(g + 1) * HEAD_DIM]
                vg = vb[:, g * HEAD_DIM:(g + 1) * HEAD_DIM]
                dkg = jnp.zeros((KBLK, HEAD_DIM), F32)
                dvg = jnp.zeros((KBLK, HEAD_DIM), F32)
                for j in range(4):
                    h = 4 * g + j
                    cols = slice(h * HEAD_DIM, (h + 1) * HEAD_DIM)
                    qh = q_ref[rows, cols]
                    doh = do_ref[rows, cols]
                    e, e_sink, rden = _exp_scores(qh, kg, bias, sink_ref[h])
                    p = e * rden
                    delta = jnp.sum(doh.astype(F32) * o_ref[rows, cols].astype(F32), axis=-1, keepdims=True)
                    dsink = dsink + jnp.where(lane == h, -jnp.sum(e_sink * rden * delta, axis=0, keepdims=True), 0.0)
                    dvg = dvg + _dot_tn(p.astype(BF16), doh)
                    dsc = (p * (_dot_nt(doh, vg) - delta)).astype(BF16)
                    dqs.append(_dot(dsc, kg) * Q_SCALE)
                    dkg = dkg + _dot_tn(dsc, qh)
                dks.append(dkg)
                dvs.append(dvg)
            dq = jnp.concatenate(dqs, axis=1)
            dq_ref[rows, :] = _rope_t(dq, c_ref[rows, :], lo_ref[rows, :], hi_ref[rows, :]).astype(BF16)
            dk_ref[pl.ds(start, KBLK), :] += jnp.concatenate(dks, axis=1)
            dv_ref[pl.ds(start, KBLK), :] += jnp.concatenate(dvs, axis=1)
        ds_ref[...] += dsink

    row = _rows(TOK, 1024)
    tab = _rows(TOK, 128)
    pad = _whole(kpad.shape)
    return pl.pallas_call(
        body, name="attn_bwd", grid=(s // TOK,),
        in_specs=[pl.BlockSpec(memory_space=pltpu.SMEM), row, pad, pad, row, row, tab, tab, tab],
        out_specs=[row, pad, pad, _whole((1, 128))],
        out_shape=[jax.ShapeDtypeStruct((s, 1024), BF16), jax.ShapeDtypeStruct(kpad.shape, F32),
                   jax.ShapeDtypeStruct(kpad.shape, F32), jax.ShapeDtypeStruct((1, 128), F32)],
        compiler_params=_params(),
    )(sink, q, kpad, vpad, o, do, *rope)


def _in_proj1_bwd(dq, dk, dv, dgate, x1, dx2, g1, w_in1, rope):
    s = x1.shape[0]

    def body(dq_ref, dk_ref, dv_ref, dgate_ref, x1_ref, dx2_ref, g_ref, w_ref, c_ref, lo_ref, hi_ref,
             dz_ref, dx1_ref, gn_ref, dx1h_ref):
        i = pl.program_id(0)

        @pl.when(i == 0)
        def _():
            gn_ref[...] = jnp.zeros_like(gn_ref)

        dz_ref[:, 0:1024] = dq_ref[...]
        dz_ref[:, 1024:1280] = _rope_t(dk_ref[...], c_ref[...], lo_ref[...], hi_ref[...]).astype(BF16)
        dz_ref[:, 1280:1536] = dv_ref[...].astype(BF16)
        dz_ref[:, 1536:2560] = dgate_ref[...]
        dh = _dot_nt(dz_ref[...], w_ref[...])
        g = g_ref[...]
        r, xh, _ = _rms_fwd(x1_ref[...], g)
        gn_ref[...] += jnp.sum(dh * xh, axis=0, keepdims=True)
        dx1 = dx2_ref[...] + _rms_bwd(dh, g, r, xh)
        dx1_ref[...] = dx1
        dx1h_ref[...] = dx1.astype(BF16)

    row = _rows(TOK, 1024)
    nar = _rows(TOK, 256)
    tab = _rows(TOK, 128)
    acc = _whole((1, 1024))
    return pl.pallas_call(
        body, name="in_proj1_bwd", grid=(s // TOK,),
        in_specs=[row, nar, nar, row, row, row, acc, _whole(w_in1.shape), tab, tab, tab],
        out_specs=[_rows(TOK, 2560), row, acc, row],
        out_shape=[jax.ShapeDtypeStruct((s, 2560), BF16), jax.ShapeDtypeStruct((s, 1024), F32),
                   jax.ShapeDtypeStruct((1, 1024), F32), jax.ShapeDtypeStruct((s, 1024), BF16)],
        compiler_params=_params(),
    )(dq, dk, dv, dgate, x1, dx2, g1, w_in1, *rope)


def _mix0_bwd(dx1, z0, w_out0, gv, ws, bsx, wg, scale, band):
    s = dx1.shape[0]
    n_tiles = s // TOK

    def body(dx1h_ref, z_ref, zp_ref, zn_ref, wout_ref, gv_ref, ws_ref, bsx_ref, wg_ref, sc_ref, band_ref,
             dza_ref, dzg_ref, dpn_ref, dws_ref, dbs_ref, dgv_ref, dsc_ref, dwg_ref):
        i = pl.program_id(0)

        @pl.when(i == 0)
        def _():
            for ref in (dws_ref, dbs_ref, dgv_ref, dsc_ref, dwg_ref):
                ref[...] = jnp.zeros_like(ref)

        dcat = _dot_nt(dx1h_ref[...], wout_ref[...])
        dya = dcat[:, 0:1024]
        dyb = dcat[:, 1024:2048]

        au = z_ref[:, 0:1024].astype(F32)
        av = z_ref[:, 1024:2048].astype(F32)
        ag = z_ref[:, 2048:3072].astype(F32)
        gv = gv_ref[...]
        u, rv, vh, v2, mixed = _mixer_a(au, av, gv, ws_ref, bsx_ref[...])
        sg = _sigmoid(ag)
        sil = ag * sg
        dza_ref[:, 2048:3072] = (dya * u * mixed * (sg * (1.0 + ag * (1.0 - sg)))).astype(BF16)
        dza_ref[:, 0:1024] = (dya * mixed * sil * _dgelu(au)).astype(BF16)
        dmixed = dya * u * sil
        lane = lax.broadcasted_iota(jnp.int32, (1, 128), 1)
        dm16 = dmixed.astype(BF16)
        dv2_rows = []
        for c in range(TOK // CHUNK):
            rows = slice(c * CHUNK, (c + 1) * CHUNK)
            cols_out = []
            for h in range(4):
                cols = slice(h * 256, (h + 1) * 256)
                dws_ref[h] += _dot_nt(dm16[rows, cols], v2[rows, cols])
                dbs_ref[...] += jnp.where(lane == h, jnp.sum(dmixed[rows, cols], axis=-1, keepdims=True), 0.0)
                cols_out.append(_dot_tn(ws_ref[h], dm16[rows, cols]))
            dv2_rows.append(jnp.concatenate(cols_out, axis=1))
        dv2 = jnp.concatenate(dv2_rows, axis=0)
        dgv_ref[...] += jnp.sum(dv2 * vh, axis=0, keepdims=True)
        dza_ref[:, 1024:2048] = (_rms_bwd(dv2, gv, rv, vh) * _dgelu(av)).astype(BF16)

        bx16 = z_ref[:, 3072:4096]
        bg = z_ref[:, 4096:5120].astype(F32)
        counts = _window_counts(i, TOK, s)
        halo = _with_halo(i, n_tiles, zp_ref, bx16, zn_ref)
        ps = [p.astype(BF16) for p in _mixer_b_pooled(bx16.astype(F32), halo, band_ref, counts)]
        pw = jnp.concatenate([_dot(ps[g], wg_ref[g]) for g in range(4)], axis=1)
        sgb = _sigmoid(bg)
        sc = sc_ref[...]
        dzg_ref[...] = (dyb * pw * sc * (sgb * (1.0 + bg * (1.0 - sgb)))).astype(BF16)
        dys = dyb * (bg * sgb)
        dsc_ref[...] += jnp.sum(dys * pw, axis=0, keepdims=True)
        dpw = (dys * sc).astype(BF16)
        for g in range(4):
            cols = slice(g * 256, (g + 1) * 256)
            dwg_ref[g] += _dot_tn(ps[g], dpw[:, cols])
            dpn_ref[:, cols] = (_dot_nt(dpw[:, cols], wg_ref[g]) / counts[g]).astype(BF16)

    prev, nxt = _halo_specs(s, 3)
    row = _rows(TOK, 1024)
    vec = _whole((1, 1024))
    return pl.pallas_call(
        body, name="mix0_bwd", grid=(n_tiles,),
        in_specs=[row, _rows(TOK, 5120), prev, nxt, _whole(w_out0.shape), vec, _whole(ws.shape),
                  _whole(bsx.shape), _whole(wg.shape), vec, _whole(band.shape)],
        out_specs=[_rows(TOK, 3072), row, row, _whole((4, 128, 128)), _whole((128, 128)), vec, vec,
                   _whole((4, 256, 256))],
        out_shape=[jax.ShapeDtypeStruct((s, 3072), BF16), jax.ShapeDtypeStruct((s, 1024), BF16),
                   jax.ShapeDtypeStruct((s, 1024), BF16), jax.ShapeDtypeStruct((4, 128, 128), F32),
                   jax.ShapeDtypeStruct((128, 128), F32), jax.ShapeDtypeStruct((1, 1024), F32),
                   jax.ShapeDtypeStruct((1, 1024), F32), jax.ShapeDtypeStruct((4, 256, 256), F32)],
        compiler_params=_params(),
    )(dx1, z0, z0, z0, w_out0, gv, ws, bsx, wg, scale, band)


def _in_proj0_bwd(dza, dzg, dpn, x, dx1, g0, w_in0, band_t):
    s = x.shape[0]
    n_tiles = s // TOK

    def body(dza_ref, dzg_ref, dpn_ref, dpp_ref, dpx_ref, x_ref, dx1_ref, g_ref, w_ref, band_ref,
             dz_ref, dx_ref, gn_ref):
        i = pl.program_id(0)

        @pl.when(i == 0)
        def _():
            gn_ref[...] = jnp.zeros_like(gn_ref)

        dz_ref[:, 0:3072] = dza_ref[...]
        dz_ref[:, 4096:5120] = dzg_ref[...]
        dpn = dpn_ref[...]
        halo = _with_halo(i, n_tiles, dpp_ref, dpn, dpx_ref)
        counts = _window_counts(i, TOK, s)
        for g in range(4):
            cols = slice(g * 256, (g + 1) * 256)
            dbx = _dot(band_ref[g], halo[:, cols]) - dpn[:, cols].astype(F32) * counts[g]
            dz_ref[:, 3072 + g * 256:3072 + (g + 1) * 256] = dbx.astype(BF16)
        dh = _dot_nt(dz_ref[...], w_ref[...])
        g0v = g_ref[...]
        r, xh, _ = _rms_fwd(x_ref[...], g0v)
        gn_ref[...] += jnp.sum(dh * xh, axis=0, keepdims=True)
        dx_ref[...] = dx1_ref[...] + _rms_bwd(dh, g0v, r, xh)

    prev, nxt = _halo_specs(s, 0)
    row = _rows(TOK, 1024)
    vec = _whole((1, 1024))
    return pl.pallas_call(
        body, name="in_proj0_bwd", grid=(n_tiles,),
        in_specs=[_rows(TOK, 3072), row, row, prev, nxt, row, row, vec, _whole(w_in0.shape), _whole(band_t.shape)],
        out_specs=[_rows(TOK, 5120), row, vec],
        out_shape=[jax.ShapeDtypeStruct((s, 5120), BF16), jax.ShapeDtypeStruct((s, 1024), F32),
                   jax.ShapeDtypeStruct((1, 1024), F32)],
        compiler_params=_params(),
    )(dza, dzg, dpn, dpn, dpn, x, dx1, g0, w_in0, band_t)


def _weight_grad(a, b, n_blocks, name):
    s, k = a.shape
    n = b.shape[1]
    tn = n // n_blocks
    ts = 512

    def body(a_ref, b_ref, o_ref):
        @pl.when(pl.program_id(1) == 0)
        def _():
            o_ref[...] = jnp.zeros_like(o_ref)

        o_ref[...] += _dot_tn(a_ref[...], b_ref[...])

    return pl.pallas_call(
        body, name=name, grid=(n_blocks, s // ts),
        in_specs=[pl.BlockSpec((ts, k), lambda j, t: (t, 0)), pl.BlockSpec((ts, tn), lambda j, t: (t, j))],
        out_specs=pl.BlockSpec((None, k, tn), lambda j, t: (j, 0, 0)),
        out_shape=jax.ShapeDtypeStruct((n_blocks, k, tn), F32),
        compiler_params=_params(),
    )(a, b)


def _row_tile(rows, cols):
    t = rows
    while t * cols * 4 > (1 << 20) and t % 16 == 0:
        t //= 2
    return t


def _add_sibling(where, g, theirs, name):
    _, _, rows, cols = g.shape
    t = _row_tile(rows, cols)

    def body(where_ref, g_ref, t_ref, o_ref):
        o_ref[...] = (g_ref[...] + t_ref[...]).astype(BF16)

    spec = pl.BlockSpec((None, t, cols), lambda s, i, p: (s, i, 0))
    return pl.pallas_call(
        body, name=name, out_shape=jax.ShapeDtypeStruct((4, rows, cols), BF16),
        grid_spec=pltpu.PrefetchScalarGridSpec(
            num_scalar_prefetch=1, grid=(4, rows // t),
            in_specs=[pl.BlockSpec((None, None, t, cols), lambda s, i, p: (s, p[1], i, 0)), spec], out_specs=spec),
        compiler_params=_params())(where, g, theirs)


def _sum_chips(where, g, theirs, slots, name):
    _, _, rows, cols = g.shape
    t = _row_tile(rows, cols)

    def body(where_ref, g_ref, t_ref, s_ref, o_ref):
        me = where_ref[0]
        own = g_ref[...] + t_ref[...]
        acc = jnp.where(me == 0, own, s_ref[0].astype(F32))
        for k in range(1, 4):
            acc = acc + jnp.where(me == k, own, s_ref[k].astype(F32))
        o_ref[...] = acc

    return pl.pallas_call(
        body, name=name, out_shape=jax.ShapeDtypeStruct((rows, cols), F32),
        grid_spec=pltpu.PrefetchScalarGridSpec(
            num_scalar_prefetch=1, grid=(rows // t,),
            in_specs=[pl.BlockSpec((None, None, t, cols), lambda i, p: (p[0], p[1], i, 0)),
                      pl.BlockSpec((None, t, cols), lambda i, p: (p[0], i, 0)),
                      pl.BlockSpec((4, t, cols), lambda i, p: (0, i, 0))],
            out_specs=pl.BlockSpec((t, cols), lambda i, p: (i, 0))),
        compiler_params=_params())(where, g, theirs, slots)


def _adamw_halves(where, w, own, theirs, m, v, name):
    rows, cols = own.shape
    t = _row_tile(rows, cols)
    per = rows // t

    def body(where_ref, w_ref, own_ref, th_ref, m_ref, v_ref, g_ref, d_ref, nm_ref, nv_ref):
        g = jnp.where(pl.program_id(0) == where_ref[1], own_ref[...], th_ref[...])
        g_ref[...] = g
        m2 = ADAM_B1 * m_ref[...] + (1.0 - ADAM_B1) * g
        v2 = ADAM_B2 * v_ref[...] + (1.0 - ADAM_B2) * (g * g)
        m_hat = m2 / (1.0 - ADAM_B1 ** ADAM_STEP)
        v_hat = v2 / (1.0 - ADAM_B2 ** ADAM_STEP)
        d_ref[...] = -ADAM_LR * (m_hat / (jnp.sqrt(v_hat) + ADAM_EPS) + ADAM_WD * w_ref[...])
        nm_ref[...] = m2
        nv_ref[...] = v2

    full = pl.BlockSpec((t, cols), lambda h, i, p: (h * per + i, 0))
    half = pl.BlockSpec((t, cols), lambda h, i, p: (i, 0))
    shp = jax.ShapeDtypeStruct(w.shape, F32)
    return pl.pallas_call(
        body, name=name, out_shape=[shp] * 4,
        grid_spec=pltpu.PrefetchScalarGridSpec(
            num_scalar_prefetch=1, grid=(2, per), in_specs=[full, half, half, full, full], out_specs=[full] * 4),
        compiler_params=_params())(where, w, own, theirs, m, v)


def _place_shard(where, w, cut, name):
    if cut.kind == "cols":
        r, n = cut.full_shape
        blk, grid = (256, n // 4), (r // 256,)
        src_map, dst_map = (lambda i, p: (i, 0)), (lambda i, p: (i, p[0]))
    elif cut.kind == "rows":
        r, n = cut.full_shape
        per = r // 4 // 256
        blk, grid = (256, n), (per,)
        src_map, dst_map = (lambda i, p: (i, 0)), (lambda i, p: (p[0] * per + i, 0))
    else:
        g, r, n = cut.full_shape
        blk, grid = (g, r // 4, n), (1,)
        src_map, dst_map = (lambda i, p: (0, 0, 0)), (lambda i, p: (0, p[0], 0))

    def body(where_ref, w_ref, o_ref):
        o_ref[...] = w_ref[...].astype(BF16)

    return pl.pallas_call(
        body, name=name, out_shape=jax.ShapeDtypeStruct(cut.full_shape, BF16),
        grid_spec=pltpu.PrefetchScalarGridSpec(
            num_scalar_prefetch=1, grid=grid, in_specs=[pl.BlockSpec(blk, src_map)],
            out_specs=pl.BlockSpec(blk, dst_map)),
        compiler_params=_params())(where, w)


def _sum_slots(parts, name):
    n, rows, cols = parts.shape
    t = _row_tile(rows, cols)

    def body(p_ref, o_ref):
        acc = p_ref[0]
        for k in range(1, n):
            acc = acc + p_ref[k]
        o_ref[...] = acc

    return pl.pallas_call(
        body, name=name, grid=(rows // t,),
        in_specs=[pl.BlockSpec((n, t, cols), lambda i: (0, i, 0))],
        out_specs=pl.BlockSpec((t, cols), lambda i: (i, 0)),
        out_shape=jax.ShapeDtypeStruct((rows, cols), F32), compiler_params=_params())(parts)


def _adamw(w, g, m, v, name):
    rows, cols = w.shape
    t = _row_tile(rows, cols)

    def body(w_ref, g_ref, m_ref, v_ref, d_ref, nm_ref, nv_ref):
        g = g_ref[...]
        m2 = ADAM_B1 * m_ref[...] + (1.0 - ADAM_B1) * g
        v2 = ADAM_B2 * v_ref[...] + (1.0 - ADAM_B2) * (g * g)
        m_hat = m2 / (1.0 - ADAM_B1 ** ADAM_STEP)
        v_hat = v2 / (1.0 - ADAM_B2 ** ADAM_STEP)
        d_ref[...] = -ADAM_LR * (m_hat / (jnp.sqrt(v_hat) + ADAM_EPS) + ADAM_WD * w_ref[...])
        nm_ref[...] = m2
        nv_ref[...] = v2

    spec = pl.BlockSpec((t, cols), lambda i: (i, 0))
    shp = jax.ShapeDtypeStruct(w.shape, F32)
    return pl.pallas_call(body, name=name, grid=(rows // t,), in_specs=[spec] * 4, out_specs=[spec] * 3,
                          out_shape=[shp] * 3, compiler_params=_params())(w, g, m, v)


def _place():
    x, y, c = lax.axis_index("x"), lax.axis_index("y"), lax.axis_index("c")
    chips = [(1 - x, y), (x, 1 - y), (1 - x, 1 - y)]
    return x, y, c, chips


class _Sharded:
    def __init__(self, kind, full_shape):
        self.kind = kind
        self.full_shape = full_shape
        self.chunk_axis = 1 if kind == "mid" else 0

    def in_full(self, ref, s, h):
        if self.kind == "cols":
            r, n = self.full_shape
            return ref.at[pl.ds(h * (r // 2), r // 2), pl.ds(pl.multiple_of(s * (n // 4), 128), n // 4)]
        if self.kind == "rows":
            r, _ = self.full_shape
            return ref.at[pl.ds(pl.multiple_of(s * (r // 4) + h * (r // 8), 8), r // 8), :]
        g, r, _ = self.full_shape
        return ref.at[pl.ds(h * (g // 2), g // 2), pl.ds(pl.multiple_of(s * (r // 4), 16), r // 4), :]


CHUNK_BYTES = 256 * 1024
MAX_CHUNKS = 32


def _cut(ref, axis, n):
    step = ref.shape[axis] // n
    lead = (slice(None),) * axis
    return [ref.at[lead + (pl.ds(k * step, step),)] for k in range(n)]


def _n_chunks(ref, axis):
    nbytes = int(np.prod(ref.shape)) * jnp.dtype(ref.dtype).itemsize
    n = 1
    while n < MAX_CHUNKS and nbytes // (2 * n) >= CHUNK_BYTES and ref.shape[axis] % (32 * n) == 0:
        n *= 2
    return n


def _remote(src, dst, send_sem, recv_sem, to):
    return pltpu.make_async_remote_copy(src_ref=src, dst_ref=dst, send_sem=send_sem, recv_sem=recv_sem,
                                        device_id=to, device_id_type=MESH)


def _start_remote(src, dst, axis, send_sem, recv_sem, to):
    n = _n_chunks(dst, axis)
    for s_k, d_k in zip(_cut(src, axis, n), _cut(dst, axis, n)):
        _remote(s_k, d_k, send_sem, recv_sem, to).start()
    return _remote(src, dst, send_sem, recv_sem, to)


def _start_local(src, dst, axis, sem):
    n = _n_chunks(dst, axis)
    for s_k, d_k in zip(_cut(src, axis, n), _cut(dst, axis, n)):
        pltpu.make_async_copy(s_k, d_k, sem).start()
    return pltpu.make_async_copy(src, dst, sem)


def _gather_weights(shards, cuts):
    n = len(shards)

    def body(*refs):
        src = refs[:n]
        out = refs[n:2 * n]
        send_sems, recv_sems = refs[2 * n:]
        x, y, c, chips = _place()
        me = 2 * x + y
        sibling = (x, y, 1 - c)

        def ends(w, s, h, from_shard):
            dst = cuts[w].in_full(out[w], s, h)
            return (cuts[w].in_full(src[w], s, h) if from_shard else dst), dst

        def remote(k, w, s, h, to, from_shard):
            return _remote(*ends(w, s, h, from_shard), send_sems.at[k], recv_sems.at[k], to)

        def start(k, w, s, h, to, from_shard):
            return _start_remote(*ends(w, s, h, from_shard), cuts[w].chunk_axis, send_sems.at[k], recv_sems.at[k], to)

        first = []
        for w in range(n):
            for j, chip in enumerate(chips):
                first.append(start(3 * w + j, w, me, c, (*chip, c), True))
        passed = []
        for w in range(n):
            for j, chip in enumerate(chips):
                s = 2 * chip[0] + chip[1]
                remote(3 * w + j, w, s, c, (x, y, c), False).wait_recv()
                passed.append(start(3 * n + 3 * w + j, w, s, c, sibling, False))
        for w in range(n):
            for j, chip in enumerate(chips):
                s = 2 * chip[0] + chip[1]
                remote(3 * n + 3 * w + j, w, s, 1 - c, (x, y, c), False).wait_recv()
        for cp in first + passed:
            cp.wait_send()

    return pl.pallas_call(
        body, name="gather_weights",
        in_specs=[ANY] * n, out_specs=[ANY] * n,
        out_shape=[jax.ShapeDtypeStruct(cuts[w].full_shape, BF16) for w in range(n)],
        input_output_aliases={w: w for w in range(n)},
        scratch_shapes=[pltpu.SemaphoreType.DMA((6 * n,)), pltpu.SemaphoreType.DMA((6 * n,))],
        compiler_params=pltpu.CompilerParams(has_side_effects=True),
    )(*shards)


def _exchange_halves(grads, small):
    n = len(grads)
    flips = [(fx, fy, fc) for fx in range(2) for fy in range(2) for fc in range(2)][1:]

    def body(*refs):
        g = refs[:n]
        small_ref = refs[n]
        theirs = refs[n + 1:2 * n + 1]
        gathered = refs[2 * n + 1]
        send_sems, recv_sems, local_sem = refs[2 * n + 2:]
        x, y, c, _ = _place()
        me = 4 * x + 2 * y + c
        local = [_start_local(small_ref, gathered.at[me], 0, local_sem)]
        sends = []
        for w in range(n):
            sends.append(_start_remote(g[w].at[:, 1 - c], theirs[w], 1, send_sems.at[w], recv_sems.at[w],
                                       (x, y, 1 - c)))
        for k, (fx, fy, fc) in enumerate(flips):
            sends.append(_start_remote(
                small_ref, gathered.at[me], 0, send_sems.at[n + k], recv_sems.at[n + k],
                (x + fx - 2 * x * fx, y + fy - 2 * y * fy, c + fc - 2 * c * fc)))
        for w in range(n):
            _remote(g[w].at[:, 1 - c], theirs[w], send_sems.at[w], recv_sems.at[w], (x, y, c)).wait_recv()
        for k, (fx, fy, fc) in enumerate(flips):
            peer = 4 * (x + fx - 2 * x * fx) + 2 * (y + fy - 2 * y * fy) + (c + fc - 2 * c * fc)
            _remote(small_ref, gathered.at[peer], send_sems.at[n + k], recv_sems.at[n + k], (x, y, c)).wait_recv()
        for cp in sends:
            cp.wait_send()
        for cp in local:
            cp.wait()

    halves = [jax.ShapeDtypeStruct((4,) + g.shape[2:], F32) for g in grads]
    return pl.pallas_call(
        body, name="exchange_halves",
        in_specs=[ANY] * (n + 1), out_specs=[ANY] * (n + 1),
        out_shape=halves + [jax.ShapeDtypeStruct((8,) + small.shape, F32)],
        scratch_shapes=[pltpu.SemaphoreType.DMA((n + 7,)), pltpu.SemaphoreType.DMA((n + 7,)),
                        pltpu.SemaphoreType.DMA],
        compiler_params=pltpu.CompilerParams(has_side_effects=True),
    )(*grads, small)


def _scatter_to_chips(parts):
    n = len(parts)

    def body(*refs):
        p = refs[:n]
        out = refs[n:2 * n]
        send_sems, recv_sems = refs[2 * n:]
        x, y, c, chips = _place()
        me = 2 * x + y
        sends = []
        for w in range(n):
            for j, chip in enumerate(chips):
                s = 2 * chip[0] + chip[1]
                sends.append(_start_remote(p[w].at[s], out[w].at[me], 0, send_sems.at[3 * w + j],
                                           recv_sems.at[3 * w + j], (*chip, c)))
        for w in range(n):
            for j, chip in enumerate(chips):
                s = 2 * chip[0] + chip[1]
                _remote(p[w].at[me], out[w].at[s], send_sems.at[3 * w + j], recv_sems.at[3 * w + j],
                        (x, y, c)).wait_recv()
        for cp in sends:
            cp.wait_send()

    return pl.pallas_call(
        body, name="scatter_to_chips",
        in_specs=[ANY] * n, out_specs=[ANY] * n,
        out_shape=[jax.ShapeDtypeStruct(a.shape, a.dtype) for a in parts],
        scratch_shapes=[pltpu.SemaphoreType.DMA((3 * n,)), pltpu.SemaphoreType.DMA((3 * n,))],
        compiler_params=pltpu.CompilerParams(has_side_effects=True),
    )(*parts)


def _share_halves(halves):
    n = len(halves)

    def body(*refs):
        hv = refs[:n]
        out = refs[n:2 * n]
        send_sems, recv_sems = refs[2 * n:]
        x, y, c, _ = _place()
        sends = [_start_remote(hv[w], out[w], 0, send_sems.at[w], recv_sems.at[w], (x, y, 1 - c)) for w in range(n)]
        for w in range(n):
            _remote(hv[w], out[w], send_sems.at[w], recv_sems.at[w], (x, y, c)).wait_recv()
        for cp in sends:
            cp.wait_send()

    return pl.pallas_call(
        body, name="share_halves",
        in_specs=[ANY] * n, out_specs=[ANY] * n,
        out_shape=[jax.ShapeDtypeStruct(a.shape, F32) for a in halves],
        scratch_shapes=[pltpu.SemaphoreType.DMA((n,)), pltpu.SemaphoreType.DMA((n,))],
        compiler_params=pltpu.CompilerParams(has_side_effects=True),
    )(*halves)


SMALL_ROWS = 80


def _pack_small(vecs, ws, bs, sink):
    top = jnp.concatenate(
        [v.reshape(1, 1024) for v in vecs]
        + [jnp.pad(bs.reshape(1, 512), ((0, 0), (0, 512))), jnp.pad(sink.reshape(1, 16), ((0, 0), (0, 1008))),
           jnp.zeros((1, 1024), F32)], axis=0)
    return jnp.concatenate([top, ws.reshape(64, 1024), jnp.zeros((8, 1024), F32)], axis=0)


def _unpack_small(p):
    vecs = [p[k] for k in range(5)]
    return vecs, p[8:72].reshape(4, 128, 128), p[5, :512].reshape(4, 128), p[6, :16]


def kernel(x, norm_0, w_in_0, a_v_norm_0, a_spatial_w_0, a_spatial_b_0, b_group_w_0, b_scale_0, w_out_0, norm_1, w_in_1, sink_1, w_out_1, final_norm, loss_target, m_norm_0, m_w_in_0, m_a_v_norm_0, m_a_spatial_w_0, m_a_spatial_b_0, m_b_group_w_0, m_b_scale_0, m_w_out_0, m_norm_1, m_w_in_1, m_sink_1, m_w_out_1, m_final_norm, v_norm_0, v_w_in_0, v_a_v_norm_0, v_a_spatial_w_0, v_a_spatial_b_0, v_b_group_w_0, v_b_scale_0, v_w_out_0, v_norm_1, v_w_in_1, v_sink_1, v_w_out_1, v_final_norm):
    s = x.shape[1]
    xs = x.reshape(s, D_MODEL)
    target = loss_target.reshape(s, D_MODEL)

    cuts = [_Sharded("cols", (1024, 5120)), _Sharded("rows", (2048, 1024)), _Sharded("cols", (1024, 2560)),
            _Sharded("rows", (1024, 1024)), _Sharded("mid", (4, 256, 256))]
    big_w = [w_in_0, w_out_0, w_in_1, w_out_1, b_group_w_0]
    big_m = [m_w_in_0, m_w_out_0, m_w_in_1, m_w_out_1, m_b_group_w_0]
    big_v = [v_w_in_0, v_w_out_0, v_w_in_1, v_w_out_1, v_b_group_w_0]
    where = jnp.stack([2 * lax.axis_index("x") + lax.axis_index("y"), lax.axis_index("c")]).astype(jnp.int32)
    w_in0, w_out0, w_in1, w_out1, wg = _gather_weights(
        [_place_shard(where, w, cut, f"place_shard{k}") for k, (w, cut) in enumerate(zip(big_w, cuts))], cuts)

    row = lambda v: v.reshape(1, 1024)
    ws16 = a_spatial_w_0.astype(BF16)
    bsx = jnp.repeat(a_spatial_b_0.T, 256, axis=1)
    band, band_t = _band_matrices(TOK)
    rope = _rope_tables(s)

    h0, z0 = _in_proj0(xs, row(norm_0), w_in0)
    cat, x1 = _mix0_fwd(xs, z0, w_out0, row(a_v_norm_0), ws16, bsx, wg, row(b_scale_0), band)
    h1, q, k, v, gate = _in_proj1(x1, row(norm_1), w_in1, rope)
    kpad = jnp.pad(k, ((ATTN_WINDOW, ATTN_WINDOW), (0, 0)))
    vpad = jnp.pad(v, ((ATTN_WINDOW, ATTN_WINDOW), (0, 0)))
    o = _attn_fwd(q, kpad, vpad, sink_1)
    y1, dx2, do, dgate, loss_lanes, g_final, dx2h = _tail(x1, o, gate, target, w_out1, row(final_norm))

    dq, dkpad, dvpad, dsink = _attn_bwd(q, kpad, vpad, sink_1, o, do, rope)
    dk = dkpad[ATTN_WINDOW:ATTN_WINDOW + s]
    dv = dvpad[ATTN_WINDOW:ATTN_WINDOW + s]
    dz1, dx1, g_norm1, dx1h = _in_proj1_bwd(dq, dk, dv, dgate, x1, dx2, row(norm_1), w_in1, rope)
    dza, dzg, dpn, d_ws, d_bs, d_gv, d_scale, d_wg = _mix0_bwd(
        dx1h, z0, w_out0, row(a_v_norm_0), ws16, bsx, wg, row(b_scale_0), band)
    dz0, grad_x, g_norm0 = _in_proj0_bwd(dza, dzg, dpn, xs, dx1, row(norm_0), w_in0, band_t)

    g_w_in0 = _weight_grad(h0, dz0, 4, "grad_w_in0")
    g_w_out0 = _weight_grad(cat, dx1h, 1, "grad_w_out0")
    g_w_in1 = _weight_grad(h1, dz1, 4, "grad_w_in1")
    g_w_out1 = _weight_grad(y1, dx2h, 1, "grad_w_out1")

    grads = [g_w_in0.reshape(4, 2, 512, 1280), g_w_out0.reshape(4, 2, 256, 1024), g_w_in1.reshape(4, 2, 512, 640),
             g_w_out1.reshape(4, 2, 128, 1024),
             d_wg.reshape(2, 2, 4, 64, 256).transpose(2, 0, 1, 3, 4).reshape(4, 2, 128, 256)]
    small = _pack_small([g_norm0, d_gv, d_scale, g_norm1, g_final], d_ws, d_bs[:, :4].T, dsink[0, :16])
    *theirs, small_all = _exchange_halves(grads, small)
    n = len(grads)
    parts = [_add_sibling(where, grads[w], theirs[w], f"add_sibling{w}") for w in range(n)]
    slots = _scatter_to_chips(parts)
    reduced = [_sum_chips(where, grads[w], theirs[w], slots[w], f"sum_chips{w}") for w in range(n)]
    from_sibling = _share_halves(reduced)

    out_g, out_d, out_m, out_v = {}, {}, {}, {}
    names = ["w_in_0", "w_out_0", "w_in_1", "w_out_1", "b_group_w_0"]
    for w in range(n):
        shape = big_w[w].shape
        two_d = (-1, shape[-1])
        outs = _adamw_halves(where, big_w[w].reshape(two_d), reduced[w], from_sibling[w], big_m[w].reshape(two_d),
                             big_v[w].reshape(two_d), f"adamw{w}")
        out_g[names[w]], out_d[names[w]], out_m[names[w]], out_v[names[w]] = (a.reshape(shape) for a in outs)

    g_small = _sum_slots(small_all, "sum_small")
    small_names = ["norm_0", "a_v_norm_0", "b_scale_0", "norm_1", "final_norm"]
    pack = lambda vecs, ws_, bs_, sk: _pack_small(vecs, ws_, bs_, sk)
    w_small = pack([norm_0, a_v_norm_0, b_scale_0, norm_1, final_norm], a_spatial_w_0, a_spatial_b_0, sink_1)
    m_small = pack([m_norm_0, m_a_v_norm_0, m_b_scale_0, m_norm_1, m_final_norm], m_a_spatial_w_0,
                   m_a_spatial_b_0, m_sink_1)
    v_small = pack([v_norm_0, v_a_v_norm_0, v_b_scale_0, v_norm_1, v_final_norm], v_a_spatial_w_0,
                   v_a_spatial_b_0, v_sink_1)
    d_small, nm_small, nv_small = _adamw(w_small, g_small, m_small, v_small, "adamw_small")
    for store, packed in ((out_g, g_small), (out_d, d_small), (out_m, nm_small), (out_v, nv_small)):
        vecs, ws_, bs_, sk = _unpack_small(packed)
        for name, vec in zip(small_names, vecs):
            store[name] = vec
        store["a_spatial_w_0"], store["a_spatial_b_0"], store["sink_1"] = ws_, bs_, sk

    loss = lax.psum(jnp.sum(loss_lanes), ("x", "y", "c"))
    order = ["norm_0", "w_in_0", "a_v_norm_0", "a_spatial_w_0", "a_spatial_b_0", "b_group_w_0", "b_scale_0",
             "w_out_0", "norm_1", "w_in_1", "sink_1", "w_out_1", "final_norm"]
    return (loss, grad_x.reshape(1, s, D_MODEL), *[out_g[k] for k in order], *[out_d[k] for k in order],
            *[out_m[k] for k in order], *[out_v[k] for k in order])
```

```python
import functools

import numpy as np
import jax
import jax.numpy as jnp
from jax import lax
from jax.experimental import pallas as pl
from jax.experimental.pallas import tpu as pltpu

F32 = jnp.float32
BF16 = jnp.bfloat16
MESH = pl.DeviceIdType.MESH

D_MODEL = 1024
EPS = 1e-6
NEG_INF = -1e30
CHUNK = 128
POOL_WINDOWS = (2, 4, 8, 16)
HALO = 16
N_HEADS = 16
HEAD_DIM = 64
ATTN_WINDOW = 128
ROPE_THETA = 500000.0
ROT_DIM = 16
ADAM_LR = 0.001
ADAM_B1 = 0.9
ADAM_B2 = 0.999
ADAM_EPS = 1e-08
ADAM_WD = 0.01
ADAM_STEP = 10

TOK = 256
VMEM_LIMIT = 56 * 1024 * 1024


def _params(**kw):
    return pltpu.CompilerParams(vmem_limit_bytes=VMEM_LIMIT, **kw)


def _whole(shape):
    nd = len(shape)
    return pl.BlockSpec(shape, lambda *_: (0,) * nd)


def _rows(t, n):
    return pl.BlockSpec((t, n), lambda i: (i, 0))


ANY = pl.BlockSpec(memory_space=pl.ANY)

_G0 = 0.7978845608028654
_G1 = 0.044715


def _gelu(x):
    return 0.5 * x * (1.0 + jnp.tanh(_G0 * (x + _G1 * x * x * x)))


def _dgelu(x):
    t = jnp.tanh(_G0 * (x + _G1 * x * x * x))
    return 0.5 * (1.0 + t) + 0.5 * x * (1.0 - t * t) * (_G0 * (1.0 + 3.0 * _G1 * x * x))


def _sigmoid(x):
    return 1.0 / (1.0 + jnp.exp(-x))


def _dot(a, b):
    return jnp.dot(a, b, preferred_element_type=F32)


def _dot_nt(a, b):
    return lax.dot_general(a, b, (((1,), (1,)), ((), ())), preferred_element_type=F32)


def _dot_tn(a, b):
    return lax.dot_general(a, b, (((0,), (0,)), ((), ())), preferred_element_type=F32)


def _rms_fwd(x, g):
    r = lax.rsqrt(jnp.mean(x * x, axis=-1, keepdims=True) + EPS)
    xh = x * r
    return r, xh, xh * g


def _rms_bwd(dy, g, r, xh):
    dxh = dy * g
    return r * (dxh - xh * jnp.mean(dxh * xh, axis=-1, keepdims=True))


def _band_matrices(t):
    r = np.arange(t)[:, None]
    j = np.arange(t + 2 * HALO)[None, :]
    fwd, bwd = [], []
    for w in POOL_WINDOWS:
        d = j - r - HALO
        fwd.append((d >= -(w // 2)) & (d < w // 2))
        bwd.append((d >= -(w // 2) + 1) & (d <= w // 2))
    return (jnp.asarray(np.stack(fwd), BF16), jnp.asarray(np.stack(bwd), BF16))


def _window_counts(i, t, s):
    tok = i * t + lax.broadcasted_iota(jnp.int32, (t, 1), 0)
    out = []
    for w in POOL_WINDOWS:
        cnt = jnp.minimum(tok + w // 2, s) - jnp.maximum(tok - w // 2, 0)
        out.append(cnt.astype(F32))
    return out


def _rope_tables(s):
    inv = np.float32(ROPE_THETA) ** (-np.arange(0, ROT_DIM, 2, dtype=np.float32) / np.float32(ROT_DIM))
    ang = np.arange(s, dtype=np.float32)[:, None] * inv.astype(np.float32)[None, :]
    cos, sin = np.cos(ang).astype(np.float32), np.sin(ang).astype(np.float32)
    z8 = np.zeros((s, 8), np.float32)
    z48 = np.zeros((s, HEAD_DIM - ROT_DIM), np.float32)
    c = np.concatenate([cos, cos, np.ones((s, HEAD_DIM - ROT_DIM), np.float32)], axis=1)
    s_lo = np.concatenate([z8, sin, z48], axis=1)
    s_hi = np.concatenate([-sin, z8, z48], axis=1)
    return tuple(jnp.asarray(np.concatenate([a, a], axis=1)) for a in (c, s_lo, s_hi))


def _rope(x, c, s_lo, s_hi):
    n = x.shape[1]
    reps = n // 128
    c, s_lo, s_hi = (jnp.tile(a, (1, reps)) for a in (c, s_lo, s_hi))
    return x * c + pltpu.roll(x, 8, 1) * s_lo + pltpu.roll(x, n - 8, 1) * s_hi


def _rope_t(dx, c, s_lo, s_hi):
    n = dx.shape[1]
    reps = n // 128
    c, s_lo, s_hi = (jnp.tile(a, (1, reps)) for a in (c, s_lo, s_hi))
    return dx * c + pltpu.roll(dx * s_lo, n - 8, 1) + pltpu.roll(dx * s_hi, 8, 1)


def _in_proj0(x, g0, w_in0):
    s = x.shape[0]
    n = w_in0.shape[1]

    def body(x_ref, g_ref, w_ref, h_ref, z_ref):
        _, _, h = _rms_fwd(x_ref[...], g_ref[...])
        h = h.astype(BF16)
        h_ref[...] = h
        for j in range(n // 1024):
            z_ref[:, j * 1024:(j + 1) * 1024] = _dot(h, w_ref[:, j * 1024:(j + 1) * 1024]).astype(BF16)

    return pl.pallas_call(
        body, name="in_proj0", grid=(s // TOK,),
        in_specs=[_rows(TOK, D_MODEL), _whole((1, D_MODEL)), _whole(w_in0.shape)],
        out_specs=[_rows(TOK, D_MODEL), _rows(TOK, n)],
        out_shape=[jax.ShapeDtypeStruct((s, D_MODEL), BF16), jax.ShapeDtypeStruct((s, n), BF16)],
        compiler_params=_params(),
    )(x, g0, w_in0)


def _halo_specs(s, col_block):
    per = TOK // HALO
    last = s // HALO - 1
    prev = pl.BlockSpec((HALO, 1024), lambda i: (jnp.maximum(i * per - 1, 0), col_block))
    nxt = pl.BlockSpec((HALO, 1024), lambda i: (jnp.minimum((i + 1) * per, last), col_block))
    return prev, nxt


def _with_halo(i, n_tiles, prev_ref, cur, next_ref):
    prev = prev_ref[...]
    nxt = next_ref[...]
    prev = jnp.where(i > 0, prev, jnp.zeros_like(prev))
    nxt = jnp.where(i < n_tiles - 1, nxt, jnp.zeros_like(nxt))
    return jnp.concatenate([prev, cur, nxt], axis=0)


def _mixer_a(au, av, gv, ws_ref, bsx):
    u = _gelu(au)
    v1 = _gelu(av)
    rv, vh, v2 = _rms_fwd(v1, gv)
    v2 = v2.astype(BF16)
    rows = []
    for c in range(au.shape[0] // CHUNK):
        cols = [_dot(ws_ref[h], v2[c * CHUNK:(c + 1) * CHUNK, h * 256:(h + 1) * 256]) for h in range(4)]
        rows.append(jnp.concatenate(cols, axis=1) + bsx)
    return u, rv, vh, v2, jnp.concatenate(rows, axis=0)


def _mixer_b_pooled(bx, halo, band_ref, counts):
    out = []
    for g in range(4):
        win = _dot(band_ref[g], halo[:, g * 256:(g + 1) * 256])
        out.append(win / counts[g] - bx[:, g * 256:(g + 1) * 256])
    return out


def _mix0_fwd(x, z0, w_out0, gv, ws, bsx, wg, scale, band):
    s = x.shape[0]
    n_tiles = s // TOK

    def body(z_ref, zp_ref, zn_ref, x_ref, wout_ref, gv_ref, ws_ref, bsx_ref, wg_ref, sc_ref, band_ref,
             cat_ref, x1_ref):
        i = pl.program_id(0)
        au = z_ref[:, 0:1024].astype(F32)
        av = z_ref[:, 1024:2048].astype(F32)
        ag = z_ref[:, 2048:3072].astype(F32)
        u, _, _, _, mixed = _mixer_a(au, av, gv_ref[...], ws_ref, bsx_ref[...])
        cat_ref[:, 0:1024] = (u * mixed * (ag * _sigmoid(ag))).astype(BF16)

        bx16 = z_ref[:, 3072:4096]
        bg = z_ref[:, 4096:5120].astype(F32)
        halo = _with_halo(i, n_tiles, zp_ref, bx16, zn_ref)
        ps = _mixer_b_pooled(bx16.astype(F32), halo, band_ref, _window_counts(i, TOK, s))
        pw = jnp.concatenate([_dot(ps[g].astype(BF16), wg_ref[g]) for g in range(4)], axis=1)
        cat_ref[:, 1024:2048] = (pw * sc_ref[...] * (bg * _sigmoid(bg))).astype(BF16)

        x1_ref[...] = x_ref[...] + _dot(cat_ref[...], wout_ref[...])

    prev, nxt = _halo_specs(s, 3)
    return pl.pallas_call(
        body, name="mix0_fwd", grid=(n_tiles,),
        in_specs=[_rows(TOK, 5120), prev, nxt, _rows(TOK, D_MODEL), _whole(w_out0.shape), _whole((1, 1024)),
                  _whole(ws.shape), _whole(bsx.shape), _whole(wg.shape), _whole((1, 1024)), _whole(band.shape)],
        out_specs=[_rows(TOK, 2048), _rows(TOK, D_MODEL)],
        out_shape=[jax.ShapeDtypeStruct((s, 2048), BF16), jax.ShapeDtypeStruct((s, D_MODEL), F32)],
        compiler_params=_params(),
    )(z0, z0, z0, x, w_out0, gv, ws, bsx, wg, scale, band)


def _in_proj1(x1, g1, w_in1, rope):
    s = x1.shape[0]

    def body(x_ref, g_ref, w_ref, c_ref, lo_ref, hi_ref, h_ref, q_ref, k_ref, v_ref, gate_ref):
        _, _, h = _rms_fwd(x_ref[...], g_ref[...])
        h = h.astype(BF16)
        h_ref[...] = h
        tabs = (c_ref[...], lo_ref[...], hi_ref[...])
        q_ref[...] = (_rope(_dot(h, w_ref[:, 0:1024]), *tabs) * Q_SCALE).astype(BF16)
        kv = _dot(h, w_ref[:, 1024:1536])
        k_ref[...] = _rope(kv[:, 0:256], *tabs).astype(BF16)
        v_ref[...] = kv[:, 256:512].astype(BF16)
        gate_ref[...] = _dot(h, w_ref[:, 1536:2560]).astype(BF16)

    tab = _rows(TOK, 128)
    return pl.pallas_call(
        body, name="in_proj1", grid=(s // TOK,),
        in_specs=[_rows(TOK, D_MODEL), _whole((1, D_MODEL)), _whole(w_in1.shape), tab, tab, tab],
        out_specs=[_rows(TOK, 1024), _rows(TOK, 1024), _rows(TOK, 256), _rows(TOK, 256), _rows(TOK, 1024)],
        out_shape=[jax.ShapeDtypeStruct((s, 1024), BF16), jax.ShapeDtypeStruct((s, 1024), BF16),
                   jax.ShapeDtypeStruct((s, 256), BF16), jax.ShapeDtypeStruct((s, 256), BF16),
                   jax.ShapeDtypeStruct((s, 1024), BF16)],
        compiler_params=_params(),
    )(x1, g1, w_in1, *rope)


QBLK = 128
KBLK = QBLK + 2 * ATTN_WINDOW
Q_SCALE = HEAD_DIM ** -0.5


def _block_bias(q0, s):
    r = lax.broadcasted_iota(jnp.int32, (QBLK, KBLK), 0)
    c = lax.broadcasted_iota(jnp.int32, (QBLK, KBLK), 1)
    kj = q0 - ATTN_WINDOW + c
    ok = (c >= r) & (c <= r + 2 * ATTN_WINDOW) & (kj >= 0) & (kj < s)
    return jnp.where(ok, 0.0, NEG_INF)


def _attn_fwd(q, kpad, vpad, sink):
    s = q.shape[0]

    def body(sink_ref, q_ref, k_ref, v_ref, o_ref):
        i = pl.program_id(0)
        for b in range(TOK // QBLK):
            rows = slice(b * QBLK, (b + 1) * QBLK)
            start = pl.multiple_of(i * TOK + b * QBLK, QBLK)
            kb = k_ref[pl.ds(start, KBLK), :]
            vb = v_ref[pl.ds(start, KBLK), :]
            bias = _block_bias(i * TOK + b * QBLK, s)
            kv = lambda t, h: t[:, (h // 4) * HEAD_DIM:(h // 4 + 1) * HEAD_DIM]
            scs = [_dot_nt(q_ref[rows, h * HEAD_DIM:(h + 1) * HEAD_DIM], kv(kb, h)) + bias for h in range(N_HEADS)]
            ms = [jnp.maximum(jnp.max(scs[h], axis=-1, keepdims=True), sink_ref[h]) for h in range(N_HEADS)]
            es = [jnp.exp(scs[h] - ms[h]) for h in range(N_HEADS)]
            rdens = [1.0 / (jnp.sum(es[h], axis=-1, keepdims=True) + jnp.exp(sink_ref[h] - ms[h]))
                     for h in range(N_HEADS)]
            outs = [_dot(es[h].astype(BF16), kv(vb, h)) * rdens[h] for h in range(N_HEADS)]
            o_ref[rows, :] = jnp.concatenate(outs, axis=1).astype(BF16)

    return pl.pallas_call(
        body, name="attn_fwd", grid=(s // TOK,),
        in_specs=[pl.BlockSpec(memory_space=pltpu.SMEM), _rows(TOK, 1024), _whole(kpad.shape), _whole(vpad.shape)],
        out_specs=_rows(TOK, 1024),
        out_shape=jax.ShapeDtypeStruct((s, 1024), BF16),
        compiler_params=_params(),
    )(sink, q, kpad, vpad)


def _tail(x1, o, gate, target, w_out1, gf):
    s = x1.shape[0]

    def body(x1_ref, o_ref, gate_ref, t_ref, w_ref, gf_ref, y1_ref, dx2_ref, do_ref, dgate_ref, loss_ref, gfn_ref,
             dx2h_ref):
        i = pl.program_id(0)

        @pl.when(i == 0)
        def _():
            loss_ref[...] = jnp.zeros_like(loss_ref)
            gfn_ref[...] = jnp.zeros_like(gfn_ref)

        g = gate_ref[...].astype(F32)
        sg = _sigmoid(g)
        sil = g * sg
        o = o_ref[...].astype(F32)
        y1 = (o * sil).astype(BF16)
        y1_ref[...] = y1
        x2 = x1_ref[...] + _dot(y1, w_ref[...])
        gf = gf_ref[...]
        r, xh, out = _rms_fwd(x2, gf)
        diff = out - t_ref[...]
        loss_ref[...] += jnp.sum(diff * diff, axis=0, keepdims=True) * (0.5 / D_MODEL)
        dout = diff * (1.0 / D_MODEL)
        gfn_ref[...] += jnp.sum(dout * xh, axis=0, keepdims=True)
        dx2 = _rms_bwd(dout, gf, r, xh)
        dx2_ref[...] = dx2
        dx2h = dx2.astype(BF16)
        dx2h_ref[...] = dx2h
        dy1 = _dot_nt(dx2h, w_ref[...])
        do_ref[...] = (dy1 * sil).astype(BF16)
        dgate_ref[...] = (dy1 * o * (sg * (1.0 + g * (1.0 - sg)))).astype(BF16)

    row = _rows(TOK, 1024)
    acc = _whole((1, 1024))
    return pl.pallas_call(
        body, name="tail", grid=(s // TOK,),
        in_specs=[row, row, row, row, _whole(w_out1.shape), acc],
        out_specs=[row, row, row, row, acc, acc, row],
        out_shape=[jax.ShapeDtypeStruct((s, 1024), BF16), jax.ShapeDtypeStruct((s, 1024), F32),
                   jax.ShapeDtypeStruct((s, 1024), BF16), jax.ShapeDtypeStruct((s, 1024), BF16),
                   jax.ShapeDtypeStruct((1, 1024), F32), jax.ShapeDtypeStruct((1, 1024), F32),
                   jax.ShapeDtypeStruct((s, 1024), BF16)],
        compiler_params=_params(),
    )(x1, o, gate, target, w_out1, gf)


def _attn_bwd(q, kpad, vpad, sink, o, do, rope):
    s = q.shape[0]

    def body(sink_ref, q_ref, k_ref, v_ref, o_ref, do_ref, c_ref, lo_ref, hi_ref, dq_ref, dk_ref, dv_ref, ds_ref):
        i = pl.program_id(0)

        @pl.when(i == 0)
        def _():
            dk_ref[...] = jnp.zeros_like(dk_ref)
            dv_ref[...] = jnp.zeros_like(dv_ref)
            ds_ref[...] = jnp.zeros_like(ds_ref)

        lane = lax.broadcasted_iota(jnp.int32, (1, 128), 1)
        dsink = jnp.zeros((1, 128), F32)
        for b in range(TOK // QBLK):
            rows = slice(b * QBLK, (b + 1) * QBLK)
            start = pl.multiple_of(i * TOK + b * QBLK, QBLK)
            kb = k_ref[pl.ds(start, KBLK), :]
            vb = v_ref[pl.ds(start, KBLK), :]
            bias = _block_bias(i * TOK + b * QBLK, s)
            heads = range(N_HEADS)
            kv = lambda t, h: t[:, (h // 4) * HEAD_DIM:(h // 4 + 1) * HEAD_DIM]
            hd = lambda ref, h: ref[rows, h * HEAD_DIM:(h + 1) * HEAD_DIM]
            scs = [_dot_nt(hd(q_ref, h), kv(kb, h)) + bias for h in heads]
            ms = [jnp.maximum(jnp.max(scs[h], axis=-1, keepdims=True), sink_ref[h]) for h in heads]
            es = [jnp.exp(scs[h] - ms[h]) for h in heads]
            e_sinks = [jnp.exp(sink_ref[h] - ms[h]) for h in heads]
            rdens = [1.0 / (jnp.sum(es[h], axis=-1, keepdims=True) + e_sinks[h]) for h in heads]
            dos = [hd(do_ref, h).astype(F32) for h in heads]
            deltas = [jnp.sum(dos[h] * hd(o_ref, h).astype(F32), axis=-1, keepdims=True) * rdens[h] for h in heads]
            do_ns = [(dos[h] * rdens[h]).astype(BF16) for h in heads]
            for h in heads:
                dsink = dsink + jnp.where(lane == h, -jnp.sum(e_sinks[h] * deltas[h], axis=0, keepdims=True), 0.0)
            dscs = [(es[h] * (_dot_nt(do_ns[h], kv(vb, h)) - deltas[h])).astype(BF16) for h in heads]
            dvs = [_dot_tn(es[h].astype(BF16), do_ns[h]) for h in heads]
            dks = [_dot_tn(dscs[h], hd(q_ref, h)) for h in heads]
            dq = jnp.concatenate([_dot(dscs[h], kv(kb, h)) * Q_SCALE for h in heads], axis=1)
            dq_ref[rows, :] = _rope_t(dq, c_ref[rows, :], lo_ref[rows, :], hi_ref[rows, :]).astype(BF16)
            group_sum = lambda parts, g: (parts[4 * g] + parts[4 * g + 1]) + (parts[4 * g + 2] + parts[4 * g + 3])
            dk_ref[pl.ds(start, KBLK), :] += jnp.concatenate([group_sum(dks, g) for g in range(4)], axis=1)
            dv_ref[pl.ds(start, KBLK), :] += jnp.concatenate([group_sum(dvs, g) for g in range(4)], axis=1)
        ds_ref[...] += dsink

    row = _rows(TOK, 1024)
    tab = _rows(TOK, 128)
    pad = _whole(kpad.shape)
    return pl.pallas_call(
        body, name="attn_bwd", grid=(s // TOK,),
        in_specs=[pl.BlockSpec(memory_space=pltpu.SMEM), row, pad, pad, row, row, tab, tab, tab],
        out_specs=[row, pad, pad, _whole((1, 128))],
        out_shape=[jax.ShapeDtypeStruct((s, 1024), BF16), jax.ShapeDtypeStruct(kpad.shape, F32),
                   jax.ShapeDtypeStruct(kpad.shape, F32), jax.ShapeDtypeStruct((1, 128), F32)],
        compiler_params=_params(),
    )(sink, q, kpad, vpad, o, do, *rope)


def _in_proj1_bwd(dq, dk, dv, dgate, x1, dx2, g1, w_in1, rope):
    s = x1.shape[0]

    def body(dq_ref, dk_ref, dv_ref, dgate_ref, x1_ref, dx2_ref, g_ref, w_ref, c_ref, lo_ref, hi_ref,
             dz_ref, dx1_ref, gn_ref, dx1h_ref):
        i = pl.program_id(0)

        @pl.when(i == 0)
        def _():
            gn_ref[...] = jnp.zeros_like(gn_ref)

        dz_ref[:, 0:1024] = dq_ref[...]
        dz_ref[:, 1024:1280] = _rope_t(dk_ref[...], c_ref[...], lo_ref[...], hi_ref[...]).astype(BF16)
        dz_ref[:, 1280:1536] = dv_ref[...].astype(BF16)
        dz_ref[:, 1536:2560] = dgate_ref[...]
        dh = _dot_nt(dz_ref[...], w_ref[...])
        g = g_ref[...]
        r, xh, _ = _rms_fwd(x1_ref[...], g)
        gn_ref[...] += jnp.sum(dh * xh, axis=0, keepdims=True)
        dx1 = dx2_ref[...] + _rms_bwd(dh, g, r, xh)
        dx1_ref[...] = dx1
        dx1h_ref[...] = dx1.astype(BF16)

    row = _rows(TOK, 1024)
    nar = _rows(TOK, 256)
    tab = _rows(TOK, 128)
    acc = _whole((1, 1024))
    return pl.pallas_call(
        body, name="in_proj1_bwd", grid=(s // TOK,),
        in_specs=[row, nar, nar, row, row, row, acc, _whole(w_in1.shape), tab, tab, tab],
        out_specs=[_rows(TOK, 2560), row, acc, row],
        out_shape=[jax.ShapeDtypeStruct((s, 2560), BF16), jax.ShapeDtypeStruct((s, 1024), F32),
                   jax.ShapeDtypeStruct((1, 1024), F32), jax.ShapeDtypeStruct((s, 1024), BF16)],
        compiler_params=_params(),
    )(dq, dk, dv, dgate, x1, dx2, g1, w_in1, *rope)


def _mix0_bwd(dx1, z0, w_out0, gv, ws, bsx, wg, scale, band):
    s = dx1.shape[0]
    n_tiles = s // TOK

    def body(dx1h_ref, z_ref, zp_ref, zn_ref, wout_ref, gv_ref, ws_ref, bsx_ref, wg_ref, sc_ref, band_ref,
             dza_ref, dzg_ref, dpn_ref, dws_ref, dbs_ref, dgv_ref, dsc_ref, dwg_ref):
        i = pl.program_id(0)

        @pl.when(i == 0)
        def _():
            for ref in (dws_ref, dbs_ref, dgv_ref, dsc_ref, dwg_ref):
                ref[...] = jnp.zeros_like(ref)

        dcat = _dot_nt(dx1h_ref[...], wout_ref[...])
        dya = dcat[:, 0:1024]
        dyb = dcat[:, 1024:2048]

        au = z_ref[:, 0:1024].astype(F32)
        av = z_ref[:, 1024:2048].astype(F32)
        ag = z_ref[:, 2048:3072].astype(F32)
        gv = gv_ref[...]
        u, rv, vh, v2, mixed = _mixer_a(au, av, gv, ws_ref, bsx_ref[...])
        sg = _sigmoid(ag)
        sil = ag * sg
        dza_ref[:, 2048:3072] = (dya * u * mixed * (sg * (1.0 + ag * (1.0 - sg)))).astype(BF16)
        dza_ref[:, 0:1024] = (dya * mixed * sil * _dgelu(au)).astype(BF16)
        dmixed = dya * u * sil
        lane = lax.broadcasted_iota(jnp.int32, (1, 128), 1)
        dm16 = dmixed.astype(BF16)
        dv2_rows = []
        for c in range(TOK // CHUNK):
            rows = slice(c * CHUNK, (c + 1) * CHUNK)
            cols_out = []
            for h in range(4):
                cols = slice(h * 256, (h + 1) * 256)
                dws_ref[h] += _dot_nt(dm16[rows, cols], v2[rows, cols])
                dbs_ref[...] += jnp.where(lane == h, jnp.sum(dmixed[rows, cols], axis=-1, keepdims=True), 0.0)
                cols_out.append(_dot_tn(ws_ref[h], dm16[rows, cols]))
            dv2_rows.append(jnp.concatenate(cols_out, axis=1))
        dv2 = jnp.concatenate(dv2_rows, axis=0)
        dgv_ref[...] += jnp.sum(dv2 * vh, axis=0, keepdims=True)
        dza_ref[:, 1024:2048] = (_rms_bwd(dv2, gv, rv, vh) * _dgelu(av)).astype(BF16)

        bx16 = z_ref[:, 3072:4096]
        bg = z_ref[:, 4096:5120].astype(F32)
        counts = _window_counts(i, TOK, s)
        halo = _with_halo(i, n_tiles, zp_ref, bx16, zn_ref)
        ps = [p.astype(BF16) for p in _mixer_b_pooled(bx16.astype(F32), halo, band_ref, counts)]
        pw = jnp.concatenate([_dot(ps[g], wg_ref[g]) for g in range(4)], axis=1)
        sgb = _sigmoid(bg)
        sc = sc_ref[...]
        dzg_ref[...] = (dyb * pw * sc * (sgb * (1.0 + bg * (1.0 - sgb)))).astype(BF16)
        dys = dyb * (bg * sgb)
        dsc_ref[...] += jnp.sum(dys * pw, axis=0, keepdims=True)
        dpw = (dys * sc).astype(BF16)
        for g in range(4):
            cols = slice(g * 256, (g + 1) * 256)
            dwg_ref[g] += _dot_tn(ps[g], dpw[:, cols])
            dpn_ref[:, cols] = (_dot_nt(dpw[:, cols], wg_ref[g]) / counts[g]).astype(BF16)

    prev, nxt = _halo_specs(s, 3)
    row = _rows(TOK, 1024)
    vec = _whole((1, 1024))
    return pl.pallas_call(
        body, name="mix0_bwd", grid=(n_tiles,),
        in_specs=[row, _rows(TOK, 5120), prev, nxt, _whole(w_out0.shape), vec, _whole(ws.shape),
                  _whole(bsx.shape), _whole(wg.shape), vec, _whole(band.shape)],
        out_specs=[_rows(TOK, 3072), row, row, _whole((4, 128, 128)), _whole((128, 128)), vec, vec,
                   _whole((4, 256, 256))],
        out_shape=[jax.ShapeDtypeStruct((s, 3072), BF16), jax.ShapeDtypeStruct((s, 1024), BF16),
                   jax.ShapeDtypeStruct((s, 1024), BF16), jax.ShapeDtypeStruct((4, 128, 128), F32),
                   jax.ShapeDtypeStruct((128, 128), F32), jax.ShapeDtypeStruct((1, 1024), F32),
                   jax.ShapeDtypeStruct((1, 1024), F32), jax.ShapeDtypeStruct((4, 256, 256), F32)],
        compiler_params=_params(),
    )(dx1, z0, z0, z0, w_out0, gv, ws, bsx, wg, scale, band)


def _in_proj0_bwd(dza, dzg, dpn, x, dx1, g0, w_in0, band_t):
    s = x.shape[0]
    n_tiles = s // TOK

    def body(dza_ref, dzg_ref, dpn_ref, dpp_ref, dpx_ref, x_ref, dx1_ref, g_ref, w_ref, band_ref,
             dz_ref, dx_ref, gn_ref):
        i = pl.program_id(0)

        @pl.when(i == 0)
        def _():
            gn_ref[...] = jnp.zeros_like(gn_ref)

        dz_ref[:, 0:3072] = dza_ref[...]
        dz_ref[:, 4096:5120] = dzg_ref[...]
        dpn = dpn_ref[...]
        halo = _with_halo(i, n_tiles, dpp_ref, dpn, dpx_ref)
        counts = _window_counts(i, TOK, s)
        for g in range(4):
            cols = slice(g * 256, (g + 1) * 256)
            dbx = _dot(band_ref[g], halo[:, cols]) - dpn[:, cols].astype(F32) * counts[g]
            dz_ref[:, 3072 + g * 256:3072 + (g + 1) * 256] = dbx.astype(BF16)
        dh = _dot_nt(dz_ref[...], w_ref[...])
        g0v = g_ref[...]
        r, xh, _ = _rms_fwd(x_ref[...], g0v)
        gn_ref[...] += jnp.sum(dh * xh, axis=0, keepdims=True)
        dx_ref[...] = dx1_ref[...] + _rms_bwd(dh, g0v, r, xh)

    prev, nxt = _halo_specs(s, 0)
    row = _rows(TOK, 1024)
    vec = _whole((1, 1024))
    return pl.pallas_call(
        body, name="in_proj0_bwd", grid=(n_tiles,),
        in_specs=[_rows(TOK, 3072), row, row, prev, nxt, row, row, vec, _whole(w_in0.shape), _whole(band_t.shape)],
        out_specs=[_rows(TOK, 5120), row, vec],
        out_shape=[jax.ShapeDtypeStruct((s, 5120), BF16), jax.ShapeDtypeStruct((s, 1024), F32),
                   jax.ShapeDtypeStruct((1, 1024), F32)],
        compiler_params=_params(),
    )(dza, dzg, dpn, dpn, dpn, x, dx1, g0, w_in0, band_t)


def _weight_grad(a, b, n_blocks, name):
    s, k = a.shape
    n = b.shape[1]
    tn = n // n_blocks
    ts = 512

    def body(a_ref, b_ref, o_ref):
        @pl.when(pl.program_id(1) == 0)
        def _():
            o_ref[...] = jnp.zeros_like(o_ref)

        o_ref[...] += _dot_tn(a_ref[...], b_ref[...])

    return pl.pallas_call(
        body, name=name, grid=(n_blocks, s // ts),
        in_specs=[pl.BlockSpec((ts, k), lambda j, t: (t, 0)), pl.BlockSpec((ts, tn), lambda j, t: (t, j))],
        out_specs=pl.BlockSpec((None, k, tn), lambda j, t: (j, 0, 0)),
        out_shape=jax.ShapeDtypeStruct((n_blocks, k, tn), F32),
        compiler_params=_params(),
    )(a, b)


def _row_tile(rows, cols):
    t = rows
    while t * cols * 4 > (1 << 20) and t % 16 == 0:
        t //= 2
    return t


def _add_sibling(where, g, theirs, name):
    _, _, rows, cols = g.shape
    t = _row_tile(rows, cols)

    def body(where_ref, g_ref, t_ref, o_ref):
        o_ref[...] = (g_ref[...] + t_ref[...]).astype(BF16)

    spec = pl.BlockSpec((None, t, cols), lambda s, i, p: (s, i, 0))
    return pl.pallas_call(
        body, name=name, out_shape=jax.ShapeDtypeStruct((4, rows, cols), BF16),
        grid_spec=pltpu.PrefetchScalarGridSpec(
            num_scalar_prefetch=1, grid=(4, rows // t),
            in_specs=[pl.BlockSpec((None, None, t, cols), lambda s, i, p: (s, p[1], i, 0)), spec], out_specs=spec),
        compiler_params=_params())(where, g, theirs)


def _sum_chips(where, g, theirs, slots, name):
    _, _, rows, cols = g.shape
    t = _row_tile(rows, cols)

    def body(where_ref, g_ref, t_ref, s_ref, o_ref):
        me = where_ref[0]
        own = g_ref[...] + t_ref[...]
        acc = jnp.where(me == 0, own, s_ref[0].astype(F32))
        for k in range(1, 4):
            acc = acc + jnp.where(me == k, own, s_ref[k].astype(F32))
        o_ref[...] = acc

    return pl.pallas_call(
        body, name=name, out_shape=jax.ShapeDtypeStruct((rows, cols), F32),
        grid_spec=pltpu.PrefetchScalarGridSpec(
            num_scalar_prefetch=1, grid=(rows // t,),
            in_specs=[pl.BlockSpec((None, None, t, cols), lambda i, p: (p[0], p[1], i, 0)),
                      pl.BlockSpec((None, t, cols), lambda i, p: (p[0], i, 0)),
                      pl.BlockSpec((4, t, cols), lambda i, p: (0, i, 0))],
            out_specs=pl.BlockSpec((t, cols), lambda i, p: (i, 0))),
        compiler_params=_params())(where, g, theirs, slots)


def _adamw_halves(where, w, own, theirs, m, v, name):
    rows, cols = own.shape
    t = _row_tile(rows, cols)
    per = rows // t

    def body(where_ref, w_ref, own_ref, th_ref, m_ref, v_ref, g_ref, d_ref, nm_ref, nv_ref):
        g = jnp.where(pl.program_id(0) == where_ref[1], own_ref[...], th_ref[...])
        g_ref[...] = g
        m2 = ADAM_B1 * m_ref[...] + (1.0 - ADAM_B1) * g
        v2 = ADAM_B2 * v_ref[...] + (1.0 - ADAM_B2) * (g * g)
        m_hat = m2 / (1.0 - ADAM_B1 ** ADAM_STEP)
        v_hat = v2 / (1.0 - ADAM_B2 ** ADAM_STEP)
        d_ref[...] = -ADAM_LR * (m_hat / (jnp.sqrt(v_hat) + ADAM_EPS) + ADAM_WD * w_ref[...])
        nm_ref[...] = m2
        nv_ref[...] = v2

    full = pl.BlockSpec((t, cols), lambda h, i, p: (h * per + i, 0))
    half = pl.BlockSpec((t, cols), lambda h, i, p: (i, 0))
    shp = jax.ShapeDtypeStruct(w.shape, F32)
    return pl.pallas_call(
        body, name=name, out_shape=[shp] * 4,
        grid_spec=pltpu.PrefetchScalarGridSpec(
            num_scalar_prefetch=1, grid=(2, per), in_specs=[full, half, half, full, full], out_specs=[full] * 4),
        compiler_params=_params())(where, w, own, theirs, m, v)


def _place_shard(where, w, cut, name):
    if cut.kind == "cols":
        r, n = cut.full_shape
        blk, grid = (256, n // 4), (r // 256,)
        src_map, dst_map = (lambda i, p: (i, 0)), (lambda i, p: (i, p[0]))
    elif cut.kind == "rows":
        r, n = cut.full_shape
        per = r // 4 // 256
        blk, grid = (256, n), (per,)
        src_map, dst_map = (lambda i, p: (i, 0)), (lambda i, p: (p[0] * per + i, 0))
    else:
        g, r, n = cut.full_shape
        blk, grid = (g, r // 4, n), (1,)
        src_map, dst_map = (lambda i, p: (0, 0, 0)), (lambda i, p: (0, p[0], 0))

    def body(where_ref, w_ref, o_ref):
        o_ref[...] = w_ref[...].astype(BF16)

    return pl.pallas_call(
        body, name=name, out_shape=jax.ShapeDtypeStruct(cut.full_shape, BF16),
        grid_spec=pltpu.PrefetchScalarGridSpec(
            num_scalar_prefetch=1, grid=grid, in_specs=[pl.BlockSpec(blk, src_map)],
            out_specs=pl.BlockSpec(blk, dst_map)),
        compiler_params=_params())(where, w)


def _sum_slots(parts, name):
    n, rows, cols = parts.shape
    t = _row_tile(rows, cols)

    def body(p_ref, o_ref):
        acc = p_ref[0]
        for k in range(1, n):
            acc = acc + p_ref[k]
        o_ref[...] = acc

    return pl.pallas_call(
        body, name=name, grid=(rows // t,),
        in_specs=[pl.BlockSpec((n, t, cols), lambda i: (0, i, 0))],
        out_specs=pl.BlockSpec((t, cols), lambda i: (i, 0)),
        out_shape=jax.ShapeDtypeStruct((rows, cols), F32), compiler_params=_params())(parts)


def _adamw(w, g, m, v, name):
    rows, cols = w.shape
    t = _row_tile(rows, cols)

    def body(w_ref, g_ref, m_ref, v_ref, d_ref, nm_ref, nv_ref):
        g = g_ref[...]
        m2 = ADAM_B1 * m_ref[...] + (1.0 - ADAM_B1) * g
        v2 = ADAM_B2 * v_ref[...] + (1.0 - ADAM_B2) * (g * g)
        m_hat = m2 / (1.0 - ADAM_B1 ** ADAM_STEP)
        v_hat = v2 / (1.0 - ADAM_B2 ** ADAM_STEP)
        d_ref[...] = -ADAM_LR * (m_hat / (jnp.sqrt(v_hat) + ADAM_EPS) + ADAM_WD * w_ref[...])
        nm_ref[...] = m2
        nv_ref[...] = v2

    spec = pl.BlockSpec((t, cols), lambda i: (i, 0))
    shp = jax.ShapeDtypeStruct(w.shape, F32)
    return pl.pallas_call(body, name=name, grid=(rows // t,), in_specs=[spec] * 4, out_specs=[spec] * 3,
                          out_shape=[shp] * 3, compiler_params=_params())(w, g, m, v)


def _place():
    x, y, c = lax.axis_index("x"), lax.axis_index("y"), lax.axis_index("c")
    chips = [(1 - x, y), (x, 1 - y), (1 - x, 1 - y)]
    return x, y, c, chips


class _Sharded:
    def __init__(self, kind, full_shape):
        self.kind = kind
        self.full_shape = full_shape
        self.chunk_axis = 1 if kind == "mid" else 0

    def in_full(self, ref, s, h):
        if self.kind == "cols":
            r, n = self.full_shape
            return ref.at[pl.ds(h * (r // 2), r // 2), pl.ds(pl.multiple_of(s * (n // 4), 128), n // 4)]
        if self.kind == "rows":
            r, _ = self.full_shape
            return ref.at[pl.ds(pl.multiple_of(s * (r // 4) + h * (r // 8), 8), r // 8), :]
        g, r, _ = self.full_shape
        return ref.at[pl.ds(h * (g // 2), g // 2), pl.ds(pl.multiple_of(s * (r // 4), 16), r // 4), :]


CHUNK_BYTES = 256 * 1024
MAX_CHUNKS = 32


def _cut(ref, axis, n):
    step = ref.shape[axis] // n
    lead = (slice(None),) * axis
    return [ref.at[lead + (pl.ds(k * step, step),)] for k in range(n)]


def _n_chunks(ref, axis):
    nbytes = int(np.prod(ref.shape)) * jnp.dtype(ref.dtype).itemsize
    n = 1
    while n < MAX_CHUNKS and nbytes // (2 * n) >= CHUNK_BYTES and ref.shape[axis] % (32 * n) == 0:
        n *= 2
    return n


def _remote(src, dst, send_sem, recv_sem, to):
    return pltpu.make_async_remote_copy(src_ref=src, dst_ref=dst, send_sem=send_sem, recv_sem=recv_sem,
                                        device_id=to, device_id_type=MESH)


def _start_remote(src, dst, axis, send_sem, recv_sem, to):
    n = _n_chunks(dst, axis)
    for s_k, d_k in zip(_cut(src, axis, n), _cut(dst, axis, n)):
        _remote(s_k, d_k, send_sem, recv_sem, to).start()
    return _remote(src, dst, send_sem, recv_sem, to)


def _start_local(src, dst, axis, sem):
    n = _n_chunks(dst, axis)
    for s_k, d_k in zip(_cut(src, axis, n), _cut(dst, axis, n)):
        pltpu.make_async_copy(s_k, d_k, sem).start()
    return pltpu.make_async_copy(src, dst, sem)


def _gather_weights(shards, cuts):
    n = len(shards)

    def body(*refs):
        src = refs[:n]
        out = refs[n:2 * n]
        send_sems, recv_sems = refs[2 * n:]
        x, y, c, chips = _place()
        me = 2 * x + y
        sibling = (x, y, 1 - c)

        def ends(w, s, h, from_shard):
            dst = cuts[w].in_full(out[w], s, h)
            return (cuts[w].in_full(src[w], s, h) if from_shard else dst), dst

        def remote(k, w, s, h, to, from_shard):
            return _remote(*ends(w, s, h, from_shard), send_sems.at[k], recv_sems.at[k], to)

        def start(k, w, s, h, to, from_shard):
            return _start_remote(*ends(w, s, h, from_shard), cuts[w].chunk_axis, send_sems.at[k], recv_sems.at[k], to)

        first = []
        for w in range(n):
            for j, chip in enumerate(chips):
                first.append(start(3 * w + j, w, me, c, (*chip, c), True))
        passed = []
        for w in range(n):
            for j, chip in enumerate(chips):
                s = 2 * chip[0] + chip[1]
                remote(3 * w + j, w, s, c, (x, y, c), False).wait_recv()
                passed.append(start(3 * n + 3 * w + j, w, s, c, sibling, False))
        for w in range(n):
            for j, chip in enumerate(chips):
                s = 2 * chip[0] + chip[1]
                remote(3 * n + 3 * w + j, w, s, 1 - c, (x, y, c), False).wait_recv()
        for cp in first + passed:
            cp.wait_send()

    return pl.pallas_call(
        body, name="gather_weights",
        in_specs=[ANY] * n, out_specs=[ANY] * n,
        out_shape=[jax.ShapeDtypeStruct(cuts[w].full_shape, BF16) for w in range(n)],
        input_output_aliases={w: w for w in range(n)},
        scratch_shapes=[pltpu.SemaphoreType.DMA((6 * n,)), pltpu.SemaphoreType.DMA((6 * n,))],
        compiler_params=pltpu.CompilerParams(has_side_effects=True),
    )(*shards)


def _exchange_halves(grads, small):
    n = len(grads)
    flips = [(fx, fy, fc) for fx in range(2) for fy in range(2) for fc in range(2)][1:]

    def body(*refs):
        g = refs[:n]
        small_ref = refs[n]
        theirs = refs[n + 1:2 * n + 1]
        gathered = refs[2 * n + 1]
        send_sems, recv_sems, local_sem = refs[2 * n + 2:]
        x, y, c, _ = _place()
        me = 4 * x + 2 * y + c
        local = [_start_local(small_ref, gathered.at[me], 0, local_sem)]
        sends = []
        for w in range(n):
            sends.append(_start_remote(g[w].at[:, 1 - c], theirs[w], 1, send_sems.at[w], recv_sems.at[w],
                                       (x, y, 1 - c)))
        for k, (fx, fy, fc) in enumerate(flips):
            sends.append(_start_remote(
                small_ref, gathered.at[me], 0, send_sems.at[n + k], recv_sems.at[n + k],
                (x + fx - 2 * x * fx, y + fy - 2 * y * fy, c + fc - 2 * c * fc)))
        for w in range(n):
            _remote(g[w].at[:, 1 - c], theirs[w], send_sems.at[w], recv_sems.at[w], (x, y, c)).wait_recv()
        for k, (fx, fy, fc) in enumerate(flips):
            peer = 4 * (x + fx - 2 * x * fx) + 2 * (y + fy - 2 * y * fy) + (c + fc - 2 * c * fc)
            _remote(small_ref, gathered.at[peer], send_sems.at[n + k], recv_sems.at[n + k], (x, y, c)).wait_recv()
        for cp in sends:
            cp.wait_send()
        for cp in local:
            cp.wait()

    halves = [jax.ShapeDtypeStruct((4,) + g.shape[2:], F32) for g in grads]
    return pl.pallas_call(
        body, name="exchange_halves",
        in_specs=[ANY] * (n + 1), out_specs=[ANY] * (n + 1),
        out_shape=halves + [jax.ShapeDtypeStruct((8,) + small.shape, F32)],
        scratch_shapes=[pltpu.SemaphoreType.DMA((n + 7,)), pltpu.SemaphoreType.DMA((n + 7,)),
                        pltpu.SemaphoreType.DMA],
        compiler_params=pltpu.CompilerParams(has_side_effects=True),
    )(*grads, small)


def _scatter_to_chips(parts):
    n = len(parts)

    def body(*refs):
        p = refs[:n]
        out = refs[n:2 * n]
        send_sems, recv_sems = refs[2 * n:]
        x, y, c, chips = _place()
        me = 2 * x + y
        sends = []
        for w in range(n):
            for j, chip in enumerate(chips):
                s = 2 * chip[0] + chip[1]
                sends.append(_start_remote(p[w].at[s], out[w].at[me], 0, send_sems.at[3 * w + j],
                                           recv_sems.at[3 * w + j], (*chip, c)))
        for w in range(n):
            for j, chip in enumerate(chips):
                s = 2 * chip[0] + chip[1]
                _remote(p[w].at[me], out[w].at[s], send_sems.at[3 * w + j], recv_sems.at[3 * w + j],
                        (x, y, c)).wait_recv()
        for cp in sends:
            cp.wait_send()

    return pl.pallas_call(
        body, name="scatter_to_chips",
        in_specs=[ANY] * n, out_specs=[ANY] * n,
        out_shape=[jax.ShapeDtypeStruct(a.shape, a.dtype) for a in parts],
        scratch_shapes=[pltpu.SemaphoreType.DMA((3 * n,)), pltpu.SemaphoreType.DMA((3 * n,))],
        compiler_params=pltpu.CompilerParams(has_side_effects=True),
    )(*parts)


def _share_halves(halves):
    n = len(halves)

    def body(*refs):
        hv = refs[:n]
        out = refs[n:2 * n]
        send_sems, recv_sems = refs[2 * n:]
        x, y, c, _ = _place()
        sends = [_start_remote(hv[w], out[w], 0, send_sems.at[w], recv_sems.at[w], (x, y, 1 - c)) for w in range(n)]
        for w in range(n):
            _remote(hv[w], out[w], send_sems.at[w], recv_sems.at[w], (x, y, c)).wait_recv()
        for cp in sends:
            cp.wait_send()

    return pl.pallas_call(
        body, name="share_halves",
        in_specs=[ANY] * n, out_specs=[ANY] * n,
        out_shape=[jax.ShapeDtypeStruct(a.shape, F32) for a in halves],
        scratch_shapes=[pltpu.SemaphoreType.DMA((n,)), pltpu.SemaphoreType.DMA((n,))],
        compiler_params=pltpu.CompilerParams(has_side_effects=True),
    )(*halves)


SMALL_ROWS = 80


def _pack_small(vecs, ws, bs, sink):
    top = jnp.concatenate(
        [v.reshape(1, 1024) for v in vecs]
        + [jnp.pad(bs.reshape(1, 512), ((0, 0), (0, 512))), jnp.pad(sink.reshape(1, 16), ((0, 0), (0, 1008))),
           jnp.zeros((1, 1024), F32)], axis=0)
    return jnp.concatenate([top, ws.reshape(64, 1024), jnp.zeros((8, 1024), F32)], axis=0)


def _unpack_small(p):
    vecs = [p[k] for k in range(5)]
    return vecs, p[8:72].reshape(4, 128, 128), p[5, :512].reshape(4, 128), p[6, :16]


def kernel(x, norm_0, w_in_0, a_v_norm_0, a_spatial_w_0, a_spatial_b_0, b_group_w_0, b_scale_0, w_out_0, norm_1, w_in_1, sink_1, w_out_1, final_norm, loss_target, m_norm_0, m_w_in_0, m_a_v_norm_0, m_a_spatial_w_0, m_a_spatial_b_0, m_b_group_w_0, m_b_scale_0, m_w_out_0, m_norm_1, m_w_in_1, m_sink_1, m_w_out_1, m_final_norm, v_norm_0, v_w_in_0, v_a_v_norm_0, v_a_spatial_w_0, v_a_spatial_b_0, v_b_group_w_0, v_b_scale_0, v_w_out_0, v_norm_1, v_w_in_1, v_sink_1, v_w_out_1, v_final_norm):
    s = x.shape[1]
    xs = x.reshape(s, D_MODEL)
    target = loss_target.reshape(s, D_MODEL)

    cuts = [_Sharded("cols", (1024, 5120)), _Sharded("rows", (2048, 1024)), _Sharded("cols", (1024, 2560)),
            _Sharded("rows", (1024, 1024)), _Sharded("mid", (4, 256, 256))]
    big_w = [w_in_0, w_out_0, w_in_1, w_out_1, b_group_w_0]
    big_m = [m_w_in_0, m_w_out_0, m_w_in_1, m_w_out_1, m_b_group_w_0]
    big_v = [v_w_in_0, v_w_out_0, v_w_in_1, v_w_out_1, v_b_group_w_0]
    where = jnp.stack([2 * lax.axis_index("x") + lax.axis_index("y"), lax.axis_index("c")]).astype(jnp.int32)
    w_in0, w_out0, w_in1, w_out1, wg = _gather_weights(
        [_place_shard(where, w, cut, f"place_shard{k}") for k, (w, cut) in enumerate(zip(big_w, cuts))], cuts)

    row = lambda v: v.reshape(1, 1024)
    ws16 = a_spatial_w_0.astype(BF16)
    bsx = jnp.repeat(a_spatial_b_0.T, 256, axis=1)
    band, band_t = _band_matrices(TOK)
    rope = _rope_tables(s)

    h0, z0 = _in_proj0(xs, row(norm_0), w_in0)
    cat, x1 = _mix0_fwd(xs, z0, w_out0, row(a_v_norm_0), ws16, bsx, wg, row(b_scale_0), band)
    h1, q, k, v, gate = _in_proj1(x1, row(norm_1), w_in1, rope)
    kpad = jnp.pad(k, ((ATTN_WINDOW, ATTN_WINDOW), (0, 0)))
    vpad = jnp.pad(v, ((ATTN_WINDOW, ATTN_WINDOW), (0, 0)))
    o = _attn_fwd(q, kpad, vpad, sink_1)
    y1, dx2, do, dgate, loss_lanes, g_final, dx2h = _tail(x1, o, gate, target, w_out1, row(final_norm))

    dq, dkpad, dvpad, dsink = _attn_bwd(q, kpad, vpad, sink_1, o, do, rope)
    dk = dkpad[ATTN_WINDOW:ATTN_WINDOW + s]
    dv = dvpad[ATTN_WINDOW:ATTN_WINDOW + s]
    dz1, dx1, g_norm1, dx1h = _in_proj1_bwd(dq, dk, dv, dgate, x1, dx2, row(norm_1), w_in1, rope)
    dza, dzg, dpn, d_ws, d_bs, d_gv, d_scale, d_wg = _mix0_bwd(
        dx1h, z0, w_out0, row(a_v_norm_0), ws16, bsx, wg, row(b_scale_0), band)
    dz0, grad_x, g_norm0 = _in_proj0_bwd(dza, dzg, dpn, xs, dx1, row(norm_0), w_in0, band_t)

    g_w_in0 = _weight_grad(h0, dz0, 4, "grad_w_in0")
    g_w_out0 = _weight_grad(cat, dx1h, 1, "grad_w_out0")
    g_w_in1 = _weight_grad(h1, dz1, 4, "grad_w_in1")
    g_w_out1 = _weight_grad(y1, dx2h, 1, "grad_w_out1")

    grads = [g_w_in0.reshape(4, 2, 512, 1280), g_w_out0.reshape(4, 2, 256, 1024), g_w_in1.reshape(4, 2, 512, 640),
             g_w_out1.reshape(4, 2, 128, 1024),
             d_wg.reshape(2, 2, 4, 64, 256).transpose(2, 0, 1, 3, 4).reshape(4, 2, 128, 256)]
    small = _pack_small([g_norm0, d_gv, d_scale, g_norm1, g_final], d_ws, d_bs[:, :4].T, dsink[0, :16])
    *theirs, small_all = _exchange_halves(grads, small)
    n = len(grads)
    parts = [_add_sibling(where, grads[w], theirs[w], f"add_sibling{w}") for w in range(n)]
    slots = _scatter_to_chips(parts)
    reduced = [_sum_chips(where, grads[w], theirs[w], slots[w], f"sum_chips{w}") for w in range(n)]
    from_sibling = _share_halves(reduced)

    out_g, out_d, out_m, out_v = {}, {}, {}, {}
    names = ["w_in_0", "w_out_0", "w_in_1", "w_out_1", "b_group_w_0"]
    for w in range(n):
        shape = big_w[w].shape
        two_d = (-1, shape[-1])
        outs = _adamw_halves(where, big_w[w].reshape(two_d), reduced[w], from_sibling[w], big_m[w].reshape(two_d),
                             big_v[w].reshape(two_d), f"adamw{w}")
        out_g[names[w]], out_d[names[w]], out_m[names[w]], out_v[names[w]] = (a.reshape(shape) for a in outs)

    g_small = _sum_slots(small_all, "sum_small")
    small_names = ["norm_0", "a_v_norm_0", "b_scale_0", "norm_1", "final_norm"]
    pack = lambda vecs, ws_, bs_, sk: _pack_small(vecs, ws_, bs_, sk)
    w_small = pack([norm_0, a_v_norm_0, b_scale_0, norm_1, final_norm], a_spatial_w_0, a_spatial_b_0, sink_1)
    m_small = pack([m_norm_0, m_a_v_norm_0, m_b_scale_0, m_norm_1, m_final_norm], m_a_spatial_w_0,
                   m_a_spatial_b_0, m_sink_1)
    v_small = pack([v_norm_0, v_a_v_norm_0, v_b_scale_0, v_norm_1, v_final_norm], v_a_spatial_w_0,
                   v_a_spatial_b_0, v_sink_1)
    d_small, nm_small, nv_small = _adamw(w_small, g_small, m_small, v_small, "adamw_small")
    for store, packed in ((out_g, g_small), (out_d, d_small), (out_m, nm_small), (out_v, nv_small)):
        vecs, ws_, bs_, sk = _unpack_small(packed)
        for name, vec in zip(small_names, vecs):
            store[name] = vec
        store["a_spatial_w_0"], store["a_spatial_b_0"], store["sink_1"] = ws_, bs_, sk

    loss = lax.psum(jnp.sum(loss_lanes), ("x", "y", "c"))
    order = ["norm_0", "w_in_0", "a_v_norm_0", "a_spatial_w_0", "a_spatial_b_0", "b_group_w_0", "b_scale_0",
             "w_out_0", "norm_1", "w_in_1", "sink_1", "w_out_1", "final_norm"]
    return (loss, grad_x.reshape(1, s, D_MODEL), *[out_g[k] for k in order], *[out_d[k] for k in order],
            *[out_m[k] for k in order], *[out_v[k] for k in order])
```

```python
import functools

import numpy as np
import jax
import jax.numpy as jnp
from jax import lax
from jax.experimental import pallas as pl
from jax.experimental.pallas import tpu as pltpu

F32 = jnp.float32
BF16 = jnp.bfloat16
MESH = pl.DeviceIdType.MESH

D_MODEL = 1024
EPS = 1e-6
NEG_INF = -1e30
CHUNK = 128
POOL_WINDOWS = (2, 4, 8, 16)
HALO = 16
N_HEADS = 16
HEAD_DIM = 64
ATTN_WINDOW = 128
ROPE_THETA = 500000.0
ROT_DIM = 16
ADAM_LR = 0.001
ADAM_B1 = 0.9
ADAM_B2 = 0.999
ADAM_EPS = 1e-08
ADAM_WD = 0.01
ADAM_STEP = 10

TOK = 256
VMEM_LIMIT = 56 * 1024 * 1024


def _params(**kw):
    return pltpu.CompilerParams(vmem_limit_bytes=VMEM_LIMIT, **kw)


def _whole(shape):
    nd = len(shape)
    return pl.BlockSpec(shape, lambda *_: (0,) * nd)


def _rows(t, n):
    return pl.BlockSpec((t, n), lambda i: (i, 0))


ANY = pl.BlockSpec(memory_space=pl.ANY)

_G0 = 0.7978845608028654
_G1 = 0.044715


def _gelu(x):
    return 0.5 * x * (1.0 + jnp.tanh(_G0 * (x + _G1 * x * x * x)))


def _gelu_and_grad(x):
    x2 = x * x
    t = jnp.tanh(_G0 * (x + _G1 * x2 * x))
    half = 0.5 * (1.0 + t)
    return x * half, half + 0.5 * x * (1.0 - t * t) * (_G0 * (1.0 + 3.0 * _G1 * x2))


def _sigmoid(x):
    return 1.0 / (1.0 + jnp.exp(-x))


def _dot(a, b):
    return jnp.dot(a, b, preferred_element_type=F32)


def _dot_nt(a, b):
    return lax.dot_general(a, b, (((1,), (1,)), ((), ())), preferred_element_type=F32)


def _dot_tn(a, b):
    return lax.dot_general(a, b, (((0,), (0,)), ((), ())), preferred_element_type=F32)


def _rms_fwd(x, g):
    r = lax.rsqrt(jnp.mean(x * x, axis=-1, keepdims=True) + EPS)
    xh = x * r
    return r, xh, xh * g


def _rms_bwd(dy, g, r, xh):
    dxh = dy * g
    return r * (dxh - xh * jnp.mean(dxh * xh, axis=-1, keepdims=True))


def _band_matrices(t):
    r = np.arange(t)[:, None]
    j = np.arange(t + 2 * HALO)[None, :]
    fwd, bwd = [], []
    for w in POOL_WINDOWS:
        d = j - r - HALO
        fwd.append((d >= -(w // 2)) & (d < w // 2))
        bwd.append((d >= -(w // 2) + 1) & (d <= w // 2))
    return (jnp.asarray(np.stack(fwd), BF16), jnp.asarray(np.stack(bwd), BF16))


def _window_counts(i, t, s):
    tok = i * t + lax.broadcasted_iota(jnp.int32, (t, 1), 0)
    out = []
    for w in POOL_WINDOWS:
        cnt = jnp.minimum(tok + w // 2, s) - jnp.maximum(tok - w // 2, 0)
        out.append(cnt.astype(F32))
    return out


def _rope_tables(s):
    inv = np.float32(ROPE_THETA) ** (-np.arange(0, ROT_DIM, 2, dtype=np.float32) / np.float32(ROT_DIM))
    ang = np.arange(s, dtype=np.float32)[:, None] * inv.astype(np.float32)[None, :]
    cos, sin = np.cos(ang).astype(np.float32), np.sin(ang).astype(np.float32)
    z8 = np.zeros((s, 8), np.float32)
    z48 = np.zeros((s, HEAD_DIM - ROT_DIM), np.float32)
    c = np.concatenate([cos, cos, np.ones((s, HEAD_DIM - ROT_DIM), np.float32)], axis=1)
    s_lo = np.concatenate([z8, sin, z48], axis=1)
    s_hi = np.concatenate([-sin, z8, z48], axis=1)
    return tuple(jnp.asarray(np.concatenate([a, a], axis=1)) for a in (c, s_lo, s_hi))


def _rope(x, c, s_lo, s_hi):
    n = x.shape[1]
    reps = n // 128
    c, s_lo, s_hi = (jnp.tile(a, (1, reps)) for a in (c, s_lo, s_hi))
    return x * c + pltpu.roll(x, 8, 1) * s_lo + pltpu.roll(x, n - 8, 1) * s_hi


def _rope_t(dx, c, s_lo, s_hi):
    n = dx.shape[1]
    reps = n // 128
    c, s_lo, s_hi = (jnp.tile(a, (1, reps)) for a in (c, s_lo, s_hi))
    return dx * c + pltpu.roll(dx * s_lo, n - 8, 1) + pltpu.roll(dx * s_hi, 8, 1)


def _in_proj0(x, g0, w_in0):
    s = x.shape[0]
    n = w_in0.shape[1]

    def body(x_ref, g_ref, w_ref, h_ref, z_ref):
        _, _, h = _rms_fwd(x_ref[...], g_ref[...])
        h = h.astype(BF16)
        h_ref[...] = h
        for j in range(n // 1024):
            z_ref[:, j * 1024:(j + 1) * 1024] = _dot(h, w_ref[:, j * 1024:(j + 1) * 1024]).astype(BF16)

    return pl.pallas_call(
        body, name="in_proj0", grid=(s // TOK,),
        in_specs=[_rows(TOK, D_MODEL), _whole((1, D_MODEL)), _whole(w_in0.shape)],
        out_specs=[_rows(TOK, D_MODEL), _rows(TOK, n)],
        out_shape=[jax.ShapeDtypeStruct((s, D_MODEL), BF16), jax.ShapeDtypeStruct((s, n), BF16)],
        compiler_params=_params(),
    )(x, g0, w_in0)


def _halo_specs(s, col_block):
    per = TOK // HALO
    last = s // HALO - 1
    prev = pl.BlockSpec((HALO, 1024), lambda i: (jnp.maximum(i * per - 1, 0), col_block))
    nxt = pl.BlockSpec((HALO, 1024), lambda i: (jnp.minimum((i + 1) * per, last), col_block))
    return prev, nxt


def _with_halo(i, n_tiles, prev_ref, cur, next_ref):
    prev = prev_ref[...]
    nxt = next_ref[...]
    prev = jnp.where(i > 0, prev, jnp.zeros_like(prev))
    nxt = jnp.where(i < n_tiles - 1, nxt, jnp.zeros_like(nxt))
    return jnp.concatenate([prev, cur, nxt], axis=0)


def _mixer_a(v1, gv, ws_ref, bsx):
    rv, vh, v2 = _rms_fwd(v1, gv)
    v2 = v2.astype(BF16)
    rows = []
    for c in range(v1.shape[0] // CHUNK):
        cols = [_dot(ws_ref[h], v2[c * CHUNK:(c + 1) * CHUNK, h * 256:(h + 1) * 256]) for h in range(4)]
        rows.append(jnp.concatenate(cols, axis=1) + bsx)
    return rv, vh, v2, jnp.concatenate(rows, axis=0)


def _mixer_b_pooled(bx, halo, band_ref, counts):
    out = []
    for g in range(4):
        win = _dot(band_ref[g], halo[:, g * 256:(g + 1) * 256])
        out.append(win / counts[g] - bx[:, g * 256:(g + 1) * 256])
    return out


def _mix0_fwd(x, z0, w_out0, gv, ws, bsx, wg, scale, band):
    s = x.shape[0]
    n_tiles = s // TOK

    def body(z_ref, zp_ref, zn_ref, x_ref, wout_ref, gv_ref, ws_ref, bsx_ref, wg_ref, sc_ref, band_ref,
             cat_ref, x1_ref):
        i = pl.program_id(0)
        au = z_ref[:, 0:1024].astype(F32)
        av = z_ref[:, 1024:2048].astype(F32)
        ag = z_ref[:, 2048:3072].astype(F32)
        _, _, _, mixed = _mixer_a(_gelu(av), gv_ref[...], ws_ref, bsx_ref[...])
        cat_ref[:, 0:1024] = (_gelu(au) * mixed * (ag * _sigmoid(ag))).astype(BF16)

        bx16 = z_ref[:, 3072:4096]
        bg = z_ref[:, 4096:5120].astype(F32)
        halo = _with_halo(i, n_tiles, zp_ref, bx16, zn_ref)
        ps = _mixer_b_pooled(bx16.astype(F32), halo, band_ref, _window_counts(i, TOK, s))
        pw = jnp.concatenate([_dot(ps[g].astype(BF16), wg_ref[g]) for g in range(4)], axis=1)
        cat_ref[:, 1024:2048] = (pw * sc_ref[...] * (bg * _sigmoid(bg))).astype(BF16)

        x1_ref[...] = x_ref[...] + _dot(cat_ref[...], wout_ref[...])

    prev, nxt = _halo_specs(s, 3)
    return pl.pallas_call(
        body, name="mix0_fwd", grid=(n_tiles,),
        in_specs=[_rows(TOK, 5120), prev, nxt, _rows(TOK, D_MODEL), _whole(w_out0.shape), _whole((1, 1024)),
                  _whole(ws.shape), _whole(bsx.shape), _whole(wg.shape), _whole((1, 1024)), _whole(band.shape)],
        out_specs=[_rows(TOK, 2048), _rows(TOK, D_MODEL)],
        out_shape=[jax.ShapeDtypeStruct((s, 2048), BF16), jax.ShapeDtypeStruct((s, D_MODEL), F32)],
        compiler_params=_params(),
    )(z0, z0, z0, x, w_out0, gv, ws, bsx, wg, scale, band)


def _in_proj1(x1, g1, w_in1, rope):
    s = x1.shape[0]

    def body(x_ref, g_ref, w_ref, c_ref, lo_ref, hi_ref, h_ref, q_ref, k_ref, v_ref, gate_ref):
        _, _, h = _rms_fwd(x_ref[...], g_ref[...])
        h = h.astype(BF16)
        h_ref[...] = h
        tabs = (c_ref[...], lo_ref[...], hi_ref[...])
        q_ref[...] = (_rope(_dot(h, w_ref[:, 0:1024]), *tabs) * Q_SCALE).astype(BF16)
        kv = _dot(h, w_ref[:, 1024:1536])
        k_ref[...] = _rope(kv[:, 0:256], *tabs).astype(BF16)
        v_ref[...] = kv[:, 256:512].astype(BF16)
        gate_ref[...] = _dot(h, w_ref[:, 1536:2560]).astype(BF16)

    tab = _rows(TOK, 128)
    return pl.pallas_call(
        body, name="in_proj1", grid=(s // TOK,),
        in_specs=[_rows(TOK, D_MODEL), _whole((1, D_MODEL)), _whole(w_in1.shape), tab, tab, tab],
        out_specs=[_rows(TOK, 1024), _rows(TOK, 1024), _rows(TOK, 256), _rows(TOK, 256), _rows(TOK, 1024)],
        out_shape=[jax.ShapeDtypeStruct((s, 1024), BF16), jax.ShapeDtypeStruct((s, 1024), BF16),
                   jax.ShapeDtypeStruct((s, 256), BF16), jax.ShapeDtypeStruct((s, 256), BF16),
                   jax.ShapeDtypeStruct((s, 1024), BF16)],
        compiler_params=_params(),
    )(x1, g1, w_in1, *rope)


QBLK = 128
KBLK = QBLK + 2 * ATTN_WINDOW
Q_SCALE = HEAD_DIM ** -0.5
HEAD_BATCH = 4


def _block_bias(q0, s):
    r = lax.broadcasted_iota(jnp.int32, (QBLK, KBLK), 0)
    c = lax.broadcasted_iota(jnp.int32, (QBLK, KBLK), 1)
    kj = q0 - ATTN_WINDOW + c
    ok = (c >= r) & (c <= r + 2 * ATTN_WINDOW) & (kj >= 0) & (kj < s)
    return jnp.where(ok, 0.0, NEG_INF)


def _attn_fwd(q, kpad, vpad, sink):
    s = q.shape[0]

    def body(sink_ref, q_ref, k_ref, v_ref, o_ref):
        i = pl.program_id(0)
        for b in range(TOK // QBLK):
            rows = slice(b * QBLK, (b + 1) * QBLK)
            start = pl.multiple_of(i * TOK + b * QBLK, QBLK)
            kb = k_ref[pl.ds(start, KBLK), :]
            vb = v_ref[pl.ds(start, KBLK), :]
            bias = _block_bias(i * TOK + b * QBLK, s)
            kv = lambda t, h: t[:, (h // 4) * HEAD_DIM:(h // 4 + 1) * HEAD_DIM]
            scs = [_dot_nt(q_ref[rows, h * HEAD_DIM:(h + 1) * HEAD_DIM], kv(kb, h)) + bias for h in range(N_HEADS)]
            ms = [jnp.maximum(jnp.max(scs[h], axis=-1, keepdims=True), sink_ref[h]) for h in range(N_HEADS)]
            es = [jnp.exp(scs[h] - ms[h]) for h in range(N_HEADS)]
            rdens = [1.0 / (jnp.sum(es[h], axis=-1, keepdims=True) + jnp.exp(sink_ref[h] - ms[h]))
                     for h in range(N_HEADS)]
            outs = [_dot(es[h].astype(BF16), kv(vb, h)) * rdens[h] for h in range(N_HEADS)]
            o_ref[rows, :] = jnp.concatenate(outs, axis=1).astype(BF16)

    return pl.pallas_call(
        body, name="attn_fwd", grid=(s // TOK,),
        in_specs=[pl.BlockSpec(memory_space=pltpu.SMEM), _rows(TOK, 1024), _whole(kpad.shape), _whole(vpad.shape)],
        out_specs=_rows(TOK, 1024),
        out_shape=jax.ShapeDtypeStruct((s, 1024), BF16),
        compiler_params=_params(),
    )(sink, q, kpad, vpad)


def _tail(x1, o, gate, target, w_out1, gf):
    s = x1.shape[0]

    def body(x1_ref, o_ref, gate_ref, t_ref, w_ref, gf_ref, y1_ref, dx2_ref, do_ref, dgate_ref, loss_ref, gfn_ref,
             dx2h_ref):
        i = pl.program_id(0)

        @pl.when(i == 0)
        def _():
            loss_ref[...] = jnp.zeros_like(loss_ref)
            gfn_ref[...] = jnp.zeros_like(gfn_ref)

        g = gate_ref[...].astype(F32)
        sg = _sigmoid(g)
        sil = g * sg
        o = o_ref[...].astype(F32)
        y1 = (o * sil).astype(BF16)
        y1_ref[...] = y1
        x2 = x1_ref[...] + _dot(y1, w_ref[...])
        gf = gf_ref[...]
        r, xh, out = _rms_fwd(x2, gf)
        diff = out - t_ref[...]
        loss_ref[...] += jnp.sum(diff * diff, axis=0, keepdims=True) * (0.5 / D_MODEL)
        dout = diff * (1.0 / D_MODEL)
        gfn_ref[...] += jnp.sum(dout * xh, axis=0, keepdims=True)
        dx2 = _rms_bwd(dout, gf, r, xh)
        dx2_ref[...] = dx2
        dx2h = dx2.astype(BF16)
        dx2h_ref[...] = dx2h
        dy1 = _dot_nt(dx2h, w_ref[...])
        do_ref[...] = (dy1 * sil).astype(BF16)
        dgate_ref[...] = (dy1 * o * (sg * (1.0 + g * (1.0 - sg)))).astype(BF16)

    row = _rows(TOK, 1024)
    acc = _whole((1, 1024))
    return pl.pallas_call(
        body, name="tail", grid=(s // TOK,),
        in_specs=[row, row, row, row, _whole(w_out1.shape), acc],
        out_specs=[row, row, row, row, acc, acc, row],
        out_shape=[jax.ShapeDtypeStruct((s, 1024), BF16), jax.ShapeDtypeStruct((s, 1024), F32),
                   jax.ShapeDtypeStruct((s, 1024), BF16), jax.ShapeDtypeStruct((s, 1024), BF16),
                   jax.ShapeDtypeStruct((1, 1024), F32), jax.ShapeDtypeStruct((1, 1024), F32),
                   jax.ShapeDtypeStruct((s, 1024), BF16)],
        compiler_params=_params(),
    )(x1, o, gate, target, w_out1, gf)


def _attn_bwd(q, kpad, vpad, sink, o, do, rope):
    s = q.shape[0]
    pad_t = (kpad.shape[1], kpad.shape[0])

    def body(sink_ref, q_ref, k_ref, v_ref, o_ref, do_ref, c_ref, lo_ref, hi_ref, dq_ref, dk_ref, dv_ref, ds_ref):
        i = pl.program_id(0)

        @pl.when(i == 0)
        def _():
            dk_ref[...] = jnp.zeros_like(dk_ref)
            dv_ref[...] = jnp.zeros_like(dv_ref)
            ds_ref[...] = jnp.zeros_like(ds_ref)

        lane = lax.broadcasted_iota(jnp.int32, (1, 128), 1)
        dsink = jnp.zeros((1, 128), F32)
        for b in range(TOK // QBLK):
            rows = slice(b * QBLK, (b + 1) * QBLK)
            start = pl.multiple_of(i * TOK + b * QBLK, QBLK)
            kb = k_ref[pl.ds(start, KBLK), :]
            vb = v_ref[pl.ds(start, KBLK), :]
            bias = _block_bias(i * TOK + b * QBLK, s)
            kv = lambda t, h: t[:, (h // 4) * HEAD_DIM:(h // 4 + 1) * HEAD_DIM]
            hd = lambda ref, h: ref[rows, h * HEAD_DIM:(h + 1) * HEAD_DIM]
            dqs, dks, dvs = {}, {}, {}
            for h0 in range(0, N_HEADS, HEAD_BATCH):
                heads = range(h0, h0 + HEAD_BATCH)
                scs = {h: _dot_nt(hd(q_ref, h), kv(kb, h)) + bias for h in heads}
                ms = {h: jnp.maximum(jnp.max(scs[h], axis=-1, keepdims=True), sink_ref[h]) for h in heads}
                es = {h: jnp.exp(scs[h] - ms[h]) for h in heads}
                e_sinks = {h: jnp.exp(sink_ref[h] - ms[h]) for h in heads}
                rdens = {h: 1.0 / (jnp.sum(es[h], axis=-1, keepdims=True) + e_sinks[h]) for h in heads}
                dos = {h: hd(do_ref, h).astype(F32) for h in heads}
                deltas = {h: jnp.sum(dos[h] * hd(o_ref, h).astype(F32), axis=-1, keepdims=True) * rdens[h]
                          for h in heads}
                do_ns = {h: (dos[h] * rdens[h]).astype(BF16) for h in heads}
                for h in heads:
                    dsink = dsink + jnp.where(
                        lane == h, -jnp.sum(e_sinks[h] * deltas[h], axis=0, keepdims=True), 0.0)
                dscs = {h: (es[h] * (_dot_nt(do_ns[h], kv(vb, h)) - deltas[h])).astype(BF16) for h in heads}
                dvs.update({h: _dot_tn(do_ns[h], es[h].astype(BF16)) for h in heads})
                dks.update({h: _dot_tn(hd(q_ref, h), dscs[h]) for h in heads})
                dqs.update({h: _dot(dscs[h], kv(kb, h)) * Q_SCALE for h in heads})
            dq = jnp.concatenate([dqs[h] for h in range(N_HEADS)], axis=1)
            dq_ref[rows, :] = _rope_t(dq, c_ref[rows, :], lo_ref[rows, :], hi_ref[rows, :]).astype(BF16)
            group_sum = lambda parts, g: (parts[4 * g] + parts[4 * g + 1]) + (parts[4 * g + 2] + parts[4 * g + 3])
            dk_ref[:, pl.ds(start, KBLK)] += jnp.concatenate([group_sum(dks, g) for g in range(4)], axis=0)
            dv_ref[:, pl.ds(start, KBLK)] += jnp.concatenate([group_sum(dvs, g) for g in range(4)], axis=0)
        ds_ref[...] += dsink

    row = _rows(TOK, 1024)
    tab = _rows(TOK, 128)
    pad = _whole(kpad.shape)
    return pl.pallas_call(
        body, name="attn_bwd", grid=(s // TOK,),
        in_specs=[pl.BlockSpec(memory_space=pltpu.SMEM), row, pad, pad, row, row, tab, tab, tab],
        out_specs=[row, _whole(pad_t), _whole(pad_t), _whole((1, 128))],
        out_shape=[jax.ShapeDtypeStruct((s, 1024), BF16), jax.ShapeDtypeStruct(pad_t, F32),
                   jax.ShapeDtypeStruct(pad_t, F32), jax.ShapeDtypeStruct((1, 128), F32)],
        compiler_params=_params(),
    )(sink, q, kpad, vpad, o, do, *rope)


def _in_proj1_bwd(dq, dk, dv, dgate, x1, dx2, g1, w_in1, rope):
    s = x1.shape[0]

    def body(dq_ref, dk_ref, dv_ref, dgate_ref, x1_ref, dx2_ref, g_ref, w_ref, c_ref, lo_ref, hi_ref,
             dz_ref, dx1_ref, gn_ref, dx1h_ref):
        i = pl.program_id(0)

        @pl.when(i == 0)
        def _():
            gn_ref[...] = jnp.zeros_like(gn_ref)

        dz_ref[:, 0:1024] = dq_ref[...]
        dz_ref[:, 1024:1280] = _rope_t(dk_ref[...], c_ref[...], lo_ref[...], hi_ref[...]).astype(BF16)
        dz_ref[:, 1280:1536] = dv_ref[...].astype(BF16)
        dz_ref[:, 1536:2560] = dgate_ref[...]
        dh = _dot_nt(dz_ref[...], w_ref[...])
        g = g_ref[...]
        r, xh, _ = _rms_fwd(x1_ref[...], g)
        gn_ref[...] += jnp.sum(dh * xh, axis=0, keepdims=True)
        dx1 = dx2_ref[...] + _rms_bwd(dh, g, r, xh)
        dx1_ref[...] = dx1
        dx1h_ref[...] = dx1.astype(BF16)

    row = _rows(TOK, 1024)
    nar = _rows(TOK, 256)
    tab = _rows(TOK, 128)
    acc = _whole((1, 1024))
    return pl.pallas_call(
        body, name="in_proj1_bwd", grid=(s // TOK,),
        in_specs=[row, nar, nar, row, row, row, acc, _whole(w_in1.shape), tab, tab, tab],
        out_specs=[_rows(TOK, 2560), row, acc, row],
        out_shape=[jax.ShapeDtypeStruct((s, 2560), BF16), jax.ShapeDtypeStruct((s, 1024), F32),
                   jax.ShapeDtypeStruct((1, 1024), F32), jax.ShapeDtypeStruct((s, 1024), BF16)],
        compiler_params=_params(),
    )(dq, dk, dv, dgate, x1, dx2, g1, w_in1, *rope)


def _mix0_bwd(dx1, z0, w_out0, gv, ws, bsx, wg, scale, band, rider):
    s = dx1.shape[0]
    n_tiles = s // TOK

    def body(dx1h_ref, z_ref, zp_ref, zn_ref, wout_ref, gv_ref, ws_ref, bsx_ref, wg_ref, sc_ref, band_ref,
             dz_ref, dpn_ref, dws_ref, dbs_ref, dgv_ref, dsc_ref, dwg_ref):
        i = pl.program_id(0)
        dz_ref[:, 3072:4096] = jnp.zeros((TOK, 1024), BF16)

        @pl.when(i == 0)
        def _():
            for ref in (dws_ref, dbs_ref, dgv_ref, dsc_ref, dwg_ref):
                ref[...] = jnp.zeros_like(ref)

        dcat = _dot_nt(dx1h_ref[...], wout_ref[...])
        dya = dcat[:, 0:1024]
        dyb = dcat[:, 1024:2048]

        au = z_ref[:, 0:1024].astype(F32)
        av = z_ref[:, 1024:2048].astype(F32)
        ag = z_ref[:, 2048:3072].astype(F32)
        gv = gv_ref[...]
        u, du = _gelu_and_grad(au)
        v1, dv1 = _gelu_and_grad(av)
        rv, vh, v2, mixed = _mixer_a(v1, gv, ws_ref, bsx_ref[...])
        sg = _sigmoid(ag)
        sil = ag * sg
        dz_ref[:, 2048:3072] = (dya * u * mixed * (sg * (1.0 + ag * (1.0 - sg)))).astype(BF16)
        dz_ref[:, 0:1024] = (dya * mixed * sil * du).astype(BF16)
        dmixed = dya * u * sil
        lane = lax.broadcasted_iota(jnp.int32, (1, 128), 1)
        dm16 = dmixed.astype(BF16)
        dv2_rows = []
        for c in range(TOK // CHUNK):
            rows = slice(c * CHUNK, (c + 1) * CHUNK)
            cols_out = []
            for h in range(4):
                cols = slice(h * 256, (h + 1) * 256)
                dws_ref[h] += _dot_nt(dm16[rows, cols], v2[rows, cols])
                dbs_ref[...] += jnp.where(lane == h, jnp.sum(dmixed[rows, cols], axis=-1, keepdims=True), 0.0)
                cols_out.append(_dot_tn(ws_ref[h], dm16[rows, cols]))
            dv2_rows.append(jnp.concatenate(cols_out, axis=1))
        dv2 = jnp.concatenate(dv2_rows, axis=0)
        dgv_ref[...] += jnp.sum(dv2 * vh, axis=0, keepdims=True)
        dz_ref[:, 1024:2048] = (_rms_bwd(dv2, gv, rv, vh) * dv1).astype(BF16)

        bx16 = z_ref[:, 3072:4096]
        bg = z_ref[:, 4096:5120].astype(F32)
        counts = _window_counts(i, TOK, s)
        halo = _with_halo(i, n_tiles, zp_ref, bx16, zn_ref)
        ps = [p.astype(BF16) for p in _mixer_b_pooled(bx16.astype(F32), halo, band_ref, counts)]
        pw = jnp.concatenate([_dot(ps[g], wg_ref[g]) for g in range(4)], axis=1)
        sgb = _sigmoid(bg)
        sc = sc_ref[...]
        dz_ref[:, 4096:5120] = (dyb * pw * sc * (sgb * (1.0 + bg * (1.0 - sgb)))).astype(BF16)
        dys = dyb * (bg * sgb)
        dsc_ref[...] += jnp.sum(dys * pw, axis=0, keepdims=True)
        dpw = (dys * sc).astype(BF16)
        for g in range(4):
            cols = slice(g * 256, (g + 1) * 256)
            dwg_ref[g] += _dot_tn(ps[g], dpw[:, cols])
            dpn_ref[:, cols] = (_dot_nt(dpw[:, cols], wg_ref[g]) / counts[g]).astype(BF16)

    prev, nxt = _halo_specs(s, 3)
    row = _rows(TOK, 1024)
    vec = _whole((1, 1024))
    return pl.pallas_call(
        rider.carried_by(body, 11, 7, n_tiles), name="mix0_bwd", grid=(n_tiles,),
        in_specs=[row, _rows(TOK, 5120), prev, nxt, _whole(w_out0.shape), vec, _whole(ws.shape),
                  _whole(bsx.shape), _whole(wg.shape), vec, _whole(band.shape)] + rider.in_specs,
        out_specs=[_rows(TOK, 5120), row, _whole((4, 128, 128)), _whole((128, 128)), vec, vec,
                   _whole((4, 256, 256))] + rider.out_specs,
        out_shape=[jax.ShapeDtypeStruct((s, 5120), BF16), jax.ShapeDtypeStruct((s, 1024), BF16),
                   jax.ShapeDtypeStruct((4, 128, 128), F32), jax.ShapeDtypeStruct((128, 128), F32),
                   jax.ShapeDtypeStruct((1, 1024), F32), jax.ShapeDtypeStruct((1, 1024), F32),
                   jax.ShapeDtypeStruct((4, 256, 256), F32)] + rider.out_shape,
        scratch_shapes=rider.scratch,
        compiler_params=_params(),
    )(dx1, z0, z0, z0, w_out0, gv, ws, bsx, wg, scale, band, *rider.parts)


def _fill_pooled_grad(dz0, dpn, band_t):
    s = dpn.shape[0]
    n_tiles = s // TOK

    def body(dz_in_ref, dpn_ref, dpp_ref, dpx_ref, band_ref, dbx_ref):
        i = pl.program_id(0)
        dpn = dpn_ref[...]
        halo = _with_halo(i, n_tiles, dpp_ref, dpn, dpx_ref)
        counts = _window_counts(i, TOK, s)
        for g in range(4):
            cols = slice(g * 256, (g + 1) * 256)
            dbx = _dot(band_ref[g], halo[:, cols]) - dpn[:, cols].astype(F32) * counts[g]
            dbx_ref[:, cols] = dbx.astype(BF16)

    prev, nxt = _halo_specs(s, 0)
    return pl.pallas_call(
        body, name="fill_pooled_grad", grid=(n_tiles,),
        in_specs=[ANY, _rows(TOK, 1024), prev, nxt, _whole(band_t.shape)],
        out_specs=pl.BlockSpec((TOK, 1024), lambda i: (i, 3)),
        out_shape=jax.ShapeDtypeStruct(dz0.shape, BF16),
        input_output_aliases={0: 0},
        compiler_params=_params(),
    )(dz0, dpn, dpn, dpn, band_t)


def _in_proj0_bwd(dz0, x, dx1, g0, w_in0, rider):
    s = x.shape[0]
    n_tiles = s // TOK

    def body(dz_ref, x_ref, dx1_ref, g_ref, w_ref, dx_ref, gn_ref):
        i = pl.program_id(0)

        @pl.when(i == 0)
        def _():
            gn_ref[...] = jnp.zeros_like(gn_ref)

        dh = _dot_nt(dz_ref[...], w_ref[...])
        g0v = g_ref[...]
        r, xh, _ = _rms_fwd(x_ref[...], g0v)
        gn_ref[...] += jnp.sum(dh * xh, axis=0, keepdims=True)
        dx_ref[...] = dx1_ref[...] + _rms_bwd(dh, g0v, r, xh)

    row = _rows(TOK, 1024)
    vec = _whole((1, 1024))
    return pl.pallas_call(
        rider.carried_by(body, 5, 2, n_tiles), name="in_proj0_bwd", grid=(n_tiles,),
        in_specs=[_rows(TOK, 5120), row, row, vec, _whole(w_in0.shape)] + rider.in_specs,
        out_specs=[row, vec] + rider.out_specs,
        out_shape=[jax.ShapeDtypeStruct((s, 1024), F32), jax.ShapeDtypeStruct((1, 1024), F32)] + rider.out_shape,
        scratch_shapes=rider.scratch,
        compiler_params=_params(),
    )(dz0, x, dx1, g0, w_in0, *rider.parts)


def _weight_grad(a, b, n_blocks, split, name):
    s, k = a.shape
    n = b.shape[1]
    tn = n // n_blocks
    w = tn // split
    ts = 512

    def body(a_ref, b_ref, o_ref):
        @pl.when(pl.program_id(1) == 0)
        def _():
            o_ref[...] = jnp.zeros_like(o_ref)

        res = _dot_tn(a_ref[...], b_ref[...])
        for q in range(split):
            o_ref[q] += res[:, q * w:(q + 1) * w]

    return pl.pallas_call(
        body, name=name, grid=(n_blocks, s // ts),
        in_specs=[pl.BlockSpec((ts, k), lambda j, t: (t, 0)), pl.BlockSpec((ts, tn), lambda j, t: (t, j))],
        out_specs=pl.BlockSpec((split, k, w), lambda j, t: (j, 0, 0)),
        out_shape=jax.ShapeDtypeStruct((n_blocks * split, k, w), F32),
        compiler_params=_params(),
    )(a, b)


def _row_tile(rows, cols):
    t = rows
    while t * cols * 4 > (1 << 20) and t % 16 == 0:
        t //= 2
    return t


def _add_sibling(where, g, theirs, name):
    _, _, rows, cols = g.shape
    t = _row_tile(rows, cols)

    def body(where_ref, g_ref, t_ref, o_ref):
        o_ref[...] = (g_ref[...] + t_ref[...]).astype(BF16)

    spec = pl.BlockSpec((None, t, cols), lambda s, i, p: (s, i, 0))
    return pl.pallas_call(
        body, name=name, out_shape=jax.ShapeDtypeStruct((4, rows, cols), BF16),
        grid_spec=pltpu.PrefetchScalarGridSpec(
            num_scalar_prefetch=1, grid=(4, rows // t),
            in_specs=[pl.BlockSpec((None, None, t, cols), lambda s, i, p: (s, p[1], i, 0)), spec], out_specs=spec),
        compiler_params=_params())(where, g, theirs)


def _sum_chips(where, g, theirs, slots, name):
    _, _, rows, cols = g.shape
    t = _row_tile(rows, cols)

    def body(where_ref, g_ref, t_ref, s_ref, o_ref):
        me = where_ref[0]
        own = g_ref[...] + t_ref[...]
        acc = jnp.where(me == 0, own, s_ref[0].astype(F32))
        for k in range(1, 4):
            acc = acc + jnp.where(me == k, own, s_ref[k].astype(F32))
        o_ref[...] = acc

    return pl.pallas_call(
        body, name=name, out_shape=jax.ShapeDtypeStruct((rows, cols), F32),
        grid_spec=pltpu.PrefetchScalarGridSpec(
            num_scalar_prefetch=1, grid=(rows // t,),
            in_specs=[pl.BlockSpec((None, None, t, cols), lambda i, p: (p[0], p[1], i, 0)),
                      pl.BlockSpec((None, t, cols), lambda i, p: (p[0], i, 0)),
                      pl.BlockSpec((4, t, cols), lambda i, p: (0, i, 0))],
            out_specs=pl.BlockSpec((t, cols), lambda i, p: (i, 0))),
        compiler_params=_params())(where, g, theirs, slots)


def _adamw_halves(where, w, own, theirs, m, v, name):
    rows, cols = own.shape
    t = _row_tile(rows, cols)
    per = rows // t

    def body(where_ref, w_ref, own_ref, th_ref, m_ref, v_ref, g_ref, d_ref, nm_ref, nv_ref):
        g = jnp.where(pl.program_id(0) == where_ref[1], own_ref[...], th_ref[...])
        g_ref[...] = g
        m2 = ADAM_B1 * m_ref[...] + (1.0 - ADAM_B1) * g
        v2 = ADAM_B2 * v_ref[...] + (1.0 - ADAM_B2) * (g * g)
        m_hat = m2 / (1.0 - ADAM_B1 ** ADAM_STEP)
        v_hat = v2 / (1.0 - ADAM_B2 ** ADAM_STEP)
        d_ref[...] = -ADAM_LR * (m_hat / (jnp.sqrt(v_hat) + ADAM_EPS) + ADAM_WD * w_ref[...])
        nm_ref[...] = m2
        nv_ref[...] = v2

    full = pl.BlockSpec((t, cols), lambda h, i, p: (h * per + i, 0))
    half = pl.BlockSpec((t, cols), lambda h, i, p: (i, 0))
    shp = jax.ShapeDtypeStruct(w.shape, F32)
    return pl.pallas_call(
        body, name=name, out_shape=[shp] * 4,
        grid_spec=pltpu.PrefetchScalarGridSpec(
            num_scalar_prefetch=1, grid=(2, per), in_specs=[full, half, half, full, full], out_specs=[full] * 4),
        compiler_params=_params())(where, w, own, theirs, m, v)


def _place_shard(where, w, cut, name):
    if cut.kind == "cols":
        r, n = cut.full_shape
        blk, grid = (256, n // 4), (r // 256,)
        src_map, dst_map = (lambda i, p: (i, 0)), (lambda i, p: (i, p[0]))
    elif cut.kind == "rows":
        r, n = cut.full_shape
        per = r // 4 // 256
        blk, grid = (256, n), (per,)
        src_map, dst_map = (lambda i, p: (i, 0)), (lambda i, p: (p[0] * per + i, 0))
    else:
        g, r, n = cut.full_shape
        blk, grid = (g, r // 4, n), (1,)
        src_map, dst_map = (lambda i, p: (0, 0, 0)), (lambda i, p: (0, p[0], 0))

    def body(where_ref, w_ref, o_ref):
        o_ref[...] = w_ref[...].astype(BF16)

    return pl.pallas_call(
        body, name=name, out_shape=jax.ShapeDtypeStruct(cut.full_shape, BF16),
        grid_spec=pltpu.PrefetchScalarGridSpec(
            num_scalar_prefetch=1, grid=grid, in_specs=[pl.BlockSpec(blk, src_map)],
            out_specs=pl.BlockSpec(blk, dst_map)),
        compiler_params=_params())(where, w)


def _sum_slots(parts, name):
    n, rows, cols = parts.shape
    t = _row_tile(rows, cols)

    def body(p_ref, o_ref):
        acc = p_ref[0]
        for k in range(1, n):
            acc = acc + p_ref[k]
        o_ref[...] = acc

    return pl.pallas_call(
        body, name=name, grid=(rows // t,),
        in_specs=[pl.BlockSpec((n, t, cols), lambda i: (0, i, 0))],
        out_specs=pl.BlockSpec((t, cols), lambda i: (i, 0)),
        out_shape=jax.ShapeDtypeStruct((rows, cols), F32), compiler_params=_params())(parts)


def _adamw(w, g, m, v, name):
    rows, cols = w.shape
    t = _row_tile(rows, cols)

    def body(w_ref, g_ref, m_ref, v_ref, d_ref, nm_ref, nv_ref):
        g = g_ref[...]
        m2 = ADAM_B1 * m_ref[...] + (1.0 - ADAM_B1) * g
        v2 = ADAM_B2 * v_ref[...] + (1.0 - ADAM_B2) * (g * g)
        m_hat = m2 / (1.0 - ADAM_B1 ** ADAM_STEP)
        v_hat = v2 / (1.0 - ADAM_B2 ** ADAM_STEP)
        d_ref[...] = -ADAM_LR * (m_hat / (jnp.sqrt(v_hat) + ADAM_EPS) + ADAM_WD * w_ref[...])
        nm_ref[...] = m2
        nv_ref[...] = v2

    spec = pl.BlockSpec((t, cols), lambda i: (i, 0))
    shp = jax.ShapeDtypeStruct(w.shape, F32)
    return pl.pallas_call(body, name=name, grid=(rows // t,), in_specs=[spec] * 4, out_specs=[spec] * 3,
                          out_shape=[shp] * 3, compiler_params=_params())(w, g, m, v)


def _place():
    x, y, c = lax.axis_index("x"), lax.axis_index("y"), lax.axis_index("c")
    chips = [(1 - x, y), (x, 1 - y), (1 - x, 1 - y)]
    return x, y, c, chips


class _Sharded:
    def __init__(self, kind, full_shape):
        self.kind = kind
        self.full_shape = full_shape
        self.chunk_axis = 1 if kind == "mid" else 0

    def in_full(self, ref, s, h):
        if self.kind == "cols":
            r, n = self.full_shape
            return ref.at[pl.ds(h * (r // 2), r // 2), pl.ds(pl.multiple_of(s * (n // 4), 128), n // 4)]
        if self.kind == "rows":
            r, _ = self.full_shape
            return ref.at[pl.ds(pl.multiple_of(s * (r // 4) + h * (r // 8), 8), r // 8), :]
        g, r, _ = self.full_shape
        return ref.at[pl.ds(h * (g // 2), g // 2), pl.ds(pl.multiple_of(s * (r // 4), 16), r // 4), :]


CHUNK_BYTES = 256 * 1024
MAX_CHUNKS = 32


def _cut(ref, axis, n):
    step = ref.shape[axis] // n
    lead = (slice(None),) * axis
    return [ref.at[lead + (pl.ds(k * step, step),)] for k in range(n)]


def _n_chunks(ref, axis):
    nbytes = int(np.prod(ref.shape)) * jnp.dtype(ref.dtype).itemsize
    n = 1
    while n < MAX_CHUNKS and nbytes // (2 * n) >= CHUNK_BYTES and ref.shape[axis] % (32 * n) == 0:
        n *= 2
    return n


def _remote(src, dst, send_sem, recv_sem, to):
    return pltpu.make_async_remote_copy(src_ref=src, dst_ref=dst, send_sem=send_sem, recv_sem=recv_sem,
                                        device_id=to, device_id_type=MESH)


def _start_remote(src, dst, axis, send_sem, recv_sem, to):
    n = _n_chunks(dst, axis)
    for s_k, d_k in zip(_cut(src, axis, n), _cut(dst, axis, n)):
        _remote(s_k, d_k, send_sem, recv_sem, to).start()
    return _remote(src, dst, send_sem, recv_sem, to)


def _start_local(src, dst, axis, sem):
    n = _n_chunks(dst, axis)
    for s_k, d_k in zip(_cut(src, axis, n), _cut(dst, axis, n)):
        pltpu.make_async_copy(s_k, d_k, sem).start()
    return pltpu.make_async_copy(src, dst, sem)


def _gather_weights(shards, cuts):
    n = len(shards)

    def body(*refs):
        src = refs[:n]
        out = refs[n:2 * n]
        send_sems, recv_sems = refs[2 * n:]
        x, y, c, chips = _place()
        me = 2 * x + y
        sibling = (x, y, 1 - c)

        def ends(w, s, h, from_shard):
            dst = cuts[w].in_full(out[w], s, h)
            return (cuts[w].in_full(src[w], s, h) if from_shard else dst), dst

        def remote(k, w, s, h, to, from_shard):
            return _remote(*ends(w, s, h, from_shard), send_sems.at[k], recv_sems.at[k], to)

        def start(k, w, s, h, to, from_shard):
            return _start_remote(*ends(w, s, h, from_shard), cuts[w].chunk_axis, send_sems.at[k], recv_sems.at[k], to)

        first = []
        for w in range(n):
            for j, chip in enumerate(chips):
                first.append(start(3 * w + j, w, me, c, (*chip, c), True))
        passed = []
        for w in range(n):
            for j, chip in enumerate(chips):
                s = 2 * chip[0] + chip[1]
                remote(3 * w + j, w, s, c, (x, y, c), False).wait_recv()
                passed.append(start(3 * n + 3 * w + j, w, s, c, sibling, False))
        for w in range(n):
            for j, chip in enumerate(chips):
                s = 2 * chip[0] + chip[1]
                remote(3 * n + 3 * w + j, w, s, 1 - c, (x, y, c), False).wait_recv()
        for cp in first + passed:
            cp.wait_send()

    return pl.pallas_call(
        body, name="gather_weights",
        in_specs=[ANY] * n, out_specs=[ANY] * n,
        out_shape=[jax.ShapeDtypeStruct(cuts[w].full_shape, BF16) for w in range(n)],
        input_output_aliases={w: w for w in range(n)},
        scratch_shapes=[pltpu.SemaphoreType.DMA((6 * n,)), pltpu.SemaphoreType.DMA((6 * n,))],
        compiler_params=pltpu.CompilerParams(has_side_effects=True),
    )(*shards)


def _exchange_halves(grads, name):
    n = len(grads)

    def body(*refs):
        g = refs[:n]
        theirs = refs[n:2 * n]
        send_sems, recv_sems = refs[2 * n:]
        x, y, c, _ = _place()
        sends = [_start_remote(g[w].at[:, 1 - c], theirs[w], 1, send_sems.at[w], recv_sems.at[w], (x, y, 1 - c))
                 for w in range(n)]
        for w in range(n):
            _remote(g[w].at[:, 1 - c], theirs[w], send_sems.at[w], recv_sems.at[w], (x, y, c)).wait_recv()
        for cp in sends:
            cp.wait_send()

    return pl.pallas_call(
        body, name=name,
        in_specs=[ANY] * n, out_specs=[ANY] * n,
        out_shape=[jax.ShapeDtypeStruct((4,) + g.shape[2:], F32) for g in grads],
        scratch_shapes=[pltpu.SemaphoreType.DMA((n,)), pltpu.SemaphoreType.DMA((n,))],
        compiler_params=pltpu.CompilerParams(has_side_effects=True),
    )(*grads)


class _ScatterRider:
    def __init__(self, parts):
        n = len(parts)
        self.parts = list(parts)
        self.in_specs = [ANY] * n
        self.out_specs = [ANY] * n
        self.out_shape = [jax.ShapeDtypeStruct(a.shape, a.dtype) for a in parts]
        self.scratch = [pltpu.SemaphoreType.DMA((3 * n,)), pltpu.SemaphoreType.DMA((3 * n,))]

    def _copies(self, p, out, send_sems, recv_sems, start):
        x, y, c, chips = _place()
        me = 2 * x + y
        for w in range(len(self.parts)):
            for j, chip in enumerate(chips):
                s = 2 * chip[0] + chip[1]
                if start:
                    _start_remote(p[w].at[s], out[w].at[me], 0, send_sems.at[3 * w + j], recv_sems.at[3 * w + j],
                                  (*chip, c))
                else:
                    cp = _remote(p[w].at[s], out[w].at[s], send_sems.at[3 * w + j], recv_sems.at[3 * w + j],
                                 (x, y, c))
                    cp.wait_recv()
                    cp.wait_send()

    def carried_by(self, body, n_in, n_out, n_tiles):
        k = len(self.parts)

        def carrier(*refs):
            ins, mine = refs[:n_in], refs[n_in:n_in + k]
            outs, slots = refs[n_in + k:n_in + k + n_out], refs[n_in + k + n_out:n_in + 2 * k + n_out]
            sems = refs[n_in + 2 * k + n_out:]

            @pl.when(pl.program_id(0) == 0)
            def _():
                self._copies(mine, slots, *sems, start=True)

            body(*ins, *outs)

            @pl.when(pl.program_id(0) == n_tiles - 1)
            def _():
                self._copies(mine, slots, *sems, start=False)

        return carrier


def _share_halves(halves, small):
    n = len(halves)
    flips = [(fx, fy, fc) for fx in range(2) for fy in range(2) for fc in range(2)][1:]

    def body(*refs):
        hv = refs[:n]
        small_ref = refs[n]
        out = refs[n + 1:2 * n + 1]
        gathered = refs[2 * n + 1]
        send_sems, recv_sems, local_sem = refs[2 * n + 2:]
        x, y, c, _ = _place()
        me = 4 * x + 2 * y + c
        local = _start_local(small_ref, gathered.at[me], 0, local_sem)
        sends = [_start_remote(hv[w], out[w], 0, send_sems.at[w], recv_sems.at[w], (x, y, 1 - c)) for w in range(n)]
        for k, (fx, fy, fc) in enumerate(flips):
            sends.append(_start_remote(
                small_ref, gathered.at[me], 0, send_sems.at[n + k], recv_sems.at[n + k],
                (x + fx - 2 * x * fx, y + fy - 2 * y * fy, c + fc - 2 * c * fc)))
        for w in range(n):
            _remote(hv[w], out[w], send_sems.at[w], recv_sems.at[w], (x, y, c)).wait_recv()
        for k, (fx, fy, fc) in enumerate(flips):
            peer = 4 * (x + fx - 2 * x * fx) + 2 * (y + fy - 2 * y * fy) + (c + fc - 2 * c * fc)
            _remote(small_ref, gathered.at[peer], send_sems.at[n + k], recv_sems.at[n + k], (x, y, c)).wait_recv()
        for cp in sends:
            cp.wait_send()
        local.wait()

    return pl.pallas_call(
        body, name="share_halves",
        in_specs=[ANY] * (n + 1), out_specs=[ANY] * (n + 1),
        out_shape=[jax.ShapeDtypeStruct(a.shape, F32) for a in halves] + [jax.ShapeDtypeStruct((8,) + small.shape, F32)],
        scratch_shapes=[pltpu.SemaphoreType.DMA((n + 7,)), pltpu.SemaphoreType.DMA((n + 7,)),
                        pltpu.SemaphoreType.DMA],
        compiler_params=pltpu.CompilerParams(has_side_effects=True),
    )(*halves, small)


SMALL_ROWS = 80


def _pack_small(vecs, ws, bs, sink):
    top = jnp.concatenate(
        [v.reshape(1, 1024) for v in vecs]
        + [jnp.pad(bs.reshape(1, 512), ((0, 0), (0, 512))), jnp.pad(sink.reshape(1, 16), ((0, 0), (0, 1008))),
           jnp.zeros((1, 1024), F32)], axis=0)
    return jnp.concatenate([top, ws.reshape(64, 1024), jnp.zeros((8, 1024), F32)], axis=0)


def _unpack_small(p):
    vecs = [p[k] for k in range(5)]
    return vecs, p[8:72].reshape(4, 128, 128), p[5, :512].reshape(4, 128), p[6, :16]


def kernel(x, norm_0, w_in_0, a_v_norm_0, a_spatial_w_0, a_spatial_b_0, b_group_w_0, b_scale_0, w_out_0, norm_1, w_in_1, sink_1, w_out_1, final_norm, loss_target, m_norm_0, m_w_in_0, m_a_v_norm_0, m_a_spatial_w_0, m_a_spatial_b_0, m_b_group_w_0, m_b_scale_0, m_w_out_0, m_norm_1, m_w_in_1, m_sink_1, m_w_out_1, m_final_norm, v_norm_0, v_w_in_0, v_a_v_norm_0, v_a_spatial_w_0, v_a_spatial_b_0, v_b_group_w_0, v_b_scale_0, v_w_out_0, v_norm_1, v_w_in_1, v_sink_1, v_w_out_1, v_final_norm):
    s = x.shape[1]
    xs = x.reshape(s, D_MODEL)
    target = loss_target.reshape(s, D_MODEL)

    cuts = [_Sharded("cols", (1024, 5120)), _Sharded("rows", (2048, 1024)), _Sharded("cols", (1024, 2560)),
            _Sharded("rows", (1024, 1024)), _Sharded("mid", (4, 256, 256))]
    big_w = [w_in_0, w_out_0, w_in_1, w_out_1, b_group_w_0]
    big_m = [m_w_in_0, m_w_out_0, m_w_in_1, m_w_out_1, m_b_group_w_0]
    big_v = [v_w_in_0, v_w_out_0, v_w_in_1, v_w_out_1, v_b_group_w_0]
    where = jnp.stack([2 * lax.axis_index("x") + lax.axis_index("y"), lax.axis_index("c")]).astype(jnp.int32)
    w_in0, w_out0, w_in1, w_out1, wg = _gather_weights(
        [_place_shard(where, w, cut, f"place_shard{k}") for k, (w, cut) in enumerate(zip(big_w, cuts))], cuts)

    row = lambda v: v.reshape(1, 1024)
    ws16 = a_spatial_w_0.astype(BF16)
    bsx = jnp.repeat(a_spatial_b_0.T, 256, axis=1)
    band, band_t = _band_matrices(TOK)
    rope = _rope_tables(s)

    h0, z0 = _in_proj0(xs, row(norm_0), w_in0)
    cat, x1 = _mix0_fwd(xs, z0, w_out0, row(a_v_norm_0), ws16, bsx, wg, row(b_scale_0), band)
    h1, q, k, v, gate = _in_proj1(x1, row(norm_1), w_in1, rope)
    kpad = jnp.pad(k, ((ATTN_WINDOW, ATTN_WINDOW), (0, 0)))
    vpad = jnp.pad(v, ((ATTN_WINDOW, ATTN_WINDOW), (0, 0)))
    o = _attn_fwd(q, kpad, vpad, sink_1)
    y1, dx2, do, dgate, loss_lanes, g_final, dx2h = _tail(x1, o, gate, target, w_out1, row(final_norm))

    dq, dkpad, dvpad, dsink = _attn_bwd(q, kpad, vpad, sink_1, o, do, rope)
    dk = dkpad[:, ATTN_WINDOW:ATTN_WINDOW + s].T
    dv = dvpad[:, ATTN_WINDOW:ATTN_WINDOW + s].T
    dz1, dx1, g_norm1, dx1h = _in_proj1_bwd(dq, dk, dv, dgate, x1, dx2, row(norm_1), w_in1, rope)

    g_w_in1 = _weight_grad(h1, dz1, 2, 2, "grad_w_in1").reshape(4, 2, 512, 640)
    g_w_out1 = _weight_grad(y1, dx2h, 1, 1, "grad_w_out1").reshape(4, 2, 128, 1024)
    g_w_out0 = _weight_grad(cat, dx1h, 1, 1, "grad_w_out0").reshape(4, 2, 256, 1024)
    first = [g_w_out0, g_w_in1, g_w_out1]
    theirs1 = _exchange_halves(first, "exchange_halves1")
    parts1 = [_add_sibling(where, g, t, f"add_sibling1_{k}") for k, (g, t) in enumerate(zip(first, theirs1))]
    dz0, dpn, d_ws, d_bs, d_gv, d_scale, d_wg, *slots1 = _mix0_bwd(
        dx1h, z0, w_out0, row(a_v_norm_0), ws16, bsx, wg, row(b_scale_0), band, _ScatterRider(parts1))
    dz0 = _fill_pooled_grad(dz0, dpn, band_t)
    g_w_in0 = _weight_grad(h0, dz0, 4, 1, "grad_w_in0").reshape(4, 2, 512, 1280)
    g_wg = d_wg.reshape(2, 2, 4, 64, 256).transpose(2, 0, 1, 3, 4).reshape(4, 2, 128, 256)
    second = [g_w_in0, g_wg]
    theirs2 = _exchange_halves(second, "exchange_halves2")
    parts2 = [_add_sibling(where, g, t, f"add_sibling2_{k}") for k, (g, t) in enumerate(zip(second, theirs2))]
    grad_x, g_norm0, *slots2 = _in_proj0_bwd(dz0, xs, dx1, row(norm_0), w_in0, _ScatterRider(parts2))

    grads = [g_w_in0, g_w_out0, g_w_in1, g_w_out1, g_wg]
    theirs = [theirs2[0], theirs1[0], theirs1[1], theirs1[2], theirs2[1]]
    slots = [slots2[0], slots1[0], slots1[1], slots1[2], slots2[1]]
    n = len(grads)
    reduced = [_sum_chips(where, grads[w], theirs[w], slots[w], f"sum_chips{w}") for w in range(n)]
    small = _pack_small([g_norm0, d_gv, d_scale, g_norm1, g_final], d_ws, d_bs[:, :4].T, dsink[0, :16])
    *from_sibling, small_all = _share_halves(reduced, small)

    out_g, out_d, out_m, out_v = {}, {}, {}, {}
    names = ["w_in_0", "w_out_0", "w_in_1", "w_out_1", "b_group_w_0"]
    for w in range(n):
        shape = big_w[w].shape
        two_d = (-1, shape[-1])
        outs = _adamw_halves(where, big_w[w].reshape(two_d), reduced[w], from_sibling[w], big_m[w].reshape(two_d),
                             big_v[w].reshape(two_d), f"adamw{w}")
        out_g[names[w]], out_d[names[w]], out_m[names[w]], out_v[names[w]] = (a.reshape(shape) for a in outs)

    g_small = _sum_slots(small_all, "sum_small")
    small_names = ["norm_0", "a_v_norm_0", "b_scale_0", "norm_1", "final_norm"]
    pack = lambda vecs, ws_, bs_, sk: _pack_small(vecs, ws_, bs_, sk)
    w_small = pack([norm_0, a_v_norm_0, b_scale_0, norm_1, final_norm], a_spatial_w_0, a_spatial_b_0, sink_1)
    m_small = pack([m_norm_0, m_a_v_norm_0, m_b_scale_0, m_norm_1, m_final_norm], m_a_spatial_w_0,
                   m_a_spatial_b_0, m_sink_1)
    v_small = pack([v_norm_0, v_a_v_norm_0, v_b_scale_0, v_norm_1, v_final_norm], v_a_spatial_w_0,
                   v_a_spatial_b_0, v_sink_1)
    d_small, nm_small, nv_small = _adamw(w_small, g_small, m_small, v_small, "adamw_small")
    for store, packed in ((out_g, g_small), (out_d, d_small), (out_m, nm_small), (out_v, nv_small)):
        vecs, ws_, bs_, sk = _unpack_small(packed)
        for name, vec in zip(small_names, vecs):
            store[name] = vec
        store["a_spatial_w_0"], store["a_spatial_b_0"], store["sink_1"] = ws_, bs_, sk

    loss = lax.psum(jnp.sum(loss_lanes), ("x", "y", "c"))
    order = ["norm_0", "w_in_0", "a_v_norm_0", "a_spatial_w_0", "a_spatial_b_0", "b_group_w_0", "b_scale_0",
             "w_out_0", "norm_1", "w_in_1", "sink_1", "w_out_1", "final_norm"]
    return (loss, grad_x.reshape(1, s, D_MODEL), *[out_g[k] for k in order], *[out_d[k] for k in order],
            *[out_m[k] for k in order], *[out_v[k] for k in order])
```

```python
import functools

import numpy as np
import jax
import jax.numpy as jnp
from jax import lax
from jax.experimental import pallas as pl
from jax.experimental.pallas import tpu as pltpu

F32 = jnp.float32
BF16 = jnp.bfloat16
MESH = pl.DeviceIdType.MESH

D_MODEL = 1024
EPS = 1e-6
NEG_INF = -1e30
CHUNK = 128
POOL_WINDOWS = (2, 4, 8, 16)
HALO = 16
N_HEADS = 16
HEAD_DIM = 64
ATTN_WINDOW = 128
ROPE_THETA = 500000.0
ROT_DIM = 16
ADAM_LR = 0.001
ADAM_B1 = 0.9
ADAM_B2 = 0.999
ADAM_EPS = 1e-08
ADAM_WD = 0.01
ADAM_STEP = 10

TOK = 256
VMEM_LIMIT = 56 * 1024 * 1024


def _params(**kw):
    return pltpu.CompilerParams(vmem_limit_bytes=VMEM_LIMIT, **kw)


def _whole(shape):
    nd = len(shape)
    return pl.BlockSpec(shape, lambda *_: (0,) * nd)


def _rows(t, n):
    return pl.BlockSpec((t, n), lambda i: (i, 0))


ANY = pl.BlockSpec(memory_space=pl.ANY)

_G0 = 0.7978845608028654
_G1 = 0.044715


def _gelu(x):
    return 0.5 * x * (1.0 + jnp.tanh(_G0 * (x + _G1 * x * x * x)))


def _gelu_and_grad(x):
    x2 = x * x
    t = jnp.tanh(_G0 * (x + _G1 * x2 * x))
    half = 0.5 * (1.0 + t)
    return x * half, half + 0.5 * x * (1.0 - t * t) * (_G0 * (1.0 + 3.0 * _G1 * x2))


def _sigmoid(x):
    return 1.0 / (1.0 + jnp.exp(-x))


def _dot(a, b):
    return jnp.dot(a, b, preferred_element_type=F32)


def _dot_nt(a, b):
    return lax.dot_general(a, b, (((1,), (1,)), ((), ())), preferred_element_type=F32)


def _dot_tn(a, b):
    return lax.dot_general(a, b, (((0,), (0,)), ((), ())), preferred_element_type=F32)


def _rms_fwd(x, g):
    r = lax.rsqrt(jnp.mean(x * x, axis=-1, keepdims=True) + EPS)
    xh = x * r
    return r, xh, xh * g


def _rms_bwd(dy, g, r, xh):
    dxh = dy * g
    return r * (dxh - xh * jnp.mean(dxh * xh, axis=-1, keepdims=True))


def _band_matrices(t):
    r = np.arange(t)[:, None]
    j = np.arange(t + 2 * HALO)[None, :]
    fwd, bwd = [], []
    for w in POOL_WINDOWS:
        d = j - r - HALO
        fwd.append((d >= -(w // 2)) & (d < w // 2))
        bwd.append((d >= -(w // 2) + 1) & (d <= w // 2))
    return (jnp.asarray(np.stack(fwd), BF16), jnp.asarray(np.stack(bwd), BF16))


def _window_counts(i, t, s):
    tok = i * t + lax.broadcasted_iota(jnp.int32, (t, 1), 0)
    out = []
    for w in POOL_WINDOWS:
        cnt = jnp.minimum(tok + w // 2, s) - jnp.maximum(tok - w // 2, 0)
        out.append(cnt.astype(F32))
    return out


def _rope_tables(s):
    inv = np.float32(ROPE_THETA) ** (-np.arange(0, ROT_DIM, 2, dtype=np.float32) / np.float32(ROT_DIM))
    ang = np.arange(s, dtype=np.float32)[:, None] * inv.astype(np.float32)[None, :]
    cos, sin = np.cos(ang).astype(np.float32), np.sin(ang).astype(np.float32)
    z8 = np.zeros((s, 8), np.float32)
    z48 = np.zeros((s, HEAD_DIM - ROT_DIM), np.float32)
    c = np.concatenate([cos, cos, np.ones((s, HEAD_DIM - ROT_DIM), np.float32)], axis=1)
    s_lo = np.concatenate([z8, sin, z48], axis=1)
    s_hi = np.concatenate([-sin, z8, z48], axis=1)
    return tuple(jnp.asarray(np.concatenate([a, a], axis=1)) for a in (c, s_lo, s_hi))


def _rope(x, c, s_lo, s_hi):
    n = x.shape[1]
    reps = n // 128
    c, s_lo, s_hi = (jnp.tile(a, (1, reps)) for a in (c, s_lo, s_hi))
    return x * c + pltpu.roll(x, 8, 1) * s_lo + pltpu.roll(x, n - 8, 1) * s_hi


def _rope_t(dx, c, s_lo, s_hi):
    n = dx.shape[1]
    reps = n // 128
    c, s_lo, s_hi = (jnp.tile(a, (1, reps)) for a in (c, s_lo, s_hi))
    return dx * c + pltpu.roll(dx * s_lo, n - 8, 1) + pltpu.roll(dx * s_hi, 8, 1)


def _in_proj0(x, g0, w_in0, rider):
    s = x.shape[0]
    n = w_in0.shape[1]
    k = len(rider.fulls)

    def body(x_ref, g_ref, w_ref, h_ref, z_ref):
        _, _, h = _rms_fwd(x_ref[...], g_ref[...])
        h = h.astype(BF16)
        h_ref[...] = h
        for j in range(n // 1024):
            z_ref[:, j * 1024:(j + 1) * 1024] = _dot(h, w_ref[:, j * 1024:(j + 1) * 1024]).astype(BF16)

    return pl.pallas_call(
        rider.carried_by(body, 3, 2, s // TOK), name="in_proj0", grid=(s // TOK,),
        in_specs=[_rows(TOK, D_MODEL), _whole((1, D_MODEL)), _whole(w_in0.shape)] + rider.in_specs,
        out_specs=[_rows(TOK, D_MODEL), _rows(TOK, n)] + rider.out_specs,
        out_shape=[jax.ShapeDtypeStruct((s, D_MODEL), BF16), jax.ShapeDtypeStruct((s, n), BF16)] + rider.out_shape,
        input_output_aliases={3 + j: 2 + j for j in range(k)},
        scratch_shapes=rider.scratch,
        compiler_params=_params(),
    )(x, g0, w_in0, *rider.fulls)


def _halo_specs(s, col_block):
    per = TOK // HALO
    last = s // HALO - 1
    prev = pl.BlockSpec((HALO, 1024), lambda i: (jnp.maximum(i * per - 1, 0), col_block))
    nxt = pl.BlockSpec((HALO, 1024), lambda i: (jnp.minimum((i + 1) * per, last), col_block))
    return prev, nxt


def _with_halo(i, n_tiles, prev_ref, cur, next_ref):
    prev = prev_ref[...]
    nxt = next_ref[...]
    prev = jnp.where(i > 0, prev, jnp.zeros_like(prev))
    nxt = jnp.where(i < n_tiles - 1, nxt, jnp.zeros_like(nxt))
    return jnp.concatenate([prev, cur, nxt], axis=0)


def _mixer_a(v1, gv, ws_ref, bsx):
    rv, vh, v2 = _rms_fwd(v1, gv)
    v2 = v2.astype(BF16)
    rows = []
    for c in range(v1.shape[0] // CHUNK):
        cols = [_dot(ws_ref[h], v2[c * CHUNK:(c + 1) * CHUNK, h * 256:(h + 1) * 256]) for h in range(4)]
        rows.append(jnp.concatenate(cols, axis=1) + bsx)
    return rv, vh, v2, jnp.concatenate(rows, axis=0)


def _mixer_b_pooled(bx, halo, band_ref, counts):
    out = []
    for g in range(4):
        win = _dot(band_ref[g], halo[:, g * 256:(g + 1) * 256])
        out.append(win / counts[g] - bx[:, g * 256:(g + 1) * 256])
    return out


def _mix0_fwd(x, z0, w_out0, gv, ws, bsx, wg, scale, band):
    s = x.shape[0]
    n_tiles = s // TOK

    def body(z_ref, zp_ref, zn_ref, x_ref, wout_ref, gv_ref, ws_ref, bsx_ref, wg_ref, sc_ref, band_ref,
             cat_ref, x1_ref):
        i = pl.program_id(0)
        au = z_ref[:, 0:1024].astype(F32)
        av = z_ref[:, 1024:2048].astype(F32)
        ag = z_ref[:, 2048:3072].astype(F32)
        _, _, _, mixed = _mixer_a(_gelu(av), gv_ref[...], ws_ref, bsx_ref[...])
        cat_ref[:, 0:1024] = (_gelu(au) * mixed * (ag * _sigmoid(ag))).astype(BF16)

        bx16 = z_ref[:, 3072:4096]
        bg = z_ref[:, 4096:5120].astype(F32)
        halo = _with_halo(i, n_tiles, zp_ref, bx16, zn_ref)
        ps = _mixer_b_pooled(bx16.astype(F32), halo, band_ref, _window_counts(i, TOK, s))
        pw = jnp.concatenate([_dot(ps[g].astype(BF16), wg_ref[g]) for g in range(4)], axis=1)
        cat_ref[:, 1024:2048] = (pw * sc_ref[...] * (bg * _sigmoid(bg))).astype(BF16)

        x1_ref[...] = x_ref[...] + _dot(cat_ref[...], wout_ref[...])

    prev, nxt = _halo_specs(s, 3)
    return pl.pallas_call(
        body, name="mix0_fwd", grid=(n_tiles,),
        in_specs=[_rows(TOK, 5120), prev, nxt, _rows(TOK, D_MODEL), _whole(w_out0.shape), _whole((1, 1024)),
                  _whole(ws.shape), _whole(bsx.shape), _whole(wg.shape), _whole((1, 1024)), _whole(band.shape)],
        out_specs=[_rows(TOK, 2048), _rows(TOK, D_MODEL)],
        out_shape=[jax.ShapeDtypeStruct((s, 2048), BF16), jax.ShapeDtypeStruct((s, D_MODEL), F32)],
        compiler_params=_params(),
    )(z0, z0, z0, x, w_out0, gv, ws, bsx, wg, scale, band)


def _in_proj1(x1, g1, w_in1, rope):
    s = x1.shape[0]

    def body(x_ref, g_ref, w_ref, c_ref, lo_ref, hi_ref, h_ref, q_ref, k_ref, v_ref, gate_ref):
        _, _, h = _rms_fwd(x_ref[...], g_ref[...])
        h = h.astype(BF16)
        h_ref[...] = h
        tabs = (c_ref[...], lo_ref[...], hi_ref[...])
        q_ref[...] = (_rope(_dot(h, w_ref[:, 0:1024]), *tabs) * Q_SCALE).astype(BF16)
        kv = _dot(h, w_ref[:, 1024:1536])
        k_ref[...] = _rope(kv[:, 0:256], *tabs).astype(BF16)
        v_ref[...] = kv[:, 256:512].astype(BF16)
        gate_ref[...] = _dot(h, w_ref[:, 1536:2560]).astype(BF16)

    tab = _rows(TOK, 128)
    return pl.pallas_call(
        body, name="in_proj1", grid=(s // TOK,),
        in_specs=[_rows(TOK, D_MODEL), _whole((1, D_MODEL)), _whole(w_in1.shape), tab, tab, tab],
        out_specs=[_rows(TOK, 1024), _rows(TOK, 1024), _rows(TOK, 256), _rows(TOK, 256), _rows(TOK, 1024)],
        out_shape=[jax.ShapeDtypeStruct((s, 1024), BF16), jax.ShapeDtypeStruct((s, 1024), BF16),
                   jax.ShapeDtypeStruct((s, 256), BF16), jax.ShapeDtypeStruct((s, 256), BF16),
                   jax.ShapeDtypeStruct((s, 1024), BF16)],
        compiler_params=_params(),
    )(x1, g1, w_in1, *rope)


QBLK = 128
KBLK = QBLK + 2 * ATTN_WINDOW
Q_SCALE = HEAD_DIM ** -0.5
HEAD_BATCH = 4


def _block_bias(q0, s):
    r = lax.broadcasted_iota(jnp.int32, (QBLK, KBLK), 0)
    c = lax.broadcasted_iota(jnp.int32, (QBLK, KBLK), 1)
    kj = q0 - ATTN_WINDOW + c
    ok = (c >= r) & (c <= r + 2 * ATTN_WINDOW) & (kj >= 0) & (kj < s)
    return jnp.where(ok, 0.0, NEG_INF)


def _attn_fwd(q, kpad, vpad, sink):
    s = q.shape[0]

    def body(sink_ref, q_ref, k_ref, v_ref, o_ref):
        i = pl.program_id(0)
        for b in range(TOK // QBLK):
            rows = slice(b * QBLK, (b + 1) * QBLK)
            start = pl.multiple_of(i * TOK + b * QBLK, QBLK)
            kb = k_ref[pl.ds(start, KBLK), :]
            vb = v_ref[pl.ds(start, KBLK), :]
            bias = _block_bias(i * TOK + b * QBLK, s)
            kv = lambda t, h: t[:, (h // 4) * HEAD_DIM:(h // 4 + 1) * HEAD_DIM]
            scs = [_dot_nt(q_ref[rows, h * HEAD_DIM:(h + 1) * HEAD_DIM], kv(kb, h)) + bias for h in range(N_HEADS)]
            ms = [jnp.maximum(jnp.max(scs[h], axis=-1, keepdims=True), sink_ref[h]) for h in range(N_HEADS)]
            es = [jnp.exp(scs[h] - ms[h]) for h in range(N_HEADS)]
            rdens = [1.0 / (jnp.sum(es[h], axis=-1, keepdims=True) + jnp.exp(sink_ref[h] - ms[h]))
                     for h in range(N_HEADS)]
            outs = [_dot(es[h].astype(BF16), kv(vb, h)) * rdens[h] for h in range(N_HEADS)]
            o_ref[rows, :] = jnp.concatenate(outs, axis=1).astype(BF16)

    return pl.pallas_call(
        body, name="attn_fwd", grid=(s // TOK,),
        in_specs=[pl.BlockSpec(memory_space=pltpu.SMEM), _rows(TOK, 1024), _whole(kpad.shape), _whole(vpad.shape)],
        out_specs=_rows(TOK, 1024),
        out_shape=jax.ShapeDtypeStruct((s, 1024), BF16),
        compiler_params=_params(),
    )(sink, q, kpad, vpad)


def _tail(x1, o, gate, target, w_out1, gf):
    s = x1.shape[0]

    def body(x1_ref, o_ref, gate_ref, t_ref, w_ref, gf_ref, y1_ref, dx2_ref, do_ref, dgate_ref, loss_ref, gfn_ref,
             dx2h_ref):
        i = pl.program_id(0)

        @pl.when(i == 0)
        def _():
            loss_ref[...] = jnp.zeros_like(loss_ref)
            gfn_ref[...] = jnp.zeros_like(gfn_ref)

        g = gate_ref[...].astype(F32)
        sg = _sigmoid(g)
        sil = g * sg
        o = o_ref[...].astype(F32)
        y1 = (o * sil).astype(BF16)
        y1_ref[...] = y1
        x2 = x1_ref[...] + _dot(y1, w_ref[...])
        gf = gf_ref[...]
        r, xh, out = _rms_fwd(x2, gf)
        diff = out - t_ref[...]
        loss_ref[...] += jnp.sum(diff * diff, axis=0, keepdims=True) * (0.5 / D_MODEL)
        dout = diff * (1.0 / D_MODEL)
        gfn_ref[...] += jnp.sum(dout * xh, axis=0, keepdims=True)
        dx2 = _rms_bwd(dout, gf, r, xh)
        dx2_ref[...] = dx2
        dx2h = dx2.astype(BF16)
        dx2h_ref[...] = dx2h
        dy1 = _dot_nt(dx2h, w_ref[...])
        do_ref[...] = (dy1 * sil).astype(BF16)
        dgate_ref[...] = (dy1 * o * (sg * (1.0 + g * (1.0 - sg)))).astype(BF16)

    row = _rows(TOK, 1024)
    acc = _whole((1, 1024))
    return pl.pallas_call(
        body, name="tail", grid=(s // TOK,),
        in_specs=[row, row, row, row, _whole(w_out1.shape), acc],
        out_specs=[row, row, row, row, acc, acc, row],
        out_shape=[jax.ShapeDtypeStruct((s, 1024), BF16), jax.ShapeDtypeStruct((s, 1024), F32),
                   jax.ShapeDtypeStruct((s, 1024), BF16), jax.ShapeDtypeStruct((s, 1024), BF16),
                   jax.ShapeDtypeStruct((1, 1024), F32), jax.ShapeDtypeStruct((1, 1024), F32),
                   jax.ShapeDtypeStruct((s, 1024), BF16)],
        compiler_params=_params(),
    )(x1, o, gate, target, w_out1, gf)


def _attn_bwd(q, kpad, vpad, sink, o, do, rope):
    s = q.shape[0]
    pad_t = (kpad.shape[1], kpad.shape[0])

    def body(sink_ref, q_ref, k_ref, v_ref, o_ref, do_ref, c_ref, lo_ref, hi_ref, dq_ref, dk_ref, dv_ref, ds_ref):
        i = pl.program_id(0)

        @pl.when(i == 0)
        def _():
            dk_ref[...] = jnp.zeros_like(dk_ref)
            dv_ref[...] = jnp.zeros_like(dv_ref)
            ds_ref[...] = jnp.zeros_like(ds_ref)

        lane = lax.broadcasted_iota(jnp.int32, (1, 128), 1)
        dsink = jnp.zeros((1, 128), F32)
        for b in range(TOK // QBLK):
            rows = slice(b * QBLK, (b + 1) * QBLK)
            start = pl.multiple_of(i * TOK + b * QBLK, QBLK)
            kb = k_ref[pl.ds(start, KBLK), :]
            vb = v_ref[pl.ds(start, KBLK), :]
            bias = _block_bias(i * TOK + b * QBLK, s)
            kv = lambda t, h: t[:, (h // 4) * HEAD_DIM:(h // 4 + 1) * HEAD_DIM]
            hd = lambda ref, h: ref[rows, h * HEAD_DIM:(h + 1) * HEAD_DIM]
            dqs, dks, dvs = {}, {}, {}
            for h0 in range(0, N_HEADS, HEAD_BATCH):
                heads = range(h0, h0 + HEAD_BATCH)
                scs = {h: _dot_nt(hd(q_ref, h), kv(kb, h)) + bias for h in heads}
                ms = {h: jnp.maximum(jnp.max(scs[h], axis=-1, keepdims=True), sink_ref[h]) for h in heads}
                es = {h: jnp.exp(scs[h] - ms[h]) for h in heads}
                e_sinks = {h: jnp.exp(sink_ref[h] - ms[h]) for h in heads}
                rdens = {h: 1.0 / (jnp.sum(es[h], axis=-1, keepdims=True) + e_sinks[h]) for h in heads}
                dos = {h: hd(do_ref, h).astype(F32) for h in heads}
                deltas = {h: jnp.sum(dos[h] * hd(o_ref, h).astype(F32), axis=-1, keepdims=True) * rdens[h]
                          for h in heads}
                do_ns = {h: (dos[h] * rdens[h]).astype(BF16) for h in heads}
                for h in heads:
                    dsink = dsink + jnp.where(
                        lane == h, -jnp.sum(e_sinks[h] * deltas[h], axis=0, keepdims=True), 0.0)
                dscs = {h: (es[h] * (_dot_nt(do_ns[h], kv(vb, h)) - deltas[h])).astype(BF16) for h in heads}
                dvs.update({h: _dot_tn(do_ns[h], es[h].astype(BF16)) for h in heads})
                dks.update({h: _dot_tn(hd(q_ref, h), dscs[h]) for h in heads})
                dqs.update({h: _dot(dscs[h], kv(kb, h)) * Q_SCALE for h in heads})
            dq = jnp.concatenate([dqs[h] for h in range(N_HEADS)], axis=1)
            dq_ref[rows, :] = _rope_t(dq, c_ref[rows, :], lo_ref[rows, :], hi_ref[rows, :]).astype(BF16)
            group_sum = lambda parts, g: (parts[4 * g] + parts[4 * g + 1]) + (parts[4 * g + 2] + parts[4 * g + 3])
            dk_ref[:, pl.ds(start, KBLK)] += jnp.concatenate([group_sum(dks, g) for g in range(4)], axis=0)
            dv_ref[:, pl.ds(start, KBLK)] += jnp.concatenate([group_sum(dvs, g) for g in range(4)], axis=0)
        ds_ref[...] += dsink

    row = _rows(TOK, 1024)
    tab = _rows(TOK, 128)
    pad = _whole(kpad.shape)
    return pl.pallas_call(
        body, name="attn_bwd", grid=(s // TOK,),
        in_specs=[pl.BlockSpec(memory_space=pltpu.SMEM), row, pad, pad, row, row, tab, tab, tab],
        out_specs=[row, _whole(pad_t), _whole(pad_t), _whole((1, 128))],
        out_shape=[jax.ShapeDtypeStruct((s, 1024), BF16), jax.ShapeDtypeStruct(pad_t, F32),
                   jax.ShapeDtypeStruct(pad_t, F32), jax.ShapeDtypeStruct((1, 128), F32)],
        compiler_params=_params(),
    )(sink, q, kpad, vpad, o, do, *rope)


def _in_proj1_bwd(dq, dk_t, dv_t, dgate, x1, dx2, g1, w_in1, rope):
    s = x1.shape[0]

    def body(dq_ref, dka_ref, dkb_ref, dva_ref, dvb_ref, dgate_ref, x1_ref, dx2_ref, g_ref, w_ref,
             c_ref, lo_ref, hi_ref, dz_ref, dx1_ref, gn_ref, dx1h_ref):
        i = pl.program_id(0)
        dk = jnp.concatenate([dka_ref[...], dkb_ref[...]], axis=1).T
        dv = jnp.concatenate([dva_ref[...], dvb_ref[...]], axis=1).T

        @pl.when(i == 0)
        def _():
            gn_ref[...] = jnp.zeros_like(gn_ref)

        dz_ref[:, 0:1024] = dq_ref[...]
        dz_ref[:, 1024:1280] = _rope_t(dk, c_ref[...], lo_ref[...], hi_ref[...]).astype(BF16)
        dz_ref[:, 1280:1536] = dv.astype(BF16)
        dz_ref[:, 1536:2560] = dgate_ref[...]
        dh = _dot_nt(dz_ref[...], w_ref[...])
        g = g_ref[...]
        r, xh, _ = _rms_fwd(x1_ref[...], g)
        gn_ref[...] += jnp.sum(dh * xh, axis=0, keepdims=True)
        dx1 = dx2_ref[...] + _rms_bwd(dh, g, r, xh)
        dx1_ref[...] = dx1
        dx1h_ref[...] = dx1.astype(BF16)

    row = _rows(TOK, 1024)
    per = TOK // ATTN_WINDOW
    half_a = pl.BlockSpec((256, TOK // 2), lambda i: (0, per * i + 1))
    half_b = pl.BlockSpec((256, TOK // 2), lambda i: (0, per * i + 2))
    tab = _rows(TOK, 128)
    acc = _whole((1, 1024))
    return pl.pallas_call(
        body, name="in_proj1_bwd", grid=(s // TOK,),
        in_specs=[row, half_a, half_b, half_a, half_b, row, row, row, acc, _whole(w_in1.shape), tab, tab, tab],
        out_specs=[_rows(TOK, 2560), row, acc, row],
        out_shape=[jax.ShapeDtypeStruct((s, 2560), BF16), jax.ShapeDtypeStruct((s, 1024), F32),
                   jax.ShapeDtypeStruct((1, 1024), F32), jax.ShapeDtypeStruct((s, 1024), BF16)],
        compiler_params=_params(),
    )(dq, dk_t, dk_t, dv_t, dv_t, dgate, x1, dx2, g1, w_in1, *rope)


def _mix0_bwd(dx1, z0, w_out0, gv, ws, bsx, wg, scale, band, rider):
    s = dx1.shape[0]
    n_tiles = s // TOK

    def body(dx1h_ref, z_ref, zp_ref, zn_ref, wout_ref, gv_ref, ws_ref, bsx_ref, wg_ref, sc_ref, band_ref,
             dz_ref, dpn_ref, dws_ref, dbs_ref, dgv_ref, dsc_ref, dwg_ref):
        i = pl.program_id(0)
        dz_ref[:, 3072:4096] = jnp.zeros((TOK, 1024), BF16)

        @pl.when(i == 0)
        def _():
            for ref in (dws_ref, dbs_ref, dgv_ref, dsc_ref, dwg_ref):
                ref[...] = jnp.zeros_like(ref)

        dcat = _dot_nt(dx1h_ref[...], wout_ref[...])
        dya = dcat[:, 0:1024]
        dyb = dcat[:, 1024:2048]

        au = z_ref[:, 0:1024].astype(F32)
        av = z_ref[:, 1024:2048].astype(F32)
        ag = z_ref[:, 2048:3072].astype(F32)
        gv = gv_ref[...]
        u, du = _gelu_and_grad(au)
        v1, dv1 = _gelu_and_grad(av)
        rv, vh, v2, mixed = _mixer_a(v1, gv, ws_ref, bsx_ref[...])
        sg = _sigmoid(ag)
        sil = ag * sg
        dz_ref[:, 2048:3072] = (dya * u * mixed * (sg * (1.0 + ag * (1.0 - sg)))).astype(BF16)
        dz_ref[:, 0:1024] = (dya * mixed * sil * du).astype(BF16)
        dmixed = dya * u * sil
        lane = lax.broadcasted_iota(jnp.int32, (1, 128), 1)
        dm16 = dmixed.astype(BF16)
        dv2_rows = []
        for c in range(TOK // CHUNK):
            rows = slice(c * CHUNK, (c + 1) * CHUNK)
            cols_out = []
            for h in range(4):
                cols = slice(h * 256, (h + 1) * 256)
                dws_ref[h] += _dot_nt(dm16[rows, cols], v2[rows, cols])
                dbs_ref[...] += jnp.where(lane == h, jnp.sum(dmixed[rows, cols], axis=-1, keepdims=True), 0.0)
                cols_out.append(_dot_tn(ws_ref[h], dm16[rows, cols]))
            dv2_rows.append(jnp.concatenate(cols_out, axis=1))
        dv2 = jnp.concatenate(dv2_rows, axis=0)
        dgv_ref[...] += jnp.sum(dv2 * vh, axis=0, keepdims=True)
        dz_ref[:, 1024:2048] = (_rms_bwd(dv2, gv, rv, vh) * dv1).astype(BF16)

        bx16 = z_ref[:, 3072:4096]
        bg = z_ref[:, 4096:5120].astype(F32)
        counts = _window_counts(i, TOK, s)
        halo = _with_halo(i, n_tiles, zp_ref, bx16, zn_ref)
        ps = [p.astype(BF16) for p in _mixer_b_pooled(bx16.astype(F32), halo, band_ref, counts)]
        pw = jnp.concatenate([_dot(ps[g], wg_ref[g]) for g in range(4)], axis=1)
        sgb = _sigmoid(bg)
        sc = sc_ref[...]
        dz_ref[:, 4096:5120] = (dyb * pw * sc * (sgb * (1.0 + bg * (1.0 - sgb)))).astype(BF16)
        dys = dyb * (bg * sgb)
        dsc_ref[...] += jnp.sum(dys * pw, axis=0, keepdims=True)
        dpw = (dys * sc).astype(BF16)
        for g in range(4):
            cols = slice(g * 256, (g + 1) * 256)
            dwg_ref[g] += _dot_tn(ps[g], dpw[:, cols])
            dpn_ref[:, cols] = (_dot_nt(dpw[:, cols], wg_ref[g]) / counts[g]).astype(BF16)

    prev, nxt = _halo_specs(s, 3)
    row = _rows(TOK, 1024)
    vec = _whole((1, 1024))
    return pl.pallas_call(
        rider.carried_by(body, 11, 7, n_tiles), name="mix0_bwd", grid=(n_tiles,),
        in_specs=[row, _rows(TOK, 5120), prev, nxt, _whole(w_out0.shape), vec, _whole(ws.shape),
                  _whole(bsx.shape), _whole(wg.shape), vec, _whole(band.shape)] + rider.in_specs,
        out_specs=[_rows(TOK, 5120), row, _whole((4, 128, 128)), _whole((128, 128)), vec, vec,
                   _whole((4, 256, 256))] + rider.out_specs,
        out_shape=[jax.ShapeDtypeStruct((s, 5120), BF16), jax.ShapeDtypeStruct((s, 1024), BF16),
                   jax.ShapeDtypeStruct((4, 128, 128), F32), jax.ShapeDtypeStruct((128, 128), F32),
                   jax.ShapeDtypeStruct((1, 1024), F32), jax.ShapeDtypeStruct((1, 1024), F32),
                   jax.ShapeDtypeStruct((4, 256, 256), F32)] + rider.out_shape,
        scratch_shapes=rider.scratch,
        compiler_params=_params(),
    )(dx1, z0, z0, z0, w_out0, gv, ws, bsx, wg, scale, band, *rider.parts)


def _fill_pooled_grad(dz0, dpn, band_t):
    s = dpn.shape[0]
    n_tiles = s // TOK

    def body(dz_in_ref, dpn_ref, dpp_ref, dpx_ref, band_ref, dbx_ref):
        i = pl.program_id(0)
        dpn = dpn_ref[...]
        halo = _with_halo(i, n_tiles, dpp_ref, dpn, dpx_ref)
        counts = _window_counts(i, TOK, s)
        for g in range(4):
            cols = slice(g * 256, (g + 1) * 256)
            dbx = _dot(band_ref[g], halo[:, cols]) - dpn[:, cols].astype(F32) * counts[g]
            dbx_ref[:, cols] = dbx.astype(BF16)

    prev, nxt = _halo_specs(s, 0)
    return pl.pallas_call(
        body, name="fill_pooled_grad", grid=(n_tiles,),
        in_specs=[ANY, _rows(TOK, 1024), prev, nxt, _whole(band_t.shape)],
        out_specs=pl.BlockSpec((TOK, 1024), lambda i: (i, 3)),
        out_shape=jax.ShapeDtypeStruct(dz0.shape, BF16),
        input_output_aliases={0: 0},
        compiler_params=_params(),
    )(dz0, dpn, dpn, dpn, band_t)


def _in_proj0_bwd(dz0, x, dx1, g0, w_in0, rider):
    s = x.shape[0]
    n_tiles = s // TOK

    def body(dz_ref, x_ref, dx1_ref, g_ref, w_ref, dx_ref, gn_ref):
        i = pl.program_id(0)

        @pl.when(i == 0)
        def _():
            gn_ref[...] = jnp.zeros_like(gn_ref)

        dh = _dot_nt(dz_ref[...], w_ref[...])
        g0v = g_ref[...]
        r, xh, _ = _rms_fwd(x_ref[...], g0v)
        gn_ref[...] += jnp.sum(dh * xh, axis=0, keepdims=True)
        dx_ref[...] = dx1_ref[...] + _rms_bwd(dh, g0v, r, xh)

    row = _rows(TOK, 1024)
    vec = _whole((1, 1024))
    return pl.pallas_call(
        rider.carried_by(body, 5, 2, n_tiles), name="in_proj0_bwd", grid=(n_tiles,),
        in_specs=[_rows(TOK, 5120), row, row, vec, _whole(w_in0.shape)] + rider.in_specs,
        out_specs=[row, vec] + rider.out_specs,
        out_shape=[jax.ShapeDtypeStruct((s, 1024), F32), jax.ShapeDtypeStruct((1, 1024), F32)] + rider.out_shape,
        scratch_shapes=rider.scratch,
        compiler_params=_params(),
    )(dz0, x, dx1, g0, w_in0, *rider.parts)


def _weight_grad(a, b, n_blocks, split, name):
    s, k = a.shape
    n = b.shape[1]
    tn = n // n_blocks
    w = tn // split
    ts = 512

    def body(a_ref, b_ref, o_ref):
        @pl.when(pl.program_id(1) == 0)
        def _():
            o_ref[...] = jnp.zeros_like(o_ref)

        res = _dot_tn(a_ref[...], b_ref[...])
        for q in range(split):
            o_ref[q] += res[:, q * w:(q + 1) * w]

    return pl.pallas_call(
        body, name=name, grid=(n_blocks, s // ts),
        in_specs=[pl.BlockSpec((ts, k), lambda j, t: (t, 0)), pl.BlockSpec((ts, tn), lambda j, t: (t, j))],
        out_specs=pl.BlockSpec((split, k, w), lambda j, t: (j, 0, 0)),
        out_shape=jax.ShapeDtypeStruct((n_blocks * split, k, w), F32),
        compiler_params=_params(),
    )(a, b)


def _row_tile(rows, cols):
    t = rows
    while t * cols * 4 > (1 << 20) and t % 16 == 0:
        t //= 2
    return t


def _add_sibling(where, g, theirs, name):
    _, _, rows, cols = g.shape
    t = _row_tile(rows, cols)

    def body(where_ref, g_ref, t_ref, o_ref):
        o_ref[...] = (g_ref[...] + t_ref[...]).astype(BF16)

    spec = pl.BlockSpec((None, t, cols), lambda s, i, p: (s, i, 0))
    return pl.pallas_call(
        body, name=name, out_shape=jax.ShapeDtypeStruct((4, rows, cols), BF16),
        grid_spec=pltpu.PrefetchScalarGridSpec(
            num_scalar_prefetch=1, grid=(4, rows // t),
            in_specs=[pl.BlockSpec((None, None, t, cols), lambda s, i, p: (s, p[1], i, 0)), spec], out_specs=spec),
        compiler_params=_params())(where, g, theirs)


def _sum_chips(where, g, theirs, slots, name):
    _, _, rows, cols = g.shape
    t = _row_tile(rows, cols)

    def body(where_ref, g_ref, t_ref, s_ref, o_ref):
        me = where_ref[0]
        own = g_ref[...] + t_ref[...]
        acc = jnp.where(me == 0, own, s_ref[0].astype(F32))
        for k in range(1, 4):
            acc = acc + jnp.where(me == k, own, s_ref[k].astype(F32))
        o_ref[...] = acc

    return pl.pallas_call(
        body, name=name, out_shape=jax.ShapeDtypeStruct((rows, cols), F32),
        grid_spec=pltpu.PrefetchScalarGridSpec(
            num_scalar_prefetch=1, grid=(rows // t,),
            in_specs=[pl.BlockSpec((None, None, t, cols), lambda i, p: (p[0], p[1], i, 0)),
                      pl.BlockSpec((None, t, cols), lambda i, p: (p[0], i, 0)),
                      pl.BlockSpec((4, t, cols), lambda i, p: (0, i, 0))],
            out_specs=pl.BlockSpec((t, cols), lambda i, p: (i, 0))),
        compiler_params=_params())(where, g, theirs, slots)


def _adamw_halves(where, w, own, theirs, m, v, name):
    rows, cols = own.shape
    t = _row_tile(rows, cols)
    per = rows // t

    def body(where_ref, w_ref, own_ref, th_ref, m_ref, v_ref, g_ref, d_ref, nm_ref, nv_ref):
        g = jnp.where(pl.program_id(0) == where_ref[1], own_ref[...], th_ref[...])
        g_ref[...] = g
        m2 = ADAM_B1 * m_ref[...] + (1.0 - ADAM_B1) * g
        v2 = ADAM_B2 * v_ref[...] + (1.0 - ADAM_B2) * (g * g)
        m_hat = m2 / (1.0 - ADAM_B1 ** ADAM_STEP)
        v_hat = v2 / (1.0 - ADAM_B2 ** ADAM_STEP)
        d_ref[...] = -ADAM_LR * (m_hat / (jnp.sqrt(v_hat) + ADAM_EPS) + ADAM_WD * w_ref[...])
        nm_ref[...] = m2
        nv_ref[...] = v2

    full = pl.BlockSpec((t, cols), lambda h, i, p: (h * per + i, 0))
    half = pl.BlockSpec((t, cols), lambda h, i, p: (i, 0))
    shp = jax.ShapeDtypeStruct(w.shape, F32)
    return pl.pallas_call(
        body, name=name, out_shape=[shp] * 4,
        grid_spec=pltpu.PrefetchScalarGridSpec(
            num_scalar_prefetch=1, grid=(2, per), in_specs=[full, half, half, full, full], out_specs=[full] * 4),
        compiler_params=_params())(where, w, own, theirs, m, v)


def _place_shard(where, w, cut, name):
    if cut.kind == "cols":
        r, n = cut.full_shape
        blk, grid = (256, n // 4), (r // 256,)
        src_map, dst_map = (lambda i, p: (i, 0)), (lambda i, p: (i, p[0]))
    elif cut.kind == "rows":
        r, n = cut.full_shape
        per = r // 4 // 256
        blk, grid = (256, n), (per,)
        src_map, dst_map = (lambda i, p: (i, 0)), (lambda i, p: (p[0] * per + i, 0))
    else:
        g, r, n = cut.full_shape
        blk, grid = (g, r // 4, n), (1,)
        src_map, dst_map = (lambda i, p: (0, 0, 0)), (lambda i, p: (0, p[0], 0))

    def body(where_ref, w_ref, o_ref):
        o_ref[...] = w_ref[...].astype(BF16)

    return pl.pallas_call(
        body, name=name, out_shape=jax.ShapeDtypeStruct(cut.full_shape, BF16),
        grid_spec=pltpu.PrefetchScalarGridSpec(
            num_scalar_prefetch=1, grid=grid, in_specs=[pl.BlockSpec(blk, src_map)],
            out_specs=pl.BlockSpec(blk, dst_map)),
        compiler_params=_params())(where, w)


def _sum_small(first, second, third):
    rows = second.shape[1]

    def body(a_ref, b_ref, c_ref, o_ref):
        top = a_ref[0] + c_ref[0] + b_ref[0, 0:8]
        rest = b_ref[0, 8:rows]
        for k in range(1, 8):
            top = top + (a_ref[k] + c_ref[k] + b_ref[k, 0:8])
            rest = rest + b_ref[k, 8:rows]
        o_ref[0:8] = top
        o_ref[8:rows] = rest

    return pl.pallas_call(
        body, name="sum_small", in_specs=[_whole(first.shape), _whole(second.shape), _whole(third.shape)],
        out_specs=_whole(second.shape[1:]), out_shape=jax.ShapeDtypeStruct(second.shape[1:], F32),
        compiler_params=_params())(first, second, third)


def _adamw(w, g, m, v, name):
    rows, cols = w.shape
    t = _row_tile(rows, cols)

    def body(w_ref, g_ref, m_ref, v_ref, d_ref, nm_ref, nv_ref):
        g = g_ref[...]
        m2 = ADAM_B1 * m_ref[...] + (1.0 - ADAM_B1) * g
        v2 = ADAM_B2 * v_ref[...] + (1.0 - ADAM_B2) * (g * g)
        m_hat = m2 / (1.0 - ADAM_B1 ** ADAM_STEP)
        v_hat = v2 / (1.0 - ADAM_B2 ** ADAM_STEP)
        d_ref[...] = -ADAM_LR * (m_hat / (jnp.sqrt(v_hat) + ADAM_EPS) + ADAM_WD * w_ref[...])
        nm_ref[...] = m2
        nv_ref[...] = v2

    spec = pl.BlockSpec((t, cols), lambda i: (i, 0))
    shp = jax.ShapeDtypeStruct(w.shape, F32)
    return pl.pallas_call(body, name=name, grid=(rows // t,), in_specs=[spec] * 4, out_specs=[spec] * 3,
                          out_shape=[shp] * 3, compiler_params=_params())(w, g, m, v)


def _place():
    x, y, c = lax.axis_index("x"), lax.axis_index("y"), lax.axis_index("c")
    chips = [(1 - x, y), (x, 1 - y), (1 - x, 1 - y)]
    return x, y, c, chips


class _Sharded:
    def __init__(self, kind, full_shape):
        self.kind = kind
        self.full_shape = full_shape
        self.chunk_axis = 1 if kind == "mid" else 0

    def in_full(self, ref, s, h):
        if self.kind == "cols":
            r, n = self.full_shape
            return ref.at[pl.ds(h * (r // 2), r // 2), pl.ds(pl.multiple_of(s * (n // 4), 128), n // 4)]
        if self.kind == "rows":
            r, _ = self.full_shape
            return ref.at[pl.ds(pl.multiple_of(s * (r // 4) + h * (r // 8), 8), r // 8), :]
        g, r, _ = self.full_shape
        return ref.at[pl.ds(h * (g // 2), g // 2), pl.ds(pl.multiple_of(s * (r // 4), 16), r // 4), :]


CHUNK_BYTES = 256 * 1024
MAX_CHUNKS = 32


def _cut(ref, axis, n):
    step = ref.shape[axis] // n
    lead = (slice(None),) * axis
    return [ref.at[lead + (pl.ds(k * step, step),)] for k in range(n)]


def _n_chunks(ref, axis):
    nbytes = int(np.prod(ref.shape)) * jnp.dtype(ref.dtype).itemsize
    n = 1
    while n < MAX_CHUNKS and nbytes // (2 * n) >= CHUNK_BYTES and ref.shape[axis] % (32 * n) == 0:
        n *= 2
    return n


def _remote(src, dst, send_sem, recv_sem, to):
    return pltpu.make_async_remote_copy(src_ref=src, dst_ref=dst, send_sem=send_sem, recv_sem=recv_sem,
                                        device_id=to, device_id_type=MESH)


def _start_remote(src, dst, axis, send_sem, recv_sem, to):
    n = _n_chunks(dst, axis)
    for s_k, d_k in zip(_cut(src, axis, n), _cut(dst, axis, n)):
        _remote(s_k, d_k, send_sem, recv_sem, to).start()
    return _remote(src, dst, send_sem, recv_sem, to)


def _start_local(src, dst, axis, sem):
    n = _n_chunks(dst, axis)
    for s_k, d_k in zip(_cut(src, axis, n), _cut(dst, axis, n)):
        pltpu.make_async_copy(s_k, d_k, sem).start()
    return pltpu.make_async_copy(src, dst, sem)


class _Gather:
    def __init__(self, fulls, cuts):
        n = len(fulls)
        self.fulls, self.cuts = list(fulls), list(cuts)
        self.in_specs = [ANY] * n
        self.out_specs = [ANY] * n
        self.out_shape = [jax.ShapeDtypeStruct(cut.full_shape, BF16) for cut in cuts]
        self.scratch = [pltpu.SemaphoreType.DMA((6 * n,)), pltpu.SemaphoreType.DMA((6 * n,))]

    def _step(self, step, src, out, send_sems, recv_sems):
        n, cuts = len(self.fulls), self.cuts
        x, y, c, chips = _place()
        me = 2 * x + y

        def ends(w, s, h, from_src):
            dst = cuts[w].in_full(out[w], s, h)
            return (cuts[w].in_full(src[w], s, h) if from_src else dst), dst

        for w in range(n):
            for j, chip in enumerate(chips):
                s = 2 * chip[0] + chip[1]
                k, k2 = 3 * w + j, 3 * n + 3 * w + j
                if step == "send":
                    _start_remote(*ends(w, me, c, True), cuts[w].chunk_axis, send_sems.at[k], recv_sems.at[k],
                                  (*chip, c))
                elif step == "pass_on":
                    _remote(*ends(w, s, c, False), send_sems.at[k], recv_sems.at[k], (x, y, c)).wait_recv()
                    _start_remote(*ends(w, s, c, False), cuts[w].chunk_axis, send_sems.at[k2], recv_sems.at[k2],
                                  (x, y, 1 - c))
                else:
                    _remote(*ends(w, s, 1 - c, False), send_sems.at[k2], recv_sems.at[k2], (x, y, c)).wait_recv()
                    _remote(*ends(w, me, c, True), send_sems.at[k], recv_sems.at[k], (x, y, c)).wait_send()
                    _remote(*ends(w, s, c, False), send_sems.at[k2], recv_sems.at[k2], (x, y, c)).wait_send()

    def alone(self, name):
        n = len(self.fulls)

        def body(*refs):
            for step in ("send", "pass_on", "finish"):
                self._step(step, refs[:n], refs[n:2 * n], *refs[2 * n:])

        return pl.pallas_call(
            body, name=name, in_specs=self.in_specs, out_specs=self.out_specs, out_shape=self.out_shape,
            input_output_aliases={w: w for w in range(n)}, scratch_shapes=self.scratch,
            compiler_params=pltpu.CompilerParams(has_side_effects=True),
        )(*self.fulls)

    def carried_by(self, body, n_in, n_out, n_tiles):
        k = len(self.fulls)

        def carrier(*refs):
            ins, src = refs[:n_in], refs[n_in:n_in + k]
            outs, out = refs[n_in + k:n_in + k + n_out], refs[n_in + k + n_out:n_in + 2 * k + n_out]
            sems = refs[n_in + 2 * k + n_out:]
            for step, at in (("send", 0), ("pass_on", n_tiles // 2), ("finish", n_tiles - 1)):
                if step == "finish":
                    body(*ins, *outs)

                @pl.when(pl.program_id(0) == at)
                def _():
                    self._step(step, src, out, *sems)

        return carrier


def _exchange_halves(grads, name):
    n = len(grads)

    def body(*refs):
        g = refs[:n]
        theirs = refs[n:2 * n]
        send_sems, recv_sems = refs[2 * n:]
        x, y, c, _ = _place()
        sends = [_start_remote(g[w].at[:, 1 - c], theirs[w], 1, send_sems.at[w], recv_sems.at[w], (x, y, 1 - c))
                 for w in range(n)]
        for w in range(n):
            _remote(g[w].at[:, 1 - c], theirs[w], send_sems.at[w], recv_sems.at[w], (x, y, c)).wait_recv()
        for cp in sends:
            cp.wait_send()

    return pl.pallas_call(
        body, name=name,
        in_specs=[ANY] * n, out_specs=[ANY] * n,
        out_shape=[jax.ShapeDtypeStruct((4,) + g.shape[2:], F32) for g in grads],
        scratch_shapes=[pltpu.SemaphoreType.DMA((n,)), pltpu.SemaphoreType.DMA((n,))],
        compiler_params=pltpu.CompilerParams(has_side_effects=True),
    )(*grads)


def _gather_small(small_ref, gathered, send_sems, recv_sems, first_sem, local_sem, start):
    x, y, c, _ = _place()
    me = 4 * x + 2 * y + c
    flips = [(fx, fy, fc) for fx in range(2) for fy in range(2) for fc in range(2)][1:]
    if start:
        _start_local(small_ref, gathered.at[me], 0, local_sem)
    else:
        pltpu.make_async_copy(small_ref, gathered.at[me], local_sem).wait()
    for k, (fx, fy, fc) in enumerate(flips):
        peer = (x + fx - 2 * x * fx, y + fy - 2 * y * fy, c + fc - 2 * c * fc)
        sems = (send_sems.at[first_sem + k], recv_sems.at[first_sem + k])
        if start:
            _start_remote(small_ref, gathered.at[me], 0, *sems, peer)
        else:
            cp = _remote(small_ref, gathered.at[4 * peer[0] + 2 * peer[1] + peer[2]], *sems, (x, y, c))
            cp.wait_recv()
            cp.wait_send()


class _ScatterRider:
    def __init__(self, parts, small):
        n = len(parts)
        self.n = n
        self.parts = list(parts) + [small]
        self.in_specs = [ANY] * (n + 1)
        self.out_specs = [ANY] * (n + 1)
        self.out_shape = ([jax.ShapeDtypeStruct(a.shape, a.dtype) for a in parts]
                          + [jax.ShapeDtypeStruct((8,) + small.shape, small.dtype)])
        self.scratch = [pltpu.SemaphoreType.DMA((3 * n + 7,)), pltpu.SemaphoreType.DMA((3 * n + 7,)),
                        pltpu.SemaphoreType.DMA]

    def _copies(self, p, out, send_sems, recv_sems, local_sem, start):
        x, y, c, chips = _place()
        me = 2 * x + y
        n = self.n
        _gather_small(p[n], out[n], send_sems, recv_sems, 3 * n, local_sem, start)
        for w in range(n):
            for j, chip in enumerate(chips):
                s = 2 * chip[0] + chip[1]
                if start:
                    _start_remote(p[w].at[s], out[w].at[me], 0, send_sems.at[3 * w + j], recv_sems.at[3 * w + j],
                                  (*chip, c))
                else:
                    cp = _remote(p[w].at[s], out[w].at[s], send_sems.at[3 * w + j], recv_sems.at[3 * w + j],
                                 (x, y, c))
                    cp.wait_recv()
                    cp.wait_send()

    def carried_by(self, body, n_in, n_out, n_tiles):
        k = len(self.parts)

        def carrier(*refs):
            ins, mine = refs[:n_in], refs[n_in:n_in + k]
            outs, slots = refs[n_in + k:n_in + k + n_out], refs[n_in + k + n_out:n_in + 2 * k + n_out]
            sems = refs[n_in + 2 * k + n_out:]

            @pl.when(pl.program_id(0) == 0)
            def _():
                self._copies(mine, slots, *sems, start=True)

            body(*ins, *outs)

            @pl.when(pl.program_id(0) == n_tiles - 1)
            def _():
                self._copies(mine, slots, *sems, start=False)

        return carrier


def _share_halves(halves, small):
    n = len(halves)
    flips = [(fx, fy, fc) for fx in range(2) for fy in range(2) for fc in range(2)][1:]

    def body(*refs):
        hv = refs[:n]
        small_ref = refs[n]
        out = refs[n + 1:2 * n + 1]
        gathered = refs[2 * n + 1]
        send_sems, recv_sems, local_sem = refs[2 * n + 2:]
        x, y, c, _ = _place()
        me = 4 * x + 2 * y + c
        local = _start_local(small_ref, gathered.at[me], 0, local_sem)
        sends = [_start_remote(hv[w], out[w], 0, send_sems.at[w], recv_sems.at[w], (x, y, 1 - c)) for w in range(n)]
        for k, (fx, fy, fc) in enumerate(flips):
            sends.append(_start_remote(
                small_ref, gathered.at[me], 0, send_sems.at[n + k], recv_sems.at[n + k],
                (x + fx - 2 * x * fx, y + fy - 2 * y * fy, c + fc - 2 * c * fc)))
        for w in range(n):
            _remote(hv[w], out[w], send_sems.at[w], recv_sems.at[w], (x, y, c)).wait_recv()
        for k, (fx, fy, fc) in enumerate(flips):
            peer = 4 * (x + fx - 2 * x * fx) + 2 * (y + fy - 2 * y * fy) + (c + fc - 2 * c * fc)
            _remote(small_ref, gathered.at[peer], send_sems.at[n + k], recv_sems.at[n + k], (x, y, c)).wait_recv()
        for cp in sends:
            cp.wait_send()
        local.wait()

    return pl.pallas_call(
        body, name="share_halves",
        in_specs=[ANY] * (n + 1), out_specs=[ANY] * (n + 1),
        out_shape=[jax.ShapeDtypeStruct(a.shape, F32) for a in halves] + [jax.ShapeDtypeStruct((8,) + small.shape, F32)],
        scratch_shapes=[pltpu.SemaphoreType.DMA((n + 7,)), pltpu.SemaphoreType.DMA((n + 7,)),
                        pltpu.SemaphoreType.DMA],
        compiler_params=pltpu.CompilerParams(has_side_effects=True),
    )(*halves, small)


SMALL_ROWS = 80


def _pack_small(vecs, ws, bs, sink):
    ws = jnp.zeros((64, 1024), F32) if ws is None else ws.reshape(64, 1024)
    bs = jnp.zeros((1, 512), F32) if bs is None else bs.reshape(1, 512)
    sink = jnp.zeros((1, 16), F32) if sink is None else sink.reshape(1, 16)
    top = jnp.concatenate(
        [v.reshape(1, 1024) for v in vecs]
        + [jnp.pad(bs, ((0, 0), (0, 512))), jnp.pad(sink, ((0, 0), (0, 1008))), jnp.zeros((1, 1024), F32)], axis=0)
    return jnp.concatenate([top, ws, jnp.zeros((8, 1024), F32)], axis=0)


def _unpack_small(p):
    vecs = [p[k] for k in range(5)]
    return vecs, p[8:72].reshape(4, 128, 128), p[5, :512].reshape(4, 128), p[6, :16]


def kernel(x, norm_0, w_in_0, a_v_norm_0, a_spatial_w_0, a_spatial_b_0, b_group_w_0, b_scale_0, w_out_0, norm_1, w_in_1, sink_1, w_out_1, final_norm, loss_target, m_norm_0, m_w_in_0, m_a_v_norm_0, m_a_spatial_w_0, m_a_spatial_b_0, m_b_group_w_0, m_b_scale_0, m_w_out_0, m_norm_1, m_w_in_1, m_sink_1, m_w_out_1, m_final_norm, v_norm_0, v_w_in_0, v_a_v_norm_0, v_a_spatial_w_0, v_a_spatial_b_0, v_b_group_w_0, v_b_scale_0, v_w_out_0, v_norm_1, v_w_in_1, v_sink_1, v_w_out_1, v_final_norm):
    s = x.shape[1]
    xs = x.reshape(s, D_MODEL)
    target = loss_target.reshape(s, D_MODEL)

    cuts = [_Sharded("cols", (1024, 5120)), _Sharded("rows", (2048, 1024)), _Sharded("cols", (1024, 2560)),
            _Sharded("rows", (1024, 1024)), _Sharded("mid", (4, 256, 256))]
    big_w = [w_in_0, w_out_0, w_in_1, w_out_1, b_group_w_0]
    big_m = [m_w_in_0, m_w_out_0, m_w_in_1, m_w_out_1, m_b_group_w_0]
    big_v = [v_w_in_0, v_w_out_0, v_w_in_1, v_w_out_1, v_b_group_w_0]
    where = jnp.stack([2 * lax.axis_index("x") + lax.axis_index("y"), lax.axis_index("c")]).astype(jnp.int32)
    placed = [_place_shard(where, w, cut, f"place_shard{k}") for k, (w, cut) in enumerate(zip(big_w, cuts))]
    w_in0, = _Gather(placed[:1], cuts[:1]).alone("gather_w_in0")

    row = lambda v: v.reshape(1, 1024)
    ws16 = a_spatial_w_0.astype(BF16)
    bsx = jnp.repeat(a_spatial_b_0.T, 256, axis=1)
    band, band_t = _band_matrices(TOK)
    rope = _rope_tables(s)

    h0, z0, w_out0, w_in1, w_out1, wg = _in_proj0(xs, row(norm_0), w_in0, _Gather(placed[1:], cuts[1:]))
    cat, x1 = _mix0_fwd(xs, z0, w_out0, row(a_v_norm_0), ws16, bsx, wg, row(b_scale_0), band)
    h1, q, k, v, gate = _in_proj1(x1, row(norm_1), w_in1, rope)
    kpad = jnp.pad(k, ((ATTN_WINDOW, ATTN_WINDOW), (0, 0)))
    vpad = jnp.pad(v, ((ATTN_WINDOW, ATTN_WINDOW), (0, 0)))
    o = _attn_fwd(q, kpad, vpad, sink_1)
    y1, dx2, do, dgate, loss_lanes, g_final, dx2h = _tail(x1, o, gate, target, w_out1, row(final_norm))

    dq, dkpad, dvpad, dsink = _attn_bwd(q, kpad, vpad, sink_1, o, do, rope)
    dz1, dx1, g_norm1, dx1h = _in_proj1_bwd(dq, dkpad, dvpad, dgate, x1, dx2, row(norm_1), w_in1, rope)

    g_w_in1 = _weight_grad(h1, dz1, 2, 2, "grad_w_in1").reshape(4, 2, 512, 640)
    g_w_out1 = _weight_grad(y1, dx2h, 1, 1, "grad_w_out1").reshape(4, 2, 128, 1024)
    g_w_out0 = _weight_grad(cat, dx1h, 1, 1, "grad_w_out0").reshape(4, 2, 256, 1024)
    first = [g_w_out0, g_w_in1, g_w_out1]
    theirs1 = _exchange_halves(first, "exchange_halves1")
    parts1 = [_add_sibling(where, g, t, f"add_sibling1_{k}") for k, (g, t) in enumerate(zip(first, theirs1))]
    zero = jnp.zeros((1024,), F32)
    small1 = _pack_small([zero, zero, zero, g_norm1, g_final], None, None, dsink[0, :16])[:8]
    dz0, dpn, d_ws, d_bs, d_gv, d_scale, d_wg, *slots1, small1_all = _mix0_bwd(
        dx1h, z0, w_out0, row(a_v_norm_0), ws16, bsx, wg, row(b_scale_0), band, _ScatterRider(parts1, small1))
    dz0 = _fill_pooled_grad(dz0, dpn, band_t)
    g_w_in0 = _weight_grad(h0, dz0, 4, 1, "grad_w_in0").reshape(4, 2, 512, 1280)
    g_wg = d_wg.reshape(2, 2, 4, 64, 256).transpose(2, 0, 1, 3, 4).reshape(4, 2, 128, 256)
    second = [g_w_in0, g_wg]
    theirs2 = _exchange_halves(second, "exchange_halves2")
    parts2 = [_add_sibling(where, g, t, f"add_sibling2_{k}") for k, (g, t) in enumerate(zip(second, theirs2))]
    small2 = _pack_small([zero, d_gv, d_scale, zero, zero], d_ws, d_bs[:, :4].T, None)
    grad_x, g_norm0, *slots2, small2_all = _in_proj0_bwd(
        dz0, xs, dx1, row(norm_0), w_in0, _ScatterRider(parts2, small2))

    grads = [g_w_in0, g_w_out0, g_w_in1, g_w_out1, g_wg]
    theirs = [theirs2[0], theirs1[0], theirs1[1], theirs1[2], theirs2[1]]
    slots = [slots2[0], slots1[0], slots1[1], slots1[2], slots2[1]]
    n = len(grads)
    reduced = [_sum_chips(where, grads[w], theirs[w], slots[w], f"sum_chips{w}") for w in range(n)]
    small3 = _pack_small([g_norm0, zero, zero, zero, zero], None, None, None)[:8]
    *from_sibling, small3_all = _share_halves(reduced, small3)

    out_g, out_d, out_m, out_v = {}, {}, {}, {}
    names = ["w_in_0", "w_out_0", "w_in_1", "w_out_1", "b_group_w_0"]
    for w in range(n):
        shape = big_w[w].shape
        two_d = (-1, shape[-1])
        outs = _adamw_halves(where, big_w[w].reshape(two_d), reduced[w], from_sibling[w], big_m[w].reshape(two_d),
                             big_v[w].reshape(two_d), f"adamw{w}")
        out_g[names[w]], out_d[names[w]], out_m[names[w]], out_v[names[w]] = (a.reshape(shape) for a in outs)

    g_small = _sum_small(small1_all, small2_all, small3_all)
    small_names = ["norm_0", "a_v_norm_0", "b_scale_0", "norm_1", "final_norm"]
    pack = lambda vecs, ws_, bs_, sk: _pack_small(vecs, ws_, bs_, sk)
    w_small = pack([norm_0, a_v_norm_0, b_scale_0, norm_1, final_norm], a_spatial_w_0, a_spatial_b_0, sink_1)
    m_small = pack([m_norm_0, m_a_v_norm_0, m_b_scale_0, m_norm_1, m_final_norm], m_a_spatial_w_0,
                   m_a_spatial_b_0, m_sink_1)
    v_small = pack([v_norm_0, v_a_v_norm_0, v_b_scale_0, v_norm_1, v_final_norm], v_a_spatial_w_0,
                   v_a_spatial_b_0, v_sink_1)
    d_small, nm_small, nv_small = _adamw(w_small, g_small, m_small, v_small, "adamw_small")
    for store, packed in ((out_g, g_small), (out_d, d_small), (out_m, nm_small), (out_v, nv_small)):
        vecs, ws_, bs_, sk = _unpack_small(packed)
        for name, vec in zip(small_names, vecs):
            store[name] = vec
        store["a_spatial_w_0"], store["a_spatial_b_0"], store["sink_1"] = ws_, bs_, sk

    loss = lax.psum(jnp.sum(loss_lanes), ("x", "y", "c"))
    order = ["norm_0", "w_in_0", "a_v_norm_0", "a_spatial_w_0", "a_spatial_b_0", "b_group_w_0", "b_scale_0",
             "w_out_0", "norm_1", "w_in_1", "sink_1", "w_out_1", "final_norm"]
    return (loss, grad_x.reshape(1, s, D_MODEL), *[out_g[k] for k in order], *[out_d[k] for k in order],
            *[out_m[k] for k in order], *[out_v[k] for k in order])
```

```python
import functools

import numpy as np
import jax
import jax.numpy as jnp
from jax import lax
from jax.experimental import pallas as pl
from jax.experimental.pallas import tpu as pltpu

F32 = jnp.float32
BF16 = jnp.bfloat16
MESH = pl.DeviceIdType.MESH

D_MODEL = 1024
EPS = 1e-6
NEG_INF = -1e30
CHUNK = 128
POOL_WINDOWS = (2, 4, 8, 16)
HALO = 16
N_HEADS = 16
HEAD_DIM = 64
ATTN_WINDOW = 128
ROPE_THETA = 500000.0
ROT_DIM = 16
ADAM_LR = 0.001
ADAM_B1 = 0.9
ADAM_B2 = 0.999
ADAM_EPS = 1e-08
ADAM_WD = 0.01
ADAM_STEP = 10

TOK = 256
VMEM_LIMIT = 56 * 1024 * 1024


def _params(**kw):
    return pltpu.CompilerParams(vmem_limit_bytes=VMEM_LIMIT, **kw)


def _whole(shape):
    nd = len(shape)
    return pl.BlockSpec(shape, lambda *_: (0,) * nd)


def _rows(t, n):
    return pl.BlockSpec((t, n), lambda i: (i, 0))


ANY = pl.BlockSpec(memory_space=pl.ANY)

_G0 = 0.7978845608028654
_G1 = 0.044715


def _gelu(x):
    return 0.5 * x * (1.0 + jnp.tanh(_G0 * (x + _G1 * x * x * x)))


def _gelu_and_grad(x):
    x2 = x * x
    t = jnp.tanh(_G0 * (x + _G1 * x2 * x))
    half = 0.5 * (1.0 + t)
    return x * half, half + 0.5 * x * (1.0 - t * t) * (_G0 * (1.0 + 3.0 * _G1 * x2))


def _sigmoid(x):
    return 1.0 / (1.0 + jnp.exp(-x))


def _dot(a, b):
    return jnp.dot(a, b, preferred_element_type=F32)


def _dot_nt(a, b):
    return lax.dot_general(a, b, (((1,), (1,)), ((), ())), preferred_element_type=F32)


def _dot_tn(a, b):
    return lax.dot_general(a, b, (((0,), (0,)), ((), ())), preferred_element_type=F32)


def _rms_fwd(x, g):
    r = lax.rsqrt(jnp.mean(x * x, axis=-1, keepdims=True) + EPS)
    xh = x * r
    return r, xh, xh * g


def _rms_bwd(dy, g, r, xh):
    dxh = dy * g
    return r * (dxh - xh * jnp.mean(dxh * xh, axis=-1, keepdims=True))


def _band_matrices(t):
    r = np.arange(t)[:, None]
    j = np.arange(t + 2 * HALO)[None, :]
    fwd, bwd = [], []
    for w in POOL_WINDOWS:
        d = j - r - HALO
        fwd.append((d >= -(w // 2)) & (d < w // 2))
        bwd.append((d >= -(w // 2) + 1) & (d <= w // 2))
    return (jnp.asarray(np.stack(fwd), BF16), jnp.asarray(np.stack(bwd), BF16))


def _window_counts(i, t, s):
    tok = i * t + lax.broadcasted_iota(jnp.int32, (t, 1), 0)
    out = []
    for w in POOL_WINDOWS:
        cnt = jnp.minimum(tok + w // 2, s) - jnp.maximum(tok - w // 2, 0)
        out.append(cnt.astype(F32))
    return out


def _rope_tables(s):
    inv = np.float32(ROPE_THETA) ** (-np.arange(0, ROT_DIM, 2, dtype=np.float32) / np.float32(ROT_DIM))
    ang = np.arange(s, dtype=np.float32)[:, None] * inv.astype(np.float32)[None, :]
    cos, sin = np.cos(ang).astype(np.float32), np.sin(ang).astype(np.float32)
    z8 = np.zeros((s, 8), np.float32)
    z48 = np.zeros((s, HEAD_DIM - ROT_DIM), np.float32)
    c = np.concatenate([cos, cos, np.ones((s, HEAD_DIM - ROT_DIM), np.float32)], axis=1)
    s_lo = np.concatenate([z8, sin, z48], axis=1)
    s_hi = np.concatenate([-sin, z8, z48], axis=1)
    return tuple(jnp.asarray(np.concatenate([a, a], axis=1)) for a in (c, s_lo, s_hi))


def _rope(x, c, s_lo, s_hi):
    n = x.shape[1]
    reps = n // 128
    c, s_lo, s_hi = (jnp.tile(a, (1, reps)) for a in (c, s_lo, s_hi))
    return x * c + pltpu.roll(x, 8, 1) * s_lo + pltpu.roll(x, n - 8, 1) * s_hi


def _rope_t(dx, c, s_lo, s_hi):
    n = dx.shape[1]
    reps = n // 128
    c, s_lo, s_hi = (jnp.tile(a, (1, reps)) for a in (c, s_lo, s_hi))
    return dx * c + pltpu.roll(dx * s_lo, n - 8, 1) + pltpu.roll(dx * s_hi, 8, 1)


def _in_proj0_own(order, x, g0, w_own, rider):
    s = x.shape[0]
    n = w_own.shape[1]

    def body(order_ref, x_ref, g_ref, w_ref, h_ref, z_ref):
        _, _, h = _rms_fwd(x_ref[...], g_ref[...])
        h = h.astype(BF16)
        h_ref[...] = h
        z_ref[...] = _dot(h, w_ref[...]).astype(BF16)

    return pl.pallas_call(
        rider.carried_by(body, 4, 2, s // TOK), name="in_proj0_own",
        grid_spec=pltpu.PrefetchScalarGridSpec(
            num_scalar_prefetch=1, grid=(s // TOK,),
            in_specs=[pl.BlockSpec((TOK, D_MODEL), lambda i, o: (i, 0)), pl.BlockSpec((1, D_MODEL), lambda i, o: (0, 0)),
                      pl.BlockSpec(w_own.shape, lambda i, o: (0, 0))] + rider.in_specs,
            out_specs=[pl.BlockSpec((TOK, D_MODEL), lambda i, o: (i, 0)),
                       pl.BlockSpec((TOK, n), lambda i, o: (i, o[0]))] + rider.out_specs,
            scratch_shapes=rider.scratch),
        out_shape=[jax.ShapeDtypeStruct((s, D_MODEL), BF16), jax.ShapeDtypeStruct((s, 4 * n), BF16)] + rider.out_shape,
        input_output_aliases={4 + j: 2 + j for j in range(len(rider.fulls))},
        compiler_params=_params(),
    )(order, x, g0, w_own, *rider.fulls)


def _in_proj0_rest(order, h0, w_in0, z0, rider):
    s = h0.shape[0]
    n_tiles = s // TOK
    n = w_in0.shape[1] // 4

    def body(order_ref, h_ref, w_ref, z_in_ref, z_ref):
        z_ref[...] = _dot(h_ref[...], w_ref[...]).astype(BF16)

    return pl.pallas_call(
        rider.carried_by(body, 4, 1, 3 * n_tiles, lambda: pl.program_id(0) * n_tiles + pl.program_id(1)),
        name="in_proj0_rest",
        grid_spec=pltpu.PrefetchScalarGridSpec(
            num_scalar_prefetch=1, grid=(3, n_tiles),
            in_specs=[pl.BlockSpec((TOK, D_MODEL), lambda k, i, o: (i, 0)),
                      pl.BlockSpec((w_in0.shape[0], n), lambda k, i, o: (0, o[1 + k])), ANY] + rider.in_specs,
            out_specs=[pl.BlockSpec((TOK, n), lambda k, i, o: (i, o[1 + k]))] + rider.out_specs,
            scratch_shapes=rider.scratch),
        out_shape=[jax.ShapeDtypeStruct(z0.shape, BF16)] + rider.out_shape,
        input_output_aliases={3: 0, **{4 + j: 1 + j for j in range(len(rider.fulls))}},
        compiler_params=_params(),
    )(order, h0, w_in0, z0, *rider.fulls)


def _halo_specs(s, col_block):
    per = TOK // HALO
    last = s // HALO - 1
    prev = pl.BlockSpec((HALO, 1024), lambda i: (jnp.maximum(i * per - 1, 0), col_block))
    nxt = pl.BlockSpec((HALO, 1024), lambda i: (jnp.minimum((i + 1) * per, last), col_block))
    return prev, nxt


def _with_halo(i, n_tiles, prev_ref, cur, next_ref):
    prev = prev_ref[...]
    nxt = next_ref[...]
    prev = jnp.where(i > 0, prev, jnp.zeros_like(prev))
    nxt = jnp.where(i < n_tiles - 1, nxt, jnp.zeros_like(nxt))
    return jnp.concatenate([prev, cur, nxt], axis=0)


def _mixer_a(v1, gv, ws_ref, bsx):
    rv, vh, v2 = _rms_fwd(v1, gv)
    v2 = v2.astype(BF16)
    rows = []
    for c in range(v1.shape[0] // CHUNK):
        cols = [_dot(ws_ref[h], v2[c * CHUNK:(c + 1) * CHUNK, h * 256:(h + 1) * 256]) for h in range(4)]
        rows.append(jnp.concatenate(cols, axis=1) + bsx)
    return rv, vh, v2, jnp.concatenate(rows, axis=0)


def _mixer_b_pooled(bx, halo, band_ref, counts):
    out = []
    for g in range(4):
        win = _dot(band_ref[g], halo[:, g * 256:(g + 1) * 256])
        out.append(win / counts[g] - bx[:, g * 256:(g + 1) * 256])
    return out


def _mix0_fwd(x, z0, w_out0, gv, ws, bsx, wg, scale, band, rider):
    s = x.shape[0]
    n_tiles = s // TOK
    k = len(rider.fulls)

    def body(z_ref, zp_ref, zn_ref, x_ref, wout_ref, gv_ref, ws_ref, bsx_ref, wg_ref, sc_ref, band_ref,
             cat_ref, x1_ref):
        i = pl.program_id(0)
        au = z_ref[:, 0:1024].astype(F32)
        av = z_ref[:, 1024:2048].astype(F32)
        ag = z_ref[:, 2048:3072].astype(F32)
        _, _, _, mixed = _mixer_a(_gelu(av), gv_ref[...], ws_ref, bsx_ref[...])
        cat_ref[:, 0:1024] = (_gelu(au) * mixed * (ag * _sigmoid(ag))).astype(BF16)

        bx16 = z_ref[:, 3072:4096]
        bg = z_ref[:, 4096:5120].astype(F32)
        halo = _with_halo(i, n_tiles, zp_ref, bx16, zn_ref)
        ps = _mixer_b_pooled(bx16.astype(F32), halo, band_ref, _window_counts(i, TOK, s))
        pw = jnp.concatenate([_dot(ps[g].astype(BF16), wg_ref[g]) for g in range(4)], axis=1)
        cat_ref[:, 1024:2048] = (pw * sc_ref[...] * (bg * _sigmoid(bg))).astype(BF16)

        x1_ref[...] = x_ref[...] + _dot(cat_ref[...], wout_ref[...])

    prev, nxt = _halo_specs(s, 3)
    return pl.pallas_call(
        rider.carried_by(body, 11, 2, n_tiles), name="mix0_fwd", grid=(n_tiles,),
        in_specs=[_rows(TOK, 5120), prev, nxt, _rows(TOK, D_MODEL), _whole(w_out0.shape), _whole((1, 1024)),
                  _whole(ws.shape), _whole(bsx.shape), _whole(wg.shape), _whole((1, 1024)), _whole(band.shape)]
        + rider.in_specs,
        out_specs=[_rows(TOK, 2048), _rows(TOK, D_MODEL)] + rider.out_specs,
        out_shape=[jax.ShapeDtypeStruct((s, 2048), BF16), jax.ShapeDtypeStruct((s, D_MODEL), F32)] + rider.out_shape,
        input_output_aliases={11 + j: 2 + j for j in range(k)},
        scratch_shapes=rider.scratch,
        compiler_params=_params(),
    )(z0, z0, z0, x, w_out0, gv, ws, bsx, wg, scale, band, *rider.fulls)


def _in_proj1(x1, g1, w_in1, rope):
    s = x1.shape[0]

    def body(x_ref, g_ref, w_ref, c_ref, lo_ref, hi_ref, h_ref, q_ref, k_ref, v_ref, gate_ref):
        _, _, h = _rms_fwd(x_ref[...], g_ref[...])
        h = h.astype(BF16)
        h_ref[...] = h
        tabs = (c_ref[...], lo_ref[...], hi_ref[...])
        q_ref[...] = (_rope(_dot(h, w_ref[:, 0:1024]), *tabs) * Q_SCALE).astype(BF16)
        kv = _dot(h, w_ref[:, 1024:1536])
        k_ref[...] = _rope(kv[:, 0:256], *tabs).astype(BF16)
        v_ref[...] = kv[:, 256:512].astype(BF16)
        gate_ref[...] = _dot(h, w_ref[:, 1536:2560]).astype(BF16)

    tab = _rows(TOK, 128)
    return pl.pallas_call(
        body, name="in_proj1", grid=(s // TOK,),
        in_specs=[_rows(TOK, D_MODEL), _whole((1, D_MODEL)), _whole(w_in1.shape), tab, tab, tab],
        out_specs=[_rows(TOK, 1024), _rows(TOK, 1024), _rows(TOK, 256), _rows(TOK, 256), _rows(TOK, 1024)],
        out_shape=[jax.ShapeDtypeStruct((s, 1024), BF16), jax.ShapeDtypeStruct((s, 1024), BF16),
                   jax.ShapeDtypeStruct((s, 256), BF16), jax.ShapeDtypeStruct((s, 256), BF16),
                   jax.ShapeDtypeStruct((s, 1024), BF16)],
        compiler_params=_params(),
    )(x1, g1, w_in1, *rope)


QBLK = 128
KBLK = QBLK + 2 * ATTN_WINDOW
Q_SCALE = HEAD_DIM ** -0.5
HEAD_BATCH = 4


def _block_bias(q0, s):
    r = lax.broadcasted_iota(jnp.int32, (QBLK, KBLK), 0)
    c = lax.broadcasted_iota(jnp.int32, (QBLK, KBLK), 1)
    kj = q0 - ATTN_WINDOW + c
    ok = (c >= r) & (c <= r + 2 * ATTN_WINDOW) & (kj >= 0) & (kj < s)
    return jnp.where(ok, 0.0, NEG_INF)


def _attn_fwd(q, kpad, vpad, sink):
    s = q.shape[0]

    def body(sink_ref, q_ref, k_ref, v_ref, o_ref):
        i = pl.program_id(0)
        for b in range(TOK // QBLK):
            rows = slice(b * QBLK, (b + 1) * QBLK)
            start = pl.multiple_of(i * TOK + b * QBLK, QBLK)
            kb = k_ref[pl.ds(start, KBLK), :]
            vb = v_ref[pl.ds(start, KBLK), :]
            bias = _block_bias(i * TOK + b * QBLK, s)
            kv = lambda t, h: t[:, (h // 4) * HEAD_DIM:(h // 4 + 1) * HEAD_DIM]
            scs = [_dot_nt(q_ref[rows, h * HEAD_DIM:(h + 1) * HEAD_DIM], kv(kb, h)) + bias for h in range(N_HEADS)]
            ms = [jnp.maximum(jnp.max(scs[h], axis=-1, keepdims=True), sink_ref[h]) for h in range(N_HEADS)]
            es = [jnp.exp(scs[h] - ms[h]) for h in range(N_HEADS)]
            rdens = [1.0 / (jnp.sum(es[h], axis=-1, keepdims=True) + jnp.exp(sink_ref[h] - ms[h]))
                     for h in range(N_HEADS)]
            outs = [_dot(es[h].astype(BF16), kv(vb, h)) * rdens[h] for h in range(N_HEADS)]
            o_ref[rows, :] = jnp.concatenate(outs, axis=1).astype(BF16)

    return pl.pallas_call(
        body, name="attn_fwd", grid=(s // TOK,),
        in_specs=[pl.BlockSpec(memory_space=pltpu.SMEM), _rows(TOK, 1024), _whole(kpad.shape), _whole(vpad.shape)],
        out_specs=_rows(TOK, 1024),
        out_shape=jax.ShapeDtypeStruct((s, 1024), BF16),
        compiler_params=_params(),
    )(sink, q, kpad, vpad)


def _tail(x1, o, gate, target, w_out1, gf):
    s = x1.shape[0]

    def body(x1_ref, o_ref, gate_ref, t_ref, w_ref, gf_ref, y1_ref, dx2_ref, do_ref, dgate_ref, loss_ref, gfn_ref,
             dx2h_ref):
        i = pl.program_id(0)

        @pl.when(i == 0)
        def _():
            loss_ref[...] = jnp.zeros_like(loss_ref)
            gfn_ref[...] = jnp.zeros_like(gfn_ref)

        g = gate_ref[...].astype(F32)
        sg = _sigmoid(g)
        sil = g * sg
        o = o_ref[...].astype(F32)
        y1 = (o * sil).astype(BF16)
        y1_ref[...] = y1
        x2 = x1_ref[...] + _dot(y1, w_ref[...])
        gf = gf_ref[...]
        r, xh, out = _rms_fwd(x2, gf)
        diff = out - t_ref[...]
        loss_ref[...] += jnp.sum(diff * diff, axis=0, keepdims=True) * (0.5 / D_MODEL)
        dout = diff * (1.0 / D_MODEL)
        gfn_ref[...] += jnp.sum(dout * xh, axis=0, keepdims=True)
        dx2 = _rms_bwd(dout, gf, r, xh)
        dx2_ref[...] = dx2
        dx2h = dx2.astype(BF16)
        dx2h_ref[...] = dx2h
        dy1 = _dot_nt(dx2h, w_ref[...])
        do_ref[...] = (dy1 * sil).astype(BF16)
        dgate_ref[...] = (dy1 * o * (sg * (1.0 + g * (1.0 - sg)))).astype(BF16)

    row = _rows(TOK, 1024)
    acc = _whole((1, 1024))
    return pl.pallas_call(
        body, name="tail", grid=(s // TOK,),
        in_specs=[row, row, row, row, _whole(w_out1.shape), acc],
        out_specs=[row, row, row, row, acc, acc, row],
        out_shape=[jax.ShapeDtypeStruct((s, 1024), BF16), jax.ShapeDtypeStruct((s, 1024), F32),
                   jax.ShapeDtypeStruct((s, 1024), BF16), jax.ShapeDtypeStruct((s, 1024), BF16),
                   jax.ShapeDtypeStruct((1, 1024), F32), jax.ShapeDtypeStruct((1, 1024), F32),
                   jax.ShapeDtypeStruct((s, 1024), BF16)],
        compiler_params=_params(),
    )(x1, o, gate, target, w_out1, gf)


def _attn_bwd(q, kpad, vpad, sink, o, do, rope):
    s = q.shape[0]
    pad_t = (kpad.shape[1], kpad.shape[0])

    def body(sink_ref, q_ref, k_ref, v_ref, o_ref, do_ref, c_ref, lo_ref, hi_ref, dq_ref, dk_ref, dv_ref, ds_ref):
        i = pl.program_id(0)

        @pl.when(i == 0)
        def _():
            dk_ref[...] = jnp.zeros_like(dk_ref)
            dv_ref[...] = jnp.zeros_like(dv_ref)
            ds_ref[...] = jnp.zeros_like(ds_ref)

        lane = lax.broadcasted_iota(jnp.int32, (1, 128), 1)
        dsink = jnp.zeros((1, 128), F32)
        for b in range(TOK // QBLK):
            rows = slice(b * QBLK, (b + 1) * QBLK)
            start = pl.multiple_of(i * TOK + b * QBLK, QBLK)
            kb = k_ref[pl.ds(start, KBLK), :]
            vb = v_ref[pl.ds(start, KBLK), :]
            bias = _block_bias(i * TOK + b * QBLK, s)
            kv = lambda t, h: t[:, (h // 4) * HEAD_DIM:(h // 4 + 1) * HEAD_DIM]
            hd = lambda ref, h: ref[rows, h * HEAD_DIM:(h + 1) * HEAD_DIM]
            dqs, dks, dvs = {}, {}, {}
            for h0 in range(0, N_HEADS, HEAD_BATCH):
                heads = range(h0, h0 + HEAD_BATCH)
                scs = {h: _dot_nt(hd(q_ref, h), kv(kb, h)) + bias for h in heads}
                ms ={h: jnp.maximum(jnp.max(scs[h], axis=-1, keepdims=True), sink_ref[h]) for h in heads}
                es = {h: jnp.exp(scs[h] - ms[h]) for h in heads}
                e_sinks = {h: jnp.exp(sink_ref[h] - ms[h]) for h in heads}
                rdens = {h: 1.0 / (jnp.sum(es[h], axis=-1, keepdims=True) + e_sinks[h]) for h in heads}
                dos = {h: hd(do_ref, h).astype(F32) for h in heads}
                deltas = {h: jnp.sum(dos[h] * hd(o_ref, h).astype(F32), axis=-1, keepdims=True) * rdens[h]
                          for h in heads}
                do_ns = {h: (dos[h] * rdens[h]).astype(BF16) for h in heads}
                for h in heads:
                    dsink = dsink + jnp.where(
                        lane == h, -jnp.sum(e_sinks[h] * deltas[h], axis=0, keepdims=True), 0.0)
                dscs = {h: (es[h] * (_dot_nt(do_ns[h], kv(vb, h)) - deltas[h])).astype(BF16) for h in heads}
                dvs.update({h: _dot_tn(do_ns[h], es[h].astype(BF16)) for h in heads})
                dks.update({h: _dot_tn(hd(q_ref, h), dscs[h]) for h in heads})
                dqs.update({h: _dot(dscs[h], kv(kb, h)) * Q_SCALE for h in heads})
            dq = jnp.concatenate([dqs[h] for h in range(N_HEADS)], axis=1)
            dq_ref[rows, :] = _rope_t(dq, c_ref[rows, :], lo_ref[rows, :], hi_ref[rows, :]).astype(BF16)
            group_sum = lambda parts, g: (parts[4 * g] + parts[4 * g + 1]) + (parts[4 * g + 2] + parts[4 * g + 3])
            dk_ref[:, pl.ds(start, KBLK)] += jnp.concatenate([group_sum(dks, g) for g in range(4)], axis=0)
            dv_ref[:, pl.ds(start, KBLK)] += jnp.concatenate([group_sum(dvs, g) for g in range(4)], axis=0)
        ds_ref[...] += dsink

    row = _rows(TOK, 1024)
    tab = _rows(TOK, 128)
    pad = _whole(kpad.shape)
    return pl.pallas_call(
        body, name="attn_bwd", grid=(s // TOK,),
        in_specs=[pl.BlockSpec(memory_space=pltpu.SMEM), row, pad, pad, row, row, tab, tab, tab],
        out_specs=[row, _whole(pad_t), _whole(pad_t), _whole((1, 128))],
        out_shape=[jax.ShapeDtypeStruct((s, 1024), BF16), jax.ShapeDtypeStruct(pad_t, F32),
                   jax.ShapeDtypeStruct(pad_t, F32), jax.ShapeDtypeStruct((1, 128), F32)],
        compiler_params=_params(),
    )(sink, q, kpad, vpad, o, do, *rope)


def _in_proj1_bwd(dq, dk_t, dv_t, dgate, x1, dx2, g1, w_in1, rope):
    s = x1.shape[0]

    def body(dq_ref, dka_ref, dkb_ref, dva_ref, dvb_ref, dgate_ref, x1_ref, dx2_ref, g_ref, w_ref,
             c_ref, lo_ref, hi_ref, dz_ref, dx1_ref, gn_ref, dx1h_ref):
        i = pl.program_id(0)
        dk = jnp.concatenate([dka_ref[...], dkb_ref[...]], axis=1).T
        dv = jnp.concatenate([dva_ref[...], dvb_ref[...]], axis=1).T

        @pl.when(i == 0)
        def _():
            gn_ref[...] = jnp.zeros_like(gn_ref)

        dz_ref[:, 0:1024] = dq_ref[...]
        dz_ref[:, 1024:1280] = _rope_t(dk, c_ref[...], lo_ref[...], hi_ref[...]).astype(BF16)
        dz_ref[:, 1280:1536] = dv.astype(BF16)
        dz_ref[:, 1536:2560] = dgate_ref[...]
        dh = _dot_nt(dz_ref[...], w_ref[...])
        g = g_ref[...]
        r, xh, _ = _rms_fwd(x1_ref[...], g)
        gn_ref[...] += jnp.sum(dh * xh, axis=0, keepdims=True)
        dx1 = dx2_ref[...] + _rms_bwd(dh, g, r, xh)
        dx1_ref[...] = dx1
        dx1h_ref[...] = dx1.astype(BF16)

    row = _rows(TOK, 1024)
    per = TOK // ATTN_WINDOW
    half_a = pl.BlockSpec((256, TOK // 2), lambda i: (0, per * i + 1))
    half_b = pl.BlockSpec((256, TOK // 2), lambda i: (0, per * i + 2))
    tab = _rows(TOK, 128)
    acc = _whole((1, 1024))
    return pl.pallas_call(
        body, name="in_proj1_bwd", grid=(s // TOK,),
        in_specs=[row, half_a, half_b, half_a, half_b, row, row, row, acc, _whole(w_in1.shape), tab, tab, tab],
        out_specs=[_rows(TOK, 2560), row, acc, row],
        out_shape=[jax.ShapeDtypeStruct((s, 2560), BF16), jax.ShapeDtypeStruct((s, 1024), F32),
                   jax.ShapeDtypeStruct((1, 1024), F32), jax.ShapeDtypeStruct((s, 1024), BF16)],
        compiler_params=_params(),
    )(dq, dk_t, dk_t, dv_t, dv_t, dgate, x1, dx2, g1, w_in1, *rope)


def _mix0_bwd(dx1, z0, w_out0, gv, ws, bsx, wg, scale, band, rider):
    s = dx1.shape[0]
    n_tiles = s // TOK

    def body(dx1h_ref, z_ref, zp_ref, zn_ref, wout_ref, gv_ref, ws_ref, bsx_ref, wg_ref, sc_ref, band_ref,
             dz_ref, dpn_ref, dws_ref, dbs_ref, dgv_ref, dsc_ref, dwg_ref):
        i = pl.program_id(0)
        dz_ref[:, 3072:4096] = jnp.zeros((TOK, 1024), BF16)

        @pl.when(i == 0)
        def _():
            for ref in (dws_ref, dbs_ref, dgv_ref, dsc_ref, dwg_ref):
                ref[...] = jnp.zeros_like(ref)

        dcat = _dot_nt(dx1h_ref[...], wout_ref[...])
        dya = dcat[:, 0:1024]
        dyb = dcat[:, 1024:2048]

        au = z_ref[:, 0:1024].astype(F32)
        av = z_ref[:, 1024:2048].astype(F32)
        ag = z_ref[:, 2048:3072].astype(F32)
        gv = gv_ref[...]
        u, du = _gelu_and_grad(au)
        v1, dv1 = _gelu_and_grad(av)
        rv, vh, v2, mixed = _mixer_a(v1, gv, ws_ref, bsx_ref[...])
        sg = _sigmoid(ag)
        sil = ag * sg
        dz_ref[:, 2048:3072] = (dya * u * mixed * (sg * (1.0 + ag * (1.0 - sg)))).astype(BF16)
        dz_ref[:, 0:1024] = (dya * mixed * sil * du).astype(BF16)
        dmixed = dya * u * sil
        lane = lax.broadcasted_iota(jnp.int32, (1, 128), 1)
        dm16 = dmixed.astype(BF16)
        dv2_rows = []
        for c in range(TOK // CHUNK):
            rows = slice(c * CHUNK, (c + 1) * CHUNK)
            cols_out = []
            for h in range(4):
                cols = slice(h * 256, (h + 1) * 256)
                dws_ref[h] += _dot_nt(dm16[rows, cols], v2[rows, cols])
                dbs_ref[...] += jnp.where(lane == h, jnp.sum(dmixed[rows, cols], axis=-1, keepdims=True), 0.0)
                cols_out.append(_dot_tn(ws_ref[h], dm16[rows, cols]))
            dv2_rows.append(jnp.concatenate(cols_out, axis=1))
        dv2 = jnp.concatenate(dv2_rows, axis=0)
        dgv_ref[...] += jnp.sum(dv2 * vh, axis=0, keepdims=True)
        dz_ref[:, 1024:2048] = (_rms_bwd(dv2, gv, rv, vh) * dv1).astype(BF16)

        bx16 = z_ref[:, 3072:4096]
        bg = z_ref[:, 4096:5120].astype(F32)
        counts = _window_counts(i, TOK, s)
        halo = _with_halo(i, n_tiles, zp_ref, bx16, zn_ref)
        ps = [p.astype(BF16) for p in _mixer_b_pooled(bx16.astype(F32), halo, band_ref, counts)]
        pw = jnp.concatenate([_dot(ps[g], wg_ref[g]) for g in range(4)], axis=1)
        sgb = _sigmoid(bg)
        sc = sc_ref[...]
        dz_ref[:, 4096:5120] = (dyb * pw * sc * (sgb * (1.0 + bg * (1.0 - sgb)))).astype(BF16)
        dys = dyb * (bg * sgb)
        dsc_ref[...] += jnp.sum(dys * pw, axis=0, keepdims=True)
        dpw = (dys * sc).astype(BF16)
        for g in range(4):
            cols = slice(g * 256, (g + 1) * 256)
            dwg_ref[g] += _dot_tn(ps[g], dpw[:, cols])
            dpn_ref[:, cols] = (_dot_nt(dpw[:, cols], wg_ref[g]) / counts[g]).astype(BF16)

    prev, nxt = _halo_specs(s, 3)
    row = _rows(TOK, 1024)
    vec = _whole((1, 1024))
    return pl.pallas_call(
        rider.carried_by(body, 11, 7, n_tiles), name="mix0_bwd", grid=(n_tiles,),
        in_specs=[row, _rows(TOK, 5120), prev, nxt, _whole(w_out0.shape), vec, _whole(ws.shape),
                  _whole(bsx.shape), _whole(wg.shape), vec, _whole(band.shape)] + rider.in_specs,
        out_specs=[_rows(TOK, 5120), row, _whole((4, 128, 128)), _whole((128, 128)), vec, vec,
                   _whole((4, 256, 256))] + rider.out_specs,
        out_shape=[jax.ShapeDtypeStruct((s, 5120), BF16), jax.ShapeDtypeStruct((s, 1024), BF16),
                   jax.ShapeDtypeStruct((4, 128, 128), F32), jax.ShapeDtypeStruct((128, 128), F32),
                   jax.ShapeDtypeStruct((1, 1024), F32), jax.ShapeDtypeStruct((1, 1024), F32),
                   jax.ShapeDtypeStruct((4, 256, 256), F32)] + rider.out_shape,
        scratch_shapes=rider.scratch,
        compiler_params=_params(),
    )(dx1, z0, z0, z0, w_out0, gv, ws, bsx, wg, scale, band, *rider.parts)


def _fill_pooled_grad(dz0, dpn, band_t):
    s = dpn.shape[0]
    n_tiles = s // TOK

    def body(dz_in_ref, dpn_ref, dpp_ref, dpx_ref, band_ref, dbx_ref):
        i = pl.program_id(0)
        dpn = dpn_ref[...]
        halo = _with_halo(i, n_tiles, dpp_ref, dpn, dpx_ref)
        counts = _window_counts(i, TOK, s)
        for g in range(4):
            cols = slice(g * 256, (g + 1) * 256)
            dbx = _dot(band_ref[g], halo[:, cols]) - dpn[:, cols].astype(F32) * counts[g]
            dbx_ref[:, cols] = dbx.astype(BF16)

    prev, nxt = _halo_specs(s, 0)
    return pl.pallas_call(
        body, name="fill_pooled_grad", grid=(n_tiles,),
        in_specs=[ANY, _rows(TOK, 1024), prev, nxt, _whole(band_t.shape)],
        out_specs=pl.BlockSpec((TOK, 1024), lambda i: (i, 3)),
        out_shape=jax.ShapeDtypeStruct(dz0.shape, BF16),
        input_output_aliases={0: 0},
        compiler_params=_params(),
    )(dz0, dpn, dpn, dpn, band_t)


def _in_proj0_bwd(dz0, x, dx1, g0, w_in0, rider):
    s = x.shape[0]
    n_tiles = s // TOK

    def body(dz_ref, x_ref, dx1_ref, g_ref, w_ref, dx_ref, gn_ref):
        i = pl.program_id(0)

        @pl.when(i == 0)
        def _():
            gn_ref[...] = jnp.zeros_like(gn_ref)

        dh = _dot_nt(dz_ref[...], w_ref[...])
        g0v = g_ref[...]
        r, xh, _ = _rms_fwd(x_ref[...], g0v)
        gn_ref[...] += jnp.sum(dh * xh, axis=0, keepdims=True)
        dx_ref[...] = dx1_ref[...] + _rms_bwd(dh, g0v, r, xh)

    row = _rows(TOK, 1024)
    vec = _whole((1, 1024))
    return pl.pallas_call(
        rider.carried_by(body, 5, 2, n_tiles), name="in_proj0_bwd", grid=(n_tiles,),
        in_specs=[_rows(TOK, 5120), row, row, vec, _whole(w_in0.shape)] + rider.in_specs,
        out_specs=[row, vec] + rider.out_specs,
        out_shape=[jax.ShapeDtypeStruct((s, 1024), F32), jax.ShapeDtypeStruct((1, 1024), F32)] + rider.out_shape,
        scratch_shapes=rider.scratch,
        compiler_params=_params(),
    )(dz0, x, dx1, g0, w_in0, *rider.parts)


def _weight_grad(a, b, n_blocks, split, name):
    s, k = a.shape
    n = b.shape[1]
    tn = n // n_blocks
    w = tn // split
    ts = 512

    def body(a_ref, b_ref, o_ref):
        @pl.when(pl.program_id(1) == 0)
        def _():
            o_ref[...] = jnp.zeros_like(o_ref)

        res = _dot_tn(a_ref[...], b_ref[...])
        for q in range(split):
            o_ref[q] += res[:, q * w:(q + 1) * w]

    return pl.pallas_call(
        body, name=name, grid=(n_blocks, s // ts),
        in_specs=[pl.BlockSpec((ts, k), lambda j, t: (t, 0)), pl.BlockSpec((ts, tn), lambda j, t: (t, j))],
        out_specs=pl.BlockSpec((split, k, w), lambda j, t: (j, 0, 0)),
        out_shape=jax.ShapeDtypeStruct((n_blocks * split, k, w), F32),
        compiler_params=_params(),
    )(a, b)


def _row_tile(rows, cols):
    t = rows
    while t * cols * 4 > (1 << 20) and t % 16 == 0:
        t //= 2
    return t


def _add_sibling(where, g, theirs, name):
    _, _, rows, cols = g.shape
    t = _row_tile(rows, cols)

    def body(where_ref, g_ref, t_ref, o_ref):
        o_ref[...] = (g_ref[...] + t_ref[...]).astype(BF16)

    spec = pl.BlockSpec((None, t, cols), lambda s, i, p: (s, i, 0))
    return pl.pallas_call(
        body, name=name, out_shape=jax.ShapeDtypeStruct((4, rows, cols), BF16),
        grid_spec=pltpu.PrefetchScalarGridSpec(
            num_scalar_prefetch=1, grid=(4, rows // t),
            in_specs=[pl.BlockSpec((None, None, t, cols), lambda s, i, p: (s, p[1], i, 0)), spec], out_specs=spec),
        compiler_params=_params())(where, g, theirs)


def _sum_chips(where, g, theirs, slots, name):
    _, _, rows, cols = g.shape
    t = _row_tile(rows, cols)

    def body(where_ref, g_ref, t_ref, s_ref, o_ref):
        me = where_ref[0]
        own = g_ref[...] + t_ref[...]
        acc = jnp.where(me == 0, own, s_ref[0].astype(F32))
        for k in range(1, 4):
            acc = acc + jnp.where(me == k, own, s_ref[k].astype(F32))
        o_ref[...] = acc

    return pl.pallas_call(
        body, name=name, out_shape=jax.ShapeDtypeStruct((rows, cols), F32),
        grid_spec=pltpu.PrefetchScalarGridSpec(
            num_scalar_prefetch=1, grid=(rows // t,),
            in_specs=[pl.BlockSpec((None, None, t, cols), lambda i, p: (p[0], p[1], i, 0)),
                      pl.BlockSpec((None, t, cols), lambda i, p: (p[0], i, 0)),
                      pl.BlockSpec((4, t, cols), lambda i, p: (0, i, 0))],
            out_specs=pl.BlockSpec((t, cols), lambda i, p: (i, 0))),
        compiler_params=_params())(where, g, theirs, slots)


def _adamw_halves(where, w, own, theirs, m, v, name):
    rows, cols = own.shape
    t = _row_tile(rows, cols)
    per = rows // t

    def body(where_ref, w_ref, own_ref, th_ref, m_ref, v_ref, g_ref, d_ref, nm_ref, nv_ref):
        g = jnp.where(pl.program_id(0) == where_ref[1], own_ref[...], th_ref[...])
        g_ref[...] = g
        m2 = ADAM_B1 * m_ref[...] + (1.0 - ADAM_B1) * g
        v2 = ADAM_B2 * v_ref[...] + (1.0 - ADAM_B2) * (g * g)
        m_hat = m2 / (1.0 - ADAM_B1 ** ADAM_STEP)
        v_hat = v2 / (1.0 - ADAM_B2 ** ADAM_STEP)
        d_ref[...] = -ADAM_LR * (m_hat / (jnp.sqrt(v_hat) + ADAM_EPS) + ADAM_WD * w_ref[...])
        nm_ref[...] = m2
        nv_ref[...] = v2

    full = pl.BlockSpec((t, cols), lambda h, i, p: (h * per + i, 0))
    half = pl.BlockSpec((t, cols), lambda h, i, p: (i, 0))
    shp = jax.ShapeDtypeStruct(w.shape, F32)
    return pl.pallas_call(
        body, name=name, out_shape=[shp] * 4,
        grid_spec=pltpu.PrefetchScalarGridSpec(
            num_scalar_prefetch=1, grid=(2, per), in_specs=[full, half, half, full, full], out_specs=[full] * 4),
        compiler_params=_params())(where, w, own, theirs, m, v)


def _place_shard(where, w, cut, name):
    if cut.kind == "cols":
        r, n = cut.full_shape
        blk, grid = (256, n // 4), (r // 256,)
        src_map, dst_map = (lambda i, p: (i, 0)), (lambda i, p: (i, p[0]))
    elif cut.kind == "rows":
        r, n = cut.full_shape
        per = r // 4 // 256
        blk, grid = (256, n), (per,)
        src_map, dst_map = (lambda i, p: (i, 0)), (lambda i, p: (p[0] * per + i, 0))
    else:
        g, r, n = cut.full_shape
        blk, grid = (g, r // 4, n), (1,)
        src_map, dst_map = (lambda i, p: (0, 0, 0)), (lambda i, p: (0, p[0], 0))

    def body(where_ref, w_ref, o_ref):
        o_ref[...] = w_ref[...].astype(BF16)

    return pl.pallas_call(
        body, name=name, out_shape=jax.ShapeDtypeStruct(cut.full_shape, BF16),
        grid_spec=pltpu.PrefetchScalarGridSpec(
            num_scalar_prefetch=1, grid=grid, in_specs=[pl.BlockSpec(blk, src_map)],
            out_specs=pl.BlockSpec(blk, dst_map)),
        compiler_params=_params())(where, w)


def _sum_small(first, second, third):
    rows = second.shape[1]

    def body(a_ref, b_ref, c_ref, o_ref):
        top = a_ref[0] + c_ref[0] + b_ref[0, 0:8]
        rest = b_ref[0, 8:rows]
        for k in range(1, 8):
            top = top + (a_ref[k] + c_ref[k] + b_ref[k, 0:8])
            rest = rest + b_ref[k, 8:rows]
        o_ref[0:8] = top
        o_ref[8:rows] = rest

    return pl.pallas_call(
        body, name="sum_small", in_specs=[_whole(first.shape), _whole(second.shape), _whole(third.shape)],
        out_specs=_whole(second.shape[1:]), out_shape=jax.ShapeDtypeStruct(second.shape[1:], F32),
        compiler_params=_params())(first, second, third)


def _adamw(w, g, m, v, name):
    rows, cols = w.shape
    t = _row_tile(rows, cols)

    def body(w_ref, g_ref, m_ref, v_ref, d_ref, nm_ref, nv_ref):
        g = g_ref[...]
        m2 = ADAM_B1 * m_ref[...] + (1.0 - ADAM_B1) * g
        v2 = ADAM_B2 * v_ref[...] + (1.0 - ADAM_B2) * (g * g)
        m_hat = m2 / (1.0 - ADAM_B1 ** ADAM_STEP)
        v_hat = v2 / (1.0 - ADAM_B2 ** ADAM_STEP)
        d_ref[...] = -ADAM_LR * (m_hat / (jnp.sqrt(v_hat) + ADAM_EPS) + ADAM_WD * w_ref[...])
        nm_ref[...] = m2
        nv_ref[...] = v2

    spec = pl.BlockSpec((t, cols), lambda i: (i, 0))
    shp = jax.ShapeDtypeStruct(w.shape, F32)
    return pl.pallas_call(body, name=name, grid=(rows // t,), in_specs=[spec] * 4, out_specs=[spec] * 3,
                          out_shape=[shp] * 3, compiler_params=_params())(w, g, m, v)


def _place():
    x, y, c = lax.axis_index("x"), lax.axis_index("y"), lax.axis_index("c")
    chips = [(1 - x, y), (x, 1 - y), (1 - x, 1 - y)]
    return x, y, c, chips


class _Sharded:
    def __init__(self, kind, full_shape):
        self.kind = kind
        self.full_shape = full_shape

    def in_full(self, ref, s, h):
        if self.kind == "cols":
            r, n = self.full_shape
            return ref.at[pl.ds(h * (r // 2), r // 2), pl.ds(pl.multiple_of(s * (n // 4), 128), n // 4)]
        if self.kind == "rows":
            r, _ = self.full_shape
            return ref.at[pl.ds(pl.multiple_of(s * (r // 4) + h * (r // 8), 8), r // 8), :]
        g, r, _ = self.full_shape
        return ref.at[pl.ds(h * (g // 2), g // 2), pl.ds(pl.multiple_of(s * (r // 4), 16), r // 4), :]


def _remote(src, dst, send_sem, recv_sem, to):
    return pltpu.make_async_remote_copy(src_ref=src, dst_ref=dst, send_sem=send_sem, recv_sem=recv_sem,
                                        device_id=to, device_id_type=MESH)


def _start_remote(src, dst, send_sem, recv_sem, to):
    cp = _remote(src, dst, send_sem, recv_sem, to)
    cp.start()
    return cp


class _Gather:
    def __init__(self, fulls, cuts):
        n = len(fulls)
        self.fulls, self.cuts = list(fulls), list(cuts)
        self.in_specs = [ANY] * n
        self.out_specs = [ANY] * n
        self.out_shape = [jax.ShapeDtypeStruct(cut.full_shape, BF16) for cut in cuts]
        self.scratch = [pltpu.SemaphoreType.DMA((6 * n,)), pltpu.SemaphoreType.DMA((6 * n,))]

    def _step(self, step, src, out, send_sems, recv_sems):
        n, cuts = len(self.fulls), self.cuts
        x, y, c, chips = _place()
        me = 2 * x + y

        def ends(w, s, h, from_src):
            dst = cuts[w].in_full(out[w], s, h)
            return (cuts[w].in_full(src[w], s, h) if from_src else dst), dst

        for w in range(n):
            for j, chip in enumerate(chips):
                s = 2 * chip[0] + chip[1]
                k, k2 = 3 * w + j, 3 * n + 3 * w + j
                if step == "send":
                    _start_remote(*ends(w, me, c, True), send_sems.at[k], recv_sems.at[k], (*chip, c))
                elif step == "pass_on":
                    _remote(*ends(w, s, c, False), send_sems.at[k], recv_sems.at[k], (x, y, c)).wait_recv()
                    _start_remote(*ends(w, s, c, False), send_sems.at[k2], recv_sems.at[k2], (x, y, 1 - c))
                else:
                    _remote(*ends(w, s, 1 - c, False), send_sems.at[k2], recv_sems.at[k2], (x, y, c)).wait_recv()
                    _remote(*ends(w, me, c, True), send_sems.at[k], recv_sems.at[k], (x, y, c)).wait_send()
                    _remote(*ends(w, s, c, False), send_sems.at[k2], recv_sems.at[k2], (x, y, c)).wait_send()

    def carried_by(self, body, n_in, n_out, n_steps, step_index=lambda: pl.program_id(0)):
        k = len(self.fulls)

        def carrier(*refs):
            ins, src = refs[:n_in], refs[n_in:n_in + k]
            outs, out = refs[n_in + k:n_in + k + n_out], refs[n_in + k + n_out:n_in + 2 * k + n_out]
            sems = refs[n_in + 2 * k + n_out:]
            for step, at in (("send", 0), ("pass_on", 3 * n_steps // 4), ("finish", n_steps - 1)):
                if step == "finish":
                    body(*ins, *outs)

                @pl.when(step_index() == at)
                def _():
                    self._step(step, src, out, *sems)

        return carrier


def _exchange_halves(grads, name):
    n = len(grads)

    def body(*refs):
        g = refs[:n]
        theirs = refs[n:2 * n]
        send_sems, recv_sems = refs[2 * n:]
        x, y, c, _ = _place()
        sends = [_start_remote(g[w].at[:, 1 - c], theirs[w], send_sems.at[w], recv_sems.at[w], (x, y, 1 - c))
                 for w in range(n)]
        for w in range(n):
            _remote(g[w].at[:, 1 - c], theirs[w], send_sems.at[w], recv_sems.at[w], (x, y, c)).wait_recv()
        for cp in sends:
            cp.wait_send()

    return pl.pallas_call(
        body, name=name,
        in_specs=[ANY] * n, out_specs=[ANY] * n,
        out_shape=[jax.ShapeDtypeStruct((4,) + g.shape[2:], F32) for g in grads],
        scratch_shapes=[pltpu.SemaphoreType.DMA((n,)), pltpu.SemaphoreType.DMA((n,))],
        compiler_params=pltpu.CompilerParams(has_side_effects=True),
    )(*grads)


def _gather_small(small_ref, gathered, send_sems, recv_sems, first_sem, local_sem, start):
    x, y, c, _ = _place()
    me = 4 * x + 2 * y + c
    flips = [(fx, fy, fc) for fx in range(2) for fy in range(2) for fc in range(2)][1:]
    own = pltpu.make_async_copy(small_ref, gathered.at[me], local_sem)
    if start:
        own.start()
    else:
        own.wait()
    for k, (fx, fy, fc) in enumerate(flips):
        peer = (x + fx - 2 * x * fx, y + fy - 2 * y * fy, c + fc - 2 * c * fc)
        sems = (send_sems.at[first_sem + k], recv_sems.at[first_sem + k])
        if start:
            _start_remote(small_ref, gathered.at[me], *sems, peer)
        else:
            cp = _remote(small_ref, gathered.at[4 * peer[0] + 2 * peer[1] + peer[2]], *sems, (x, y, c))
            cp.wait_recv()
            cp.wait_send()


class _ScatterRider:
    def __init__(self, parts, small):
        n = len(parts)
        self.n = n
        self.parts = list(parts) + [small]
        self.in_specs = [ANY] * (n + 1)
        self.out_specs = [ANY] * (n + 1)
        self.out_shape = ([jax.ShapeDtypeStruct(a.shape, a.dtype) for a in parts]
                          + [jax.ShapeDtypeStruct((8,) + small.shape, small.dtype)])
        self.scratch = [pltpu.SemaphoreType.DMA((3 * n + 7,)), pltpu.SemaphoreType.DMA((3 * n + 7,)),
                        pltpu.SemaphoreType.DMA]

    def _copies(self, p, out, send_sems, recv_sems, local_sem, start):
        x, y, c, chips = _place()
        me = 2 * x + y
        n = self.n
        _gather_small(p[n], out[n], send_sems, recv_sems, 3 * n, local_sem, start)
        for w in range(n):
            for j, chip in enumerate(chips):
                s = 2 * chip[0] + chip[1]
                if start:
                    _start_remote(p[w].at[s], out[w].at[me], send_sems.at[3 * w + j], recv_sems.at[3 * w + j],
                                  (*chip, c))
                else:
                    cp = _remote(p[w].at[s], out[w].at[s], send_sems.at[3 * w + j], recv_sems.at[3 * w + j],
                                 (x, y, c))
                    cp.wait_recv()
                    cp.wait_send()

    def carried_by(self, body, n_in, n_out, n_tiles):
        k = len(self.parts)

        def carrier(*refs):
            ins, mine = refs[:n_in], refs[n_in:n_in + k]
            outs, slots = refs[n_in + k:n_in + k + n_out], refs[n_in + k + n_out:n_in + 2 * k + n_out]
            sems = refs[n_in + 2 * k + n_out:]

            @pl.when(pl.program_id(0) == 0)
            def _():
                self._copies(mine, slots, *sems, start=True)

            body(*ins, *outs)

            @pl.when(pl.program_id(0) == n_tiles - 1)
            def _():
                self._copies(mine, slots, *sems, start=False)

        return carrier


def _share_halves(halves, small):
    n = len(halves)

    def body(*refs):
        hv = refs[:n]
        small_ref = refs[n]
        out = refs[n + 1:2 * n + 1]
        gathered = refs[2 * n + 1]
        send_sems, recv_sems, local_sem = refs[2 * n + 2:]
        x, y, c, _ = _place()
        sends = [_start_remote(hv[w], out[w], send_sems.at[w], recv_sems.at[w], (x, y, 1 - c)) for w in range(n)]
        _gather_small(small_ref, gathered, send_sems, recv_sems, n, local_sem, True)
        for w in range(n):
            _remote(hv[w], out[w], send_sems.at[w], recv_sems.at[w], (x, y, c)).wait_recv()
        for cp in sends:
            cp.wait_send()
        _gather_small(small_ref, gathered, send_sems, recv_sems, n, local_sem, False)

    return pl.pallas_call(
        body, name="share_halves",
        in_specs=[ANY] * (n + 1), out_specs=[ANY] * (n + 1),
        out_shape=[jax.ShapeDtypeStruct(a.shape, F32) for a in halves] + [jax.ShapeDtypeStruct((8,) + small.shape, F32)],
        scratch_shapes=[pltpu.SemaphoreType.DMA((n + 7,)), pltpu.SemaphoreType.DMA((n + 7,)),
                        pltpu.SemaphoreType.DMA],
        compiler_params=pltpu.CompilerParams(has_side_effects=True),
    )(*halves, small)


SMALL_ROWS = 80


def _pack_small(vecs, ws, bs, sink):
    ws = jnp.zeros((64, 1024), F32) if ws is None else ws.reshape(64, 1024)
    bs = jnp.zeros((1, 512), F32) if bs is None else bs.reshape(1, 512)
    sink = jnp.zeros((1, 16), F32) if sink is None else sink.reshape(1, 16)
    top = jnp.concatenate(
        [v.reshape(1, 1024) for v in vecs]
        + [jnp.pad(bs, ((0, 0), (0, 512))), jnp.pad(sink, ((0, 0), (0, 1008))), jnp.zeros((1, 1024), F32)], axis=0)
    return jnp.concatenate([top, ws, jnp.zeros((8, 1024), F32)], axis=0)


def _unpack_small(p):
    vecs = [p[k] for k in range(5)]
    return vecs, p[8:72].reshape(4, 128, 128), p[5, :512].reshape(4, 128), p[6, :16]


def kernel(x, norm_0, w_in_0, a_v_norm_0, a_spatial_w_0, a_spatial_b_0, b_group_w_0, b_scale_0, w_out_0, norm_1, w_in_1, sink_1, w_out_1, final_norm, loss_target, m_norm_0, m_w_in_0, m_a_v_norm_0, m_a_spatial_w_0, m_a_spatial_b_0, m_b_group_w_0, m_b_scale_0, m_w_out_0, m_norm_1, m_w_in_1, m_sink_1, m_w_out_1, m_final_norm, v_norm_0, v_w_in_0, v_a_v_norm_0, v_a_spatial_w_0, v_a_spatial_b_0, v_b_group_w_0, v_b_scale_0, v_w_out_0, v_norm_1, v_w_in_1, v_sink_1, v_w_out_1, v_final_norm):
    s = x.shape[1]
    xs = x.reshape(s, D_MODEL)
    target = loss_target.reshape(s, D_MODEL)

    cuts = [_Sharded("cols", (1024, 5120)), _Sharded("rows", (2048, 1024)), _Sharded("cols", (1024, 2560)),
            _Sharded("rows", (1024, 1024)), _Sharded("mid", (4, 256, 256))]
    big_w = [w_in_0, w_out_0, w_in_1, w_out_1, b_group_w_0]
    big_m = [m_w_in_0, m_w_out_0, m_w_in_1, m_w_out_1, m_b_group_w_0]
    big_v = [v_w_in_0, v_w_out_0, v_w_in_1, v_w_out_1, v_b_group_w_0]
    where = jnp.stack([2 * lax.axis_index("x") + lax.axis_index("y"), lax.axis_index("c")]).astype(jnp.int32)
    placed = [_place_shard(where, w, cut, f"place_shard{k}") for k, (w, cut) in enumerate(zip(big_w, cuts))]
    cx, cy = lax.axis_index("x"), lax.axis_index("y")
    order = jnp.stack([2 * cx + cy, 2 * (1 - cx) + cy, 2 * cx + 1 - cy, 2 * (1 - cx) + 1 - cy]).astype(jnp.int32)

    row = lambda v: v.reshape(1, 1024)
    ws16 = a_spatial_w_0.astype(BF16)
    bsx = jnp.repeat(a_spatial_b_0.T, 256, axis=1)
    band, band_t = _band_matrices(TOK)
    rope = _rope_tables(s)

    h0, z0, w_in0 = _in_proj0_own(order, xs, row(norm_0), w_in_0.astype(BF16), _Gather(placed[:1], cuts[:1]))
    z0, w_out0, wg = _in_proj0_rest(order, h0, w_in0, z0, _Gather([placed[1], placed[4]], [cuts[1], cuts[4]]))
    cat, x1, w_in1, w_out1 = _mix0_fwd(xs, z0, w_out0, row(a_v_norm_0), ws16, bsx, wg, row(b_scale_0), band,
                                       _Gather(placed[2:4], cuts[2:4]))
    h1, q, k, v, gate = _in_proj1(x1, row(norm_1), w_in1, rope)
    kpad = jnp.pad(k, ((ATTN_WINDOW, ATTN_WINDOW), (0, 0)))
    vpad = jnp.pad(v, ((ATTN_WINDOW, ATTN_WINDOW), (0, 0)))
    o = _attn_fwd(q, kpad, vpad, sink_1)
    y1, dx2, do, dgate, loss_lanes, g_final, dx2h = _tail(x1, o, gate, target, w_out1, row(final_norm))

    dq, dkpad, dvpad, dsink = _attn_bwd(q, kpad, vpad, sink_1, o, do, rope)
    dz1, dx1, g_norm1, dx1h = _in_proj1_bwd(dq, dkpad, dvpad, dgate, x1, dx2, row(norm_1), w_in1, rope)

    g_w_in1 = _weight_grad(h1, dz1, 2, 2, "grad_w_in1").reshape(4, 2, 512, 640)
    g_w_out1 = _weight_grad(y1, dx2h, 1, 1, "grad_w_out1").reshape(4, 2, 128, 1024)
    g_w_out0 = _weight_grad(cat, dx1h, 1, 1, "grad_w_out0").reshape(4, 2, 256, 1024)
    first = [g_w_out0, g_w_in1, g_w_out1]
    theirs1 = _exchange_halves(first, "exchange_halves1")
    parts1 = [_add_sibling(where, g, t, f"add_sibling1_{k}") for k, (g, t) in enumerate(zip(first, theirs1))]
    zero = jnp.zeros((1024,), F32)
    small1 = _pack_small([zero, zero, zero, g_norm1, g_final], None, None, dsink[0, :16])[:8]
    dz0, dpn, d_ws, d_bs, d_gv, d_scale, d_wg, *slots1, small1_all = _mix0_bwd(
        dx1h, z0, w_out0, row(a_v_norm_0), ws16, bsx, wg, row(b_scale_0), band, _ScatterRider(parts1, small1))
    dz0 = _fill_pooled_grad(dz0, dpn, band_t)
    g_w_in0 = _weight_grad(h0, dz0, 4, 1, "grad_w_in0").reshape(4, 2, 512, 1280)
    g_wg = d_wg.reshape(2, 2, 4, 64, 256).transpose(2, 0, 1, 3, 4).reshape(4, 2, 128, 256)
    second = [g_w_in0, g_wg]
    theirs2 = _exchange_halves(second, "exchange_halves2")
    parts2 = [_add_sibling(where, g, t, f"add_sibling2_{k}") for k, (g, t) in enumerate(zip(second, theirs2))]
    small2 = _pack_small([zero, d_gv, d_scale, zero, zero], d_ws, d_bs[:, :4].T, None)
    grad_x, g_norm0, *slots2, small2_all = _in_proj0_bwd(
        dz0, xs, dx1, row(norm_0), w_in0, _ScatterRider(parts2, small2))

    grads = [g_w_in0, g_w_out0, g_w_in1, g_w_out1, g_wg]
    theirs = [theirs2[0], theirs1[0], theirs1[1], theirs1[2], theirs2[1]]
    slots = [slots2[0], slots1[0], slots1[1], slots1[2], slots2[1]]
    n = len(grads)
    reduced = [_sum_chips(where, grads[w], theirs[w], slots[w], f"sum_chips{w}") for w in range(n)]
    small3 = _pack_small([g_norm0, zero, zero, zero, zero], None, None, None)[:8]
    *from_sibling, small3_all = _share_halves(reduced, small3)

    out_g, out_d, out_m, out_v = {}, {}, {}, {}
    names = ["w_in_0", "w_out_0", "w_in_1", "w_out_1", "b_group_w_0"]
    for w in range(n):
        shape = big_w[w].shape
        two_d = (-1, shape[-1])
        outs = _adamw_halves(where, big_w[w].reshape(two_d), reduced[w], from_sibling[w], big_m[w].reshape(two_d),
                             big_v[w].reshape(two_d), f"adamw{w}")
        out_g[names[w]], out_d[names[w]], out_m[names[w]], out_v[names[w]] = (a.reshape(shape) for a in outs)

    g_small = _sum_small(small1_all, small2_all, small3_all)
    small_names = ["norm_0", "a_v_norm_0", "b_scale_0", "norm_1", "final_norm"]
    pack = lambda vecs, ws_, bs_, sk: _pack_small(vecs, ws_, bs_, sk)
    w_small = pack([norm_0, a_v_norm_0, b_scale_0, norm_1, final_norm], a_spatial_w_0, a_spatial_b_0, sink_1)
    m_small = pack([m_norm_0, m_a_v_norm_0, m_b_scale_0, m_norm_1, m_final_norm], m_a_spatial_w_0,
                   m_a_spatial_b_0, m_sink_1)
    v_small = pack([v_norm_0, v_a_v_norm_0, v_b_scale_0, v_norm_1, v_final_norm], v_a_spatial_w_0,
                   v_a_spatial_b_0, v_sink_1)
    d_small, nm_small, nv_small = _adamw(w_small, g_small, m_small, v_small, "adamw_small")
    for store, packed in ((out_g, g_small), (out_d, d_small), (out_m, nm_small), (out_v, nv_small)):
        vecs, ws_, bs_, sk = _unpack_small(packed)
        for name, vec in zip(small_names, vecs):
            store[name] = vec
        store["a_spatial_w_0"], store["a_spatial_b_0"], store["sink_1"] = ws_, bs_, sk

    loss = lax.psum(jnp.sum(loss_lanes), ("x", "y", "c"))
    order = ["norm_0", "w_in_0", "a_v_norm_0", "a_spatial_w_0", "a_spatial_b_0", "b_group_w_0", "b_scale_0",
             "w_out_0", "norm_1", "w_in_1", "sink_1", "w_out_1", "final_norm"]
    return (loss, grad_x.reshape(1, s, D_MODEL), *[out_g[k] for k in order], *[out_d[k] for k in order],
            *[out_m[k] for k in order], *[out_v[k] for k in order])
```

```python
import functools

import numpy as np
import jax
import jax.numpy as jnp
from jax import lax
from jax.experimental import pallas as pl
from jax.experimental.pallas import tpu as pltpu

F32 = jnp.float32
BF16 = jnp.bfloat16
MESH = pl.DeviceIdType.MESH

D_MODEL = 1024
EPS = 1e-6
NEG_INF = -1e30
CHUNK = 128
POOL_WINDOWS = (2, 4, 8, 16)
HALO = 16
N_HEADS = 16
HEAD_DIM = 64
ATTN_WINDOW = 128
ROPE_THETA = 500000.0
ROT_DIM = 16
ADAM_LR = 0.001
ADAM_B1 = 0.9
ADAM_B2 = 0.999
ADAM_EPS = 1e-08
ADAM_WD = 0.01
ADAM_STEP = 10

TOK = 256
VMEM_LIMIT = 56 * 1024 * 1024


def _params(**kw):
    return pltpu.CompilerParams(vmem_limit_bytes=VMEM_LIMIT, **kw)


def _whole(shape):
    nd = len(shape)
    return pl.BlockSpec(shape, lambda *_: (0,) * nd)


def _rows(t, n):
    return pl.BlockSpec((t, n), lambda i: (i, 0))


ANY = pl.BlockSpec(memory_space=pl.ANY)

_G0 = 0.7978845608028654
_G1 = 0.044715


def _gelu(x):
    return 0.5 * x * (1.0 + jnp.tanh(_G0 * (x + _G1 * x * x * x)))


def _gelu_and_grad(x):
    x2 = x * x
    t = jnp.tanh(_G0 * (x + _G1 * x2 * x))
    half = 0.5 * (1.0 + t)
    return x * half, half + 0.5 * x * (1.0 - t * t) * (_G0 * (1.0 + 3.0 * _G1 * x2))


def _sigmoid(x):
    return 1.0 / (1.0 + jnp.exp(-x))


def _dot(a, b):
    return jnp.dot(a, b, preferred_element_type=F32)


def _dot_nt(a, b):
    return lax.dot_general(a, b, (((1,), (1,)), ((), ())), preferred_element_type=F32)


def _dot_tn(a, b):
    return lax.dot_general(a, b, (((0,), (0,)), ((), ())), preferred_element_type=F32)


def _rms_fwd(x, g):
    r = lax.rsqrt(jnp.mean(x * x, axis=-1, keepdims=True) + EPS)
    xh = x * r
    return r, xh, xh * g


def _rms_bwd(dy, g, r, xh):
    dxh = dy * g
    return r * (dxh - xh * jnp.mean(dxh * xh, axis=-1, keepdims=True))


def _band_matrices(t):
    r = np.arange(t)[:, None]
    j = np.arange(t + 2 * HALO)[None, :]
    fwd, bwd = [], []
    for w in POOL_WINDOWS:
        d = j - r - HALO
        fwd.append((d >= -(w // 2)) & (d < w // 2))
        bwd.append((d >= -(w // 2) + 1) & (d <= w // 2))
    return (jnp.asarray(np.stack(fwd), BF16), jnp.asarray(np.stack(bwd), BF16))


def _window_counts(i, t, s):
    tok = i * t + lax.broadcasted_iota(jnp.int32, (t, 1), 0)
    out = []
    for w in POOL_WINDOWS:
        cnt = jnp.minimum(tok + w // 2, s) - jnp.maximum(tok - w // 2, 0)
        out.append(cnt.astype(F32))
    return out


def _rope_tables(s):
    inv = np.float32(ROPE_THETA) ** (-np.arange(0, ROT_DIM, 2, dtype=np.float32) / np.float32(ROT_DIM))
    ang = np.arange(s, dtype=np.float32)[:, None] * inv.astype(np.float32)[None, :]
    cos, sin = np.cos(ang).astype(np.float32), np.sin(ang).astype(np.float32)
    z8 = np.zeros((s, 8), np.float32)
    z48 = np.zeros((s, HEAD_DIM - ROT_DIM), np.float32)
    c = np.concatenate([cos, cos, np.ones((s, HEAD_DIM - ROT_DIM), np.float32)], axis=1)
    s_lo = np.concatenate([z8, sin, z48], axis=1)
    s_hi = np.concatenate([-sin, z8, z48], axis=1)
    return tuple(jnp.asarray(np.concatenate([a, a], axis=1)) for a in (c, s_lo, s_hi))


def _rope(x, c, s_lo, s_hi):
    n = x.shape[1]
    reps = n // 128
    c, s_lo, s_hi = (jnp.tile(a, (1, reps)) for a in (c, s_lo, s_hi))
    return x * c + pltpu.roll(x, 8, 1) * s_lo + pltpu.roll(x, n - 8, 1) * s_hi


def _rope_t(dx, c, s_lo, s_hi):
    n = dx.shape[1]
    reps = n // 128
    c, s_lo, s_hi = (jnp.tile(a, (1, reps)) for a in (c, s_lo, s_hi))
    return dx * c + pltpu.roll(dx * s_lo, n - 8, 1) + pltpu.roll(dx * s_hi, 8, 1)


def _in_proj0_own(order, x, g0, w_own, rider):
    s = x.shape[0]
    n = w_own.shape[1]

    def body(order_ref, x_ref, g_ref, w_ref, h_ref, z_ref):
        _, _, h = _rms_fwd(x_ref[...], g_ref[...])
        h = h.astype(BF16)
        h_ref[...] = h
        z_ref[...] = _dot(h, w_ref[...]).astype(BF16)

    return pl.pallas_call(
        rider.carried_by(body, 4, 2, s // TOK), name="in_proj0_own",
        grid_spec=pltpu.PrefetchScalarGridSpec(
            num_scalar_prefetch=1, grid=(s // TOK,),
            in_specs=[pl.BlockSpec((TOK, D_MODEL), lambda i, o: (i, 0)), pl.BlockSpec((1, D_MODEL), lambda i, o: (0, 0)),
                      pl.BlockSpec(w_own.shape, lambda i, o: (0, 0))] + rider.in_specs,
            out_specs=[pl.BlockSpec((TOK, D_MODEL), lambda i, o: (i, 0)),
                       pl.BlockSpec((TOK, n), lambda i, o: (i, o[0]))] + rider.out_specs,
            scratch_shapes=rider.scratch),
        out_shape=[jax.ShapeDtypeStruct((s, D_MODEL), BF16), jax.ShapeDtypeStruct((s, 4 * n), BF16)] + rider.out_shape,
        input_output_aliases={4 + j: 2 + j for j in range(len(rider.fulls))},
        compiler_params=_params(),
    )(order, x, g0, w_own, *rider.fulls)


def _in_proj0_rest(order, h0, w_in0, z0, rider):
    s = h0.shape[0]
    tok = min(s, 4 * TOK)
    n_tiles = s // tok
    n = w_in0.shape[1] // 4

    def body(order_ref, h_ref, w_ref, z_in_ref, z_ref):
        z_ref[...] = _dot(h_ref[...], w_ref[...]).astype(BF16)

    return pl.pallas_call(
        rider.carried_by(body, 4, 1, 3 * n_tiles, lambda: pl.program_id(0) * n_tiles + pl.program_id(1)),
        name="in_proj0_rest",
        grid_spec=pltpu.PrefetchScalarGridSpec(
            num_scalar_prefetch=1, grid=(3, n_tiles),
            in_specs=[pl.BlockSpec((tok, D_MODEL), lambda k, i, o: (i, 0)),
                      pl.BlockSpec((w_in0.shape[0], n), lambda k, i, o: (0, o[1 + k])), ANY] + rider.in_specs,
            out_specs=[pl.BlockSpec((tok, n), lambda k, i, o: (i, o[1 + k]))] + rider.out_specs,
            scratch_shapes=rider.scratch),
        out_shape=[jax.ShapeDtypeStruct(z0.shape, BF16)] + rider.out_shape,
        input_output_aliases={3: 0, **{4 + j: 1 + j for j in range(len(rider.fulls))}},
        compiler_params=_params(),
    )(order, h0, w_in0, z0, *rider.fulls)


def _halo_specs(s, col_block):
    per = TOK // HALO
    last = s // HALO - 1
    prev = pl.BlockSpec((HALO, 1024), lambda i: (jnp.maximum(i * per - 1, 0), col_block))
    nxt = pl.BlockSpec((HALO, 1024), lambda i: (jnp.minimum((i + 1) * per, last), col_block))
    return prev, nxt


def _with_halo(i, n_tiles, prev_ref, cur, next_ref):
    prev = prev_ref[...]
    nxt = next_ref[...]
    prev = jnp.where(i > 0, prev, jnp.zeros_like(prev))
    nxt = jnp.where(i < n_tiles - 1, nxt, jnp.zeros_like(nxt))
    return jnp.concatenate([prev, cur, nxt], axis=0)


def _mixer_a(v1, gv, ws_ref, bsx):
    rv, vh, v2 = _rms_fwd(v1, gv)
    v2 = v2.astype(BF16)
    rows = []
    for c in range(v1.shape[0] // CHUNK):
        cols = [_dot(ws_ref[h], v2[c * CHUNK:(c + 1) * CHUNK, h * 256:(h + 1) * 256]) for h in range(4)]
        rows.append(jnp.concatenate(cols, axis=1) + bsx)
    return rv, vh, v2, jnp.concatenate(rows, axis=0)


def _mixer_b_pooled(bx, halo, band_ref, counts):
    out = []
    for g in range(4):
        win = _dot(band_ref[g], halo[:, g * 256:(g + 1) * 256])
        out.append(win / counts[g] - bx[:, g * 256:(g + 1) * 256])
    return out


def _mix0_fwd(x, z0, w_out0, gv, ws, bsx, wg, scale, band, rider):
    s = x.shape[0]
    n_tiles = s // TOK
    k = len(rider.fulls)

    def body(z_ref, zp_ref, zn_ref, x_ref, wout_ref, gv_ref, ws_ref, bsx_ref, wg_ref, sc_ref, band_ref,
             cat_ref, x1_ref):
        i = pl.program_id(0)
        au = z_ref[:, 0:1024].astype(F32)
        av = z_ref[:, 1024:2048].astype(F32)
        ag = z_ref[:, 2048:3072].astype(F32)
        _, _, _, mixed = _mixer_a(_gelu(av), gv_ref[...], ws_ref, bsx_ref[...])
        cat_ref[:, 0:1024] = (_gelu(au) * mixed * (ag * _sigmoid(ag))).astype(BF16)

        bx16 = z_ref[:, 3072:4096]
        bg = z_ref[:, 4096:5120].astype(F32)
        halo = _with_halo(i, n_tiles, zp_ref, bx16, zn_ref)
        ps = _mixer_b_pooled(bx16.astype(F32), halo, band_ref, _window_counts(i, TOK, s))
        pw = jnp.concatenate([_dot(ps[g].astype(BF16), wg_ref[g]) for g in range(4)], axis=1)
        cat_ref[:, 1024:2048] = (pw * sc_ref[...] * (bg * _sigmoid(bg))).astype(BF16)

        x1_ref[...] = x_ref[...] + _dot(cat_ref[...], wout_ref[...])

    prev, nxt = _halo_specs(s, 3)
    return pl.pallas_call(
        rider.carried_by(body, 11, 2, n_tiles), name="mix0_fwd", grid=(n_tiles,),
        in_specs=[_rows(TOK, 5120), prev, nxt, _rows(TOK, D_MODEL), _whole(w_out0.shape), _whole((1, 1024)),
                  _whole(ws.shape), _whole(bsx.shape), _whole(wg.shape), _whole((1, 1024)), _whole(band.shape)]
        + rider.in_specs,
        out_specs=[_rows(TOK, 2048), _rows(TOK, D_MODEL)] + rider.out_specs,
        out_shape=[jax.ShapeDtypeStruct((s, 2048), BF16), jax.ShapeDtypeStruct((s, D_MODEL), F32)] + rider.out_shape,
        input_output_aliases={11 + j: 2 + j for j in range(k)},
        scratch_shapes=rider.scratch,
        compiler_params=_params(),
    )(z0, z0, z0, x, w_out0, gv, ws, bsx, wg, scale, band, *rider.fulls)


def _in_proj1(x1, g1, w_in1, rope):
    s = x1.shape[0]

    def body(x_ref, g_ref, w_ref, c_ref, lo_ref, hi_ref, h_ref, q_ref, k_ref, v_ref, gate_ref):
        _, _, h = _rms_fwd(x_ref[...], g_ref[...])
        h = h.astype(BF16)
        h_ref[...] = h
        tabs = (c_ref[...], lo_ref[...], hi_ref[...])
        q_ref[...] = (_rope(_dot(h, w_ref[:, 0:1024]), *tabs) * Q_SCALE).astype(BF16)
        kv = _dot(h, w_ref[:, 1024:1536])
        k_ref[...] = _rope(kv[:, 0:256], *tabs).astype(BF16)
        v_ref[...] = kv[:, 256:512].astype(BF16)
        gate_ref[...] = _dot(h, w_ref[:, 1536:2560]).astype(BF16)

    tab = _rows(TOK, 128)
    return pl.pallas_call(
        body, name="in_proj1", grid=(s // TOK,),
        in_specs=[_rows(TOK, D_MODEL), _whole((1, D_MODEL)), _whole(w_in1.shape), tab, tab, tab],
        out_specs=[_rows(TOK, 1024), _rows(TOK, 1024), _rows(TOK, 256), _rows(TOK, 256), _rows(TOK, 1024)],
        out_shape=[jax.ShapeDtypeStruct((s, 1024), BF16), jax.ShapeDtypeStruct((s, 1024), BF16),
                   jax.ShapeDtypeStruct((s, 256), BF16), jax.ShapeDtypeStruct((s, 256), BF16),
                   jax.ShapeDtypeStruct((s, 1024), BF16)],
        compiler_params=_params(),
    )(x1, g1, w_in1, *rope)


QBLK = 128
KBLK = QBLK + 2 * ATTN_WINDOW
Q_SCALE = HEAD_DIM ** -0.5
HEAD_BATCH = 4


def _block_bias(q0, s):
    r = lax.broadcasted_iota(jnp.int32, (QBLK, KBLK), 0)
    c = lax.broadcasted_iota(jnp.int32, (QBLK, KBLK), 1)
    kj = q0 - ATTN_WINDOW + c
    ok = (c >= r) & (c <= r + 2 * ATTN_WINDOW) & (kj >= 0) & (kj < s)
    return jnp.where(ok, 0.0, NEG_INF)


def _attn_fwd(q, kpad, vpad, sink):
    s = q.shape[0]

    def body(sink_ref, q_ref, k_ref, v_ref, o_ref, lse_ref):
        i = pl.program_id(0)
        lane = lax.broadcasted_iota(jnp.int32, (1, 128), 1)
        for b in range(TOK // QBLK):
            rows = slice(b * QBLK, (b + 1) * QBLK)
            start = pl.multiple_of(i * TOK + b * QBLK, QBLK)
            kb = k_ref[pl.ds(start, KBLK), :]
            vb = v_ref[pl.ds(start, KBLK), :]
            bias = _block_bias(i * TOK + b * QBLK, s)
            kv = lambda t, h: t[:, (h // 4) * HEAD_DIM:(h // 4 + 1) * HEAD_DIM]
            scs = [_dot_nt(q_ref[rows, h * HEAD_DIM:(h + 1) * HEAD_DIM], kv(kb, h)) + bias for h in range(N_HEADS)]
            ms = [jnp.maximum(jnp.max(scs[h], axis=-1, keepdims=True), sink_ref[h]) for h in range(N_HEADS)]
            es = [jnp.exp(scs[h] - ms[h]) for h in range(N_HEADS)]
            dens = [jnp.sum(es[h], axis=-1, keepdims=True) + jnp.exp(sink_ref[h] - ms[h]) for h in range(N_HEADS)]
            outs = [_dot(es[h].astype(BF16), kv(vb, h)) * (1.0 / dens[h]) for h in range(N_HEADS)]
            o_ref[rows, :] = jnp.concatenate(outs, axis=1).astype(BF16)
            lse = jnp.zeros((QBLK, 128), F32)
            for h in range(N_HEADS):
                lse = lse + jnp.where(lane == h, ms[h] + jnp.log(dens[h]), 0.0)
            lse_ref[rows, :] = lse

    return pl.pallas_call(
        body, name="attn_fwd", grid=(s // TOK,),
        in_specs=[pl.BlockSpec(memory_space=pltpu.SMEM), _rows(TOK, 1024), _whole(kpad.shape), _whole(vpad.shape)],
        out_specs=[_rows(TOK, 1024), _rows(TOK, 128)],
        out_shape=[jax.ShapeDtypeStruct((s, 1024), BF16), jax.ShapeDtypeStruct((s, 128), F32)],
        compiler_params=_params(),
    )(sink, q, kpad, vpad)


def _tail(x1, o, gate, target, w_out1, gf):
    s = x1.shape[0]

    def body(x1_ref, o_ref, gate_ref, t_ref, w_ref, gf_ref, y1_ref, dx2_ref, do_ref, dgate_ref, loss_ref, gfn_ref,
             dx2h_ref):
        i = pl.program_id(0)

        @pl.when(i == 0)
        def _():
            loss_ref[...] = jnp.zeros_like(loss_ref)
            gfn_ref[...] = jnp.zeros_like(gfn_ref)

        g = gate_ref[...].astype(F32)
        sg = _sigmoid(g)
        sil = g * sg
        o = o_ref[...].astype(F32)
        y1 = (o * sil).astype(BF16)
        y1_ref[...] = y1
        x2 = x1_ref[...] + _dot(y1, w_ref[...])
        gf = gf_ref[...]
        r, xh, out = _rms_fwd(x2, gf)
        diff = out - t_ref[...]
        loss_ref[...] += jnp.sum(diff * diff, axis=0, keepdims=True) * (0.5 / D_MODEL)
        dout = diff * (1.0 / D_MODEL)
        gfn_ref[...] += jnp.sum(dout * xh, axis=0, keepdims=True)
        dx2 = _rms_bwd(dout, gf, r, xh)
        dx2_ref[...] = dx2
        dx2h = dx2.astype(BF16)
        dx2h_ref[...] = dx2h
        dy1 = _dot_nt(dx2h, w_ref[...])
        do_ref[...] = (dy1 * sil).astype(BF16)
        dgate_ref[...] = (dy1 * o * (sg * (1.0 + g * (1.0 - sg)))).astype(BF16)

    row = _rows(TOK, 1024)
    acc = _whole((1, 1024))
    return pl.pallas_call(
        body, name="tail", grid=(s // TOK,),
        in_specs=[row, row, row, row, _whole(w_out1.shape), acc],
        out_specs=[row, row, row, row, acc, acc, row],
        out_shape=[jax.ShapeDtypeStruct((s, 1024), BF16), jax.ShapeDtypeStruct((s, 1024), F32),
                   jax.ShapeDtypeStruct((s, 1024), BF16), jax.ShapeDtypeStruct((s, 1024), BF16),
                   jax.ShapeDtypeStruct((1, 1024), F32), jax.ShapeDtypeStruct((1, 1024), F32),
                   jax.ShapeDtypeStruct((s, 1024), BF16)],
        compiler_params=_params(),
    )(x1, o, gate, target, w_out1, gf)


def _attn_bwd(q, kpad, vpad, sink, o, lse, do, rope):
    s = q.shape[0]
    pad_t = (kpad.shape[1], kpad.shape[0])

    def body(sink_ref, q_ref, k_ref, v_ref, o_ref, lse_ref, do_ref, c_ref, lo_ref, hi_ref,
             dq_ref, dk_ref, dv_ref, ds_ref):
        i = pl.program_id(0)

        @pl.when(i == 0)
        def _():
            dk_ref[...] = jnp.zeros_like(dk_ref)
            dv_ref[...] = jnp.zeros_like(dv_ref)
            ds_ref[...] = jnp.zeros_like(ds_ref)

        lane = lax.broadcasted_iota(jnp.int32, (1, 128), 1)
        dsink = jnp.zeros((1, 128), F32)
        for b in range(TOK // QBLK):
            rows = slice(b * QBLK, (b + 1) * QBLK)
            start = pl.multiple_of(i * TOK + b * QBLK, QBLK)
            kb = k_ref[pl.ds(start, KBLK), :]
            vb = v_ref[pl.ds(start, KBLK), :]
            bias = _block_bias(i * TOK + b * QBLK, s)
            kv = lambda t, h: t[:, (h // 4) * HEAD_DIM:(h // 4 + 1) * HEAD_DIM]
            hd = lambda ref, h: ref[rows, h * HEAD_DIM:(h + 1) * HEAD_DIM]
            dqs, dks, dvs = {}, {}, {}
            for h0 in range(0, N_HEADS, HEAD_BATCH):
                heads = range(h0, h0 + HEAD_BATCH)
                lses = {h: lse_ref[rows, h:h + 1] for h in heads}
                ps = {h: jnp.exp(_dot_nt(hd(q_ref, h), kv(kb, h)) + bias - lses[h]) for h in heads}
                dos = {h: hd(do_ref, h) for h in heads}
                deltas = {h: jnp.sum(dos[h].astype(F32) * hd(o_ref, h).astype(F32), axis=-1, keepdims=True)
                          for h in heads}
                for h in heads:
                    dsink = dsink + jnp.where(
                        lane == h, -jnp.sum(jnp.exp(sink_ref[h] - lses[h]) * deltas[h], axis=0, keepdims=True), 0.0)
                dscs = {h: (ps[h] * (_dot_nt(dos[h], kv(vb, h)) - deltas[h])).astype(BF16) for h in heads}
                dvs.update({h: _dot_tn(dos[h], ps[h].astype(BF16)) for h in heads})
                dks.update({h: _dot_tn(hd(q_ref, h), dscs[h]) for h in heads})
                dqs.update({h: _dot(dscs[h], kv(kb, h)) * Q_SCALE for h in heads})
            dq = jnp.concatenate([dqs[h] for h in range(N_HEADS)], axis=1)
            dq_ref[rows, :] = _rope_t(dq, c_ref[rows, :], lo_ref[rows, :], hi_ref[rows, :]).astype(BF16)
            group_sum = lambda parts, g: (parts[4 * g] + parts[4 * g + 1]) + (parts[4 * g + 2] + parts[4 * g + 3])
            dk_ref[:, pl.ds(start, KBLK)] += jnp.concatenate([group_sum(dks, g) for g in range(4)], axis=0)
            dv_ref[:, pl.ds(start, KBLK)] += jnp.concatenate([group_sum(dvs, g) for g in range(4)], axis=0)
        ds_ref[...] += dsink

    row = _rows(TOK, 1024)
    tab = _rows(TOK, 128)
    pad = _whole(kpad.shape)
    return pl.pallas_call(
        body, name="attn_bwd", grid=(s // TOK,),
        in_specs=[pl.BlockSpec(memory_space=pltpu.SMEM), row, pad, pad, row, tab, row, tab, tab, tab],
        out_specs=[row, _whole(pad_t), _whole(pad_t), _whole((1, 128))],
        out_shape=[jax.ShapeDtypeStruct((s, 1024), BF16), jax.ShapeDtypeStruct(pad_t, F32),
                   jax.ShapeDtypeStruct(pad_t, F32), jax.ShapeDtypeStruct((1, 128), F32)],
        compiler_params=_params(),
    )(sink, q, kpad, vpad, o, lse, do, *rope)


def _in_proj1_bwd(dq, dk_t, dv_t, dgate, x1, dx2, g1, w_in1, rope):
    s = x1.shape[0]

    def body(dq_ref, dka_ref, dkb_ref, dva_ref, dvb_ref, dgate_ref, x1_ref, dx2_ref, g_ref, w_ref,
             c_ref, lo_ref, hi_ref, dz_ref, dx1_ref, gn_ref, dx1h_ref):
        i = pl.program_id(0)
        dk = jnp.concatenate([dka_ref[...], dkb_ref[...]], axis=1).T
        dv = jnp.concatenate([dva_ref[...], dvb_ref[...]], axis=1).T

        @pl.when(i == 0)
        def _():
            gn_ref[...] = jnp.zeros_like(gn_ref)

        dz_ref[:, 0:1024] = dq_ref[...]
        dz_ref[:, 1024:1280] = _rope_t(dk, c_ref[...], lo_ref[...], hi_ref[...]).astype(BF16)
        dz_ref[:, 1280:1536] = dv.astype(BF16)
        dz_ref[:, 1536:2560] = dgate_ref[...]
        dh = _dot_nt(dz_ref[...], w_ref[...])
        g = g_ref[...]
        r, xh, _ = _rms_fwd(x1_ref[...], g)
        gn_ref[...] += jnp.sum(dh * xh, axis=0, keepdims=True)
        dx1 = dx2_ref[...] + _rms_bwd(dh, g, r, xh)
        dx1_ref[...] = dx1
        dx1h_ref[...] = dx1.astype(BF16)

    row = _rows(TOK, 1024)
    per = TOK // ATTN_WINDOW
    half_a = pl.BlockSpec((256, TOK // 2), lambda i: (0, per * i + 1))
    half_b = pl.BlockSpec((256, TOK // 2), lambda i: (0, per * i + 2))
    tab = _rows(TOK, 128)
    acc = _whole((1, 1024))
    return pl.pallas_call(
        body, name="in_proj1_bwd", grid=(s // TOK,),
        in_specs=[row, half_a, half_b, half_a, half_b, row, row, row, acc, _whole(w_in1.shape), tab, tab, tab],
        out_specs=[_rows(TOK, 2560), row, acc, row],
        out_shape=[jax.ShapeDtypeStruct((s, 2560), BF16), jax.ShapeDtypeStruct((s, 1024), F32),
                   jax.ShapeDtypeStruct((1, 1024), F32), jax.ShapeDtypeStruct((s, 1024), BF16)],
        compiler_params=_params(),
    )(dq, dk_t, dk_t, dv_t, dv_t, dgate, x1, dx2, g1, w_in1, *rope)


def _mix0_bwd(dx1, z0, w_out0, gv, ws, bsx, wg, scale, band, rider):
    s = dx1.shape[0]
    n_tiles = s // TOK

    def body(dx1h_ref, z_ref, zp_ref, zn_ref, wout_ref, gv_ref, ws_ref, bsx_ref, wg_ref, sc_ref, band_ref,
             dz_ref, dpn_ref, dws_ref, dbs_ref, dgv_ref, dsc_ref, dwg_ref):
        i = pl.program_id(0)
        dz_ref[:, 3072:4096] = jnp.zeros((TOK, 1024), BF16)

        @pl.when(i == 0)
        def _():
            for ref in (dws_ref, dbs_ref, dgv_ref, dsc_ref, dwg_ref):
                ref[...] = jnp.zeros_like(ref)

        dcat = _dot_nt(dx1h_ref[...], wout_ref[...])
        dya = dcat[:, 0:1024]
        dyb = dcat[:, 1024:2048]

        au = z_ref[:, 0:1024].astype(F32)
        av = z_ref[:, 1024:2048].astype(F32)
        ag = z_ref[:, 2048:3072].astype(F32)
        gv = gv_ref[...]
        u, du = _gelu_and_grad(au)
        v1, dv1 = _gelu_and_grad(av)
        rv, vh, v2, mixed = _mixer_a(v1, gv, ws_ref, bsx_ref[...])
        sg = _sigmoid(ag)
        sil = ag * sg
        dz_ref[:, 2048:3072] = (dya * u * mixed * (sg * (1.0 + ag * (1.0 - sg)))).astype(BF16)
        dz_ref[:, 0:1024] = (dya * mixed * sil * du).astype(BF16)
        dmixed = dya * u * sil
        lane = lax.broadcasted_iota(jnp.int32, (1, 128), 1)
        dm16 = dmixed.astype(BF16)
        dv2_rows = []
        for c in range(TOK // CHUNK):
            rows = slice(c * CHUNK, (c + 1) * CHUNK)
            cols_out = []
            for h in range(4):
                cols = slice(h * 256, (h + 1) * 256)
                dws_ref[h] += _dot_nt(dm16[rows, cols], v2[rows, cols])
                dbs_ref[...] += jnp.where(lane == h, jnp.sum(dmixed[rows, cols], axis=-1, keepdims=True), 0.0)
                cols_out.append(_dot_tn(ws_ref[h], dm16[rows, cols]))
            dv2_rows.append(jnp.concatenate(cols_out, axis=1))
        dv2 = jnp.concatenate(dv2_rows, axis=0)
        dgv_ref[...] += jnp.sum(dv2 * vh, axis=0, keepdims=True)
        dz_ref[:, 1024:2048] = (_rms_bwd(dv2, gv, rv, vh) * dv1).astype(BF16)

        bx16 = z_ref[:, 3072:4096]
        bg = z_ref[:, 4096:5120].astype(F32)
        counts = _window_counts(i, TOK, s)
        halo = _with_halo(i, n_tiles, zp_ref, bx16, zn_ref)
        ps = [p.astype(BF16) for p in _mixer_b_pooled(bx16.astype(F32), halo, band_ref, counts)]
        pw = jnp.concatenate([_dot(ps[g], wg_ref[g]) for g in range(4)], axis=1)
        sgb = _sigmoid(bg)
        sc = sc_ref[...]
        dz_ref[:, 4096:5120] = (dyb * pw * sc * (sgb * (1.0 + bg * (1.0 - sgb)))).astype(BF16)
        dys = dyb * (bg * sgb)
        dsc_ref[...] += jnp.sum(dys * pw, axis=0, keepdims=True)
        dpw = (dys * sc).astype(BF16)
        for g in range(4):
            cols = slice(g * 256, (g + 1) * 256)
            dwg_ref[g] += _dot_tn(ps[g], dpw[:, cols])
            dpn_ref[:, cols] = (_dot_nt(dpw[:, cols], wg_ref[g]) / counts[g]).astype(BF16)

    prev, nxt = _halo_specs(s, 3)
    row = _rows(TOK, 1024)
    vec = _whole((1, 1024))
    return pl.pallas_call(
        rider.carried_by(body, 11, 7, n_tiles), name="mix0_bwd", grid=(n_tiles,),
        in_specs=[row, _rows(TOK, 5120), prev, nxt, _whole(w_out0.shape), vec, _whole(ws.shape),
                  _whole(bsx.shape), _whole(wg.shape), vec, _whole(band.shape)] + rider.in_specs,
        out_specs=[_rows(TOK, 5120), row, _whole((4, 128, 128)), _whole((128, 128)), vec, vec,
                   _whole((4, 256, 256))] + rider.out_specs,
        out_shape=[jax.ShapeDtypeStruct((s, 5120), BF16), jax.ShapeDtypeStruct((s, 1024), BF16),
                   jax.ShapeDtypeStruct((4, 128, 128), F32), jax.ShapeDtypeStruct((128, 128), F32),
                   jax.ShapeDtypeStruct((1, 1024), F32), jax.ShapeDtypeStruct((1, 1024), F32),
                   jax.ShapeDtypeStruct((4, 256, 256), F32)] + rider.out_shape,
        scratch_shapes=rider.scratch,
        compiler_params=_params(),
    )(dx1, z0, z0, z0, w_out0, gv, ws, bsx, wg, scale, band, *rider.parts)


def _fill_pooled_grad(dz0, dpn, band_t):
    s = dpn.shape[0]
    n_tiles = s // TOK

    def body(dz_in_ref, dpn_ref, dpp_ref, dpx_ref, band_ref, dbx_ref):
        i = pl.program_id(0)
        dpn = dpn_ref[...]
        halo = _with_halo(i, n_tiles, dpp_ref, dpn, dpx_ref)
        counts = _window_counts(i, TOK, s)
        for g in range(4):
            cols = slice(g * 256, (g + 1) * 256)
            dbx = _dot(band_ref[g], halo[:, cols]) - dpn[:, cols].astype(F32) * counts[g]
            dbx_ref[:, cols] = dbx.astype(BF16)

    prev, nxt = _halo_specs(s, 0)
    return pl.pallas_call(
        body, name="fill_pooled_grad", grid=(n_tiles,),
        in_specs=[ANY, _rows(TOK, 1024), prev, nxt, _whole(band_t.shape)],
        out_specs=pl.BlockSpec((TOK, 1024), lambda i: (i, 3)),
        out_shape=jax.ShapeDtypeStruct(dz0.shape, BF16),
        input_output_aliases={0: 0},
        compiler_params=_params(),
    )(dz0, dpn, dpn, dpn, band_t)


def _in_proj0_bwd(dz0, x, dx1, g0, w_in0, rider):
    s = x.shape[0]
    n_tiles = s // TOK

    def body(dz_ref, x_ref, dx1_ref, g_ref, w_ref, dx_ref, gn_ref):
        i = pl.program_id(0)

        @pl.when(i == 0)
        def _():
            gn_ref[...] = jnp.zeros_like(gn_ref)

        dh = _dot_nt(dz_ref[...], w_ref[...])
        g0v = g_ref[...]
        r, xh, _ = _rms_fwd(x_ref[...], g0v)
        gn_ref[...] += jnp.sum(dh * xh, axis=0, keepdims=True)
        dx_ref[...] = dx1_ref[...] + _rms_bwd(dh, g0v, r, xh)

    row = _rows(TOK, 1024)
    vec = _whole((1, 1024))
    return pl.pallas_call(
        rider.carried_by(body, 5, 2, n_tiles), name="in_proj0_bwd", grid=(n_tiles,),
        in_specs=[_rows(TOK, 5120), row, row, vec, _whole(w_in0.shape)] + rider.in_specs,
        out_specs=[row, vec] + rider.out_specs,
        out_shape=[jax.ShapeDtypeStruct((s, 1024), F32), jax.ShapeDtypeStruct((1, 1024), F32)] + rider.out_shape,
        scratch_shapes=rider.scratch,
        compiler_params=_params(),
    )(dz0, x, dx1, g0, w_in0, *rider.parts)


def _weight_grad(a, b, n_blocks, split, name):
    s, k = a.shape
    n = b.shape[1]
    tn = n // n_blocks
    w = tn // split
    ts = 512

    def body(a_ref, b_ref, o_ref):
        @pl.when(pl.program_id(1) == 0)
        def _():
            o_ref[...] = jnp.zeros_like(o_ref)

        res = _dot_tn(a_ref[...], b_ref[...])
        for q in range(split):
            o_ref[q] += res[:, q * w:(q + 1) * w]

    return pl.pallas_call(
        body, name=name, grid=(n_blocks, s // ts),
        in_specs=[pl.BlockSpec((ts, k), lambda j, t: (t, 0)), pl.BlockSpec((ts, tn), lambda j, t: (t, j))],
        out_specs=pl.BlockSpec((split, k, w), lambda j, t: (j, 0, 0)),
        out_shape=jax.ShapeDtypeStruct((n_blocks * split, k, w), F32),
        compiler_params=_params(),
    )(a, b)


def _row_tile(rows, cols):
    t = rows
    while t * cols * 4 > (1 << 20) and t % 16 == 0:
        t //= 2
    return t


def _add_sibling(where, g, theirs, name):
    _, _, rows, cols = g.shape
    t = _row_tile(rows, cols)

    def body(where_ref, g_ref, t_ref, o_ref):
        o_ref[...] = (g_ref[...] + t_ref[...]).astype(BF16)

    spec = pl.BlockSpec((None, t, cols), lambda s, i, p: (s, i, 0))
    return pl.pallas_call(
        body, name=name, out_shape=jax.ShapeDtypeStruct((4, rows, cols), BF16),
        grid_spec=pltpu.PrefetchScalarGridSpec(
            num_scalar_prefetch=1, grid=(4, rows // t),
            in_specs=[pl.BlockSpec((None, None, t, cols), lambda s, i, p: (s, p[1], i, 0)), spec], out_specs=spec),
        compiler_params=_params())(where, g, theirs)


def _sum_chips(where, g, theirs, slots, name):
    _, _, rows, cols = g.shape
    t = _row_tile(rows, cols)

    def body(where_ref, g_ref, t_ref, s_ref, o_ref):
        me = where_ref[0]
        own = g_ref[...] + t_ref[...]
        acc = jnp.where(me == 0, own, s_ref[0].astype(F32))
        for k in range(1, 4):
            acc = acc + jnp.where(me == k, own, s_ref[k].astype(F32))
        o_ref[...] = acc

    return pl.pallas_call(
        body, name=name, out_shape=jax.ShapeDtypeStruct((rows, cols), F32),
        grid_spec=pltpu.PrefetchScalarGridSpec(
            num_scalar_prefetch=1, grid=(rows // t,),
            in_specs=[pl.BlockSpec((None, None, t, cols), lambda i, p: (p[0], p[1], i, 0)),
                      pl.BlockSpec((None, t, cols), lambda i, p: (p[0], i, 0)),
                      pl.BlockSpec((4, t, cols), lambda i, p: (0, i, 0))],
            out_specs=pl.BlockSpec((t, cols), lambda i, p: (i, 0))),
        compiler_params=_params())(where, g, theirs, slots)


def _adamw_halves(where, w, own, theirs, m, v, name):
    rows, cols = own.shape
    t = _row_tile(rows, cols)
    per = rows // t

    def body(where_ref, w_ref, own_ref, th_ref, m_ref, v_ref, g_ref, d_ref, nm_ref, nv_ref):
        g = jnp.where(pl.program_id(0) == where_ref[1], own_ref[...], th_ref[...])
        g_ref[...] = g
        m2 = ADAM_B1 * m_ref[...] + (1.0 - ADAM_B1) * g
        v2 = ADAM_B2 * v_ref[...] + (1.0 - ADAM_B2) * (g * g)
        m_hat = m2 / (1.0 - ADAM_B1 ** ADAM_STEP)
        v_hat = v2 / (1.0 - ADAM_B2 ** ADAM_STEP)
        d_ref[...] = -ADAM_LR * (m_hat / (jnp.sqrt(v_hat) + ADAM_EPS) + ADAM_WD * w_ref[...])
        nm_ref[...] = m2
        nv_ref[...] = v2

    full = pl.BlockSpec((t, cols), lambda h, i, p: (h * per + i, 0))
    half = pl.BlockSpec((t, cols), lambda h, i, p: (i, 0))
    shp = jax.ShapeDtypeStruct(w.shape, F32)
    return pl.pallas_call(
        body, name=name, out_shape=[shp] * 4,
        grid_spec=pltpu.PrefetchScalarGridSpec(
            num_scalar_prefetch=1, grid=(2, per), in_specs=[full, half, half, full, full], out_specs=[full] * 4),
        compiler_params=_params())(where, w, own, theirs, m, v)


def _place_shard(where, w, cut, name):
    if cut.kind == "cols":
        r, n = cut.full_shape
        blk, grid = (256, n // 4), (r // 256,)
        src_map, dst_map = (lambda i, p: (i, 0)), (lambda i, p: (i, p[0]))
    elif cut.kind == "rows":
        r, n = cut.full_shape
        per = r // 4 // 256
        blk, grid = (256, n), (per,)
        src_map, dst_map = (lambda i, p: (i, 0)), (lambda i, p: (p[0] * per + i, 0))
    else:
        g, r, n = cut.full_shape
        blk, grid = (g, r // 4, n), (1,)
        src_map, dst_map = (lambda i, p: (0, 0, 0)), (lambda i, p: (0, p[0], 0))

    def body(where_ref, w_ref, o_ref):
        o_ref[...] = w_ref[...].astype(BF16)

    return pl.pallas_call(
        body, name=name, out_shape=jax.ShapeDtypeStruct(cut.full_shape, BF16),
        grid_spec=pltpu.PrefetchScalarGridSpec(
            num_scalar_prefetch=1, grid=grid, in_specs=[pl.BlockSpec(blk, src_map)],
            out_specs=pl.BlockSpec(blk, dst_map)),
        compiler_params=_params())(where, w)


def _sum_small(first, second, third):
    rows = second.shape[1]

    def body(a_ref, b_ref, c_ref, o_ref):
        top = a_ref[0] + c_ref[0] + b_ref[0, 0:8]
        rest = b_ref[0, 8:rows]
        for k in range(1, 8):
            top = top + (a_ref[k] + c_ref[k] + b_ref[k, 0:8])
            rest = rest + b_ref[k, 8:rows]
        o_ref[0:8] = top
        o_ref[8:rows] = rest

    return pl.pallas_call(
        body, name="sum_small", in_specs=[_whole(first.shape), _whole(second.shape), _whole(third.shape)],
        out_specs=_whole(second.shape[1:]), out_shape=jax.ShapeDtypeStruct(second.shape[1:], F32),
        compiler_params=_params())(first, second, third)


def _adamw(w, g, m, v, name):
    rows, cols = w.shape
    t = _row_tile(rows, cols)

    def body(w_ref, g_ref, m_ref, v_ref, d_ref, nm_ref, nv_ref):
        g = g_ref[...]
        m2 = ADAM_B1 * m_ref[...] + (1.0 - ADAM_B1) * g
        v2 = ADAM_B2 * v_ref[...] + (1.0 - ADAM_B2) * (g * g)
        m_hat = m2 / (1.0 - ADAM_B1 ** ADAM_STEP)
        v_hat = v2 / (1.0 - ADAM_B2 ** ADAM_STEP)
        d_ref[...] = -ADAM_LR * (m_hat / (jnp.sqrt(v_hat) + ADAM_EPS) + ADAM_WD * w_ref[...])
        nm_ref[...] = m2
        nv_ref[...] = v2

    spec = pl.BlockSpec((t, cols), lambda i: (i, 0))
    shp = jax.ShapeDtypeStruct(w.shape, F32)
    return pl.pallas_call(body, name=name, grid=(rows // t,), in_specs=[spec] * 4, out_specs=[spec] * 3,
                          out_shape=[shp] * 3, compiler_params=_params())(w, g, m, v)


def _place():
    x, y, c = lax.axis_index("x"), lax.axis_index("y"), lax.axis_index("c")
    chips = [(1 - x, y), (x, 1 - y), (1 - x, 1 - y)]
    return x, y, c, chips


class _Sharded:
    def __init__(self, kind, full_shape):
        self.kind = kind
        self.full_shape = full_shape

    def in_full(self, ref, s, h):
        if self.kind == "cols":
            r, n = self.full_shape
            return ref.at[pl.ds(h * (r // 2), r // 2), pl.ds(pl.multiple_of(s * (n // 4), 128), n // 4)]
        if self.kind == "rows":
            r, _ = self.full_shape
            return ref.at[pl.ds(pl.multiple_of(s * (r // 4) + h * (r // 8), 8), r // 8), :]
        g, r, _ = self.full_shape
        return ref.at[pl.ds(h * (g // 2), g // 2), pl.ds(pl.multiple_of(s * (r // 4), 16), r // 4), :]


def _remote(src, dst, send_sem, recv_sem, to):
    return pltpu.make_async_remote_copy(src_ref=src, dst_ref=dst, send_sem=send_sem, recv_sem=recv_sem,
                                        device_id=to, device_id_type=MESH)


def _start_remote(src, dst, send_sem, recv_sem, to):
    cp = _remote(src, dst, send_sem, recv_sem, to)
    cp.start()
    return cp


class _Gather:
    def __init__(self, fulls, cuts):
        n = len(fulls)
        self.fulls, self.cuts = list(fulls), list(cuts)
        self.in_specs = [ANY] * n
        self.out_specs = [ANY] * n
        self.out_shape = [jax.ShapeDtypeStruct(cut.full_shape, BF16) for cut in cuts]
        self.scratch = [pltpu.SemaphoreType.DMA((6 * n,)), pltpu.SemaphoreType.DMA((6 * n,))]

    def _step(self, step, src, out, send_sems, recv_sems):
        n, cuts = len(self.fulls), self.cuts
        x, y, c, chips = _place()
        me = 2 * x + y

        def ends(w, s, h, from_src):
            dst = cuts[w].in_full(out[w], s, h)
            return (cuts[w].in_full(src[w], s, h) if from_src else dst), dst

        for w in range(n):
            for j, chip in enumerate(chips):
                s = 2 * chip[0] + chip[1]
                k, k2 = 3 * w + j, 3 * n + 3 * w + j
                if step == "send":
                    _start_remote(*ends(w, me, c, True), send_sems.at[k], recv_sems.at[k], (*chip, c))
                elif step == "pass_on":
                    _remote(*ends(w, s, c, False), send_sems.at[k], recv_sems.at[k], (x, y, c)).wait_recv()
                    _start_remote(*ends(w, s, c, False), send_sems.at[k2], recv_sems.at[k2], (x, y, 1 - c))
                else:
                    _remote(*ends(w, s, 1 - c, False), send_sems.at[k2], recv_sems.at[k2], (x, y, c)).wait_recv()
                    _remote(*ends(w, me, c, True), send_sems.at[k], recv_sems.at[k], (x, y, c)).wait_send()
                    _remote(*ends(w, s, c, False), send_sems.at[k2], recv_sems.at[k2], (x, y, c)).wait_send()

    def carried_by(self, body, n_in, n_out, n_steps, step_index=lambda: pl.program_id(0)):
        k = len(self.fulls)

        def carrier(*refs):
            ins, src = refs[:n_in], refs[n_in:n_in + k]
            outs, out = refs[n_in + k:n_in + k + n_out], refs[n_in + k + n_out:n_in + 2 * k + n_out]
            sems = refs[n_in + 2 * k + n_out:]
            for step, at in (("send", 0), ("pass_on", 3 * n_steps // 4), ("finish", n_steps - 1)):
                if step == "finish":
                    body(*ins, *outs)

                @pl.when(step_index() == at)
                def _():
                    self._step(step, src, out, *sems)

        return carrier


def _exchange_halves(grads, name):
    n = len(grads)

    def body(*refs):
        g = refs[:n]
        theirs = refs[n:2 * n]
        send_sems, recv_sems = refs[2 * n:]
        x, y, c, _ = _place()
        sends = [_start_remote(g[w].at[:, 1 - c], theirs[w], send_sems.at[w], recv_sems.at[w], (x, y, 1 - c))
                 for w in range(n)]
        for w in range(n):
            _remote(g[w].at[:, 1 - c], theirs[w], send_sems.at[w], recv_sems.at[w], (x, y, c)).wait_recv()
        for cp in sends:
            cp.wait_send()

    return pl.pallas_call(
        body, name=name,
        in_specs=[ANY] * n, out_specs=[ANY] * n,
        out_shape=[jax.ShapeDtypeStruct((4,) + g.shape[2:], F32) for g in grads],
        scratch_shapes=[pltpu.SemaphoreType.DMA((n,)), pltpu.SemaphoreType.DMA((n,))],
        compiler_params=pltpu.CompilerParams(has_side_effects=True),
    )(*grads)


def _gather_small(small_ref, gathered, send_sems, recv_sems, first_sem, local_sem, start):
    x, y, c, _ = _place()
    me = 4 * x + 2 * y + c
    flips = [(fx, fy, fc) for fx in range(2) for fy in range(2) for fc in range(2)][1:]
    own = pltpu.make_async_copy(small_ref, gathered.at[me], local_sem)
    if start:
        own.start()
    else:
        own.wait()
    for k, (fx, fy, fc) in enumerate(flips):
        peer = (x + fx - 2 * x * fx, y + fy - 2 * y * fy, c + fc - 2 * c * fc)
        sems = (send_sems.at[first_sem + k], recv_sems.at[first_sem + k])
        if start:
            _start_remote(small_ref, gathered.at[me], *sems, peer)
        else:
            cp = _remote(small_ref, gathered.at[4 * peer[0] + 2 * peer[1] + peer[2]], *sems, (x, y, c))
            cp.wait_recv()
            cp.wait_send()


class _ScatterRider:
    def __init__(self, parts, small):
        n = len(parts)
        self.n = n
        self.parts = list(parts) + [small]
        self.in_specs = [ANY] * (n + 1)
        self.out_specs = [ANY] * (n + 1)
        self.out_shape = ([jax.ShapeDtypeStruct(a.shape, a.dtype) for a in parts]
                          + [jax.ShapeDtypeStruct((8,) + small.shape, small.dtype)])
        self.scratch = [pltpu.SemaphoreType.DMA((3 * n + 7,)), pltpu.SemaphoreType.DMA((3 * n + 7,)),
                        pltpu.SemaphoreType.DMA]

    def _copies(self, p, out, send_sems, recv_sems, local_sem, start):
        x, y, c, chips = _place()
        me = 2 * x + y
        n = self.n
        _gather_small(p[n], out[n], send_sems, recv_sems, 3 * n, local_sem, start)
        for w in range(n):
            for j, chip in enumerate(chips):
                s = 2 * chip[0] + chip[1]
                if start:
                    _start_remote(p[w].at[s], out[w].at[me], send_sems.at[3 * w + j], recv_sems.at[3 * w + j],
                                  (*chip, c))
                else:
                    cp = _remote(p[w].at[s], out[w].at[s], send_sems.at[3 * w + j], recv_sems.at[3 * w + j],
                                 (x, y, c))
                    cp.wait_recv()
                    cp.wait_send()

    def carried_by(self, body, n_in, n_out, n_tiles):
        k = len(self.parts)

        def carrier(*refs):
            ins, mine = refs[:n_in], refs[n_in:n_in + k]
            outs, slots = refs[n_in + k:n_in + k + n_out], refs[n_in + k + n_out:n_in + 2 * k + n_out]
            sems = refs[n_in + 2 * k + n_out:]

            @pl.when(pl.program_id(0) == 0)
            def _():
                self._copies(mine, slots, *sems, start=True)

            body(*ins, *outs)

            @pl.when(pl.program_id(0) == n_tiles - 1)
            def _():
                self._copies(mine, slots, *sems, start=False)

        return carrier


def _share_halves(halves, small):
    n = len(halves)

    def body(*refs):
        hv = refs[:n]
        small_ref = refs[n]
        out = refs[n + 1:2 * n + 1]
        gathered = refs[2 * n + 1]
        send_sems, recv_sems, local_sem = refs[2 * n + 2:]
        x, y, c, _ = _place()
        sends = [_start_remote(hv[w], out[w], send_sems.at[w], recv_sems.at[w], (x, y, 1 - c)) for w in range(n)]
        _gather_small(small_ref, gathered, send_sems, recv_sems, n, local_sem, True)
        for w in range(n):
            _remote(hv[w], out[w], send_sems.at[w], recv_sems.at[w], (x, y, c)).wait_recv()
        for cp in sends:
            cp.wait_send()
        _gather_small(small_ref, gathered, send_sems, recv_sems, n, local_sem, False)

    return pl.pallas_call(
        body, name="share_halves",
        in_specs=[ANY] * (n + 1), out_specs=[ANY] * (n + 1),
        out_shape=[jax.ShapeDtypeStruct(a.shape, F32) for a in halves] + [jax.ShapeDtypeStruct((8,) + small.shape, F32)],
        scratch_shapes=[pltpu.SemaphoreType.DMA((n + 7,)), pltpu.SemaphoreType.DMA((n + 7,)),
                        pltpu.SemaphoreType.DMA],
        compiler_params=pltpu.CompilerParams(has_side_effects=True),
    )(*halves, small)


SMALL_ROWS = 80


def _pack_small(vecs, ws, bs, sink):
    ws = jnp.zeros((64, 1024), F32) if ws is None else ws.reshape(64, 1024)
    bs = jnp.zeros((1, 512), F32) if bs is None else bs.reshape(1, 512)
    sink = jnp.zeros((1, 16), F32) if sink is None else sink.reshape(1, 16)
    top = jnp.concatenate(
        [v.reshape(1, 1024) for v in vecs]
        + [jnp.pad(bs, ((0, 0), (0, 512))), jnp.pad(sink, ((0, 0), (0, 1008))), jnp.zeros((1, 1024), F32)], axis=0)
    return jnp.concatenate([top, ws, jnp.zeros((8, 1024), F32)], axis=0)


def _unpack_small(p):
    vecs = [p[k] for k in range(5)]
    return vecs, p[8:72].reshape(4, 128, 128), p[5, :512].reshape(4, 128), p[6, :16]


def kernel(x, norm_0, w_in_0, a_v_norm_0, a_spatial_w_0, a_spatial_b_0, b_group_w_0, b_scale_0, w_out_0, norm_1, w_in_1, sink_1, w_out_1, final_norm, loss_target, m_norm_0, m_w_in_0, m_a_v_norm_0, m_a_spatial_w_0, m_a_spatial_b_0, m_b_group_w_0, m_b_scale_0, m_w_out_0, m_norm_1, m_w_in_1, m_sink_1, m_w_out_1, m_final_norm, v_norm_0, v_w_in_0, v_a_v_norm_0, v_a_spatial_w_0, v_a_spatial_b_0, v_b_group_w_0, v_b_scale_0, v_w_out_0, v_norm_1, v_w_in_1, v_sink_1, v_w_out_1, v_final_norm):
    s = x.shape[1]
    xs = x.reshape(s, D_MODEL)
    target = loss_target.reshape(s, D_MODEL)

    cuts = [_Sharded("cols", (1024, 5120)), _Sharded("rows", (2048, 1024)), _Sharded("cols", (1024, 2560)),
            _Sharded("rows", (1024, 1024)), _Sharded("mid", (4, 256, 256))]
    big_w = [w_in_0, w_out_0, w_in_1, w_out_1, b_group_w_0]
    big_m = [m_w_in_0, m_w_out_0, m_w_in_1, m_w_out_1, m_b_group_w_0]
    big_v = [v_w_in_0, v_w_out_0, v_w_in_1, v_w_out_1, v_b_group_w_0]
    where = jnp.stack([2 * lax.axis_index("x") + lax.axis_index("y"), lax.axis_index("c")]).astype(jnp.int32)
    placed = [_place_shard(where, w, cut, f"place_shard{k}") for k, (w, cut) in enumerate(zip(big_w, cuts))]
    cx, cy = lax.axis_index("x"), lax.axis_index("y")
    order = jnp.stack([2 * cx + cy, 2 * (1 - cx) + cy, 2 * cx + 1 - cy, 2 * (1 - cx) + 1 - cy]).astype(jnp.int32)

    row = lambda v: v.reshape(1, 1024)
    ws16 = a_spatial_w_0.astype(BF16)
    bsx = jnp.repeat(a_spatial_b_0.T, 256, axis=1)
    band, band_t = _band_matrices(TOK)
    rope = _rope_tables(s)

    h0, z0, w_in0 = _in_proj0_own(order, xs, row(norm_0), w_in_0.astype(BF16), _Gather(placed[:1], cuts[:1]))
    z0, w_out0, wg = _in_proj0_rest(order, h0, w_in0, z0, _Gather([placed[1], placed[4]], [cuts[1], cuts[4]]))
    cat, x1, w_in1, w_out1 = _mix0_fwd(xs, z0, w_out0, row(a_v_norm_0), ws16, bsx, wg, row(b_scale_0), band,
                                       _Gather(placed[2:4], cuts[2:4]))
    h1, q, k, v, gate = _in_proj1(x1, row(norm_1), w_in1, rope)
    kpad = jnp.pad(k, ((ATTN_WINDOW, ATTN_WINDOW), (0, 0)))
    vpad = jnp.pad(v, ((ATTN_WINDOW, ATTN_WINDOW), (0, 0)))
    o, lse = _attn_fwd(q, kpad, vpad, sink_1)
    y1, dx2, do, dgate, loss_lanes, g_final, dx2h = _tail(x1, o, gate, target, w_out1, row(final_norm))

    dq, dkpad, dvpad, dsink = _attn_bwd(q, kpad, vpad, sink_1, o, lse, do, rope)
    dz1, dx1, g_norm1, dx1h = _in_proj1_bwd(dq, dkpad, dvpad, dgate, x1, dx2, row(norm_1), w_in1, rope)

    g_w_in1 = _weight_grad(h1, dz1, 2, 2, "grad_w_in1").reshape(4, 2, 512, 640)
    g_w_out1 = _weight_grad(y1, dx2h, 1, 1, "grad_w_out1").reshape(4, 2, 128, 1024)
    g_w_out0 = _weight_grad(cat, dx1h, 1, 1, "grad_w_out0").reshape(4, 2, 256, 1024)
    first = [g_w_out0, g_w_in1, g_w_out1]
    theirs1 = _exchange_halves(first, "exchange_halves1")
    parts1 = [_add_sibling(where, g, t, f"add_sibling1_{k}") for k, (g, t) in enumerate(zip(first, theirs1))]
    zero = jnp.zeros((1024,), F32)
    small1 = _pack_small([zero, zero, zero, g_norm1, g_final], None, None, dsink[0, :16])[:8]
    dz0, dpn, d_ws, d_bs, d_gv, d_scale, d_wg, *slots1, small1_all = _mix0_bwd(
        dx1h, z0, w_out0, row(a_v_norm_0), ws16, bsx, wg, row(b_scale_0), band, _ScatterRider(parts1, small1))
    dz0 = _fill_pooled_grad(dz0, dpn, band_t)
    g_w_in0 = _weight_grad(h0, dz0, 4, 1, "grad_w_in0").reshape(4, 2, 512, 1280)
    g_wg = d_wg.reshape(2, 2, 4, 64, 256).transpose(2, 0, 1, 3, 4).reshape(4, 2, 128, 256)
    second = [g_w_in0, g_wg]
    theirs2 = _exchange_halves(second, "exchange_halves2")
    parts2 = [_add_sibling(where, g, t, f"add_sibling2_{k}") for k, (g, t) in enumerate(zip(second, theirs2))]
    small2 = _pack_small([zero, d_gv, d_scale, zero, zero], d_ws, d_bs[:, :4].T, None)
    grad_x, g_norm0, *slots2, small2_all = _in_proj0_bwd(
        dz0, xs, dx1, row(norm_0), w_in0, _ScatterRider(parts2, small2))

    grads = [g_w_in0, g_w_out0, g_w_in1, g_w_out1, g_wg]
    theirs = [theirs2[0], theirs1[0], theirs1[1], theirs1[2], theirs2[1]]
    slots = [slots2[0], slots1[0], slots1[1], slots1[2], slots2[1]]
    n = len(grads)
    reduced = [_sum_chips(where, grads[w], theirs[w], slots[w], f"sum_chips{w}") for w in range(n)]
    small3 = _pack_small([g_norm0, zero, zero, zero, zero], None, None, None)[:8]
    *from_sibling, small3_all = _share_halves(reduced, small3)

    out_g, out_d, out_m, out_v = {}, {}, {}, {}
    names = ["w_in_0", "w_out_0", "w_in_1", "w_out_1", "b_group_w_0"]
    for w in range(n):
        shape = big_w[w].shape
        two_d = (-1, shape[-1])
        outs = _adamw_halves(where, big_w[w].reshape(two_d), reduced[w], from_sibling[w], big_m[w].reshape(two_d),
                             big_v[w].reshape(two_d), f"adamw{w}")
        out_g[names[w]], out_d[names[w]], out_m[names[w]], out_v[names[w]] = (a.reshape(shape) for a in outs)

    g_small = _sum_small(small1_all, small2_all, small3_all)
    small_names = ["norm_0", "a_v_norm_0", "b_scale_0", "norm_1", "final_norm"]
    pack = lambda vecs, ws_, bs_, sk: _pack_small(vecs, ws_, bs_, sk)
    w_small = pack([norm_0, a_v_norm_0, b_scale_0, norm_1, final_norm], a_spatial_w_0, a_spatial_b_0, sink_1)
    m_small = pack([m_norm_0, m_a_v_norm_0, m_b_scale_0, m_norm_1, m_final_norm], m_a_spatial_w_0,
                   m_a_spatial_b_0, m_sink_1)
    v_small = pack([v_norm_0, v_a_v_norm_0, v_b_scale_0, v_norm_1, v_final_norm], v_a_spatial_w_0,
                   v_a_spatial_b_0, v_sink_1)
    d_small, nm_small, nv_small = _adamw(w_small, g_small, m_small, v_small, "adamw_small")
    for store, packed in ((out_g, g_small), (out_d, d_small), (out_m, nm_small), (out_v, nv_small)):
        vecs, ws_, bs_, sk = _unpack_small(packed)
        for name, vec in zip(small_names, vecs):
            store[name] = vec
        store["a_spatial_w_0"], store["a_spatial_b_0"], store["sink_1"] = ws_, bs_, sk

    loss = lax.psum(jnp.sum(loss_lanes), ("x", "y", "c"))
    order = ["norm_0", "w_in_0", "a_v_norm_0", "a_spatial_w_0", "a_spatial_b_0", "b_group_w_0", "b_scale_0",
             "w_out_0", "norm_1", "w_in_1", "sink_1", "w_out_1", "final_norm"]
    return (loss, grad_x.reshape(1, s, D_MODEL), *[out_g[k] for k in order], *[out_d[k] for k in order],
            *[out_m[k] for k in order], *[out_v[k] for k in order])
```

```python
import functools

import numpy as np
import jax
import jax.numpy as jnp
from jax import lax
from jax.experimental import pallas as pl
from jax.experimental.pallas import tpu as pltpu

F32 = jnp.float32
BF16 = jnp.bfloat16
MESH = pl.DeviceIdType.MESH

D_MODEL = 1024
EPS = 1e-6
NEG_INF = -1e30
CHUNK = 128
POOL_WINDOWS = (2, 4, 8, 16)
HALO = 16
N_HEADS = 16
HEAD_DIM = 64
ATTN_WINDOW = 128
ROPE_THETA = 500000.0
ROT_DIM = 16
ADAM_LR = 0.001
ADAM_B1 = 0.9
ADAM_B2 = 0.999
ADAM_EPS = 1e-08
ADAM_WD = 0.01
ADAM_STEP = 10

TOK = 256
VMEM_LIMIT = 56 * 1024 * 1024


def _params(**kw):
    return pltpu.CompilerParams(vmem_limit_bytes=VMEM_LIMIT, **kw)


def _whole(shape):
    nd = len(shape)
    return pl.BlockSpec(shape, lambda *_: (0,) * nd)


def _rows(t, n):
    return pl.BlockSpec((t, n), lambda i: (i, 0))


ANY = pl.BlockSpec(memory_space=pl.ANY)

_G0 = 0.7978845608028654
_G1 = 0.044715


def _gelu(x):
    return 0.5 * x * (1.0 + jnp.tanh(_G0 * (x + _G1 * x * x * x)))


def _gelu_and_grad(x):
    x2 = x * x
    t = jnp.tanh(_G0 * (x + _G1 * x2 * x))
    half = 0.5 * (1.0 + t)
    return x * half, half + 0.5 * x * (1.0 - t * t) * (_G0 * (1.0 + 3.0 * _G1 * x2))


def _sigmoid(x):
    return 1.0 / (1.0 + jnp.exp(-x))


def _dot(a, b):
    return jnp.dot(a, b, preferred_element_type=F32)


def _dot_nt(a, b):
    return lax.dot_general(a, b, (((1,), (1,)), ((), ())), preferred_element_type=F32)


def _dot_tn(a, b):
    return lax.dot_general(a, b, (((0,), (0,)), ((), ())), preferred_element_type=F32)


def _rms_fwd(x, g):
    r = lax.rsqrt(jnp.mean(x * x, axis=-1, keepdims=True) + EPS)
    xh = x * r
    return r, xh, xh * g


def _rms_bwd(dy, g, r, xh):
    dxh = dy * g
    return r * (dxh - xh * jnp.mean(dxh * xh, axis=-1, keepdims=True))


def _band_matrices(t):
    r = np.arange(t)[:, None]
    j = np.arange(t + 2 * HALO)[None, :]
    fwd, bwd = [], []
    for w in POOL_WINDOWS:
        d = j - r - HALO
        fwd.append((d >= -(w // 2)) & (d < w // 2))
        bwd.append((d >= -(w // 2) + 1) & (d <= w // 2))
    return (jnp.asarray(np.stack(fwd), BF16), jnp.asarray(np.stack(bwd), BF16))


def _window_counts(i, t, s):
    tok = i * t + lax.broadcasted_iota(jnp.int32, (t, 1), 0)
    out = []
    for w in POOL_WINDOWS:
        cnt = jnp.minimum(tok + w // 2, s) - jnp.maximum(tok - w // 2, 0)
        out.append(cnt.astype(F32))
    return out


def _rope_tables(s):
    inv = np.float32(ROPE_THETA) ** (-np.arange(0, ROT_DIM, 2, dtype=np.float32) / np.float32(ROT_DIM))
    ang = np.arange(s, dtype=np.float32)[:, None] * inv.astype(np.float32)[None, :]
    cos, sin = np.cos(ang).astype(np.float32), np.sin(ang).astype(np.float32)
    z8 = np.zeros((s, 8), np.float32)
    z48 = np.zeros((s, HEAD_DIM - ROT_DIM), np.float32)
    c = np.concatenate([cos, cos, np.ones((s, HEAD_DIM - ROT_DIM), np.float32)], axis=1)
    s_lo = np.concatenate([z8, sin, z48], axis=1)
    s_hi = np.concatenate([-sin, z8, z48], axis=1)
    return tuple(jnp.asarray(np.concatenate([a, a], axis=1)) for a in (c, s_lo, s_hi))


def _rope(x, c, s_lo, s_hi):
    n = x.shape[1]
    reps = n // 128
    c, s_lo, s_hi = (jnp.tile(a, (1, reps)) for a in (c, s_lo, s_hi))
    return x * c + pltpu.roll(x, 8, 1) * s_lo + pltpu.roll(x, n - 8, 1) * s_hi


def _rope_t(dx, c, s_lo, s_hi):
    n = dx.shape[1]
    reps = n // 128
    c, s_lo, s_hi = (jnp.tile(a, (1, reps)) for a in (c, s_lo, s_hi))
    return dx * c + pltpu.roll(dx * s_lo, n - 8, 1) + pltpu.roll(dx * s_hi, 8, 1)


def _in_proj0_own(order, x, g0, w_own, rider):
    s = x.shape[0]
    n = w_own.shape[1]

    def body(order_ref, x_ref, g_ref, w_ref, h_ref, z_ref):
        _, _, h = _rms_fwd(x_ref[...], g_ref[...])
        h = h.astype(BF16)
        h_ref[...] = h
        z_ref[...] = _dot(h, w_ref[...]).astype(BF16)

    return pl.pallas_call(
        rider.carried_by(body, 4, 2, s // TOK), name="in_proj0_own",
        grid_spec=pltpu.PrefetchScalarGridSpec(
            num_scalar_prefetch=1, grid=(s // TOK,),
            in_specs=[pl.BlockSpec((TOK, D_MODEL), lambda i, o: (i, 0)), pl.BlockSpec((1, D_MODEL), lambda i, o: (0, 0)),
                      pl.BlockSpec(w_own.shape, lambda i, o: (0, 0))] + rider.in_specs,
            out_specs=[pl.BlockSpec((TOK, D_MODEL), lambda i, o: (i, 0)),
                       pl.BlockSpec((TOK, n), lambda i, o: (i, o[0]))] + rider.out_specs,
            scratch_shapes=rider.scratch),
        out_shape=[jax.ShapeDtypeStruct((s, D_MODEL), BF16), jax.ShapeDtypeStruct((s, 4 * n), BF16)] + rider.out_shape,
        input_output_aliases={4 + j: 2 + j for j in range(len(rider.fulls))},
        compiler_params=_params(),
    )(order, x, g0, w_own, *rider.fulls)


def _in_proj0_rest(order, h0, w_in0, z0, rider):
    s = h0.shape[0]
    tok = min(s, 4 * TOK)
    n_tiles = s // tok
    n = w_in0.shape[1] // 4

    def body(order_ref, h_ref, w_ref, z_in_ref, z_ref):
        z_ref[...] = _dot(h_ref[...], w_ref[...]).astype(BF16)

    return pl.pallas_call(
        rider.carried_by(body, 4, 1, 3 * n_tiles, lambda: pl.program_id(0) * n_tiles + pl.program_id(1)),
        name="in_proj0_rest",
        grid_spec=pltpu.PrefetchScalarGridSpec(
            num_scalar_prefetch=1, grid=(3, n_tiles),
            in_specs=[pl.BlockSpec((tok, D_MODEL), lambda k, i, o: (i, 0)),
                      pl.BlockSpec((w_in0.shape[0], n), lambda k, i, o: (0, o[1 + k])), ANY] + rider.in_specs,
            out_specs=[pl.BlockSpec((tok, n), lambda k, i, o: (i, o[1 + k]))] + rider.out_specs,
            scratch_shapes=rider.scratch),
        out_shape=[jax.ShapeDtypeStruct(z0.shape, BF16)] + rider.out_shape,
        input_output_aliases={3: 0, **{4 + j: 1 + j for j in range(len(rider.fulls))}},
        compiler_params=_params(),
    )(order, h0, w_in0, z0, *rider.fulls)


def _halo_specs(s, col_block):
    per = TOK // HALO
    last = s // HALO - 1
    prev = pl.BlockSpec((HALO, 1024), lambda i: (jnp.maximum(i * per - 1, 0), col_block))
    nxt = pl.BlockSpec((HALO, 1024), lambda i: (jnp.minimum((i + 1) * per, last), col_block))
    return prev, nxt


def _with_halo(i, n_tiles, prev_ref, cur, next_ref):
    prev = prev_ref[...]
    nxt = next_ref[...]
    prev = jnp.where(i > 0, prev, jnp.zeros_like(prev))
    nxt = jnp.where(i < n_tiles - 1, nxt, jnp.zeros_like(nxt))
    return jnp.concatenate([prev, cur, nxt], axis=0)


def _mixer_a(v1, gv, ws_ref, bsx):
    rv, vh, v2 = _rms_fwd(v1, gv)
    v2 = v2.astype(BF16)
    rows = []
    for c in range(v1.shape[0] // CHUNK):
        cols = [_dot(ws_ref[h], v2[c * CHUNK:(c + 1) * CHUNK, h * 256:(h + 1) * 256]) for h in range(4)]
        rows.append(jnp.concatenate(cols, axis=1) + bsx)
    return rv, vh, v2, jnp.concatenate(rows, axis=0)


def _mixer_b_pooled(bx, halo, band_ref, counts):
    out = []
    for g in range(4):
        win = _dot(band_ref[g], halo[:, g * 256:(g + 1) * 256])
        out.append(win / counts[g] - bx[:, g * 256:(g + 1) * 256])
    return out


def _mix0_fwd(x, z0, w_out0, gv, ws, bsx, wg, scale, band, rider):
    s = x.shape[0]
    n_tiles = s // TOK
    k = len(rider.fulls)

    def body(z_ref, zp_ref, zn_ref, x_ref, wout_ref, gv_ref, ws_ref, bsx_ref, wg_ref, sc_ref, band_ref,
             cat_ref, x1_ref):
        i = pl.program_id(0)
        au = z_ref[:, 0:1024].astype(F32)
        av = z_ref[:, 1024:2048].astype(F32)
        ag = z_ref[:, 2048:3072].astype(F32)
        _, _, _, mixed = _mixer_a(_gelu(av), gv_ref[...], ws_ref, bsx_ref[...])
        cat_ref[:, 0:1024] = (_gelu(au) * mixed * (ag * _sigmoid(ag))).astype(BF16)

        bx16 = z_ref[:, 3072:4096]
        bg = z_ref[:, 4096:5120].astype(F32)
        halo = _with_halo(i, n_tiles, zp_ref, bx16, zn_ref)
        ps = _mixer_b_pooled(bx16.astype(F32), halo, band_ref, _window_counts(i, TOK, s))
        pw = jnp.concatenate([_dot(ps[g].astype(BF16), wg_ref[g]) for g in range(4)], axis=1)
        cat_ref[:, 1024:2048] = (pw * sc_ref[...] * (bg * _sigmoid(bg))).astype(BF16)

        x1_ref[...] = x_ref[...] + _dot(cat_ref[...], wout_ref[...])

    prev, nxt = _halo_specs(s, 3)
    return pl.pallas_call(
        rider.carried_by(body, 11, 2, n_tiles), name="mix0_fwd", grid=(n_tiles,),
        in_specs=[_rows(TOK, 5120), prev, nxt, _rows(TOK, D_MODEL), _whole(w_out0.shape), _whole((1, 1024)),
                  _whole(ws.shape), _whole(bsx.shape), _whole(wg.shape), _whole((1, 1024)), _whole(band.shape)]
        + rider.in_specs,
        out_specs=[_rows(TOK, 2048), _rows(TOK, D_MODEL)] + rider.out_specs,
        out_shape=[jax.ShapeDtypeStruct((s, 2048), BF16), jax.ShapeDtypeStruct((s, D_MODEL), F32)] + rider.out_shape,
        input_output_aliases={11 + j: 2 + j for j in range(k)},
        scratch_shapes=rider.scratch,
        compiler_params=_params(),
    )(z0, z0, z0, x, w_out0, gv, ws, bsx, wg, scale, band, *rider.fulls)


def _in_proj1(x1, g1, w_in1, rope):
    s = x1.shape[0]

    def body(x_ref, g_ref, w_ref, c_ref, lo_ref, hi_ref, h_ref, q_ref, k_ref, v_ref, gate_ref):
        _, _, h = _rms_fwd(x_ref[...], g_ref[...])
        h = h.astype(BF16)
        h_ref[...] = h
        tabs = (c_ref[...], lo_ref[...], hi_ref[...])
        q_ref[...] = (_rope(_dot(h, w_ref[:, 0:1024]), *tabs) * Q_SCALE).astype(BF16)
        kv = _dot(h, w_ref[:, 1024:1536])
        k_ref[...] = _rope(kv[:, 0:256], *tabs).astype(BF16)
        v_ref[...] = kv[:, 256:512].astype(BF16)
        gate_ref[...] = _dot(h, w_ref[:, 1536:2560]).astype(BF16)

    tab = _rows(TOK, 128)
    return pl.pallas_call(
        body, name="in_proj1", grid=(s // TOK,),
        in_specs=[_rows(TOK, D_MODEL), _whole((1, D_MODEL)), _whole(w_in1.shape), tab, tab, tab],
        out_specs=[_rows(TOK, 1024), _rows(TOK, 1024), _rows(TOK, 256), _rows(TOK, 256), _rows(TOK, 1024)],
        out_shape=[jax.ShapeDtypeStruct((s, 1024), BF16), jax.ShapeDtypeStruct((s, 1024), BF16),
                   jax.ShapeDtypeStruct((s, 256), BF16), jax.ShapeDtypeStruct((s, 256), BF16),
                   jax.ShapeDtypeStruct((s, 1024), BF16)],
        compiler_params=_params(),
    )(x1, g1, w_in1, *rope)


QBLK = 128
KBLK = QBLK + 2 * ATTN_WINDOW
Q_SCALE = HEAD_DIM ** -0.5
HEAD_BATCH = 4


def _block_bias(q0, s):
    r = lax.broadcasted_iota(jnp.int32, (QBLK, KBLK), 0)
    c = lax.broadcasted_iota(jnp.int32, (QBLK, KBLK), 1)
    kj = q0 - ATTN_WINDOW + c
    ok = (c >= r) & (c <= r + 2 * ATTN_WINDOW) & (kj >= 0) & (kj < s)
    return jnp.where(ok, 0.0, NEG_INF)


def _attn_fwd(q, kpad, vpad, sink):
    s = q.shape[0]

    def body(sink_ref, q_ref, k_ref, v_ref, o_ref, lse_ref):
        i = pl.program_id(0)
        lane = lax.broadcasted_iota(jnp.int32, (1, 128), 1)
        for b in range(TOK // QBLK):
            rows = slice(b * QBLK, (b + 1) * QBLK)
            start = pl.multiple_of(i * TOK + b * QBLK, QBLK)
            kb = k_ref[pl.ds(start, KBLK), :]
            vb = v_ref[pl.ds(start, KBLK), :]
            bias = _block_bias(i * TOK + b * QBLK, s)
            kv = lambda t, h: t[:, (h // 4) * HEAD_DIM:(h // 4 + 1) * HEAD_DIM]
            scs = [_dot_nt(q_ref[rows, h * HEAD_DIM:(h + 1) * HEAD_DIM], kv(kb, h)) + bias for h in range(N_HEADS)]
            ms = [jnp.maximum(jnp.max(scs[h], axis=-1, keepdims=True), sink_ref[h]) for h in range(N_HEADS)]
            es = [jnp.exp(scs[h] - ms[h]) for h in range(N_HEADS)]
            dens = [jnp.sum(es[h], axis=-1, keepdims=True) + jnp.exp(sink_ref[h] - ms[h]) for h in range(N_HEADS)]
            outs = [_dot(es[h].astype(BF16), kv(vb, h)) * (1.0 / dens[h]) for h in range(N_HEADS)]
            o_ref[rows, :] = jnp.concatenate(outs, axis=1).astype(BF16)
            lse = jnp.zeros((QBLK, 128), F32)
            for h in range(N_HEADS):
                lse = lse + jnp.where(lane == h, ms[h] + jnp.log(dens[h]), 0.0)
            lse_ref[rows, :] = lse

    return pl.pallas_call(
        body, name="attn_fwd", grid=(s // TOK,),
        in_specs=[pl.BlockSpec(memory_space=pltpu.SMEM), _rows(TOK, 1024), _whole(kpad.shape), _whole(vpad.shape)],
        out_specs=[_rows(TOK, 1024), _rows(TOK, 128)],
        out_shape=[jax.ShapeDtypeStruct((s, 1024), BF16), jax.ShapeDtypeStruct((s, 128), F32)],
        compiler_params=_params(),
    )(sink, q, kpad, vpad)


def _tail(x1, o, gate, target, w_out1, gf):
    s = x1.shape[0]

    def body(x1_ref, o_ref, gate_ref, t_ref, w_ref, gf_ref, y1_ref, dx2_ref, do_ref, dgate_ref, loss_ref, gfn_ref,
             dx2h_ref):
        i = pl.program_id(0)

        @pl.when(i == 0)
        def _():
            loss_ref[...] = jnp.zeros_like(loss_ref)
            gfn_ref[...] = jnp.zeros_like(gfn_ref)

        g = gate_ref[...].astype(F32)
        sg = _sigmoid(g)
        sil = g * sg
        o = o_ref[...].astype(F32)
        y1 = (o * sil).astype(BF16)
        y1_ref[...] = y1
        x2 = x1_ref[...] + _dot(y1, w_ref[...])
        gf = gf_ref[...]
        r, xh, out = _rms_fwd(x2, gf)
        diff = out - t_ref[...]
        loss_ref[...] += jnp.sum(diff * diff, axis=0, keepdims=True) * (0.5 / D_MODEL)
        dout = diff * (1.0 / D_MODEL)
        gfn_ref[...] += jnp.sum(dout * xh, axis=0, keepdims=True)
        dx2 = _rms_bwd(dout, gf, r, xh)
        dx2_ref[...] = dx2
        dx2h = dx2.astype(BF16)
        dx2h_ref[...] = dx2h
        dy1 = _dot_nt(dx2h, w_ref[...])
        do_ref[...] = (dy1 * sil).astype(BF16)
        dgate_ref[...] = (dy1 * o * (sg * (1.0 + g * (1.0 - sg)))).astype(BF16)

    row = _rows(TOK, 1024)
    acc = _whole((1, 1024))
    return pl.pallas_call(
        body, name="tail", grid=(s // TOK,),
        in_specs=[row, row, row, row, _whole(w_out1.shape), acc],
        out_specs=[row, row, row, row, acc, acc, row],
        out_shape=[jax.ShapeDtypeStruct((s, 1024), BF16), jax.ShapeDtypeStruct((s, 1024), F32),
                   jax.ShapeDtypeStruct((s, 1024), BF16), jax.ShapeDtypeStruct((s, 1024), BF16),
                   jax.ShapeDtypeStruct((1, 1024), F32), jax.ShapeDtypeStruct((1, 1024), F32),
                   jax.ShapeDtypeStruct((s, 1024), BF16)],
        compiler_params=_params(),
    )(x1, o, gate, target, w_out1, gf)


def _attn_bwd(q, kpad, vpad, sink, o, lse, do, rope):
    s = q.shape[0]
    pad_t = (kpad.shape[1], kpad.shape[0])

    def body(sink_ref, q_ref, k_ref, v_ref, o_ref, lse_ref, do_ref, c_ref, lo_ref, hi_ref,
             dq_ref, dk_ref, dv_ref, ds_ref):
        i = pl.program_id(0)

        @pl.when(i == 0)
        def _():
            dk_ref[...] = jnp.zeros_like(dk_ref)
            dv_ref[...] = jnp.zeros_like(dv_ref)
            ds_ref[...] = jnp.zeros_like(ds_ref)

        lane = lax.broadcasted_iota(jnp.int32, (1, 128), 1)
        dsink = jnp.zeros((1, 128), F32)
        for b in range(TOK // QBLK):
            rows = slice(b * QBLK, (b + 1) * QBLK)
            start = pl.multiple_of(i * TOK + b * QBLK, QBLK)
            kb = k_ref[pl.ds(start, KBLK), :]
            vb = v_ref[pl.ds(start, KBLK), :]
            bias = _block_bias(i * TOK + b * QBLK, s)
            kv = lambda t, h: t[:, (h // 4) * HEAD_DIM:(h // 4 + 1) * HEAD_DIM]
            hd = lambda ref, h: ref[rows, h * HEAD_DIM:(h + 1) * HEAD_DIM]
            dqs, dks, dvs = {}, {}, {}
            for h0 in range(0, N_HEADS, HEAD_BATCH):
                heads = range(h0, h0 + HEAD_BATCH)
                lses = {h: lse_ref[rows, h:h + 1] for h in heads}
                ps = {h: jnp.exp(_dot_nt(hd(q_ref, h), kv(kb, h)) + bias - lses[h]) for h in heads}
                dos = {h: hd(do_ref, h) for h in heads}
                deltas = {h: jnp.sum(dos[h].astype(F32) * hd(o_ref, h).astype(F32), axis=-1, keepdims=True)
                          for h in heads}
                for h in heads:
                    dsink = dsink + jnp.where(
                        lane == h, -jnp.sum(jnp.exp(sink_ref[h] - lses[h]) * deltas[h], axis=0, keepdims=True), 0.0)
                dscs = {h: (ps[h] * (_dot_nt(dos[h], kv(vb, h)) - deltas[h])).astype(BF16) for h in heads}
                dvs.update({h: _dot_tn(dos[h], ps[h].astype(BF16)) for h in heads})
                dks.update({h: _dot_tn(hd(q_ref, h), dscs[h]) for h in heads})
                dqs.update({h: _dot(dscs[h], kv(kb, h)) * Q_SCALE for h in heads})
            dq = jnp.concatenate([dqs[h] for h in range(N_HEADS)], axis=1)
            dq_ref[rows, :] = _rope_t(dq, c_ref[rows, :], lo_ref[rows, :], hi_ref[rows, :]).astype(BF16)
            group_sum = lambda parts, g: (parts[4 * g] + parts[4 * g + 1]) + (parts[4 * g + 2] + parts[4 * g + 3])
            dk_ref[:, pl.ds(start, KBLK)] += jnp.concatenate([group_sum(dks, g) for g in range(4)], axis=0)
            dv_ref[:, pl.ds(start, KBLK)] += jnp.concatenate([group_sum(dvs, g) for g in range(4)], axis=0)
        ds_ref[...] += dsink

    row = _rows(TOK, 1024)
    tab = _rows(TOK, 128)
    pad = _whole(kpad.shape)
    return pl.pallas_call(
        body, name="attn_bwd", grid=(s // TOK,),
        in_specs=[pl.BlockSpec(memory_space=pltpu.SMEM), row, pad, pad, row, tab, row, tab, tab, tab],
        out_specs=[row, _whole(pad_t), _whole(pad_t), _whole((1, 128))],
        out_shape=[jax.ShapeDtypeStruct((s, 1024), BF16), jax.ShapeDtypeStruct(pad_t, F32),
                   jax.ShapeDtypeStruct(pad_t, F32), jax.ShapeDtypeStruct((1, 128), F32)],
        compiler_params=_params(),
    )(sink, q, kpad, vpad, o, lse, do, *rope)


def _in_proj1_bwd(dq, dk_t, dv_t, dgate, x1, dx2, g1, w_in1, rope):
    s = x1.shape[0]

    def body(dq_ref, dka_ref, dkb_ref, dva_ref, dvb_ref, dgate_ref, x1_ref, dx2_ref, g_ref, w_ref,
             c_ref, lo_ref, hi_ref, dz_ref, dx1_ref, gn_ref, dx1h_ref):
        i = pl.program_id(0)
        dk = jnp.concatenate([dka_ref[...], dkb_ref[...]], axis=1).T
        dv = jnp.concatenate([dva_ref[...], dvb_ref[...]], axis=1).T

        @pl.when(i == 0)
        def _():
            gn_ref[...] = jnp.zeros_like(gn_ref)

        dz_ref[:, 0:1024] = dq_ref[...]
        dz_ref[:, 1024:1280] = _rope_t(dk, c_ref[...], lo_ref[...], hi_ref[...]).astype(BF16)
        dz_ref[:, 1280:1536] = dv.astype(BF16)
        dz_ref[:, 1536:2560] = dgate_ref[...]
        dh = _dot_nt(dz_ref[...], w_ref[...])
        g = g_ref[...]
        r, xh, _ = _rms_fwd(x1_ref[...], g)
        gn_ref[...] += jnp.sum(dh * xh, axis=0, keepdims=True)
        dx1 = dx2_ref[...] + _rms_bwd(dh, g, r, xh)
        dx1_ref[...] = dx1
        dx1h_ref[...] = dx1.astype(BF16)

    row = _rows(TOK, 1024)
    per = TOK // ATTN_WINDOW
    half_a = pl.BlockSpec((256, TOK // 2), lambda i: (0, per * i + 1))
    half_b = pl.BlockSpec((256, TOK // 2), lambda i: (0, per * i + 2))
    tab = _rows(TOK, 128)
    acc = _whole((1, 1024))
    return pl.pallas_call(
        body, name="in_proj1_bwd", grid=(s // TOK,),
        in_specs=[row, half_a, half_b, half_a, half_b, row, row, row, acc, _whole(w_in1.shape), tab, tab, tab],
        out_specs=[_rows(TOK, 2560), row, acc, row],
        out_shape=[jax.ShapeDtypeStruct((s, 2560), BF16), jax.ShapeDtypeStruct((s, 1024), F32),
                   jax.ShapeDtypeStruct((1, 1024), F32), jax.ShapeDtypeStruct((s, 1024), BF16)],
        compiler_params=_params(),
    )(dq, dk_t, dk_t, dv_t, dv_t, dgate, x1, dx2, g1, w_in1, *rope)


def _mix0_bwd(dx1, z0, w_out0, gv, ws, bsx, wg, scale, band, rider):
    s = dx1.shape[0]
    n_tiles = s // TOK

    def body(dx1h_ref, z_ref, zp_ref, zn_ref, wout_ref, gv_ref, ws_ref, bsx_ref, wg_ref, sc_ref, band_ref,
             dz_ref, dpn_ref, dws_ref, dbs_ref, dgv_ref, dsc_ref, dwg_ref):
        i = pl.program_id(0)
        dz_ref[:, 3072:4096] = jnp.zeros((TOK, 1024), BF16)

        @pl.when(i == 0)
        def _():
            for ref in (dws_ref, dbs_ref, dgv_ref, dsc_ref, dwg_ref):
                ref[...] = jnp.zeros_like(ref)

        dcat = _dot_nt(dx1h_ref[...], wout_ref[...])
        dya = dcat[:, 0:1024]
        dyb = dcat[:, 1024:2048]

        au = z_ref[:, 0:1024].astype(F32)
        av = z_ref[:, 1024:2048].astype(F32)
        ag = z_ref[:, 2048:3072].astype(F32)
        gv = gv_ref[...]
        u, du = _gelu_and_grad(au)
        v1, dv1 = _gelu_and_grad(av)
        rv, vh, v2, mixed = _mixer_a(v1, gv, ws_ref, bsx_ref[...])
        sg = _sigmoid(ag)
        sil = ag * sg
        dz_ref[:, 2048:3072] = (dya * u * mixed * (sg * (1.0 + ag * (1.0 - sg)))).astype(BF16)
        dz_ref[:, 0:1024] = (dya * mixed * sil * du).astype(BF16)
        dmixed = dya * u * sil
        lane = lax.broadcasted_iota(jnp.int32, (1, 128), 1)
        dm16 = dmixed.astype(BF16)
        dv2_rows = []
        for c in range(TOK // CHUNK):
            rows = slice(c * CHUNK, (c + 1) * CHUNK)
            cols_out = []
            for h in range(4):
                cols = slice(h * 256, (h + 1) * 256)
                dws_ref[h] += _dot_nt(dm16[rows, cols], v2[rows, cols])
                dbs_ref[...] += jnp.where(lane == h, jnp.sum(dmixed[rows, cols], axis=-1, keepdims=True), 0.0)
                cols_out.append(_dot_tn(ws_ref[h], dm16[rows, cols]))
            dv2_rows.append(jnp.concatenate(cols_out, axis=1))
        dv2 = jnp.concatenate(dv2_rows, axis=0)
        dgv_ref[...] += jnp.sum(dv2 * vh, axis=0, keepdims=True)
        dz_ref[:, 1024:2048] = (_rms_bwd(dv2, gv, rv, vh) * dv1).astype(BF16)

        bx16 = z_ref[:, 3072:4096]
        bg = z_ref[:, 4096:5120].astype(F32)
        counts = _window_counts(i, TOK, s)
        halo = _with_halo(i, n_tiles, zp_ref, bx16, zn_ref)
        ps = [p.astype(BF16) for p in _mixer_b_pooled(bx16.astype(F32), halo, band_ref, counts)]
        pw = jnp.concatenate([_dot(ps[g], wg_ref[g]) for g in range(4)], axis=1)
        sgb = _sigmoid(bg)
        sc = sc_ref[...]
        dz_ref[:, 4096:5120] = (dyb * pw * sc * (sgb * (1.0 + bg * (1.0 - sgb)))).astype(BF16)
        dys = dyb * (bg * sgb)
        dsc_ref[...] += jnp.sum(dys * pw, axis=0, keepdims=True)
        dpw = (dys * sc).astype(BF16)
        for g in range(4):
            cols = slice(g * 256, (g + 1) * 256)
            dwg_ref[g] += _dot_tn(ps[g], dpw[:, cols])
            dpn_ref[:, cols] = (_dot_nt(dpw[:, cols], wg_ref[g]) / counts[g]).astype(BF16)

    prev, nxt = _halo_specs(s, 3)
    row = _rows(TOK, 1024)
    vec = _whole((1, 1024))
    return pl.pallas_call(
        rider.carried_by(body, 11, 7, n_tiles), name="mix0_bwd", grid=(n_tiles,),
        in_specs=[row, _rows(TOK, 5120), prev, nxt, _whole(w_out0.shape), vec, _whole(ws.shape),
                  _whole(bsx.shape), _whole(wg.shape), vec, _whole(band.shape)] + rider.in_specs,
        out_specs=[_rows(TOK, 5120), row, _whole((4, 128, 128)), _whole((128, 128)), vec, vec,
                   _whole((4, 256, 256))] + rider.out_specs,
        out_shape=[jax.ShapeDtypeStruct((s, 5120), BF16), jax.ShapeDtypeStruct((s, 1024), BF16),
                   jax.ShapeDtypeStruct((4, 128, 128), F32), jax.ShapeDtypeStruct((128, 128), F32),
                   jax.ShapeDtypeStruct((1, 1024), F32), jax.ShapeDtypeStruct((1, 1024), F32),
                   jax.ShapeDtypeStruct((4, 256, 256), F32)] + rider.out_shape,
        scratch_shapes=rider.scratch,
        compiler_params=_params(),
    )(dx1, z0, z0, z0, w_out0, gv, ws, bsx, wg, scale, band, *rider.parts)


def _fill_pooled_grad(dz0, dpn, band_t):
    s = dpn.shape[0]
    n_tiles = s // TOK

    def body(dz_in_ref, dpn_ref, dpp_ref, dpx_ref, band_ref, dbx_ref):
        i = pl.program_id(0)
        dpn = dpn_ref[...]
        halo = _with_halo(i, n_tiles, dpp_ref, dpn, dpx_ref)
        counts = _window_counts(i, TOK, s)
        for g in range(4):
            cols = slice(g * 256, (g + 1) * 256)
            dbx = _dot(band_ref[g], halo[:, cols]) - dpn[:, cols].astype(F32) * counts[g]
            dbx_ref[:, cols] = dbx.astype(BF16)

    prev, nxt = _halo_specs(s, 0)
    return pl.pallas_call(
        body, name="fill_pooled_grad", grid=(n_tiles,),
        in_specs=[ANY, _rows(TOK, 1024), prev, nxt, _whole(band_t.shape)],
        out_specs=pl.BlockSpec((TOK, 1024), lambda i: (i, 3)),
        out_shape=jax.ShapeDtypeStruct(dz0.shape, BF16),
        input_output_aliases={0: 0},
        compiler_params=_params(),
    )(dz0, dpn, dpn, dpn, band_t)


def _in_proj0_bwd(dz0, x, dx1, g0, w_in0, rider):
    s = x.shape[0]
    n_tiles = s // TOK

    def body(dz_ref, x_ref, dx1_ref, g_ref, w_ref, dx_ref, gn_ref):
        i = pl.program_id(0)

        @pl.when(i == 0)
        def _():
            gn_ref[...] = jnp.zeros_like(gn_ref)

        dh = _dot_nt(dz_ref[...], w_ref[...])
        g0v = g_ref[...]
        r, xh, _ = _rms_fwd(x_ref[...], g0v)
        gn_ref[...] += jnp.sum(dh * xh, axis=0, keepdims=True)
        dx_ref[...] = dx1_ref[...] + _rms_bwd(dh, g0v, r, xh)

    row = _rows(TOK, 1024)
    vec = _whole((1, 1024))
    return pl.pallas_call(
        rider.carried_by(body, 5, 2, n_tiles), name="in_proj0_bwd", grid=(n_tiles,),
        in_specs=[_rows(TOK, 5120), row, row, vec, _whole(w_in0.shape)] + rider.in_specs,
        out_specs=[row, vec] + rider.out_specs,
        out_shape=[jax.ShapeDtypeStruct((s, 1024), F32), jax.ShapeDtypeStruct((1, 1024), F32)] + rider.out_shape,
        scratch_shapes=rider.scratch,
        compiler_params=_params(),
    )(dz0, x, dx1, g0, w_in0, *rider.parts)


def _weight_grad(a, b, n_blocks, split, name):
    s, k = a.shape
    n = b.shape[1]
    tn = n // n_blocks
    w = tn // split
    ts = min(s, 1024)

    def body(a_ref, b_ref, o_ref):
        @pl.when(pl.program_id(1) == 0)
        def _():
            o_ref[...] = jnp.zeros_like(o_ref)

        res = _dot_tn(a_ref[...], b_ref[...])
        for q in range(split):
            o_ref[q] += res[:, q * w:(q + 1) * w]

    return pl.pallas_call(
        body, name=name, grid=(n_blocks, s // ts),
        in_specs=[pl.BlockSpec((ts, k), lambda j, t: (t, 0)), pl.BlockSpec((ts, tn), lambda j, t: (t, j))],
        out_specs=pl.BlockSpec((split, k, w), lambda j, t: (j, 0, 0)),
        out_shape=jax.ShapeDtypeStruct((n_blocks * split, k, w), F32),
        compiler_params=_params(),
    )(a, b)


def _row_tile(rows, cols):
    t = rows
    while t * cols * 4 > (1 << 20) and t % 16 == 0:
        t //= 2
    return t


def _add_sibling(where, g, theirs, name):
    _, _, rows, cols = g.shape
    t = _row_tile(rows, cols)

    def body(where_ref, g_ref, t_ref, o_ref):
        o_ref[...] = (g_ref[...] + t_ref[...]).astype(BF16)

    spec = pl.BlockSpec((None, t, cols), lambda s, i, p: (s, i, 0))
    return pl.pallas_call(
        body, name=name, out_shape=jax.ShapeDtypeStruct((4, rows, cols), BF16),
        grid_spec=pltpu.PrefetchScalarGridSpec(
            num_scalar_prefetch=1, grid=(4, rows // t),
            in_specs=[pl.BlockSpec((None, None, t, cols), lambda s, i, p: (s, p[1], i, 0)), spec], out_specs=spec),
        compiler_params=_params())(where, g, theirs)


def _sum_chips(where, g, theirs, slots, name):
    _, _, rows, cols = g.shape
    t = _row_tile(rows, cols)

    def body(where_ref, g_ref, t_ref, s_ref, o_ref):
        me = where_ref[0]
        own = g_ref[...] + t_ref[...]
        acc = jnp.where(me == 0, own, s_ref[0].astype(F32))
        for k in range(1, 4):
            acc = acc + jnp.where(me == k, own, s_ref[k].astype(F32))
        o_ref[...] = acc

    return pl.pallas_call(
        body, name=name, out_shape=jax.ShapeDtypeStruct((rows, cols), F32),
        grid_spec=pltpu.PrefetchScalarGridSpec(
            num_scalar_prefetch=1, grid=(rows // t,),
            in_specs=[pl.BlockSpec((None, None, t, cols), lambda i, p: (p[0], p[1], i, 0)),
                      pl.BlockSpec((None, t, cols), lambda i, p: (p[0], i, 0)),
                      pl.BlockSpec((4, t, cols), lambda i, p: (0, i, 0))],
            out_specs=pl.BlockSpec((t, cols), lambda i, p: (i, 0))),
        compiler_params=_params())(where, g, theirs, slots)


def _adamw_halves(where, w, own, theirs, m, v, name):
    rows, cols = own.shape
    t = _row_tile(rows, cols)
    per = rows // t

    def body(where_ref, w_ref, own_ref, th_ref, m_ref, v_ref, g_ref, d_ref, nm_ref, nv_ref):
        g = jnp.where(pl.program_id(0) == where_ref[1], own_ref[...], th_ref[...])
        g_ref[...] = g
        m2 = ADAM_B1 * m_ref[...] + (1.0 - ADAM_B1) * g
        v2 = ADAM_B2 * v_ref[...] + (1.0 - ADAM_B2) * (g * g)
        m_hat = m2 / (1.0 - ADAM_B1 ** ADAM_STEP)
        v_hat = v2 / (1.0 - ADAM_B2 ** ADAM_STEP)
        d_ref[...] = -ADAM_LR * (m_hat / (jnp.sqrt(v_hat) + ADAM_EPS) + ADAM_WD * w_ref[...])
        nm_ref[...] = m2
        nv_ref[...] = v2

    full = pl.BlockSpec((t, cols), lambda h, i, p: (h * per + i, 0))
    half = pl.BlockSpec((t, cols), lambda h, i, p: (i, 0))
    shp = jax.ShapeDtypeStruct(w.shape, F32)
    return pl.pallas_call(
        body, name=name, out_shape=[shp] * 4,
        grid_spec=pltpu.PrefetchScalarGridSpec(
            num_scalar_prefetch=1, grid=(2, per), in_specs=[full, half, half, full, full], out_specs=[full] * 4),
        compiler_params=_params())(where, w, own, theirs, m, v)


def _place_shard(where, w, cut, name):
    if cut.kind == "cols":
        r, n = cut.full_shape
        blk, grid = (256, n // 4), (r // 256,)
        src_map, dst_map = (lambda i, p: (i, 0)), (lambda i, p: (i, p[0]))
    elif cut.kind == "rows":
        r, n = cut.full_shape
        per = r // 4 // 256
        blk, grid = (256, n), (per,)
        src_map, dst_map = (lambda i, p: (i, 0)), (lambda i, p: (p[0] * per + i, 0))
    else:
        g, r, n = cut.full_shape
        blk, grid = (g, r // 4, n), (1,)
        src_map, dst_map = (lambda i, p: (0, 0, 0)), (lambda i, p: (0, p[0], 0))

    def body(where_ref, w_ref, o_ref):
        o_ref[...] = w_ref[...].astype(BF16)

    return pl.pallas_call(
        body, name=name, out_shape=jax.ShapeDtypeStruct(cut.full_shape, BF16),
        grid_spec=pltpu.PrefetchScalarGridSpec(
            num_scalar_prefetch=1, grid=grid, in_specs=[pl.BlockSpec(blk, src_map)],
            out_specs=pl.BlockSpec(blk, dst_map)),
        compiler_params=_params())(where, w)


def _sum_small(first, second, third):
    rows = second.shape[1]

    def body(a_ref, b_ref, c_ref, o_ref):
        top = a_ref[0] + c_ref[0] + b_ref[0, 0:8]
        rest = b_ref[0, 8:rows]
        for k in range(1, 8):
            top = top + (a_ref[k] + c_ref[k] + b_ref[k, 0:8])
            rest = rest + b_ref[k, 8:rows]
        o_ref[0:8] = top
        o_ref[8:rows] = rest

    return pl.pallas_call(
        body, name="sum_small", in_specs=[_whole(first.shape), _whole(second.shape), _whole(third.shape)],
        out_specs=_whole(second.shape[1:]), out_shape=jax.ShapeDtypeStruct(second.shape[1:], F32),
        compiler_params=_params())(first, second, third)


def _adamw(w, g, m, v, name):
    rows, cols = w.shape
    t = _row_tile(rows, cols)

    def body(w_ref, g_ref, m_ref, v_ref, d_ref, nm_ref, nv_ref):
        g = g_ref[...]
        m2 = ADAM_B1 * m_ref[...] + (1.0 - ADAM_B1) * g
        v2 = ADAM_B2 * v_ref[...] + (1.0 - ADAM_B2) * (g * g)
        m_hat = m2 / (1.0 - ADAM_B1 ** ADAM_STEP)
        v_hat = v2 / (1.0 - ADAM_B2 ** ADAM_STEP)
        d_ref[...] = -ADAM_LR * (m_hat / (jnp.sqrt(v_hat) + ADAM_EPS) + ADAM_WD * w_ref[...])
        nm_ref[...] = m2
        nv_ref[...] = v2

    spec = pl.BlockSpec((t, cols), lambda i: (i, 0))
    shp = jax.ShapeDtypeStruct(w.shape, F32)
    return pl.pallas_call(body, name=name, grid=(rows // t,), in_specs=[spec] * 4, out_specs=[spec] * 3,
                          out_shape=[shp] * 3, compiler_params=_params())(w, g, m, v)


def _place():
    x, y, c = lax.axis_index("x"), lax.axis_index("y"), lax.axis_index("c")
    chips = [(1 - x, y), (x, 1 - y), (1 - x, 1 - y)]
    return x, y, c, chips


class _Sharded:
    def __init__(self, kind, full_shape):
        self.kind = kind
        self.full_shape = full_shape

    def in_full(self, ref, s, h):
        if self.kind == "cols":
            r, n = self.full_shape
            return ref.at[pl.ds(h * (r // 2), r // 2), pl.ds(pl.multiple_of(s * (n // 4), 128), n // 4)]
        if self.kind == "rows":
            r, _ = self.full_shape
            return ref.at[pl.ds(pl.multiple_of(s * (r // 4) + h * (r // 8), 8), r // 8), :]
        g, r, _ = self.full_shape
        return ref.at[pl.ds(h * (g // 2), g // 2), pl.ds(pl.multiple_of(s * (r // 4), 16), r // 4), :]


def _remote(src, dst, send_sem, recv_sem, to):
    return pltpu.make_async_remote_copy(src_ref=src, dst_ref=dst, send_sem=send_sem, recv_sem=recv_sem,
                                        device_id=to, device_id_type=MESH)


def _start_remote(src, dst, send_sem, recv_sem, to):
    cp = _remote(src, dst, send_sem, recv_sem, to)
    cp.start()
    return cp


class _Gather:
    def __init__(self, fulls, cuts):
        n = len(fulls)
        self.fulls, self.cuts = list(fulls), list(cuts)
        self.in_specs = [ANY] * n
        self.out_specs = [ANY] * n
        self.out_shape = [jax.ShapeDtypeStruct(cut.full_shape, BF16) for cut in cuts]
        self.scratch = [pltpu.SemaphoreType.DMA((6 * n,)), pltpu.SemaphoreType.DMA((6 * n,))]

    def _step(self, step, src, out, send_sems, recv_sems):
        n, cuts = len(self.fulls), self.cuts
        x, y, c, chips = _place()
        me = 2 * x + y

        def ends(w, s, h, from_src):
            dst = cuts[w].in_full(out[w], s, h)
            return (cuts[w].in_full(src[w], s, h) if from_src else dst), dst

        for w in range(n):
            for j, chip in enumerate(chips):
                s = 2 * chip[0] + chip[1]
                k, k2 = 3 * w + j, 3 * n + 3 * w + j
                if step == "send":
                    _start_remote(*ends(w, me, c, True), send_sems.at[k], recv_sems.at[k], (*chip, c))
                elif step == "pass_on":
                    _remote(*ends(w, s, c, False), send_sems.at[k], recv_sems.at[k], (x, y, c)).wait_recv()
                    _start_remote(*ends(w, s, c, False), send_sems.at[k2], recv_sems.at[k2], (x, y, 1 - c))
                else:
                    _remote(*ends(w, s, 1 - c, False), send_sems.at[k2], recv_sems.at[k2], (x, y, c)).wait_recv()
                    _remote(*ends(w, me, c, True), send_sems.at[k], recv_sems.at[k], (x, y, c)).wait_send()
                    _remote(*ends(w, s, c, False), send_sems.at[k2], recv_sems.at[k2], (x, y, c)).wait_send()

    def carried_by(self, body, n_in, n_out, n_steps, step_index=lambda: pl.program_id(0)):
        k = len(self.fulls)

        def carrier(*refs):
            ins, src = refs[:n_in], refs[n_in:n_in + k]
            outs, out = refs[n_in + k:n_in + k + n_out], refs[n_in + k + n_out:n_in + 2 * k + n_out]
            sems = refs[n_in + 2 * k + n_out:]
            for step, at in (("send", 0), ("pass_on", 3 * n_steps // 4), ("finish", n_steps - 1)):
                if step == "finish":
                    body(*ins, *outs)

                @pl.when(step_index() == at)
                def _():
                    self._step(step, src, out, *sems)

        return carrier


def _exchange_halves(grads, name):
    n = len(grads)

    def body(*refs):
        g = refs[:n]
        theirs = refs[n:2 * n]
        send_sems, recv_sems = refs[2 * n:]
        x, y, c, _ = _place()
        sends = [_start_remote(g[w].at[:, 1 - c], theirs[w], send_sems.at[w], recv_sems.at[w], (x, y, 1 - c))
                 for w in range(n)]
        for w in range(n):
            _remote(g[w].at[:, 1 - c], theirs[w], send_sems.at[w], recv_sems.at[w], (x, y, c)).wait_recv()
        for cp in sends:
            cp.wait_send()

    return pl.pallas_call(
        body, name=name,
        in_specs=[ANY] * n, out_specs=[ANY] * n,
        out_shape=[jax.ShapeDtypeStruct((4,) + g.shape[2:], F32) for g in grads],
        scratch_shapes=[pltpu.SemaphoreType.DMA((n,)), pltpu.SemaphoreType.DMA((n,))],
        compiler_params=pltpu.CompilerParams(has_side_effects=True),
    )(*grads)


def _gather_small(small_ref, gathered, send_sems, recv_sems, first_sem, local_sem, start):
    x, y, c, _ = _place()
    me = 4 * x + 2 * y + c
    flips = [(fx, fy, fc) for fx in range(2) for fy in range(2) for fc in range(2)][1:]
    own = pltpu.make_async_copy(small_ref, gathered.at[me], local_sem)
    if start:
        own.start()
    else:
        own.wait()
    for k, (fx, fy, fc) in enumerate(flips):
        peer = (x + fx - 2 * x * fx, y + fy - 2 * y * fy, c + fc - 2 * c * fc)
        sems = (send_sems.at[first_sem + k], recv_sems.at[first_sem + k])
        if start:
            _start_remote(small_ref, gathered.at[me], *sems, peer)
        else:
            cp = _remote(small_ref, gathered.at[4 * peer[0] + 2 * peer[1] + peer[2]], *sems, (x, y, c))
            cp.wait_recv()
            cp.wait_send()


class _ScatterRider:
    def __init__(self, parts, small):
        n = len(parts)
        self.n = n
        self.parts = list(parts) + [small]
        self.in_specs = [ANY] * (n + 1)
        self.out_specs = [ANY] * (n + 1)
        self.out_shape = ([jax.ShapeDtypeStruct(a.shape, a.dtype) for a in parts]
                          + [jax.ShapeDtypeStruct((8,) + small.shape, small.dtype)])
        self.scratch = [pltpu.SemaphoreType.DMA((3 * n + 7,)), pltpu.SemaphoreType.DMA((3 * n + 7,)),
                        pltpu.SemaphoreType.DMA]

    def _copies(self, p, out, send_sems, recv_sems, local_sem, start):
        x, y, c, chips = _place()
        me = 2 * x + y
        n = self.n
        _gather_small(p[n], out[n], send_sems, recv_sems, 3 * n, local_sem, start)
        for w in range(n):
            for j, chip in enumerate(chips):
                s = 2 * chip[0] + chip[1]
                if start:
                    _start_remote(p[w].at[s], out[w].at[me], send_sems.at[3 * w + j], recv_sems.at[3 * w + j],
                                  (*chip, c))
                else:
                    cp = _remote(p[w].at[s], out[w].at[s], send_sems.at[3 * w + j], recv_sems.at[3 * w + j],
                                 (x, y, c))
                    cp.wait_recv()
                    cp.wait_send()

    def carried_by(self, body, n_in, n_out, n_tiles):
        k = len(self.parts)

        def carrier(*refs):
            ins, mine = refs[:n_in], refs[n_in:n_in + k]
            outs, slots = refs[n_in + k:n_in + k + n_out], refs[n_in + k + n_out:n_in + 2 * k + n_out]
            sems = refs[n_in + 2 * k + n_out:]

            @pl.when(pl.program_id(0) == 0)
            def _():
                self._copies(mine, slots, *sems, start=True)

            body(*ins, *outs)

            @pl.when(pl.program_id(0) == n_tiles - 1)
            def _():
                self._copies(mine, slots, *sems, start=False)

        return carrier


def _share_halves(halves, small):
    n = len(halves)

    def body(*refs):
        hv = refs[:n]
        small_ref = refs[n]
        out = refs[n + 1:2 * n + 1]
        gathered = refs[2 * n + 1]
        send_sems, recv_sems, local_sem = refs[2 * n + 2:]
        x, y, c, _ = _place()
        sends = [_start_remote(hv[w], out[w], send_sems.at[w], recv_sems.at[w], (x, y, 1 - c)) for w in range(n)]
        _gather_small(small_ref, gathered, send_sems, recv_sems, n, local_sem, True)
        for w in range(n):
            _remote(hv[w], out[w], send_sems.at[w], recv_sems.at[w], (x, y, c)).wait_recv()
        for cp in sends:
            cp.wait_send()
        _gather_small(small_ref, gathered, send_sems, recv_sems, n, local_sem, False)

    return pl.pallas_call(
        body, name="share_halves",
        in_specs=[ANY] * (n + 1), out_specs=[ANY] * (n + 1),
        out_shape=[jax.ShapeDtypeStruct(a.shape, F32) for a in halves] + [jax.ShapeDtypeStruct((8,) + small.shape, F32)],
        scratch_shapes=[pltpu.SemaphoreType.DMA((n + 7,)), pltpu.SemaphoreType.DMA((n + 7,)),
                        pltpu.SemaphoreType.DMA],
        compiler_params=pltpu.CompilerParams(has_side_effects=True),
    )(*halves, small)


SMALL_ROWS = 80


def _pack_small(vecs, ws, bs, sink, extra=None):
    ws = jnp.zeros((64, 1024), F32) if ws is None else ws.reshape(64, 1024)
    bs = jnp.zeros((1, 512), F32) if bs is None else bs.reshape(1, 512)
    sink = jnp.zeros((1, 16), F32) if sink is None else sink.reshape(1, 16)
    extra = jnp.zeros((1, 1024), F32) if extra is None else extra.reshape(1, 1024)
    top = jnp.concatenate(
        [v.reshape(1, 1024) for v in vecs]
        + [jnp.pad(bs, ((0, 0), (0, 512))), jnp.pad(sink, ((0, 0), (0, 1008))), extra], axis=0)
    return jnp.concatenate([top, ws, jnp.zeros((8, 1024), F32)], axis=0)


def _unpack_small(p):
    vecs = [p[k] for k in range(5)]
    return vecs, p[8:72].reshape(4, 128, 128), p[5, :512].reshape(4, 128), p[6, :16]


def kernel(x, norm_0, w_in_0, a_v_norm_0, a_spatial_w_0, a_spatial_b_0, b_group_w_0, b_scale_0, w_out_0, norm_1, w_in_1, sink_1, w_out_1, final_norm, loss_target, m_norm_0, m_w_in_0, m_a_v_norm_0, m_a_spatial_w_0, m_a_spatial_b_0, m_b_group_w_0, m_b_scale_0, m_w_out_0, m_norm_1, m_w_in_1, m_sink_1, m_w_out_1, m_final_norm, v_norm_0, v_w_in_0, v_a_v_norm_0, v_a_spatial_w_0, v_a_spatial_b_0, v_b_group_w_0, v_b_scale_0, v_w_out_0, v_norm_1, v_w_in_1, v_sink_1, v_w_out_1, v_final_norm):
    s = x.shape[1]
    xs = x.reshape(s, D_MODEL)
    target = loss_target.reshape(s, D_MODEL)

    cuts = [_Sharded("cols", (1024, 5120)), _Sharded("rows", (2048, 1024)), _Sharded("cols", (1024, 2560)),
            _Sharded("rows", (1024, 1024)), _Sharded("mid", (4, 256, 256))]
    big_w = [w_in_0, w_out_0, w_in_1, w_out_1, b_group_w_0]
    big_m = [m_w_in_0, m_w_out_0, m_w_in_1, m_w_out_1, m_b_group_w_0]
    big_v = [v_w_in_0, v_w_out_0, v_w_in_1, v_w_out_1, v_b_group_w_0]
    where = jnp.stack([2 * lax.axis_index("x") + lax.axis_index("y"), lax.axis_index("c")]).astype(jnp.int32)
    placed = [_place_shard(where, w, cut, f"place_shard{k}") for k, (w, cut) in enumerate(zip(big_w, cuts))]
    cx, cy = lax.axis_index("x"), lax.axis_index("y")
    order = jnp.stack([2 * cx + cy, 2 * (1 - cx) + cy, 2 * cx + 1 - cy, 2 * (1 - cx) + 1 - cy]).astype(jnp.int32)

    row = lambda v: v.reshape(1, 1024)
    ws16 = a_spatial_w_0.astype(BF16)
    bsx = jnp.repeat(a_spatial_b_0.T, 256, axis=1)
    band, band_t = _band_matrices(TOK)
    rope = _rope_tables(s)

    h0, z0, w_in0 = _in_proj0_own(order, xs, row(norm_0), w_in_0.astype(BF16), _Gather(placed[:1], cuts[:1]))
    z0, w_out0, wg = _in_proj0_rest(order, h0, w_in0, z0, _Gather([placed[1], placed[4]], [cuts[1], cuts[4]]))
    cat, x1, w_in1, w_out1 = _mix0_fwd(xs, z0, w_out0, row(a_v_norm_0), ws16, bsx, wg, row(b_scale_0), band,
                                       _Gather(placed[2:4], cuts[2:4]))
    h1, q, k, v, gate = _in_proj1(x1, row(norm_1), w_in1, rope)
    kpad = jnp.pad(k, ((ATTN_WINDOW, ATTN_WINDOW), (0, 0)))
    vpad = jnp.pad(v, ((ATTN_WINDOW, ATTN_WINDOW), (0, 0)))
    o, lse = _attn_fwd(q, kpad, vpad, sink_1)
    y1, dx2, do, dgate, loss_lanes, g_final, dx2h = _tail(x1, o, gate, target, w_out1, row(final_norm))

    dq, dkpad, dvpad, dsink = _attn_bwd(q, kpad, vpad, sink_1, o, lse, do, rope)
    dz1, dx1, g_norm1, dx1h = _in_proj1_bwd(dq, dkpad, dvpad, dgate, x1, dx2, row(norm_1), w_in1, rope)

    g_w_in1 = _weight_grad(h1, dz1, 2, 2, "grad_w_in1").reshape(4, 2, 512, 640)
    g_w_out1 = _weight_grad(y1, dx2h, 1, 1, "grad_w_out1").reshape(4, 2, 128, 1024)
    g_w_out0 = _weight_grad(cat, dx1h, 1, 1, "grad_w_out0").reshape(4, 2, 256, 1024)
    first = [g_w_out0, g_w_in1, g_w_out1]
    theirs1 = _exchange_halves(first, "exchange_halves1")
    parts1 = [_add_sibling(where, g, t, f"add_sibling1_{k}") for k, (g, t) in enumerate(zip(first, theirs1))]
    zero = jnp.zeros((1024,), F32)
    small1 = _pack_small([zero, zero, zero, g_norm1, g_final], None, None, dsink[0, :16])[:8]
    dz0, dpn, d_ws, d_bs, d_gv, d_scale, d_wg, *slots1, small1_all = _mix0_bwd(
        dx1h, z0, w_out0, row(a_v_norm_0), ws16, bsx, wg, row(b_scale_0), band, _ScatterRider(parts1, small1))
    dz0 = _fill_pooled_grad(dz0, dpn, band_t)
    g_w_in0 = _weight_grad(h0, dz0, 4, 1, "grad_w_in0").reshape(4, 2, 512, 1280)
    g_wg = d_wg.reshape(2, 2, 4, 64, 256).transpose(2, 0, 1, 3, 4).reshape(4, 2, 128, 256)
    second = [g_w_in0, g_wg]
    theirs2 = _exchange_halves(second, "exchange_halves2")
    parts2 = [_add_sibling(where, g, t, f"add_sibling2_{k}") for k, (g, t) in enumerate(zip(second, theirs2))]
    small2 = _pack_small([zero, d_gv, d_scale, zero, zero], d_ws, d_bs[:, :4].T, None)
    grad_x, g_norm0, *slots2, small2_all = _in_proj0_bwd(
        dz0, xs, dx1, row(norm_0), w_in0, _ScatterRider(parts2, small2))

    grads = [g_w_in0, g_w_out0, g_w_in1, g_w_out1, g_wg]
    theirs = [theirs2[0], theirs1[0], theirs1[1], theirs1[2], theirs2[1]]
    slots = [slots2[0], slots1[0], slots1[1], slots1[2], slots2[1]]
    n = len(grads)
    reduced = [_sum_chips(where, grads[w], theirs[w], slots[w], f"sum_chips{w}") for w in range(n)]
    small3 = _pack_small([g_norm0, zero, zero, zero, zero], None, None, None, loss_lanes)[:8]
    *from_sibling, small3_all = _share_halves(reduced, small3)

    out_g, out_d, out_m, out_v = {}, {}, {}, {}
    names = ["w_in_0", "w_out_0", "w_in_1", "w_out_1", "b_group_w_0"]
    for w in range(n):
        shape = big_w[w].shape
        two_d = (-1, shape[-1])
        outs = _adamw_halves(where, big_w[w].reshape(two_d), reduced[w], from_sibling[w], big_m[w].reshape(two_d),
                             big_v[w].reshape(two_d), f"adamw{w}")
        out_g[names[w]], out_d[names[w]], out_m[names[w]], out_v[names[w]] = (a.reshape(shape) for a in outs)

    g_small = _sum_small(small1_all, small2_all, small3_all)
    small_names = ["norm_0", "a_v_norm_0", "b_scale_0", "norm_1", "final_norm"]
    pack = lambda vecs, ws_, bs_, sk: _pack_small(vecs, ws_, bs_, sk)
    w_small = pack([norm_0, a_v_norm_0, b_scale_0, norm_1, final_norm], a_spatial_w_0, a_spatial_b_0, sink_1)
    m_small = pack([m_norm_0, m_a_v_norm_0, m_b_scale_0, m_norm_1, m_final_norm], m_a_spatial_w_0,
                   m_a_spatial_b_0, m_sink_1)
    v_small = pack([v_norm_0, v_a_v_norm_0, v_b_scale_0, v_norm_1, v_final_norm], v_a_spatial_w_0,
                   v_a_spatial_b_0, v_sink_1)
    d_small, nm_small, nv_small = _adamw(w_small, g_small, m_small, v_small, "adamw_small")
    for store, packed in ((out_g, g_small), (out_d, d_small), (out_m, nm_small), (out_v, nv_small)):
        vecs, ws_, bs_, sk = _unpack_small(packed)
        for name, vec in zip(small_names, vecs):
            store[name] = vec
        store["a_spatial_w_0"], store["a_spatial_b_0"], store["sink_1"] = ws_, bs_, sk

    loss = jnp.sum(g_small[7])
    order = ["norm_0", "w_in_0", "a_v_norm_0", "a_spatial_w_0", "a_spatial_b_0", "b_group_w_0", "b_scale_0",
             "w_out_0", "norm_1", "w_in_1", "sink_1", "w_out_1", "final_norm"]
    return (loss, grad_x.reshape(1, s, D_MODEL), *[out_g[k] for k in order], *[out_d[k] for k in order],
            *[out_m[k] for k in order], *[out_v[k] for k in order])
```

```python
import functools

import numpy as np
import jax
import jax.numpy as jnp
from jax import lax
from jax.experimental import pallas as pl
from jax.experimental.pallas import tpu as pltpu

F32 = jnp.float32
BF16 = jnp.bfloat16
MESH = pl.DeviceIdType.MESH

D_MODEL = 1024
EPS = 1e-6
NEG_INF = -1e30
CHUNK = 128
POOL_WINDOWS = (2, 4, 8, 16)
HALO = 16
N_HEADS = 16
HEAD_DIM = 64
ATTN_WINDOW = 128
ROPE_THETA = 500000.0
ROT_DIM = 16
ADAM_LR = 0.001
ADAM_B1 = 0.9
ADAM_B2 = 0.999
ADAM_EPS = 1e-08
ADAM_WD = 0.01
ADAM_STEP = 10

TOK = 256
VMEM_LIMIT = 56 * 1024 * 1024


def _params(**kw):
    return pltpu.CompilerParams(vmem_limit_bytes=VMEM_LIMIT, **kw)


def _whole(shape):
    nd = len(shape)
    return pl.BlockSpec(shape, lambda *_: (0,) * nd)


def _rows(t, n):
    return pl.BlockSpec((t, n), lambda i: (i, 0))


ANY = pl.BlockSpec(memory_space=pl.ANY)

_G0 = 0.7978845608028654
_G1 = 0.044715


def _gelu(x):
    return 0.5 * x * (1.0 + jnp.tanh(_G0 * (x + _G1 * x * x * x)))


def _gelu_and_grad(x):
    x2 = x * x
    t = jnp.tanh(_G0 * (x + _G1 * x2 * x))
    half = 0.5 * (1.0 + t)
    return x * half, half + 0.5 * x * (1.0 - t * t) * (_G0 * (1.0 + 3.0 * _G1 * x2))


def _sigmoid(x):
    return 1.0 / (1.0 + jnp.exp(-x))


def _dot(a, b):
    return jnp.dot(a, b, preferred_element_type=F32)


def _dot_nt(a, b):
    return lax.dot_general(a, b, (((1,), (1,)), ((), ())), preferred_element_type=F32)


def _dot_tn(a, b):
    return lax.dot_general(a, b, (((0,), (0,)), ((), ())), preferred_element_type=F32)


def _rms_fwd(x, g):
    r = lax.rsqrt(jnp.mean(x * x, axis=-1, keepdims=True) + EPS)
    xh = x * r
    return r, xh, xh * g


def _rms_bwd(dy, g, r, xh):
    dxh = dy * g
    return r * (dxh - xh * jnp.mean(dxh * xh, axis=-1, keepdims=True))


def _band_matrices(t):
    r = np.arange(t)[:, None]
    j = np.arange(t + 2 * HALO)[None, :]
    fwd, bwd = [], []
    for w in POOL_WINDOWS:
        d = j - r - HALO
        fwd.append((d >= -(w // 2)) & (d < w // 2))
        bwd.append((d >= -(w // 2) + 1) & (d <= w // 2))
    return (jnp.asarray(np.stack(fwd), BF16), jnp.asarray(np.stack(bwd), BF16))


def _window_counts(i, t, s):
    tok = i * t + lax.broadcasted_iota(jnp.int32, (t, 1), 0)
    out = []
    for w in POOL_WINDOWS:
        cnt = jnp.minimum(tok + w // 2, s) - jnp.maximum(tok - w // 2, 0)
        out.append(cnt.astype(F32))
    return out


def _rope_tables(s):
    inv = np.float32(ROPE_THETA) ** (-np.arange(0, ROT_DIM, 2, dtype=np.float32) / np.float32(ROT_DIM))
    ang = np.arange(s, dtype=np.float32)[:, None] * inv.astype(np.float32)[None, :]
    cos, sin = np.cos(ang).astype(np.float32), np.sin(ang).astype(np.float32)
    z8 = np.zeros((s, 8), np.float32)
    z48 = np.zeros((s, HEAD_DIM - ROT_DIM), np.float32)
    c = np.concatenate([cos, cos, np.ones((s, HEAD_DIM - ROT_DIM), np.float32)], axis=1)
    s_lo = np.concatenate([z8, sin, z48], axis=1)
    s_hi = np.concatenate([-sin, z8, z48], axis=1)
    return tuple(jnp.asarray(np.concatenate([a, a], axis=1)) for a in (c, s_lo, s_hi))


def _rope(x, c, s_lo, s_hi):
    n = x.shape[1]
    reps = n // 128
    c, s_lo, s_hi = (jnp.tile(a, (1, reps)) for a in (c, s_lo, s_hi))
    return x * c + pltpu.roll(x, 8, 1) * s_lo + pltpu.roll(x, n - 8, 1) * s_hi


def _rope_t(dx, c, s_lo, s_hi):
    n = dx.shape[1]
    reps = n // 128
    c, s_lo, s_hi = (jnp.tile(a, (1, reps)) for a in (c, s_lo, s_hi))
    return dx * c + pltpu.roll(dx * s_lo, n - 8, 1) + pltpu.roll(dx * s_hi, 8, 1)


def _in_proj0_own(order, x, g0, w_own, rider):
    s = x.shape[0]
    n = w_own.shape[1]

    def body(order_ref, x_ref, g_ref, w_ref, h_ref, z_ref):
        _, _, h = _rms_fwd(x_ref[...], g_ref[...])
        h = h.astype(BF16)
        h_ref[...] = h
        z_ref[...] = _dot(h, w_ref[...])

    return pl.pallas_call(
        rider.carried_by(body, 4, 2, s // TOK), name="in_proj0_own",
        grid_spec=pltpu.PrefetchScalarGridSpec(
            num_scalar_prefetch=1, grid=(s // TOK,),
            in_specs=[pl.BlockSpec((TOK, D_MODEL), lambda i, o: (i, 0)), pl.BlockSpec((1, D_MODEL), lambda i, o: (0, 0)),
                      pl.BlockSpec(w_own.shape, lambda i, o: (0, 0))] + rider.in_specs,
            out_specs=[pl.BlockSpec((TOK, D_MODEL), lambda i, o: (i, 0)),
                       pl.BlockSpec((TOK, n), lambda i, o: (i, o[0]))] + rider.out_specs,
            scratch_shapes=rider.scratch),
        out_shape=[jax.ShapeDtypeStruct((s, D_MODEL), BF16), jax.ShapeDtypeStruct((s, 4 * n), F32)] + rider.out_shape,
        input_output_aliases={4 + j: 2 + j for j in range(len(rider.fulls))},
        compiler_params=_params(),
    )(order, x, g0, w_own, *rider.fulls)


def _in_proj0_rest(order, h0, w_in0, z0, rider):
    s = h0.shape[0]
    tok = min(s, 4 * TOK)
    n_tiles = s // tok
    n = w_in0.shape[1] // 4

    def body(order_ref, h_ref, w_ref, z_in_ref, z_ref):
        z_ref[...] = _dot(h_ref[...], w_ref[...])

    return pl.pallas_call(
        rider.carried_by(body, 4, 1, 3 * n_tiles, lambda: pl.program_id(0) * n_tiles + pl.program_id(1)),
        name="in_proj0_rest",
        grid_spec=pltpu.PrefetchScalarGridSpec(
            num_scalar_prefetch=1, grid=(3, n_tiles),
            in_specs=[pl.BlockSpec((tok, D_MODEL), lambda k, i, o: (i, 0)),
                      pl.BlockSpec((w_in0.shape[0], n), lambda k, i, o: (0, o[1 + k])), ANY] + rider.in_specs,
            out_specs=[pl.BlockSpec((tok, n), lambda k, i, o: (i, o[1 + k]))] + rider.out_specs,
            scratch_shapes=rider.scratch),
        out_shape=[jax.ShapeDtypeStruct(z0.shape, F32)] + rider.out_shape,
        input_output_aliases={3: 0, **{4 + j: 1 + j for j in range(len(rider.fulls))}},
        compiler_params=_params(),
    )(order, h0, w_in0, z0, *rider.fulls)


def _halo_specs(s, col_block, tok=TOK):
    per = tok // HALO
    last = s // HALO - 1
    prev = pl.BlockSpec((HALO, 1024), lambda i: (jnp.maximum(i * per - 1, 0), col_block))
    nxt = pl.BlockSpec((HALO, 1024), lambda i: (jnp.minimum((i + 1) * per, last), col_block))
    return prev, nxt


def _with_halo(i, n_tiles, prev_ref, cur, next_ref):
    prev = prev_ref[...]
    nxt = next_ref[...]
    prev = jnp.where(i > 0, prev, jnp.zeros_like(prev))
    nxt = jnp.where(i < n_tiles - 1, nxt, jnp.zeros_like(nxt))
    return jnp.concatenate([prev, cur, nxt], axis=0)


def _mixer_a(v1, gv, ws_ref, bsx):
    rv, vh, v2 = _rms_fwd(v1, gv)
    v2 = v2.astype(BF16)
    rows = []
    for c in range(v1.shape[0] // CHUNK):
        cols = [_dot(ws_ref[h], v2[c * CHUNK:(c + 1) * CHUNK, h * 256:(h + 1) * 256]) for h in range(4)]
        rows.append(jnp.concatenate(cols, axis=1) + bsx)
    return rv, vh, v2, jnp.concatenate(rows, axis=0)


def _mixer_b_pooled(bx, halo, band_ref, counts):
    out = []
    for g in range(4):
        win = _dot(band_ref[g], halo[:, g * 256:(g + 1) * 256])
        out.append(win / counts[g] - bx[:, g * 256:(g + 1) * 256])
    return out


def _mix0_fwd(x, z0, w_out0, gv, ws, bsx, wg, scale, band, rider):
    s = x.shape[0]
    n_tiles = s // TOK
    k = len(rider.fulls)

    def body(z_ref, zp_ref, zn_ref, x_ref, wout_ref, gv_ref, ws_ref, bsx_ref, wg_ref, sc_ref, band_ref,
             cat_ref, x1_ref):
        i = pl.program_id(0)
        au = z_ref[:, 0:1024]
        av = z_ref[:, 1024:2048]
        ag = z_ref[:, 2048:3072]
        _, _, _, mixed = _mixer_a(_gelu(av), gv_ref[...], ws_ref, bsx_ref[...])
        cat_ref[:, 0:1024] = (_gelu(au) * mixed * (ag * _sigmoid(ag))).astype(BF16)

        bx = z_ref[:, 3072:4096]
        bg = z_ref[:, 4096:5120]
        halo = _with_halo(i, n_tiles, zp_ref, bx, zn_ref).astype(BF16)
        ps = _mixer_b_pooled(bx, halo, band_ref, _window_counts(i, TOK, s))
        pw = jnp.concatenate([_dot(ps[g].astype(BF16), wg_ref[g]) for g in range(4)], axis=1)
        cat_ref[:, 1024:2048] = (pw * sc_ref[...] * (bg * _sigmoid(bg))).astype(BF16)

        x1_ref[...] = x_ref[...] + _dot(cat_ref[...], wout_ref[...])

    prev, nxt = _halo_specs(s, 3)
    return pl.pallas_call(
        rider.carried_by(body, 11, 2, n_tiles), name="mix0_fwd", grid=(n_tiles,),
        in_specs=[_rows(TOK, 5120), prev, nxt, _rows(TOK, D_MODEL), _whole(w_out0.shape), _whole((1, 1024)),
                  _whole(ws.shape), _whole(bsx.shape), _whole(wg.shape), _whole((1, 1024)), _whole(band.shape)]
        + rider.in_specs,
        out_specs=[_rows(TOK, 2048), _rows(TOK, D_MODEL)] + rider.out_specs,
        out_shape=[jax.ShapeDtypeStruct((s, 2048), BF16), jax.ShapeDtypeStruct((s, D_MODEL), F32)] + rider.out_shape,
        input_output_aliases={11 + j: 2 + j for j in range(k)},
        scratch_shapes=rider.scratch,
        compiler_params=_params(),
    )(z0, z0, z0, x, w_out0, gv, ws, bsx, wg, scale, band, *rider.fulls)


def _in_proj1(x1, g1, w_in1, rope):
    s = x1.shape[0]

    def body(x_ref, g_ref, w_ref, c_ref, lo_ref, hi_ref, h_ref, q_ref, k_ref, v_ref, gate_ref):
        _, _, h = _rms_fwd(x_ref[...], g_ref[...])
        h = h.astype(BF16)
        h_ref[...] = h
        tabs = (c_ref[...], lo_ref[...], hi_ref[...])
        q_ref[...] = (_rope(_dot(h, w_ref[:, 0:1024]), *tabs) * Q_SCALE).astype(BF16)
        kv = _dot(h, w_ref[:, 1024:1536])
        k_ref[...] = _rope(kv[:, 0:256], *tabs).astype(BF16)
        v_ref[...] = kv[:, 256:512].astype(BF16)
        gate_ref[...] = _dot(h, w_ref[:, 1536:2560]).astype(BF16)

    tab = _rows(TOK, 128)
    return pl.pallas_call(
        body, name="in_proj1", grid=(s // TOK,),
        in_specs=[_rows(TOK, D_MODEL), _whole((1, D_MODEL)), _whole(w_in1.shape), tab, tab, tab],
        out_specs=[_rows(TOK, 1024), _rows(TOK, 1024), _rows(TOK, 256), _rows(TOK, 256), _rows(TOK, 1024)],
        out_shape=[jax.ShapeDtypeStruct((s, 1024), BF16), jax.ShapeDtypeStruct((s, 1024), BF16),
                   jax.ShapeDtypeStruct((s, 256), BF16), jax.ShapeDtypeStruct((s, 256), BF16),
                   jax.ShapeDtypeStruct((s, 1024), BF16)],
        compiler_params=_params(),
    )(x1, g1, w_in1, *rope)


QBLK = 128
KBLK = QBLK + 2 * ATTN_WINDOW
Q_SCALE = HEAD_DIM ** -0.5
HEAD_BATCH = 4


def _block_bias(q0, s):
    r = lax.broadcasted_iota(jnp.int32, (QBLK, KBLK), 0)
    c = lax.broadcasted_iota(jnp.int32, (QBLK, KBLK), 1)
    kj = q0 - ATTN_WINDOW + c
    ok = (c >= r) & (c <= r + 2 * ATTN_WINDOW) & (kj >= 0) & (kj < s)
    return jnp.where(ok, 0.0, NEG_INF)


def _attn_fwd(q, kpad, vpad, sink):
    s = q.shape[0]

    def body(sink_ref, q_ref, k_ref, v_ref, o_ref, lse_ref):
        i = pl.program_id(0)
        lane = lax.broadcasted_iota(jnp.int32, (1, 128), 1)
        for b in range(TOK // QBLK):
            rows = slice(b * QBLK, (b + 1) * QBLK)
            start = pl.multiple_of(i * TOK + b * QBLK, QBLK)
            kb = k_ref[pl.ds(start, KBLK), :]
            vb = v_ref[pl.ds(start, KBLK), :]
            bias = _block_bias(i * TOK + b * QBLK, s)
            kv = lambda t, h: t[:, (h // 4) * HEAD_DIM:(h // 4 + 1) * HEAD_DIM]
            scs = [_dot_nt(q_ref[rows, h * HEAD_DIM:(h + 1) * HEAD_DIM], kv(kb, h)) + bias for h in range(N_HEADS)]
            ms = [jnp.maximum(jnp.max(scs[h], axis=-1, keepdims=True), sink_ref[h]) for h in range(N_HEADS)]
            es = [jnp.exp(scs[h] - ms[h]) for h in range(N_HEADS)]
            dens = [jnp.sum(es[h], axis=-1, keepdims=True) + jnp.exp(sink_ref[h] - ms[h]) for h in range(N_HEADS)]
            outs = [_dot(es[h].astype(BF16), kv(vb, h)) * (1.0 / dens[h]) for h in range(N_HEADS)]
            o_ref[rows, :] = jnp.concatenate(outs, axis=1).astype(BF16)
            lse = jnp.zeros((QBLK, 128), F32)
            for h in range(N_HEADS):
                lse = lse + jnp.where(lane == h, ms[h] + jnp.log(dens[h]), 0.0)
            lse_ref[rows, :] = lse

    return pl.pallas_call(
        body, name="attn_fwd", grid=(s // TOK,),
        in_specs=[pl.BlockSpec(memory_space=pltpu.SMEM), _rows(TOK, 1024), _whole(kpad.shape), _whole(vpad.shape)],
        out_specs=[_rows(TOK, 1024), _rows(TOK, 128)],
        out_shape=[jax.ShapeDtypeStruct((s, 1024), BF16), jax.ShapeDtypeStruct((s, 128), F32)],
        compiler_params=_params(),
    )(sink, q, kpad, vpad)


def _tail(x1, o, gate, target, w_out1, gf):
    s = x1.shape[0]

    def body(x1_ref, o_ref, gate_ref, t_ref, w_ref, gf_ref, y1_ref, dx2_ref, do_ref, dgate_ref, loss_ref, gfn_ref,
             dx2h_ref):
        i = pl.program_id(0)

        @pl.when(i == 0)
        def _():
            loss_ref[...] = jnp.zeros_like(loss_ref)
            gfn_ref[...] = jnp.zeros_like(gfn_ref)

        g = gate_ref[...].astype(F32)
        sg = _sigmoid(g)
        sil = g * sg
        o = o_ref[...].astype(F32)
        y1 = (o * sil).astype(BF16)
        y1_ref[...] = y1
        x2 = x1_ref[...] + _dot(y1, w_ref[...])
        gf = gf_ref[...]
        r, xh, out = _rms_fwd(x2, gf)
        diff = out - t_ref[...]
        loss_ref[...] += jnp.sum(diff * diff, axis=0, keepdims=True) * (0.5 / D_MODEL)
        dout = diff * (1.0 / D_MODEL)
        gfn_ref[...] += jnp.sum(dout * xh, axis=0, keepdims=True)
        dx2 = _rms_bwd(dout, gf, r, xh)
        dx2_ref[...] = dx2
        dx2h = dx2.astype(BF16)
        dx2h_ref[...] = dx2h
        dy1 = _dot_nt(dx2h, w_ref[...])
        do_ref[...] = (dy1 * sil).astype(BF16)
        dgate_ref[...] = (dy1 * o * (sg * (1.0 + g * (1.0 - sg)))).astype(BF16)

    row = _rows(TOK, 1024)
    acc = _whole((1, 1024))
    return pl.pallas_call(
        body, name="tail", grid=(s // TOK,),
        in_specs=[row, row, row, row, _whole(w_out1.shape), acc],
        out_specs=[row, row, row, row, acc, acc, row],
        out_shape=[jax.ShapeDtypeStruct((s, 1024), BF16), jax.ShapeDtypeStruct((s, 1024), F32),
                   jax.ShapeDtypeStruct((s, 1024), BF16), jax.ShapeDtypeStruct((s, 1024), BF16),
                   jax.ShapeDtypeStruct((1, 1024), F32), jax.ShapeDtypeStruct((1, 1024), F32),
                   jax.ShapeDtypeStruct((s, 1024), BF16)],
        compiler_params=_params(),
    )(x1, o, gate, target, w_out1, gf)


def _attn_bwd(q, kpad, vpad, sink, o, lse, do, rope):
    s = q.shape[0]
    pad_t = (kpad.shape[1], kpad.shape[0])

    def body(sink_ref, q_ref, k_ref, v_ref, o_ref, lse_ref, do_ref, c_ref, lo_ref, hi_ref,
             dq_ref, dk_ref, dv_ref, ds_ref):
        i = pl.program_id(0)

        @pl.when(i == 0)
        def _():
            dk_ref[...] = jnp.zeros_like(dk_ref)
            dv_ref[...] = jnp.zeros_like(dv_ref)
            ds_ref[...] = jnp.zeros_like(ds_ref)

        lane = lax.broadcasted_iota(jnp.int32, (1, 128), 1)
        dsink = jnp.zeros((1, 128), F32)
        for b in range(TOK // QBLK):
            rows = slice(b * QBLK, (b + 1) * QBLK)
            start = pl.multiple_of(i * TOK + b * QBLK, QBLK)
            kb = k_ref[pl.ds(start, KBLK), :]
            vb = v_ref[pl.ds(start, KBLK), :]
            bias = _block_bias(i * TOK + b * QBLK, s)
            kv = lambda t, h: t[:, (h // 4) * HEAD_DIM:(h // 4 + 1) * HEAD_DIM]
            hd = lambda ref, h: ref[rows, h * HEAD_DIM:(h + 1) * HEAD_DIM]
            dqs, dks, dvs = {}, {}, {}
            for h0 in range(0, N_HEADS, HEAD_BATCH):
                heads = range(h0, h0 + HEAD_BATCH)
                lses = {h: lse_ref[rows, h:h + 1] for h in heads}
                ps = {h: jnp.exp(_dot_nt(hd(q_ref, h), kv(kb, h)) + bias - lses[h]) for h in heads}
                dos = {h: hd(do_ref, h) for h in heads}
                deltas = {h: jnp.sum(dos[h].astype(F32) * hd(o_ref, h).astype(F32), axis=-1, keepdims=True)
                          for h in heads}
                for h in heads:
                    dsink = dsink + jnp.where(
                        lane == h, -jnp.sum(jnp.exp(sink_ref[h] - lses[h]) * deltas[h], axis=0, keepdims=True), 0.0)
                dscs = {h: (ps[h] * (_dot_nt(dos[h], kv(vb, h)) - deltas[h])).astype(BF16) for h in heads}
                dvs.update({h: _dot_tn(dos[h], ps[h].astype(BF16)) for h in heads})
                dks.update({h: _dot_tn(hd(q_ref, h), dscs[h]) for h in heads})
                dqs.update({h: _dot(dscs[h], kv(kb, h)) * Q_SCALE for h in heads})
            dq = jnp.concatenate([dqs[h] for h in range(N_HEADS)], axis=1)
            dq_ref[rows, :] = _rope_t(dq, c_ref[rows, :], lo_ref[rows, :], hi_ref[rows, :]).astype(BF16)
            group_sum = lambda parts, g: (parts[4 * g] + parts[4 * g + 1]) + (parts[4 * g + 2] + parts[4 * g + 3])
            dk_ref[:, pl.ds(start, KBLK)] += jnp.concatenate([group_sum(dks, g) for g in range(4)], axis=0)
            dv_ref[:, pl.ds(start, KBLK)] += jnp.concatenate([group_sum(dvs, g) for g in range(4)], axis=0)
        ds_ref[...] += dsink

    row = _rows(TOK, 1024)
    tab = _rows(TOK, 128)
    pad = _whole(kpad.shape)
    return pl.pallas_call(
        body, name="attn_bwd", grid=(s // TOK,),
        in_specs=[pl.BlockSpec(memory_space=pltpu.SMEM), row, pad, pad, row, tab, row, tab, tab, tab],
        out_specs=[row, _whole(pad_t), _whole(pad_t), _whole((1, 128))],
        out_shape=[jax.ShapeDtypeStruct((s, 1024), BF16), jax.ShapeDtypeStruct(pad_t, F32),
                   jax.ShapeDtypeStruct(pad_t, F32), jax.ShapeDtypeStruct((1, 128), F32)],
        compiler_params=_params(),
    )(sink, q, kpad, vpad, o, lse, do, *rope)


def _in_proj1_bwd(dq, dk_t, dv_t, dgate, x1, dx2, g1, w_in1, rope):
    s = x1.shape[0]

    def body(dq_ref, dka_ref, dkb_ref, dva_ref, dvb_ref, dgate_ref, x1_ref, dx2_ref, g_ref, w_ref,
             c_ref, lo_ref, hi_ref, dz_ref, dx1_ref, gn_ref, dx1h_ref):
        i = pl.program_id(0)
        dk = jnp.concatenate([dka_ref[...], dkb_ref[...]], axis=1).T
        dv = jnp.concatenate([dva_ref[...], dvb_ref[...]], axis=1).T

        @pl.when(i == 0)
        def _():
            gn_ref[...] = jnp.zeros_like(gn_ref)

        dz_ref[:, 0:1024] = dq_ref[...]
        dz_ref[:, 1024:1280] = _rope_t(dk, c_ref[...], lo_ref[...], hi_ref[...]).astype(BF16)
        dz_ref[:, 1280:1536] = dv.astype(BF16)
        dz_ref[:, 1536:2560] = dgate_ref[...]
        dh = _dot_nt(dz_ref[...], w_ref[...])
        g = g_ref[...]
        r, xh, _ = _rms_fwd(x1_ref[...], g)
        gn_ref[...] += jnp.sum(dh * xh, axis=0, keepdims=True)
        dx1 = dx2_ref[...] + _rms_bwd(dh, g, r, xh)
        dx1_ref[...] = dx1
        dx1h_ref[...] = dx1.astype(BF16)

    row = _rows(TOK, 1024)
    per = TOK // ATTN_WINDOW
    half_a = pl.BlockSpec((256, TOK // 2), lambda i: (0, per * i + 1))
    half_b = pl.BlockSpec((256, TOK // 2), lambda i: (0, per * i + 2))
    tab = _rows(TOK, 128)
    acc = _whole((1, 1024))
    return pl.pallas_call(
        body, name="in_proj1_bwd", grid=(s // TOK,),
        in_specs=[row, half_a, half_b, half_a, half_b, row, row, row, acc, _whole(w_in1.shape), tab, tab, tab],
        out_specs=[_rows(TOK, 2560), row, acc, row],
        out_shape=[jax.ShapeDtypeStruct((s, 2560), BF16), jax.ShapeDtypeStruct((s, 1024), F32),
                   jax.ShapeDtypeStruct((1, 1024), F32), jax.ShapeDtypeStruct((s, 1024), BF16)],
        compiler_params=_params(),
    )(dq, dk_t, dk_t, dv_t, dv_t, dgate, x1, dx2, g1, w_in1, *rope)


def _mix0_bwd(dx1, z0, w_out0, gv, ws, bsx, wg, scale, band, rider):
    s = dx1.shape[0]
    n_tiles = s // TOK

    def body(dx1h_ref, z_ref, zp_ref, zn_ref, wout_ref, gv_ref, ws_ref, bsx_ref, wg_ref, sc_ref, band_ref,
             dz_ref, dpn_ref, dws_ref, dbs_ref, dgv_ref, dsc_ref, dwg_ref):
        i = pl.program_id(0)
        dz_ref[:, 3072:4096] = jnp.zeros((TOK, 1024), BF16)

        @pl.when(i == 0)
        def _():
            for ref in (dws_ref, dbs_ref, dgv_ref, dsc_ref, dwg_ref):
                ref[...] = jnp.zeros_like(ref)

        dcat = _dot_nt(dx1h_ref[...], wout_ref[...])
        dya = dcat[:, 0:1024]
        dyb = dcat[:, 1024:2048]

        au = z_ref[:, 0:1024]
        av = z_ref[:, 1024:2048]
        ag = z_ref[:, 2048:3072]
        gv = gv_ref[...]
        u, du = _gelu_and_grad(au)
        v1, dv1 = _gelu_and_grad(av)
        rv, vh, v2, mixed = _mixer_a(v1, gv, ws_ref, bsx_ref[...])
        sg = _sigmoid(ag)
        sil = ag * sg
        dz_ref[:, 2048:3072] = (dya * u * mixed * (sg * (1.0 + ag * (1.0 - sg)))).astype(BF16)
        dz_ref[:, 0:1024] = (dya * mixed * sil * du).astype(BF16)
        dmixed = dya * u * sil
        lane = lax.broadcasted_iota(jnp.int32, (1, 128), 1)
        dm16 = dmixed.astype(BF16)
        dv2_rows = []
        for c in range(TOK // CHUNK):
            rows = slice(c * CHUNK, (c + 1) * CHUNK)
            cols_out = []
            for h in range(4):
                cols = slice(h * 256, (h + 1) * 256)
                dws_ref[h] += _dot_nt(dm16[rows, cols], v2[rows, cols])
                dbs_ref[...] += jnp.where(lane == h, jnp.sum(dmixed[rows, cols], axis=-1, keepdims=True), 0.0)
                cols_out.append(_dot_tn(ws_ref[h], dm16[rows, cols]))
            dv2_rows.append(jnp.concatenate(cols_out, axis=1))
        dv2 = jnp.concatenate(dv2_rows, axis=0)
        dgv_ref[...] += jnp.sum(dv2 * vh, axis=0, keepdims=True)
        dz_ref[:, 1024:2048] = (_rms_bwd(dv2, gv, rv, vh) * dv1).astype(BF16)

        bx = z_ref[:, 3072:4096]
        bg = z_ref[:, 4096:5120]
        counts = _window_counts(i, TOK, s)
        halo = _with_halo(i, n_tiles, zp_ref, bx, zn_ref).astype(BF16)
        ps = [p.astype(BF16) for p in _mixer_b_pooled(bx, halo, band_ref, counts)]
        pw = jnp.concatenate([_dot(ps[g], wg_ref[g]) for g in range(4)], axis=1)
        sgb = _sigmoid(bg)
        sc = sc_ref[...]
        dz_ref[:, 4096:5120] = (dyb * pw * sc * (sgb * (1.0 + bg * (1.0 - sgb)))).astype(BF16)
        dys = dyb * (bg * sgb)
        dsc_ref[...] += jnp.sum(dys * pw, axis=0, keepdims=True)
        dpw = (dys * sc).astype(BF16)
        for g in range(4):
            cols = slice(g * 256, (g + 1) * 256)
            dwg_ref[g] += _dot_tn(ps[g], dpw[:, cols])
            dpn_ref[:, cols] = (_dot_nt(dpw[:, cols], wg_ref[g]) / counts[g]).astype(BF16)

    prev, nxt = _halo_specs(s, 3)
    row = _rows(TOK, 1024)
    vec = _whole((1, 1024))
    return pl.pallas_call(
        rider.carried_by(body, 11, 7, n_tiles), name="mix0_bwd", grid=(n_tiles,),
        in_specs=[row, _rows(TOK, 5120), prev, nxt, _whole(w_out0.shape), vec, _whole(ws.shape),
                  _whole(bsx.shape), _whole(wg.shape), vec, _whole(band.shape)] + rider.in_specs,
        out_specs=[_rows(TOK, 5120), row, _whole((4, 128, 128)), _whole((128, 128)), vec, vec,
                   _whole((4, 256, 256))] + rider.out_specs,
        out_shape=[jax.ShapeDtypeStruct((s, 5120), BF16), jax.ShapeDtypeStruct((s, 1024), BF16),
                   jax.ShapeDtypeStruct((4, 128, 128), F32), jax.ShapeDtypeStruct((128, 128), F32),
                   jax.ShapeDtypeStruct((1, 1024), F32), jax.ShapeDtypeStruct((1, 1024), F32),
                   jax.ShapeDtypeStruct((4, 256, 256), F32)] + rider.out_shape,
        scratch_shapes=rider.scratch,
        compiler_params=_params(),
    )(dx1, z0, z0, z0, w_out0, gv, ws, bsx, wg, scale, band, *rider.parts)


def _fill_pooled_grad(dz0, dpn):
    s = dpn.shape[0]
    tok = min(s, 2 * TOK)
    n_tiles = s // tok
    band_t = _band_matrices(tok)[1]

    def body(dz_in_ref, dpn_ref, dpp_ref, dpx_ref, band_ref, dbx_ref):
        i = pl.program_id(0)
        dpn = dpn_ref[...]
        halo = _with_halo(i, n_tiles, dpp_ref, dpn, dpx_ref)
        counts = _window_counts(i, tok, s)
        for g in range(4):
            cols = slice(g * 256, (g + 1) * 256)
            dbx = _dot(band_ref[g], halo[:, cols]) - dpn[:, cols].astype(F32) * counts[g]
            dbx_ref[:, cols] = dbx.astype(BF16)

    prev, nxt = _halo_specs(s, 0, tok)
    return pl.pallas_call(
        body, name="fill_pooled_grad", grid=(n_tiles,),
        in_specs=[ANY, _rows(tok, 1024), prev, nxt, _whole(band_t.shape)],
        out_specs=pl.BlockSpec((tok, 1024), lambda i: (i, 3)),
        out_shape=jax.ShapeDtypeStruct(dz0.shape, BF16),
        input_output_aliases={0: 0},
        compiler_params=_params(),
    )(dz0, dpn, dpn, dpn, band_t)


def _in_proj0_bwd(dz0, x, dx1, g0, w_in0, rider):
    s = x.shape[0]
    n_tiles = s // TOK

    def body(dz_ref, x_ref, dx1_ref, g_ref, w_ref, dx_ref, gn_ref):
        i = pl.program_id(0)

        @pl.when(i == 0)
        def _():
            gn_ref[...] = jnp.zeros_like(gn_ref)

        dh = _dot_nt(dz_ref[...], w_ref[...])
        g0v = g_ref[...]
        r, xh, _ = _rms_fwd(x_ref[...], g0v)
        gn_ref[...] += jnp.sum(dh * xh, axis=0, keepdims=True)
        dx_ref[...] = dx1_ref[...] + _rms_bwd(dh, g0v, r, xh)

    row = _rows(TOK, 1024)
    vec = _whole((1, 1024))
    return pl.pallas_call(
        rider.carried_by(body, 5, 2, n_tiles), name="in_proj0_bwd", grid=(n_tiles,),
        in_specs=[_rows(TOK, 5120), row, row, vec, _whole(w_in0.shape)] + rider.in_specs,
        out_specs=[row, vec] + rider.out_specs,
        out_shape=[jax.ShapeDtypeStruct((s, 1024), F32), jax.ShapeDtypeStruct((1, 1024), F32)] + rider.out_shape,
        scratch_shapes=rider.scratch,
        compiler_params=_params(),
    )(dz0, x, dx1, g0, w_in0, *rider.parts)


def _weight_grad(a, b, n_blocks, split, name):
    s, k = a.shape
    n = b.shape[1]
    tn = n // n_blocks
    w = tn // split
    ts = min(s, 1024)

    def body(a_ref, b_ref, o_ref):
        @pl.when(pl.program_id(1) == 0)
        def _():
            o_ref[...] = jnp.zeros_like(o_ref)

        res = _dot_tn(a_ref[...], b_ref[...])
        for q in range(split):
            o_ref[q] += res[:, q * w:(q + 1) * w]

    return pl.pallas_call(
        body, name=name, grid=(n_blocks, s // ts),
        in_specs=[pl.BlockSpec((ts, k), lambda j, t: (t, 0)), pl.BlockSpec((ts, tn), lambda j, t: (t, j))],
        out_specs=pl.BlockSpec((split, k, w), lambda j, t: (j, 0, 0)),
        out_shape=jax.ShapeDtypeStruct((n_blocks * split, k, w), F32),
        compiler_params=_params(),
    )(a, b)


def _row_tile(rows, cols):
    t = rows
    while t * cols * 4 > (1 << 20) and t % 16 == 0:
        t //= 2
    return t


def _add_sibling(where, g, theirs, name):
    _, _, rows, cols = g.shape
    t = _row_tile(rows, cols)

    def body(where_ref, g_ref, t_ref, o_ref):
        o_ref[...] = (g_ref[...] + t_ref[...]).astype(BF16)

    spec = pl.BlockSpec((None, t, cols), lambda s, i, p: (s, i, 0))
    return pl.pallas_call(
        body, name=name, out_shape=jax.ShapeDtypeStruct((4, rows, cols), BF16),
        grid_spec=pltpu.PrefetchScalarGridSpec(
            num_scalar_prefetch=1, grid=(4, rows // t),
            in_specs=[pl.BlockSpec((None, None, t, cols), lambda s, i, p: (s, p[1], i, 0)), spec], out_specs=spec),
        compiler_params=_params())(where, g, theirs)


def _sum_chips(where, g, theirs, slots, name):
    _, _, rows, cols = g.shape
    t = _row_tile(rows, cols)

    def body(where_ref, g_ref, t_ref, s_ref, o_ref):
        me = where_ref[0]
        own = g_ref[...] + t_ref[...]
        acc = jnp.where(me == 0, own, s_ref[0].astype(F32))
        for k in range(1, 4):
            acc = acc + jnp.where(me == k, own, s_ref[k].astype(F32))
        o_ref[...] = acc

    return pl.pallas_call(
        body, name=name, out_shape=jax.ShapeDtypeStruct((rows, cols), F32),
        grid_spec=pltpu.PrefetchScalarGridSpec(
            num_scalar_prefetch=1, grid=(rows // t,),
            in_specs=[pl.BlockSpec((None, None, t, cols), lambda i, p: (p[0], p[1], i, 0)),
                      pl.BlockSpec((None, t, cols), lambda i, p: (p[0], i, 0)),
                      pl.BlockSpec((4, t, cols), lambda i, p: (0, i, 0))],
            out_specs=pl.BlockSpec((t, cols), lambda i, p: (i, 0))),
        compiler_params=_params())(where, g, theirs, slots)


def _adamw_halves(where, w, own, theirs, m, v, name):
    rows, cols = own.shape
    t = _row_tile(rows, cols)
    per = rows // t

    def body(where_ref, w_ref, own_ref, th_ref, m_ref, v_ref, g_ref, d_ref, nm_ref, nv_ref):
        g = jnp.where(pl.program_id(0) == where_ref[1], own_ref[...], th_ref[...])
        g_ref[...] = g
        m2 = ADAM_B1 * m_ref[...] + (1.0 - ADAM_B1) * g
        v2 = ADAM_B2 * v_ref[...] + (1.0 - ADAM_B2) * (g * g)
        m_hat = m2 / (1.0 - ADAM_B1 ** ADAM_STEP)
        v_hat = v2 / (1.0 - ADAM_B2 ** ADAM_STEP)
        d_ref[...] = -ADAM_LR * (m_hat / (jnp.sqrt(v_hat) + ADAM_EPS) + ADAM_WD * w_ref[...])
        nm_ref[...] = m2
        nv_ref[...] = v2

    full = pl.BlockSpec((t, cols), lambda h, i, p: (h * per + i, 0))
    half = pl.BlockSpec((t, cols), lambda h, i, p: (i, 0))
    shp = jax.ShapeDtypeStruct(w.shape, F32)
    return pl.pallas_call(
        body, name=name, out_shape=[shp] * 4,
        grid_spec=pltpu.PrefetchScalarGridSpec(
            num_scalar_prefetch=1, grid=(2, per), in_specs=[full, half, half, full, full], out_specs=[full] * 4),
        compiler_params=_params())(where, w, own, theirs, m, v)


def _place_shard(where, w, cut, name):
    if cut.kind == "cols":
        r, n = cut.full_shape
        blk, grid = (256, n // 4), (r // 256,)
        src_map, dst_map = (lambda i, p: (i, 0)), (lambda i, p: (i, p[0]))
    elif cut.kind == "rows":
        r, n = cut.full_shape
        per = r // 4 // 256
        blk, grid = (256, n), (per,)
        src_map, dst_map = (lambda i, p: (i, 0)), (lambda i, p: (p[0] * per + i, 0))
    else:
        g, r, n = cut.full_shape
        blk, grid = (g, r // 4, n), (1,)
        src_map, dst_map = (lambda i, p: (0, 0, 0)), (lambda i, p: (0, p[0], 0))

    def body(where_ref, w_ref, o_ref):
        o_ref[...] = w_ref[...].astype(BF16)

    return pl.pallas_call(
        body, name=name, out_shape=jax.ShapeDtypeStruct(cut.full_shape, BF16),
        grid_spec=pltpu.PrefetchScalarGridSpec(
            num_scalar_prefetch=1, grid=grid, in_specs=[pl.BlockSpec(blk, src_map)],
            out_specs=pl.BlockSpec(blk, dst_map)),
        compiler_params=_params())(where, w)


def _sum_small(first, second, third):
    rows = second.shape[1]

    def body(a_ref, b_ref, c_ref, o_ref):
        top = a_ref[0] + c_ref[0] + b_ref[0, 0:8]
        rest = b_ref[0, 8:rows]
        for k in range(1, 8):
            top = top + (a_ref[k] + c_ref[k] + b_ref[k, 0:8])
            rest = rest + b_ref[k, 8:rows]
        o_ref[0:8] = top
        o_ref[8:rows] = rest

    return pl.pallas_call(
        body, name="sum_small", in_specs=[_whole(first.shape), _whole(second.shape), _whole(third.shape)],
        out_specs=_whole(second.shape[1:]), out_shape=jax.ShapeDtypeStruct(second.shape[1:], F32),
        compiler_params=_params())(first, second, third)


def _adamw(w, g, m, v, name):
    rows, cols = w.shape
    t = _row_tile(rows, cols)

    def body(w_ref, g_ref, m_ref, v_ref, d_ref, nm_ref, nv_ref):
        g = g_ref[...]
        m2 = ADAM_B1 * m_ref[...] + (1.0 - ADAM_B1) * g
        v2 = ADAM_B2 * v_ref[...] + (1.0 - ADAM_B2) * (g * g)
        m_hat = m2 / (1.0 - ADAM_B1 ** ADAM_STEP)
        v_hat = v2 / (1.0 - ADAM_B2 ** ADAM_STEP)
        d_ref[...] = -ADAM_LR * (m_hat / (jnp.sqrt(v_hat) + ADAM_EPS) + ADAM_WD * w_ref[...])
        nm_ref[...] = m2
        nv_ref[...] = v2

    spec = pl.BlockSpec((t, cols), lambda i: (i, 0))
    shp = jax.ShapeDtypeStruct(w.shape, F32)
    return pl.pallas_call(body, name=name, grid=(rows // t,), in_specs=[spec] * 4, out_specs=[spec] * 3,
                          out_shape=[shp] * 3, compiler_params=_params())(w, g, m, v)


def _place():
    x, y, c = lax.axis_index("x"), lax.axis_index("y"), lax.axis_index("c")
    chips = [(1 - x, y), (x, 1 - y), (1 - x, 1 - y)]
    return x, y, c, chips


class _Sharded:
    def __init__(self, kind, full_shape):
        self.kind = kind
        self.full_shape = full_shape

    def in_full(self, ref, s, h):
        if self.kind == "cols":
            r, n = self.full_shape
            return ref.at[pl.ds(h * (r // 2), r // 2), pl.ds(pl.multiple_of(s * (n // 4), 128), n // 4)]
        if self.kind == "rows":
            r, _ = self.full_shape
            return ref.at[pl.ds(pl.multiple_of(s * (r // 4) + h * (r // 8), 8), r // 8), :]
        g, r, _ = self.full_shape
        return ref.at[pl.ds(h * (g // 2), g // 2), pl.ds(pl.multiple_of(s * (r // 4), 16), r // 4), :]


def _remote(src, dst, send_sem, recv_sem, to):
    return pltpu.make_async_remote_copy(src_ref=src, dst_ref=dst, send_sem=send_sem, recv_sem=recv_sem,
                                        device_id=to, device_id_type=MESH)


def _start_remote(src, dst, send_sem, recv_sem, to):
    cp = _remote(src, dst, send_sem, recv_sem, to)
    cp.start()
    return cp


class _Gather:
    def __init__(self, fulls, cuts):
        n = len(fulls)
        self.fulls, self.cuts = list(fulls), list(cuts)
        self.in_specs = [ANY] * n
        self.out_specs = [ANY] * n
        self.out_shape = [jax.ShapeDtypeStruct(cut.full_shape, BF16) for cut in cuts]
        self.scratch = [pltpu.SemaphoreType.DMA((6 * n,)), pltpu.SemaphoreType.DMA((6 * n,))]

    def _step(self, step, src, out, send_sems, recv_sems):
        n, cuts = len(self.fulls), self.cuts
        x, y, c, chips = _place()
        me = 2 * x + y

        def ends(w, s, h, from_src):
            dst = cuts[w].in_full(out[w], s, h)
            return (cuts[w].in_full(src[w], s, h) if from_src else dst), dst

        for w in range(n):
            for j, chip in enumerate(chips):
                s = 2 * chip[0] + chip[1]
                k, k2 = 3 * w + j, 3 * n + 3 * w + j
                if step == "send":
                    _start_remote(*ends(w, me, c, True), send_sems.at[k], recv_sems.at[k], (*chip, c))
                elif step == "pass_on":
                    _remote(*ends(w, s, c, False), send_sems.at[k], recv_sems.at[k], (x, y, c)).wait_recv()
                    _start_remote(*ends(w, s, c, False), send_sems.at[k2], recv_sems.at[k2], (x, y, 1 - c))
                else:
                    _remote(*ends(w, s, 1 - c, False), send_sems.at[k2], recv_sems.at[k2], (x, y, c)).wait_recv()
                    _remote(*ends(w, me, c, True), send_sems.at[k], recv_sems.at[k], (x, y, c)).wait_send()
                    _remote(*ends(w, s, c, False), send_sems.at[k2], recv_sems.at[k2], (x, y, c)).wait_send()

    def carried_by(self, body, n_in, n_out, n_steps, step_index=lambda: pl.program_id(0)):
        k = len(self.fulls)

        def carrier(*refs):
            ins, src = refs[:n_in], refs[n_in:n_in + k]
            outs, out = refs[n_in + k:n_in + k + n_out], refs[n_in + k + n_out:n_in + 2 * k + n_out]
            sems = refs[n_in + 2 * k + n_out:]
            for step, at in (("send", 0), ("pass_on", 3 * n_steps // 4), ("finish", n_steps - 1)):
                if step == "finish":
                    body(*ins, *outs)

                @pl.when(step_index() == at)
                def _():
                    self._step(step, src, out, *sems)

        return carrier


def _exchange_halves(grads, name):
    n = len(grads)

    def body(*refs):
        g = refs[:n]
        theirs = refs[n:2 * n]
        send_sems, recv_sems = refs[2 * n:]
        x, y, c, _ = _place()
        sends = [_start_remote(g[w].at[:, 1 - c], theirs[w], send_sems.at[w], recv_sems.at[w], (x, y, 1 - c))
                 for w in range(n)]
        for w in range(n):
            _remote(g[w].at[:, 1 - c], theirs[w], send_sems.at[w], recv_sems.at[w], (x, y, c)).wait_recv()
        for cp in sends:
            cp.wait_send()

    return pl.pallas_call(
        body, name=name,
        in_specs=[ANY] * n, out_specs=[ANY] * n,
        out_shape=[jax.ShapeDtypeStruct((4,) + g.shape[2:], F32) for g in grads],
        scratch_shapes=[pltpu.SemaphoreType.DMA((n,)), pltpu.SemaphoreType.DMA((n,))],
        compiler_params=pltpu.CompilerParams(has_side_effects=True),
    )(*grads)


def _gather_small(small_ref, gathered, send_sems, recv_sems, first_sem, local_sem, start):
    x, y, c, _ = _place()
    me = 4 * x + 2 * y + c
    flips = [(fx, fy, fc) for fx in range(2) for fy in range(2) for fc in range(2)][1:]
    own = pltpu.make_async_copy(small_ref, gathered.at[me], local_sem)
    if start:
        own.start()
    else:
        own.wait()
    for k, (fx, fy, fc) in enumerate(flips):
        peer = (x + fx - 2 * x * fx, y + fy - 2 * y * fy, c + fc - 2 * c * fc)
        sems = (send_sems.at[first_sem + k], recv_sems.at[first_sem + k])
        if start:
            _start_remote(small_ref, gathered.at[me], *sems, peer)
        else:
            cp = _remote(small_ref, gathered.at[4 * peer[0] + 2 * peer[1] + peer[2]], *sems, (x, y, c))
            cp.wait_recv()
            cp.wait_send()


class _ScatterRider:
    def __init__(self, parts, small):
        n = len(parts)
        self.n = n
        self.parts = list(parts) + [small]
        self.in_specs = [ANY] * (n + 1)
        self.out_specs = [ANY] * (n + 1)
        self.out_shape = ([jax.ShapeDtypeStruct(a.shape, a.dtype) for a in parts]
                          + [jax.ShapeDtypeStruct((8,) + small.shape, small.dtype)])
        self.scratch = [pltpu.SemaphoreType.DMA((3 * n + 7,)), pltpu.SemaphoreType.DMA((3 * n + 7,)),
                        pltpu.SemaphoreType.DMA]

    def _copies(self, p, out, send_sems, recv_sems, local_sem, start):
        x, y, c, chips = _place()
        me = 2 * x + y
        n = self.n
        _gather_small(p[n], out[n], send_sems, recv_sems, 3 * n, local_sem, start)
        for w in range(n):
            for j, chip in enumerate(chips):
                s = 2 * chip[0] + chip[1]
                if start:
                    _start_remote(p[w].at[s], out[w].at[me], send_sems.at[3 * w + j], recv_sems.at[3 * w + j],
                                  (*chip, c))
                else:
                    cp = _remote(p[w].at[s], out[w].at[s], send_sems.at[3 * w + j], recv_sems.at[3 * w + j],
                                 (x, y, c))
                    cp.wait_recv()
                    cp.wait_send()

    def carried_by(self, body, n_in, n_out, n_tiles):
        k = len(self.parts)

        def carrier(*refs):
            ins, mine = refs[:n_in], refs[n_in:n_in + k]
            outs, slots = refs[n_in + k:n_in + k + n_out], refs[n_in + k + n_out:n_in + 2 * k + n_out]
            sems = refs[n_in + 2 * k + n_out:]

            @pl.when(pl.program_id(0) == 0)
            def _():
                self._copies(mine, slots, *sems, start=True)

            body(*ins, *outs)

            @pl.when(pl.program_id(0) == n_tiles - 1)
            def _():
                self._copies(mine, slots, *sems, start=False)

        return carrier


def _share_halves(halves, small):
    n = len(halves)

    def body(*refs):
        hv = refs[:n]
        small_ref = refs[n]
        out = refs[n + 1:2 * n + 1]
        gathered = refs[2 * n + 1]
        send_sems, recv_sems, local_sem = refs[2 * n + 2:]
        x, y, c, _ = _place()
        sends = [_start_remote(hv[w], out[w], send_sems.at[w], recv_sems.at[w], (x, y, 1 - c)) for w in range(n)]
        _gather_small(small_ref, gathered, send_sems, recv_sems, n, local_sem, True)
        for w in range(n):
            _remote(hv[w], out[w], send_sems.at[w], recv_sems.at[w], (x, y, c)).wait_recv()
        for cp in sends:
            cp.wait_send()
        _gather_small(small_ref, gathered, send_sems, recv_sems, n, local_sem, False)

    return pl.pallas_call(
        body, name="share_halves",
        in_specs=[ANY] * (n + 1), out_specs=[ANY] * (n + 1),
        out_shape=[jax.ShapeDtypeStruct(a.shape, F32) for a in halves] + [jax.ShapeDtypeStruct((8,) + small.shape, F32)],
        scratch_shapes=[pltpu.SemaphoreType.DMA((n + 7,)), pltpu.SemaphoreType.DMA((n + 7,)),
                        pltpu.SemaphoreType.DMA],
        compiler_params=pltpu.CompilerParams(has_side_effects=True),
    )(*halves, small)


SMALL_ROWS = 80


def _pack_small(vecs, ws, bs, sink, extra=None):
    ws = jnp.zeros((64, 1024), F32) if ws is None else ws.reshape(64, 1024)
    bs = jnp.zeros((1, 512), F32) if bs is None else bs.reshape(1, 512)
    sink = jnp.zeros((1, 16), F32) if sink is None else sink.reshape(1, 16)
    extra = jnp.zeros((1, 1024), F32) if extra is None else extra.reshape(1, 1024)
    top = jnp.concatenate(
        [v.reshape(1, 1024) for v in vecs]
        + [jnp.pad(bs, ((0, 0), (0, 512))), jnp.pad(sink, ((0, 0), (0, 1008))), extra], axis=0)
    return jnp.concatenate([top, ws, jnp.zeros((8, 1024), F32)], axis=0)


def _unpack_small(p):
    vecs = [p[k] for k in range(5)]
    return vecs, p[8:72].reshape(4, 128, 128), p[5, :512].reshape(4, 128), p[6, :16]


def kernel(x, norm_0, w_in_0, a_v_norm_0, a_spatial_w_0, a_spatial_b_0, b_group_w_0, b_scale_0, w_out_0, norm_1, w_in_1, sink_1, w_out_1, final_norm, loss_target, m_norm_0, m_w_in_0, m_a_v_norm_0, m_a_spatial_w_0, m_a_spatial_b_0, m_b_group_w_0, m_b_scale_0, m_w_out_0, m_norm_1, m_w_in_1, m_sink_1, m_w_out_1, m_final_norm, v_norm_0, v_w_in_0, v_a_v_norm_0, v_a_spatial_w_0, v_a_spatial_b_0, v_b_group_w_0, v_b_scale_0, v_w_out_0, v_norm_1, v_w_in_1, v_sink_1, v_w_out_1, v_final_norm):
    s = x.shape[1]
    xs = x.reshape(s, D_MODEL)
    target = loss_target.reshape(s, D_MODEL)

    cuts = [_Sharded("cols", (1024, 5120)), _Sharded("rows", (2048, 1024)), _Sharded("cols", (1024, 2560)),
            _Sharded("rows", (1024, 1024)), _Sharded("mid", (4, 256, 256))]
    big_w = [w_in_0, w_out_0, w_in_1, w_out_1, b_group_w_0]
    big_m = [m_w_in_0, m_w_out_0, m_w_in_1, m_w_out_1, m_b_group_w_0]
    big_v = [v_w_in_0, v_w_out_0, v_w_in_1, v_w_out_1, v_b_group_w_0]
    where = jnp.stack([2 * lax.axis_index("x") + lax.axis_index("y"), lax.axis_index("c")]).astype(jnp.int32)
    placed = [_place_shard(where, w, cut, f"place_shard{k}") for k, (w, cut) in enumerate(zip(big_w, cuts))]
    cx, cy = lax.axis_index("x"), lax.axis_index("y")
    order = jnp.stack([2 * cx + cy, 2 * (1 - cx) + cy, 2 * cx + 1 - cy, 2 * (1 - cx) + 1 - cy]).astype(jnp.int32)

    row = lambda v: v.reshape(1, 1024)
    ws16 = a_spatial_w_0.astype(BF16)
    bsx = jnp.repeat(a_spatial_b_0.T, 256, axis=1)
    band = _band_matrices(TOK)[0]
    rope = _rope_tables(s)

    h0, z0, w_in0 = _in_proj0_own(order, xs, row(norm_0), w_in_0.astype(BF16), _Gather(placed[:1], cuts[:1]))
    z0, w_out0, wg = _in_proj0_rest(order, h0, w_in0, z0, _Gather([placed[1], placed[4]], [cuts[1], cuts[4]]))
    cat, x1, w_in1, w_out1 = _mix0_fwd(xs, z0, w_out0, row(a_v_norm_0), ws16, bsx, wg, row(b_scale_0), band,
                                       _Gather(placed[2:4], cuts[2:4]))
    h1, q, k, v, gate = _in_proj1(x1, row(norm_1), w_in1, rope)
    kpad = jnp.pad(k, ((ATTN_WINDOW, ATTN_WINDOW), (0, 0)))
    vpad = jnp.pad(v, ((ATTN_WINDOW, ATTN_WINDOW), (0, 0)))
    o, lse = _attn_fwd(q, kpad, vpad, sink_1)
    y1, dx2, do, dgate, loss_lanes, g_final, dx2h = _tail(x1, o, gate, target, w_out1, row(final_norm))

    dq, dkpad, dvpad, dsink = _attn_bwd(q, kpad, vpad, sink_1, o, lse, do, rope)
    dz1, dx1, g_norm1, dx1h = _in_proj1_bwd(dq, dkpad, dvpad, dgate, x1, dx2, row(norm_1), w_in1, rope)

    g_w_in1 = _weight_grad(h1, dz1, 2, 2, "grad_w_in1").reshape(4, 2, 512, 640)
    g_w_out1 = _weight_grad(y1, dx2h, 1, 1, "grad_w_out1").reshape(4, 2, 128, 1024)
    g_w_out0 = _weight_grad(cat, dx1h, 1, 1, "grad_w_out0").reshape(4, 2, 256, 1024)
    first = [g_w_out0, g_w_in1, g_w_out1]
    theirs1 = _exchange_halves(first, "exchange_halves1")
    parts1 = [_add_sibling(where, g, t, f"add_sibling1_{k}") for k, (g, t) in enumerate(zip(first, theirs1))]
    zero = jnp.zeros((1024,), F32)
    small1 = _pack_small([zero, zero, zero, g_norm1, g_final], None, None, dsink[0, :16])[:8]
    dz0, dpn, d_ws, d_bs, d_gv, d_scale, d_wg, *slots1, small1_all = _mix0_bwd(
        dx1h, z0, w_out0, row(a_v_norm_0), ws16, bsx, wg, row(b_scale_0), band, _ScatterRider(parts1, small1))
    dz0 = _fill_pooled_grad(dz0, dpn)
    g_w_in0 = _weight_grad(h0, dz0, 4, 1, "grad_w_in0").reshape(4, 2, 512, 1280)
    g_wg = d_wg.reshape(2, 2, 4, 64, 256).transpose(2, 0, 1, 3, 4).reshape(4, 2, 128, 256)
    second = [g_w_in0, g_wg]
    theirs2 = _exchange_halves(second, "exchange_halves2")
    parts2 = [_add_sibling(where, g, t, f"add_sibling2_{k}") for k, (g, t) in enumerate(zip(second, theirs2))]
    small2 = _pack_small([zero, d_gv, d_scale, zero, zero], d_ws, d_bs[:, :4].T, None)
    grad_x, g_norm0, *slots2, small2_all = _in_proj0_bwd(
        dz0, xs, dx1, row(norm_0), w_in0, _ScatterRider(parts2, small2))

    grads = [g_w_in0, g_w_out0, g_w_in1, g_w_out1, g_wg]
    theirs = [theirs2[0], theirs1[0], theirs1[1], theirs1[2], theirs2[1]]
    slots = [slots2[0], slots1[0], slots1[1], slots1[2], slots2[1]]
    n = len(grads)
    reduced = [_sum_chips(where, grads[w], theirs[w], slots[w], f"sum_chips{w}") for w in range(n)]
    small3 = _pack_small([g_norm0, zero, zero, zero, zero], None, None, None, loss_lanes)[:8]
    *from_sibling, small3_all = _share_halves(reduced, small3)

    out_g, out_d, out_m, out_v = {}, {}, {}, {}
    names = ["w_in_0", "w_out_0", "w_in_1", "w_out_1", "b_group_w_0"]
    for w in range(n):
        shape = big_w[w].shape
        two_d = (-1, shape[-1])
        outs = _adamw_halves(where, big_w[w].reshape(two_d), reduced[w], from_sibling[w], big_m[w].reshape(two_d),
                             big_v[w].reshape(two_d), f"adamw{w}")
        out_g[names[w]], out_d[names[w]], out_m[names[w]], out_v[names[w]] = (a.reshape(shape) for a in outs)

    g_small = _sum_small(small1_all, small2_all, small3_all)
    small_names = ["norm_0", "a_v_norm_0", "b_scale_0", "norm_1", "final_norm"]
    pack = lambda vecs, ws_, bs_, sk: _pack_small(vecs, ws_, bs_, sk)
    w_small = pack([norm_0, a_v_norm_0, b_scale_0, norm_1, final_norm], a_spatial_w_0, a_spatial_b_0, sink_1)
    m_small = pack([m_norm_0, m_a_v_norm_0, m_b_scale_0, m_norm_1, m_final_norm], m_a_spatial_w_0,
                   m_a_spatial_b_0, m_sink_1)
    v_small = pack([v_norm_0, v_a_v_norm_0, v_b_scale_0, v_norm_1, v_final_norm], v_a_spatial_w_0,
                   v_a_spatial_b_0, v_sink_1)
    d_small, nm_small, nv_small = _adamw(w_small, g_small, m_small, v_small, "adamw_small")
    for store, packed in ((out_g, g_small), (out_d, d_small), (out_m, nm_small), (out_v, nv_small)):
        vecs, ws_, bs_, sk = _unpack_small(packed)
        for name, vec in zip(small_names, vecs):
            store[name] = vec
        store["a_spatial_w_0"], store["a_spatial_b_0"], store["sink_1"] = ws_, bs_, sk

    loss = jnp.sum(g_small[7])
    order = ["norm_0", "w_in_0", "a_v_norm_0", "a_spatial_w_0", "a_spatial_b_0", "b_group_w_0", "b_scale_0",
             "w_out_0", "norm_1", "w_in_1", "sink_1", "w_out_1", "final_norm"]
    return (loss, grad_x.reshape(1, s, D_MODEL), *[out_g[k] for k in order], *[out_d[k] for k in order],
            *[out_m[k] for k in order], *[out_v[k] for k in order])
```

```python
import functools

import numpy as np
import jax
import jax.numpy as jnp
from jax import lax
from jax.experimental import pallas as pl
from jax.experimental.pallas import tpu as pltpu

F32 = jnp.float32
BF16 = jnp.bfloat16
MESH = pl.DeviceIdType.MESH

D_MODEL = 1024
EPS = 1e-6
NEG_INF = -1e30
CHUNK = 128
POOL_WINDOWS = (2, 4, 8, 16)
HALO = 16
N_HEADS = 16
HEAD_DIM = 64
ATTN_WINDOW = 128
ROPE_THETA = 500000.0
ROT_DIM = 16
ADAM_LR = 0.001
ADAM_B1 = 0.9
ADAM_B2 = 0.999
ADAM_EPS = 1e-08
ADAM_WD = 0.01
ADAM_STEP = 10

TOK = 256
VMEM_LIMIT = 56 * 1024 * 1024


def _params(**kw):
    return pltpu.CompilerParams(vmem_limit_bytes=VMEM_LIMIT, **kw)


def _whole(shape):
    nd = len(shape)
    return pl.BlockSpec(shape, lambda *_: (0,) * nd)


def _rows(t, n):
    return pl.BlockSpec((t, n), lambda i: (i, 0))


ANY = pl.BlockSpec(memory_space=pl.ANY)

_G0 = 0.7978845608028654
_G1 = 0.044715


def _gelu(x):
    return 0.5 * x * (1.0 + jnp.tanh(_G0 * (x + _G1 * x * x * x)))


def _gelu_and_grad(x):
    x2 = x * x
    t = jnp.tanh(_G0 * (x + _G1 * x2 * x))
    half = 0.5 * (1.0 + t)
    return x * half, half + 0.5 * x * (1.0 - t * t) * (_G0 * (1.0 + 3.0 * _G1 * x2))


def _sigmoid(x):
    return 1.0 / (1.0 + jnp.exp(-x))


def _dot(a, b):
    return jnp.dot(a, b, preferred_element_type=F32)


def _dot_nt(a, b):
    return lax.dot_general(a, b, (((1,), (1,)), ((), ())), preferred_element_type=F32)


def _dot_tn(a, b):
    return lax.dot_general(a, b, (((0,), (0,)), ((), ())), preferred_element_type=F32)


def _rms_fwd(x, g):
    r = lax.rsqrt(jnp.mean(x * x, axis=-1, keepdims=True) + EPS)
    xh = x * r
    return r, xh, xh * g


def _rms_bwd(dy, g, r, xh):
    dxh = dy * g
    return r * (dxh - xh * jnp.mean(dxh * xh, axis=-1, keepdims=True))


def _band_matrices(t):
    r = np.arange(t)[:, None]
    j = np.arange(t + 2 * HALO)[None, :]
    fwd, bwd = [], []
    for w in POOL_WINDOWS:
        d = j - r - HALO
        fwd.append((d >= -(w // 2)) & (d < w // 2))
        bwd.append((d >= -(w // 2) + 1) & (d <= w // 2))
    return (jnp.asarray(np.stack(fwd), BF16), jnp.asarray(np.stack(bwd), BF16))


def _window_counts(i, t, s):
    tok = i * t + lax.broadcasted_iota(jnp.int32, (t, 1), 0)
    out = []
    for w in POOL_WINDOWS:
        cnt = jnp.minimum(tok + w // 2, s) - jnp.maximum(tok - w // 2, 0)
        out.append(cnt.astype(F32))
    return out


def _rope_tables(s):
    inv = np.float32(ROPE_THETA) ** (-np.arange(0, ROT_DIM, 2, dtype=np.float32) / np.float32(ROT_DIM))
    ang = np.arange(s, dtype=np.float32)[:, None] * inv.astype(np.float32)[None, :]
    cos, sin = np.cos(ang).astype(np.float32), np.sin(ang).astype(np.float32)
    z8 = np.zeros((s, 8), np.float32)
    z48 = np.zeros((s, HEAD_DIM - ROT_DIM), np.float32)
    c = np.concatenate([cos, cos, np.ones((s, HEAD_DIM - ROT_DIM), np.float32)], axis=1)
    s_lo = np.concatenate([z8, sin, z48], axis=1)
    s_hi = np.concatenate([-sin, z8, z48], axis=1)
    return tuple(jnp.asarray(np.concatenate([a, a], axis=1)) for a in (c, s_lo, s_hi))


def _rope(x, c, s_lo, s_hi):
    n = x.shape[1]
    reps = n // 128
    c, s_lo, s_hi = (jnp.tile(a, (1, reps)) for a in (c, s_lo, s_hi))
    return x * c + pltpu.roll(x, 8, 1) * s_lo + pltpu.roll(x, n - 8, 1) * s_hi


def _rope_t(dx, c, s_lo, s_hi):
    n = dx.shape[1]
    reps = n // 128
    c, s_lo, s_hi = (jnp.tile(a, (1, reps)) for a in (c, s_lo, s_hi))
    return dx * c + pltpu.roll(dx * s_lo, n - 8, 1) + pltpu.roll(dx * s_hi, 8, 1)


def _in_proj0_own(order, x, g0, w_own, rider):
    s = x.shape[0]
    n = w_own.shape[1]

    def body(order_ref, x_ref, g_ref, w_ref, h_ref, z_ref):
        _, _, h = _rms_fwd(x_ref[...], g_ref[...])
        h = h.astype(BF16)
        h_ref[...] = h
        z_ref[...] = _dot(h, w_ref[...])

    return pl.pallas_call(
        rider.carried_by(body, 4, 2, s // TOK), name="in_proj0_own",
        grid_spec=pltpu.PrefetchScalarGridSpec(
            num_scalar_prefetch=1, grid=(s // TOK,),
            in_specs=[pl.BlockSpec((TOK, D_MODEL), lambda i, o: (i, 0)), pl.BlockSpec((1, D_MODEL), lambda i, o: (0, 0)),
                      pl.BlockSpec(w_own.shape, lambda i, o: (0, 0))] + rider.in_specs,
            out_specs=[pl.BlockSpec((TOK, D_MODEL), lambda i, o: (i, 0)),
                       pl.BlockSpec((TOK, n), lambda i, o: (i, o[0]))] + rider.out_specs,
            scratch_shapes=rider.scratch),
        out_shape=[jax.ShapeDtypeStruct((s, D_MODEL), BF16), jax.ShapeDtypeStruct((s, 4 * n), F32)] + rider.out_shape,
        input_output_aliases={4 + j: 2 + j for j in range(len(rider.fulls))},
        compiler_params=_params(),
    )(order, x, g0, w_own, *rider.fulls)


def _in_proj0_rest(order, h0, w_in0, z0, rider):
    s = h0.shape[0]
    tok = min(s, 4 * TOK)
    n_tiles = s // tok
    n = w_in0.shape[1] // 4

    def body(order_ref, h_ref, w_ref, z_in_ref, z_ref):
        z_ref[...] = _dot(h_ref[...], w_ref[...])

    return pl.pallas_call(
        rider.carried_by(body, 4, 1, 3 * n_tiles, lambda: pl.program_id(0) * n_tiles + pl.program_id(1)),
        name="in_proj0_rest",
        grid_spec=pltpu.PrefetchScalarGridSpec(
            num_scalar_prefetch=1, grid=(3, n_tiles),
            in_specs=[pl.BlockSpec((tok, D_MODEL), lambda k, i, o: (i, 0)),
                      pl.BlockSpec((w_in0.shape[0], n), lambda k, i, o: (0, o[1 + k])), ANY] + rider.in_specs,
            out_specs=[pl.BlockSpec((tok, n), lambda k, i, o: (i, o[1 + k]))] + rider.out_specs,
            scratch_shapes=rider.scratch),
        out_shape=[jax.ShapeDtypeStruct(z0.shape, F32)] + rider.out_shape,
        input_output_aliases={3: 0, **{4 + j: 1 + j for j in range(len(rider.fulls))}},
        compiler_params=_params(),
    )(order, h0, w_in0, z0, *rider.fulls)


def _halo_specs(s, col_block, tok=TOK):
    per = tok // HALO
    last = s // HALO - 1
    prev = pl.BlockSpec((HALO, 1024), lambda i: (jnp.maximum(i * per - 1, 0), col_block))
    nxt = pl.BlockSpec((HALO, 1024), lambda i: (jnp.minimum((i + 1) * per, last), col_block))
    return prev, nxt


def _with_halo(i, n_tiles, prev_ref, cur, next_ref):
    prev = prev_ref[...]
    nxt = next_ref[...]
    prev = jnp.where(i > 0, prev, jnp.zeros_like(prev))
    nxt = jnp.where(i < n_tiles - 1, nxt, jnp.zeros_like(nxt))
    return jnp.concatenate([prev, cur, nxt], axis=0)


def _mixer_a(v1, gv, ws_ref, bsx):
    rv, vh, v2 = _rms_fwd(v1, gv)
    v2 = v2.astype(BF16)
    rows = []
    for c in range(v1.shape[0] // CHUNK):
        cols = [_dot(ws_ref[h], v2[c * CHUNK:(c + 1) * CHUNK, h * 256:(h + 1) * 256]) for h in range(4)]
        rows.append(jnp.concatenate(cols, axis=1) + bsx)
    return rv, vh, v2, jnp.concatenate(rows, axis=0)


def _mixer_b_pooled(bx, halo, band_ref, counts):
    out = []
    for g in range(4):
        win = _dot(band_ref[g], halo[:, g * 256:(g + 1) * 256])
        out.append(win / counts[g] - bx[:, g * 256:(g + 1) * 256])
    return out


def _mix0_fwd(x, z0, w_out0, gv, ws, bsx, wg, scale, band, rider):
    s = x.shape[0]
    n_tiles = s // TOK
    k = len(rider.fulls)

    def body(z_ref, zp_ref, zn_ref, x_ref, wout_ref, gv_ref, ws_ref, bsx_ref, wg_ref, sc_ref, band_ref,
             cat_ref, x1_ref):
        i = pl.program_id(0)
        au = z_ref[:, 0:1024]
        av = z_ref[:, 1024:2048]
        ag = z_ref[:, 2048:3072]
        _, _, _, mixed = _mixer_a(_gelu(av), gv_ref[...], ws_ref, bsx_ref[...])
        cat_ref[:, 0:1024] = (_gelu(au) * mixed * (ag * _sigmoid(ag))).astype(BF16)

        bx = z_ref[:, 3072:4096]
        bg = z_ref[:, 4096:5120]
        halo = _with_halo(i, n_tiles, zp_ref, bx, zn_ref).astype(BF16)
        ps = _mixer_b_pooled(bx, halo, band_ref, _window_counts(i, TOK, s))
        pw = jnp.concatenate([_dot(ps[g].astype(BF16), wg_ref[g]) for g in range(4)], axis=1)
        cat_ref[:, 1024:2048] = (pw * sc_ref[...] * (bg * _sigmoid(bg))).astype(BF16)

        x1_ref[...] = x_ref[...] + _dot(cat_ref[...], wout_ref[...])

    prev, nxt = _halo_specs(s, 3)
    return pl.pallas_call(
        rider.carried_by(body, 11, 2, n_tiles), name="mix0_fwd", grid=(n_tiles,),
        in_specs=[_rows(TOK, 5120), prev, nxt, _rows(TOK, D_MODEL), _whole(w_out0.shape), _whole((1, 1024)),
                  _whole(ws.shape), _whole(bsx.shape), _whole(wg.shape), _whole((1, 1024)), _whole(band.shape)]
        + rider.in_specs,
        out_specs=[_rows(TOK, 2048), _rows(TOK, D_MODEL)] + rider.out_specs,
        out_shape=[jax.ShapeDtypeStruct((s, 2048), BF16), jax.ShapeDtypeStruct((s, D_MODEL), F32)] + rider.out_shape,
        input_output_aliases={11 + j: 2 + j for j in range(k)},
        scratch_shapes=rider.scratch,
        compiler_params=_params(),
    )(z0, z0, z0, x, w_out0, gv, ws, bsx, wg, scale, band, *rider.fulls)


def _in_proj1(x1, g1, w_in1, rope):
    s = x1.shape[0]

    def body(x_ref, g_ref, w_ref, c_ref, lo_ref, hi_ref, h_ref, q_ref, k_ref, v_ref, gate_ref):
        _, _, h = _rms_fwd(x_ref[...], g_ref[...])
        h = h.astype(BF16)
        h_ref[...] = h
        tabs = (c_ref[...], lo_ref[...], hi_ref[...])
        q_ref[...] = (_rope(_dot(h, w_ref[:, 0:1024]), *tabs) * Q_SCALE).astype(BF16)
        kv = _dot(h, w_ref[:, 1024:1536])
        k_ref[...] = _rope(kv[:, 0:256], *tabs).astype(BF16)
        v_ref[...] = kv[:, 256:512].astype(BF16)
        gate_ref[...] = _dot(h, w_ref[:, 1536:2560]).astype(BF16)

    tab = _rows(TOK, 128)
    return pl.pallas_call(
        body, name="in_proj1", grid=(s // TOK,),
        in_specs=[_rows(TOK, D_MODEL), _whole((1, D_MODEL)), _whole(w_in1.shape), tab, tab, tab],
        out_specs=[_rows(TOK, 1024), _rows(TOK, 1024), _rows(TOK, 256), _rows(TOK, 256), _rows(TOK, 1024)],
        out_shape=[jax.ShapeDtypeStruct((s, 1024), BF16), jax.ShapeDtypeStruct((s, 1024), BF16),
                   jax.ShapeDtypeStruct((s, 256), BF16), jax.ShapeDtypeStruct((s, 256), BF16),
                   jax.ShapeDtypeStruct((s, 1024), BF16)],
        compiler_params=_params(),
    )(x1, g1, w_in1, *rope)


QBLK = 128
KBLK = QBLK + 2 * ATTN_WINDOW
Q_SCALE = HEAD_DIM ** -0.5
HEAD_BATCH = 4


def _block_bias(q0, s):
    r = lax.broadcasted_iota(jnp.int32, (QBLK, KBLK), 0)
    c = lax.broadcasted_iota(jnp.int32, (QBLK, KBLK), 1)
    kj = q0 - ATTN_WINDOW + c
    ok = (c >= r) & (c <= r + 2 * ATTN_WINDOW) & (kj >= 0) & (kj < s)
    return jnp.where(ok, 0.0, NEG_INF)


def _pair_operands(t):
    lane = lax.broadcasted_iota(jnp.int32, (1, 128), 1)
    first = lane < HEAD_DIM
    zero = jnp.zeros((KBLK, 128), BF16)
    out = []
    for j in range(2):
        slab = t[:, 128 * j:128 * (j + 1)]
        turned = pltpu.bitcast(pltpu.roll(pltpu.bitcast(slab, jnp.uint32), HEAD_DIM, 1), BF16)
        for own_first in (True, False):
            top = jnp.where(first, slab if own_first else turned, zero)
            bottom = jnp.where(first, zero, turned if own_first else slab)
            out.append(jnp.concatenate([top, bottom], axis=0))
    return out


def _attn_fwd(q, kpad, vpad, sink):
    s = q.shape[0]

    def body(sink_ref, q_ref, k_ref, v_ref, o_ref, lse_ref):
        i = pl.program_id(0)
        lane = lax.broadcasted_iota(jnp.int32, (1, 128), 1)
        for b in range(TOK // QBLK):
            rows = slice(b * QBLK, (b + 1) * QBLK)
            start = pl.multiple_of(i * TOK + b * QBLK, QBLK)
            k_bd = _pair_operands(k_ref[pl.ds(start, KBLK), :])
            v_bd = _pair_operands(v_ref[pl.ds(start, KBLK), :])
            bias = _block_bias(i * TOK + b * QBLK, s)
            pairs = range(N_HEADS // 2)
            sc4 = [_dot_nt(jnp.concatenate([q_ref[rows, 256 * g:256 * g + 128], q_ref[rows, 256 * g + 128:256 * (g + 1)]],
                                           axis=0), k_bd[g]) for g in range(4)]
            sc2 = [sc4[m // 2][(m % 2) * QBLK:(m % 2 + 1) * QBLK] for m in pairs]
            scs = [sc2[h // 2][:, (h % 2) * KBLK:(h % 2 + 1) * KBLK] + bias for h in range(N_HEADS)]
            ms = [jnp.maximum(jnp.max(scs[h], axis=-1, keepdims=True), sink_ref[h]) for h in range(N_HEADS)]
            es = [jnp.exp(scs[h] - ms[h]) for h in range(N_HEADS)]
            dens = [jnp.sum(es[h], axis=-1, keepdims=True) + jnp.exp(sink_ref[h] - ms[h]) for h in range(N_HEADS)]
            first = lane < HEAD_DIM
            e2 = [jnp.concatenate([es[2 * m].astype(BF16), es[2 * m + 1].astype(BF16)], axis=1) for m in pairs]
            o4 = [_dot(jnp.concatenate([e2[2 * g], e2[2 * g + 1]], axis=0), v_bd[g]) for g in range(4)]
            outs = [o4[m // 2][(m % 2) * QBLK:(m % 2 + 1) * QBLK]
                    * jnp.where(first, 1.0 / dens[2 * m], 1.0 / dens[2 * m + 1]) for m in pairs]
            o_ref[rows, :] = jnp.concatenate(outs, axis=1).astype(BF16)
            lse = jnp.zeros((QBLK, 128), F32)
            for h in range(N_HEADS):
                lse = lse + jnp.where(lane == h, ms[h] + jnp.log(dens[h]), 0.0)
            lse_ref[rows, :] = lse

    return pl.pallas_call(
        body, name="attn_fwd", grid=(s // TOK,),
        in_specs=[pl.BlockSpec(memory_space=pltpu.SMEM), _rows(TOK, 1024), _whole(kpad.shape), _whole(vpad.shape)],
        out_specs=[_rows(TOK, 1024), _rows(TOK, 128)],
        out_shape=[jax.ShapeDtypeStruct((s, 1024), BF16), jax.ShapeDtypeStruct((s, 128), F32)],
        compiler_params=_params(),
    )(sink, q, kpad, vpad)


def _tail(x1, o, gate, target, w_out1, gf):
    s = x1.shape[0]

    def body(x1_ref, o_ref, gate_ref, t_ref, w_ref, gf_ref, y1_ref, dx2_ref, do_ref, dgate_ref, loss_ref, gfn_ref,
             dx2h_ref):
        i = pl.program_id(0)

        @pl.when(i == 0)
        def _():
            loss_ref[...] = jnp.zeros_like(loss_ref)
            gfn_ref[...] = jnp.zeros_like(gfn_ref)

        g = gate_ref[...].astype(F32)
        sg = _sigmoid(g)
        sil = g * sg
        o = o_ref[...].astype(F32)
        y1 = (o * sil).astype(BF16)
        y1_ref[...] = y1
        x2 = x1_ref[...] + _dot(y1, w_ref[...])
        gf = gf_ref[...]
        r, xh, out = _rms_fwd(x2, gf)
        diff = out - t_ref[...]
        loss_ref[...] += jnp.sum(diff * diff, axis=0, keepdims=True) * (0.5 / D_MODEL)
        dout = diff * (1.0 / D_MODEL)
        gfn_ref[...] += jnp.sum(dout * xh, axis=0, keepdims=True)
        dx2 = _rms_bwd(dout, gf, r, xh)
        dx2_ref[...] = dx2
        dx2h = dx2.astype(BF16)
        dx2h_ref[...] = dx2h
        dy1 = _dot_nt(dx2h, w_ref[...])
        do_ref[...] = (dy1 * sil).astype(BF16)
        dgate_ref[...] = (dy1 * o * (sg * (1.0 + g * (1.0 - sg)))).astype(BF16)

    row = _rows(TOK, 1024)
    acc = _whole((1, 1024))
    return pl.pallas_call(
        body, name="tail", grid=(s // TOK,),
        in_specs=[row, row, row, row, _whole(w_out1.shape), acc],
        out_specs=[row, row, row, row, acc, acc, row],
        out_shape=[jax.ShapeDtypeStruct((s, 1024), BF16), jax.ShapeDtypeStruct((s, 1024), F32),
                   jax.ShapeDtypeStruct((s, 1024), BF16), jax.ShapeDtypeStruct((s, 1024), BF16),
                   jax.ShapeDtypeStruct((1, 1024), F32), jax.ShapeDtypeStruct((1, 1024), F32),
                   jax.ShapeDtypeStruct((s, 1024), BF16)],
        compiler_params=_params(),
    )(x1, o, gate, target, w_out1, gf)


def _attn_bwd(q, kpad, vpad, sink, o, lse, do, rope):
    s = q.shape[0]
    pad_t = (kpad.shape[1], kpad.shape[0])

    def body(sink_ref, q_ref, k_ref, v_ref, o_ref, lse_ref, do_ref, c_ref, lo_ref, hi_ref,
             dq_ref, dk_ref, dv_ref, ds_ref):
        i = pl.program_id(0)

        @pl.when(i == 0)
        def _():
            dk_ref[...] = jnp.zeros_like(dk_ref)
            dv_ref[...] = jnp.zeros_like(dv_ref)
            ds_ref[...] = jnp.zeros_like(ds_ref)

        lane = lax.broadcasted_iota(jnp.int32, (1, 128), 1)
        dsink = jnp.zeros((1, 128), F32)
        for b in range(TOK // QBLK):
            rows = slice(b * QBLK, (b + 1) * QBLK)
            start = pl.multiple_of(i * TOK + b * QBLK, QBLK)
            k_bd = _pair_operands(k_ref[pl.ds(start, KBLK), :])
            v_bd = _pair_operands(v_ref[pl.ds(start, KBLK), :])
            bias = _block_bias(i * TOK + b * QBLK, s)
            lanes_of = (lane < HEAD_DIM, lane >= HEAD_DIM)
            half = lambda t, j: t[:, j * KBLK:(j + 1) * KBLK]
            dqs, dks, dvs = [], [], []
            for g in range(4):
                pairs = (2 * g, 2 * g + 1)
                qs = {m: q_ref[rows, 128 * m:128 * (m + 1)] for m in pairs}
                dos = {m: do_ref[rows, 128 * m:128 * (m + 1)] for m in pairs}
                lses = {h: lse_ref[rows, h:h + 1] for h in range(4 * g, 4 * g + 4)}
                stacked = lambda parts: jnp.concatenate([parts[m] for m in pairs], axis=0)
                unstack = lambda t: {m: t[j * QBLK:(j + 1) * QBLK] for j, m in enumerate(pairs)}
                sc2 = unstack(_dot_nt(stacked(qs), k_bd[g]))
                ps = {2 * m + j: jnp.exp(half(sc2[m], j) + bias - lses[2 * m + j]) for m in pairs for j in range(2)}
                prods = {m: dos[m].astype(F32) * o_ref[rows, 128 * m:128 * (m + 1)].astype(F32) for m in pairs}
                deltas = {2 * m + j: jnp.sum(jnp.where(lanes_of[j], prods[m], 0.0), axis=-1, keepdims=True)
                          for m in pairs for j in range(2)}
                for h in range(4 * g, 4 * g + 4):
                    dsink = dsink + jnp.where(
                        lane == h, -jnp.sum(jnp.exp(sink_ref[h] - lses[h]) * deltas[h], axis=0, keepdims=True), 0.0)
                dp2 = unstack(_dot_nt(stacked(dos), v_bd[g]))
                ds2 = {m: jnp.concatenate(
                    [(ps[2 * m + j] * (half(dp2[m], j) - deltas[2 * m + j])).astype(BF16) for j in range(2)], axis=1)
                    for m in pairs}
                p2 = {m: jnp.concatenate([ps[2 * m].astype(BF16), ps[2 * m + 1].astype(BF16)], axis=1) for m in pairs}
                diag = lambda t: t[0:HEAD_DIM, 0:KBLK] + t[HEAD_DIM:128, KBLK:2 * KBLK]
                dvs.append(diag(_dot_tn(dos[pairs[0]], p2[pairs[0]])) + diag(_dot_tn(dos[pairs[1]], p2[pairs[1]])))
                dks.append(diag(_dot_tn(qs[pairs[0]], ds2[pairs[0]])) + diag(_dot_tn(qs[pairs[1]], ds2[pairs[1]])))
                dq2 = unstack(_dot(stacked(ds2), k_bd[g]) * Q_SCALE)
                dqs += [dq2[m] for m in pairs]
            dq = jnp.concatenate(dqs, axis=1)
            dq_ref[rows, :] = _rope_t(dq, c_ref[rows, :], lo_ref[rows, :], hi_ref[rows, :]).astype(BF16)
            dk_ref[:, pl.ds(start, KBLK)] += jnp.concatenate(dks, axis=0)
            dv_ref[:, pl.ds(start, KBLK)] += jnp.concatenate(dvs, axis=0)
        ds_ref[...] += dsink

    row = _rows(TOK, 1024)
    tab = _rows(TOK, 128)
    pad = _whole(kpad.shape)
    return pl.pallas_call(
        body, name="attn_bwd", grid=(s // TOK,),
        in_specs=[pl.BlockSpec(memory_space=pltpu.SMEM), row, pad, pad, row, tab, row, tab, tab, tab],
        out_specs=[row, _whole(pad_t), _whole(pad_t), _whole((1, 128))],
        out_shape=[jax.ShapeDtypeStruct((s, 1024), BF16), jax.ShapeDtypeStruct(pad_t, F32),
                   jax.ShapeDtypeStruct(pad_t, F32), jax.ShapeDtypeStruct((1, 128), F32)],
        compiler_params=_params(),
    )(sink, q, kpad, vpad, o, lse, do, *rope)


def _in_proj1_bwd(dq, dk_t, dv_t, dgate, x1, dx2, g1, w_in1, rope):
    s = x1.shape[0]

    def body(dq_ref, dka_ref, dkb_ref, dva_ref, dvb_ref, dgate_ref, x1_ref, dx2_ref, g_ref, w_ref,
             c_ref, lo_ref, hi_ref, dz_ref, dx1_ref, gn_ref, dx1h_ref):
        i = pl.program_id(0)
        dk = jnp.concatenate([dka_ref[...], dkb_ref[...]], axis=1).T
        dv = jnp.concatenate([dva_ref[...], dvb_ref[...]], axis=1).T

        @pl.when(i == 0)
        def _():
            gn_ref[...] = jnp.zeros_like(gn_ref)

        dz_ref[:, 0:1024] = dq_ref[...]
        dz_ref[:, 1024:1280] = _rope_t(dk, c_ref[...], lo_ref[...], hi_ref[...]).astype(BF16)
        dz_ref[:, 1280:1536] = dv.astype(BF16)
        dz_ref[:, 1536:2560] = dgate_ref[...]
        dh = _dot_nt(dz_ref[...], w_ref[...])
        g = g_ref[...]
        r, xh, _ = _rms_fwd(x1_ref[...], g)
        gn_ref[...] += jnp.sum(dh * xh, axis=0, keepdims=True)
        dx1 = dx2_ref[...] + _rms_bwd(dh, g, r, xh)
        dx1_ref[...] = dx1
        dx1h_ref[...] = dx1.astype(BF16)

    row = _rows(TOK, 1024)
    per = TOK // ATTN_WINDOW
    half_a = pl.BlockSpec((256, TOK // 2), lambda i: (0, per * i + 1))
    half_b = pl.BlockSpec((256, TOK // 2), lambda i: (0, per * i + 2))
    tab = _rows(TOK, 128)
    acc = _whole((1, 1024))
    return pl.pallas_call(
        body, name="in_proj1_bwd", grid=(s // TOK,),
        in_specs=[row, half_a, half_b, half_a, half_b, row, row, row, acc, _whole(w_in1.shape), tab, tab, tab],
        out_specs=[_rows(TOK, 2560), row, acc, row],
        out_shape=[jax.ShapeDtypeStruct((s, 2560), BF16), jax.ShapeDtypeStruct((s, 1024), F32),
                   jax.ShapeDtypeStruct((1, 1024), F32), jax.ShapeDtypeStruct((s, 1024), BF16)],
        compiler_params=_params(),
    )(dq, dk_t, dk_t, dv_t, dv_t, dgate, x1, dx2, g1, w_in1, *rope)


def _mix0_bwd(dx1, z0, w_out0, gv, ws, bsx, wg, scale, band, rider):
    s = dx1.shape[0]
    n_tiles = s // TOK

    def body(dx1h_ref, z_ref, zp_ref, zn_ref, wout_ref, gv_ref, ws_ref, bsx_ref, wg_ref, sc_ref, band_ref,
             dz_ref, dpn_ref, dws_ref, dbs_ref, dgv_ref, dsc_ref, dwg_ref):
        i = pl.program_id(0)
        dz_ref[:, 3072:4096] = jnp.zeros((TOK, 1024), BF16)

        @pl.when(i == 0)
        def _():
            for ref in (dws_ref, dbs_ref, dgv_ref, dsc_ref, dwg_ref):
                ref[...] = jnp.zeros_like(ref)

        dcat = _dot_nt(dx1h_ref[...], wout_ref[...])
        dya = dcat[:, 0:1024]
        dyb = dcat[:, 1024:2048]

        au = z_ref[:, 0:1024]
        av = z_ref[:, 1024:2048]
        ag = z_ref[:, 2048:3072]
        gv = gv_ref[...]
        u, du = _gelu_and_grad(au)
        v1, dv1 = _gelu_and_grad(av)
        rv, vh, v2, mixed = _mixer_a(v1, gv, ws_ref, bsx_ref[...])
        sg = _sigmoid(ag)
        sil = ag * sg
        dz_ref[:, 2048:3072] = (dya * u * mixed * (sg * (1.0 + ag * (1.0 - sg)))).astype(BF16)
        dz_ref[:, 0:1024] = (dya * mixed * sil * du).astype(BF16)
        dmixed = dya * u * sil
        lane = lax.broadcasted_iota(jnp.int32, (1, 128), 1)
        dm16 = dmixed.astype(BF16)
        dv2_rows = []
        for c in range(TOK // CHUNK):
            rows = slice(c * CHUNK, (c + 1) * CHUNK)
            cols_out = []
            for h in range(4):
                cols = slice(h * 256, (h + 1) * 256)
                dws_ref[h] += _dot_nt(dm16[rows, cols], v2[rows, cols])
                dbs_ref[...] += jnp.where(lane == h, jnp.sum(dmixed[rows, cols], axis=-1, keepdims=True), 0.0)
                cols_out.append(_dot_tn(ws_ref[h], dm16[rows, cols]))
            dv2_rows.append(jnp.concatenate(cols_out, axis=1))
        dv2 = jnp.concatenate(dv2_rows, axis=0)
        dgv_ref[...] += jnp.sum(dv2 * vh, axis=0, keepdims=True)
        dz_ref[:, 1024:2048] = (_rms_bwd(dv2, gv, rv, vh) * dv1).astype(BF16)

        bx = z_ref[:, 3072:4096]
        bg = z_ref[:, 4096:5120]
        counts = _window_counts(i, TOK, s)
        halo = _with_halo(i, n_tiles, zp_ref, bx, zn_ref).astype(BF16)
        ps = [p.astype(BF16) for p in _mixer_b_pooled(bx, halo, band_ref, counts)]
        pw = jnp.concatenate([_dot(ps[g], wg_ref[g]) for g in range(4)], axis=1)
        sgb = _sigmoid(bg)
        sc = sc_ref[...]
        dz_ref[:, 4096:5120] = (dyb * pw * sc * (sgb * (1.0 + bg * (1.0 - sgb)))).astype(BF16)
        dys = dyb * (bg * sgb)
        dsc_ref[...] += jnp.sum(dys * pw, axis=0, keepdims=True)
        dpw = (dys * sc).astype(BF16)
        for g in range(4):
            cols = slice(g * 256, (g + 1) * 256)
            dwg_ref[g] += _dot_tn(ps[g], dpw[:, cols])
            dpn_ref[:, cols] = (_dot_nt(dpw[:, cols], wg_ref[g]) / counts[g]).astype(BF16)

    prev, nxt = _halo_specs(s, 3)
    row = _rows(TOK, 1024)
    vec = _whole((1, 1024))
    return pl.pallas_call(
        rider.carried_by(body, 11, 7, n_tiles), name="mix0_bwd", grid=(n_tiles,),
        in_specs=[row, _rows(TOK, 5120), prev, nxt, _whole(w_out0.shape), vec, _whole(ws.shape),
                  _whole(bsx.shape), _whole(wg.shape), vec, _whole(band.shape)] + rider.in_specs,
        out_specs=[_rows(TOK, 5120), row, _whole((4, 128, 128)), _whole((128, 128)), vec, vec,
                   _whole((4, 256, 256))] + rider.out_specs,
        out_shape=[jax.ShapeDtypeStruct((s, 5120), BF16), jax.ShapeDtypeStruct((s, 1024), BF16),
                   jax.ShapeDtypeStruct((4, 128, 128), F32), jax.ShapeDtypeStruct((128, 128), F32),
                   jax.ShapeDtypeStruct((1, 1024), F32), jax.ShapeDtypeStruct((1, 1024), F32),
                   jax.ShapeDtypeStruct((4, 256, 256), F32)] + rider.out_shape,
        scratch_shapes=rider.scratch,
        compiler_params=_params(),
    )(dx1, z0, z0, z0, w_out0, gv, ws, bsx, wg, scale, band, *rider.parts)


def _fill_pooled_grad(dz0, dpn):
    s = dpn.shape[0]
    tok = min(s, 2 * TOK)
    n_tiles = s // tok
    band_t = _band_matrices(tok)[1]

    def body(dz_in_ref, dpn_ref, dpp_ref, dpx_ref, band_ref, dbx_ref):
        i = pl.program_id(0)
        dpn = dpn_ref[...]
        halo = _with_halo(i, n_tiles, dpp_ref, dpn, dpx_ref)
        counts = _window_counts(i, tok, s)
        for g in range(4):
            cols = slice(g * 256, (g + 1) * 256)
            dbx = _dot(band_ref[g], halo[:, cols]) - dpn[:, cols].astype(F32) * counts[g]
            dbx_ref[:, cols] = dbx.astype(BF16)

    prev, nxt = _halo_specs(s, 0, tok)
    return pl.pallas_call(
        body, name="fill_pooled_grad", grid=(n_tiles,),
        in_specs=[ANY, _rows(tok, 1024), prev, nxt, _whole(band_t.shape)],
        out_specs=pl.BlockSpec((tok, 1024), lambda i: (i, 3)),
        out_shape=jax.ShapeDtypeStruct(dz0.shape, BF16),
        input_output_aliases={0: 0},
        compiler_params=_params(),
    )(dz0, dpn, dpn, dpn, band_t)


def _in_proj0_bwd(dz0, x, dx1, g0, w_in0, rider):
    s = x.shape[0]
    n_tiles = s // TOK

    def body(dz_ref, x_ref, dx1_ref, g_ref, w_ref, dx_ref, gn_ref):
        i = pl.program_id(0)

        @pl.when(i == 0)
        def _():
            gn_ref[...] = jnp.zeros_like(gn_ref)

        dh = _dot_nt(dz_ref[...], w_ref[...])
        g0v = g_ref[...]
        r, xh, _ = _rms_fwd(x_ref[...], g0v)
        gn_ref[...] += jnp.sum(dh * xh, axis=0, keepdims=True)
        dx_ref[...] = dx1_ref[...] + _rms_bwd(dh, g0v, r, xh)

    row = _rows(TOK, 1024)
    vec = _whole((1, 1024))
    return pl.pallas_call(
        rider.carried_by(body, 5, 2, n_tiles), name="in_proj0_bwd", grid=(n_tiles,),
        in_specs=[_rows(TOK, 5120), row, row, vec, _whole(w_in0.shape)] + rider.in_specs,
        out_specs=[row, vec] + rider.out_specs,
        out_shape=[jax.ShapeDtypeStruct((s, 1024), F32), jax.ShapeDtypeStruct((1, 1024), F32)] + rider.out_shape,
        scratch_shapes=rider.scratch,
        compiler_params=_params(),
    )(dz0, x, dx1, g0, w_in0, *rider.parts)


def _weight_grad(a, b, n_blocks, split, name):
    s, k = a.shape
    n = b.shape[1]
    tn = n // n_blocks
    w = tn // split
    ts = min(s, 1024)

    def body(a_ref, b_ref, o_ref):
        @pl.when(pl.program_id(1) == 0)
        def _():
            o_ref[...] = jnp.zeros_like(o_ref)

        res = _dot_tn(a_ref[...], b_ref[...])
        for q in range(split):
            o_ref[q] += res[:, q * w:(q + 1) * w]

    return pl.pallas_call(
        body, name=name, grid=(n_blocks, s // ts),
        in_specs=[pl.BlockSpec((ts, k), lambda j, t: (t, 0)), pl.BlockSpec((ts, tn), lambda j, t: (t, j))],
        out_specs=pl.BlockSpec((split, k, w), lambda j, t: (j, 0, 0)),
        out_shape=jax.ShapeDtypeStruct((n_blocks * split, k, w), F32),
        compiler_params=_params(),
    )(a, b)


def _row_tile(rows, cols):
    t = rows
    while t * cols * 4 > (1 << 20) and t % 16 == 0:
        t //= 2
    return t


def _add_sibling(where, g, theirs, name):
    _, _, rows, cols = g.shape
    t = _row_tile(rows, cols)

    def body(where_ref, g_ref, t_ref, o_ref):
        o_ref[...] = (g_ref[...] + t_ref[...]).astype(BF16)

    spec = pl.BlockSpec((None, t, cols), lambda s, i, p: (s, i, 0))
    return pl.pallas_call(
        body, name=name, out_shape=jax.ShapeDtypeStruct((4, rows, cols), BF16),
        grid_spec=pltpu.PrefetchScalarGridSpec(
            num_scalar_prefetch=1, grid=(4, rows // t),
            in_specs=[pl.BlockSpec((None, None, t, cols), lambda s, i, p: (s, p[1], i, 0)), spec], out_specs=spec),
        compiler_params=_params())(where, g, theirs)


def _sum_chips(where, g, theirs, slots, name):
    _, _, rows, cols = g.shape
    t = _row_tile(rows, cols)

    def body(where_ref, g_ref, t_ref, s_ref, o_ref):
        me = where_ref[0]
        own = g_ref[...] + t_ref[...]
        acc = jnp.where(me == 0, own, s_ref[0].astype(F32))
        for k in range(1, 4):
            acc = acc + jnp.where(me == k, own, s_ref[k].astype(F32))
        o_ref[...] = acc

    return pl.pallas_call(
        body, name=name, out_shape=jax.ShapeDtypeStruct((rows, cols), F32),
        grid_spec=pltpu.PrefetchScalarGridSpec(
            num_scalar_prefetch=1, grid=(rows // t,),
            in_specs=[pl.BlockSpec((None, None, t, cols), lambda i, p: (p[0], p[1], i, 0)),
                      pl.BlockSpec((None, t, cols), lambda i, p: (p[0], i, 0)),
                      pl.BlockSpec((4, t, cols), lambda i, p: (0, i, 0))],
            out_specs=pl.BlockSpec((t, cols), lambda i, p: (i, 0))),
        compiler_params=_params())(where, g, theirs, slots)


def _adamw_halves(where, w, own, theirs, m, v, name):
    rows, cols = own.shape
    t = _row_tile(rows, cols)
    per = rows // t

    def body(where_ref, w_ref, own_ref, th_ref, m_ref, v_ref, g_ref, d_ref, nm_ref, nv_ref):
        g = jnp.where(pl.program_id(0) == where_ref[1], own_ref[...], th_ref[...])
        g_ref[...] = g
        m2 = ADAM_B1 * m_ref[...] + (1.0 - ADAM_B1) * g
        v2 = ADAM_B2 * v_ref[...] + (1.0 - ADAM_B2) * (g * g)
        m_hat = m2 / (1.0 - ADAM_B1 ** ADAM_STEP)
        v_hat = v2 / (1.0 - ADAM_B2 ** ADAM_STEP)
        d_ref[...] = -ADAM_LR * (m_hat / (jnp.sqrt(v_hat) + ADAM_EPS) + ADAM_WD * w_ref[...])
        nm_ref[...] = m2
        nv_ref[...] = v2

    full = pl.BlockSpec((t, cols), lambda h, i, p: (h * per + i, 0))
    half = pl.BlockSpec((t, cols), lambda h, i, p: (i, 0))
    shp = jax.ShapeDtypeStruct(w.shape, F32)
    return pl.pallas_call(
        body, name=name, out_shape=[shp] * 4,
        grid_spec=pltpu.PrefetchScalarGridSpec(
            num_scalar_prefetch=1, grid=(2, per), in_specs=[full, half, half, full, full], out_specs=[full] * 4),
        compiler_params=_params())(where, w, own, theirs, m, v)


def _place_shard(where, w, cut, name):
    if cut.kind == "cols":
        r, n = cut.full_shape
        blk, grid = (256, n // 4), (r // 256,)
        src_map, dst_map = (lambda i, p: (i, 0)), (lambda i, p: (i, p[0]))
    elif cut.kind == "rows":
        r, n = cut.full_shape
        per = r // 4 // 256
        blk, grid = (256, n), (per,)
        src_map, dst_map = (lambda i, p: (i, 0)), (lambda i, p: (p[0] * per + i, 0))
    else:
        g, r, n = cut.full_shape
        blk, grid = (g, r // 4, n), (1,)
        src_map, dst_map = (lambda i, p: (0, 0, 0)), (lambda i, p: (0, p[0], 0))

    def body(where_ref, w_ref, o_ref):
        o_ref[...] = w_ref[...].astype(BF16)

    return pl.pallas_call(
        body, name=name, out_shape=jax.ShapeDtypeStruct(cut.full_shape, BF16),
        grid_spec=pltpu.PrefetchScalarGridSpec(
            num_scalar_prefetch=1, grid=grid, in_specs=[pl.BlockSpec(blk, src_map)],
            out_specs=pl.BlockSpec(blk, dst_map)),
        compiler_params=_params())(where, w)


def _sum_small(first, second, third):
    rows = second.shape[1]

    def body(a_ref, b_ref, c_ref, o_ref):
        top = a_ref[0] + c_ref[0] + b_ref[0, 0:8]
        rest = b_ref[0, 8:rows]
        for k in range(1, 8):
            top = top + (a_ref[k] + c_ref[k] + b_ref[k, 0:8])
            rest = rest + b_ref[k, 8:rows]
        o_ref[0:8] = top
        o_ref[8:rows] = rest

    return pl.pallas_call(
        body, name="sum_small", in_specs=[_whole(first.shape), _whole(second.shape), _whole(third.shape)],
        out_specs=_whole(second.shape[1:]), out_shape=jax.ShapeDtypeStruct(second.shape[1:], F32),
        compiler_params=_params())(first, second, third)


def _adamw(w, g, m, v, name):
    rows, cols = w.shape
    t = _row_tile(rows, cols)

    def body(w_ref, g_ref, m_ref, v_ref, d_ref, nm_ref, nv_ref):
        g = g_ref[...]
        m2 = ADAM_B1 * m_ref[...] + (1.0 - ADAM_B1) * g
        v2 = ADAM_B2 * v_ref[...] + (1.0 - ADAM_B2) * (g * g)
        m_hat = m2 / (1.0 - ADAM_B1 ** ADAM_STEP)
        v_hat = v2 / (1.0 - ADAM_B2 ** ADAM_STEP)
        d_ref[...] = -ADAM_LR * (m_hat / (jnp.sqrt(v_hat) + ADAM_EPS) + ADAM_WD * w_ref[...])
        nm_ref[...] = m2
        nv_ref[...] = v2

    spec = pl.BlockSpec((t, cols), lambda i: (i, 0))
    shp = jax.ShapeDtypeStruct(w.shape, F32)
    return pl.pallas_call(body, name=name, grid=(rows // t,), in_specs=[spec] * 4, out_specs=[spec] * 3,
                          out_shape=[shp] * 3, compiler_params=_params())(w, g, m, v)


def _place():
    x, y, c = lax.axis_index("x"), lax.axis_index("y"), lax.axis_index("c")
    chips = [(1 - x, y), (x, 1 - y), (1 - x, 1 - y)]
    return x, y, c, chips


class _Sharded:
    def __init__(self, kind, full_shape):
        self.kind = kind
        self.full_shape = full_shape

    def in_full(self, ref, s, h):
        if self.kind == "cols":
            r, n = self.full_shape
            return ref.at[pl.ds(h * (r // 2), r // 2), pl.ds(pl.multiple_of(s * (n // 4), 128), n // 4)]
        if self.kind == "rows":
            r, _ = self.full_shape
            return ref.at[pl.ds(pl.multiple_of(s * (r // 4) + h * (r // 8), 8), r // 8), :]
        g, r, _ = self.full_shape
        return ref.at[pl.ds(h * (g // 2), g // 2), pl.ds(pl.multiple_of(s * (r // 4), 16), r // 4), :]


def _remote(src, dst, send_sem, recv_sem, to):
    return pltpu.make_async_remote_copy(src_ref=src, dst_ref=dst, send_sem=send_sem, recv_sem=recv_sem,
                                        device_id=to, device_id_type=MESH)


def _start_remote(src, dst, send_sem, recv_sem, to):
    cp = _remote(src, dst, send_sem, recv_sem, to)
    cp.start()
    return cp


class _Gather:
    def __init__(self, fulls, cuts):
        n = len(fulls)
        self.fulls, self.cuts = list(fulls), list(cuts)
        self.in_specs = [ANY] * n
        self.out_specs = [ANY] * n
        self.out_shape = [jax.ShapeDtypeStruct(cut.full_shape, BF16) for cut in cuts]
        self.scratch = [pltpu.SemaphoreType.DMA((6 * n,)), pltpu.SemaphoreType.DMA((6 * n,))]

    def _step(self, step, src, out, send_sems, recv_sems):
        n, cuts = len(self.fulls), self.cuts
        x, y, c, chips = _place()
        me = 2 * x + y

        def ends(w, s, h, from_src):
            dst = cuts[w].in_full(out[w], s, h)
            return (cuts[w].in_full(src[w], s, h) if from_src else dst), dst

        for w in range(n):
            for j, chip in enumerate(chips):
                s = 2 * chip[0] + chip[1]
                k, k2 = 3 * w + j, 3 * n + 3 * w + j
                if step == "send":
                    _start_remote(*ends(w, me, c, True), send_sems.at[k], recv_sems.at[k], (*chip, c))
                elif step == "pass_on":
                    _remote(*ends(w, s, c, False), send_sems.at[k], recv_sems.at[k], (x, y, c)).wait_recv()
                    _start_remote(*ends(w, s, c, False), send_sems.at[k2], recv_sems.at[k2], (x, y, 1 - c))
                else:
                    _remote(*ends(w, s, 1 - c, False), send_sems.at[k2], recv_sems.at[k2], (x, y, c)).wait_recv()
                    _remote(*ends(w, me, c, True), send_sems.at[k], recv_sems.at[k], (x, y, c)).wait_send()
                    _remote(*ends(w, s, c, False), send_sems.at[k2], recv_sems.at[k2], (x, y, c)).wait_send()

    def carried_by(self, body, n_in, n_out, n_steps, step_index=lambda: pl.program_id(0)):
        k = len(self.fulls)

        def carrier(*refs):
            ins, src = refs[:n_in], refs[n_in:n_in + k]
            outs, out = refs[n_in + k:n_in + k + n_out], refs[n_in + k + n_out:n_in + 2 * k + n_out]
            sems = refs[n_in + 2 * k + n_out:]
            for step, at in (("send", 0), ("pass_on", 3 * n_steps // 4), ("finish", n_steps - 1)):
                if step == "finish":
                    body(*ins, *outs)

                @pl.when(step_index() == at)
                def _():
                    self._step(step, src, out, *sems)

        return carrier


def _exchange_halves(grads, name):
    n = len(grads)

    def body(*refs):
        g = refs[:n]
        theirs = refs[n:2 * n]
        send_sems, recv_sems = refs[2 * n:]
        x, y, c, _ = _place()
        sends = [_start_remote(g[w].at[:, 1 - c], theirs[w], send_sems.at[w], recv_sems.at[w], (x, y, 1 - c))
                 for w in range(n)]
        for w in range(n):
            _remote(g[w].at[:, 1 - c], theirs[w], send_sems.at[w], recv_sems.at[w], (x, y, c)).wait_recv()
        for cp in sends:
            cp.wait_send()

    return pl.pallas_call(
        body, name=name,
        in_specs=[ANY] * n, out_specs=[ANY] * n,
        out_shape=[jax.ShapeDtypeStruct((4,) + g.shape[2:], F32) for g in grads],
        scratch_shapes=[pltpu.SemaphoreType.DMA((n,)), pltpu.SemaphoreType.DMA((n,))],
        compiler_params=pltpu.CompilerParams(has_side_effects=True),
    )(*grads)


def _gather_small(small_ref, gathered, send_sems, recv_sems, first_sem, local_sem, start):
    x, y, c, _ = _place()
    me = 4 * x + 2 * y + c
    flips = [(fx, fy, fc) for fx in range(2) for fy in range(2) for fc in range(2)][1:]
    own = pltpu.make_async_copy(small_ref, gathered.at[me], local_sem)
    if start:
        own.start()
    else:
        own.wait()
    for k, (fx, fy, fc) in enumerate(flips):
        peer = (x + fx - 2 * x * fx, y + fy - 2 * y * fy, c + fc - 2 * c * fc)
        sems = (send_sems.at[first_sem + k], recv_sems.at[first_sem + k])
        if start:
            _start_remote(small_ref, gathered.at[me], *sems, peer)
        else:
            cp = _remote(small_ref, gathered.at[4 * peer[0] + 2 * peer[1] + peer[2]], *sems, (x, y, c))
            cp.wait_recv()
            cp.wait_send()


class _ScatterRider:
    def __init__(self, parts, small):
        n = len(parts)
        self.n = n
        self.parts = list(parts) + [small]
        self.in_specs = [ANY] * (n + 1)
        self.out_specs = [ANY] * (n + 1)
        self.out_shape = ([jax.ShapeDtypeStruct(a.shape, a.dtype) for a in parts]
                          + [jax.ShapeDtypeStruct((8,) + small.shape, small.dtype)])
        self.scratch = [pltpu.SemaphoreType.DMA((3 * n + 7,)), pltpu.SemaphoreType.DMA((3 * n + 7,)),
                        pltpu.SemaphoreType.DMA]

    def _copies(self, p, out, send_sems, recv_sems, local_sem, start):
        x, y, c, chips = _place()
        me = 2 * x + y
        n = self.n
        _gather_small(p[n], out[n], send_sems, recv_sems, 3 * n, local_sem, start)
        for w in range(n):
            for j, chip in enumerate(chips):
                s = 2 * chip[0] + chip[1]
                if start:
                    _start_remote(p[w].at[s], out[w].at[me], send_sems.at[3 * w + j], recv_sems.at[3 * w + j],
                                  (*chip, c))
                else:
                    cp = _remote(p[w].at[s], out[w].at[s], send_sems.at[3 * w + j], recv_sems.at[3 * w + j],
                                 (x, y, c))
                    cp.wait_recv()
                    cp.wait_send()

    def carried_by(self, body, n_in, n_out, n_tiles):
        k = len(self.parts)

        def carrier(*refs):
            ins, mine = refs[:n_in], refs[n_in:n_in + k]
            outs, slots = refs[n_in + k:n_in + k + n_out], refs[n_in + k + n_out:n_in + 2 * k + n_out]
            sems = refs[n_in + 2 * k + n_out:]

            @pl.when(pl.program_id(0) == 0)
            def _():
                self._copies(mine, slots, *sems, start=True)

            body(*ins, *outs)

            @pl.when(pl.program_id(0) == n_tiles - 1)
            def _():
                self._copies(mine, slots, *sems, start=False)

        return carrier


def _share_halves(halves, small):
    n = len(halves)

    def body(*refs):
        hv = refs[:n]
        small_ref = refs[n]
        out = refs[n + 1:2 * n + 1]
        gathered = refs[2 * n + 1]
        send_sems, recv_sems, local_sem = refs[2 * n + 2:]
        x, y, c, _ = _place()
        sends = [_start_remote(hv[w], out[w], send_sems.at[w], recv_sems.at[w], (x, y, 1 - c)) for w in range(n)]
        _gather_small(small_ref, gathered, send_sems, recv_sems, n, local_sem, True)
        for w in range(n):
            _remote(hv[w], out[w], send_sems.at[w], recv_sems.at[w], (x, y, c)).wait_recv()
        for cp in sends:
            cp.wait_send()
        _gather_small(small_ref, gathered, send_sems, recv_sems, n, local_sem, False)

    return pl.pallas_call(
        body, name="share_halves",
        in_specs=[ANY] * (n + 1), out_specs=[ANY] * (n + 1),
        out_shape=[jax.ShapeDtypeStruct(a.shape, F32) for a in halves] + [jax.ShapeDtypeStruct((8,) + small.shape, F32)],
        scratch_shapes=[pltpu.SemaphoreType.DMA((n + 7,)), pltpu.SemaphoreType.DMA((n + 7,)),
                        pltpu.SemaphoreType.DMA],
        compiler_params=pltpu.CompilerParams(has_side_effects=True),
    )(*halves, small)


SMALL_ROWS = 80


def _pack_small(vecs, ws, bs, sink, extra=None):
    ws = jnp.zeros((64, 1024), F32) if ws is None else ws.reshape(64, 1024)
    bs = jnp.zeros((1, 512), F32) if bs is None else bs.reshape(1, 512)
    sink = jnp.zeros((1, 16), F32) if sink is None else sink.reshape(1, 16)
    extra = jnp.zeros((1, 1024), F32) if extra is None else extra.reshape(1, 1024)
    top = jnp.concatenate(
        [v.reshape(1, 1024) for v in vecs]
        + [jnp.pad(bs, ((0, 0), (0, 512))), jnp.pad(sink, ((0, 0), (0, 1008))), extra], axis=0)
    return jnp.concatenate([top, ws, jnp.zeros((8, 1024), F32)], axis=0)


def _unpack_small(p):
    vecs = [p[k] for k in range(5)]
    return vecs, p[8:72].reshape(4, 128, 128), p[5, :512].reshape(4, 128), p[6, :16]


def kernel(x, norm_0, w_in_0, a_v_norm_0, a_spatial_w_0, a_spatial_b_0, b_group_w_0, b_scale_0, w_out_0, norm_1, w_in_1, sink_1, w_out_1, final_norm, loss_target, m_norm_0, m_w_in_0, m_a_v_norm_0, m_a_spatial_w_0, m_a_spatial_b_0, m_b_group_w_0, m_b_scale_0, m_w_out_0, m_norm_1, m_w_in_1, m_sink_1, m_w_out_1, m_final_norm, v_norm_0, v_w_in_0, v_a_v_norm_0, v_a_spatial_w_0, v_a_spatial_b_0, v_b_group_w_0, v_b_scale_0, v_w_out_0, v_norm_1, v_w_in_1, v_sink_1, v_w_out_1, v_final_norm):
    s = x.shape[1]
    xs = x.reshape(s, D_MODEL)
    target = loss_target.reshape(s, D_MODEL)

    cuts = [_Sharded("cols", (1024, 5120)), _Sharded("rows", (2048, 1024)), _Sharded("cols", (1024, 2560)),
            _Sharded("rows", (1024, 1024)), _Sharded("mid", (4, 256, 256))]
    big_w = [w_in_0, w_out_0, w_in_1, w_out_1, b_group_w_0]
    big_m = [m_w_in_0, m_w_out_0, m_w_in_1, m_w_out_1, m_b_group_w_0]
    big_v = [v_w_in_0, v_w_out_0, v_w_in_1, v_w_out_1, v_b_group_w_0]
    where = jnp.stack([2 * lax.axis_index("x") + lax.axis_index("y"), lax.axis_index("c")]).astype(jnp.int32)
    placed = [_place_shard(where, w, cut, f"place_shard{k}") for k, (w, cut) in enumerate(zip(big_w, cuts))]
    cx, cy = lax.axis_index("x"), lax.axis_index("y")
    order = jnp.stack([2 * cx + cy, 2 * (1 - cx) + cy, 2 * cx + 1 - cy, 2 * (1 - cx) + 1 - cy]).astype(jnp.int32)

    row = lambda v: v.reshape(1, 1024)
    ws16 = a_spatial_w_0.astype(BF16)
    bsx = jnp.repeat(a_spatial_b_0.T, 256, axis=1)
    band = _band_matrices(TOK)[0]
    rope = _rope_tables(s)

    h0, z0, w_in0 = _in_proj0_own(order, xs, row(norm_0), w_in_0.astype(BF16), _Gather(placed[:1], cuts[:1]))
    z0, w_out0, wg = _in_proj0_rest(order, h0, w_in0, z0, _Gather([placed[1], placed[4]], [cuts[1], cuts[4]]))
    cat, x1, w_in1, w_out1 = _mix0_fwd(xs, z0, w_out0, row(a_v_norm_0), ws16, bsx, wg, row(b_scale_0), band,
                                       _Gather(placed[2:4], cuts[2:4]))
    h1, q, k, v, gate = _in_proj1(x1, row(norm_1), w_in1, rope)
    kpad = jnp.pad(k, ((ATTN_WINDOW, ATTN_WINDOW), (0, 0)))
    vpad = jnp.pad(v, ((ATTN_WINDOW, ATTN_WINDOW), (0, 0)))
    o, lse = _attn_fwd(q, kpad, vpad, sink_1)
    y1, dx2, do, dgate, loss_lanes, g_final, dx2h = _tail(x1, o, gate, target, w_out1, row(final_norm))

    dq, dkpad, dvpad, dsink = _attn_bwd(q, kpad, vpad, sink_1, o, lse, do, rope)
    dz1, dx1, g_norm1, dx1h = _in_proj1_bwd(dq, dkpad, dvpad, dgate, x1, dx2, row(norm_1), w_in1, rope)

    g_w_in1 = _weight_grad(h1, dz1, 2, 2, "grad_w_in1").reshape(4, 2, 512, 640)
    g_w_out1 = _weight_grad(y1, dx2h, 1, 1, "grad_w_out1").reshape(4, 2, 128, 1024)
    g_w_out0 = _weight_grad(cat, dx1h, 1, 1, "grad_w_out0").reshape(4, 2, 256, 1024)
    first = [g_w_out0, g_w_in1, g_w_out1]
    theirs1 = _exchange_halves(first, "exchange_halves1")
    parts1 = [_add_sibling(where, g, t, f"add_sibling1_{k}") for k, (g, t) in enumerate(zip(first, theirs1))]
    zero = jnp.zeros((1024,), F32)
    small1 = _pack_small([zero, zero, zero, g_norm1, g_final], None, None, dsink[0, :16])[:8]
    dz0, dpn, d_ws, d_bs, d_gv, d_scale, d_wg, *slots1, small1_all = _mix0_bwd(
        dx1h, z0, w_out0, row(a_v_norm_0), ws16, bsx, wg, row(b_scale_0), band, _ScatterRider(parts1, small1))
    dz0 = _fill_pooled_grad(dz0, dpn)
    g_w_in0 = _weight_grad(h0, dz0, 4, 1, "grad_w_in0").reshape(4, 2, 512, 1280)
    g_wg = d_wg.reshape(2, 2, 4, 64, 256).transpose(2, 0, 1, 3, 4).reshape(4, 2, 128, 256)
    second = [g_w_in0, g_wg]
    theirs2 = _exchange_halves(second, "exchange_halves2")
    parts2 = [_add_sibling(where, g, t, f"add_sibling2_{k}") for k, (g, t) in enumerate(zip(second, theirs2))]
    small2 = _pack_small([zero, d_gv, d_scale, zero, zero], d_ws, d_bs[:, :4].T, None)
    grad_x, g_norm0, *slots2, small2_all = _in_proj0_bwd(
        dz0, xs, dx1, row(norm_0), w_in0, _ScatterRider(parts2, small2))

    grads = [g_w_in0, g_w_out0, g_w_in1, g_w_out1, g_wg]
    theirs = [theirs2[0], theirs1[0], theirs1[1], theirs1[2], theirs2[1]]
    slots = [slots2[0], slots1[0], slots1[1], slots1[2], slots2[1]]
    n = len(grads)
    reduced = [_sum_chips(where, grads[w], theirs[w], slots[w], f"sum_chips{w}") for w in range(n)]
    small3 = _pack_small([g_norm0, zero, zero, zero, zero], None, None, None, loss_lanes)[:8]
    *from_sibling, small3_all = _share_halves(reduced, small3)

    out_g, out_d, out_m, out_v = {}, {}, {}, {}
    names = ["w_in_0", "w_out_0", "w_in_1", "w_out_1", "b_group_w_0"]
    for w in range(n):
        shape = big_w[w].shape
        two_d = (-1, shape[-1])
        outs = _adamw_halves(where, big_w[w].reshape(two_d), reduced[w], from_sibling[w], big_m[w].reshape(two_d),
                             big_v[w].reshape(two_d), f"adamw{w}")
        out_g[names[w]], out_d[names[w]], out_m[names[w]], out_v[names[w]] = (a.reshape(shape) for a in outs)

    g_small = _sum_small(small1_all, small2_all, small3_all)
    small_names = ["norm_0", "a_v_norm_0", "b_scale_0", "norm_1", "final_norm"]
    pack = lambda vecs, ws_, bs_, sk: _pack_small(vecs, ws_, bs_, sk)
    w_small = pack([norm_0, a_v_norm_0, b_scale_0, norm_1, final_norm], a_spatial_w_0, a_spatial_b_0, sink_1)
    m_small = pack([m_norm_0, m_a_v_norm_0, m_b_scale_0, m_norm_1, m_final_norm], m_a_spatial_w_0,
                   m_a_spatial_b_0, m_sink_1)
    v_small = pack([v_norm_0, v_a_v_norm_0, v_b_scale_0, v_norm_1, v_final_norm], v_a_spatial_w_0,
                   v_a_spatial_b_0, v_sink_1)
    d_small, nm_small, nv_small = _adamw(w_small, g_small, m_small, v_small, "adamw_small")
    for store, packed in ((out_g, g_small), (out_d, d_small), (out_m, nm_small), (out_v, nv_small)):
        vecs, ws_, bs_, sk = _unpack_small(packed)
        for name, vec in zip(small_names, vecs):
            store[name] = vec
        store["a_spatial_w_0"], store["a_spatial_b_0"], store["sink_1"] = ws_, bs_, sk

    loss = jnp.sum(g_small[7])
    order = ["norm_0", "w_in_0", "a_v_norm_0", "a_spatial_w_0", "a_spatial_b_0", "b_group_w_0", "b_scale_0",
             "w_out_0", "norm_1", "w_in_1", "sink_1", "w_out_1", "final_norm"]
    return (loss, grad_x.reshape(1, s, D_MODEL), *[out_g[k] for k in order], *[out_d[k] for k in order],
            *[out_m[k] for k in order], *[out_v[k] for k in order])
```

```python
import functools

import numpy as np
import jax
import jax.numpy as jnp
from jax import lax
from jax.experimental import pallas as pl
from jax.experimental.pallas import tpu as pltpu

F32 = jnp.float32
BF16 = jnp.bfloat16
MESH = pl.DeviceIdType.MESH

D_MODEL = 1024
EPS = 1e-6
NEG_INF = -1e30
CHUNK = 128
POOL_WINDOWS = (2, 4, 8, 16)
HALO = 16
N_HEADS = 16
HEAD_DIM = 64
ATTN_WINDOW = 128
ROPE_THETA = 500000.0
ROT_DIM = 16
ADAM_LR = 0.001
ADAM_B1 = 0.9
ADAM_B2 = 0.999
ADAM_EPS = 1e-08
ADAM_WD = 0.01
ADAM_STEP = 10

TOK = 256
VMEM_LIMIT = 56 * 1024 * 1024


def _params(**kw):
    return pltpu.CompilerParams(vmem_limit_bytes=VMEM_LIMIT, **kw)


def _whole(shape):
    nd = len(shape)
    return pl.BlockSpec(shape, lambda *_: (0,) * nd)


def _rows(t, n):
    return pl.BlockSpec((t, n), lambda i: (i, 0))


ANY = pl.BlockSpec(memory_space=pl.ANY)

_G0 = 0.7978845608028654
_G1 = 0.044715


def _gelu(x):
    return 0.5 * x * (1.0 + jnp.tanh(_G0 * (x + _G1 * x * x * x)))


def _gelu_and_grad(x):
    x2 = x * x
    t = jnp.tanh(_G0 * (x + _G1 * x2 * x))
    half = 0.5 * (1.0 + t)
    return x * half, half + 0.5 * x * (1.0 - t * t) * (_G0 * (1.0 + 3.0 * _G1 * x2))


def _sigmoid(x):
    return 1.0 / (1.0 + jnp.exp(-x))


def _dot(a, b):
    return jnp.dot(a, b, preferred_element_type=F32)


def _dot_nt(a, b):
    return lax.dot_general(a, b, (((1,), (1,)), ((), ())), preferred_element_type=F32)


def _dot_tn(a, b):
    return lax.dot_general(a, b, (((0,), (0,)), ((), ())), preferred_element_type=F32)


def _rms_fwd(x, g):
    r = lax.rsqrt(jnp.mean(x * x, axis=-1, keepdims=True) + EPS)
    xh = x * r
    return r, xh, xh * g


def _rms_bwd(dy, g, r, xh):
    dxh = dy * g
    return r * (dxh - xh * jnp.mean(dxh * xh, axis=-1, keepdims=True))


def _band_matrices(t):
    r = np.arange(t)[:, None]
    j = np.arange(t + 2 * HALO)[None, :]
    fwd, bwd = [], []
    for w in POOL_WINDOWS:
        d = j - r - HALO
        fwd.append((d >= -(w // 2)) & (d < w // 2))
        bwd.append((d >= -(w // 2) + 1) & (d <= w // 2))
    return (jnp.asarray(np.stack(fwd), BF16), jnp.asarray(np.stack(bwd), BF16))


def _window_counts(i, t, s):
    tok = i * t + lax.broadcasted_iota(jnp.int32, (t, 1), 0)
    out = []
    for w in POOL_WINDOWS:
        cnt = jnp.minimum(tok + w // 2, s) - jnp.maximum(tok - w // 2, 0)
        out.append(cnt.astype(F32))
    return out


def _rope_tables(s):
    inv = np.float32(ROPE_THETA) ** (-np.arange(0, ROT_DIM, 2, dtype=np.float32) / np.float32(ROT_DIM))
    ang = np.arange(s, dtype=np.float32)[:, None] * inv.astype(np.float32)[None, :]
    cos, sin = np.cos(ang).astype(np.float32), np.sin(ang).astype(np.float32)
    z8 = np.zeros((s, 8), np.float32)
    z48 = np.zeros((s, HEAD_DIM - ROT_DIM), np.float32)
    c = np.concatenate([cos, cos, np.ones((s, HEAD_DIM - ROT_DIM), np.float32)], axis=1)
    s_lo = np.concatenate([z8, sin, z48], axis=1)
    s_hi = np.concatenate([-sin, z8, z48], axis=1)
    return tuple(jnp.asarray(np.concatenate([a, a], axis=1)) for a in (c, s_lo, s_hi))


def _rope(x, c, s_lo, s_hi):
    n = x.shape[1]
    reps = n // 128
    c, s_lo, s_hi = (jnp.tile(a, (1, reps)) for a in (c, s_lo, s_hi))
    return x * c + pltpu.roll(x, 8, 1) * s_lo + pltpu.roll(x, n - 8, 1) * s_hi


def _rope_t(dx, c, s_lo, s_hi):
    n = dx.shape[1]
    reps = n // 128
    c, s_lo, s_hi = (jnp.tile(a, (1, reps)) for a in (c, s_lo, s_hi))
    return dx * c + pltpu.roll(dx * s_lo, n - 8, 1) + pltpu.roll(dx * s_hi, 8, 1)


def _in_proj0_own(order, x, g0, w_own, rider):
    s = x.shape[0]
    n = w_own.shape[1]

    def body(order_ref, x_ref, g_ref, w_ref, h_ref, z_ref):
        _, _, h = _rms_fwd(x_ref[...], g_ref[...])
        h = h.astype(BF16)
        h_ref[...] = h
        z_ref[...] = _dot(h, w_ref[...])

    return pl.pallas_call(
        rider.carried_by(body, 4, 2, s // TOK), name="in_proj0_own",
        grid_spec=pltpu.PrefetchScalarGridSpec(
            num_scalar_prefetch=1, grid=(s // TOK,),
            in_specs=[pl.BlockSpec((TOK, D_MODEL), lambda i, o: (i, 0)), pl.BlockSpec((1, D_MODEL), lambda i, o: (0, 0)),
                      pl.BlockSpec(w_own.shape, lambda i, o: (0, 0))] + rider.in_specs,
            out_specs=[pl.BlockSpec((TOK, D_MODEL), lambda i, o: (i, 0)),
                       pl.BlockSpec((TOK, n), lambda i, o: (i, o[0]))] + rider.out_specs,
            scratch_shapes=rider.scratch),
        out_shape=[jax.ShapeDtypeStruct((s, D_MODEL), BF16), jax.ShapeDtypeStruct((s, 4 * n), F32)] + rider.out_shape,
        input_output_aliases={4 + j: 2 + j for j in range(len(rider.fulls))},
        compiler_params=_params(),
    )(order, x, g0, w_own, *rider.fulls)


def _in_proj0_rest(order, h0, w_in0, z0, rider):
    s = h0.shape[0]
    tok = min(s, 4 * TOK)
    n_tiles = s // tok
    n = w_in0.shape[1] // 4

    def body(order_ref, h_ref, w_ref, z_in_ref, z_ref):
        z_ref[...] = _dot(h_ref[...], w_ref[...])

    return pl.pallas_call(
        rider.carried_by(body, 4, 1, 3 * n_tiles, lambda: pl.program_id(0) * n_tiles + pl.program_id(1)),
        name="in_proj0_rest",
        grid_spec=pltpu.PrefetchScalarGridSpec(
            num_scalar_prefetch=1, grid=(3, n_tiles),
            in_specs=[pl.BlockSpec((tok, D_MODEL), lambda k, i, o: (i, 0)),
                      pl.BlockSpec((w_in0.shape[0], n), lambda k, i, o: (0, o[1 + k])), ANY] + rider.in_specs,
            out_specs=[pl.BlockSpec((tok, n), lambda k, i, o: (i, o[1 + k]))] + rider.out_specs,
            scratch_shapes=rider.scratch),
        out_shape=[jax.ShapeDtypeStruct(z0.shape, F32)] + rider.out_shape,
        input_output_aliases={3: 0, **{4 + j: 1 + j for j in range(len(rider.fulls))}},
        compiler_params=_params(),
    )(order, h0, w_in0, z0, *rider.fulls)


def _halo_specs(s, col_block, tok=TOK):
    per = tok // HALO
    last = s // HALO - 1
    prev = pl.BlockSpec((HALO, 1024), lambda i: (jnp.maximum(i * per - 1, 0), col_block))
    nxt = pl.BlockSpec((HALO, 1024), lambda i: (jnp.minimum((i + 1) * per, last), col_block))
    return prev, nxt


def _with_halo(i, n_tiles, prev_ref, cur, next_ref):
    prev = prev_ref[...]
    nxt = next_ref[...]
    prev = jnp.where(i > 0, prev, jnp.zeros_like(prev))
    nxt = jnp.where(i < n_tiles - 1, nxt, jnp.zeros_like(nxt))
    return jnp.concatenate([prev, cur, nxt], axis=0)


def _mixer_a(v1, gv, ws_ref, bsx):
    rv, vh, v2 = _rms_fwd(v1, gv)
    v2 = v2.astype(BF16)
    rows = []
    for c in range(v1.shape[0] // CHUNK):
        cols = [_dot(ws_ref[h], v2[c * CHUNK:(c + 1) * CHUNK, h * 256:(h + 1) * 256]) for h in range(4)]
        rows.append(jnp.concatenate(cols, axis=1) + bsx)
    return rv, vh, v2, jnp.concatenate(rows, axis=0)


def _mixer_b_pooled(bx, halo, band_ref, counts):
    out = []
    for g in range(4):
        win = _dot(band_ref[g], halo[:, g * 256:(g + 1) * 256])
        out.append(win / counts[g] - bx[:, g * 256:(g + 1) * 256])
    return out


def _mix0_fwd(x, z0, w_out0, gv, ws, bsx, wg, scale, band, rider):
    s = x.shape[0]
    n_tiles = s // TOK
    k = len(rider.fulls)

    def body(z_ref, zp_ref, zn_ref, x_ref, wout_ref, gv_ref, ws_ref, bsx_ref, wg_ref, sc_ref, band_ref,
             cat_ref, x1_ref):
        i = pl.program_id(0)
        au = z_ref[:, 0:1024]
        av = z_ref[:, 1024:2048]
        ag = z_ref[:, 2048:3072]
        _, _, _, mixed = _mixer_a(_gelu(av), gv_ref[...], ws_ref, bsx_ref[...])
        cat_ref[:, 0:1024] = (_gelu(au) * mixed * (ag * _sigmoid(ag))).astype(BF16)

        bx = z_ref[:, 3072:4096]
        bg = z_ref[:, 4096:5120]
        halo = _with_halo(i, n_tiles, zp_ref, bx, zn_ref).astype(BF16)
        ps = _mixer_b_pooled(bx, halo, band_ref, _window_counts(i, TOK, s))
        pw = jnp.concatenate([_dot(ps[g].astype(BF16), wg_ref[g]) for g in range(4)], axis=1)
        cat_ref[:, 1024:2048] = (pw * sc_ref[...] * (bg * _sigmoid(bg))).astype(BF16)

        x1_ref[...] = x_ref[...] + _dot(cat_ref[...], wout_ref[...])

    prev, nxt = _halo_specs(s, 3)
    return pl.pallas_call(
        rider.carried_by(body, 11, 2, n_tiles), name="mix0_fwd", grid=(n_tiles,),
        in_specs=[_rows(TOK, 5120), prev, nxt, _rows(TOK, D_MODEL), _whole(w_out0.shape), _whole((1, 1024)),
                  _whole(ws.shape), _whole(bsx.shape), _whole(wg.shape), _whole((1, 1024)), _whole(band.shape)]
        + rider.in_specs,
        out_specs=[_rows(TOK, 2048), _rows(TOK, D_MODEL)] + rider.out_specs,
        out_shape=[jax.ShapeDtypeStruct((s, 2048), BF16), jax.ShapeDtypeStruct((s, D_MODEL), F32)] + rider.out_shape,
        input_output_aliases={11 + j: 2 + j for j in range(k)},
        scratch_shapes=rider.scratch,
        compiler_params=_params(),
    )(z0, z0, z0, x, w_out0, gv, ws, bsx, wg, scale, band, *rider.fulls)


def _in_proj1(x1, g1, w_in1, rope):
    s = x1.shape[0]

    def body(x_ref, g_ref, w_ref, c_ref, lo_ref, hi_ref, h_ref, q_ref, k_ref, v_ref, gate_ref):
        halves = [slice(r * TOK, (r + 1) * TOK) for r in range(tok // TOK)]
        hs = [_rms_fwd(x_ref[rows, :], g_ref[...])[2].astype(BF16) for rows in halves]
        for rows, h in zip(halves, hs):
            h_ref[rows, :] = h
        qs = [_dot(h, w_ref[:, 0:1024]) for h in hs]
        kvs = [_dot(h, w_ref[:, 1024:1536]) for h in hs]
        for rows, q, kv in zip(halves, qs, kvs):
            tabs = (c_ref[rows, :], lo_ref[rows, :], hi_ref[rows, :])
            q_ref[rows, :] = (_rope(q, *tabs) * Q_SCALE).astype(BF16)
            k_ref[rows, :] = _rope(kv[:, 0:256], *tabs).astype(BF16)
            v_ref[rows, :] = kv[:, 256:512].astype(BF16)
        for rows, h in zip(halves, hs):
            gate_ref[rows, :] = _dot(h, w_ref[:, 1536:2560]).astype(BF16)

    tok = min(s, 2 * TOK)
    tab = _rows(tok, 128)
    return pl.pallas_call(
        body, name="in_proj1", grid=(s // tok,),
        in_specs=[_rows(tok, D_MODEL), _whole((1, D_MODEL)), _whole(w_in1.shape), tab, tab, tab],
        out_specs=[_rows(tok, 1024), _rows(tok, 1024), _rows(tok, 256), _rows(tok, 256), _rows(tok, 1024)],
        out_shape=[jax.ShapeDtypeStruct((s, 1024), BF16), jax.ShapeDtypeStruct((s, 1024), BF16),
                   jax.ShapeDtypeStruct((s, 256), BF16), jax.ShapeDtypeStruct((s, 256), BF16),
                   jax.ShapeDtypeStruct((s, 1024), BF16)],
        compiler_params=_params(),
    )(x1, g1, w_in1, *rope)


QBLK = 128
KBLK = QBLK + 2 * ATTN_WINDOW
Q_SCALE = HEAD_DIM ** -0.5
HEAD_BATCH = 4


def _block_bias(q0, s):
    r = lax.broadcasted_iota(jnp.int32, (QBLK, KBLK), 0)
    c = lax.broadcasted_iota(jnp.int32, (QBLK, KBLK), 1)
    kj = q0 - ATTN_WINDOW + c
    ok = (c >= r) & (c <= r + 2 * ATTN_WINDOW) & (kj >= 0) & (kj < s)
    return jnp.where(ok, 0.0, NEG_INF)


def _pair_operands(t):
    lane = lax.broadcasted_iota(jnp.int32, (1, 128), 1)
    first = lane < HEAD_DIM
    zero = jnp.zeros((KBLK, 128), BF16)
    out = []
    for j in range(2):
        slab = t[:, 128 * j:128 * (j + 1)]
        turned = pltpu.bitcast(pltpu.roll(pltpu.bitcast(slab, jnp.uint32), HEAD_DIM, 1), BF16)
        for own_first in (True, False):
            top = jnp.where(first, slab if own_first else turned, zero)
            bottom = jnp.where(first, zero, turned if own_first else slab)
            out.append(jnp.concatenate([top, bottom], axis=0))
    return out


def _attn_fwd(q, kpad, vpad, sink):
    s = q.shape[0]

    def body(sink_ref, q_ref, k_ref, v_ref, o_ref, lse_ref):
        i = pl.program_id(0)
        lane = lax.broadcasted_iota(jnp.int32, (1, 128), 1)
        for b in range(TOK // QBLK):
            rows = slice(b * QBLK, (b + 1) * QBLK)
            start = pl.multiple_of(i * TOK + b * QBLK, QBLK)
            k_bd = _pair_operands(k_ref[pl.ds(start, KBLK), :])
            v_bd = _pair_operands(v_ref[pl.ds(start, KBLK), :])
            bias = _block_bias(i * TOK + b * QBLK, s)
            pairs = range(N_HEADS // 2)
            sc4 = [_dot_nt(jnp.concatenate([q_ref[rows, 256 * g:256 * g + 128], q_ref[rows, 256 * g + 128:256 * (g + 1)]],
                                           axis=0), k_bd[g]) for g in range(4)]
            sc2 = [sc4[m // 2][(m % 2) * QBLK:(m % 2 + 1) * QBLK] for m in pairs]
            scs = [sc2[h // 2][:, (h % 2) * KBLK:(h % 2 + 1) * KBLK] + bias for h in range(N_HEADS)]
            ms = [jnp.maximum(jnp.max(scs[h], axis=-1, keepdims=True), sink_ref[h]) for h in range(N_HEADS)]
            es = [jnp.exp(scs[h] - ms[h]) for h in range(N_HEADS)]
            dens = [jnp.sum(es[h], axis=-1, keepdims=True) + jnp.exp(sink_ref[h] - ms[h]) for h in range(N_HEADS)]
            first = lane < HEAD_DIM
            e2 = [jnp.concatenate([es[2 * m].astype(BF16), es[2 * m + 1].astype(BF16)], axis=1) for m in pairs]
            o4 = [_dot(jnp.concatenate([e2[2 * g], e2[2 * g + 1]], axis=0), v_bd[g]) for g in range(4)]
            outs = [o4[m // 2][(m % 2) * QBLK:(m % 2 + 1) * QBLK]
                    * jnp.where(first, 1.0 / dens[2 * m], 1.0 / dens[2 * m + 1]) for m in pairs]
            o_ref[rows, :] = jnp.concatenate(outs, axis=1).astype(BF16)
            lse = jnp.zeros((QBLK, 128), F32)
            for h in range(N_HEADS):
                lse = lse + jnp.where(lane == h, ms[h] + jnp.log(dens[h]), 0.0)
            lse_ref[rows, :] = lse

    return pl.pallas_call(
        body, name="attn_fwd", grid=(s // TOK,),
        in_specs=[pl.BlockSpec(memory_space=pltpu.SMEM), _rows(TOK, 1024), _whole(kpad.shape), _whole(vpad.shape)],
        out_specs=[_rows(TOK, 1024), _rows(TOK, 128)],
        out_shape=[jax.ShapeDtypeStruct((s, 1024), BF16), jax.ShapeDtypeStruct((s, 128), F32)],
        compiler_params=_params(),
    )(sink, q, kpad, vpad)


def _tail(x1, o, gate, target, w_out1, gf):
    s = x1.shape[0]

    def body(x1_ref, o_ref, gate_ref, t_ref, w_ref, gf_ref, y1_ref, dx2_ref, do_ref, dgate_ref, loss_ref, gfn_ref,
             dx2h_ref):
        i = pl.program_id(0)

        @pl.when(i == 0)
        def _():
            loss_ref[...] = jnp.zeros_like(loss_ref)
            gfn_ref[...] = jnp.zeros_like(gfn_ref)

        halves = [slice(r * TOK, (r + 1) * TOK) for r in range(tok // TOK)]
        gf = gf_ref[...]
        gs = [gate_ref[rows, :].astype(F32) for rows in halves]
        sgs = [_sigmoid(g) for g in gs]
        sils = [g * sg for g, sg in zip(gs, sgs)]
        os_ = [o_ref[rows, :].astype(F32) for rows in halves]
        y1s = [(o * sil).astype(BF16) for o, sil in zip(os_, sils)]
        for rows, y1 in zip(halves, y1s):
            y1_ref[rows, :] = y1
        x2s = [x1_ref[rows, :] + _dot(y1, w_ref[...]) for rows, y1 in zip(halves, y1s)]
        dx2hs = []
        for rows, x2 in zip(halves, x2s):
            r, xh, out = _rms_fwd(x2, gf)
            diff = out - t_ref[rows, :]
            loss_ref[...] += jnp.sum(diff * diff, axis=0, keepdims=True) * (0.5 / D_MODEL)
            dout = diff * (1.0 / D_MODEL)
            gfn_ref[...] += jnp.sum(dout * xh, axis=0, keepdims=True)
            dx2 = _rms_bwd(dout, gf, r, xh)
            dx2_ref[rows, :] = dx2
            dx2hs.append(dx2.astype(BF16))
            dx2h_ref[rows, :] = dx2hs[-1]
        dy1s = [_dot_nt(dx2h, w_ref[...]) for dx2h in dx2hs]
        for rows, dy1, sil, o, sg, g in zip(halves, dy1s, sils, os_, sgs, gs):
            do_ref[rows, :] = (dy1 * sil).astype(BF16)
            dgate_ref[rows, :] = (dy1 * o * (sg * (1.0 + g * (1.0 - sg)))).astype(BF16)

    tok = min(s, 2 * TOK)
    row = _rows(tok, 1024)
    acc = _whole((1, 1024))
    return pl.pallas_call(
        body, name="tail", grid=(s // tok,),
        in_specs=[row, row, row, row, _whole(w_out1.shape), acc],
        out_specs=[row, row, row, row, acc, acc, row],
        out_shape=[jax.ShapeDtypeStruct((s, 1024), BF16), jax.ShapeDtypeStruct((s, 1024), F32),
                   jax.ShapeDtypeStruct((s, 1024), BF16), jax.ShapeDtypeStruct((s, 1024), BF16),
                   jax.ShapeDtypeStruct((1, 1024), F32), jax.ShapeDtypeStruct((1, 1024), F32),
                   jax.ShapeDtypeStruct((s, 1024), BF16)],
        compiler_params=_params(),
    )(x1, o, gate, target, w_out1, gf)


def _attn_bwd(q, kpad, vpad, sink, o, lse, do, rope):
    s = q.shape[0]
    pad_t = (kpad.shape[1], kpad.shape[0])

    def body(sink_ref, q_ref, k_ref, v_ref, o_ref, lse_ref, do_ref, c_ref, lo_ref, hi_ref,
             dq_ref, dk_ref, dv_ref, ds_ref):
        i = pl.program_id(0)

        @pl.when(i == 0)
        def _():
            dk_ref[...] = jnp.zeros_like(dk_ref)
            dv_ref[...] = jnp.zeros_like(dv_ref)
            ds_ref[...] = jnp.zeros_like(ds_ref)

        lane = lax.broadcasted_iota(jnp.int32, (1, 128), 1)
        dsink = jnp.zeros((1, 128), F32)
        for b in range(TOK // QBLK):
            rows = slice(b * QBLK, (b + 1) * QBLK)
            start = pl.multiple_of(i * TOK + b * QBLK, QBLK)
            k_bd = _pair_operands(k_ref[pl.ds(start, KBLK), :])
            v_bd = _pair_operands(v_ref[pl.ds(start, KBLK), :])
            bias = _block_bias(i * TOK + b * QBLK, s)
            lanes_of = (lane < HEAD_DIM, lane >= HEAD_DIM)
            half = lambda t, j: t[:, j * KBLK:(j + 1) * KBLK]
            dqs, dks, dvs = [], [], []
            for g in range(4):
                pairs = (2 * g, 2 * g + 1)
                qs = {m: q_ref[rows, 128 * m:128 * (m + 1)] for m in pairs}
                dos = {m: do_ref[rows, 128 * m:128 * (m + 1)] for m in pairs}
                lses = {h: lse_ref[rows, h:h + 1] for h in range(4 * g, 4 * g + 4)}
                stacked = lambda parts: jnp.concatenate([parts[m] for m in pairs], axis=0)
                unstack = lambda t: {m: t[j * QBLK:(j + 1) * QBLK] for j, m in enumerate(pairs)}
                sc2 = unstack(_dot_nt(stacked(qs), k_bd[g]))
                ps = {2 * m + j: jnp.exp(half(sc2[m], j) + bias - lses[2 * m + j]) for m in pairs for j in range(2)}
                prods = {m: dos[m].astype(F32) * o_ref[rows, 128 * m:128 * (m + 1)].astype(F32) for m in pairs}
                deltas = {2 * m + j: jnp.sum(jnp.where(lanes_of[j], prods[m], 0.0), axis=-1, keepdims=True)
                          for m in pairs for j in range(2)}
                for h in range(4 * g, 4 * g + 4):
                    dsink = dsink + jnp.where(
                        lane == h, -jnp.sum(jnp.exp(sink_ref[h] - lses[h]) * deltas[h], axis=0, keepdims=True), 0.0)
                dp2 = unstack(_dot_nt(stacked(dos), v_bd[g]))
                ds2 = {m: jnp.concatenate(
                    [(ps[2 * m + j] * (half(dp2[m], j) - deltas[2 * m + j])).astype(BF16) for j in range(2)], axis=1)
                    for m in pairs}
                p2 = {m: jnp.concatenate([ps[2 * m].astype(BF16), ps[2 * m + 1].astype(BF16)], axis=1) for m in pairs}
                diag = lambda t: t[0:HEAD_DIM, 0:KBLK] + t[HEAD_DIM:128, KBLK:2 * KBLK]
                dvs.append(diag(_dot_tn(dos[pairs[0]], p2[pairs[0]])) + diag(_dot_tn(dos[pairs[1]], p2[pairs[1]])))
                dks.append(diag(_dot_tn(qs[pairs[0]], ds2[pairs[0]])) + diag(_dot_tn(qs[pairs[1]], ds2[pairs[1]])))
                dq2 = unstack(_dot(stacked(ds2), k_bd[g]) * Q_SCALE)
                dqs += [dq2[m] for m in pairs]
            dq = jnp.concatenate(dqs, axis=1)
            dq_ref[rows, :] = _rope_t(dq, c_ref[rows, :], lo_ref[rows, :], hi_ref[rows, :]).astype(BF16)
            dk_ref[:, pl.ds(start, KBLK)] += jnp.concatenate(dks, axis=0)
            dv_ref[:, pl.ds(start, KBLK)] += jnp.concatenate(dvs, axis=0)
        ds_ref[...] += dsink

    row = _rows(TOK, 1024)
    tab = _rows(TOK, 128)
    pad = _whole(kpad.shape)
    return pl.pallas_call(
        body, name="attn_bwd", grid=(s // TOK,),
        in_specs=[pl.BlockSpec(memory_space=pltpu.SMEM), row, pad, pad, row, tab, row, tab, tab, tab],
        out_specs=[row, _whole(pad_t), _whole(pad_t), _whole((1, 128))],
        out_shape=[jax.ShapeDtypeStruct((s, 1024), BF16), jax.ShapeDtypeStruct(pad_t, F32),
                   jax.ShapeDtypeStruct(pad_t, F32), jax.ShapeDtypeStruct((1, 128), F32)],
        compiler_params=_params(),
    )(sink, q, kpad, vpad, o, lse, do, *rope)


def _in_proj1_bwd(dq, dk_t, dv_t, dgate, x1, dx2, g1, w_in1, rope):
    s = x1.shape[0]
    tok = min(s, 2 * TOK)
    n_sub = tok // ATTN_WINDOW

    def body(*refs):
        dq_ref = refs[0]
        dk_refs, dv_refs = refs[1:1 + n_sub], refs[1 + n_sub:1 + 2 * n_sub]
        (dgate_ref, x1_ref, dx2_ref, g_ref, w_ref, c_ref, lo_ref, hi_ref,
         dz_ref, dx1_ref, gn_ref, dx1h_ref) = refs[1 + 2 * n_sub:]

        @pl.when(pl.program_id(0) == 0)
        def _():
            gn_ref[...] = jnp.zeros_like(gn_ref)

        halves = [slice(r * TOK, (r + 1) * TOK) for r in range(tok // TOK)]
        per = TOK // ATTN_WINDOW
        g = g_ref[...]
        for r, rows in enumerate(halves):
            dk = jnp.concatenate([ref[...] for ref in dk_refs[r * per:(r + 1) * per]], axis=1).T
            dv = jnp.concatenate([ref[...] for ref in dv_refs[r * per:(r + 1) * per]], axis=1).T
            dz_ref[rows, 0:1024] = dq_ref[rows, :]
            dz_ref[rows, 1024:1280] = _rope_t(dk, c_ref[rows, :], lo_ref[rows, :], hi_ref[rows, :]).astype(BF16)
            dz_ref[rows, 1280:1536] = dv.astype(BF16)
            dz_ref[rows, 1536:2560] = dgate_ref[rows, :]
        dhs = [_dot_nt(dz_ref[rows, :], w_ref[...]) for rows in halves]
        for rows, dh in zip(halves, dhs):
            r, xh, _ = _rms_fwd(x1_ref[rows, :], g)
            gn_ref[...] += jnp.sum(dh * xh, axis=0, keepdims=True)
            dx1 = dx2_ref[rows, :] + _rms_bwd(dh, g, r, xh)
            dx1_ref[rows, :] = dx1
            dx1h_ref[rows, :] = dx1.astype(BF16)

    row = _rows(tok, 1024)
    subs = [pl.BlockSpec((256, ATTN_WINDOW), lambda i, j=j: (0, n_sub * i + 1 + j)) for j in range(n_sub)]
    tab = _rows(tok, 128)
    acc = _whole((1, 1024))
    return pl.pallas_call(
        body, name="in_proj1_bwd", grid=(s // tok,),
        in_specs=[row] + subs + subs + [row, row, row, acc, _whole(w_in1.shape), tab, tab, tab],
        out_specs=[_rows(tok, 2560), row, acc, row],
        out_shape=[jax.ShapeDtypeStruct((s, 2560), BF16), jax.ShapeDtypeStruct((s, 1024), F32),
                   jax.ShapeDtypeStruct((1, 1024), F32), jax.ShapeDtypeStruct((s, 1024), BF16)],
        compiler_params=_params(),
    )(dq, *[dk_t] * n_sub, *[dv_t] * n_sub, dgate, x1, dx2, g1, w_in1, *rope)


def _mix0_bwd(dx1, z0, w_out0, gv, ws, bsx, wg, scale, band, rider):
    s = dx1.shape[0]
    n_tiles = s // TOK

    def body(dx1h_ref, z_ref, zp_ref, zn_ref, wout_ref, gv_ref, ws_ref, bsx_ref, wg_ref, sc_ref, band_ref,
             dz_ref, dpn_ref, dws_ref, dbs_ref, dgv_ref, dsc_ref, dwg_ref):
        i = pl.program_id(0)
        dz_ref[:, 3072:4096] = jnp.zeros((TOK, 1024), BF16)

        @pl.when(i == 0)
        def _():
            for ref in (dws_ref, dbs_ref, dgv_ref, dsc_ref, dwg_ref):
                ref[...] = jnp.zeros_like(ref)

        dcat = _dot_nt(dx1h_ref[...], wout_ref[...])
        dya = dcat[:, 0:1024]
        dyb = dcat[:, 1024:2048]

        au = z_ref[:, 0:1024]
        av = z_ref[:, 1024:2048]
        ag = z_ref[:, 2048:3072]
        gv = gv_ref[...]
        u, du = _gelu_and_grad(au)
        v1, dv1 = _gelu_and_grad(av)
        rv, vh, v2, mixed = _mixer_a(v1, gv, ws_ref, bsx_ref[...])
        sg = _sigmoid(ag)
        sil = ag * sg
        dz_ref[:, 2048:3072] = (dya * u * mixed * (sg * (1.0 + ag * (1.0 - sg)))).astype(BF16)
        dz_ref[:, 0:1024] = (dya * mixed * sil * du).astype(BF16)
        dmixed = dya * u * sil
        lane = lax.broadcasted_iota(jnp.int32, (1, 128), 1)
        dm16 = dmixed.astype(BF16)
        dv2_rows = []
        for c in range(TOK // CHUNK):
            rows = slice(c * CHUNK, (c + 1) * CHUNK)
            cols_out = []
            for h in range(4):
                cols = slice(h * 256, (h + 1) * 256)
                dws_ref[h] += _dot_nt(dm16[rows, cols], v2[rows, cols])
                dbs_ref[...] += jnp.where(lane == h, jnp.sum(dmixed[rows, cols], axis=-1, keepdims=True), 0.0)
                cols_out.append(_dot_tn(ws_ref[h], dm16[rows, cols]))
            dv2_rows.append(jnp.concatenate(cols_out, axis=1))
        dv2 = jnp.concatenate(dv2_rows, axis=0)
        dgv_ref[...] += jnp.sum(dv2 * vh, axis=0, keepdims=True)
        dz_ref[:, 1024:2048] = (_rms_bwd(dv2, gv, rv, vh) * dv1).astype(BF16)

        bx = z_ref[:, 3072:4096]
        bg = z_ref[:, 4096:5120]
        counts = _window_counts(i, TOK, s)
        halo = _with_halo(i, n_tiles, zp_ref, bx, zn_ref).astype(BF16)
        ps = [p.astype(BF16) for p in _mixer_b_pooled(bx, halo, band_ref, counts)]
        pw = jnp.concatenate([_dot(ps[g], wg_ref[g]) for g in range(4)], axis=1)
        sgb = _sigmoid(bg)
        sc = sc_ref[...]
        dz_ref[:, 4096:5120] = (dyb * pw * sc * (sgb * (1.0 + bg * (1.0 - sgb)))).astype(BF16)
        dys = dyb * (bg * sgb)
        dsc_ref[...] += jnp.sum(dys * pw, axis=0, keepdims=True)
        dpw = (dys * sc).astype(BF16)
        for g in range(4):
            cols = slice(g * 256, (g + 1) * 256)
            dwg_ref[g] += _dot_tn(ps[g], dpw[:, cols])
            dpn_ref[:, cols] = (_dot_nt(dpw[:, cols], wg_ref[g]) / counts[g]).astype(BF16)

    prev, nxt = _halo_specs(s, 3)
    row = _rows(TOK, 1024)
    vec = _whole((1, 1024))
    return pl.pallas_call(
        rider.carried_by(body, 11, 7, n_tiles), name="mix0_bwd", grid=(n_tiles,),
        in_specs=[row, _rows(TOK, 5120), prev, nxt, _whole(w_out0.shape), vec, _whole(ws.shape),
                  _whole(bsx.shape), _whole(wg.shape), vec, _whole(band.shape)] + rider.in_specs,
        out_specs=[_rows(TOK, 5120), row, _whole((4, 128, 128)), _whole((128, 128)), vec, vec,
                   _whole((4, 256, 256))] + rider.out_specs,
        out_shape=[jax.ShapeDtypeStruct((s, 5120), BF16), jax.ShapeDtypeStruct((s, 1024), BF16),
                   jax.ShapeDtypeStruct((4, 128, 128), F32), jax.ShapeDtypeStruct((128, 128), F32),
                   jax.ShapeDtypeStruct((1, 1024), F32), jax.ShapeDtypeStruct((1, 1024), F32),
                   jax.ShapeDtypeStruct((4, 256, 256), F32)] + rider.out_shape,
        scratch_shapes=rider.scratch,
        compiler_params=_params(),
    )(dx1, z0, z0, z0, w_out0, gv, ws, bsx, wg, scale, band, *rider.parts)


def _fill_pooled_grad(dz0, dpn):
    s = dpn.shape[0]
    tok = min(s, 2 * TOK)
    n_tiles = s // tok
    band_t = _band_matrices(tok)[1]

    def body(dz_in_ref, dpn_ref, dpp_ref, dpx_ref, band_ref, dbx_ref):
        i = pl.program_id(0)
        dpn = dpn_ref[...]
        halo = _with_halo(i, n_tiles, dpp_ref, dpn, dpx_ref)
        counts = _window_counts(i, tok, s)
        for g in range(4):
            cols = slice(g * 256, (g + 1) * 256)
            dbx = _dot(band_ref[g], halo[:, cols]) - dpn[:, cols].astype(F32) * counts[g]
            dbx_ref[:, cols] = dbx.astype(BF16)

    prev, nxt = _halo_specs(s, 0, tok)
    return pl.pallas_call(
        body, name="fill_pooled_grad", grid=(n_tiles,),
        in_specs=[ANY, _rows(tok, 1024), prev, nxt, _whole(band_t.shape)],
        out_specs=pl.BlockSpec((tok, 1024), lambda i: (i, 3)),
        out_shape=jax.ShapeDtypeStruct(dz0.shape, BF16),
        input_output_aliases={0: 0},
        compiler_params=_params(),
    )(dz0, dpn, dpn, dpn, band_t)


def _in_proj0_bwd(dz0, x, dx1, g0, w_in0, rider):
    s = x.shape[0]
    tok = min(s, 2 * TOK)
    n_tiles = s // tok

    def body(dz_ref, x_ref, dx1_ref, g_ref, w_ref, dx_ref, gn_ref):
        i = pl.program_id(0)

        @pl.when(i == 0)
        def _():
            gn_ref[...] = jnp.zeros_like(gn_ref)

        halves = [slice(r * TOK, (r + 1) * TOK) for r in range(tok // TOK)]
        g0v = g_ref[...]
        dhs = [_dot_nt(dz_ref[rows, :], w_ref[...]) for rows in halves]
        for rows, dh in zip(halves, dhs):
            r, xh, _ = _rms_fwd(x_ref[rows, :], g0v)
            gn_ref[...] += jnp.sum(dh * xh, axis=0, keepdims=True)
            dx_ref[rows, :] = dx1_ref[rows, :] + _rms_bwd(dh, g0v, r, xh)

    row = _rows(tok, 1024)
    vec = _whole((1, 1024))
    return pl.pallas_call(
        rider.carried_by(body, 5, 2, n_tiles), name="in_proj0_bwd", grid=(n_tiles,),
        in_specs=[_rows(tok, 5120), row, row, vec, _whole(w_in0.shape)] + rider.in_specs,
        out_specs=[row, vec] + rider.out_specs,
        out_shape=[jax.ShapeDtypeStruct((s, 1024), F32), jax.ShapeDtypeStruct((1, 1024), F32)] + rider.out_shape,
        scratch_shapes=rider.scratch,
        compiler_params=_params(),
    )(dz0, x, dx1, g0, w_in0, *rider.parts)


def _weight_grad(a, b, n_blocks, split, name):
    s, k = a.shape
    n = b.shape[1]
    tn = n // n_blocks
    w = tn // split
    ts = min(s, 1024)

    def body(a_ref, b_ref, o_ref):
        @pl.when(pl.program_id(1) == 0)
        def _():
            o_ref[...] = jnp.zeros_like(o_ref)

        res = _dot_tn(a_ref[...], b_ref[...])
        for q in range(split):
            o_ref[q] += res[:, q * w:(q + 1) * w]

    return pl.pallas_call(
        body, name=name, grid=(n_blocks, s // ts),
        in_specs=[pl.BlockSpec((ts, k), lambda j, t: (t, 0)), pl.BlockSpec((ts, tn), lambda j, t: (t, j))],
        out_specs=pl.BlockSpec((split, k, w), lambda j, t: (j, 0, 0)),
        out_shape=jax.ShapeDtypeStruct((n_blocks * split, k, w), F32),
        compiler_params=_params(),
    )(a, b)


def _row_tile(rows, cols):
    t = rows
    while t * cols * 4 > (1 << 20) and t % 16 == 0:
        t //= 2
    return t


def _add_sibling(where, g, theirs, name):
    _, _, rows, cols = g.shape
    t = _row_tile(rows, cols)

    def body(where_ref, g_ref, t_ref, o_ref):
        o_ref[...] = (g_ref[...] + t_ref[...]).astype(BF16)

    spec = pl.BlockSpec((None, t, cols), lambda s, i, p: (s, i, 0))
    return pl.pallas_call(
        body, name=name, out_shape=jax.ShapeDtypeStruct((4, rows, cols), BF16),
        grid_spec=pltpu.PrefetchScalarGridSpec(
            num_scalar_prefetch=1, grid=(4, rows // t),
            in_specs=[pl.BlockSpec((None, None, t, cols), lambda s, i, p: (s, p[1], i, 0)), spec], out_specs=spec),
        compiler_params=_params())(where, g, theirs)


def _sum_chips(where, g, theirs, slots, name):
    _, _, rows, cols = g.shape
    t = _row_tile(rows, cols)

    def body(where_ref, g_ref, t_ref, s_ref, o_ref):
        me = where_ref[0]
        own = g_ref[...] + t_ref[...]
        acc = jnp.where(me == 0, own, s_ref[0].astype(F32))
        for k in range(1, 4):
            acc = acc + jnp.where(me == k, own, s_ref[k].astype(F32))
        o_ref[...] = acc

    return pl.pallas_call(
        body, name=name, out_shape=jax.ShapeDtypeStruct((rows, cols), F32),
        grid_spec=pltpu.PrefetchScalarGridSpec(
            num_scalar_prefetch=1, grid=(rows // t,),
            in_specs=[pl.BlockSpec((None, None, t, cols), lambda i, p: (p[0], p[1], i, 0)),
                      pl.BlockSpec((None, t, cols), lambda i, p: (p[0], i, 0)),
                      pl.BlockSpec((4, t, cols), lambda i, p: (0, i, 0))],
            out_specs=pl.BlockSpec((t, cols), lambda i, p: (i, 0))),
        compiler_params=_params())(where, g, theirs, slots)


def _adamw_halves(where, w, own, theirs, m, v, name):
    rows, cols = own.shape
    t = _row_tile(rows, cols)
    per = rows // t

    def body(where_ref, w_ref, own_ref, th_ref, m_ref, v_ref, g_ref, d_ref, nm_ref, nv_ref):
        g = jnp.where(pl.program_id(0) == where_ref[1], own_ref[...], th_ref[...])
        g_ref[...] = g
        m2 = ADAM_B1 * m_ref[...] + (1.0 - ADAM_B1) * g
        v2 = ADAM_B2 * v_ref[...] + (1.0 - ADAM_B2) * (g * g)
        m_hat = m2 / (1.0 - ADAM_B1 ** ADAM_STEP)
        v_hat = v2 / (1.0 - ADAM_B2 ** ADAM_STEP)
        d_ref[...] = -ADAM_LR * (m_hat / (jnp.sqrt(v_hat) + ADAM_EPS) + ADAM_WD * w_ref[...])
        nm_ref[...] = m2
        nv_ref[...] = v2

    full = pl.BlockSpec((t, cols), lambda h, i, p: (h * per + i, 0))
    half = pl.BlockSpec((t, cols), lambda h, i, p: (i, 0))
    shp = jax.ShapeDtypeStruct(w.shape, F32)
    return pl.pallas_call(
        body, name=name, out_shape=[shp] * 4,
        grid_spec=pltpu.PrefetchScalarGridSpec(
            num_scalar_prefetch=1, grid=(2, per), in_specs=[full, half, half, full, full], out_specs=[full] * 4),
        compiler_params=_params())(where, w, own, theirs, m, v)


def _place_shard(where, w, cut, name):
    if cut.kind == "cols":
        r, n = cut.full_shape
        blk, grid = (256, n // 4), (r // 256,)
        src_map, dst_map = (lambda i, p: (i, 0)), (lambda i, p: (i, p[0]))
    elif cut.kind == "rows":
        r, n = cut.full_shape
        per = r // 4 // 256
        blk, grid = (256, n), (per,)
        src_map, dst_map = (lambda i, p: (i, 0)), (lambda i, p: (p[0] * per + i, 0))
    else:
        g, r, n = cut.full_shape
        blk, grid = (g, r // 4, n), (1,)
        src_map, dst_map = (lambda i, p: (0, 0, 0)), (lambda i, p: (0, p[0], 0))

    def body(where_ref, w_ref, o_ref):
        o_ref[...] = w_ref[...].astype(BF16)

    return pl.pallas_call(
        body, name=name, out_shape=jax.ShapeDtypeStruct(cut.full_shape, BF16),
        grid_spec=pltpu.PrefetchScalarGridSpec(
            num_scalar_prefetch=1, grid=grid, in_specs=[pl.BlockSpec(blk, src_map)],
            out_specs=pl.BlockSpec(blk, dst_map)),
        compiler_params=_params())(where, w)


def _sum_small(first, second, third):
    rows = second.shape[1]

    def body(a_ref, b_ref, c_ref, o_ref):
        top = a_ref[0] + c_ref[0] + b_ref[0, 0:8]
        rest = b_ref[0, 8:rows]
        for k in range(1, 8):
            top = top + (a_ref[k] + c_ref[k] + b_ref[k, 0:8])
            rest = rest + b_ref[k, 8:rows]
        o_ref[0:8] = top
        o_ref[8:rows] = rest

    return pl.pallas_call(
        body, name="sum_small", in_specs=[_whole(first.shape), _whole(second.shape), _whole(third.shape)],
        out_specs=_whole(second.shape[1:]), out_shape=jax.ShapeDtypeStruct(second.shape[1:], F32),
        compiler_params=_params())(first, second, third)


def _adamw(w, g, m, v, name):
    rows, cols = w.shape
    t = _row_tile(rows, cols)

    def body(w_ref, g_ref, m_ref, v_ref, d_ref, nm_ref, nv_ref):
        g = g_ref[...]
        m2 = ADAM_B1 * m_ref[...] + (1.0 - ADAM_B1) * g
        v2 = ADAM_B2 * v_ref[...] + (1.0 - ADAM_B2) * (g * g)
        m_hat = m2 / (1.0 - ADAM_B1 ** ADAM_STEP)
        v_hat = v2 / (1.0 - ADAM_B2 ** ADAM_STEP)
        d_ref[...] = -ADAM_LR * (m_hat / (jnp.sqrt(v_hat) + ADAM_EPS) + ADAM_WD * w_ref[...])
        nm_ref[...] = m2
        nv_ref[...] = v2

    spec = pl.BlockSpec((t, cols), lambda i: (i, 0))
    shp = jax.ShapeDtypeStruct(w.shape, F32)
    return pl.pallas_call(body, name=name, grid=(rows // t,), in_specs=[spec] * 4, out_specs=[spec] * 3,
                          out_shape=[shp] * 3, compiler_params=_params())(w, g, m, v)


def _place():
    x, y, c = lax.axis_index("x"), lax.axis_index("y"), lax.axis_index("c")
    chips = [(1 - x, y), (x, 1 - y), (1 - x, 1 - y)]
    return x, y, c, chips


class _Sharded:
    def __init__(self, kind, full_shape):
        self.kind = kind
        self.full_shape = full_shape

    def in_full(self, ref, s, h):
        if self.kind == "cols":
            r, n = self.full_shape
            return ref.at[pl.ds(h * (r // 2), r // 2), pl.ds(pl.multiple_of(s * (n // 4), 128), n // 4)]
        if self.kind == "rows":
            r, _ = self.full_shape
            return ref.at[pl.ds(pl.multiple_of(s * (r // 4) + h * (r // 8), 8), r // 8), :]
        g, r, _ = self.full_shape
        return ref.at[pl.ds(h * (g // 2), g // 2), pl.ds(pl.multiple_of(s * (r // 4), 16), r // 4), :]


def _remote(src, dst, send_sem, recv_sem, to):
    return pltpu.make_async_remote_copy(src_ref=src, dst_ref=dst, send_sem=send_sem, recv_sem=recv_sem,
                                        device_id=to, device_id_type=MESH)


def _start_remote(src, dst, send_sem, recv_sem, to):
    cp = _remote(src, dst, send_sem, recv_sem, to)
    cp.start()
    return cp


class _Gather:
    def __init__(self, fulls, cuts):
        n = len(fulls)
        self.fulls, self.cuts = list(fulls), list(cuts)
        self.in_specs = [ANY] * n
        self.out_specs = [ANY] * n
        self.out_shape = [jax.ShapeDtypeStruct(cut.full_shape, BF16) for cut in cuts]
        self.scratch = [pltpu.SemaphoreType.DMA((6 * n,)), pltpu.SemaphoreType.DMA((6 * n,))]

    def _step(self, step, src, out, send_sems, recv_sems):
        n, cuts = len(self.fulls), self.cuts
        x, y, c, chips = _place()
        me = 2 * x + y

        def ends(w, s, h, from_src):
            dst = cuts[w].in_full(out[w], s, h)
            return (cuts[w].in_full(src[w], s, h) if from_src else dst), dst

        for w in range(n):
            for j, chip in enumerate(chips):
                s = 2 * chip[0] + chip[1]
                k, k2 = 3 * w + j, 3 * n + 3 * w + j
                if step == "send":
                    _start_remote(*ends(w, me, c, True), send_sems.at[k], recv_sems.at[k], (*chip, c))
                elif step == "pass_on":
                    _remote(*ends(w, s, c, False), send_sems.at[k], recv_sems.at[k], (x, y, c)).wait_recv()
                    _start_remote(*ends(w, s, c, False), send_sems.at[k2], recv_sems.at[k2], (x, y, 1 - c))
                else:
                    _remote(*ends(w, s, 1 - c, False), send_sems.at[k2], recv_sems.at[k2], (x, y, c)).wait_recv()
                    _remote(*ends(w, me, c, True), send_sems.at[k], recv_sems.at[k], (x, y, c)).wait_send()
                    _remote(*ends(w, s, c, False), send_sems.at[k2], recv_sems.at[k2], (x, y, c)).wait_send()

    def carried_by(self, body, n_in, n_out, n_steps, step_index=lambda: pl.program_id(0)):
        k = len(self.fulls)

        def carrier(*refs):
            ins, src = refs[:n_in], refs[n_in:n_in + k]
            outs, out = refs[n_in + k:n_in + k + n_out], refs[n_in + k + n_out:n_in + 2 * k + n_out]
            sems = refs[n_in + 2 * k + n_out:]
            for step, at in (("send", 0), ("pass_on", 3 * n_steps // 4), ("finish", n_steps - 1)):
                if step == "finish":
                    body(*ins, *outs)

                @pl.when(step_index() == at)
                def _():
                    self._step(step, src, out, *sems)

        return carrier


def _exchange_halves(grads, name):
    n = len(grads)

    def body(*refs):
        g = refs[:n]
        theirs = refs[n:2 * n]
        send_sems, recv_sems = refs[2 * n:]
        x, y, c, _ = _place()
        sends = [_start_remote(g[w].at[:, 1 - c], theirs[w], send_sems.at[w], recv_sems.at[w], (x, y, 1 - c))
                 for w in range(n)]
        for w in range(n):
            _remote(g[w].at[:, 1 - c], theirs[w], send_sems.at[w], recv_sems.at[w], (x, y, c)).wait_recv()
        for cp in sends:
            cp.wait_send()

    return pl.pallas_call(
        body, name=name,
        in_specs=[ANY] * n, out_specs=[ANY] * n,
        out_shape=[jax.ShapeDtypeStruct((4,) + g.shape[2:], F32) for g in grads],
        scratch_shapes=[pltpu.SemaphoreType.DMA((n,)), pltpu.SemaphoreType.DMA((n,))],
        compiler_params=pltpu.CompilerParams(has_side_effects=True),
    )(*grads)


def _gather_small(small_ref, gathered, send_sems, recv_sems, first_sem, local_sem, start):
    x, y, c, _ = _place()
    me = 4 * x + 2 * y + c
    flips = [(fx, fy, fc) for fx in range(2) for fy in range(2) for fc in range(2)][1:]
    own = pltpu.make_async_copy(small_ref, gathered.at[me], local_sem)
    if start:
        own.start()
    else:
        own.wait()
    for k, (fx, fy, fc) in enumerate(flips):
        peer = (x + fx - 2 * x * fx, y + fy - 2 * y * fy, c + fc - 2 * c * fc)
        sems = (send_sems.at[first_sem + k], recv_sems.at[first_sem + k])
        if start:
            _start_remote(small_ref, gathered.at[me], *sems, peer)
        else:
            cp = _remote(small_ref, gathered.at[4 * peer[0] + 2 * peer[1] + peer[2]], *sems, (x, y, c))
            cp.wait_recv()
            cp.wait_send()


class _ScatterRider:
    def __init__(self, parts, small):
        n = len(parts)
        self.n = n
        self.parts = list(parts) + [small]
        self.in_specs = [ANY] * (n + 1)
        self.out_specs = [ANY] * (n + 1)
        self.out_shape = ([jax.ShapeDtypeStruct(a.shape, a.dtype) for a in parts]
                          + [jax.ShapeDtypeStruct((8,) + small.shape, small.dtype)])
        self.scratch = [pltpu.SemaphoreType.DMA((3 * n + 7,)), pltpu.SemaphoreType.DMA((3 * n + 7,)),
                        pltpu.SemaphoreType.DMA]

    def _copies(self, p, out, send_sems, recv_sems, local_sem, start):
        x, y, c, chips = _place()
        me = 2 * x + y
        n = self.n
        _gather_small(p[n], out[n], send_sems, recv_sems, 3 * n, local_sem, start)
        for w in range(n):
            for j, chip in enumerate(chips):
                s = 2 * chip[0] + chip[1]
                if start:
                    _start_remote(p[w].at[s], out[w].at[me], send_sems.at[3 * w + j], recv_sems.at[3 * w + j],
                                  (*chip, c))
                else:
                    cp = _remote(p[w].at[s], out[w].at[s], send_sems.at[3 * w + j], recv_sems.at[3 * w + j],
                                 (x, y, c))
                    cp.wait_recv()
                    cp.wait_send()

    def carried_by(self, body, n_in, n_out, n_tiles):
        k = len(self.parts)

        def carrier(*refs):
            ins, mine = refs[:n_in], refs[n_in:n_in + k]
            outs, slots = refs[n_in + k:n_in + k + n_out], refs[n_in + k + n_out:n_in + 2 * k + n_out]
            sems = refs[n_in + 2 * k + n_out:]

            @pl.when(pl.program_id(0) == 0)
            def _():
                self._copies(mine, slots, *sems, start=True)

            body(*ins, *outs)

            @pl.when(pl.program_id(0) == n_tiles - 1)
            def _():
                self._copies(mine, slots, *sems, start=False)

        return carrier


def _share_halves(halves, small):
    n = len(halves)

    def body(*refs):
        hv = refs[:n]
        small_ref = refs[n]
        out = refs[n + 1:2 * n + 1]
        gathered = refs[2 * n + 1]
        send_sems, recv_sems, local_sem = refs[2 * n + 2:]
        x, y, c, _ = _place()
        sends = [_start_remote(hv[w], out[w], send_sems.at[w], recv_sems.at[w], (x, y, 1 - c)) for w in range(n)]
        _gather_small(small_ref, gathered, send_sems, recv_sems, n, local_sem, True)
        for w in range(n):
            _remote(hv[w], out[w], send_sems.at[w], recv_sems.at[w], (x, y, c)).wait_recv()
        for cp in sends:
            cp.wait_send()
        _gather_small(small_ref, gathered, send_sems, recv_sems, n, local_sem, False)

    return pl.pallas_call(
        body, name="share_halves",
        in_specs=[ANY] * (n + 1), out_specs=[ANY] * (n + 1),
        out_shape=[jax.ShapeDtypeStruct(a.shape, F32) for a in halves] + [jax.ShapeDtypeStruct((8,) + small.shape, F32)],
        scratch_shapes=[pltpu.SemaphoreType.DMA((n + 7,)), pltpu.SemaphoreType.DMA((n + 7,)),
                        pltpu.SemaphoreType.DMA],
        compiler_params=pltpu.CompilerParams(has_side_effects=True),
    )(*halves, small)


SMALL_ROWS = 80


def _pack_small(vecs, ws, bs, sink, extra=None):
    ws = jnp.zeros((64, 1024), F32) if ws is None else ws.reshape(64, 1024)
    bs = jnp.zeros((1, 512), F32) if bs is None else bs.reshape(1, 512)
    sink = jnp.zeros((1, 16), F32) if sink is None else sink.reshape(1, 16)
    extra = jnp.zeros((1, 1024), F32) if extra is None else extra.reshape(1, 1024)
    top = jnp.concatenate(
        [v.reshape(1, 1024) for v in vecs]
        + [jnp.pad(bs, ((0, 0), (0, 512))), jnp.pad(sink, ((0, 0), (0, 1008))), extra], axis=0)
    return jnp.concatenate([top, ws, jnp.zeros((8, 1024), F32)], axis=0)


def _unpack_small(p):
    vecs = [p[k] for k in range(5)]
    return vecs, p[8:72].reshape(4, 128, 128), p[5, :512].reshape(4, 128), p[6, :16]


def kernel(x, norm_0, w_in_0, a_v_norm_0, a_spatial_w_0, a_spatial_b_0, b_group_w_0, b_scale_0, w_out_0, norm_1, w_in_1, sink_1, w_out_1, final_norm, loss_target, m_norm_0, m_w_in_0, m_a_v_norm_0, m_a_spatial_w_0, m_a_spatial_b_0, m_b_group_w_0, m_b_scale_0, m_w_out_0, m_norm_1, m_w_in_1, m_sink_1, m_w_out_1, m_final_norm, v_norm_0, v_w_in_0, v_a_v_norm_0, v_a_spatial_w_0, v_a_spatial_b_0, v_b_group_w_0, v_b_scale_0, v_w_out_0, v_norm_1, v_w_in_1, v_sink_1, v_w_out_1, v_final_norm):
    s = x.shape[1]
    xs = x.reshape(s, D_MODEL)
    target = loss_target.reshape(s, D_MODEL)

    cuts = [_Sharded("cols", (1024, 5120)), _Sharded("rows", (2048, 1024)), _Sharded("cols", (1024, 2560)),
            _Sharded("rows", (1024, 1024)), _Sharded("mid", (4, 256, 256))]
    big_w = [w_in_0, w_out_0, w_in_1, w_out_1, b_group_w_0]
    big_m = [m_w_in_0, m_w_out_0, m_w_in_1, m_w_out_1, m_b_group_w_0]
    big_v = [v_w_in_0, v_w_out_0, v_w_in_1, v_w_out_1, v_b_group_w_0]
    where = jnp.stack([2 * lax.axis_index("x") + lax.axis_index("y"), lax.axis_index("c")]).astype(jnp.int32)
    placed = [_place_shard(where, w, cut, f"place_shard{k}") for k, (w, cut) in enumerate(zip(big_w, cuts))]
    cx, cy = lax.axis_index("x"), lax.axis_index("y")
    order = jnp.stack([2 * cx + cy, 2 * (1 - cx) + cy, 2 * cx + 1 - cy, 2 * (1 - cx) + 1 - cy]).astype(jnp.int32)

    row = lambda v: v.reshape(1, 1024)
    ws16 = a_spatial_w_0.astype(BF16)
    bsx = jnp.repeat(a_spatial_b_0.T, 256, axis=1)
    band = _band_matrices(TOK)[0]
    rope = _rope_tables(s)

    h0, z0, w_in0 = _in_proj0_own(order, xs, row(norm_0), w_in_0.astype(BF16), _Gather(placed[:1], cuts[:1]))
    z0, w_out0, wg = _in_proj0_rest(order, h0, w_in0, z0, _Gather([placed[1], placed[4]], [cuts[1], cuts[4]]))
    cat, x1, w_in1, w_out1 = _mix0_fwd(xs, z0, w_out0, row(a_v_norm_0), ws16, bsx, wg, row(b_scale_0), band,
                                       _Gather(placed[2:4], cuts[2:4]))
    h1, q, k, v, gate = _in_proj1(x1, row(norm_1), w_in1, rope)
    kpad = jnp.pad(k, ((ATTN_WINDOW, ATTN_WINDOW), (0, 0)))
    vpad = jnp.pad(v, ((ATTN_WINDOW, ATTN_WINDOW), (0, 0)))
    o, lse = _attn_fwd(q, kpad, vpad, sink_1)
    y1, dx2, do, dgate, loss_lanes, g_final, dx2h = _tail(x1, o, gate, target, w_out1, row(final_norm))

    dq, dkpad, dvpad, dsink = _attn_bwd(q, kpad, vpad, sink_1, o, lse, do, rope)
    dz1, dx1, g_norm1, dx1h = _in_proj1_bwd(dq, dkpad, dvpad, dgate, x1, dx2, row(norm_1), w_in1, rope)

    g_w_in1 = _weight_grad(h1, dz1, 2, 2, "grad_w_in1").reshape(4, 2, 512, 640)
    g_w_out1 = _weight_grad(y1, dx2h, 1, 1, "grad_w_out1").reshape(4, 2, 128, 1024)
    g_w_out0 = _weight_grad(cat, dx1h, 1, 1, "grad_w_out0").reshape(4, 2, 256, 1024)
    first = [g_w_out0, g_w_in1, g_w_out1]
    theirs1 = _exchange_halves(first, "exchange_halves1")
    parts1 = [_add_sibling(where, g, t, f"add_sibling1_{k}") for k, (g, t) in enumerate(zip(first, theirs1))]
    zero = jnp.zeros((1024,), F32)
    small1 = _pack_small([zero, zero, zero, g_norm1, g_final], None, None, dsink[0, :16])[:8]
    dz0, dpn, d_ws, d_bs, d_gv, d_scale, d_wg, *slots1, small1_all = _mix0_bwd(
        dx1h, z0, w_out0, row(a_v_norm_0), ws16, bsx, wg, row(b_scale_0), band, _ScatterRider(parts1, small1))
    dz0 = _fill_pooled_grad(dz0, dpn)
    g_w_in0 = _weight_grad(h0, dz0, 4, 1, "grad_w_in0").reshape(4, 2, 512, 1280)
    g_wg = d_wg.reshape(2, 2, 4, 64, 256).transpose(2, 0, 1, 3, 4).reshape(4, 2, 128, 256)
    second = [g_w_in0, g_wg]
    theirs2 = _exchange_halves(second, "exchange_halves2")
    parts2 = [_add_sibling(where, g, t, f"add_sibling2_{k}") for k, (g, t) in enumerate(zip(second, theirs2))]
    small2 = _pack_small([zero, d_gv, d_scale, zero, zero], d_ws, d_bs[:, :4].T, None)
    grad_x, g_norm0, *slots2, small2_all = _in_proj0_bwd(
        dz0, xs, dx1, row(norm_0), w_in0, _ScatterRider(parts2, small2))

    grads = [g_w_in0, g_w_out0, g_w_in1, g_w_out1, g_wg]
    theirs = [theirs2[0], theirs1[0], theirs1[1], theirs1[2], theirs2[1]]
    slots = [slots2[0], slots1[0], slots1[1], slots1[2], slots2[1]]
    n = len(grads)
    reduced = [_sum_chips(where, grads[w], theirs[w], slots[w], f"sum_chips{w}") for w in range(n)]
    small3 = _pack_small([g_norm0, zero, zero, zero, zero], None, None, None, loss_lanes)[:8]
    *from_sibling, small3_all = _share_halves(reduced, small3)

    out_g, out_d, out_m, out_v = {}, {}, {}, {}
    names = ["w_in_0", "w_out_0", "w_in_1", "w_out_1", "b_group_w_0"]
    for w in range(n):
        shape = big_w[w].shape
        two_d = (-1, shape[-1])
        outs = _adamw_halves(where, big_w[w].reshape(two_d), reduced[w], from_sibling[w], big_m[w].reshape(two_d),
                             big_v[w].reshape(two_d), f"adamw{w}")
        out_g[names[w]], out_d[names[w]], out_m[names[w]], out_v[names[w]] = (a.reshape(shape) for a in outs)

    g_small = _sum_small(small1_all, small2_all, small3_all)
    small_names = ["norm_0", "a_v_norm_0", "b_scale_0", "norm_1", "final_norm"]
    pack = lambda vecs, ws_, bs_, sk: _pack_small(vecs, ws_, bs_, sk)
    w_small = pack([norm_0, a_v_norm_0, b_scale_0, norm_1, final_norm], a_spatial_w_0, a_spatial_b_0, sink_1)
    m_small = pack([m_norm_0, m_a_v_norm_0, m_b_scale_0, m_norm_1, m_final_norm], m_a_spatial_w_0,
                   m_a_spatial_b_0, m_sink_1)
    v_small = pack([v_norm_0, v_a_v_norm_0, v_b_scale_0, v_norm_1, v_final_norm], v_a_spatial_w_0,
                   v_a_spatial_b_0, v_sink_1)
    d_small, nm_small, nv_small = _adamw(w_small, g_small, m_small, v_small, "adamw_small")
    for store, packed in ((out_g, g_small), (out_d, d_small), (out_m, nm_small), (out_v, nv_small)):
        vecs, ws_, bs_, sk = _unpack_small(packed)
        for name, vec in zip(small_names, vecs):
            store[name] = vec
        store["a_spatial_w_0"], store["a_spatial_b_0"], store["sink_1"] = ws_, bs_, sk

    loss = jnp.sum(g_small[7])
    order = ["norm_0", "w_in_0", "a_v_norm_0", "a_spatial_w_0", "a_spatial_b_0", "b_group_w_0", "b_scale_0",
             "w_out_0", "norm_1", "w_in_1", "sink_1", "w_out_1", "final_norm"]
    return (loss, grad_x.reshape(1, s, D_MODEL), *[out_g[k] for k in order], *[out_d[k] for k in order],
            *[out_m[k] for k in order], *[out_v[k] for k in order])
```

```python
import functools

import numpy as np
import jax
import jax.numpy as jnp
from jax import lax
from jax.experimental import pallas as pl
from jax.experimental.pallas import tpu as pltpu

F32 = jnp.float32
BF16 = jnp.bfloat16
MESH = pl.DeviceIdType.MESH

D_MODEL = 1024
EPS = 1e-6
NEG_INF = -1e30
CHUNK = 128
POOL_WINDOWS = (2, 4, 8, 16)
HALO = 16
N_HEADS = 16
HEAD_DIM = 64
ATTN_WINDOW = 128
ROPE_THETA = 500000.0
ROT_DIM = 16
ADAM_LR = 0.001
ADAM_B1 = 0.9
ADAM_B2 = 0.999
ADAM_EPS = 1e-08
ADAM_WD = 0.01
ADAM_STEP = 10

TOK = 256
VMEM_LIMIT = 56 * 1024 * 1024


def _params(**kw):
    return pltpu.CompilerParams(vmem_limit_bytes=VMEM_LIMIT, **kw)


def _whole(shape):
    nd = len(shape)
    return pl.BlockSpec(shape, lambda *_: (0,) * nd)


def _rows(t, n):
    return pl.BlockSpec((t, n), lambda i: (i, 0))


ANY = pl.BlockSpec(memory_space=pl.ANY)

_G0 = 0.7978845608028654
_G1 = 0.044715


def _gelu(x):
    return 0.5 * x * (1.0 + jnp.tanh(_G0 * (x + _G1 * x * x * x)))


def _gelu_and_grad(x):
    x2 = x * x
    t = jnp.tanh(_G0 * (x + _G1 * x2 * x))
    half = 0.5 * (1.0 + t)
    return x * half, half + 0.5 * x * (1.0 - t * t) * (_G0 * (1.0 + 3.0 * _G1 * x2))


def _sigmoid(x):
    return 1.0 / (1.0 + jnp.exp(-x))


def _dot(a, b):
    return jnp.dot(a, b, preferred_element_type=F32)


def _dot_nt(a, b):
    return lax.dot_general(a, b, (((1,), (1,)), ((), ())), preferred_element_type=F32)


def _dot_tn(a, b):
    return lax.dot_general(a, b, (((0,), (0,)), ((), ())), preferred_element_type=F32)


def _rms_fwd(x, g):
    r = lax.rsqrt(jnp.mean(x * x, axis=-1, keepdims=True) + EPS)
    xh = x * r
    return r, xh, xh * g


def _rms_bwd(dy, g, r, xh):
    dxh = dy * g
    return r * (dxh - xh * jnp.mean(dxh * xh, axis=-1, keepdims=True))


def _band_matrices(t):
    r = np.arange(t)[:, None]
    j = np.arange(t + 2 * HALO)[None, :]
    fwd, bwd = [], []
    for w in POOL_WINDOWS:
        d = j - r - HALO
        fwd.append((d >= -(w // 2)) & (d < w // 2))
        bwd.append((d >= -(w // 2) + 1) & (d <= w // 2))
    return (jnp.asarray(np.stack(fwd), BF16), jnp.asarray(np.stack(bwd), BF16))


def _window_counts(i, t, s):
    tok = i * t + lax.broadcasted_iota(jnp.int32, (t, 1), 0)
    out = []
    for w in POOL_WINDOWS:
        cnt = jnp.minimum(tok + w // 2, s) - jnp.maximum(tok - w // 2, 0)
        out.append(cnt.astype(F32))
    return out


def _rope_tables(s):
    inv = np.float32(ROPE_THETA) ** (-np.arange(0, ROT_DIM, 2, dtype=np.float32) / np.float32(ROT_DIM))
    ang = np.arange(s, dtype=np.float32)[:, None] * inv.astype(np.float32)[None, :]
    cos, sin = np.cos(ang).astype(np.float32), np.sin(ang).astype(np.float32)
    z8 = np.zeros((s, 8), np.float32)
    z48 = np.zeros((s, HEAD_DIM - ROT_DIM), np.float32)
    c = np.concatenate([cos, cos, np.ones((s, HEAD_DIM - ROT_DIM), np.float32)], axis=1)
    s_lo = np.concatenate([z8, sin, z48], axis=1)
    s_hi = np.concatenate([-sin, z8, z48], axis=1)
    return tuple(jnp.asarray(np.concatenate([a, a], axis=1)) for a in (c, s_lo, s_hi))


def _rope(x, c, s_lo, s_hi):
    n = x.shape[1]
    reps = n // 128
    c, s_lo, s_hi = (jnp.tile(a, (1, reps)) for a in (c, s_lo, s_hi))
    return x * c + pltpu.roll(x, 8, 1) * s_lo + pltpu.roll(x, n - 8, 1) * s_hi


def _rope_t(dx, c, s_lo, s_hi):
    n = dx.shape[1]
    reps = n // 128
    c, s_lo, s_hi = (jnp.tile(a, (1, reps)) for a in (c, s_lo, s_hi))
    return dx * c + pltpu.roll(dx * s_lo, n - 8, 1) + pltpu.roll(dx * s_hi, 8, 1)


def _in_proj0_own(order, x, g0, w_own, rider):
    s = x.shape[0]
    n = w_own.shape[1]

    def body(order_ref, x_ref, g_ref, w_ref, h_ref, z_ref):
        _, _, h = _rms_fwd(x_ref[...], g_ref[...])
        h = h.astype(BF16)
        h_ref[...] = h
        z_ref[...] = _dot(h, w_ref[...]).astype(BF16)

    return pl.pallas_call(
        rider.carried_by(body, 4, 2, s // TOK), name="in_proj0_own",
        grid_spec=pltpu.PrefetchScalarGridSpec(
            num_scalar_prefetch=1, grid=(s // TOK,),
            in_specs=[pl.BlockSpec((TOK, D_MODEL), lambda i, o: (i, 0)), pl.BlockSpec((1, D_MODEL), lambda i, o: (0, 0)),
                      pl.BlockSpec(w_own.shape, lambda i, o: (0, 0))] + rider.in_specs,
            out_specs=[pl.BlockSpec((TOK, D_MODEL), lambda i, o: (i, 0)),
                       pl.BlockSpec((TOK, n), lambda i, o: (i, o[0]))] + rider.out_specs,
            scratch_shapes=rider.scratch),
        out_shape=[jax.ShapeDtypeStruct((s, D_MODEL), BF16), jax.ShapeDtypeStruct((s, 4 * n), BF16)] + rider.out_shape,
        input_output_aliases={4 + j: 2 + j for j in range(len(rider.fulls))},
        compiler_params=_params(),
    )(order, x, g0, w_own, *rider.fulls)


def _in_proj0_rest(order, h0, w_in0, z0, rider):
    s = h0.shape[0]
    tok = min(s, 4 * TOK)
    n_tiles = s // tok
    n = w_in0.shape[1] // 4

    def body(order_ref, h_ref, w_ref, z_in_ref, z_ref):
        z_ref[...] = _dot(h_ref[...], w_ref[...]).astype(BF16)

    return pl.pallas_call(
        rider.carried_by(body, 4, 1, 3 * n_tiles, lambda: pl.program_id(0) * n_tiles + pl.program_id(1)),
        name="in_proj0_rest",
        grid_spec=pltpu.PrefetchScalarGridSpec(
            num_scalar_prefetch=1, grid=(3, n_tiles),
            in_specs=[pl.BlockSpec((tok, D_MODEL), lambda k, i, o: (i, 0)),
                      pl.BlockSpec((w_in0.shape[0], n), lambda k, i, o: (0, o[1 + k])), ANY] + rider.in_specs,
            out_specs=[pl.BlockSpec((tok, n), lambda k, i, o: (i, o[1 + k]))] + rider.out_specs,
            scratch_shapes=rider.scratch),
        out_shape=[jax.ShapeDtypeStruct(z0.shape, BF16)] + rider.out_shape,
        input_output_aliases={3: 0, **{4 + j: 1 + j for j in range(len(rider.fulls))}},
        compiler_params=_params(),
    )(order, h0, w_in0, z0, *rider.fulls)


def _halo_specs(s, col_block, tok=TOK):
    per = tok // HALO
    last = s // HALO - 1
    prev = pl.BlockSpec((HALO, 1024), lambda i: (jnp.maximum(i * per - 1, 0), col_block))
    nxt = pl.BlockSpec((HALO, 1024), lambda i: (jnp.minimum((i + 1) * per, last), col_block))
    return prev, nxt


def _with_halo(i, n_tiles, prev_ref, cur, next_ref):
    prev = prev_ref[...]
    nxt = next_ref[...]
    prev = jnp.where(i > 0, prev, jnp.zeros_like(prev))
    nxt = jnp.where(i < n_tiles - 1, nxt, jnp.zeros_like(nxt))
    return jnp.concatenate([prev, cur, nxt], axis=0)


def _mixer_a(v1, gv, ws_ref, bsx):
    rv, vh, v2 = _rms_fwd(v1, gv)
    v2 = v2.astype(BF16)
    rows = []
    for c in range(v1.shape[0] // CHUNK):
        cols = [_dot(ws_ref[h], v2[c * CHUNK:(c + 1) * CHUNK, h * 256:(h + 1) * 256]) for h in range(4)]
        rows.append(jnp.concatenate(cols, axis=1) + bsx)
    return rv, vh, v2, jnp.concatenate(rows, axis=0)


def _mixer_b_pooled(bx, halo, band_ref, counts):
    out = []
    for g in range(4):
        win = _dot(band_ref[g], halo[:, g * 256:(g + 1) * 256])
        out.append(win / counts[g] - bx[:, g * 256:(g + 1) * 256])
    return out


def _mix0_fwd(x, z0, w_out0, gv, ws, bsx, wg, scale, band, rider):
    s = x.shape[0]
    n_tiles = s // TOK
    k = len(rider.fulls)

    def body(z_ref, zp_ref, zn_ref, x_ref, wout_ref, gv_ref, ws_ref, bsx_ref, wg_ref, sc_ref, band_ref,
             cat_ref, x1_ref):
        i = pl.program_id(0)
        au = z_ref[:, 0:1024].astype(F32)
        av = z_ref[:, 1024:2048].astype(F32)
        ag = z_ref[:, 2048:3072].astype(F32)
        _, _, _, mixed = _mixer_a(_gelu(av), gv_ref[...], ws_ref, bsx_ref[...])
        cat_ref[:, 0:1024] = (_gelu(au) * mixed * (ag * _sigmoid(ag))).astype(BF16)

        bx = z_ref[:, 3072:4096]
        bg = z_ref[:, 4096:5120].astype(F32)
        halo = _with_halo(i, n_tiles, zp_ref, bx, zn_ref)
        ps = _mixer_b_pooled(bx.astype(F32), halo, band_ref, _window_counts(i, TOK, s))
        pw = jnp.concatenate([_dot(ps[g].astype(BF16), wg_ref[g]) for g in range(4)], axis=1)
        cat_ref[:, 1024:2048] = (pw * sc_ref[...] * (bg * _sigmoid(bg))).astype(BF16)

        x1_ref[...] = x_ref[...] + _dot(cat_ref[...], wout_ref[...])

    prev, nxt = _halo_specs(s, 3)
    return pl.pallas_call(
        rider.carried_by(body, 11, 2, n_tiles), name="mix0_fwd", grid=(n_tiles,),
        in_specs=[_rows(TOK, 5120), prev, nxt, _rows(TOK, D_MODEL), _whole(w_out0.shape), _whole((1, 1024)),
                  _whole(ws.shape), _whole(bsx.shape), _whole(wg.shape), _whole((1, 1024)), _whole(band.shape)]
        + rider.in_specs,
        out_specs=[_rows(TOK, 2048), _rows(TOK, D_MODEL)] + rider.out_specs,
        out_shape=[jax.ShapeDtypeStruct((s, 2048), BF16), jax.ShapeDtypeStruct((s, D_MODEL), F32)] + rider.out_shape,
        input_output_aliases={11 + j: 2 + j for j in range(k)},
        scratch_shapes=rider.scratch,
        compiler_params=_params(),
    )(z0, z0, z0, x, w_out0, gv, ws, bsx, wg, scale, band, *rider.fulls)


def _in_proj1(x1, g1, w_in1, rope):
    s = x1.shape[0]

    def body(x_ref, g_ref, w_ref, c_ref, lo_ref, hi_ref, h_ref, q_ref, k_ref, v_ref, gate_ref):
        halves = [slice(r * TOK, (r + 1) * TOK) for r in range(tok // TOK)]
        hs = [_rms_fwd(x_ref[rows, :], g_ref[...])[2].astype(BF16) for rows in halves]
        for rows, h in zip(halves, hs):
            h_ref[rows, :] = h
        qs = [_dot(h, w_ref[:, 0:1024]) for h in hs]
        kvs = [_dot(h, w_ref[:, 1024:1536]) for h in hs]
        for rows, q, kv in zip(halves, qs, kvs):
            tabs = (c_ref[rows, :], lo_ref[rows, :], hi_ref[rows, :])
            q_ref[rows, :] = (_rope(q, *tabs) * Q_SCALE).astype(BF16)
            k_ref[rows, :] = _rope(kv[:, 0:256], *tabs).astype(BF16)
            v_ref[rows, :] = kv[:, 256:512].astype(BF16)
        for rows, h in zip(halves, hs):
            gate_ref[rows, :] = _dot(h, w_ref[:, 1536:2560]).astype(BF16)

    tok = min(s, 2 * TOK)
    tab = _rows(tok, 128)
    return pl.pallas_call(
        body, name="in_proj1", grid=(s // tok,),
        in_specs=[_rows(tok, D_MODEL), _whole((1, D_MODEL)), _whole(w_in1.shape), tab, tab, tab],
        out_specs=[_rows(tok, 1024), _rows(tok, 1024), _rows(tok, 256), _rows(tok, 256), _rows(tok, 1024)],
        out_shape=[jax.ShapeDtypeStruct((s, 1024), BF16), jax.ShapeDtypeStruct((s, 1024), BF16),
                   jax.ShapeDtypeStruct((s, 256), BF16), jax.ShapeDtypeStruct((s, 256), BF16),
                   jax.ShapeDtypeStruct((s, 1024), BF16)],
        compiler_params=_params(),
    )(x1, g1, w_in1, *rope)


QBLK = 128
KBLK = QBLK + 2 * ATTN_WINDOW
Q_SCALE = HEAD_DIM ** -0.5


def _block_bias(q0, s):
    r = lax.broadcasted_iota(jnp.int32, (QBLK, KBLK), 0)
    c = lax.broadcasted_iota(jnp.int32, (QBLK, KBLK), 1)
    kj = q0 - ATTN_WINDOW + c
    ok = (c >= r) & (c <= r + 2 * ATTN_WINDOW) & (kj >= 0) & (kj < s)
    return jnp.where(ok, 0.0, NEG_INF)


def _pair_operands(t):
    lane = lax.broadcasted_iota(jnp.int32, (1, 128), 1)
    first = lane < HEAD_DIM
    zero = jnp.zeros((KBLK, 128), BF16)
    out = []
    for j in range(2):
        slab = t[:, 128 * j:128 * (j + 1)]
        turned = pltpu.bitcast(pltpu.roll(pltpu.bitcast(slab, jnp.uint32), HEAD_DIM, 1), BF16)
        for own_first in (True, False):
            top = jnp.where(first, slab if own_first else turned, zero)
            bottom = jnp.where(first, zero, turned if own_first else slab)
            out.append(jnp.concatenate([top, bottom], axis=0))
    return out


def _attn_fwd(q, kpad, vpad, sink):
    s = q.shape[0]

    def body(sink_ref, q_ref, k_ref, v_ref, o_ref, lse_ref):
        i = pl.program_id(0)
        lane = lax.broadcasted_iota(jnp.int32, (1, 128), 1)
        for b in range(TOK // QBLK):
            rows = slice(b * QBLK, (b + 1) * QBLK)
            start = pl.multiple_of(i * TOK + b * QBLK, QBLK)
            k_bd = _pair_operands(k_ref[pl.ds(start, KBLK), :])
            v_bd = _pair_operands(v_ref[pl.ds(start, KBLK), :])
            bias = _block_bias(i * TOK + b * QBLK, s)
            pairs = range(N_HEADS // 2)
            sc4 = [_dot_nt(jnp.concatenate([q_ref[rows, 256 * g:256 * g + 128], q_ref[rows, 256 * g + 128:256 * (g + 1)]],
                                           axis=0), k_bd[g]) for g in range(4)]
            sc2 = [sc4[m // 2][(m % 2) * QBLK:(m % 2 + 1) * QBLK] for m in pairs]
            scs = [sc2[h // 2][:, (h % 2) * KBLK:(h % 2 + 1) * KBLK] + bias for h in range(N_HEADS)]
            ms = [jnp.maximum(jnp.max(scs[h], axis=-1, keepdims=True), sink_ref[h]) for h in range(N_HEADS)]
            es = [jnp.exp(scs[h] - ms[h]) for h in range(N_HEADS)]
            dens = [jnp.sum(es[h], axis=-1, keepdims=True) + jnp.exp(sink_ref[h] - ms[h]) for h in range(N_HEADS)]
            first = lane < HEAD_DIM
            e2 = [jnp.concatenate([es[2 * m].astype(BF16), es[2 * m + 1].astype(BF16)], axis=1) for m in pairs]
            o4 = [_dot(jnp.concatenate([e2[2 * g], e2[2 * g + 1]], axis=0), v_bd[g]) for g in range(4)]
            outs = [o4[m // 2][(m % 2) * QBLK:(m % 2 + 1) * QBLK]
                    * jnp.where(first, 1.0 / dens[2 * m], 1.0 / dens[2 * m + 1]) for m in pairs]
            o_ref[rows, :] = jnp.concatenate(outs, axis=1).astype(BF16)
            lse = jnp.zeros((QBLK, 128), F32)
            for h in range(N_HEADS):
                lse = lse + jnp.where(lane == h, ms[h] + jnp.log(dens[h]), 0.0)
            lse_ref[rows, :] = lse

    return pl.pallas_call(
        body, name="attn_fwd", grid=(s // TOK,),
        in_specs=[pl.BlockSpec(memory_space=pltpu.SMEM), _rows(TOK, 1024), _whole(kpad.shape), _whole(vpad.shape)],
        out_specs=[_rows(TOK, 1024), _rows(TOK, 128)],
        out_shape=[jax.ShapeDtypeStruct((s, 1024), BF16), jax.ShapeDtypeStruct((s, 128), F32)],
        compiler_params=_params(),
    )(sink, q, kpad, vpad)


def _tail(x1, o, gate, target, w_out1, gf):
    s = x1.shape[0]

    def body(x1_ref, o_ref, gate_ref, t_ref, w_ref, gf_ref, y1_ref, dx2_ref, do_ref, dgate_ref, loss_ref, gfn_ref,
             dx2h_ref):
        i = pl.program_id(0)

        @pl.when(i == 0)
        def _():
            loss_ref[...] = jnp.zeros_like(loss_ref)
            gfn_ref[...] = jnp.zeros_like(gfn_ref)

        halves = [slice(r * TOK, (r + 1) * TOK) for r in range(tok // TOK)]
        gf = gf_ref[...]
        gs = [gate_ref[rows, :].astype(F32) for rows in halves]
        sgs = [_sigmoid(g) for g in gs]
        sils = [g * sg for g, sg in zip(gs, sgs)]
        os_ = [o_ref[rows, :].astype(F32) for rows in halves]
        y1s = [(o * sil).astype(BF16) for o, sil in zip(os_, sils)]
        for rows, y1 in zip(halves, y1s):
            y1_ref[rows, :] = y1
        x2s = [x1_ref[rows, :] + _dot(y1, w_ref[...]) for rows, y1 in zip(halves, y1s)]
        dx2hs = []
        for rows, x2 in zip(halves, x2s):
            r, xh, out = _rms_fwd(x2, gf)
            diff = out - t_ref[rows, :]
            loss_ref[...] += jnp.sum(diff * diff, axis=0, keepdims=True) * (0.5 / D_MODEL)
            dout = diff * (1.0 / D_MODEL)
            gfn_ref[...] += jnp.sum(dout * xh, axis=0, keepdims=True)
            dx2 = _rms_bwd(dout, gf, r, xh)
            dx2_ref[rows, :] = dx2
            dx2hs.append(dx2.astype(BF16))
            dx2h_ref[rows, :] = dx2hs[-1]
        dy1s = [_dot_nt(dx2h, w_ref[...]) for dx2h in dx2hs]
        for rows, dy1, sil, o, sg, g in zip(halves, dy1s, sils, os_, sgs, gs):
            do_ref[rows, :] = (dy1 * sil).astype(BF16)
            dgate_ref[rows, :] = (dy1 * o * (sg * (1.0 + g * (1.0 - sg)))).astype(BF16)

    tok = min(s, 2 * TOK)
    row = _rows(tok, 1024)
    acc = _whole((1, 1024))
    return pl.pallas_call(
        body, name="tail", grid=(s // tok,),
        in_specs=[row, row, row, row, _whole(w_out1.shape), acc],
        out_specs=[row, row, row, row, acc, acc, row],
        out_shape=[jax.ShapeDtypeStruct((s, 1024), BF16), jax.ShapeDtypeStruct((s, 1024), F32),
                   jax.ShapeDtypeStruct((s, 1024), BF16), jax.ShapeDtypeStruct((s, 1024), BF16),
                   jax.ShapeDtypeStruct((1, 1024), F32), jax.ShapeDtypeStruct((1, 1024), F32),
                   jax.ShapeDtypeStruct((s, 1024), BF16)],
        compiler_params=_params(),
    )(x1, o, gate, target, w_out1, gf)


def _attn_bwd(q, kpad, vpad, sink, o, lse, do, rope):
    s = q.shape[0]
    pad_t = (kpad.shape[1], kpad.shape[0])

    def body(sink_ref, q_ref, k_ref, v_ref, o_ref, lse_ref, do_ref, c_ref, lo_ref, hi_ref,
             dq_ref, dk_ref, dv_ref, ds_ref):
        i = pl.program_id(0)

        @pl.when(i == 0)
        def _():
            dk_ref[...] = jnp.zeros_like(dk_ref)
            dv_ref[...] = jnp.zeros_like(dv_ref)
            ds_ref[...] = jnp.zeros_like(ds_ref)

        lane = lax.broadcasted_iota(jnp.int32, (1, 128), 1)
        dsink = jnp.zeros((1, 128), F32)
        for b in range(TOK // QBLK):
            rows = slice(b * QBLK, (b + 1) * QBLK)
            start = pl.multiple_of(i * TOK + b * QBLK, QBLK)
            k_bd = _pair_operands(k_ref[pl.ds(start, KBLK), :])
            v_bd = _pair_operands(v_ref[pl.ds(start, KBLK), :])
            bias = _block_bias(i * TOK + b * QBLK, s)
            lanes_of = (lane < HEAD_DIM, lane >= HEAD_DIM)
            half = lambda t, j: t[:, j * KBLK:(j + 1) * KBLK]
            dqs, dks, dvs = [], [], []
            for g in range(4):
                pairs = (2 * g, 2 * g + 1)
                qs = {m: q_ref[rows, 128 * m:128 * (m + 1)] for m in pairs}
                dos = {m: do_ref[rows, 128 * m:128 * (m + 1)] for m in pairs}
                lses = {h: lse_ref[rows, h:h + 1] for h in range(4 * g, 4 * g + 4)}
                stacked = lambda parts: jnp.concatenate([parts[m] for m in pairs], axis=0)
                unstack = lambda t: {m: t[j * QBLK:(j + 1) * QBLK] for j, m in enumerate(pairs)}
                sc2 = unstack(_dot_nt(stacked(qs), k_bd[g]))
                ps = {2 * m + j: jnp.exp(half(sc2[m], j) + bias - lses[2 * m + j]) for m in pairs for j in range(2)}
                prods = {m: dos[m].astype(F32) * o_ref[rows, 128 * m:128 * (m + 1)].astype(F32) for m in pairs}
                deltas = {2 * m + j: jnp.sum(jnp.where(lanes_of[j], prods[m], 0.0), axis=-1, keepdims=True)
                          for m in pairs for j in range(2)}
                for h in range(4 * g, 4 * g + 4):
                    dsink = dsink + jnp.where(
                        lane == h, -jnp.sum(jnp.exp(sink_ref[h] - lses[h]) * deltas[h], axis=0, keepdims=True), 0.0)
                dp2 = unstack(_dot_nt(stacked(dos), v_bd[g]))
                ds2 = {m: jnp.concatenate(
                    [(ps[2 * m + j] * (half(dp2[m], j) - deltas[2 * m + j])).astype(BF16) for j in range(2)], axis=1)
                    for m in pairs}
                p2 = {m: jnp.concatenate([ps[2 * m].astype(BF16), ps[2 * m + 1].astype(BF16)], axis=1) for m in pairs}
                diag = lambda t: t[0:HEAD_DIM, 0:KBLK] + t[HEAD_DIM:128, KBLK:2 * KBLK]
                dvs.append(diag(_dot_tn(stacked(dos), stacked(p2))))
                dks.append(diag(_dot_tn(stacked(qs), stacked(ds2))))
                dq2 = unstack(_dot(stacked(ds2), k_bd[g]) * Q_SCALE)
                dqs += [dq2[m] for m in pairs]
            dq = jnp.concatenate(dqs, axis=1)
            dq_ref[rows, :] = _rope_t(dq, c_ref[rows, :], lo_ref[rows, :], hi_ref[rows, :]).astype(BF16)
            dk_ref[:, pl.ds(start, KBLK)] += jnp.concatenate(dks, axis=0)
            dv_ref[:, pl.ds(start, KBLK)] += jnp.concatenate(dvs, axis=0)
        ds_ref[...] += dsink

    row = _rows(TOK, 1024)
    tab = _rows(TOK, 128)
    pad = _whole(kpad.shape)
    return pl.pallas_call(
        body, name="attn_bwd", grid=(s // TOK,),
        in_specs=[pl.BlockSpec(memory_space=pltpu.SMEM), row, pad, pad, row, tab, row, tab, tab, tab],
        out_specs=[row, _whole(pad_t), _whole(pad_t), _whole((1, 128))],
        out_shape=[jax.ShapeDtypeStruct((s, 1024), BF16), jax.ShapeDtypeStruct(pad_t, F32),
                   jax.ShapeDtypeStruct(pad_t, F32), jax.ShapeDtypeStruct((1, 128), F32)],
        compiler_params=_params(),
    )(sink, q, kpad, vpad, o, lse, do, *rope)


def _in_proj1_bwd(dq, dk_t, dv_t, dgate, x1, dx2, g1, w_in1, rope):
    s = x1.shape[0]
    tok = min(s, 2 * TOK)
    n_sub = tok // ATTN_WINDOW

    def body(*refs):
        dq_ref = refs[0]
        dk_refs, dv_refs = refs[1:1 + n_sub], refs[1 + n_sub:1 + 2 * n_sub]
        (dgate_ref, x1_ref, dx2_ref, g_ref, w_ref, c_ref, lo_ref, hi_ref,
         dz_ref, dx1_ref, gn_ref, dx1h_ref) = refs[1 + 2 * n_sub:]

        @pl.when(pl.program_id(0) == 0)
        def _():
            gn_ref[...] = jnp.zeros_like(gn_ref)

        halves = [slice(r * TOK, (r + 1) * TOK) for r in range(tok // TOK)]
        per = TOK // ATTN_WINDOW
        g = g_ref[...]
        for r, rows in enumerate(halves):
            dk = jnp.concatenate([ref[...] for ref in dk_refs[r * per:(r + 1) * per]], axis=1).T
            dv = jnp.concatenate([ref[...] for ref in dv_refs[r * per:(r + 1) * per]], axis=1).T
            dz_ref[rows, 0:1024] = dq_ref[rows, :]
            dz_ref[rows, 1024:1280] = _rope_t(dk, c_ref[rows, :], lo_ref[rows, :], hi_ref[rows, :]).astype(BF16)
            dz_ref[rows, 1280:1536] = dv.astype(BF16)
            dz_ref[rows, 1536:2560] = dgate_ref[rows, :]
        dhs = [_dot_nt(dz_ref[rows, :], w_ref[...]) for rows in halves]
        for rows, dh in zip(halves, dhs):
            r, xh, _ = _rms_fwd(x1_ref[rows, :], g)
            gn_ref[...] += jnp.sum(dh * xh, axis=0, keepdims=True)
            dx1 = dx2_ref[rows, :] + _rms_bwd(dh, g, r, xh)
            dx1_ref[rows, :] = dx1
            dx1h_ref[rows, :] = dx1.astype(BF16)

    row = _rows(tok, 1024)
    subs = [pl.BlockSpec((256, ATTN_WINDOW), lambda i, j=j: (0, n_sub * i + 1 + j)) for j in range(n_sub)]
    tab = _rows(tok, 128)
    acc = _whole((1, 1024))
    return pl.pallas_call(
        body, name="in_proj1_bwd", grid=(s // tok,),
        in_specs=[row] + subs + subs + [row, row, row, acc, _whole(w_in1.shape), tab, tab, tab],
        out_specs=[_rows(tok, 2560), row, acc, row],
        out_shape=[jax.ShapeDtypeStruct((s, 2560), BF16), jax.ShapeDtypeStruct((s, 1024), F32),
                   jax.ShapeDtypeStruct((1, 1024), F32), jax.ShapeDtypeStruct((s, 1024), BF16)],
        compiler_params=_params(),
    )(dq, *[dk_t] * n_sub, *[dv_t] * n_sub, dgate, x1, dx2, g1, w_in1, *rope)


def _mix0_bwd(dx1, z0, w_out0, gv, ws, bsx, wg, scale, band, rider):
    s = dx1.shape[0]
    n_tiles = s // TOK

    def body(dx1h_ref, z_ref, zp_ref, zn_ref, wout_ref, gv_ref, ws_ref, bsx_ref, wg_ref, sc_ref, band_ref,
             dz_ref, dpn_ref, dws_ref, dbs_ref, dgv_ref, dsc_ref, dwg_ref):
        i = pl.program_id(0)
        dz_ref[:, 3072:4096] = jnp.zeros((TOK, 1024), BF16)

        @pl.when(i == 0)
        def _():
            for ref in (dws_ref, dbs_ref, dgv_ref, dsc_ref, dwg_ref):
                ref[...] = jnp.zeros_like(ref)

        dcat = _dot_nt(dx1h_ref[...], wout_ref[...])
        dya = dcat[:, 0:1024]
        dyb = dcat[:, 1024:2048]

        au = z_ref[:, 0:1024].astype(F32)
        av = z_ref[:, 1024:2048].astype(F32)
        ag = z_ref[:, 2048:3072].astype(F32)
        gv = gv_ref[...]
        u, du = _gelu_and_grad(au)
        v1, dv1 = _gelu_and_grad(av)
        rv, vh, v2, mixed = _mixer_a(v1, gv, ws_ref, bsx_ref[...])
        sg = _sigmoid(ag)
        sil = ag * sg
        dz_ref[:, 2048:3072] = (dya * u * mixed * (sg * (1.0 + ag * (1.0 - sg)))).astype(BF16)
        dz_ref[:, 0:1024] = (dya * mixed * sil * du).astype(BF16)
        dmixed = dya * u * sil
        lane = lax.broadcasted_iota(jnp.int32, (1, 128), 1)
        dm16 = dmixed.astype(BF16)
        dv2_rows = []
        for c in range(TOK // CHUNK):
            rows = slice(c * CHUNK, (c + 1) * CHUNK)
            cols_out = []
            for h in range(4):
                cols = slice(h * 256, (h + 1) * 256)
                dws_ref[h] += _dot_nt(dm16[rows, cols], v2[rows, cols])
                dbs_ref[...] += jnp.where(lane == h, jnp.sum(dmixed[rows, cols], axis=-1, keepdims=True), 0.0)
                cols_out.append(_dot_tn(ws_ref[h], dm16[rows, cols]))
            dv2_rows.append(jnp.concatenate(cols_out, axis=1))
        dv2 = jnp.concatenate(dv2_rows, axis=0)
        dgv_ref[...] += jnp.sum(dv2 * vh, axis=0, keepdims=True)
        dz_ref[:, 1024:2048] = (_rms_bwd(dv2, gv, rv, vh) * dv1).astype(BF16)

        bx = z_ref[:, 3072:4096]
        bg = z_ref[:, 4096:5120].astype(F32)
        counts = _window_counts(i, TOK, s)
        halo = _with_halo(i, n_tiles, zp_ref, bx, zn_ref)
        ps = [p.astype(BF16) for p in _mixer_b_pooled(bx.astype(F32), halo, band_ref, counts)]
        pw = jnp.concatenate([_dot(ps[g], wg_ref[g]) for g in range(4)], axis=1)
        sgb = _sigmoid(bg)
        sc = sc_ref[...]
        dz_ref[:, 4096:5120] = (dyb * pw * sc * (sgb * (1.0 + bg * (1.0 - sgb)))).astype(BF16)
        dys = dyb * (bg * sgb)
        dsc_ref[...] += jnp.sum(dys * pw, axis=0, keepdims=True)
        dpw = (dys * sc).astype(BF16)
        for g in range(4):
            cols = slice(g * 256, (g + 1) * 256)
            dwg_ref[g] += _dot_tn(ps[g], dpw[:, cols])
            dpn_ref[:, cols] = (_dot_nt(dpw[:, cols], wg_ref[g]) / counts[g]).astype(BF16)

    prev, nxt = _halo_specs(s, 3)
    row = _rows(TOK, 1024)
    vec = _whole((1, 1024))
    return pl.pallas_call(
        rider.carried_by(body, 11, 7, n_tiles), name="mix0_bwd", grid=(n_tiles,),
        in_specs=[row, _rows(TOK, 5120), prev, nxt, _whole(w_out0.shape), vec, _whole(ws.shape),
                  _whole(bsx.shape), _whole(wg.shape), vec, _whole(band.shape)] + rider.in_specs,
        out_specs=[_rows(TOK, 5120), row, _whole((4, 128, 128)), _whole((128, 128)), vec, vec,
                   _whole((4, 256, 256))] + rider.out_specs,
        out_shape=[jax.ShapeDtypeStruct((s, 5120), BF16), jax.ShapeDtypeStruct((s, 1024), BF16),
                   jax.ShapeDtypeStruct((4, 128, 128), F32), jax.ShapeDtypeStruct((128, 128), F32),
                   jax.ShapeDtypeStruct((1, 1024), F32), jax.ShapeDtypeStruct((1, 1024), F32),
                   jax.ShapeDtypeStruct((4, 256, 256), F32)] + rider.out_shape,
        scratch_shapes=rider.scratch,
        compiler_params=_params(),
    )(dx1, z0, z0, z0, w_out0, gv, ws, bsx, wg, scale, band, *rider.parts)


def _fill_pooled_grad(dz0, dpn):
    s = dpn.shape[0]
    tok = min(s, 2 * TOK)
    n_tiles = s // tok
    band_t = _band_matrices(tok)[1]

    def body(dz_in_ref, dpn_ref, dpp_ref, dpx_ref, band_ref, dbx_ref):
        i = pl.program_id(0)
        dpn = dpn_ref[...]
        halo = _with_halo(i, n_tiles, dpp_ref, dpn, dpx_ref)
        counts = _window_counts(i, tok, s)
        for g in range(4):
            cols = slice(g * 256, (g + 1) * 256)
            dbx = _dot(band_ref[g], halo[:, cols]) - dpn[:, cols].astype(F32) * counts[g]
            dbx_ref[:, cols] = dbx.astype(BF16)

    prev, nxt = _halo_specs(s, 0, tok)
    return pl.pallas_call(
        body, name="fill_pooled_grad", grid=(n_tiles,),
        in_specs=[ANY, _rows(tok, 1024), prev, nxt, _whole(band_t.shape)],
        out_specs=pl.BlockSpec((tok, 1024), lambda i: (i, 3)),
        out_shape=jax.ShapeDtypeStruct(dz0.shape, BF16),
        input_output_aliases={0: 0},
        compiler_params=_params(),
    )(dz0, dpn, dpn, dpn, band_t)


def _in_proj0_bwd(dz0, x, dx1, g0, w_in0, rider):
    s = x.shape[0]
    tok = min(s, 2 * TOK)
    n_tiles = s // tok

    def body(dz_ref, x_ref, dx1_ref, g_ref, w_ref, dx_ref, gn_ref):
        i = pl.program_id(0)

        @pl.when(i == 0)
        def _():
            gn_ref[...] = jnp.zeros_like(gn_ref)

        halves = [slice(r * TOK, (r + 1) * TOK) for r in range(tok // TOK)]
        g0v = g_ref[...]
        dhs = [_dot_nt(dz_ref[rows, :], w_ref[...]) for rows in halves]
        for rows, dh in zip(halves, dhs):
            r, xh, _ = _rms_fwd(x_ref[rows, :], g0v)
            gn_ref[...] += jnp.sum(dh * xh, axis=0, keepdims=True)
            dx_ref[rows, :] = dx1_ref[rows, :] + _rms_bwd(dh, g0v, r, xh)

    row = _rows(tok, 1024)
    vec = _whole((1, 1024))
    return pl.pallas_call(
        rider.carried_by(body, 5, 2, n_tiles), name="in_proj0_bwd", grid=(n_tiles,),
        in_specs=[_rows(tok, 5120), row, row, vec, _whole(w_in0.shape)] + rider.in_specs,
        out_specs=[row, vec] + rider.out_specs,
        out_shape=[jax.ShapeDtypeStruct((s, 1024), F32), jax.ShapeDtypeStruct((1, 1024), F32)] + rider.out_shape,
        scratch_shapes=rider.scratch,
        compiler_params=_params(),
    )(dz0, x, dx1, g0, w_in0, *rider.parts)


def _weight_grad(a, b, n_blocks, split, name):
    s, k = a.shape
    n = b.shape[1]
    tn = n // n_blocks
    w = tn // split
    ts = min(s, 1024)

    def body(a_ref, b_ref, o_ref):
        @pl.when(pl.program_id(1) == 0)
        def _():
            o_ref[...] = jnp.zeros_like(o_ref)

        res = _dot_tn(a_ref[...], b_ref[...])
        for q in range(split):
            o_ref[q] += res[:, q * w:(q + 1) * w]

    return pl.pallas_call(
        body, name=name, grid=(n_blocks, s // ts),
        in_specs=[pl.BlockSpec((ts, k), lambda j, t: (t, 0)), pl.BlockSpec((ts, tn), lambda j, t: (t, j))],
        out_specs=pl.BlockSpec((split, k, w), lambda j, t: (j, 0, 0)),
        out_shape=jax.ShapeDtypeStruct((n_blocks * split, k, w), F32),
        compiler_params=_params(),
    )(a, b)


def _row_tile(rows, cols):
    t = rows
    while t * cols * 4 > (1 << 20) and t % 16 == 0:
        t //= 2
    return t


def _add_sibling(where, g, theirs, name):
    _, _, rows, cols = g.shape
    t = _row_tile(rows, cols)

    def body(where_ref, g_ref, t_ref, o_ref):
        o_ref[...] = (g_ref[...] + t_ref[...]).astype(BF16)

    spec = pl.BlockSpec((None, t, cols), lambda s, i, p: (s, i, 0))
    return pl.pallas_call(
        body, name=name, out_shape=jax.ShapeDtypeStruct((4, rows, cols), BF16),
        grid_spec=pltpu.PrefetchScalarGridSpec(
            num_scalar_prefetch=1, grid=(4, rows // t),
            in_specs=[pl.BlockSpec((None, None, t, cols), lambda s, i, p: (s, p[1], i, 0)), spec], out_specs=spec),
        compiler_params=_params())(where, g, theirs)


def _sum_chips(where, g, theirs, slots, name):
    _, _, rows, cols = g.shape
    t = _row_tile(rows, cols)

    def body(where_ref, g_ref, t_ref, s_ref, o_ref):
        me = where_ref[0]
        own = g_ref[...] + t_ref[...]
        acc = jnp.where(me == 0, own, s_ref[0].astype(F32))
        for k in range(1, 4):
            acc = acc + jnp.where(me == k, own, s_ref[k].astype(F32))
        o_ref[...] = acc

    return pl.pallas_call(
        body, name=name, out_shape=jax.ShapeDtypeStruct((rows, cols), F32),
        grid_spec=pltpu.PrefetchScalarGridSpec(
            num_scalar_prefetch=1, grid=(rows // t,),
            in_specs=[pl.BlockSpec((None, None, t, cols), lambda i, p: (p[0], p[1], i, 0)),
                      pl.BlockSpec((None, t, cols), lambda i, p: (p[0], i, 0)),
                      pl.BlockSpec((4, t, cols), lambda i, p: (0, i, 0))],
            out_specs=pl.BlockSpec((t, cols), lambda i, p: (i, 0))),
        compiler_params=_params())(where, g, theirs, slots)


def _adamw_halves(where, w, own, theirs, m, v, name):
    rows, cols = own.shape
    t = _row_tile(rows, cols)
    per = rows // t

    def body(where_ref, w_ref, own_ref, th_ref, m_ref, v_ref, g_ref, d_ref, nm_ref, nv_ref):
        g = jnp.where(pl.program_id(0) == where_ref[1], own_ref[...], th_ref[...])
        g_ref[...] = g
        m2 = ADAM_B1 * m_ref[...] + (1.0 - ADAM_B1) * g
        v2 = ADAM_B2 * v_ref[...] + (1.0 - ADAM_B2) * (g * g)
        m_hat = m2 / (1.0 - ADAM_B1 ** ADAM_STEP)
        v_hat = v2 / (1.0 - ADAM_B2 ** ADAM_STEP)
        d_ref[...] = -ADAM_LR * (m_hat / (jnp.sqrt(v_hat) + ADAM_EPS) + ADAM_WD * w_ref[...])
        nm_ref[...] = m2
        nv_ref[...] = v2

    full = pl.BlockSpec((t, cols), lambda h, i, p: (h * per + i, 0))
    half = pl.BlockSpec((t, cols), lambda h, i, p: (i, 0))
    shp = jax.ShapeDtypeStruct(w.shape, F32)
    return pl.pallas_call(
        body, name=name, out_shape=[shp] * 4,
        grid_spec=pltpu.PrefetchScalarGridSpec(
            num_scalar_prefetch=1, grid=(2, per), in_specs=[full, half, half, full, full], out_specs=[full] * 4),
        compiler_params=_params())(where, w, own, theirs, m, v)


def _place_shard(where, w, cut, name):
    if cut.kind == "cols":
        r, n = cut.full_shape
        blk, grid = (256, n // 4), (r // 256,)
        src_map, dst_map = (lambda i, p: (i, 0)), (lambda i, p: (i, p[0]))
    elif cut.kind == "rows":
        r, n = cut.full_shape
        per = r // 4 // 256
        blk, grid = (256, n), (per,)
        src_map, dst_map = (lambda i, p: (i, 0)), (lambda i, p: (p[0] * per + i, 0))
    else:
        g, r, n = cut.full_shape
        blk, grid = (g, r // 4, n), (1,)
        src_map, dst_map = (lambda i, p: (0, 0, 0)), (lambda i, p: (0, p[0], 0))

    def body(where_ref, w_ref, o_ref):
        o_ref[...] = w_ref[...].astype(BF16)

    return pl.pallas_call(
        body, name=name, out_shape=jax.ShapeDtypeStruct(cut.full_shape, BF16),
        grid_spec=pltpu.PrefetchScalarGridSpec(
            num_scalar_prefetch=1, grid=grid, in_specs=[pl.BlockSpec(blk, src_map)],
            out_specs=pl.BlockSpec(blk, dst_map)),
        compiler_params=_params())(where, w)


def _sum_small(first, second, third):
    rows = second.shape[1]

    def body(a_ref, b_ref, c_ref, o_ref):
        top = a_ref[0] + c_ref[0] + b_ref[0, 0:8]
        rest = b_ref[0, 8:rows]
        for k in range(1, 8):
            top = top + (a_ref[k] + c_ref[k] + b_ref[k, 0:8])
            rest = rest + b_ref[k, 8:rows]
        o_ref[0:8] = top
        o_ref[8:rows] = rest

    return pl.pallas_call(
        body, name="sum_small", in_specs=[_whole(first.shape), _whole(second.shape), _whole(third.shape)],
        out_specs=_whole(second.shape[1:]), out_shape=jax.ShapeDtypeStruct(second.shape[1:], F32),
        compiler_params=_params())(first, second, third)


def _adamw(w, g, m, v, name):
    rows, cols = w.shape
    t = _row_tile(rows, cols)

    def body(w_ref, g_ref, m_ref, v_ref, d_ref, nm_ref, nv_ref):
        g = g_ref[...]
        m2 = ADAM_B1 * m_ref[...] + (1.0 - ADAM_B1) * g
        v2 = ADAM_B2 * v_ref[...] + (1.0 - ADAM_B2) * (g * g)
        m_hat = m2 / (1.0 - ADAM_B1 ** ADAM_STEP)
        v_hat = v2 / (1.0 - ADAM_B2 ** ADAM_STEP)
        d_ref[...] = -ADAM_LR * (m_hat / (jnp.sqrt(v_hat) + ADAM_EPS) + ADAM_WD * w_ref[...])
        nm_ref[...] = m2
        nv_ref[...] = v2

    spec = pl.BlockSpec((t, cols), lambda i: (i, 0))
    shp = jax.ShapeDtypeStruct(w.shape, F32)
    return pl.pallas_call(body, name=name, grid=(rows // t,), in_specs=[spec] * 4, out_specs=[spec] * 3,
                          out_shape=[shp] * 3, compiler_params=_params())(w, g, m, v)


def _place():
    x, y, c = lax.axis_index("x"), lax.axis_index("y"), lax.axis_index("c")
    chips = [(1 - x, y), (x, 1 - y), (1 - x, 1 - y)]
    return x, y, c, chips


class _Sharded:
    def __init__(self, kind, full_shape):
        self.kind = kind
        self.full_shape = full_shape

    def in_full(self, ref, s, h):
        if self.kind == "cols":
            r, n = self.full_shape
            return ref.at[pl.ds(h * (r // 2), r // 2), pl.ds(pl.multiple_of(s * (n // 4), 128), n // 4)]
        if self.kind == "rows":
            r, _ = self.full_shape
            return ref.at[pl.ds(pl.multiple_of(s * (r // 4) + h * (r // 8), 8), r // 8), :]
        g, r, _ = self.full_shape
        return ref.at[pl.ds(h * (g // 2), g // 2), pl.ds(pl.multiple_of(s * (r // 4), 16), r // 4), :]


def _remote(src, dst, send_sem, recv_sem, to):
    return pltpu.make_async_remote_copy(src_ref=src, dst_ref=dst, send_sem=send_sem, recv_sem=recv_sem,
                                        device_id=to, device_id_type=MESH)


def _start_remote(src, dst, send_sem, recv_sem, to):
    cp = _remote(src, dst, send_sem, recv_sem, to)
    cp.start()
    return cp


class _Gather:
    def __init__(self, fulls, cuts):
        n = len(fulls)
        self.fulls, self.cuts = list(fulls), list(cuts)
        self.in_specs = [ANY] * n
        self.out_specs = [ANY] * n
        self.out_shape = [jax.ShapeDtypeStruct(cut.full_shape, BF16) for cut in cuts]
        self.scratch = [pltpu.SemaphoreType.DMA((6 * n,)), pltpu.SemaphoreType.DMA((6 * n,))]

    def _step(self, step, src, out, send_sems, recv_sems):
        n, cuts = len(self.fulls), self.cuts
        x, y, c, chips = _place()
        me = 2 * x + y

        def ends(w, s, h, from_src):
            dst = cuts[w].in_full(out[w], s, h)
            return (cuts[w].in_full(src[w], s, h) if from_src else dst), dst

        for w in range(n):
            for j, chip in enumerate(chips):
                s = 2 * chip[0] + chip[1]
                k, k2 = 3 * w + j, 3 * n + 3 * w + j
                if step == "send":
                    _start_remote(*ends(w, me, c, True), send_sems.at[k], recv_sems.at[k], (*chip, c))
                elif step == "pass_on":
                    _remote(*ends(w, s, c, False), send_sems.at[k], recv_sems.at[k], (x, y, c)).wait_recv()
                    _start_remote(*ends(w, s, c, False), send_sems.at[k2], recv_sems.at[k2], (x, y, 1 - c))
                else:
                    _remote(*ends(w, s, 1 - c, False), send_sems.at[k2], recv_sems.at[k2], (x, y, c)).wait_recv()
                    _remote(*ends(w, me, c, True), send_sems.at[k], recv_sems.at[k], (x, y, c)).wait_send()
                    _remote(*ends(w, s, c, False), send_sems.at[k2], recv_sems.at[k2], (x, y, c)).wait_send()

    def carried_by(self, body, n_in, n_out, n_steps, step_index=lambda: pl.program_id(0)):
        k = len(self.fulls)

        def carrier(*refs):
            ins, src = refs[:n_in], refs[n_in:n_in + k]
            outs, out = refs[n_in + k:n_in + k + n_out], refs[n_in + k + n_out:n_in + 2 * k + n_out]
            sems = refs[n_in + 2 * k + n_out:]
            for step, at in (("send", 0), ("pass_on", 3 * n_steps // 4), ("finish", n_steps - 1)):
                if step == "finish":
                    body(*ins, *outs)

                @pl.when(step_index() == at)
                def _():
                    self._step(step, src, out, *sems)

        return carrier


def _exchange_halves(grads, name):
    n = len(grads)

    def body(*refs):
        g = refs[:n]
        theirs = refs[n:2 * n]
        send_sems, recv_sems = refs[2 * n:]
        x, y, c, _ = _place()
        sends = [_start_remote(g[w].at[:, 1 - c], theirs[w], send_sems.at[w], recv_sems.at[w], (x, y, 1 - c))
                 for w in range(n)]
        for w in range(n):
            _remote(g[w].at[:, 1 - c], theirs[w], send_sems.at[w], recv_sems.at[w], (x, y, c)).wait_recv()
        for cp in sends:
            cp.wait_send()

    return pl.pallas_call(
        body, name=name,
        in_specs=[ANY] * n, out_specs=[ANY] * n,
        out_shape=[jax.ShapeDtypeStruct((4,) + g.shape[2:], F32) for g in grads],
        scratch_shapes=[pltpu.SemaphoreType.DMA((n,)), pltpu.SemaphoreType.DMA((n,))],
        compiler_params=pltpu.CompilerParams(has_side_effects=True),
    )(*grads)


def _gather_small(small_ref, gathered, send_sems, recv_sems, first_sem, local_sem, start):
    x, y, c, _ = _place()
    me = 4 * x + 2 * y + c
    flips = [(fx, fy, fc) for fx in range(2) for fy in range(2) for fc in range(2)][1:]
    own = pltpu.make_async_copy(small_ref, gathered.at[me], local_sem)
    if start:
        own.start()
    else:
        own.wait()
    for k, (fx, fy, fc) in enumerate(flips):
        peer = (x + fx - 2 * x * fx, y + fy - 2 * y * fy, c + fc - 2 * c * fc)
        sems = (send_sems.at[first_sem + k], recv_sems.at[first_sem + k])
        if start:
            _start_remote(small_ref, gathered.at[me], *sems, peer)
        else:
            cp = _remote(small_ref, gathered.at[4 * peer[0] + 2 * peer[1] + peer[2]], *sems, (x, y, c))
            cp.wait_recv()
            cp.wait_send()


class _ScatterRider:
    def __init__(self, parts, small):
        n = len(parts)
        self.n = n
        self.parts = list(parts) + [small]
        self.in_specs = [ANY] * (n + 1)
        self.out_specs = [ANY] * (n + 1)
        self.out_shape = ([jax.ShapeDtypeStruct(a.shape, a.dtype) for a in parts]
                          + [jax.ShapeDtypeStruct((8,) + small.shape, small.dtype)])
        self.scratch = [pltpu.SemaphoreType.DMA((3 * n + 7,)), pltpu.SemaphoreType.DMA((3 * n + 7,)),
                        pltpu.SemaphoreType.DMA]

    def _copies(self, p, out, send_sems, recv_sems, local_sem, start):
        x, y, c, chips = _place()
        me = 2 * x + y
        n = self.n
        _gather_small(p[n], out[n], send_sems, recv_sems, 3 * n, local_sem, start)
        for w in range(n):
            for j, chip in enumerate(chips):
                s = 2 * chip[0] + chip[1]
                if start:
                    _start_remote(p[w].at[s], out[w].at[me], send_sems.at[3 * w + j], recv_sems.at[3 * w + j],
                                  (*chip, c))
                else:
                    cp = _remote(p[w].at[s], out[w].at[s], send_sems.at[3 * w + j], recv_sems.at[3 * w + j],
                                 (x, y, c))
                    cp.wait_recv()
                    cp.wait_send()

    def carried_by(self, body, n_in, n_out, n_tiles):
        k = len(self.parts)

        def carrier(*refs):
            ins, mine = refs[:n_in], refs[n_in:n_in + k]
            outs, slots = refs[n_in + k:n_in + k + n_out], refs[n_in + k + n_out:n_in + 2 * k + n_out]
            sems = refs[n_in + 2 * k + n_out:]

            @pl.when(pl.program_id(0) == 0)
            def _():
                self._copies(mine, slots, *sems, start=True)

            body(*ins, *outs)

            @pl.when(pl.program_id(0) == n_tiles - 1)
            def _():
                self._copies(mine, slots, *sems, start=False)

        return carrier


def _share_halves(halves, small):
    n = len(halves)

    def body(*refs):
        hv = refs[:n]
        small_ref = refs[n]
        out = refs[n + 1:2 * n + 1]
        gathered = refs[2 * n + 1]
        send_sems, recv_sems, local_sem = refs[2 * n + 2:]
        x, y, c, _ = _place()
        sends = [_start_remote(hv[w], out[w], send_sems.at[w], recv_sems.at[w], (x, y, 1 - c)) for w in range(n)]
        _gather_small(small_ref, gathered, send_sems, recv_sems, n, local_sem, True)
        for w in range(n):
            _remote(hv[w], out[w], send_sems.at[w], recv_sems.at[w], (x, y, c)).wait_recv()
        for cp in sends:
            cp.wait_send()
        _gather_small(small_ref, gathered, send_sems, recv_sems, n, local_sem, False)

    return pl.pallas_call(
        body, name="share_halves",
        in_specs=[ANY] * (n + 1), out_specs=[ANY] * (n + 1),
        out_shape=[jax.ShapeDtypeStruct(a.shape, F32) for a in halves] + [jax.ShapeDtypeStruct((8,) + small.shape, F32)],
        scratch_shapes=[pltpu.SemaphoreType.DMA((n + 7,)), pltpu.SemaphoreType.DMA((n + 7,)),
                        pltpu.SemaphoreType.DMA],
        compiler_params=pltpu.CompilerParams(has_side_effects=True),
    )(*halves, small)


SMALL_ROWS = 80


def _pack_small(vecs, ws, bs, sink, extra=None):
    ws = jnp.zeros((64, 1024), F32) if ws is None else ws.reshape(64, 1024)
    bs = jnp.zeros((1, 512), F32) if bs is None else bs.reshape(1, 512)
    sink = jnp.zeros((1, 16), F32) if sink is None else sink.reshape(1, 16)
    extra = jnp.zeros((1, 1024), F32) if extra is None else extra.reshape(1, 1024)
    top = jnp.concatenate(
        [v.reshape(1, 1024) for v in vecs]
        + [jnp.pad(bs, ((0, 0), (0, 512))), jnp.pad(sink, ((0, 0), (0, 1008))), extra], axis=0)
    return jnp.concatenate([top, ws, jnp.zeros((8, 1024), F32)], axis=0)


def _unpack_small(p):
    vecs = [p[k] for k in range(5)]
    return vecs, p[8:72].reshape(4, 128, 128), p[5, :512].reshape(4, 128), p[6, :16]


def kernel(x, norm_0, w_in_0, a_v_norm_0, a_spatial_w_0, a_spatial_b_0, b_group_w_0, b_scale_0, w_out_0, norm_1, w_in_1, sink_1, w_out_1, final_norm, loss_target, m_norm_0, m_w_in_0, m_a_v_norm_0, m_a_spatial_w_0, m_a_spatial_b_0, m_b_group_w_0, m_b_scale_0, m_w_out_0, m_norm_1, m_w_in_1, m_sink_1, m_w_out_1, m_final_norm, v_norm_0, v_w_in_0, v_a_v_norm_0, v_a_spatial_w_0, v_a_spatial_b_0, v_b_group_w_0, v_b_scale_0, v_w_out_0, v_norm_1, v_w_in_1, v_sink_1, v_w_out_1, v_final_norm):
    s = x.shape[1]
    xs = x.reshape(s, D_MODEL)
    target = loss_target.reshape(s, D_MODEL)

    cuts = [_Sharded("cols", (1024, 5120)), _Sharded("rows", (2048, 1024)), _Sharded("cols", (1024, 2560)),
            _Sharded("rows", (1024, 1024)), _Sharded("mid", (4, 256, 256))]
    big_w = [w_in_0, w_out_0, w_in_1, w_out_1, b_group_w_0]
    big_m = [m_w_in_0, m_w_out_0, m_w_in_1, m_w_out_1, m_b_group_w_0]
    big_v = [v_w_in_0, v_w_out_0, v_w_in_1, v_w_out_1, v_b_group_w_0]
    where = jnp.stack([2 * lax.axis_index("x") + lax.axis_index("y"), lax.axis_index("c")]).astype(jnp.int32)
    placed = [_place_shard(where, w, cut, f"place_shard{k}") for k, (w, cut) in enumerate(zip(big_w, cuts))]
    cx, cy = lax.axis_index("x"), lax.axis_index("y")
    order = jnp.stack([2 * cx + cy, 2 * (1 - cx) + cy, 2 * cx + 1 - cy, 2 * (1 - cx) + 1 - cy]).astype(jnp.int32)

    row = lambda v: v.reshape(1, 1024)
    ws16 = a_spatial_w_0.astype(BF16)
    bsx = jnp.repeat(a_spatial_b_0.T, 256, axis=1)
    band = _band_matrices(TOK)[0]
    rope = _rope_tables(s)

    h0, z0, w_in0 = _in_proj0_own(order, xs, row(norm_0), w_in_0.astype(BF16), _Gather(placed[:1], cuts[:1]))
    z0, w_out0, wg = _in_proj0_rest(order, h0, w_in0, z0, _Gather([placed[1], placed[4]], [cuts[1], cuts[4]]))
    cat, x1, w_in1, w_out1 = _mix0_fwd(xs, z0, w_out0, row(a_v_norm_0), ws16, bsx, wg, row(b_scale_0), band,
                                       _Gather(placed[2:4], cuts[2:4]))
    h1, q, k, v, gate = _in_proj1(x1, row(norm_1), w_in1, rope)
    kpad = jnp.pad(k, ((ATTN_WINDOW, ATTN_WINDOW), (0, 0)))
    vpad = jnp.pad(v, ((ATTN_WINDOW, ATTN_WINDOW), (0, 0)))
    o, lse = _attn_fwd(q, kpad, vpad, sink_1)
    y1, dx2, do, dgate, loss_lanes, g_final, dx2h = _tail(x1, o, gate, target, w_out1, row(final_norm))

    dq, dkpad, dvpad, dsink = _attn_bwd(q, kpad, vpad, sink_1, o, lse, do, rope)
    dz1, dx1, g_norm1, dx1h = _in_proj1_bwd(dq, dkpad, dvpad, dgate, x1, dx2, row(norm_1), w_in1, rope)

    g_w_in1 = _weight_grad(h1, dz1, 2, 2, "grad_w_in1").reshape(4, 2, 512, 640)
    g_w_out1 = _weight_grad(y1, dx2h, 1, 1, "grad_w_out1").reshape(4, 2, 128, 1024)
    g_w_out0 = _weight_grad(cat, dx1h, 1, 1, "grad_w_out0").reshape(4, 2, 256, 1024)
    first = [g_w_out0, g_w_in1, g_w_out1]
    theirs1 = _exchange_halves(first, "exchange_halves1")
    parts1 = [_add_sibling(where, g, t, f"add_sibling1_{k}") for k, (g, t) in enumerate(zip(first, theirs1))]
    zero = jnp.zeros((1024,), F32)
    small1 = _pack_small([zero, zero, zero, g_norm1, g_final], None, None, dsink[0, :16])[:8]
    dz0, dpn, d_ws, d_bs, d_gv, d_scale, d_wg, *slots1, small1_all = _mix0_bwd(
        dx1h, z0, w_out0, row(a_v_norm_0), ws16, bsx, wg, row(b_scale_0), band, _ScatterRider(parts1, small1))
    dz0 = _fill_pooled_grad(dz0, dpn)
    g_w_in0 = _weight_grad(h0, dz0, 4, 1, "grad_w_in0").reshape(4, 2, 512, 1280)
    g_wg = d_wg.reshape(2, 2, 4, 64, 256).transpose(2, 0, 1, 3, 4).reshape(4, 2, 128, 256)
    second = [g_w_in0, g_wg]
    theirs2 = _exchange_halves(second, "exchange_halves2")
    parts2 = [_add_sibling(where, g, t, f"add_sibling2_{k}") for k, (g, t) in enumerate(zip(second, theirs2))]
    small2 = _pack_small([zero, d_gv, d_scale, zero, zero], d_ws, d_bs[:, :4].T, None)
    grad_x, g_norm0, *slots2, small2_all = _in_proj0_bwd(
        dz0, xs, dx1, row(norm_0), w_in0, _ScatterRider(parts2, small2))

    grads = [g_w_in0, g_w_out0, g_w_in1, g_w_out1, g_wg]
    theirs = [theirs2[0], theirs1[0], theirs1[1], theirs1[2], theirs2[1]]
    slots = [slots2[0], slots1[0], slots1[1], slots1[2], slots2[1]]
    n = len(grads)
    reduced = [_sum_chips(where, grads[w], theirs[w], slots[w], f"sum_chips{w}") for w in range(n)]
    small3 = _pack_small([g_norm0, zero, zero, zero, zero], None, None, None, loss_lanes)[:8]
    *from_sibling, small3_all = _share_halves(reduced, small3)

    out_g, out_d, out_m, out_v = {}, {}, {}, {}
    names = ["w_in_0", "w_out_0", "w_in_1", "w_out_1", "b_group_w_0"]
    for w in range(n):
        shape = big_w[w].shape
        two_d = (-1, shape[-1])
        outs = _adamw_halves(where, big_w[w].reshape(two_d), reduced[w], from_sibling[w], big_m[w].reshape(two_d),
                             big_v[w].reshape(two_d), f"adamw{w}")
        out_g[names[w]], out_d[names[w]], out_m[names[w]], out_v[names[w]] = (a.reshape(shape) for a in outs)

    g_small = _sum_small(small1_all, small2_all, small3_all)
    small_names = ["norm_0", "a_v_norm_0", "b_scale_0", "norm_1", "final_norm"]
    pack = lambda vecs, ws_, bs_, sk: _pack_small(vecs, ws_, bs_, sk)
    w_small = pack([norm_0, a_v_norm_0, b_scale_0, norm_1, final_norm], a_spatial_w_0, a_spatial_b_0, sink_1)
    m_small = pack([m_norm_0, m_a_v_norm_0, m_b_scale_0, m_norm_1, m_final_norm], m_a_spatial_w_0,
                   m_a_spatial_b_0, m_sink_1)
    v_small = pack([v_norm_0, v_a_v_norm_0, v_b_scale_0, v_norm_1, v_final_norm], v_a_spatial_w_0,
                   v_a_spatial_b_0, v_sink_1)
    d_small, nm_small, nv_small = _adamw(w_small, g_small, m_small, v_small, "adamw_small")
    for store, packed in ((out_g, g_small), (out_d, d_small), (out_m, nm_small), (out_v, nv_small)):
        vecs, ws_, bs_, sk = _unpack_small(packed)
        for name, vec in zip(small_names, vecs):
            store[name] = vec
        store["a_spatial_w_0"], store["a_spatial_b_0"], store["sink_1"] = ws_, bs_, sk

    loss = jnp.sum(g_small[7])
    order = ["norm_0", "w_in_0", "a_v_norm_0", "a_spatial_w_0", "a_spatial_b_0", "b_group_w_0", "b_scale_0",
             "w_out_0", "norm_1", "w_in_1", "sink_1", "w_out_1", "final_norm"]
    return (loss, grad_x.reshape(1, s, D_MODEL), *[out_g[k] for k in order], *[out_d[k] for k in order],
            *[out_m[k] for k in order], *[out_v[k] for k in order])
```

```python
import functools

import numpy as np
import jax
import jax.numpy as jnp
from jax import lax
from jax.experimental import pallas as pl
from jax.experimental.pallas import tpu as pltpu

F32 = jnp.float32
BF16 = jnp.bfloat16
MESH = pl.DeviceIdType.MESH

D_MODEL = 1024
EPS = 1e-6
NEG_INF = -1e30
CHUNK = 128
POOL_WINDOWS = (2, 4, 8, 16)
HALO = 16
N_HEADS = 16
HEAD_DIM = 64
ATTN_WINDOW = 128
ROPE_THETA = 500000.0
ROT_DIM = 16
ADAM_LR = 0.001
ADAM_B1 = 0.9
ADAM_B2 = 0.999
ADAM_EPS = 1e-08
ADAM_WD = 0.01
ADAM_STEP = 10

TOK = 256
VMEM_LIMIT = 56 * 1024 * 1024


def _params(**kw):
    return pltpu.CompilerParams(vmem_limit_bytes=VMEM_LIMIT, **kw)


def _whole(shape):
    nd = len(shape)
    return pl.BlockSpec(shape, lambda *_: (0,) * nd)


def _rows(t, n):
    return pl.BlockSpec((t, n), lambda i: (i, 0))


ANY = pl.BlockSpec(memory_space=pl.ANY)

_G0 = 0.7978845608028654
_G1 = 0.044715


def _gelu(x):
    return 0.5 * x * (1.0 + jnp.tanh(_G0 * (x + _G1 * x * x * x)))


def _gelu_and_grad(x):
    x2 = x * x
    t = jnp.tanh(_G0 * (x + _G1 * x2 * x))
    half = 0.5 * (1.0 + t)
    return x * half, half + 0.5 * x * (1.0 - t * t) * (_G0 * (1.0 + 3.0 * _G1 * x2))


def _sigmoid(x):
    return 1.0 / (1.0 + jnp.exp(-x))


def _dot(a, b):
    return jnp.dot(a, b, preferred_element_type=F32)


def _dot_nt(a, b):
    return lax.dot_general(a, b, (((1,), (1,)), ((), ())), preferred_element_type=F32)


def _dot_tn(a, b):
    return lax.dot_general(a, b, (((0,), (0,)), ((), ())), preferred_element_type=F32)


def _rms_fwd(x, g):
    r = lax.rsqrt(jnp.mean(x * x, axis=-1, keepdims=True) + EPS)
    xh = x * r
    return r, xh, xh * g


def _rms_bwd(dy, g, r, xh):
    dxh = dy * g
    return r * (dxh - xh * jnp.mean(dxh * xh, axis=-1, keepdims=True))


def _band_matrices(t):
    r = np.arange(t)[:, None]
    j = np.arange(t + 2 * HALO)[None, :]
    fwd, bwd = [], []
    for w in POOL_WINDOWS:
        d = j - r - HALO
        fwd.append((d >= -(w // 2)) & (d < w // 2))
        bwd.append((d >= -(w // 2) + 1) & (d <= w // 2))
    return (jnp.asarray(np.stack(fwd), BF16), jnp.asarray(np.stack(bwd), BF16))


def _window_counts(i, t, s):
    tok = i * t + lax.broadcasted_iota(jnp.int32, (t, 1), 0)
    out = []
    for w in POOL_WINDOWS:
        cnt = jnp.minimum(tok + w // 2, s) - jnp.maximum(tok - w // 2, 0)
        out.append(cnt.astype(F32))
    return out


def _rope_tables(s):
    inv = np.float32(ROPE_THETA) ** (-np.arange(0, ROT_DIM, 2, dtype=np.float32) / np.float32(ROT_DIM))
    ang = np.arange(s, dtype=np.float32)[:, None] * inv.astype(np.float32)[None, :]
    cos, sin = np.cos(ang).astype(np.float32), np.sin(ang).astype(np.float32)
    z8 = np.zeros((s, 8), np.float32)
    z48 = np.zeros((s, HEAD_DIM - ROT_DIM), np.float32)
    c = np.concatenate([cos, cos, np.ones((s, HEAD_DIM - ROT_DIM), np.float32)], axis=1)
    s_lo = np.concatenate([z8, sin, z48], axis=1)
    s_hi = np.concatenate([-sin, z8, z48], axis=1)
    return tuple(jnp.asarray(np.concatenate([a, a], axis=1)) for a in (c, s_lo, s_hi))


def _rope(x, c, s_lo, s_hi):
    n = x.shape[1]
    reps = n // 128
    c, s_lo, s_hi = (jnp.tile(a, (1, reps)) for a in (c, s_lo, s_hi))
    return x * c + pltpu.roll(x, 8, 1) * s_lo + pltpu.roll(x, n - 8, 1) * s_hi


def _rope_t(dx, c, s_lo, s_hi):
    n = dx.shape[1]
    reps = n // 128
    c, s_lo, s_hi = (jnp.tile(a, (1, reps)) for a in (c, s_lo, s_hi))
    return dx * c + pltpu.roll(dx * s_lo, n - 8, 1) + pltpu.roll(dx * s_hi, 8, 1)


def _in_proj0_own(order, x, g0, w_own, rider):
    s = x.shape[0]
    n = w_own.shape[1]

    def body(order_ref, x_ref, g_ref, w_ref, h_ref, z_ref):
        _, _, h = _rms_fwd(x_ref[...], g_ref[...])
        h = h.astype(BF16)
        h_ref[...] = h
        z_ref[...] = _dot(h, w_ref[...]).astype(BF16)

    return pl.pallas_call(
        rider.carried_by(body, 4, 2, s // TOK), name="in_proj0_own",
        grid_spec=pltpu.PrefetchScalarGridSpec(
            num_scalar_prefetch=1, grid=(s // TOK,),
            in_specs=[pl.BlockSpec((TOK, D_MODEL), lambda i, o: (i, 0)), pl.BlockSpec((1, D_MODEL), lambda i, o: (0, 0)),
                      pl.BlockSpec(w_own.shape, lambda i, o: (0, 0))] + rider.in_specs,
            out_specs=[pl.BlockSpec((TOK, D_MODEL), lambda i, o: (i, 0)),
                       pl.BlockSpec((TOK, n), lambda i, o: (i, o[0]))] + rider.out_specs,
            scratch_shapes=rider.scratch),
        out_shape=[jax.ShapeDtypeStruct((s, D_MODEL), BF16), jax.ShapeDtypeStruct((s, 4 * n), BF16)] + rider.out_shape,
        input_output_aliases={4 + j: 2 + j for j in range(len(rider.fulls))},
        compiler_params=_params(),
    )(order, x, g0, w_own, *rider.fulls)


def _in_proj0_rest(order, h0, w_in0, z0, rider):
    s = h0.shape[0]
    tok = min(s, 4 * TOK)
    n_tiles = s // tok
    n = w_in0.shape[1] // 4

    def body(order_ref, h_ref, w_ref, z_in_ref, z_ref):
        z_ref[...] = _dot(h_ref[...], w_ref[...]).astype(BF16)

    return pl.pallas_call(
        rider.carried_by(body, 4, 1, 3 * n_tiles, lambda: pl.program_id(0) * n_tiles + pl.program_id(1)),
        name="in_proj0_rest",
        grid_spec=pltpu.PrefetchScalarGridSpec(
            num_scalar_prefetch=1, grid=(3, n_tiles),
            in_specs=[pl.BlockSpec((tok, D_MODEL), lambda k, i, o: (i, 0)),
                      pl.BlockSpec((w_in0.shape[0], n), lambda k, i, o: (0, o[1 + k])), ANY] + rider.in_specs,
            out_specs=[pl.BlockSpec((tok, n), lambda k, i, o: (i, o[1 + k]))] + rider.out_specs,
            scratch_shapes=rider.scratch),
        out_shape=[jax.ShapeDtypeStruct(z0.shape, BF16)] + rider.out_shape,
        input_output_aliases={3: 0, **{4 + j: 1 + j for j in range(len(rider.fulls))}},
        compiler_params=_params(),
    )(order, h0, w_in0, z0, *rider.fulls)


def _halo_specs(s, col_block, tok=TOK):
    per = tok // HALO
    last = s // HALO - 1
    prev = pl.BlockSpec((HALO, 1024), lambda i: (jnp.maximum(i * per - 1, 0), col_block))
    nxt = pl.BlockSpec((HALO, 1024), lambda i: (jnp.minimum((i + 1) * per, last), col_block))
    return prev, nxt


def _with_halo(i, n_tiles, prev_ref, cur, next_ref):
    prev = prev_ref[...]
    nxt = next_ref[...]
    prev = jnp.where(i > 0, prev, jnp.zeros_like(prev))
    nxt = jnp.where(i < n_tiles - 1, nxt, jnp.zeros_like(nxt))
    return jnp.concatenate([prev, cur, nxt], axis=0)


def _mixer_a(v1, gv, ws_ref, bsx):
    rv, vh, v2 = _rms_fwd(v1, gv)
    v2 = v2.astype(BF16)
    rows = []
    for c in range(v1.shape[0] // CHUNK):
        cols = [_dot(ws_ref[h], v2[c * CHUNK:(c + 1) * CHUNK, h * 256:(h + 1) * 256]) for h in range(4)]
        rows.append(jnp.concatenate(cols, axis=1) + bsx)
    return rv, vh, v2, jnp.concatenate(rows, axis=0)


def _mixer_b_pooled(bx, halo, band_ref, counts):
    out = []
    for g in range(4):
        win = _dot(band_ref[g], halo[:, g * 256:(g + 1) * 256])
        out.append(win / counts[g] - bx[:, g * 256:(g + 1) * 256])
    return out


def _mix0_fwd(x, z0, w_out0, gv, ws, bsx, wg, scale, band, rider):
    s = x.shape[0]
    n_tiles = s // TOK
    k = len(rider.fulls)

    def body(z_ref, zp_ref, zn_ref, x_ref, wout_ref, gv_ref, ws_ref, bsx_ref, wg_ref, sc_ref, band_ref,
             cat_ref, x1_ref):
        i = pl.program_id(0)
        au = z_ref[:, 0:1024].astype(F32)
        av = z_ref[:, 1024:2048].astype(F32)
        ag = z_ref[:, 2048:3072].astype(F32)
        _, _, _, mixed = _mixer_a(_gelu(av), gv_ref[...], ws_ref, bsx_ref[...])
        cat_ref[:, 0:1024] = (_gelu(au) * mixed * (ag * _sigmoid(ag))).astype(BF16)

        bx = z_ref[:, 3072:4096]
        bg = z_ref[:, 4096:5120].astype(F32)
        halo = _with_halo(i, n_tiles, zp_ref, bx, zn_ref)
        ps = _mixer_b_pooled(bx.astype(F32), halo, band_ref, _window_counts(i, TOK, s))
        pw = jnp.concatenate([_dot(ps[g].astype(BF16), wg_ref[g]) for g in range(4)], axis=1)
        cat_ref[:, 1024:2048] = (pw * sc_ref[...] * (bg * _sigmoid(bg))).astype(BF16)

        x1_ref[...] = x_ref[...] + _dot(cat_ref[...], wout_ref[...])

    prev, nxt = _halo_specs(s, 3)
    return pl.pallas_call(
        rider.carried_by(body, 11, 2, n_tiles), name="mix0_fwd", grid=(n_tiles,),
        in_specs=[_rows(TOK, 5120), prev, nxt, _rows(TOK, D_MODEL), _whole(w_out0.shape), _whole((1, 1024)),
                  _whole(ws.shape), _whole(bsx.shape), _whole(wg.shape), _whole((1, 1024)), _whole(band.shape)]
        + rider.in_specs,
        out_specs=[_rows(TOK, 2048), _rows(TOK, D_MODEL)] + rider.out_specs,
        out_shape=[jax.ShapeDtypeStruct((s, 2048), BF16), jax.ShapeDtypeStruct((s, D_MODEL), F32)] + rider.out_shape,
        input_output_aliases={11 + j: 2 + j for j in range(k)},
        scratch_shapes=rider.scratch,
        compiler_params=_params(),
    )(z0, z0, z0, x, w_out0, gv, ws, bsx, wg, scale, band, *rider.fulls)


def _in_proj1(x1, g1, w_in1, rope):
    s = x1.shape[0]

    def body(x_ref, g_ref, w_ref, c_ref, lo_ref, hi_ref, h_ref, q_ref, k_ref, v_ref, gate_ref):
        halves = [slice(r * TOK, (r + 1) * TOK) for r in range(tok // TOK)]
        hs = [_rms_fwd(x_ref[rows, :], g_ref[...])[2].astype(BF16) for rows in halves]
        for rows, h in zip(halves, hs):
            h_ref[rows, :] = h
        qs = [_dot(h, w_ref[:, 0:1024]) for h in hs]
        kvs = [_dot(h, w_ref[:, 1024:1536]) for h in hs]
        for rows, q, kv in zip(halves, qs, kvs):
            tabs = (c_ref[rows, :], lo_ref[rows, :], hi_ref[rows, :])
            q_ref[rows, :] = (_rope(q, *tabs) * Q_SCALE).astype(BF16)
            k_ref[rows, :] = _rope(kv[:, 0:256], *tabs).astype(BF16)
            v_ref[rows, :] = kv[:, 256:512].astype(BF16)
        for rows, h in zip(halves, hs):
            gate_ref[rows, :] = _dot(h, w_ref[:, 1536:2560]).astype(BF16)

    tok = min(s, 2 * TOK)
    tab = _rows(tok, 128)
    return pl.pallas_call(
        body, name="in_proj1", grid=(s // tok,),
        in_specs=[_rows(tok, D_MODEL), _whole((1, D_MODEL)), _whole(w_in1.shape), tab, tab, tab],
        out_specs=[_rows(tok, 1024), _rows(tok, 1024), _rows(tok, 256), _rows(tok, 256), _rows(tok, 1024)],
        out_shape=[jax.ShapeDtypeStruct((s, 1024), BF16), jax.ShapeDtypeStruct((s, 1024), BF16),
                   jax.ShapeDtypeStruct((s, 256), BF16), jax.ShapeDtypeStruct((s, 256), BF16),
                   jax.ShapeDtypeStruct((s, 1024), BF16)],
        compiler_params=_params(),
    )(x1, g1, w_in1, *rope)


QBLK = 128
KBLK = QBLK + 2 * ATTN_WINDOW
Q_SCALE = HEAD_DIM ** -0.5


def _block_bias(q0, s):
    r = lax.broadcasted_iota(jnp.int32, (QBLK, KBLK), 0)
    c = lax.broadcasted_iota(jnp.int32, (QBLK, KBLK), 1)
    kj = q0 - ATTN_WINDOW + c
    ok = (c >= r) & (c <= r + 2 * ATTN_WINDOW) & (kj >= 0) & (kj < s)
    return jnp.where(ok, 0.0, NEG_INF)


def _pair_operands(t):
    lane = lax.broadcasted_iota(jnp.int32, (1, 128), 1)
    first = lane < HEAD_DIM
    zero = jnp.zeros((KBLK, 128), BF16)
    out = []
    for j in range(2):
        slab = t[:, 128 * j:128 * (j + 1)]
        turned = pltpu.bitcast(pltpu.roll(pltpu.bitcast(slab, jnp.uint32), HEAD_DIM, 1), BF16)
        for own_first in (True, False):
            top = jnp.where(first, slab if own_first else turned, zero)
            bottom = jnp.where(first, zero, turned if own_first else slab)
            out.append(jnp.concatenate([top, bottom], axis=0))
    return out


def _attn_fwd(q, kpad, vpad, sink):
    s = q.shape[0]

    def body(sink_ref, q_ref, k_ref, v_ref, o_ref, lse_ref):
        i = pl.program_id(0)
        lane = lax.broadcasted_iota(jnp.int32, (1, 128), 1)
        for b in range(TOK // QBLK):
            rows = slice(b * QBLK, (b + 1) * QBLK)
            start = pl.multiple_of(i * TOK + b * QBLK, QBLK)
            k_bd = _pair_operands(k_ref[pl.ds(start, KBLK), :])
            v_bd = _pair_operands(v_ref[pl.ds(start, KBLK), :])
            bias = _block_bias(i * TOK + b * QBLK, s)
            pairs = range(N_HEADS // 2)
            sc4 = [_dot_nt(jnp.concatenate([q_ref[rows, 256 * g:256 * g + 128], q_ref[rows, 256 * g + 128:256 * (g + 1)]],
                                           axis=0), k_bd[g]) for g in range(4)]
            sc2 = [sc4[m // 2][(m % 2) * QBLK:(m % 2 + 1) * QBLK] for m in pairs]
            scs = [sc2[h // 2][:, (h % 2) * KBLK:(h % 2 + 1) * KBLK] + bias for h in range(N_HEADS)]
            ms = [jnp.maximum(jnp.max(scs[h], axis=-1, keepdims=True), sink_ref[h]) for h in range(N_HEADS)]
            es = [jnp.exp(scs[h] - ms[h]) for h in range(N_HEADS)]
            dens = [jnp.sum(es[h], axis=-1, keepdims=True) + jnp.exp(sink_ref[h] - ms[h]) for h in range(N_HEADS)]
            first = lane < HEAD_DIM
            e2 = [jnp.concatenate([es[2 * m].astype(BF16), es[2 * m + 1].astype(BF16)], axis=1) for m in pairs]
            o4 = [_dot(jnp.concatenate([e2[2 * g], e2[2 * g + 1]], axis=0), v_bd[g]) for g in range(4)]
            outs = [o4[m // 2][(m % 2) * QBLK:(m % 2 + 1) * QBLK]
                    * jnp.where(first, 1.0 / dens[2 * m], 1.0 / dens[2 * m + 1]) for m in pairs]
            o_ref[rows, :] = jnp.concatenate(outs, axis=1).astype(BF16)
            lse = jnp.zeros((QBLK, 128), F32)
            for h in range(N_HEADS):
                lse = lse + jnp.where(lane == h, ms[h] + jnp.log(dens[h]), 0.0)
            lse_ref[rows, :] = lse

    return pl.pallas_call(
        body, name="attn_fwd", grid=(s // TOK,),
        in_specs=[pl.BlockSpec(memory_space=pltpu.SMEM), _rows(TOK, 1024), _whole(kpad.shape), _whole(vpad.shape)],
        out_specs=[_rows(TOK, 1024), _rows(TOK, 128)],
        out_shape=[jax.ShapeDtypeStruct((s, 1024), BF16), jax.ShapeDtypeStruct((s, 128), F32)],
        compiler_params=_params(),
    )(sink, q, kpad, vpad)


def _tail(x1, o, gate, target, w_out1, gf):
    s = x1.shape[0]

    def body(x1_ref, o_ref, gate_ref, t_ref, w_ref, gf_ref, y1_ref, dx2_ref, do_ref, dgate_ref, loss_ref, gfn_ref,
             dx2h_ref):
        i = pl.program_id(0)

        @pl.when(i == 0)
        def _():
            loss_ref[...] = jnp.zeros_like(loss_ref)
            gfn_ref[...] = jnp.zeros_like(gfn_ref)

        halves = [slice(r * TOK, (r + 1) * TOK) for r in range(tok // TOK)]
        gf = gf_ref[...]
        gs = [gate_ref[rows, :].astype(F32) for rows in halves]
        sgs = [_sigmoid(g) for g in gs]
        sils = [g * sg for g, sg in zip(gs, sgs)]
        os_ = [o_ref[rows, :].astype(F32) for rows in halves]
        y1s = [(o * sil).astype(BF16) for o, sil in zip(os_, sils)]
        for rows, y1 in zip(halves, y1s):
            y1_ref[rows, :] = y1
        x2s = [x1_ref[rows, :] + _dot(y1, w_ref[...]) for rows, y1 in zip(halves, y1s)]
        dx2hs = []
        for rows, x2 in zip(halves, x2s):
            r, xh, out = _rms_fwd(x2, gf)
            diff = out - t_ref[rows, :]
            loss_ref[...] += jnp.sum(diff * diff, axis=0, keepdims=True) * (0.5 / D_MODEL)
            dout = diff * (1.0 / D_MODEL)
            gfn_ref[...] += jnp.sum(dout * xh, axis=0, keepdims=True)
            dx2 = _rms_bwd(dout, gf, r, xh)
            dx2_ref[rows, :] = dx2
            dx2hs.append(dx2.astype(BF16))
            dx2h_ref[rows, :] = dx2hs[-1]
        dy1s = [_dot_nt(dx2h, w_ref[...]) for dx2h in dx2hs]
        for rows, dy1, sil, o, sg, g in zip(halves, dy1s, sils, os_, sgs, gs):
            do_ref[rows, :] = (dy1 * sil).astype(BF16)
            dgate_ref[rows, :] = (dy1 * o * (sg * (1.0 + g * (1.0 - sg)))).astype(BF16)

    tok = min(s, 2 * TOK)
    row = _rows(tok, 1024)
    acc = _whole((1, 1024))
    return pl.pallas_call(
        body, name="tail", grid=(s // tok,),
        in_specs=[row, row, row, row, _whole(w_out1.shape), acc],
        out_specs=[row, row, row, row, acc, acc, row],
        out_shape=[jax.ShapeDtypeStruct((s, 1024), BF16), jax.ShapeDtypeStruct((s, 1024), F32),
                   jax.ShapeDtypeStruct((s, 1024), BF16), jax.ShapeDtypeStruct((s, 1024), BF16),
                   jax.ShapeDtypeStruct((1, 1024), F32), jax.ShapeDtypeStruct((1, 1024), F32),
                   jax.ShapeDtypeStruct((s, 1024), BF16)],
        compiler_params=_params(),
    )(x1, o, gate, target, w_out1, gf)


def _attn_bwd(q, kpad, vpad, sink, o, lse, do, rope):
    s = q.shape[0]
    pad_t = (kpad.shape[1], kpad.shape[0])

    def body(sink_ref, q_ref, k_ref, v_ref, o_ref, lse_ref, do_ref, c_ref, lo_ref, hi_ref,
             dq_ref, dk_ref, dv_ref, ds_ref):
        i = pl.program_id(0)

        @pl.when(i == 0)
        def _():
            dk_ref[...] = jnp.zeros_like(dk_ref)
            dv_ref[...] = jnp.zeros_like(dv_ref)
            ds_ref[...] = jnp.zeros_like(ds_ref)

        lane = lax.broadcasted_iota(jnp.int32, (1, 128), 1)
        dsink = jnp.zeros((1, 128), F32)
        for b in range(TOK // QBLK):
            rows = slice(b * QBLK, (b + 1) * QBLK)
            start = pl.multiple_of(i * TOK + b * QBLK, QBLK)
            k_bd = _pair_operands(k_ref[pl.ds(start, KBLK), :])
            v_bd = _pair_operands(v_ref[pl.ds(start, KBLK), :])
            bias = _block_bias(i * TOK + b * QBLK, s)
            lanes_of = (lane < HEAD_DIM, lane >= HEAD_DIM)
            half = lambda t, j: t[:, j * KBLK:(j + 1) * KBLK]
            dqs, dks, dvs = [], [], []
            for g in range(4):
                pairs = (2 * g, 2 * g + 1)
                qs = {m: q_ref[rows, 128 * m:128 * (m + 1)] for m in pairs}
                dos = {m: do_ref[rows, 128 * m:128 * (m + 1)] for m in pairs}
                lses = {h: lse_ref[rows, h:h + 1] for h in range(4 * g, 4 * g + 4)}
                stacked = lambda parts: jnp.concatenate([parts[m] for m in pairs], axis=0)
                unstack = lambda t: {m: t[j * QBLK:(j + 1) * QBLK] for j, m in enumerate(pairs)}
                sc2 = unstack(_dot_nt(stacked(qs), k_bd[g]))
                ps = {2 * m + j: jnp.exp(half(sc2[m], j) + bias - lses[2 * m + j]) for m in pairs for j in range(2)}
                prods = {m: dos[m].astype(F32) * o_ref[rows, 128 * m:128 * (m + 1)].astype(F32) for m in pairs}
                deltas = {2 * m + j: jnp.sum(jnp.where(lanes_of[j], prods[m], 0.0), axis=-1, keepdims=True)
                          for m in pairs for j in range(2)}
                for h in range(4 * g, 4 * g + 4):
                    dsink = dsink + jnp.where(
                        lane == h, -jnp.sum(jnp.exp(sink_ref[h] - lses[h]) * deltas[h], axis=0, keepdims=True), 0.0)
                dp2 = unstack(_dot_nt(stacked(dos), v_bd[g]))
                ds2 = {m: jnp.concatenate(
                    [(ps[2 * m + j] * (half(dp2[m], j) - deltas[2 * m + j])).astype(BF16) for j in range(2)], axis=1)
                    for m in pairs}
                p2 = {m: jnp.concatenate([ps[2 * m].astype(BF16), ps[2 * m + 1].astype(BF16)], axis=1) for m in pairs}
                diag = lambda t: t[0:HEAD_DIM, 0:KBLK] + t[HEAD_DIM:128, KBLK:2 * KBLK]
                dvs.append(diag(_dot_tn(stacked(dos), stacked(p2))))
                dks.append(diag(_dot_tn(stacked(qs), stacked(ds2))))
                dq2 = unstack(_dot(stacked(ds2), k_bd[g]) * Q_SCALE)
                dqs += [dq2[m] for m in pairs]
            dq = jnp.concatenate(dqs, axis=1)
            dq_ref[rows, :] = _rope_t(dq, c_ref[rows, :], lo_ref[rows, :], hi_ref[rows, :]).astype(BF16)
            dk_ref[:, pl.ds(start, KBLK)] += jnp.concatenate(dks, axis=0)
            dv_ref[:, pl.ds(start, KBLK)] += jnp.concatenate(dvs, axis=0)
        ds_ref[...] += dsink

    row = _rows(TOK, 1024)
    tab = _rows(TOK, 128)
    pad = _whole(kpad.shape)
    return pl.pallas_call(
        body, name="attn_bwd", grid=(s // TOK,),
        in_specs=[pl.BlockSpec(memory_space=pltpu.SMEM), row, pad, pad, row, tab, row, tab, tab, tab],
        out_specs=[row, _whole(pad_t), _whole(pad_t), _whole((1, 128))],
        out_shape=[jax.ShapeDtypeStruct((s, 1024), BF16), jax.ShapeDtypeStruct(pad_t, F32),
                   jax.ShapeDtypeStruct(pad_t, F32), jax.ShapeDtypeStruct((1, 128), F32)],
        compiler_params=_params(),
    )(sink, q, kpad, vpad, o, lse, do, *rope)


def _in_proj1_bwd(dq, dk_t, dv_t, dgate, x1, dx2, g1, w_in1, rope):
    s = x1.shape[0]
    tok = min(s, 2 * TOK)
    n_sub = tok // ATTN_WINDOW

    def body(*refs):
        dq_ref = refs[0]
        dk_refs, dv_refs = refs[1:1 + n_sub], refs[1 + n_sub:1 + 2 * n_sub]
        (dgate_ref, x1_ref, dx2_ref, g_ref, w_ref, c_ref, lo_ref, hi_ref,
         dz_ref, dx1_ref, gn_ref, dx1h_ref) = refs[1 + 2 * n_sub:]

        @pl.when(pl.program_id(0) == 0)
        def _():
            gn_ref[...] = jnp.zeros_like(gn_ref)

        halves = [slice(r * TOK, (r + 1) * TOK) for r in range(tok // TOK)]
        per = TOK // ATTN_WINDOW
        g = g_ref[...]
        for r, rows in enumerate(halves):
            dk = jnp.concatenate([ref[...] for ref in dk_refs[r * per:(r + 1) * per]], axis=1).T
            dv = jnp.concatenate([ref[...] for ref in dv_refs[r * per:(r + 1) * per]], axis=1).T
            dz_ref[rows, 0:1024] = dq_ref[rows, :]
            dz_ref[rows, 1024:1280] = _rope_t(dk, c_ref[rows, :], lo_ref[rows, :], hi_ref[rows, :]).astype(BF16)
            dz_ref[rows, 1280:1536] = dv.astype(BF16)
            dz_ref[rows, 1536:2560] = dgate_ref[rows, :]
        dhs = [_dot_nt(dz_ref[rows, :], w_ref[...]) for rows in halves]
        for rows, dh in zip(halves, dhs):
            r, xh, _ = _rms_fwd(x1_ref[rows, :], g)
            gn_ref[...] += jnp.sum(dh * xh, axis=0, keepdims=True)
            dx1 = dx2_ref[rows, :] + _rms_bwd(dh, g, r, xh)
            dx1_ref[rows, :] = dx1
            dx1h_ref[rows, :] = dx1.astype(BF16)

    row = _rows(tok, 1024)
    subs = [pl.BlockSpec((256, ATTN_WINDOW), lambda i, j=j: (0, n_sub * i + 1 + j)) for j in range(n_sub)]
    tab = _rows(tok, 128)
    acc = _whole((1, 1024))
    return pl.pallas_call(
        body, name="in_proj1_bwd", grid=(s // tok,),
        in_specs=[row] + subs + subs + [row, row, row, acc, _whole(w_in1.shape), tab, tab, tab],
        out_specs=[_rows(tok, 2560), row, acc, row],
        out_shape=[jax.ShapeDtypeStruct((s, 2560), BF16), jax.ShapeDtypeStruct((s, 1024), F32),
                   jax.ShapeDtypeStruct((1, 1024), F32), jax.ShapeDtypeStruct((s, 1024), BF16)],
        compiler_params=_params(),
    )(dq, *[dk_t] * n_sub, *[dv_t] * n_sub, dgate, x1, dx2, g1, w_in1, *rope)


def _mix0_bwd(dx1, z0, w_out0, gv, ws, bsx, wg, scale, band, rider):
    s = dx1.shape[0]
    n_tiles = s // TOK

    def body(dx1h_ref, z_ref, zp_ref, zn_ref, wout_ref, gv_ref, ws_ref, bsx_ref, wg_ref, sc_ref, band_ref,
             dz_ref, dpn_ref, dws_ref, dbs_ref, dgv_ref, dsc_ref, dwg_ref):
        i = pl.program_id(0)
        dz_ref[:, 3072:4096] = jnp.zeros((TOK, 1024), BF16)

        @pl.when(i == 0)
        def _():
            for ref in (dws_ref, dbs_ref, dgv_ref, dsc_ref, dwg_ref):
                ref[...] = jnp.zeros_like(ref)

        dcat = _dot_nt(dx1h_ref[...], wout_ref[...])
        dya = dcat[:, 0:1024]
        dyb = dcat[:, 1024:2048]

        au = z_ref[:, 0:1024].astype(F32)
        av = z_ref[:, 1024:2048].astype(F32)
        ag = z_ref[:, 2048:3072].astype(F32)
        gv = gv_ref[...]
        u, du = _gelu_and_grad(au)
        v1, dv1 = _gelu_and_grad(av)
        rv, vh, v2, mixed = _mixer_a(v1, gv, ws_ref, bsx_ref[...])
        sg = _sigmoid(ag)
        sil = ag * sg
        dz_ref[:, 2048:3072] = (dya * u * mixed * (sg * (1.0 + ag * (1.0 - sg)))).astype(BF16)
        dz_ref[:, 0:1024] = (dya * mixed * sil * du).astype(BF16)
        dmixed = dya * u * sil
        lane = lax.broadcasted_iota(jnp.int32, (1, 128), 1)
        dm16 = dmixed.astype(BF16)
        dv2_rows = []
        for c in range(TOK // CHUNK):
            rows = slice(c * CHUNK, (c + 1) * CHUNK)
            cols_out = []
            for h in range(4):
                cols = slice(h * 256, (h + 1) * 256)
                dws_ref[h] += _dot_nt(dm16[rows, cols], v2[rows, cols])
                dbs_ref[...] += jnp.where(lane == h, jnp.sum(dmixed[rows, cols], axis=-1, keepdims=True), 0.0)
                cols_out.append(_dot_tn(ws_ref[h], dm16[rows, cols]))
            dv2_rows.append(jnp.concatenate(cols_out, axis=1))
        dv2 = jnp.concatenate(dv2_rows, axis=0)
        dgv_ref[...] += jnp.sum(dv2 * vh, axis=0, keepdims=True)
        dz_ref[:, 1024:2048] = (_rms_bwd(dv2, gv, rv, vh) * dv1).astype(BF16)

        bx = z_ref[:, 3072:4096]
        bg = z_ref[:, 4096:5120].astype(F32)
        counts = _window_counts(i, TOK, s)
        halo = _with_halo(i, n_tiles, zp_ref, bx, zn_ref)
        ps = [p.astype(BF16) for p in _mixer_b_pooled(bx.astype(F32), halo, band_ref, counts)]
        pw = jnp.concatenate([_dot(ps[g], wg_ref[g]) for g in range(4)], axis=1)
        sgb = _sigmoid(bg)
        sc = sc_ref[...]
        dz_ref[:, 4096:5120] = (dyb * pw * sc * (sgb * (1.0 + bg * (1.0 - sgb)))).astype(BF16)
        dys = dyb * (bg * sgb)
        dsc_ref[...] += jnp.sum(dys * pw, axis=0, keepdims=True)
        dpw = (dys * sc).astype(BF16)
        for g in range(4):
            cols = slice(g * 256, (g + 1) * 256)
            dwg_ref[g] += _dot_tn(ps[g], dpw[:, cols])
            dpn_ref[:, cols] = (_dot_nt(dpw[:, cols], wg_ref[g]) / counts[g]).astype(BF16)

    prev, nxt = _halo_specs(s, 3)
    row = _rows(TOK, 1024)
    vec = _whole((1, 1024))
    return pl.pallas_call(
        rider.carried_by(body, 11, 7, n_tiles), name="mix0_bwd", grid=(n_tiles,),
        in_specs=[row, _rows(TOK, 5120), prev, nxt, _whole(w_out0.shape), vec, _whole(ws.shape),
                  _whole(bsx.shape), _whole(wg.shape), vec, _whole(band.shape)] + rider.in_specs,
        out_specs=[_rows(TOK, 5120), row, _whole((4, 128, 128)), _whole((128, 128)), vec, vec,
                   _whole((4, 256, 256))] + rider.out_specs,
        out_shape=[jax.ShapeDtypeStruct((s, 5120), BF16), jax.ShapeDtypeStruct((s, 1024), BF16),
                   jax.ShapeDtypeStruct((4, 128, 128), F32), jax.ShapeDtypeStruct((128, 128), F32),
                   jax.ShapeDtypeStruct((1, 1024), F32), jax.ShapeDtypeStruct((1, 1024), F32),
                   jax.ShapeDtypeStruct((4, 256, 256), F32)] + rider.out_shape,
        scratch_shapes=rider.scratch,
        compiler_params=_params(),
    )(dx1, z0, z0, z0, w_out0, gv, ws, bsx, wg, scale, band, *rider.parts)


def _fill_pooled_grad(dz0, dpn):
    s = dpn.shape[0]
    tok = min(s, 2 * TOK)
    n_tiles = s // tok
    band_t = _band_matrices(tok)[1]

    def body(dz_in_ref, dpn_ref, dpp_ref, dpx_ref, band_ref, dbx_ref):
        i = pl.program_id(0)
        dpn = dpn_ref[...]
        halo = _with_halo(i, n_tiles, dpp_ref, dpn, dpx_ref)
        counts = _window_counts(i, tok, s)
        for g in range(4):
            cols = slice(g * 256, (g + 1) * 256)
            dbx = _dot(band_ref[g], halo[:, cols]) - dpn[:, cols].astype(F32) * counts[g]
            dbx_ref[:, cols] = dbx.astype(BF16)

    prev, nxt = _halo_specs(s, 0, tok)
    return pl.pallas_call(
        body, name="fill_pooled_grad", grid=(n_tiles,),
        in_specs=[ANY, _rows(tok, 1024), prev, nxt, _whole(band_t.shape)],
        out_specs=pl.BlockSpec((tok, 1024), lambda i: (i, 3)),
        out_shape=jax.ShapeDtypeStruct(dz0.shape, BF16),
        input_output_aliases={0: 0},
        compiler_params=_params(),
    )(dz0, dpn, dpn, dpn, band_t)


def _in_proj0_bwd(dz0, x, dx1, g0, w_in0, rider):
    s = x.shape[0]
    tok = min(s, 2 * TOK)
    n_tiles = s // tok

    def body(dz_ref, x_ref, dx1_ref, g_ref, w_ref, dx_ref, gn_ref):
        i = pl.program_id(0)

        @pl.when(i == 0)
        def _():
            gn_ref[...] = jnp.zeros_like(gn_ref)

        halves = [slice(r * TOK, (r + 1) * TOK) for r in range(tok // TOK)]
        g0v = g_ref[...]
        dhs = [_dot_nt(dz_ref[rows, :], w_ref[...]) for rows in halves]
        for rows, dh in zip(halves, dhs):
            r, xh, _ = _rms_fwd(x_ref[rows, :], g0v)
            gn_ref[...] += jnp.sum(dh * xh, axis=0, keepdims=True)
            dx_ref[rows, :] = dx1_ref[rows, :] + _rms_bwd(dh, g0v, r, xh)

    row = _rows(tok, 1024)
    vec = _whole((1, 1024))
    return pl.pallas_call(
        rider.carried_by(body, 5, 2, n_tiles), name="in_proj0_bwd", grid=(n_tiles,),
        in_specs=[_rows(tok, 5120), row, row, vec, _whole(w_in0.shape)] + rider.in_specs,
        out_specs=[row, vec] + rider.out_specs,
        out_shape=[jax.ShapeDtypeStruct((s, 1024), F32), jax.ShapeDtypeStruct((1, 1024), F32)] + rider.out_shape,
        scratch_shapes=rider.scratch,
        compiler_params=_params(),
    )(dz0, x, dx1, g0, w_in0, *rider.parts)


def _weight_grad(a, b, n_blocks, split, name, rider=None):
    s, k = a.shape
    n = b.shape[1]
    tn = n // n_blocks
    w = tn // split
    ts = min(s, 1024)

    def body(a_ref, b_ref, o_ref):
        @pl.when(pl.program_id(1) == 0)
        def _():
            o_ref[...] = jnp.zeros_like(o_ref)

        res = _dot_tn(a_ref[...], b_ref[...])
        for q in range(split):
            o_ref[q] += res[:, q * w:(q + 1) * w]

    in_specs = [pl.BlockSpec((ts, k), lambda j, t: (t, 0)), pl.BlockSpec((ts, tn), lambda j, t: (t, j))]
    out_spec = pl.BlockSpec((split, k, w), lambda j, t: (j, 0, 0))
    out_shape = jax.ShapeDtypeStruct((n_blocks * split, k, w), F32)
    if rider is None:
        return pl.pallas_call(body, name=name, grid=(n_blocks, s // ts), in_specs=in_specs, out_specs=out_spec,
                              out_shape=out_shape, compiler_params=_params())(a, b)
    steps = s // ts
    return pl.pallas_call(
        rider.carried_by(body, 2, 1, n_blocks * steps, lambda: pl.program_id(0) * steps + pl.program_id(1)),
        name=name, grid=(n_blocks, steps), in_specs=in_specs + rider.in_specs,
        out_specs=[out_spec] + rider.out_specs, out_shape=[out_shape] + rider.out_shape,
        scratch_shapes=rider.scratch, compiler_params=_params(),
    )(a, b, *rider.parts)


def _row_tile(rows, cols):
    t = rows
    while t * cols * 4 > (1 << 21) and t % 32 == 0:
        t //= 2
    return t


def _add_sibling(where, g, theirs, name):
    _, _, rows, cols = g.shape
    t = _row_tile(rows, cols)

    def body(where_ref, g_ref, t_ref, o_ref):
        o_ref[...] = (g_ref[...] + t_ref[...]).astype(BF16)

    spec = pl.BlockSpec((None, t, cols), lambda s, i, p: (s, i, 0))
    return pl.pallas_call(
        body, name=name, out_shape=jax.ShapeDtypeStruct((4, rows, cols), BF16),
        grid_spec=pltpu.PrefetchScalarGridSpec(
            num_scalar_prefetch=1, grid=(4, rows // t),
            in_specs=[pl.BlockSpec((None, None, t, cols), lambda s, i, p: (s, p[1], i, 0)), spec], out_specs=spec),
        compiler_params=_params())(where, g, theirs)


def _sum_chips(where, g, theirs, slots, name):
    _, _, rows, cols = g.shape
    t = _row_tile(rows, cols)

    def body(where_ref, g_ref, t_ref, s_ref, o_ref):
        me = where_ref[0]
        own = g_ref[...] + t_ref[...]
        acc = jnp.where(me == 0, own, s_ref[0].astype(F32))
        for k in range(1, 4):
            acc = acc + jnp.where(me == k, own, s_ref[k].astype(F32))
        o_ref[...] = acc

    return pl.pallas_call(
        body, name=name, out_shape=jax.ShapeDtypeStruct((rows, cols), F32),
        grid_spec=pltpu.PrefetchScalarGridSpec(
            num_scalar_prefetch=1, grid=(rows // t,),
            in_specs=[pl.BlockSpec((None, None, t, cols), lambda i, p: (p[0], p[1], i, 0)),
                      pl.BlockSpec((None, t, cols), lambda i, p: (p[0], i, 0)),
                      pl.BlockSpec((4, t, cols), lambda i, p: (0, i, 0))],
            out_specs=pl.BlockSpec((t, cols), lambda i, p: (i, 0))),
        compiler_params=_params())(where, g, theirs, slots)


def _adamw_halves(where, w, own, theirs, m, v, name):
    rows, cols = own.shape
    t = _row_tile(rows, cols)
    per = rows // t

    def body(where_ref, w_ref, own_ref, th_ref, m_ref, v_ref, g_ref, d_ref, nm_ref, nv_ref):
        g = jnp.where(pl.program_id(0) == where_ref[1], own_ref[...], th_ref[...])
        g_ref[...] = g
        m2 = ADAM_B1 * m_ref[...] + (1.0 - ADAM_B1) * g
        v2 = ADAM_B2 * v_ref[...] + (1.0 - ADAM_B2) * (g * g)
        m_hat = m2 / (1.0 - ADAM_B1 ** ADAM_STEP)
        v_hat = v2 / (1.0 - ADAM_B2 ** ADAM_STEP)
        d_ref[...] = -ADAM_LR * (m_hat / (jnp.sqrt(v_hat) + ADAM_EPS) + ADAM_WD * w_ref[...])
        nm_ref[...] = m2
        nv_ref[...] = v2

    full = pl.BlockSpec((t, cols), lambda h, i, p: (h * per + i, 0))
    half = pl.BlockSpec((t, cols), lambda h, i, p: (i, 0))
    shp = jax.ShapeDtypeStruct(w.shape, F32)
    return pl.pallas_call(
        body, name=name, out_shape=[shp] * 4,
        grid_spec=pltpu.PrefetchScalarGridSpec(
            num_scalar_prefetch=1, grid=(2, per), in_specs=[full, half, half, full, full], out_specs=[full] * 4),
        compiler_params=_params())(where, w, own, theirs, m, v)


def _place_shard(where, w, cut, name):
    if cut.kind == "cols":
        r, n = cut.full_shape
        blk, grid = (256, n // 4), (r // 256,)
        src_map, dst_map = (lambda i, p: (i, 0)), (lambda i, p: (i, p[0]))
    elif cut.kind == "rows":
        r, n = cut.full_shape
        per = r // 4 // 256
        blk, grid = (256, n), (per,)
        src_map, dst_map = (lambda i, p: (i, 0)), (lambda i, p: (p[0] * per + i, 0))
    else:
        g, r, n = cut.full_shape
        blk, grid = (g, r // 4, n), (1,)
        src_map, dst_map = (lambda i, p: (0, 0, 0)), (lambda i, p: (0, p[0], 0))

    def body(where_ref, w_ref, o_ref):
        o_ref[...] = w_ref[...].astype(BF16)

    return pl.pallas_call(
        body, name=name, out_shape=jax.ShapeDtypeStruct(cut.full_shape, BF16),
        grid_spec=pltpu.PrefetchScalarGridSpec(
            num_scalar_prefetch=1, grid=grid, in_specs=[pl.BlockSpec(blk, src_map)],
            out_specs=pl.BlockSpec(blk, dst_map)),
        compiler_params=_params())(where, w)


def _sum_small(first, second, third):
    rows = second.shape[1]

    def body(a_ref, b_ref, c_ref, o_ref):
        top = a_ref[0] + c_ref[0] + b_ref[0, 0:8]
        rest = b_ref[0, 8:rows]
        for k in range(1, 8):
            top = top + (a_ref[k] + c_ref[k] + b_ref[k, 0:8])
            rest = rest + b_ref[k, 8:rows]
        o_ref[0:8] = top
        o_ref[8:rows] = rest

    return pl.pallas_call(
        body, name="sum_small", in_specs=[_whole(first.shape), _whole(second.shape), _whole(third.shape)],
        out_specs=_whole(second.shape[1:]), out_shape=jax.ShapeDtypeStruct(second.shape[1:], F32),
        compiler_params=_params())(first, second, third)


def _adamw(w, g, m, v, name):
    rows, cols = w.shape
    t = _row_tile(rows, cols)

    def body(w_ref, g_ref, m_ref, v_ref, d_ref, nm_ref, nv_ref):
        g = g_ref[...]
        m2 = ADAM_B1 * m_ref[...] + (1.0 - ADAM_B1) * g
        v2 = ADAM_B2 * v_ref[...] + (1.0 - ADAM_B2) * (g * g)
        m_hat = m2 / (1.0 - ADAM_B1 ** ADAM_STEP)
        v_hat = v2 / (1.0 - ADAM_B2 ** ADAM_STEP)
        d_ref[...] = -ADAM_LR * (m_hat / (jnp.sqrt(v_hat) + ADAM_EPS) + ADAM_WD * w_ref[...])
        nm_ref[...] = m2
        nv_ref[...] = v2

    spec = pl.BlockSpec((t, cols), lambda i: (i, 0))
    shp = jax.ShapeDtypeStruct(w.shape, F32)
    return pl.pallas_call(body, name=name, grid=(rows // t,), in_specs=[spec] * 4, out_specs=[spec] * 3,
                          out_shape=[shp] * 3, compiler_params=_params())(w, g, m, v)


def _place():
    x, y, c = lax.axis_index("x"), lax.axis_index("y"), lax.axis_index("c")
    chips = [(1 - x, y), (x, 1 - y), (1 - x, 1 - y)]
    return x, y, c, chips


class _Sharded:
    def __init__(self, kind, full_shape):
        self.kind = kind
        self.full_shape = full_shape

    def in_full(self, ref, s, h):
        if self.kind == "cols":
            r, n = self.full_shape
            return ref.at[pl.ds(h * (r // 2), r // 2), pl.ds(pl.multiple_of(s * (n // 4), 128), n // 4)]
        if self.kind == "rows":
            r, _ = self.full_shape
            return ref.at[pl.ds(pl.multiple_of(s * (r // 4) + h * (r // 8), 8), r // 8), :]
        g, r, _ = self.full_shape
        return ref.at[pl.ds(h * (g // 2), g // 2), pl.ds(pl.multiple_of(s * (r // 4), 16), r // 4), :]


def _remote(src, dst, send_sem, recv_sem, to):
    return pltpu.make_async_remote_copy(src_ref=src, dst_ref=dst, send_sem=send_sem, recv_sem=recv_sem,
                                        device_id=to, device_id_type=MESH)


def _start_remote(src, dst, send_sem, recv_sem, to):
    cp = _remote(src, dst, send_sem, recv_sem, to)
    cp.start()
    return cp


class _Gather:
    def __init__(self, fulls, cuts):
        n = len(fulls)
        self.fulls, self.cuts = list(fulls), list(cuts)
        self.in_specs = [ANY] * n
        self.out_specs = [ANY] * n
        self.out_shape = [jax.ShapeDtypeStruct(cut.full_shape, BF16) for cut in cuts]
        self.scratch = [pltpu.SemaphoreType.DMA((6 * n,)), pltpu.SemaphoreType.DMA((6 * n,))]

    def _step(self, step, src, out, send_sems, recv_sems):
        n, cuts = len(self.fulls), self.cuts
        x, y, c, chips = _place()
        me = 2 * x + y

        def ends(w, s, h, from_src):
            dst = cuts[w].in_full(out[w], s, h)
            return (cuts[w].in_full(src[w], s, h) if from_src else dst), dst

        for w in range(n):
            for j, chip in enumerate(chips):
                s = 2 * chip[0] + chip[1]
                k, k2 = 3 * w + j, 3 * n + 3 * w + j
                if step == "send":
                    _start_remote(*ends(w, me, c, True), send_sems.at[k], recv_sems.at[k], (*chip, c))
                elif step == "pass_on":
                    _remote(*ends(w, s, c, False), send_sems.at[k], recv_sems.at[k], (x, y, c)).wait_recv()
                    _start_remote(*ends(w, s, c, False), send_sems.at[k2], recv_sems.at[k2], (x, y, 1 - c))
                else:
                    _remote(*ends(w, s, 1 - c, False), send_sems.at[k2], recv_sems.at[k2], (x, y, c)).wait_recv()
                    _remote(*ends(w, me, c, True), send_sems.at[k], recv_sems.at[k], (x, y, c)).wait_send()
                    _remote(*ends(w, s, c, False), send_sems.at[k2], recv_sems.at[k2], (x, y, c)).wait_send()

    def carried_by(self, body, n_in, n_out, n_steps, step_index=lambda: pl.program_id(0)):
        k = len(self.fulls)

        def carrier(*refs):
            ins, src = refs[:n_in], refs[n_in:n_in + k]
            outs, out = refs[n_in + k:n_in + k + n_out], refs[n_in + k + n_out:n_in + 2 * k + n_out]
            sems = refs[n_in + 2 * k + n_out:]
            for step, at in (("send", 0), ("pass_on", 3 * n_steps // 4), ("finish", n_steps - 1)):
                if step == "finish":
                    body(*ins, *outs)

                @pl.when(step_index() == at)
                def _():
                    self._step(step, src, out, *sems)

        return carrier


def _exchange_halves(grads, name):
    n = len(grads)

    def body(*refs):
        g = refs[:n]
        theirs = refs[n:2 * n]
        send_sems, recv_sems = refs[2 * n:]
        x, y, c, _ = _place()
        sends = [_start_remote(g[w].at[:, 1 - c], theirs[w], send_sems.at[w], recv_sems.at[w], (x, y, 1 - c))
                 for w in range(n)]
        for w in range(n):
            _remote(g[w].at[:, 1 - c], theirs[w], send_sems.at[w], recv_sems.at[w], (x, y, c)).wait_recv()
        for cp in sends:
            cp.wait_send()

    return pl.pallas_call(
        body, name=name,
        in_specs=[ANY] * n, out_specs=[ANY] * n,
        out_shape=[jax.ShapeDtypeStruct((4,) + g.shape[2:], F32) for g in grads],
        scratch_shapes=[pltpu.SemaphoreType.DMA((n,)), pltpu.SemaphoreType.DMA((n,))],
        compiler_params=pltpu.CompilerParams(has_side_effects=True),
    )(*grads)


def _gather_small(small_ref, gathered, send_sems, recv_sems, first_sem, local_sem, start):
    x, y, c, _ = _place()
    me = 4 * x + 2 * y + c
    flips = [(fx, fy, fc) for fx in range(2) for fy in range(2) for fc in range(2)][1:]
    own = pltpu.make_async_copy(small_ref, gathered.at[me], local_sem)
    if start:
        own.start()
    else:
        own.wait()
    for k, (fx, fy, fc) in enumerate(flips):
        peer = (x + fx - 2 * x * fx, y + fy - 2 * y * fy, c + fc - 2 * c * fc)
        sems = (send_sems.at[first_sem + k], recv_sems.at[first_sem + k])
        if start:
            _start_remote(small_ref, gathered.at[me], *sems, peer)
        else:
            cp = _remote(small_ref, gathered.at[4 * peer[0] + 2 * peer[1] + peer[2]], *sems, (x, y, c))
            cp.wait_recv()
            cp.wait_send()


class _Rider:
    def __init__(self, kind, parts, small=None):
        n = len(parts)
        self.kind, self.n = kind, n
        self.per = 3 if kind == "scatter" else 1
        self.parts = list(parts) + ([] if small is None else [small])
        k = len(self.parts)
        self.in_specs = [ANY] * k
        self.out_specs = [ANY] * k
        dtype = lambda a: a.dtype if kind == "scatter" else F32
        self.out_shape = [jax.ShapeDtypeStruct((4,) + a.shape[-2:], dtype(a)) for a in parts]
        if small is not None:
            self.out_shape.append(jax.ShapeDtypeStruct((8,) + small.shape, small.dtype))
        self.scratch = [pltpu.SemaphoreType.DMA((self.per * n + 7,)), pltpu.SemaphoreType.DMA((self.per * n + 7,)),
                        pltpu.SemaphoreType.DMA]

    def _copies(self, p, out, send_sems, recv_sems, local_sem, start):
        x, y, c, chips = _place()
        me = 2 * x + y
        n = self.n
        if len(self.parts) > n:
            _gather_small(p[n], out[n], send_sems, recv_sems, self.per * n, local_sem, start)
        for w in range(n):
            if self.kind == "exchange":
                ends = [(p[w].at[:, 1 - c], out[w], out[w], (x, y, 1 - c))]
            else:
                ends = [(p[w].at[2 * cx + cy], out[w].at[me], out[w].at[2 * cx + cy], (cx, cy, c)) for cx, cy in chips]
            for j, (src, dst_there, dst_here, to) in enumerate(ends):
                sems = (send_sems.at[self.per * w + j], recv_sems.at[self.per * w + j])
                if start:
                    _start_remote(src, dst_there, *sems, to)
                else:
                    cp = _remote(src, dst_here, *sems, (x, y, c))
                    cp.wait_recv()
                    cp.wait_send()

    def carried_by(self, body, n_in, n_out, n_steps, step_index=lambda: pl.program_id(0)):
        k = len(self.parts)

        def carrier(*refs):
            ins, mine = refs[:n_in], refs[n_in:n_in + k]
            outs, theirs = refs[n_in + k:n_in + k + n_out], refs[n_in + k + n_out:n_in + 2 * k + n_out]
            sems = refs[n_in + 2 * k + n_out:]

            @pl.when(step_index() == 0)
            def _():
                self._copies(mine, theirs, *sems, start=True)

            body(*ins, *outs)

            @pl.when(step_index() == n_steps - 1)
            def _():
                self._copies(mine, theirs, *sems, start=False)

        return carrier


def _share_halves(halves, small):
    n = len(halves)

    def body(*refs):
        hv = refs[:n]
        small_ref = refs[n]
        out = refs[n + 1:2 * n + 1]
        gathered = refs[2 * n + 1]
        send_sems, recv_sems, local_sem = refs[2 * n + 2:]
        x, y, c, _ = _place()
        sends = [_start_remote(hv[w], out[w], send_sems.at[w], recv_sems.at[w], (x, y, 1 - c)) for w in range(n)]
        _gather_small(small_ref, gathered, send_sems, recv_sems, n, local_sem, True)
        for w in range(n):
            _remote(hv[w], out[w], send_sems.at[w], recv_sems.at[w], (x, y, c)).wait_recv()
        for cp in sends:
            cp.wait_send()
        _gather_small(small_ref, gathered, send_sems, recv_sems, n, local_sem, False)

    return pl.pallas_call(
        body, name="share_halves",
        in_specs=[ANY] * (n + 1), out_specs=[ANY] * (n + 1),
        out_shape=[jax.ShapeDtypeStruct(a.shape, F32) for a in halves] + [jax.ShapeDtypeStruct((8,) + small.shape, F32)],
        scratch_shapes=[pltpu.SemaphoreType.DMA((n + 7,)), pltpu.SemaphoreType.DMA((n + 7,)),
                        pltpu.SemaphoreType.DMA],
        compiler_params=pltpu.CompilerParams(has_side_effects=True),
    )(*halves, small)


SMALL_ROWS = 80


def _pack_small(vecs, ws, bs, sink, extra=None):
    ws = jnp.zeros((64, 1024), F32) if ws is None else ws.reshape(64, 1024)
    bs = jnp.zeros((1, 512), F32) if bs is None else bs.reshape(1, 512)
    sink = jnp.zeros((1, 16), F32) if sink is None else sink.reshape(1, 16)
    extra = jnp.zeros((1, 1024), F32) if extra is None else extra.reshape(1, 1024)
    top = jnp.concatenate(
        [v.reshape(1, 1024) for v in vecs]
        + [jnp.pad(bs, ((0, 0), (0, 512))), jnp.pad(sink, ((0, 0), (0, 1008))), extra], axis=0)
    return jnp.concatenate([top, ws, jnp.zeros((8, 1024), F32)], axis=0)


def _unpack_small(p):
    vecs = [p[k] for k in range(5)]
    return vecs, p[8:72].reshape(4, 128, 128), p[5, :512].reshape(4, 128), p[6, :16]


def kernel(x, norm_0, w_in_0, a_v_norm_0, a_spatial_w_0, a_spatial_b_0, b_group_w_0, b_scale_0, w_out_0, norm_1, w_in_1, sink_1, w_out_1, final_norm, loss_target, m_norm_0, m_w_in_0, m_a_v_norm_0, m_a_spatial_w_0, m_a_spatial_b_0, m_b_group_w_0, m_b_scale_0, m_w_out_0, m_norm_1, m_w_in_1, m_sink_1, m_w_out_1, m_final_norm, v_norm_0, v_w_in_0, v_a_v_norm_0, v_a_spatial_w_0, v_a_spatial_b_0, v_b_group_w_0, v_b_scale_0, v_w_out_0, v_norm_1, v_w_in_1, v_sink_1, v_w_out_1, v_final_norm):
    s = x.shape[1]
    xs = x.reshape(s, D_MODEL)
    target = loss_target.reshape(s, D_MODEL)

    cuts = [_Sharded("cols", (1024, 5120)), _Sharded("rows", (2048, 1024)), _Sharded("cols", (1024, 2560)),
            _Sharded("rows", (1024, 1024)), _Sharded("mid", (4, 256, 256))]
    big_w = [w_in_0, w_out_0, w_in_1, w_out_1, b_group_w_0]
    big_m = [m_w_in_0, m_w_out_0, m_w_in_1, m_w_out_1, m_b_group_w_0]
    big_v = [v_w_in_0, v_w_out_0, v_w_in_1, v_w_out_1, v_b_group_w_0]
    where = jnp.stack([2 * lax.axis_index("x") + lax.axis_index("y"), lax.axis_index("c")]).astype(jnp.int32)
    placed = [_place_shard(where, w, cut, f"place_shard{k}") for k, (w, cut) in enumerate(zip(big_w, cuts))]
    cx, cy = lax.axis_index("x"), lax.axis_index("y")
    order = jnp.stack([2 * cx + cy, 2 * (1 - cx) + cy, 2 * cx + 1 - cy, 2 * (1 - cx) + 1 - cy]).astype(jnp.int32)

    row = lambda v: v.reshape(1, 1024)
    ws16 = a_spatial_w_0.astype(BF16)
    bsx = jnp.repeat(a_spatial_b_0.T, 256, axis=1)
    band = _band_matrices(TOK)[0]
    rope = _rope_tables(s)

    h0, z0, w_in0 = _in_proj0_own(order, xs, row(norm_0), w_in_0.astype(BF16), _Gather(placed[:1], cuts[:1]))
    z0, w_out0, wg = _in_proj0_rest(order, h0, w_in0, z0, _Gather([placed[1], placed[4]], [cuts[1], cuts[4]]))
    cat, x1, w_in1, w_out1 = _mix0_fwd(xs, z0, w_out0, row(a_v_norm_0), ws16, bsx, wg, row(b_scale_0), band,
                                       _Gather(placed[2:4], cuts[2:4]))
    h1, q, k, v, gate = _in_proj1(x1, row(norm_1), w_in1, rope)
    kpad = jnp.pad(k, ((ATTN_WINDOW, ATTN_WINDOW), (0, 0)))
    vpad = jnp.pad(v, ((ATTN_WINDOW, ATTN_WINDOW), (0, 0)))
    o, lse = _attn_fwd(q, kpad, vpad, sink_1)
    y1, dx2, do, dgate, loss_lanes, g_final, dx2h = _tail(x1, o, gate, target, w_out1, row(final_norm))

    dq, dkpad, dvpad, dsink = _attn_bwd(q, kpad, vpad, sink_1, o, lse, do, rope)
    dz1, dx1, g_norm1, dx1h = _in_proj1_bwd(dq, dkpad, dvpad, dgate, x1, dx2, row(norm_1), w_in1, rope)

    g_w_in1 = _weight_grad(h1, dz1, 2, 2, "grad_w_in1").reshape(4, 2, 512, 640)
    g_w_out1 = _weight_grad(y1, dx2h, 1, 1, "grad_w_out1").reshape(4, 2, 128, 1024)
    g_w_out0 = _weight_grad(cat, dx1h, 1, 1, "grad_w_out0").reshape(4, 2, 256, 1024)
    first = [g_w_out0, g_w_in1, g_w_out1]
    zero = jnp.zeros((1024,), F32)
    small1 = _pack_small([zero, zero, zero, g_norm1, g_final], None, None, dsink[0, :16])[:8]
    dz0, dpn, d_ws, d_bs, d_gv, d_scale, d_wg, *theirs1, small1_all = _mix0_bwd(
        dx1h, z0, w_out0, row(a_v_norm_0), ws16, bsx, wg, row(b_scale_0), band, _Rider("exchange", first, small1))
    parts1 = [_add_sibling(where, g, t, f"add_sibling1_{k}") for k, (g, t) in enumerate(zip(first, theirs1))]
    dz0 = _fill_pooled_grad(dz0, dpn)
    g_w_in0, *slots1 = _weight_grad(h0, dz0, 4, 1, "grad_w_in0", _Rider("scatter", parts1))
    g_w_in0 = g_w_in0.reshape(4, 2, 512, 1280)
    g_wg = d_wg.reshape(2, 2, 4, 64, 256).transpose(2, 0, 1, 3, 4).reshape(4, 2, 128, 256)
    second = [g_w_in0, g_wg]
    theirs2 = _exchange_halves(second, "exchange_halves2")
    parts2 = [_add_sibling(where, g, t, f"add_sibling2_{k}") for k, (g, t) in enumerate(zip(second, theirs2))]
    small2 = _pack_small([zero, d_gv, d_scale, zero, zero], d_ws, d_bs[:, :4].T, None)
    grad_x, g_norm0, *slots2, small2_all = _in_proj0_bwd(
        dz0, xs, dx1, row(norm_0), w_in0, _Rider("scatter", parts2, small2))

    grads = [g_w_in0, g_w_out0, g_w_in1, g_w_out1, g_wg]
    theirs = [theirs2[0], theirs1[0], theirs1[1], theirs1[2], theirs2[1]]
    slots = [slots2[0], slots1[0], slots1[1], slots1[2], slots2[1]]
    n = len(grads)
    reduced = [_sum_chips(where, grads[w], theirs[w], slots[w], f"sum_chips{w}") for w in range(n)]
    small3 = _pack_small([g_norm0, zero, zero, zero, zero], None, None, None, loss_lanes)[:8]
    *from_sibling, small3_all = _share_halves(reduced, small3)

    out_g, out_d, out_m, out_v = {}, {}, {}, {}
    names = ["w_in_0", "w_out_0", "w_in_1", "w_out_1", "b_group_w_0"]
    for w in range(n):
        shape = big_w[w].shape
        two_d = (-1, shape[-1])
        outs = _adamw_halves(where, big_w[w].reshape(two_d), reduced[w], from_sibling[w], big_m[w].reshape(two_d),
                             big_v[w].reshape(two_d), f"adamw{w}")
        out_g[names[w]], out_d[names[w]], out_m[names[w]], out_v[names[w]] = (a.reshape(shape) for a in outs)

    g_small = _sum_small(small1_all, small2_all, small3_all)
    small_names = ["norm_0", "a_v_norm_0", "b_scale_0", "norm_1", "final_norm"]
    pack = lambda vecs, ws_, bs_, sk: _pack_small(vecs, ws_, bs_, sk)
    w_small = pack([norm_0, a_v_norm_0, b_scale_0, norm_1, final_norm], a_spatial_w_0, a_spatial_b_0, sink_1)
    m_small = pack([m_norm_0, m_a_v_norm_0, m_b_scale_0, m_norm_1, m_final_norm], m_a_spatial_w_0,
                   m_a_spatial_b_0, m_sink_1)
    v_small = pack([v_norm_0, v_a_v_norm_0, v_b_scale_0, v_norm_1, v_final_norm], v_a_spatial_w_0,
                   v_a_spatial_b_0, v_sink_1)
    d_small, nm_small, nv_small = _adamw(w_small, g_small, m_small, v_small, "adamw_small")
    for store, packed in ((out_g, g_small), (out_d, d_small), (out_m, nm_small), (out_v, nv_small)):
        vecs, ws_, bs_, sk = _unpack_small(packed)
        for name, vec in zip(small_names, vecs):
            store[name] = vec
        store["a_spatial_w_0"], store["a_spatial_b_0"], store["sink_1"] = ws_, bs_, sk

    loss = jnp.sum(g_small[7])
    order = ["norm_0", "w_in_0", "a_v_norm_0", "a_spatial_w_0", "a_spatial_b_0", "b_group_w_0", "b_scale_0",
             "w_out_0", "norm_1", "w_in_1", "sink_1", "w_out_1", "final_norm"]
    return (loss, grad_x.reshape(1, s, D_MODEL), *[out_g[k] for k in order], *[out_d[k] for k in order],
            *[out_m[k] for k in order], *[out_v[k] for k in order])
```

```python
import functools

import numpy as np
import jax
import jax.numpy as jnp
from jax import lax
from jax.experimental import pallas as pl
from jax.experimental.pallas import tpu as pltpu

F32 = jnp.float32
BF16 = jnp.bfloat16
MESH = pl.DeviceIdType.MESH

D_MODEL = 1024
EPS = 1e-6
NEG_INF = -1e30
CHUNK = 128
POOL_WINDOWS = (2, 4, 8, 16)
HALO = 16
N_HEADS = 16
HEAD_DIM = 64
ATTN_WINDOW = 128
ROPE_THETA = 500000.0
ROT_DIM = 16
ADAM_LR = 0.001
ADAM_B1 = 0.9
ADAM_B2 = 0.999
ADAM_EPS = 1e-08
ADAM_WD = 0.01
ADAM_STEP = 10

TOK = 256
VMEM_LIMIT = 56 * 1024 * 1024


def _params(**kw):
    return pltpu.CompilerParams(vmem_limit_bytes=VMEM_LIMIT, **kw)


def _whole(shape):
    nd = len(shape)
    return pl.BlockSpec(shape, lambda *_: (0,) * nd)


def _rows(t, n):
    return pl.BlockSpec((t, n), lambda i: (i, 0))


ANY = pl.BlockSpec(memory_space=pl.ANY)

_G0 = 0.7978845608028654
_G1 = 0.044715


def _gelu(x):
    return 0.5 * x * (1.0 + jnp.tanh(_G0 * (x + _G1 * x * x * x)))


def _gelu_and_grad(x):
    x2 = x * x
    t = jnp.tanh(_G0 * (x + _G1 * x2 * x))
    half = 0.5 * (1.0 + t)
    return x * half, half + 0.5 * x * (1.0 - t * t) * (_G0 * (1.0 + 3.0 * _G1 * x2))


def _sigmoid(x):
    return 1.0 / (1.0 + jnp.exp(-x))


def _dot(a, b):
    return jnp.dot(a, b, preferred_element_type=F32)


def _dot_nt(a, b):
    return lax.dot_general(a, b, (((1,), (1,)), ((), ())), preferred_element_type=F32)


def _dot_tn(a, b):
    return lax.dot_general(a, b, (((0,), (0,)), ((), ())), preferred_element_type=F32)


def _rms_fwd(x, g):
    r = lax.rsqrt(jnp.mean(x * x, axis=-1, keepdims=True) + EPS)
    xh = x * r
    return r, xh, xh * g


def _rms_bwd(dy, g, r, xh):
    dxh = dy * g
    return r * (dxh - xh * jnp.mean(dxh * xh, axis=-1, keepdims=True))


def _band_matrices(t):
    r = np.arange(t)[:, None]
    j = np.arange(t + 2 * HALO)[None, :]
    fwd, bwd = [], []
    for w in POOL_WINDOWS:
        d = j - r - HALO
        fwd.append((d >= -(w // 2)) & (d < w // 2))
        bwd.append((d >= -(w // 2) + 1) & (d <= w // 2))
    return (jnp.asarray(np.stack(fwd), BF16), jnp.asarray(np.stack(bwd), BF16))


def _window_counts(i, t, s):
    tok = i * t + lax.broadcasted_iota(jnp.int32, (t, 1), 0)
    out = []
    for w in POOL_WINDOWS:
        cnt = jnp.minimum(tok + w // 2, s) - jnp.maximum(tok - w // 2, 0)
        out.append(cnt.astype(F32))
    return out


def _rope_tables(s):
    inv = np.float32(ROPE_THETA) ** (-np.arange(0, ROT_DIM, 2, dtype=np.float32) / np.float32(ROT_DIM))
    ang = np.arange(s, dtype=np.float32)[:, None] * inv.astype(np.float32)[None, :]
    cos, sin = np.cos(ang).astype(np.float32), np.sin(ang).astype(np.float32)
    z8 = np.zeros((s, 8), np.float32)
    z48 = np.zeros((s, HEAD_DIM - ROT_DIM), np.float32)
    c = np.concatenate([cos, cos, np.ones((s, HEAD_DIM - ROT_DIM), np.float32)], axis=1)
    s_lo = np.concatenate([z8, sin, z48], axis=1)
    s_hi = np.concatenate([-sin, z8, z48], axis=1)
    return tuple(jnp.asarray(np.concatenate([a, a], axis=1)) for a in (c, s_lo, s_hi))


def _rope(x, c, s_lo, s_hi):
    n = x.shape[1]
    reps = n // 128
    c, s_lo, s_hi = (jnp.tile(a, (1, reps)) for a in (c, s_lo, s_hi))
    return x * c + pltpu.roll(x, 8, 1) * s_lo + pltpu.roll(x, n - 8, 1) * s_hi


def _rope_t(dx, c, s_lo, s_hi):
    n = dx.shape[1]
    reps = n // 128
    c, s_lo, s_hi = (jnp.tile(a, (1, reps)) for a in (c, s_lo, s_hi))
    return dx * c + pltpu.roll(dx * s_lo, n - 8, 1) + pltpu.roll(dx * s_hi, 8, 1)


def _in_proj0_own(order, x, g0, w_own, rider):
    s = x.shape[0]
    n = w_own.shape[1]

    def body(order_ref, x_ref, g_ref, w_ref, h_ref, z_ref):
        _, _, h = _rms_fwd(x_ref[...], g_ref[...])
        h = h.astype(BF16)
        h_ref[...] = h
        z_ref[...] = _dot(h, w_ref[...]).astype(BF16)

    return pl.pallas_call(
        rider.carried_by(body, 4, 2, s // TOK), name="in_proj0_own",
        grid_spec=pltpu.PrefetchScalarGridSpec(
            num_scalar_prefetch=1, grid=(s // TOK,),
            in_specs=[pl.BlockSpec((TOK, D_MODEL), lambda i, o: (i, 0)), pl.BlockSpec((1, D_MODEL), lambda i, o: (0, 0)),
                      pl.BlockSpec(w_own.shape, lambda i, o: (0, 0))] + rider.in_specs,
            out_specs=[pl.BlockSpec((TOK, D_MODEL), lambda i, o: (i, 0)),
                       pl.BlockSpec((TOK, n), lambda i, o: (i, o[0]))] + rider.out_specs,
            scratch_shapes=rider.scratch),
        out_shape=[jax.ShapeDtypeStruct((s, D_MODEL), BF16), jax.ShapeDtypeStruct((s, 4 * n), BF16)] + rider.out_shape,
        input_output_aliases={4 + j: 2 + j for j in range(len(rider.fulls))},
        compiler_params=_params(),
    )(order, x, g0, w_own, *rider.fulls)


def _in_proj0_rest(order, h0, w_in0, z0, rider):
    s = h0.shape[0]
    tok = min(s, 4 * TOK)
    n_tiles = s // tok
    n = w_in0.shape[1] // 4

    def body(order_ref, h_ref, w_ref, z_in_ref, z_ref):
        z_ref[...] = _dot(h_ref[...], w_ref[...]).astype(BF16)

    return pl.pallas_call(
        rider.carried_by(body, 4, 1, 3 * n_tiles, lambda: pl.program_id(0) * n_tiles + pl.program_id(1)),
        name="in_proj0_rest",
        grid_spec=pltpu.PrefetchScalarGridSpec(
            num_scalar_prefetch=1, grid=(3, n_tiles),
            in_specs=[pl.BlockSpec((tok, D_MODEL), lambda k, i, o: (i, 0)),
                      pl.BlockSpec((w_in0.shape[0], n), lambda k, i, o: (0, o[1 + k])), ANY] + rider.in_specs,
            out_specs=[pl.BlockSpec((tok, n), lambda k, i, o: (i, o[1 + k]))] + rider.out_specs,
            scratch_shapes=rider.scratch),
        out_shape=[jax.ShapeDtypeStruct(z0.shape, BF16)] + rider.out_shape,
        input_output_aliases={3: 0, **{4 + j: 1 + j for j in range(len(rider.fulls))}},
        compiler_params=_params(),
    )(order, h0, w_in0, z0, *rider.fulls)


def _halo_specs(s, col_block, tok=TOK):
    per = tok // HALO
    last = s // HALO - 1
    prev = pl.BlockSpec((HALO, 1024), lambda i: (jnp.maximum(i * per - 1, 0), col_block))
    nxt = pl.BlockSpec((HALO, 1024), lambda i: (jnp.minimum((i + 1) * per, last), col_block))
    return prev, nxt


def _with_halo(i, n_tiles, prev_ref, cur, next_ref):
    prev = prev_ref[...]
    nxt = next_ref[...]
    prev = jnp.where(i > 0, prev, jnp.zeros_like(prev))
    nxt = jnp.where(i < n_tiles - 1, nxt, jnp.zeros_like(nxt))
    return jnp.concatenate([prev, cur, nxt], axis=0)


def _mixer_a(v1, gv, ws_ref, bsx):
    rv, vh, v2 = _rms_fwd(v1, gv)
    v2 = v2.astype(BF16)
    rows = []
    for c in range(v1.shape[0] // CHUNK):
        cols = [_dot(ws_ref[h], v2[c * CHUNK:(c + 1) * CHUNK, h * 256:(h + 1) * 256]) for h in range(4)]
        rows.append(jnp.concatenate(cols, axis=1) + bsx)
    return rv, vh, v2, jnp.concatenate(rows, axis=0)


def _mixer_b_pooled(bx, halo, band_ref, counts):
    out = []
    for g in range(4):
        win = _dot(band_ref[g], halo[:, g * 256:(g + 1) * 256])
        out.append(win / counts[g] - bx[:, g * 256:(g + 1) * 256])
    return out


def _mix0_fwd(x, z0, w_out0, gv, ws, bsx, wg, scale, band, rider):
    s = x.shape[0]
    n_tiles = s // TOK
    k = len(rider.fulls)

    def body(z_ref, zp_ref, zn_ref, x_ref, wout_ref, gv_ref, ws_ref, bsx_ref, wg_ref, sc_ref, band_ref,
             cat_ref, x1_ref):
        i = pl.program_id(0)
        au = z_ref[:, 0:1024].astype(F32)
        av = z_ref[:, 1024:2048].astype(F32)
        ag = z_ref[:, 2048:3072].astype(F32)
        _, _, _, mixed = _mixer_a(_gelu(av), gv_ref[...], ws_ref, bsx_ref[...])
        cat_ref[:, 0:1024] = (_gelu(au) * mixed * (ag * _sigmoid(ag))).astype(BF16)

        bx = z_ref[:, 3072:4096]
        bg = z_ref[:, 4096:5120].astype(F32)
        halo = _with_halo(i, n_tiles, zp_ref, bx, zn_ref)
        ps = _mixer_b_pooled(bx.astype(F32), halo, band_ref, _window_counts(i, TOK, s))
        pw = jnp.concatenate([_dot(ps[g].astype(BF16), wg_ref[g]) for g in range(4)], axis=1)
        cat_ref[:, 1024:2048] = (pw * sc_ref[...] * (bg * _sigmoid(bg))).astype(BF16)

        x1_ref[...] = x_ref[...] + _dot(cat_ref[...], wout_ref[...])

    prev, nxt = _halo_specs(s, 3)
    return pl.pallas_call(
        rider.carried_by(body, 11, 2, n_tiles), name="mix0_fwd", grid=(n_tiles,),
        in_specs=[_rows(TOK, 5120), prev, nxt, _rows(TOK, D_MODEL), _whole(w_out0.shape), _whole((1, 1024)),
                  _whole(ws.shape), _whole(bsx.shape), _whole(wg.shape), _whole((1, 1024)), _whole(band.shape)]
        + rider.in_specs,
        out_specs=[_rows(TOK, 2048), _rows(TOK, D_MODEL)] + rider.out_specs,
        out_shape=[jax.ShapeDtypeStruct((s, 2048), BF16), jax.ShapeDtypeStruct((s, D_MODEL), F32)] + rider.out_shape,
        input_output_aliases={11 + j: 2 + j for j in range(k)},
        scratch_shapes=rider.scratch,
        compiler_params=_params(),
    )(z0, z0, z0, x, w_out0, gv, ws, bsx, wg, scale, band, *rider.fulls)


def _in_proj1(x1, g1, w_in1, rope):
    s = x1.shape[0]

    def body(x_ref, g_ref, w_ref, c_ref, lo_ref, hi_ref, h_ref, q_ref, k_ref, v_ref, gate_ref):
        halves = [slice(r * TOK, (r + 1) * TOK) for r in range(tok // TOK)]
        hs = [_rms_fwd(x_ref[rows, :], g_ref[...])[2].astype(BF16) for rows in halves]
        for rows, h in zip(halves, hs):
            h_ref[rows, :] = h
        qs = [_dot(h, w_ref[:, 0:1024]) for h in hs]
        kvs = [_dot(h, w_ref[:, 1024:1536]) for h in hs]
        for rows, q, kv in zip(halves, qs, kvs):
            tabs = (c_ref[rows, :], lo_ref[rows, :], hi_ref[rows, :])
            q_ref[rows, :] = (_rope(q, *tabs) * Q_SCALE).astype(BF16)
            k_ref[rows, :] = _rope(kv[:, 0:256], *tabs).astype(BF16)
            v_ref[rows, :] = kv[:, 256:512].astype(BF16)
        for rows, h in zip(halves, hs):
            gate_ref[rows, :] = _dot(h, w_ref[:, 1536:2560]).astype(BF16)

    tok = min(s, 2 * TOK)
    tab = _rows(tok, 128)
    return pl.pallas_call(
        body, name="in_proj1", grid=(s // tok,),
        in_specs=[_rows(tok, D_MODEL), _whole((1, D_MODEL)), _whole(w_in1.shape), tab, tab, tab],
        out_specs=[_rows(tok, 1024), _rows(tok, 1024), _rows(tok, 256), _rows(tok, 256), _rows(tok, 1024)],
        out_shape=[jax.ShapeDtypeStruct((s, 1024), BF16), jax.ShapeDtypeStruct((s, 1024), BF16),
                   jax.ShapeDtypeStruct((s, 256), BF16), jax.ShapeDtypeStruct((s, 256), BF16),
                   jax.ShapeDtypeStruct((s, 1024), BF16)],
        compiler_params=_params(),
    )(x1, g1, w_in1, *rope)


QBLK = 128
KBLK = QBLK + 2 * ATTN_WINDOW
Q_SCALE = HEAD_DIM ** -0.5


def _block_bias(q0, s):
    r = lax.broadcasted_iota(jnp.int32, (QBLK, KBLK), 0)
    c = lax.broadcasted_iota(jnp.int32, (QBLK, KBLK), 1)
    kj = q0 - ATTN_WINDOW + c
    ok = (c >= r) & (c <= r + 2 * ATTN_WINDOW) & (kj >= 0) & (kj < s)
    return jnp.where(ok, 0.0, NEG_INF)


def _pair_operands(t):
    lane = lax.broadcasted_iota(jnp.int32, (1, 128), 1)
    first = lane < HEAD_DIM
    zero = jnp.zeros((KBLK, 128), BF16)
    out = []
    for j in range(2):
        slab = t[:, 128 * j:128 * (j + 1)]
        turned = pltpu.bitcast(pltpu.roll(pltpu.bitcast(slab, jnp.uint32), HEAD_DIM, 1), BF16)
        for own_first in (True, False):
            top = jnp.where(first, slab if own_first else turned, zero)
            bottom = jnp.where(first, zero, turned if own_first else slab)
            out.append(jnp.concatenate([top, bottom], axis=0))
    return out


def _attn_fwd(q, kpad, vpad, sink):
    s = q.shape[0]

    def body(sink_ref, q_ref, k_ref, v_ref, o_ref, lse_ref):
        i = pl.program_id(0)
        lane = lax.broadcasted_iota(jnp.int32, (1, 128), 1)
        for b in range(TOK // QBLK):
            rows = slice(b * QBLK, (b + 1) * QBLK)
            start = pl.multiple_of(i * TOK + b * QBLK, QBLK)
            k_bd = _pair_operands(k_ref[pl.ds(start, KBLK), :])
            v_bd = _pair_operands(v_ref[pl.ds(start, KBLK), :])
            bias = _block_bias(i * TOK + b * QBLK, s)
            pairs = range(N_HEADS // 2)
            sc4 = [_dot_nt(jnp.concatenate([q_ref[rows, 256 * g:256 * g + 128], q_ref[rows, 256 * g + 128:256 * (g + 1)]],
                                           axis=0), k_bd[g]) for g in range(4)]
            sc2 = [sc4[m // 2][(m % 2) * QBLK:(m % 2 + 1) * QBLK] for m in pairs]
            scs = [sc2[h // 2][:, (h % 2) * KBLK:(h % 2 + 1) * KBLK] + bias for h in range(N_HEADS)]
            ms = [jnp.maximum(jnp.max(scs[h], axis=-1, keepdims=True), sink_ref[h]) for h in range(N_HEADS)]
            es = [jnp.exp(scs[h] - ms[h]) for h in range(N_HEADS)]
            dens = [jnp.sum(es[h], axis=-1, keepdims=True) + jnp.exp(sink_ref[h] - ms[h]) for h in range(N_HEADS)]
            first = lane < HEAD_DIM
            e2 = [jnp.concatenate([es[2 * m].astype(BF16), es[2 * m + 1].astype(BF16)], axis=1) for m in pairs]
            o4 = [_dot(jnp.concatenate([e2[2 * g], e2[2 * g + 1]], axis=0), v_bd[g]) for g in range(4)]
            outs = [o4[m // 2][(m % 2) * QBLK:(m % 2 + 1) * QBLK]
                    * jnp.where(first, 1.0 / dens[2 * m], 1.0 / dens[2 * m + 1]) for m in pairs]
            o_ref[rows, :] = jnp.concatenate(outs, axis=1).astype(BF16)
            lse = jnp.zeros((QBLK, 128), F32)
            for h in range(N_HEADS):
                lse = lse + jnp.where(lane == h, ms[h] + jnp.log(dens[h]), 0.0)
            lse_ref[rows, :] = lse

    return pl.pallas_call(
        body, name="attn_fwd", grid=(s // TOK,),
        in_specs=[pl.BlockSpec(memory_space=pltpu.SMEM), _rows(TOK, 1024), _whole(kpad.shape), _whole(vpad.shape)],
        out_specs=[_rows(TOK, 1024), _rows(TOK, 128)],
        out_shape=[jax.ShapeDtypeStruct((s, 1024), BF16), jax.ShapeDtypeStruct((s, 128), F32)],
        compiler_params=_params(),
    )(sink, q, kpad, vpad)


def _tail(x1, o, gate, target, w_out1, gf):
    s = x1.shape[0]

    def body(x1_ref, o_ref, gate_ref, t_ref, w_ref, gf_ref, y1_ref, dx2_ref, do_ref, dgate_ref, loss_ref, gfn_ref,
             dx2h_ref):
        i = pl.program_id(0)

        @pl.when(i == 0)
        def _():
            loss_ref[...] = jnp.zeros_like(loss_ref)
            gfn_ref[...] = jnp.zeros_like(gfn_ref)

        halves = [slice(r * TOK, (r + 1) * TOK) for r in range(tok // TOK)]
        gf = gf_ref[...]
        gs = [gate_ref[rows, :].astype(F32) for rows in halves]
        sgs = [_sigmoid(g) for g in gs]
        sils = [g * sg for g, sg in zip(gs, sgs)]
        os_ = [o_ref[rows, :].astype(F32) for rows in halves]
        y1s = [(o * sil).astype(BF16) for o, sil in zip(os_, sils)]
        for rows, y1 in zip(halves, y1s):
            y1_ref[rows, :] = y1
        x2s = [x1_ref[rows, :] + _dot(y1, w_ref[...]) for rows, y1 in zip(halves, y1s)]
        dx2hs = []
        for rows, x2 in zip(halves, x2s):
            r, xh, out = _rms_fwd(x2, gf)
            diff = out - t_ref[rows, :]
            loss_ref[...] += jnp.sum(diff * diff, axis=0, keepdims=True) * (0.5 / D_MODEL)
            dout = diff * (1.0 / D_MODEL)
            gfn_ref[...] += jnp.sum(dout * xh, axis=0, keepdims=True)
            dx2 = _rms_bwd(dout, gf, r, xh)
            dx2_ref[rows, :] = dx2
            dx2hs.append(dx2.astype(BF16))
            dx2h_ref[rows, :] = dx2hs[-1]
        dy1s = [_dot_nt(dx2h, w_ref[...]) for dx2h in dx2hs]
        for rows, dy1, sil, o, sg, g in zip(halves, dy1s, sils, os_, sgs, gs):
            do_ref[rows, :] = (dy1 * sil).astype(BF16)
            dgate_ref[rows, :] = (dy1 * o * (sg * (1.0 + g * (1.0 - sg)))).astype(BF16)

    tok = min(s, 2 * TOK)
    row = _rows(tok, 1024)
    acc = _whole((1, 1024))
    return pl.pallas_call(
        body, name="tail", grid=(s // tok,),
        in_specs=[row, row, row, row, _whole(w_out1.shape), acc],
        out_specs=[row, row, row, row, acc, acc, row],
        out_shape=[jax.ShapeDtypeStruct((s, 1024), BF16), jax.ShapeDtypeStruct((s, 1024), F32),
                   jax.ShapeDtypeStruct((s, 1024), BF16), jax.ShapeDtypeStruct((s, 1024), BF16),
                   jax.ShapeDtypeStruct((1, 1024), F32), jax.ShapeDtypeStruct((1, 1024), F32),
                   jax.ShapeDtypeStruct((s, 1024), BF16)],
        compiler_params=_params(),
    )(x1, o, gate, target, w_out1, gf)


def _attn_bwd(q, kpad, vpad, sink, o, lse, do, rope):
    s = q.shape[0]
    pad_t = (kpad.shape[1], kpad.shape[0])

    def body(sink_ref, q_ref, k_ref, v_ref, o_ref, lse_ref, do_ref, c_ref, lo_ref, hi_ref,
             dq_ref, dk_ref, dv_ref, ds_ref):
        i = pl.program_id(0)

        @pl.when(i == 0)
        def _():
            dk_ref[...] = jnp.zeros_like(dk_ref)
            dv_ref[...] = jnp.zeros_like(dv_ref)
            ds_ref[...] = jnp.zeros_like(ds_ref)

        lane = lax.broadcasted_iota(jnp.int32, (1, 128), 1)
        dsink = jnp.zeros((1, 128), F32)
        for b in range(TOK // QBLK):
            rows = slice(b * QBLK, (b + 1) * QBLK)
            start = pl.multiple_of(i * TOK + b * QBLK, QBLK)
            k_bd = _pair_operands(k_ref[pl.ds(start, KBLK), :])
            v_bd = _pair_operands(v_ref[pl.ds(start, KBLK), :])
            bias = _block_bias(i * TOK + b * QBLK, s)
            lanes_of = (lane < HEAD_DIM, lane >= HEAD_DIM)
            half = lambda t, j: t[:, j * KBLK:(j + 1) * KBLK]
            dqs, dks, dvs = [], [], []
            for g in range(4):
                pairs = (2 * g, 2 * g + 1)
                qs = {m: q_ref[rows, 128 * m:128 * (m + 1)] for m in pairs}
                dos = {m: do_ref[rows, 128 * m:128 * (m + 1)] for m in pairs}
                lses = {h: lse_ref[rows, h:h + 1] for h in range(4 * g, 4 * g + 4)}
                stacked = lambda parts: jnp.concatenate([parts[m] for m in pairs], axis=0)
                unstack = lambda t: {m: t[j * QBLK:(j + 1) * QBLK] for j, m in enumerate(pairs)}
                sc2 = unstack(_dot_nt(stacked(qs), k_bd[g]))
                ps = {2 * m + j: jnp.exp(half(sc2[m], j) + bias - lses[2 * m + j]) for m in pairs for j in range(2)}
                prods = {m: dos[m].astype(F32) * o_ref[rows, 128 * m:128 * (m + 1)].astype(F32) for m in pairs}
                deltas = {2 * m + j: jnp.sum(jnp.where(lanes_of[j], prods[m], 0.0), axis=-1, keepdims=True)
                          for m in pairs for j in range(2)}
                for h in range(4 * g, 4 * g + 4):
                    dsink = dsink + jnp.where(
                        lane == h, -jnp.sum(jnp.exp(sink_ref[h] - lses[h]) * deltas[h], axis=0, keepdims=True), 0.0)
                dp2 = unstack(_dot_nt(stacked(dos), v_bd[g]))
                ds2 = {m: jnp.concatenate(
                    [(ps[2 * m + j] * (half(dp2[m], j) - deltas[2 * m + j])).astype(BF16) for j in range(2)], axis=1)
                    for m in pairs}
                p2 = {m: jnp.concatenate([ps[2 * m].astype(BF16), ps[2 * m + 1].astype(BF16)], axis=1) for m in pairs}
                diag = lambda t: t[0:HEAD_DIM, 0:KBLK] + t[HEAD_DIM:128, KBLK:2 * KBLK]
                dvs.append(diag(_dot_tn(stacked(dos), stacked(p2))))
                dks.append(diag(_dot_tn(stacked(qs), stacked(ds2))))
                dq2 = unstack(_dot(stacked(ds2), k_bd[g]) * Q_SCALE)
                dqs += [dq2[m] for m in pairs]
            dq = jnp.concatenate(dqs, axis=1)
            dq_ref[rows, :] = _rope_t(dq, c_ref[rows, :], lo_ref[rows, :], hi_ref[rows, :]).astype(BF16)
            dk_ref[:, pl.ds(start, KBLK)] += jnp.concatenate(dks, axis=0)
            dv_ref[:, pl.ds(start, KBLK)] += jnp.concatenate(dvs, axis=0)
        ds_ref[...] += dsink

    row = _rows(TOK, 1024)
    tab = _rows(TOK, 128)
    pad = _whole(kpad.shape)
    return pl.pallas_call(
        body, name="attn_bwd", grid=(s // TOK,),
        in_specs=[pl.BlockSpec(memory_space=pltpu.SMEM), row, pad, pad, row, tab, row, tab, tab, tab],
        out_specs=[row, _whole(pad_t), _whole(pad_t), _whole((1, 128))],
        out_shape=[jax.ShapeDtypeStruct((s, 1024), BF16), jax.ShapeDtypeStruct(pad_t, F32),
                   jax.ShapeDtypeStruct(pad_t, F32), jax.ShapeDtypeStruct((1, 128), F32)],
        compiler_params=_params(),
    )(sink, q, kpad, vpad, o, lse, do, *rope)


def _in_proj1_bwd(dq, dk_t, dv_t, dgate, x1, dx2, g1, w_in1, rope):
    s = x1.shape[0]
    tok = min(s, 2 * TOK)
    n_sub = tok // ATTN_WINDOW

    def body(*refs):
        dq_ref = refs[0]
        dk_refs, dv_refs = refs[1:1 + n_sub], refs[1 + n_sub:1 + 2 * n_sub]
        (dgate_ref, x1_ref, dx2_ref, g_ref, w_ref, c_ref, lo_ref, hi_ref,
         dz_ref, dx1_ref, gn_ref, dx1h_ref) = refs[1 + 2 * n_sub:]

        @pl.when(pl.program_id(0) == 0)
        def _():
            gn_ref[...] = jnp.zeros_like(gn_ref)

        halves = [slice(r * TOK, (r + 1) * TOK) for r in range(tok // TOK)]
        per = TOK // ATTN_WINDOW
        g = g_ref[...]
        for r, rows in enumerate(halves):
            dk = jnp.concatenate([ref[...] for ref in dk_refs[r * per:(r + 1) * per]], axis=1).T
            dv = jnp.concatenate([ref[...] for ref in dv_refs[r * per:(r + 1) * per]], axis=1).T
            dz_ref[rows, 0:1024] = dq_ref[rows, :]
            dz_ref[rows, 1024:1280] = _rope_t(dk, c_ref[rows, :], lo_ref[rows, :], hi_ref[rows, :]).astype(BF16)
            dz_ref[rows, 1280:1536] = dv.astype(BF16)
            dz_ref[rows, 1536:2560] = dgate_ref[rows, :]
        dhs = [_dot_nt(dz_ref[rows, :], w_ref[...]) for rows in halves]
        for rows, dh in zip(halves, dhs):
            r, xh, _ = _rms_fwd(x1_ref[rows, :], g)
            gn_ref[...] += jnp.sum(dh * xh, axis=0, keepdims=True)
            dx1 = dx2_ref[rows, :] + _rms_bwd(dh, g, r, xh)
            dx1_ref[rows, :] = dx1
            dx1h_ref[rows, :] = dx1.astype(BF16)

    row = _rows(tok, 1024)
    subs = [pl.BlockSpec((256, ATTN_WINDOW), lambda i, j=j: (0, n_sub * i + 1 + j)) for j in range(n_sub)]
    tab = _rows(tok, 128)
    acc = _whole((1, 1024))
    return pl.pallas_call(
        body, name="in_proj1_bwd", grid=(s // tok,),
        in_specs=[row] + subs + subs + [row, row, row, acc, _whole(w_in1.shape), tab, tab, tab],
        out_specs=[_rows(tok, 2560), row, acc, row],
        out_shape=[jax.ShapeDtypeStruct((s, 2560), BF16), jax.ShapeDtypeStruct((s, 1024), F32),
                   jax.ShapeDtypeStruct((1, 1024), F32), jax.ShapeDtypeStruct((s, 1024), BF16)],
        compiler_params=_params(),
    )(dq, *[dk_t] * n_sub, *[dv_t] * n_sub, dgate, x1, dx2, g1, w_in1, *rope)


def _mix0_bwd(dx1, z0, w_out0, gv, ws, bsx, wg, scale, band, rider):
    s = dx1.shape[0]
    n_tiles = s // TOK

    def body(dx1h_ref, z_ref, zp_ref, zn_ref, wout_ref, gv_ref, ws_ref, bsx_ref, wg_ref, sc_ref, band_ref,
             dz_ref, dpn_ref, dws_ref, dbs_ref, dgv_ref, dsc_ref, dwg_ref):
        i = pl.program_id(0)
        dz_ref[:, 3072:4096] = jnp.zeros((TOK, 1024), BF16)

        @pl.when(i == 0)
        def _():
            for ref in (dws_ref, dbs_ref, dgv_ref, dsc_ref, dwg_ref):
                ref[...] = jnp.zeros_like(ref)

        dcat = _dot_nt(dx1h_ref[...], wout_ref[...])
        dya = dcat[:, 0:1024]
        dyb = dcat[:, 1024:2048]

        au = z_ref[:, 0:1024].astype(F32)
        av = z_ref[:, 1024:2048].astype(F32)
        ag = z_ref[:, 2048:3072].astype(F32)
        gv = gv_ref[...]
        u, du = _gelu_and_grad(au)
        v1, dv1 = _gelu_and_grad(av)
        rv, vh, v2, mixed = _mixer_a(v1, gv, ws_ref, bsx_ref[...])
        sg = _sigmoid(ag)
        sil = ag * sg
        dz_ref[:, 2048:3072] = (dya * u * mixed * (sg * (1.0 + ag * (1.0 - sg)))).astype(BF16)
        dz_ref[:, 0:1024] = (dya * mixed * sil * du).astype(BF16)
        dmixed = dya * u * sil
        lane = lax.broadcasted_iota(jnp.int32, (1, 128), 1)
        dm16 = dmixed.astype(BF16)
        dv2_rows = []
        for c in range(TOK // CHUNK):
            rows = slice(c * CHUNK, (c + 1) * CHUNK)
            cols_out = []
            for h in range(4):
                cols = slice(h * 256, (h + 1) * 256)
                dws_ref[h] += _dot_nt(dm16[rows, cols], v2[rows, cols])
                dbs_ref[...] += jnp.where(lane == h, jnp.sum(dmixed[rows, cols], axis=-1, keepdims=True), 0.0)
                cols_out.append(_dot_tn(ws_ref[h], dm16[rows, cols]))
            dv2_rows.append(jnp.concatenate(cols_out, axis=1))
        dv2 = jnp.concatenate(dv2_rows, axis=0)
        dgv_ref[...] += jnp.sum(dv2 * vh, axis=0, keepdims=True)
        dz_ref[:, 1024:2048] = (_rms_bwd(dv2, gv, rv, vh) * dv1).astype(BF16)

        bx = z_ref[:, 3072:4096]
        bg = z_ref[:, 4096:5120].astype(F32)
        counts = _window_counts(i, TOK, s)
        halo = _with_halo(i, n_tiles, zp_ref, bx, zn_ref)
        ps = [p.astype(BF16) for p in _mixer_b_pooled(bx.astype(F32), halo, band_ref, counts)]
        pw = jnp.concatenate([_dot(ps[g], wg_ref[g]) for g in range(4)], axis=1)
        sgb = _sigmoid(bg)
        sc = sc_ref[...]
        dz_ref[:, 4096:5120] = (dyb * pw * sc * (sgb * (1.0 + bg * (1.0 - sgb)))).astype(BF16)
        dys = dyb * (bg * sgb)
        dsc_ref[...] += jnp.sum(dys * pw, axis=0, keepdims=True)
        dpw = (dys * sc).astype(BF16)
        for g in range(4):
            cols = slice(g * 256, (g + 1) * 256)
            dwg_ref[g] += _dot_tn(ps[g], dpw[:, cols])
            dpn_ref[:, cols] = (_dot_nt(dpw[:, cols], wg_ref[g]) / counts[g]).astype(BF16)

    prev, nxt = _halo_specs(s, 3)
    row = _rows(TOK, 1024)
    vec = _whole((1, 1024))
    return pl.pallas_call(
        rider.carried_by(body, 11, 7, n_tiles), name="mix0_bwd", grid=(n_tiles,),
        in_specs=[row, _rows(TOK, 5120), prev, nxt, _whole(w_out0.shape), vec, _whole(ws.shape),
                  _whole(bsx.shape), _whole(wg.shape), vec, _whole(band.shape)] + rider.in_specs,
        out_specs=[_rows(TOK, 5120), row, _whole((4, 128, 128)), _whole((128, 128)), vec, vec,
                   _whole((4, 256, 256))] + rider.out_specs,
        out_shape=[jax.ShapeDtypeStruct((s, 5120), BF16), jax.ShapeDtypeStruct((s, 1024), BF16),
                   jax.ShapeDtypeStruct((4, 128, 128), F32), jax.ShapeDtypeStruct((128, 128), F32),
                   jax.ShapeDtypeStruct((1, 1024), F32), jax.ShapeDtypeStruct((1, 1024), F32),
                   jax.ShapeDtypeStruct((4, 256, 256), F32)] + rider.out_shape,
        scratch_shapes=rider.scratch,
        compiler_params=_params(),
    )(dx1, z0, z0, z0, w_out0, gv, ws, bsx, wg, scale, band, *rider.parts)


def _fill_pooled_grad(dz0, dpn):
    s = dpn.shape[0]
    tok = min(s, 2 * TOK)
    n_tiles = s // tok
    band_t = _band_matrices(tok)[1]

    def body(dz_in_ref, dpn_ref, dpp_ref, dpx_ref, band_ref, dbx_ref):
        i = pl.program_id(0)
        dpn = dpn_ref[...]
        halo = _with_halo(i, n_tiles, dpp_ref, dpn, dpx_ref)
        counts = _window_counts(i, tok, s)
        for g in range(4):
            cols = slice(g * 256, (g + 1) * 256)
            dbx = _dot(band_ref[g], halo[:, cols]) - dpn[:, cols].astype(F32) * counts[g]
            dbx_ref[:, cols] = dbx.astype(BF16)

    prev, nxt = _halo_specs(s, 0, tok)
    return pl.pallas_call(
        body, name="fill_pooled_grad", grid=(n_tiles,),
        in_specs=[ANY, _rows(tok, 1024), prev, nxt, _whole(band_t.shape)],
        out_specs=pl.BlockSpec((tok, 1024), lambda i: (i, 3)),
        out_shape=jax.ShapeDtypeStruct(dz0.shape, BF16),
        input_output_aliases={0: 0},
        compiler_params=_params(),
    )(dz0, dpn, dpn, dpn, band_t)


def _in_proj0_bwd(dz0, x, dx1, g0, w_in0, rider, name, tiles, dx_so_far=None):
    s = x.shape[0]
    tok = min(s, 2 * TOK)
    first, n_tiles = tiles

    def body(dz_ref, x_ref, dx1_ref, g_ref, w_ref, dx_ref, gn_ref):
        i = pl.program_id(0)

        @pl.when(i == 0)
        def _():
            gn_ref[...] = jnp.zeros_like(gn_ref)

        halves = [slice(r * TOK, (r + 1) * TOK) for r in range(tok // TOK)]
        g0v = g_ref[...]
        dhs = [_dot_nt(dz_ref[rows, :], w_ref[...]) for rows in halves]
        for rows, dh in zip(halves, dhs):
            r, xh, _ = _rms_fwd(x_ref[rows, :], g0v)
            gn_ref[...] += jnp.sum(dh * xh, axis=0, keepdims=True)
            dx_ref[rows, :] = dx1_ref[rows, :] + _rms_bwd(dh, g0v, r, xh)

    row = pl.BlockSpec((tok, 1024), lambda i: (first + i, 0))
    vec = _whole((1, 1024))
    k = len(rider.parts)
    carried = [] if dx_so_far is None else [dx_so_far]
    if dx_so_far is not None:
        inner = body
        body = lambda *refs: inner(*refs[:5], *refs[6:])
    return pl.pallas_call(
        rider.carried_by(body, 5 + len(carried), 2, n_tiles), name=name, grid=(n_tiles,),
        in_specs=[pl.BlockSpec((tok, 5120), lambda i: (first + i, 0)), row, row, vec, _whole(w_in0.shape)]
        + [ANY] * len(carried) + rider.in_specs,
        out_specs=[row, vec] + rider.out_specs,
        out_shape=[jax.ShapeDtypeStruct((s, 1024), F32), jax.ShapeDtypeStruct((1, 1024), F32)] + rider.out_shape,
        input_output_aliases={5: 0} if carried else {},
        scratch_shapes=rider.scratch,
        compiler_params=_params(),
    )(dz0, x, dx1, g0, w_in0, *carried, *rider.parts)


def _weight_grad(a, b, n_blocks, split, name, rider=None):
    s, k = a.shape
    n = b.shape[1]
    tn = n // n_blocks
    w = tn // split
    ts = min(s, 1024)

    def body(a_ref, b_ref, o_ref):
        @pl.when(pl.program_id(1) == 0)
        def _():
            o_ref[...] = jnp.zeros_like(o_ref)

        res = _dot_tn(a_ref[...], b_ref[...])
        for q in range(split):
            o_ref[q] += res[:, q * w:(q + 1) * w]

    in_specs = [pl.BlockSpec((ts, k), lambda j, t: (t, 0)), pl.BlockSpec((ts, tn), lambda j, t: (t, j))]
    out_spec = pl.BlockSpec((split, k, w), lambda j, t: (j, 0, 0))
    out_shape = jax.ShapeDtypeStruct((n_blocks * split, k, w), F32)
    if rider is None:
        return pl.pallas_call(body, name=name, grid=(n_blocks, s // ts), in_specs=in_specs, out_specs=out_spec,
                              out_shape=out_shape, compiler_params=_params())(a, b)
    steps = s // ts
    return pl.pallas_call(
        rider.carried_by(body, 2, 1, n_blocks * steps, lambda: pl.program_id(0) * steps + pl.program_id(1)),
        name=name, grid=(n_blocks, steps), in_specs=in_specs + rider.in_specs,
        out_specs=[out_spec] + rider.out_specs, out_shape=[out_shape] + rider.out_shape,
        scratch_shapes=rider.scratch, compiler_params=_params(),
    )(a, b, *rider.parts)


def _row_tile(rows, cols):
    t = rows
    while t * cols * 4 > (1 << 21) and t % 32 == 0:
        t //= 2
    return t


def _add_sibling(where, g, theirs, name):
    _, _, rows, cols = g.shape
    t = _row_tile(rows, cols)

    def body(where_ref, g_ref, t_ref, o_ref):
        o_ref[...] = (g_ref[...] + t_ref[...]).astype(BF16)

    spec = pl.BlockSpec((None, t, cols), lambda s, i, p: (s, i, 0))
    return pl.pallas_call(
        body, name=name, out_shape=jax.ShapeDtypeStruct((4, rows, cols), BF16),
        grid_spec=pltpu.PrefetchScalarGridSpec(
            num_scalar_prefetch=1, grid=(4, rows // t),
            in_specs=[pl.BlockSpec((None, None, t, cols), lambda s, i, p: (s, p[1], i, 0)), spec], out_specs=spec),
        compiler_params=_params())(where, g, theirs)


def _sum_chips(where, g, theirs, slots, name):
    _, _, rows, cols = g.shape
    t = _row_tile(rows, cols)

    def body(where_ref, g_ref, t_ref, s_ref, o_ref):
        me = where_ref[0]
        own = g_ref[...] + t_ref[...]
        acc = jnp.where(me == 0, own, s_ref[0].astype(F32))
        for k in range(1, 4):
            acc = acc + jnp.where(me == k, own, s_ref[k].astype(F32))
        o_ref[...] = acc

    return pl.pallas_call(
        body, name=name, out_shape=jax.ShapeDtypeStruct((rows, cols), F32),
        grid_spec=pltpu.PrefetchScalarGridSpec(
            num_scalar_prefetch=1, grid=(rows // t,),
            in_specs=[pl.BlockSpec((None, None, t, cols), lambda i, p: (p[0], p[1], i, 0)),
                      pl.BlockSpec((None, t, cols), lambda i, p: (p[0], i, 0)),
                      pl.BlockSpec((4, t, cols), lambda i, p: (0, i, 0))],
            out_specs=pl.BlockSpec((t, cols), lambda i, p: (i, 0))),
        compiler_params=_params())(where, g, theirs, slots)


def _adamw_halves(where, w, own, theirs, m, v, name):
    rows, cols = own.shape
    t = _row_tile(rows, cols)
    per = rows // t

    def body(where_ref, w_ref, own_ref, th_ref, m_ref, v_ref, g_ref, d_ref, nm_ref, nv_ref):
        g = jnp.where(pl.program_id(0) == where_ref[1], own_ref[...], th_ref[...])
        g_ref[...] = g
        m2 = ADAM_B1 * m_ref[...] + (1.0 - ADAM_B1) * g
        v2 = ADAM_B2 * v_ref[...] + (1.0 - ADAM_B2) * (g * g)
        m_hat = m2 / (1.0 - ADAM_B1 ** ADAM_STEP)
        v_hat = v2 / (1.0 - ADAM_B2 ** ADAM_STEP)
        d_ref[...] = -ADAM_LR * (m_hat / (jnp.sqrt(v_hat) + ADAM_EPS) + ADAM_WD * w_ref[...])
        nm_ref[...] = m2
        nv_ref[...] = v2

    full = pl.BlockSpec((t, cols), lambda h, i, p: (h * per + i, 0))
    half = pl.BlockSpec((t, cols), lambda h, i, p: (i, 0))
    shp = jax.ShapeDtypeStruct(w.shape, F32)
    return pl.pallas_call(
        body, name=name, out_shape=[shp] * 4,
        grid_spec=pltpu.PrefetchScalarGridSpec(
            num_scalar_prefetch=1, grid=(2, per), in_specs=[full, half, half, full, full], out_specs=[full] * 4),
        compiler_params=_params())(where, w, own, theirs, m, v)


def _place_shard(where, w, cut, name):
    if cut.kind == "cols":
        r, n = cut.full_shape
        blk, grid = (256, n // 4), (r // 256,)
        src_map, dst_map = (lambda i, p: (i, 0)), (lambda i, p: (i, p[0]))
    elif cut.kind == "rows":
        r, n = cut.full_shape
        per = r // 4 // 256
        blk, grid = (256, n), (per,)
        src_map, dst_map = (lambda i, p: (i, 0)), (lambda i, p: (p[0] * per + i, 0))
    else:
        g, r, n = cut.full_shape
        blk, grid = (g, r // 4, n), (1,)
        src_map, dst_map = (lambda i, p: (0, 0, 0)), (lambda i, p: (0, p[0], 0))

    def body(where_ref, w_ref, o_ref):
        o_ref[...] = w_ref[...].astype(BF16)

    return pl.pallas_call(
        body, name=name, out_shape=jax.ShapeDtypeStruct(cut.full_shape, BF16),
        grid_spec=pltpu.PrefetchScalarGridSpec(
            num_scalar_prefetch=1, grid=grid, in_specs=[pl.BlockSpec(blk, src_map)],
            out_specs=pl.BlockSpec(blk, dst_map)),
        compiler_params=_params())(where, w)


def _sum_small(first, second, third):
    rows = second.shape[1]

    def body(a_ref, b_ref, c_ref, o_ref):
        top = a_ref[0] + c_ref[0] + b_ref[0, 0:8]
        rest = b_ref[0, 8:rows]
        for k in range(1, 8):
            top = top + (a_ref[k] + c_ref[k] + b_ref[k, 0:8])
            rest = rest + b_ref[k, 8:rows]
        o_ref[0:8] = top
        o_ref[8:rows] = rest

    return pl.pallas_call(
        body, name="sum_small", in_specs=[_whole(first.shape), _whole(second.shape), _whole(third.shape)],
        out_specs=_whole(second.shape[1:]), out_shape=jax.ShapeDtypeStruct(second.shape[1:], F32),
        compiler_params=_params())(first, second, third)


def _adamw(w, g, m, v, name):
    rows, cols = w.shape
    t = _row_tile(rows, cols)

    def body(w_ref, g_ref, m_ref, v_ref, d_ref, nm_ref, nv_ref):
        g = g_ref[...]
        m2 = ADAM_B1 * m_ref[...] + (1.0 - ADAM_B1) * g
        v2 = ADAM_B2 * v_ref[...] + (1.0 - ADAM_B2) * (g * g)
        m_hat = m2 / (1.0 - ADAM_B1 ** ADAM_STEP)
        v_hat = v2 / (1.0 - ADAM_B2 ** ADAM_STEP)
        d_ref[...] = -ADAM_LR * (m_hat / (jnp.sqrt(v_hat) + ADAM_EPS) + ADAM_WD * w_ref[...])
        nm_ref[...] = m2
        nv_ref[...] = v2

    spec = pl.BlockSpec((t, cols), lambda i: (i, 0))
    shp = jax.ShapeDtypeStruct(w.shape, F32)
    return pl.pallas_call(body, name=name, grid=(rows // t,), in_specs=[spec] * 4, out_specs=[spec] * 3,
                          out_shape=[shp] * 3, compiler_params=_params())(w, g, m, v)


def _place():
    x, y, c = lax.axis_index("x"), lax.axis_index("y"), lax.axis_index("c")
    chips = [(1 - x, y), (x, 1 - y), (1 - x, 1 - y)]
    return x, y, c, chips


class _Sharded:
    def __init__(self, kind, full_shape):
        self.kind = kind
        self.full_shape = full_shape

    def in_full(self, ref, s, h):
        if self.kind == "cols":
            r, n = self.full_shape
            return ref.at[pl.ds(h * (r // 2), r // 2), pl.ds(pl.multiple_of(s * (n // 4), 128), n // 4)]
        if self.kind == "rows":
            r, _ = self.full_shape
            return ref.at[pl.ds(pl.multiple_of(s * (r // 4) + h * (r // 8), 8), r // 8), :]
        g, r, _ = self.full_shape
        return ref.at[pl.ds(h * (g // 2), g // 2), pl.ds(pl.multiple_of(s * (r // 4), 16), r // 4), :]


def _remote(src, dst, send_sem, recv_sem, to):
    return pltpu.make_async_remote_copy(src_ref=src, dst_ref=dst, send_sem=send_sem, recv_sem=recv_sem,
                                        device_id=to, device_id_type=MESH)


def _start_remote(src, dst, send_sem, recv_sem, to):
    cp = _remote(src, dst, send_sem, recv_sem, to)
    cp.start()
    return cp


class _Gather:
    def __init__(self, fulls, cuts):
        n = len(fulls)
        self.fulls, self.cuts = list(fulls), list(cuts)
        self.in_specs = [ANY] * n
        self.out_specs = [ANY] * n
        self.out_shape = [jax.ShapeDtypeStruct(cut.full_shape, BF16) for cut in cuts]
        self.scratch = [pltpu.SemaphoreType.DMA((6 * n,)), pltpu.SemaphoreType.DMA((6 * n,))]

    def _step(self, step, src, out, send_sems, recv_sems):
        n, cuts = len(self.fulls), self.cuts
        x, y, c, chips = _place()
        me = 2 * x + y

        def ends(w, s, h, from_src):
            dst = cuts[w].in_full(out[w], s, h)
            return (cuts[w].in_full(src[w], s, h) if from_src else dst), dst

        for w in range(n):
            for j, chip in enumerate(chips):
                s = 2 * chip[0] + chip[1]
                k, k2 = 3 * w + j, 3 * n + 3 * w + j
                if step == "send":
                    _start_remote(*ends(w, me, c, True), send_sems.at[k], recv_sems.at[k], (*chip, c))
                elif step == "pass_on":
                    _remote(*ends(w, s, c, False), send_sems.at[k], recv_sems.at[k], (x, y, c)).wait_recv()
                    _start_remote(*ends(w, s, c, False), send_sems.at[k2], recv_sems.at[k2], (x, y, 1 - c))
                else:
                    _remote(*ends(w, s, 1 - c, False), send_sems.at[k2], recv_sems.at[k2], (x, y, c)).wait_recv()
                    _remote(*ends(w, me, c, True), send_sems.at[k], recv_sems.at[k], (x, y, c)).wait_send()
                    _remote(*ends(w, s, c, False), send_sems.at[k2], recv_sems.at[k2], (x, y, c)).wait_send()

    def carried_by(self, body, n_in, n_out, n_steps, step_index=lambda: pl.program_id(0)):
        k = len(self.fulls)

        def carrier(*refs):
            ins, src = refs[:n_in], refs[n_in:n_in + k]
            outs, out = refs[n_in + k:n_in + k + n_out], refs[n_in + k + n_out:n_in + 2 * k + n_out]
            sems = refs[n_in + 2 * k + n_out:]
            for step, at in (("send", 0), ("pass_on", 3 * n_steps // 4), ("finish", n_steps - 1)):
                if step == "finish":
                    body(*ins, *outs)

                @pl.when(step_index() == at)
                def _():
                    self._step(step, src, out, *sems)

        return carrier


def _gather_small(small_ref, gathered, send_sems, recv_sems, first_sem, local_sem, start):
    x, y, c, _ = _place()
    me = 4 * x + 2 * y + c
    flips = [(fx, fy, fc) for fx in range(2) for fy in range(2) for fc in range(2)][1:]
    own = pltpu.make_async_copy(small_ref, gathered.at[me], local_sem)
    if start:
        own.start()
    else:
        own.wait()
    for k, (fx, fy, fc) in enumerate(flips):
        peer = (x + fx - 2 * x * fx, y + fy - 2 * y * fy, c + fc - 2 * c * fc)
        sems = (send_sems.at[first_sem + k], recv_sems.at[first_sem + k])
        if start:
            _start_remote(small_ref, gathered.at[me], *sems, peer)
        else:
            cp = _remote(small_ref, gathered.at[4 * peer[0] + 2 * peer[1] + peer[2]], *sems, (x, y, c))
            cp.wait_recv()
            cp.wait_send()


class _Rider:
    def __init__(self, kind, parts, small=None):
        n = len(parts)
        self.kind, self.n = kind, n
        self.per = 3 if kind == "scatter" else 1
        self.parts = list(parts) + ([] if small is None else [small])
        k = len(self.parts)
        self.in_specs = [ANY] * k
        self.out_specs = [ANY] * k
        dtype = lambda a: a.dtype if kind == "scatter" else F32
        self.out_shape = [jax.ShapeDtypeStruct((4,) + a.shape[-2:], dtype(a)) for a in parts]
        if small is not None:
            self.out_shape.append(jax.ShapeDtypeStruct((8,) + small.shape, small.dtype))
        self.scratch = [pltpu.SemaphoreType.DMA((self.per * n + 7,)), pltpu.SemaphoreType.DMA((self.per * n + 7,)),
                        pltpu.SemaphoreType.DMA]

    def _copies(self, p, out, send_sems, recv_sems, local_sem, start):
        x, y, c, chips = _place()
        me = 2 * x + y
        n = self.n
        if len(self.parts) > n:
            _gather_small(p[n], out[n], send_sems, recv_sems, self.per * n, local_sem, start)
        for w in range(n):
            if self.kind == "exchange":
                ends = [(p[w].at[:, 1 - c], out[w], out[w], (x, y, 1 - c))]
            else:
                ends = [(p[w].at[2 * cx + cy], out[w].at[me], out[w].at[2 * cx + cy], (cx, cy, c)) for cx, cy in chips]
            for j, (src, dst_there, dst_here, to) in enumerate(ends):
                sems = (send_sems.at[self.per * w + j], recv_sems.at[self.per * w + j])
                if start:
                    _start_remote(src, dst_there, *sems, to)
                else:
                    cp = _remote(src, dst_here, *sems, (x, y, c))
                    cp.wait_recv()
                    cp.wait_send()

    def carried_by(self, body, n_in, n_out, n_steps, step_index=lambda: pl.program_id(0)):
        k = len(self.parts)

        def carrier(*refs):
            ins, mine = refs[:n_in], refs[n_in:n_in + k]
            outs, theirs = refs[n_in + k:n_in + k + n_out], refs[n_in + k + n_out:n_in + 2 * k + n_out]
            sems = refs[n_in + 2 * k + n_out:]

            @pl.when(step_index() == 0)
            def _():
                self._copies(mine, theirs, *sems, start=True)

            body(*ins, *outs)

            @pl.when(step_index() == n_steps - 1)
            def _():
                self._copies(mine, theirs, *sems, start=False)

        return carrier


def _share_halves(halves, small):
    n = len(halves)

    def body(*refs):
        hv = refs[:n]
        small_ref = refs[n]
        out = refs[n + 1:2 * n + 1]
        gathered = refs[2 * n + 1]
        send_sems, recv_sems, local_sem = refs[2 * n + 2:]
        x, y, c, _ = _place()
        sends = [_start_remote(hv[w], out[w], send_sems.at[w], recv_sems.at[w], (x, y, 1 - c)) for w in range(n)]
        _gather_small(small_ref, gathered, send_sems, recv_sems, n, local_sem, True)
        for w in range(n):
            _remote(hv[w], out[w], send_sems.at[w], recv_sems.at[w], (x, y, c)).wait_recv()
        for cp in sends:
            cp.wait_send()
        _gather_small(small_ref, gathered, send_sems, recv_sems, n, local_sem, False)

    return pl.pallas_call(
        body, name="share_halves",
        in_specs=[ANY] * (n + 1), out_specs=[ANY] * (n + 1),
        out_shape=[jax.ShapeDtypeStruct(a.shape, F32) for a in halves] + [jax.ShapeDtypeStruct((8,) + small.shape, F32)],
        scratch_shapes=[pltpu.SemaphoreType.DMA((n + 7,)), pltpu.SemaphoreType.DMA((n + 7,)),
                        pltpu.SemaphoreType.DMA],
        compiler_params=pltpu.CompilerParams(has_side_effects=True),
    )(*halves, small)


SMALL_ROWS = 80


def _pack_small(vecs, ws, bs, sink, extra=None):
    ws = jnp.zeros((64, 1024), F32) if ws is None else ws.reshape(64, 1024)
    bs = jnp.zeros((1, 512), F32) if bs is None else bs.reshape(1, 512)
    sink = jnp.zeros((1, 16), F32) if sink is None else sink.reshape(1, 16)
    extra = jnp.zeros((1, 1024), F32) if extra is None else extra.reshape(1, 1024)
    top = jnp.concatenate(
        [v.reshape(1, 1024) for v in vecs]
        + [jnp.pad(bs, ((0, 0), (0, 512))), jnp.pad(sink, ((0, 0), (0, 1008))), extra], axis=0)
    return jnp.concatenate([top, ws, jnp.zeros((8, 1024), F32)], axis=0)


def _unpack_small(p):
    vecs = [p[k] for k in range(5)]
    return vecs, p[8:72].reshape(4, 128, 128), p[5, :512].reshape(4, 128), p[6, :16]


def kernel(x, norm_0, w_in_0, a_v_norm_0, a_spatial_w_0, a_spatial_b_0, b_group_w_0, b_scale_0, w_out_0, norm_1, w_in_1, sink_1, w_out_1, final_norm, loss_target, m_norm_0, m_w_in_0, m_a_v_norm_0, m_a_spatial_w_0, m_a_spatial_b_0, m_b_group_w_0, m_b_scale_0, m_w_out_0, m_norm_1, m_w_in_1, m_sink_1, m_w_out_1, m_final_norm, v_norm_0, v_w_in_0, v_a_v_norm_0, v_a_spatial_w_0, v_a_spatial_b_0, v_b_group_w_0, v_b_scale_0, v_w_out_0, v_norm_1, v_w_in_1, v_sink_1, v_w_out_1, v_final_norm):
    s = x.shape[1]
    xs = x.reshape(s, D_MODEL)
    target = loss_target.reshape(s, D_MODEL)

    cuts = [_Sharded("cols", (1024, 5120)), _Sharded("rows", (2048, 1024)), _Sharded("cols", (1024, 2560)),
            _Sharded("rows", (1024, 1024)), _Sharded("mid", (4, 256, 256))]
    big_w = [w_in_0, w_out_0, w_in_1, w_out_1, b_group_w_0]
    big_m = [m_w_in_0, m_w_out_0, m_w_in_1, m_w_out_1, m_b_group_w_0]
    big_v = [v_w_in_0, v_w_out_0, v_w_in_1, v_w_out_1, v_b_group_w_0]
    where = jnp.stack([2 * lax.axis_index("x") + lax.axis_index("y"), lax.axis_index("c")]).astype(jnp.int32)
    placed = [_place_shard(where, w, cut, f"place_shard{k}") for k, (w, cut) in enumerate(zip(big_w, cuts))]
    cx, cy = lax.axis_index("x"), lax.axis_index("y")
    order = jnp.stack([2 * cx + cy, 2 * (1 - cx) + cy, 2 * cx + 1 - cy, 2 * (1 - cx) + 1 - cy]).astype(jnp.int32)

    row = lambda v: v.reshape(1, 1024)
    ws16 = a_spatial_w_0.astype(BF16)
    bsx = jnp.repeat(a_spatial_b_0.T, 256, axis=1)
    band = _band_matrices(TOK)[0]
    rope = _rope_tables(s)

    h0, z0, w_in0 = _in_proj0_own(order, xs, row(norm_0), w_in_0.astype(BF16), _Gather(placed[:1], cuts[:1]))
    z0, w_out0, wg = _in_proj0_rest(order, h0, w_in0, z0, _Gather([placed[1], placed[4]], [cuts[1], cuts[4]]))
    cat, x1, w_in1, w_out1 = _mix0_fwd(xs, z0, w_out0, row(a_v_norm_0), ws16, bsx, wg, row(b_scale_0), band,
                                       _Gather(placed[2:4], cuts[2:4]))
    h1, q, k, v, gate = _in_proj1(x1, row(norm_1), w_in1, rope)
    kpad = jnp.pad(k, ((ATTN_WINDOW, ATTN_WINDOW), (0, 0)))
    vpad = jnp.pad(v, ((ATTN_WINDOW, ATTN_WINDOW), (0, 0)))
    o, lse = _attn_fwd(q, kpad, vpad, sink_1)
    y1, dx2, do, dgate, loss_lanes, g_final, dx2h = _tail(x1, o, gate, target, w_out1, row(final_norm))

    dq, dkpad, dvpad, dsink = _attn_bwd(q, kpad, vpad, sink_1, o, lse, do, rope)
    dz1, dx1, g_norm1, dx1h = _in_proj1_bwd(dq, dkpad, dvpad, dgate, x1, dx2, row(norm_1), w_in1, rope)

    g_w_in1 = _weight_grad(h1, dz1, 2, 2, "grad_w_in1").reshape(4, 2, 512, 640)
    g_w_out1 = _weight_grad(y1, dx2h, 1, 1, "grad_w_out1").reshape(4, 2, 128, 1024)
    g_w_out0 = _weight_grad(cat, dx1h, 1, 1, "grad_w_out0").reshape(4, 2, 256, 1024)
    first = [g_w_out0, g_w_in1, g_w_out1]
    zero = jnp.zeros((1024,), F32)
    small1 = _pack_small([zero, zero, zero, g_norm1, g_final], None, None, dsink[0, :16])[:8]
    dz0, dpn, d_ws, d_bs, d_gv, d_scale, d_wg, *theirs1, small1_all = _mix0_bwd(
        dx1h, z0, w_out0, row(a_v_norm_0), ws16, bsx, wg, row(b_scale_0), band, _Rider("exchange", first, small1))
    parts1 = [_add_sibling(where, g, t, f"add_sibling1_{k}") for k, (g, t) in enumerate(zip(first, theirs1))]
    dz0 = _fill_pooled_grad(dz0, dpn)
    g_w_in0, *slots1 = _weight_grad(h0, dz0, 4, 1, "grad_w_in0", _Rider("scatter", parts1))
    g_w_in0 = g_w_in0.reshape(4, 2, 512, 1280)
    g_wg = d_wg.reshape(2, 2, 4, 64, 256).transpose(2, 0, 1, 3, 4).reshape(4, 2, 128, 256)
    second = [g_w_in0, g_wg]
    n_tiles = s // min(s, 2 * TOK)
    dx_half, g_norm0_a, *theirs2 = _in_proj0_bwd(
        dz0, xs, dx1, row(norm_0), w_in0, _Rider("exchange", second), "in_proj0_bwd_a", (0, n_tiles // 2))
    parts2 = [_add_sibling(where, g, t, f"add_sibling2_{k}") for k, (g, t) in enumerate(zip(second, theirs2))]
    small2 = _pack_small([zero, d_gv, d_scale, zero, zero], d_ws, d_bs[:, :4].T, None)
    grad_x, g_norm0_b, *slots2, small2_all = _in_proj0_bwd(
        dz0, xs, dx1, row(norm_0), w_in0, _Rider("scatter", parts2, small2), "in_proj0_bwd_b",
        (n_tiles // 2, n_tiles - n_tiles // 2), dx_half)
    g_norm0 = g_norm0_a + g_norm0_b

    grads = [g_w_in0, g_w_out0, g_w_in1, g_w_out1, g_wg]
    theirs = [theirs2[0], theirs1[0], theirs1[1], theirs1[2], theirs2[1]]
    slots = [slots2[0], slots1[0], slots1[1], slots1[2], slots2[1]]
    n = len(grads)
    reduced = [_sum_chips(where, grads[w], theirs[w], slots[w], f"sum_chips{w}") for w in range(n)]
    small3 = _pack_small([g_norm0, zero, zero, zero, zero], None, None, None, loss_lanes)[:8]
    *from_sibling, small3_all = _share_halves(reduced, small3)

    out_g, out_d, out_m, out_v = {}, {}, {}, {}
    names = ["w_in_0", "w_out_0", "w_in_1", "w_out_1", "b_group_w_0"]
    for w in range(n):
        shape = big_w[w].shape
        two_d = (-1, shape[-1])
        outs = _adamw_halves(where, big_w[w].reshape(two_d), reduced[w], from_sibling[w], big_m[w].reshape(two_d),
                             big_v[w].reshape(two_d), f"adamw{w}")
        out_g[names[w]], out_d[names[w]], out_m[names[w]], out_v[names[w]] = (a.reshape(shape) for a in outs)

    g_small = _sum_small(small1_all, small2_all, small3_all)
    small_names = ["norm_0", "a_v_norm_0", "b_scale_0", "norm_1", "final_norm"]
    pack = lambda vecs, ws_, bs_, sk: _pack_small(vecs, ws_, bs_, sk)
    w_small = pack([norm_0, a_v_norm_0, b_scale_0, norm_1, final_norm], a_spatial_w_0, a_spatial_b_0, sink_1)
    m_small = pack([m_norm_0, m_a_v_norm_0, m_b_scale_0, m_norm_1, m_final_norm], m_a_spatial_w_0,
                   m_a_spatial_b_0, m_sink_1)
    v_small = pack([v_norm_0, v_a_v_norm_0, v_b_scale_0, v_norm_1, v_final_norm], v_a_spatial_w_0,
                   v_a_spatial_b_0, v_sink_1)
    d_small, nm_small, nv_small = _adamw(w_small, g_small, m_small, v_small, "adamw_small")
    for store, packed in ((out_g, g_small), (out_d, d_small), (out_m, nm_small), (out_v, nv_small)):
        vecs, ws_, bs_, sk = _unpack_small(packed)
        for name, vec in zip(small_names, vecs):
            store[name] = vec
        store["a_spatial_w_0"], store["a_spatial_b_0"], store["sink_1"] = ws_, bs_, sk

    loss = jnp.sum(g_small[7])
    order = ["norm_0", "w_in_0", "a_v_norm_0", "a_spatial_w_0", "a_spatial_b_0", "b_group_w_0", "b_scale_0",
             "w_out_0", "norm_1", "w_in_1", "sink_1", "w_out_1", "final_norm"]
    return (loss, grad_x.reshape(1, s, D_MODEL), *[out_g[k] for k in order], *[out_d[k] for k in order],
            *[out_m[k] for k in order], *[out_v[k] for k in order])
```

```python
import functools

import numpy as np
import jax
import jax.numpy as jnp
from jax import lax
from jax.experimental import pallas as pl
from jax.experimental.pallas import tpu as pltpu

F32 = jnp.float32
BF16 = jnp.bfloat16
MESH = pl.DeviceIdType.MESH

D_MODEL = 1024
EPS = 1e-6
NEG_INF = -1e30
CHUNK = 128
POOL_WINDOWS = (2, 4, 8, 16)
HALO = 16
N_HEADS = 16
HEAD_DIM = 64
ATTN_WINDOW = 128
ROPE_THETA = 500000.0
ROT_DIM = 16
ADAM_LR = 0.001
ADAM_B1 = 0.9
ADAM_B2 = 0.999
ADAM_EPS = 1e-08
ADAM_WD = 0.01
ADAM_STEP = 10

TOK = 256
VMEM_LIMIT = 56 * 1024 * 1024


def _params(**kw):
    return pltpu.CompilerParams(vmem_limit_bytes=VMEM_LIMIT, **kw)


def _whole(shape):
    nd = len(shape)
    return pl.BlockSpec(shape, lambda *_: (0,) * nd)


def _rows(t, n):
    return pl.BlockSpec((t, n), lambda i: (i, 0))


ANY = pl.BlockSpec(memory_space=pl.ANY)

_G0 = 0.7978845608028654
_G1 = 0.044715


def _gelu(x):
    return 0.5 * x * (1.0 + jnp.tanh(_G0 * (x + _G1 * x * x * x)))


def _gelu_and_grad(x):
    x2 = x * x
    t = jnp.tanh(_G0 * (x + _G1 * x2 * x))
    half = 0.5 * (1.0 + t)
    return x * half, half + 0.5 * x * (1.0 - t * t) * (_G0 * (1.0 + 3.0 * _G1 * x2))


def _sigmoid(x):
    return 1.0 / (1.0 + jnp.exp(-x))


def _dot(a, b):
    return jnp.dot(a, b, preferred_element_type=F32)


def _dot_nt(a, b):
    return lax.dot_general(a, b, (((1,), (1,)), ((), ())), preferred_element_type=F32)


def _dot_tn(a, b):
    return lax.dot_general(a, b, (((0,), (0,)), ((), ())), preferred_element_type=F32)


def _rms_fwd(x, g):
    r = lax.rsqrt(jnp.mean(x * x, axis=-1, keepdims=True) + EPS)
    xh = x * r
    return r, xh, xh * g


def _rms_bwd(dy, g, r, xh):
    dxh = dy * g
    return r * (dxh - xh * jnp.mean(dxh * xh, axis=-1, keepdims=True))


def _band_matrices(t):
    r = np.arange(t)[:, None]
    j = np.arange(t + 2 * HALO)[None, :]
    fwd, bwd = [], []
    for w in POOL_WINDOWS:
        d = j - r - HALO
        fwd.append((d >= -(w // 2)) & (d < w // 2))
        bwd.append((d >= -(w // 2) + 1) & (d <= w // 2))
    return (jnp.asarray(np.stack(fwd), BF16), jnp.asarray(np.stack(bwd), BF16))


def _window_counts(i, t, s):
    tok = i * t + lax.broadcasted_iota(jnp.int32, (t, 1), 0)
    out = []
    for w in POOL_WINDOWS:
        cnt = jnp.minimum(tok + w // 2, s) - jnp.maximum(tok - w // 2, 0)
        out.append(cnt.astype(F32))
    return out


def _rope_tables(s):
    inv = np.float32(ROPE_THETA) ** (-np.arange(0, ROT_DIM, 2, dtype=np.float32) / np.float32(ROT_DIM))
    ang = np.arange(s, dtype=np.float32)[:, None] * inv.astype(np.float32)[None, :]
    cos, sin = np.cos(ang).astype(np.float32), np.sin(ang).astype(np.float32)
    z8 = np.zeros((s, 8), np.float32)
    z48 = np.zeros((s, HEAD_DIM - ROT_DIM), np.float32)
    c = np.concatenate([cos, cos, np.ones((s, HEAD_DIM - ROT_DIM), np.float32)], axis=1)
    s_lo = np.concatenate([z8, sin, z48], axis=1)
    s_hi = np.concatenate([-sin, z8, z48], axis=1)
    return tuple(jnp.asarray(np.concatenate([a, a], axis=1)) for a in (c, s_lo, s_hi))


def _rope(x, c, s_lo, s_hi):
    n = x.shape[1]
    reps = n // 128
    c, s_lo, s_hi = (jnp.tile(a, (1, reps)) for a in (c, s_lo, s_hi))
    return x * c + pltpu.roll(x, 8, 1) * s_lo + pltpu.roll(x, n - 8, 1) * s_hi


def _rope_t(dx, c, s_lo, s_hi):
    n = dx.shape[1]
    reps = n // 128
    c, s_lo, s_hi = (jnp.tile(a, (1, reps)) for a in (c, s_lo, s_hi))
    return dx * c + pltpu.roll(dx * s_lo, n - 8, 1) + pltpu.roll(dx * s_hi, 8, 1)


def _in_proj0_own(order, x, g0, w_own, rider):
    s = x.shape[0]
    n = w_own.shape[1]

    def body(order_ref, x_ref, g_ref, w_ref, h_ref, z_ref):
        _, _, h = _rms_fwd(x_ref[...], g_ref[...])
        h = h.astype(BF16)
        h_ref[...] = h
        z_ref[...] = _dot(h, w_ref[...]).astype(BF16)

    return pl.pallas_call(
        rider.carried_by(body, 4, 2, s // TOK, late=True), name="in_proj0_own",
        grid_spec=pltpu.PrefetchScalarGridSpec(
            num_scalar_prefetch=1, grid=(s // TOK,),
            in_specs=[pl.BlockSpec((TOK, D_MODEL), lambda i, o: (i, 0)), pl.BlockSpec((1, D_MODEL), lambda i, o: (0, 0)),
                      pl.BlockSpec(w_own.shape, lambda i, o: (0, 0))] + rider.in_specs,
            out_specs=[pl.BlockSpec((TOK, D_MODEL), lambda i, o: (i, 0)),
                       pl.BlockSpec((TOK, n), lambda i, o: (i, o[0]))] + rider.out_specs,
            scratch_shapes=rider.scratch),
        out_shape=[jax.ShapeDtypeStruct((s, D_MODEL), BF16), jax.ShapeDtypeStruct((s, 4 * n), BF16)] + rider.out_shape,
        input_output_aliases={4 + j: 2 + j for j in range(len(rider.fulls))},
        compiler_params=_params(),
    )(order, x, g0, w_own, *rider.fulls)


def _in_proj0_rest(order, h0, w_in0, z0, rider):
    s = h0.shape[0]
    tok = min(s, 4 * TOK)
    n_tiles = s // tok
    n = w_in0.shape[1] // 4

    def body(order_ref, h_ref, w_ref, z_in_ref, z_ref):
        z_ref[...] = _dot(h_ref[...], w_ref[...]).astype(BF16)

    return pl.pallas_call(
        rider.carried_by(body, 4, 1, 3 * n_tiles, lambda: pl.program_id(0) * n_tiles + pl.program_id(1)),
        name="in_proj0_rest",
        grid_spec=pltpu.PrefetchScalarGridSpec(
            num_scalar_prefetch=1, grid=(3, n_tiles),
            in_specs=[pl.BlockSpec((tok, D_MODEL), lambda k, i, o: (i, 0)),
                      pl.BlockSpec((w_in0.shape[0], n), lambda k, i, o: (0, o[1 + k])), ANY] + rider.in_specs,
            out_specs=[pl.BlockSpec((tok, n), lambda k, i, o: (i, o[1 + k]))] + rider.out_specs,
            scratch_shapes=rider.scratch),
        out_shape=[jax.ShapeDtypeStruct(z0.shape, BF16)] + rider.out_shape,
        input_output_aliases={3: 0, **{4 + j: 1 + j for j in range(len(rider.fulls))}},
        compiler_params=_params(),
    )(order, h0, w_in0, z0, *rider.fulls)


def _halo_specs(s, col_block, tok=TOK):
    per = tok // HALO
    last = s // HALO - 1
    prev = pl.BlockSpec((HALO, 1024), lambda i: (jnp.maximum(i * per - 1, 0), col_block))
    nxt = pl.BlockSpec((HALO, 1024), lambda i: (jnp.minimum((i + 1) * per, last), col_block))
    return prev, nxt


def _with_halo(i, n_tiles, prev_ref, cur, next_ref):
    prev = prev_ref[...]
    nxt = next_ref[...]
    prev = jnp.where(i > 0, prev, jnp.zeros_like(prev))
    nxt = jnp.where(i < n_tiles - 1, nxt, jnp.zeros_like(nxt))
    return jnp.concatenate([prev, cur, nxt], axis=0)


def _mixer_a(v1, gv, ws_ref, bsx):
    rv, vh, v2 = _rms_fwd(v1, gv)
    v2 = v2.astype(BF16)
    rows = []
    for c in range(v1.shape[0] // CHUNK):
        cols = [_dot(ws_ref[h], v2[c * CHUNK:(c + 1) * CHUNK, h * 256:(h + 1) * 256]) for h in range(4)]
        rows.append(jnp.concatenate(cols, axis=1) + bsx)
    return rv, vh, v2, jnp.concatenate(rows, axis=0)


def _mixer_b_pooled(bx, halo, band_ref, counts):
    out = []
    for g in range(4):
        win = _dot(band_ref[g], halo[:, g * 256:(g + 1) * 256])
        out.append(win / counts[g] - bx[:, g * 256:(g + 1) * 256])
    return out


def _mix0_fwd(x, z0, w_out0, gv, ws, bsx, wg, scale, band, rider):
    s = x.shape[0]
    n_tiles = s // TOK
    k = len(rider.fulls)

    def body(z_ref, zp_ref, zn_ref, x_ref, wout_ref, gv_ref, ws_ref, bsx_ref, wg_ref, sc_ref, band_ref,
             cat_ref, x1_ref):
        i = pl.program_id(0)
        au = z_ref[:, 0:1024].astype(F32)
        av = z_ref[:, 1024:2048].astype(F32)
        ag = z_ref[:, 2048:3072].astype(F32)
        _, _, _, mixed = _mixer_a(_gelu(av), gv_ref[...], ws_ref, bsx_ref[...])
        cat_ref[:, 0:1024] = (_gelu(au) * mixed * (ag * _sigmoid(ag))).astype(BF16)

        bx = z_ref[:, 3072:4096]
        bg = z_ref[:, 4096:5120].astype(F32)
        halo = _with_halo(i, n_tiles, zp_ref, bx, zn_ref)
        ps = _mixer_b_pooled(bx.astype(F32), halo, band_ref, _window_counts(i, TOK, s))
        pw = jnp.concatenate([_dot(ps[g].astype(BF16), wg_ref[g]) for g in range(4)], axis=1)
        cat_ref[:, 1024:2048] = (pw * sc_ref[...] * (bg * _sigmoid(bg))).astype(BF16)

        x1_ref[...] = x_ref[...] + _dot(cat_ref[...], wout_ref[...])

    prev, nxt = _halo_specs(s, 3)
    return pl.pallas_call(
        rider.carried_by(body, 11, 2, n_tiles), name="mix0_fwd", grid=(n_tiles,),
        in_specs=[_rows(TOK, 5120), prev, nxt, _rows(TOK, D_MODEL), _whole(w_out0.shape), _whole((1, 1024)),
                  _whole(ws.shape), _whole(bsx.shape), _whole(wg.shape), _whole((1, 1024)), _whole(band.shape)]
        + rider.in_specs,
        out_specs=[_rows(TOK, 2048), _rows(TOK, D_MODEL)] + rider.out_specs,
        out_shape=[jax.ShapeDtypeStruct((s, 2048), BF16), jax.ShapeDtypeStruct((s, D_MODEL), F32)] + rider.out_shape,
        input_output_aliases={11 + j: 2 + j for j in range(k)},
        scratch_shapes=rider.scratch,
        compiler_params=_params(),
    )(z0, z0, z0, x, w_out0, gv, ws, bsx, wg, scale, band, *rider.fulls)


def _in_proj1(x1, g1, w_in1, rope):
    s = x1.shape[0]

    def body(x_ref, g_ref, w_ref, c_ref, lo_ref, hi_ref, h_ref, q_ref, k_ref, v_ref, gate_ref):
        halves = [slice(r * TOK, (r + 1) * TOK) for r in range(tok // TOK)]
        hs = [_rms_fwd(x_ref[rows, :], g_ref[...])[2].astype(BF16) for rows in halves]
        for rows, h in zip(halves, hs):
            h_ref[rows, :] = h
        qs = [_dot(h, w_ref[:, 0:1024]) for h in hs]
        kvs = [_dot(h, w_ref[:, 1024:1536]) for h in hs]
        for rows, q, kv in zip(halves, qs, kvs):
            tabs = (c_ref[rows, :], lo_ref[rows, :], hi_ref[rows, :])
            q_ref[rows, :] = (_rope(q, *tabs) * Q_SCALE).astype(BF16)
            k_ref[rows, :] = _rope(kv[:, 0:256], *tabs).astype(BF16)
            v_ref[rows, :] = kv[:, 256:512].astype(BF16)
        for rows, h in zip(halves, hs):
            gate_ref[rows, :] = _dot(h, w_ref[:, 1536:2560]).astype(BF16)

    tok = min(s, 2 * TOK)
    tab = _rows(tok, 128)
    return pl.pallas_call(
        body, name="in_proj1", grid=(s // tok,),
        in_specs=[_rows(tok, D_MODEL), _whole((1, D_MODEL)), _whole(w_in1.shape), tab, tab, tab],
        out_specs=[_rows(tok, 1024), _rows(tok, 1024), _rows(tok, 256), _rows(tok, 256), _rows(tok, 1024)],
        out_shape=[jax.ShapeDtypeStruct((s, 1024), BF16), jax.ShapeDtypeStruct((s, 1024), BF16),
                   jax.ShapeDtypeStruct((s, 256), BF16), jax.ShapeDtypeStruct((s, 256), BF16),
                   jax.ShapeDtypeStruct((s, 1024), BF16)],
        compiler_params=_params(),
    )(x1, g1, w_in1, *rope)


QBLK = 128
KBLK = QBLK + 2 * ATTN_WINDOW
Q_SCALE = HEAD_DIM ** -0.5


def _block_bias(q0, s):
    r = lax.broadcasted_iota(jnp.int32, (QBLK, KBLK), 0)
    c = lax.broadcasted_iota(jnp.int32, (QBLK, KBLK), 1)
    kj = q0 - ATTN_WINDOW + c
    ok = (c >= r) & (c <= r + 2 * ATTN_WINDOW) & (kj >= 0) & (kj < s)
    return jnp.where(ok, 0.0, NEG_INF)


def _pair_operands(t):
    lane = lax.broadcasted_iota(jnp.int32, (1, 128), 1)
    first = lane < HEAD_DIM
    zero = jnp.zeros((KBLK, 128), BF16)
    out = []
    for j in range(2):
        slab = t[:, 128 * j:128 * (j + 1)]
        turned = pltpu.bitcast(pltpu.roll(pltpu.bitcast(slab, jnp.uint32), HEAD_DIM, 1), BF16)
        for own_first in (True, False):
            top = jnp.where(first, slab if own_first else turned, zero)
            bottom = jnp.where(first, zero, turned if own_first else slab)
            out.append(jnp.concatenate([top, bottom], axis=0))
    return out


def _attn_fwd(q, kpad, vpad, sink):
    s = q.shape[0]

    def body(sink_ref, q_ref, k_ref, v_ref, o_ref, lse_ref):
        i = pl.program_id(0)
        lane = lax.broadcasted_iota(jnp.int32, (1, 128), 1)
        for b in range(TOK // QBLK):
            rows = slice(b * QBLK, (b + 1) * QBLK)
            start = pl.multiple_of(i * TOK + b * QBLK, QBLK)
            k_bd = _pair_operands(k_ref[pl.ds(start, KBLK), :])
            v_bd = _pair_operands(v_ref[pl.ds(start, KBLK), :])
            bias = _block_bias(i * TOK + b * QBLK, s)
            pairs = range(N_HEADS // 2)
            sc4 = [_dot_nt(jnp.concatenate([q_ref[rows, 256 * g:256 * g + 128], q_ref[rows, 256 * g + 128:256 * (g + 1)]],
                                           axis=0), k_bd[g]) for g in range(4)]
            sc2 = [sc4[m // 2][(m % 2) * QBLK:(m % 2 + 1) * QBLK] for m in pairs]
            scs = [sc2[h // 2][:, (h % 2) * KBLK:(h % 2 + 1) * KBLK] + bias for h in range(N_HEADS)]
            ms = [jnp.maximum(jnp.max(scs[h], axis=-1, keepdims=True), sink_ref[h]) for h in range(N_HEADS)]
            es = [jnp.exp(scs[h] - ms[h]) for h in range(N_HEADS)]
            dens = [jnp.sum(es[h], axis=-1, keepdims=True) + jnp.exp(sink_ref[h] - ms[h]) for h in range(N_HEADS)]
            first = lane < HEAD_DIM
            e2 = [jnp.concatenate([es[2 * m].astype(BF16), es[2 * m + 1].astype(BF16)], axis=1) for m in pairs]
            o4 = [_dot(jnp.concatenate([e2[2 * g], e2[2 * g + 1]], axis=0), v_bd[g]) for g in range(4)]
            outs = [o4[m // 2][(m % 2) * QBLK:(m % 2 + 1) * QBLK]
                    * jnp.where(first, 1.0 / dens[2 * m], 1.0 / dens[2 * m + 1]) for m in pairs]
            o_ref[rows, :] = jnp.concatenate(outs, axis=1).astype(BF16)
            lse = jnp.zeros((QBLK, 128), F32)
            for h in range(N_HEADS):
                lse = lse + jnp.where(lane == h, ms[h] + jnp.log(dens[h]), 0.0)
            lse_ref[rows, :] = lse

    return pl.pallas_call(
        body, name="attn_fwd", grid=(s // TOK,),
        in_specs=[pl.BlockSpec(memory_space=pltpu.SMEM), _rows(TOK, 1024), _whole(kpad.shape), _whole(vpad.shape)],
        out_specs=[_rows(TOK, 1024), _rows(TOK, 128)],
        out_shape=[jax.ShapeDtypeStruct((s, 1024), BF16), jax.ShapeDtypeStruct((s, 128), F32)],
        compiler_params=_params(),
    )(sink, q, kpad, vpad)


def _tail(x1, o, gate, target, w_out1, gf):
    s = x1.shape[0]

    def body(x1_ref, o_ref, gate_ref, t_ref, w_ref, gf_ref, y1_ref, dx2_ref, do_ref, dgate_ref, loss_ref, gfn_ref,
             dx2h_ref):
        i = pl.program_id(0)

        @pl.when(i == 0)
        def _():
            loss_ref[...] = jnp.zeros_like(loss_ref)
            gfn_ref[...] = jnp.zeros_like(gfn_ref)

        halves = [slice(r * TOK, (r + 1) * TOK) for r in range(tok // TOK)]
        gf = gf_ref[...]
        gs = [gate_ref[rows, :].astype(F32) for rows in halves]
        sgs = [_sigmoid(g) for g in gs]
        sils = [g * sg for g, sg in zip(gs, sgs)]
        os_ = [o_ref[rows, :].astype(F32) for rows in halves]
        y1s = [(o * sil).astype(BF16) for o, sil in zip(os_, sils)]
        for rows, y1 in zip(halves, y1s):
            y1_ref[rows, :] = y1
        x2s = [x1_ref[rows, :] + _dot(y1, w_ref[...]) for rows, y1 in zip(halves, y1s)]
        dx2hs = []
        for rows, x2 in zip(halves, x2s):
            r, xh, out = _rms_fwd(x2, gf)
            diff = out - t_ref[rows, :]
            loss_ref[...] += jnp.sum(diff * diff, axis=0, keepdims=True) * (0.5 / D_MODEL)
            dout = diff * (1.0 / D_MODEL)
            gfn_ref[...] += jnp.sum(dout * xh, axis=0, keepdims=True)
            dx2 = _rms_bwd(dout, gf, r, xh)
            dx2_ref[rows, :] = dx2
            dx2hs.append(dx2.astype(BF16))
            dx2h_ref[rows, :] = dx2hs[-1]
        dy1s = [_dot_nt(dx2h, w_ref[...]) for dx2h in dx2hs]
        for rows, dy1, sil, o, sg, g in zip(halves, dy1s, sils, os_, sgs, gs):
            do_ref[rows, :] = (dy1 * sil).astype(BF16)
            dgate_ref[rows, :] = (dy1 * o * (sg * (1.0 + g * (1.0 - sg)))).astype(BF16)

    tok = min(s, 2 * TOK)
    row = _rows(tok, 1024)
    acc = _whole((1, 1024))
    return pl.pallas_call(
        body, name="tail", grid=(s // tok,),
        in_specs=[row, row, row, row, _whole(w_out1.shape), acc],
        out_specs=[row, row, row, row, acc, acc, row],
        out_shape=[jax.ShapeDtypeStruct((s, 1024), BF16), jax.ShapeDtypeStruct((s, 1024), F32),
                   jax.ShapeDtypeStruct((s, 1024), BF16), jax.ShapeDtypeStruct((s, 1024), BF16),
                   jax.ShapeDtypeStruct((1, 1024), F32), jax.ShapeDtypeStruct((1, 1024), F32),
                   jax.ShapeDtypeStruct((s, 1024), BF16)],
        compiler_params=_params(),
    )(x1, o, gate, target, w_out1, gf)


def _attn_bwd(q, kpad, vpad, sink, o, lse, do, rope):
    s = q.shape[0]
    pad_t = (kpad.shape[1], kpad.shape[0])

    def body(sink_ref, q_ref, k_ref, v_ref, o_ref, lse_ref, do_ref, c_ref, lo_ref, hi_ref,
             dq_ref, dk_ref, dv_ref, ds_ref):
        i = pl.program_id(0)

        @pl.when(i == 0)
        def _():
            dk_ref[...] = jnp.zeros_like(dk_ref)
            dv_ref[...] = jnp.zeros_like(dv_ref)
            ds_ref[...] = jnp.zeros_like(ds_ref)

        lane = lax.broadcasted_iota(jnp.int32, (1, 128), 1)
        dsink = jnp.zeros((1, 128), F32)
        for b in range(TOK // QBLK):
            rows = slice(b * QBLK, (b + 1) * QBLK)
            start = pl.multiple_of(i * TOK + b * QBLK, QBLK)
            k_bd = _pair_operands(k_ref[pl.ds(start, KBLK), :])
            v_bd = _pair_operands(v_ref[pl.ds(start, KBLK), :])
            bias = _block_bias(i * TOK + b * QBLK, s)
            lanes_of = (lane < HEAD_DIM, lane >= HEAD_DIM)
            half = lambda t, j: t[:, j * KBLK:(j + 1) * KBLK]
            dqs, dks, dvs = [], [], []
            for g in range(4):
                pairs = (2 * g, 2 * g + 1)
                qs = {m: q_ref[rows, 128 * m:128 * (m + 1)] for m in pairs}
                dos = {m: do_ref[rows, 128 * m:128 * (m + 1)] for m in pairs}
                lses = {h: lse_ref[rows, h:h + 1] for h in range(4 * g, 4 * g + 4)}
                stacked = lambda parts: jnp.concatenate([parts[m] for m in pairs], axis=0)
                unstack = lambda t: {m: t[j * QBLK:(j + 1) * QBLK] for j, m in enumerate(pairs)}
                sc2 = unstack(_dot_nt(stacked(qs), k_bd[g]))
                ps = {2 * m + j: jnp.exp(half(sc2[m], j) + bias - lses[2 * m + j]) for m in pairs for j in range(2)}
                prods = {m: dos[m].astype(F32) * o_ref[rows, 128 * m:128 * (m + 1)].astype(F32) for m in pairs}
                deltas = {2 * m + j: jnp.sum(jnp.where(lanes_of[j], prods[m], 0.0), axis=-1, keepdims=True)
                          for m in pairs for j in range(2)}
                for h in range(4 * g, 4 * g + 4):
                    dsink = dsink + jnp.where(
                        lane == h, -jnp.sum(jnp.exp(sink_ref[h] - lses[h]) * deltas[h], axis=0, keepdims=True), 0.0)
                dp2 = unstack(_dot_nt(stacked(dos), v_bd[g]))
                ds2 = {m: jnp.concatenate(
                    [(ps[2 * m + j] * (half(dp2[m], j) - deltas[2 * m + j])).astype(BF16) for j in range(2)], axis=1)
                    for m in pairs}
                p2 = {m: jnp.concatenate([ps[2 * m].astype(BF16), ps[2 * m + 1].astype(BF16)], axis=1) for m in pairs}
                diag = lambda t: t[0:HEAD_DIM, 0:KBLK] + t[HEAD_DIM:128, KBLK:2 * KBLK]
                dvs.append(diag(_dot_tn(stacked(dos), stacked(p2))))
                dks.append(diag(_dot_tn(stacked(qs), stacked(ds2))))
                dq2 = unstack(_dot(stacked(ds2), k_bd[g]) * Q_SCALE)
                dqs += [dq2[m] for m in pairs]
            dq = jnp.concatenate(dqs, axis=1)
            dq_ref[rows, :] = _rope_t(dq, c_ref[rows, :], lo_ref[rows, :], hi_ref[rows, :]).astype(BF16)
            dk_ref[:, pl.ds(start, KBLK)] += jnp.concatenate(dks, axis=0)
            dv_ref[:, pl.ds(start, KBLK)] += jnp.concatenate(dvs, axis=0)
        ds_ref[...] += dsink

    row = _rows(TOK, 1024)
    tab = _rows(TOK, 128)
    pad = _whole(kpad.shape)
    return pl.pallas_call(
        body, name="attn_bwd", grid=(s // TOK,),
        in_specs=[pl.BlockSpec(memory_space=pltpu.SMEM), row, pad, pad, row, tab, row, tab, tab, tab],
        out_specs=[row, _whole(pad_t), _whole(pad_t), _whole((1, 128))],
        out_shape=[jax.ShapeDtypeStruct((s, 1024), BF16), jax.ShapeDtypeStruct(pad_t, F32),
                   jax.ShapeDtypeStruct(pad_t, F32), jax.ShapeDtypeStruct((1, 128), F32)],
        compiler_params=_params(),
    )(sink, q, kpad, vpad, o, lse, do, *rope)


def _in_proj1_bwd(dq, dk_t, dv_t, dgate, x1, dx2, g1, w_in1, rope):
    s = x1.shape[0]
    tok = min(s, 2 * TOK)
    n_sub = tok // ATTN_WINDOW

    def body(*refs):
        dq_ref = refs[0]
        dk_refs, dv_refs = refs[1:1 + n_sub], refs[1 + n_sub:1 + 2 * n_sub]
        (dgate_ref, x1_ref, dx2_ref, g_ref, w_ref, c_ref, lo_ref, hi_ref,
         dz_ref, dx1_ref, gn_ref, dx1h_ref) = refs[1 + 2 * n_sub:]

        @pl.when(pl.program_id(0) == 0)
        def _():
            gn_ref[...] = jnp.zeros_like(gn_ref)

        halves = [slice(r * TOK, (r + 1) * TOK) for r in range(tok // TOK)]
        per = TOK // ATTN_WINDOW
        g = g_ref[...]
        for r, rows in enumerate(halves):
            dk = jnp.concatenate([ref[...] for ref in dk_refs[r * per:(r + 1) * per]], axis=1).T
            dv = jnp.concatenate([ref[...] for ref in dv_refs[r * per:(r + 1) * per]], axis=1).T
            dz_ref[rows, 0:1024] = dq_ref[rows, :]
            dz_ref[rows, 1024:1280] = _rope_t(dk, c_ref[rows, :], lo_ref[rows, :], hi_ref[rows, :]).astype(BF16)
            dz_ref[rows, 1280:1536] = dv.astype(BF16)
            dz_ref[rows, 1536:2560] = dgate_ref[rows, :]
        dhs = [_dot_nt(dz_ref[rows, :], w_ref[...]) for rows in halves]
        for rows, dh in zip(halves, dhs):
            r, xh, _ = _rms_fwd(x1_ref[rows, :], g)
            gn_ref[...] += jnp.sum(dh * xh, axis=0, keepdims=True)
            dx1 = dx2_ref[rows, :] + _rms_bwd(dh, g, r, xh)
            dx1_ref[rows, :] = dx1
            dx1h_ref[rows, :] = dx1.astype(BF16)

    row = _rows(tok, 1024)
    subs = [pl.BlockSpec((256, ATTN_WINDOW), lambda i, j=j: (0, n_sub * i + 1 + j)) for j in range(n_sub)]
    tab = _rows(tok, 128)
    acc = _whole((1, 1024))
    return pl.pallas_call(
        body, name="in_proj1_bwd", grid=(s // tok,),
        in_specs=[row] + subs + subs + [row, row, row, acc, _whole(w_in1.shape), tab, tab, tab],
        out_specs=[_rows(tok, 2560), row, acc, row],
        out_shape=[jax.ShapeDtypeStruct((s, 2560), BF16), jax.ShapeDtypeStruct((s, 1024), F32),
                   jax.ShapeDtypeStruct((1, 1024), F32), jax.ShapeDtypeStruct((s, 1024), BF16)],
        compiler_params=_params(),
    )(dq, *[dk_t] * n_sub, *[dv_t] * n_sub, dgate, x1, dx2, g1, w_in1, *rope)


def _mix0_bwd(dx1, z0, w_out0, gv, ws, bsx, wg, scale, band, rider):
    s = dx1.shape[0]
    n_tiles = s // TOK

    def body(dx1h_ref, z_ref, zp_ref, zn_ref, wout_ref, gv_ref, ws_ref, bsx_ref, wg_ref, sc_ref, band_ref,
             dz_ref, dpn_ref, dws_ref, dbs_ref, dgv_ref, dsc_ref, dwg_ref):
        i = pl.program_id(0)
        dz_ref[:, 3072:4096] = jnp.zeros((TOK, 1024), BF16)

        @pl.when(i == 0)
        def _():
            for ref in (dws_ref, dbs_ref, dgv_ref, dsc_ref, dwg_ref):
                ref[...] = jnp.zeros_like(ref)

        dcat = _dot_nt(dx1h_ref[...], wout_ref[...])
        dya = dcat[:, 0:1024]
        dyb = dcat[:, 1024:2048]

        au = z_ref[:, 0:1024].astype(F32)
        av = z_ref[:, 1024:2048].astype(F32)
        ag = z_ref[:, 2048:3072].astype(F32)
        gv = gv_ref[...]
        u, du = _gelu_and_grad(au)
        v1, dv1 = _gelu_and_grad(av)
        rv, vh, v2, mixed = _mixer_a(v1, gv, ws_ref, bsx_ref[...])
        sg = _sigmoid(ag)
        sil = ag * sg
        dz_ref[:, 2048:3072] = (dya * u * mixed * (sg * (1.0 + ag * (1.0 - sg)))).astype(BF16)
        dz_ref[:, 0:1024] = (dya * mixed * sil * du).astype(BF16)
        dmixed = dya * u * sil
        lane = lax.broadcasted_iota(jnp.int32, (1, 128), 1)
        dm16 = dmixed.astype(BF16)
        dv2_rows = []
        for c in range(TOK // CHUNK):
            rows = slice(c * CHUNK, (c + 1) * CHUNK)
            cols_out = []
            for h in range(4):
                cols = slice(h * 256, (h + 1) * 256)
                dws_ref[h] += _dot_nt(dm16[rows, cols], v2[rows, cols])
                dbs_ref[...] += jnp.where(lane == h, jnp.sum(dmixed[rows, cols], axis=-1, keepdims=True), 0.0)
                cols_out.append(_dot_tn(ws_ref[h], dm16[rows, cols]))
            dv2_rows.append(jnp.concatenate(cols_out, axis=1))
        dv2 = jnp.concatenate(dv2_rows, axis=0)
        dgv_ref[...] += jnp.sum(dv2 * vh, axis=0, keepdims=True)
        dz_ref[:, 1024:2048] = (_rms_bwd(dv2, gv, rv, vh) * dv1).astype(BF16)

        bx = z_ref[:, 3072:4096]
        bg = z_ref[:, 4096:5120].astype(F32)
        counts = _window_counts(i, TOK, s)
        halo = _with_halo(i, n_tiles, zp_ref, bx, zn_ref)
        ps = [p.astype(BF16) for p in _mixer_b_pooled(bx.astype(F32), halo, band_ref, counts)]
        pw = jnp.concatenate([_dot(ps[g], wg_ref[g]) for g in range(4)], axis=1)
        sgb = _sigmoid(bg)
        sc = sc_ref[...]
        dz_ref[:, 4096:5120] = (dyb * pw * sc * (sgb * (1.0 + bg * (1.0 - sgb)))).astype(BF16)
        dys = dyb * (bg * sgb)
        dsc_ref[...] += jnp.sum(dys * pw, axis=0, keepdims=True)
        dpw = (dys * sc).astype(BF16)
        for g in range(4):
            cols = slice(g * 256, (g + 1) * 256)
            dwg_ref[g] += _dot_tn(ps[g], dpw[:, cols])
            dpn_ref[:, cols] = (_dot_nt(dpw[:, cols], wg_ref[g]) / counts[g]).astype(BF16)

    prev, nxt = _halo_specs(s, 3)
    row = _rows(TOK, 1024)
    vec = _whole((1, 1024))
    return pl.pallas_call(
        rider.carried_by(body, 11, 7, n_tiles), name="mix0_bwd", grid=(n_tiles,),
        in_specs=[row, _rows(TOK, 5120), prev, nxt, _whole(w_out0.shape), vec, _whole(ws.shape),
                  _whole(bsx.shape), _whole(wg.shape), vec, _whole(band.shape)] + rider.in_specs,
        out_specs=[_rows(TOK, 5120), row, _whole((4, 128, 128)), _whole((128, 128)), vec, vec,
                   _whole((4, 256, 256))] + rider.out_specs,
        out_shape=[jax.ShapeDtypeStruct((s, 5120), BF16), jax.ShapeDtypeStruct((s, 1024), BF16),
                   jax.ShapeDtypeStruct((4, 128, 128), F32), jax.ShapeDtypeStruct((128, 128), F32),
                   jax.ShapeDtypeStruct((1, 1024), F32), jax.ShapeDtypeStruct((1, 1024), F32),
                   jax.ShapeDtypeStruct((4, 256, 256), F32)] + rider.out_shape,
        scratch_shapes=rider.scratch,
        compiler_params=_params(),
    )(dx1, z0, z0, z0, w_out0, gv, ws, bsx, wg, scale, band, *rider.parts)


def _fill_pooled_grad(dz0, dpn):
    s = dpn.shape[0]
    tok = min(s, 2 * TOK)
    n_tiles = s // tok
    band_t = _band_matrices(tok)[1]

    def body(dz_in_ref, dpn_ref, dpp_ref, dpx_ref, band_ref, dbx_ref):
        i = pl.program_id(0)
        dpn = dpn_ref[...]
        halo = _with_halo(i, n_tiles, dpp_ref, dpn, dpx_ref)
        counts = _window_counts(i, tok, s)
        for g in range(4):
            cols = slice(g * 256, (g + 1) * 256)
            dbx = _dot(band_ref[g], halo[:, cols]) - dpn[:, cols].astype(F32) * counts[g]
            dbx_ref[:, cols] = dbx.astype(BF16)

    prev, nxt = _halo_specs(s, 0, tok)
    return pl.pallas_call(
        body, name="fill_pooled_grad", grid=(n_tiles,),
        in_specs=[ANY, _rows(tok, 1024), prev, nxt, _whole(band_t.shape)],
        out_specs=pl.BlockSpec((tok, 1024), lambda i: (i, 3)),
        out_shape=jax.ShapeDtypeStruct(dz0.shape, BF16),
        input_output_aliases={0: 0},
        compiler_params=_params(),
    )(dz0, dpn, dpn, dpn, band_t)


def _in_proj0_bwd(dz0, x, dx1, g0, w_in0, rider):
    s = x.shape[0]
    tok = min(s, 2 * TOK)
    n_tiles = s // tok

    def body(dz_ref, x_ref, dx1_ref, g_ref, w_ref, dx_ref, gn_ref):
        i = pl.program_id(0)

        @pl.when(i == 0)
        def _():
            gn_ref[...] = jnp.zeros_like(gn_ref)

        halves = [slice(r * TOK, (r + 1) * TOK) for r in range(tok // TOK)]
        g0v = g_ref[...]
        dhs = [_dot_nt(dz_ref[rows, :], w_ref[...]) for rows in halves]
        for rows, dh in zip(halves, dhs):
            r, xh, _ = _rms_fwd(x_ref[rows, :], g0v)
            gn_ref[...] += jnp.sum(dh * xh, axis=0, keepdims=True)
            dx_ref[rows, :] = dx1_ref[rows, :] + _rms_bwd(dh, g0v, r, xh)

    row = _rows(tok, 1024)
    vec = _whole((1, 1024))
    return pl.pallas_call(
        rider.carried_by(body, 5, 2, n_tiles), name="in_proj0_bwd", grid=(n_tiles,),
        in_specs=[_rows(tok, 5120), row, row, vec, _whole(w_in0.shape)] + rider.in_specs,
        out_specs=[row, vec] + rider.out_specs,
        out_shape=[jax.ShapeDtypeStruct((s, 1024), F32), jax.ShapeDtypeStruct((1, 1024), F32)] + rider.out_shape,
        scratch_shapes=rider.scratch,
        compiler_params=_params(),
    )(dz0, x, dx1, g0, w_in0, *rider.parts)


def _weight_grad(a, b, n_blocks, split, name, rider=None):
    s, k = a.shape
    n = b.shape[1]
    tn = n // n_blocks
    w = tn // split
    ts = min(s, 1024)

    def body(a_ref, b_ref, o_ref):
        @pl.when(pl.program_id(1) == 0)
        def _():
            o_ref[...] = jnp.zeros_like(o_ref)

        res = _dot_tn(a_ref[...], b_ref[...])
        for q in range(split):
            o_ref[q] += res[:, q * w:(q + 1) * w]

    in_specs = [pl.BlockSpec((ts, k), lambda j, t: (t, 0)), pl.BlockSpec((ts, tn), lambda j, t: (t, j))]
    out_spec = pl.BlockSpec((split, k, w), lambda j, t: (j, 0, 0))
    out_shape = jax.ShapeDtypeStruct((n_blocks * split, k, w), F32)
    if rider is None:
        return pl.pallas_call(body, name=name, grid=(n_blocks, s // ts), in_specs=in_specs, out_specs=out_spec,
                              out_shape=out_shape, compiler_params=_params())(a, b)
    steps = s // ts
    return pl.pallas_call(
        rider.carried_by(body, 2, 1, n_blocks * steps, lambda: pl.program_id(0) * steps + pl.program_id(1)),
        name=name, grid=(n_blocks, steps), in_specs=in_specs + rider.in_specs,
        out_specs=[out_spec] + rider.out_specs, out_shape=[out_shape] + rider.out_shape,
        scratch_shapes=rider.scratch, compiler_params=_params(),
    )(a, b, *rider.parts)


def _row_tile(rows, cols):
    t = rows
    while t * cols * 4 > (1 << 21) and t % 32 == 0:
        t //= 2
    return t


def _add_sibling(where, g, theirs, name):
    _, _, rows, cols = g.shape
    t = _row_tile(rows, cols)

    def body(where_ref, g_ref, t_ref, o_ref):
        o_ref[...] = (g_ref[...] + t_ref[...]).astype(BF16)

    spec = pl.BlockSpec((None, t, cols), lambda s, i, p: (s, i, 0))
    return pl.pallas_call(
        body, name=name, out_shape=jax.ShapeDtypeStruct((4, rows, cols), BF16),
        grid_spec=pltpu.PrefetchScalarGridSpec(
            num_scalar_prefetch=1, grid=(4, rows // t),
            in_specs=[pl.BlockSpec((None, None, t, cols), lambda s, i, p: (s, p[1], i, 0)), spec], out_specs=spec),
        compiler_params=_params())(where, g, theirs)


def _sum_chips(where, g, theirs, slots, name):
    _, _, rows, cols = g.shape
    t = _row_tile(rows, cols)

    def body(where_ref, g_ref, t_ref, s_ref, o_ref):
        me = where_ref[0]
        own = g_ref[...] + t_ref[...]
        acc = jnp.where(me == 0, own, s_ref[0].astype(F32))
        for k in range(1, 4):
            acc = acc + jnp.where(me == k, own, s_ref[k].astype(F32))
        o_ref[...] = acc

    return pl.pallas_call(
        body, name=name, out_shape=jax.ShapeDtypeStruct((rows, cols), F32),
        grid_spec=pltpu.PrefetchScalarGridSpec(
            num_scalar_prefetch=1, grid=(rows // t,),
            in_specs=[pl.BlockSpec((None, None, t, cols), lambda i, p: (p[0], p[1], i, 0)),
                      pl.BlockSpec((None, t, cols), lambda i, p: (p[0], i, 0)),
                      pl.BlockSpec((4, t, cols), lambda i, p: (0, i, 0))],
            out_specs=pl.BlockSpec((t, cols), lambda i, p: (i, 0))),
        compiler_params=_params())(where, g, theirs, slots)


def _adamw_halves(where, w, own, theirs, m, v, name):
    rows, cols = own.shape
    t = _row_tile(rows, cols)
    per = rows // t

    def body(where_ref, w_ref, own_ref, th_ref, m_ref, v_ref, g_ref, d_ref, nm_ref, nv_ref):
        g = jnp.where(pl.program_id(0) == where_ref[1], own_ref[...], th_ref[...])
        g_ref[...] = g
        m2 = ADAM_B1 * m_ref[...] + (1.0 - ADAM_B1) * g
        v2 = ADAM_B2 * v_ref[...] + (1.0 - ADAM_B2) * (g * g)
        m_hat = m2 / (1.0 - ADAM_B1 ** ADAM_STEP)
        v_hat = v2 / (1.0 - ADAM_B2 ** ADAM_STEP)
        d_ref[...] = -ADAM_LR * (m_hat / (jnp.sqrt(v_hat) + ADAM_EPS) + ADAM_WD * w_ref[...])
        nm_ref[...] = m2
        nv_ref[...] = v2

    full = pl.BlockSpec((t, cols), lambda h, i, p: (h * per + i, 0))
    half = pl.BlockSpec((t, cols), lambda h, i, p: (i, 0))
    shp = jax.ShapeDtypeStruct(w.shape, F32)
    return pl.pallas_call(
        body, name=name, out_shape=[shp] * 4,
        grid_spec=pltpu.PrefetchScalarGridSpec(
            num_scalar_prefetch=1, grid=(2, per), in_specs=[full, half, half, full, full], out_specs=[full] * 4),
        compiler_params=_params())(where, w, own, theirs, m, v)


def _place_shard(where, w, cut, name):
    if cut.kind == "cols":
        r, n = cut.full_shape
        blk, grid = (256, n // 4), (r // 256,)
        src_map, dst_map = (lambda i, p: (i, 0)), (lambda i, p: (i, p[0]))
    elif cut.kind == "rows":
        r, n = cut.full_shape
        per = r // 4 // 256
        blk, grid = (256, n), (per,)
        src_map, dst_map = (lambda i, p: (i, 0)), (lambda i, p: (p[0] * per + i, 0))
    else:
        g, r, n = cut.full_shape
        blk, grid = (g, r // 4, n), (1,)
        src_map, dst_map = (lambda i, p: (0, 0, 0)), (lambda i, p: (0, p[0], 0))

    def body(where_ref, w_ref, o_ref):
        o_ref[...] = w_ref[...].astype(BF16)

    return pl.pallas_call(
        body, name=name, out_shape=jax.ShapeDtypeStruct(cut.full_shape, BF16),
        grid_spec=pltpu.PrefetchScalarGridSpec(
            num_scalar_prefetch=1, grid=grid, in_specs=[pl.BlockSpec(blk, src_map)],
            out_specs=pl.BlockSpec(blk, dst_map)),
        compiler_params=_params())(where, w)


def _sum_small(first, second, third):
    rows = second.shape[1]

    def body(a_ref, b_ref, c_ref, o_ref):
        top = a_ref[0] + c_ref[0] + b_ref[0, 0:8]
        rest = b_ref[0, 8:rows]
        for k in range(1, 8):
            top = top + (a_ref[k] + c_ref[k] + b_ref[k, 0:8])
            rest = rest + b_ref[k, 8:rows]
        o_ref[0:8] = top
        o_ref[8:rows] = rest

    return pl.pallas_call(
        body, name="sum_small", in_specs=[_whole(first.shape), _whole(second.shape), _whole(third.shape)],
        out_specs=_whole(second.shape[1:]), out_shape=jax.ShapeDtypeStruct(second.shape[1:], F32),
        compiler_params=_params())(first, second, third)


def _adamw(w, g, m, v, name):
    rows, cols = w.shape
    t = _row_tile(rows, cols)

    def body(w_ref, g_ref, m_ref, v_ref, d_ref, nm_ref, nv_ref):
        g = g_ref[...]
        m2 = ADAM_B1 * m_ref[...] + (1.0 - ADAM_B1) * g
        v2 = ADAM_B2 * v_ref[...] + (1.0 - ADAM_B2) * (g * g)
        m_hat = m2 / (1.0 - ADAM_B1 ** ADAM_STEP)
        v_hat = v2 / (1.0 - ADAM_B2 ** ADAM_STEP)
        d_ref[...] = -ADAM_LR * (m_hat / (jnp.sqrt(v_hat) + ADAM_EPS) + ADAM_WD * w_ref[...])
        nm_ref[...] = m2
        nv_ref[...] = v2

    spec = pl.BlockSpec((t, cols), lambda i: (i, 0))
    shp = jax.ShapeDtypeStruct(w.shape, F32)
    return pl.pallas_call(body, name=name, grid=(rows // t,), in_specs=[spec] * 4, out_specs=[spec] * 3,
                          out_shape=[shp] * 3, compiler_params=_params())(w, g, m, v)


def _place():
    x, y, c = lax.axis_index("x"), lax.axis_index("y"), lax.axis_index("c")
    chips = [(1 - x, y), (x, 1 - y), (1 - x, 1 - y)]
    return x, y, c, chips


class _Sharded:
    def __init__(self, kind, full_shape):
        self.kind = kind
        self.full_shape = full_shape

    def in_full(self, ref, s, h):
        if self.kind == "cols":
            r, n = self.full_shape
            return ref.at[pl.ds(h * (r // 2), r // 2), pl.ds(pl.multiple_of(s * (n // 4), 128), n // 4)]
        if self.kind == "rows":
            r, _ = self.full_shape
            return ref.at[pl.ds(pl.multiple_of(s * (r // 4) + h * (r // 8), 8), r // 8), :]
        g, r, _ = self.full_shape
        return ref.at[pl.ds(h * (g // 2), g // 2), pl.ds(pl.multiple_of(s * (r // 4), 16), r // 4), :]


def _remote(src, dst, send_sem, recv_sem, to):
    return pltpu.make_async_remote_copy(src_ref=src, dst_ref=dst, send_sem=send_sem, recv_sem=recv_sem,
                                        device_id=to, device_id_type=MESH)


def _start_remote(src, dst, send_sem, recv_sem, to):
    cp = _remote(src, dst, send_sem, recv_sem, to)
    cp.start()
    return cp


class _Gather:
    def __init__(self, fulls, cuts):
        n = len(fulls)
        self.fulls, self.cuts = list(fulls), list(cuts)
        self.in_specs = [ANY] * n
        self.out_specs = [ANY] * n
        self.out_shape = [jax.ShapeDtypeStruct(cut.full_shape, BF16) for cut in cuts]
        self.scratch = [pltpu.SemaphoreType.DMA((6 * n,)), pltpu.SemaphoreType.DMA((6 * n,))]

    def _step(self, step, src, out, send_sems, recv_sems):
        n, cuts = len(self.fulls), self.cuts
        x, y, c, chips = _place()
        me = 2 * x + y

        def ends(w, s, h, from_src):
            dst = cuts[w].in_full(out[w], s, h)
            return (cuts[w].in_full(src[w], s, h) if from_src else dst), dst

        for w in range(n):
            for j, chip in enumerate(chips):
                s = 2 * chip[0] + chip[1]
                k, k2 = 3 * w + j, 3 * n + 3 * w + j
                if step == "send":
                    _start_remote(*ends(w, me, c, True), send_sems.at[k], recv_sems.at[k], (*chip, c))
                elif step == "pass_on":
                    _remote(*ends(w, s, c, False), send_sems.at[k], recv_sems.at[k], (x, y, c)).wait_recv()
                    _start_remote(*ends(w, s, c, False), send_sems.at[k2], recv_sems.at[k2], (x, y, 1 - c))
                else:
                    _remote(*ends(w, s, 1 - c, False), send_sems.at[k2], recv_sems.at[k2], (x, y, c)).wait_recv()
                    _remote(*ends(w, me, c, True), send_sems.at[k], recv_sems.at[k], (x, y, c)).wait_send()
                    _remote(*ends(w, s, c, False), send_sems.at[k2], recv_sems.at[k2], (x, y, c)).wait_send()

    def carried_by(self, body, n_in, n_out, n_steps, step_index=lambda: pl.program_id(0), late=False):
        k = len(self.fulls)

        def carrier(*refs):
            ins, src = refs[:n_in], refs[n_in:n_in + k]
            outs, out = refs[n_in + k:n_in + k + n_out], refs[n_in + k + n_out:n_in + 2 * k + n_out]
            sems = refs[n_in + 2 * k + n_out:]

            def at_step(step, at):
                @pl.when(step_index() == at)
                def _():
                    self._step(step, src, out, *sems)

            at_step("send", 0)
            if not late:
                at_step("pass_on", 3 * n_steps // 4)
            body(*ins, *outs)
            if late:
                at_step("pass_on", n_steps - 1)
            at_step("finish", n_steps - 1)

        return carrier


def _exchange_halves(grads, name):
    n = len(grads)

    def body(*refs):
        g = refs[:n]
        theirs = refs[n:2 * n]
        send_sems, recv_sems = refs[2 * n:]
        x, y, c, _ = _place()
        sends = [_start_remote(g[w].at[:, 1 - c], theirs[w], send_sems.at[w], recv_sems.at[w], (x, y, 1 - c))
                 for w in range(n)]
        for w in range(n):
            _remote(g[w].at[:, 1 - c], theirs[w], send_sems.at[w], recv_sems.at[w], (x, y, c)).wait_recv()
        for cp in sends:
            cp.wait_send()

    return pl.pallas_call(
        body, name=name,
        in_specs=[ANY] * n, out_specs=[ANY] * n,
        out_shape=[jax.ShapeDtypeStruct((4,) + g.shape[2:], F32) for g in grads],
        scratch_shapes=[pltpu.SemaphoreType.DMA((n,)), pltpu.SemaphoreType.DMA((n,))],
        compiler_params=pltpu.CompilerParams(has_side_effects=True),
    )(*grads)


def _gather_small(small_ref, gathered, send_sems, recv_sems, first_sem, local_sem, start):
    x, y, c, _ = _place()
    me = 4 * x + 2 * y + c
    flips = [(fx, fy, fc) for fx in range(2) for fy in range(2) for fc in range(2)][1:]
    own = pltpu.make_async_copy(small_ref, gathered.at[me], local_sem)
    if start:
        own.start()
    else:
        own.wait()
    for k, (fx, fy, fc) in enumerate(flips):
        peer = (x + fx - 2 * x * fx, y + fy - 2 * y * fy, c + fc - 2 * c * fc)
        sems = (send_sems.at[first_sem + k], recv_sems.at[first_sem + k])
        if start:
            _start_remote(small_ref, gathered.at[me], *sems, peer)
        else:
            cp = _remote(small_ref, gathered.at[4 * peer[0] + 2 * peer[1] + peer[2]], *sems, (x, y, c))
            cp.wait_recv()
            cp.wait_send()


class _Rider:
    def __init__(self, kind, parts, small=None):
        n = len(parts)
        self.kind, self.n = kind, n
        self.per = 3 if kind == "scatter" else 1
        self.parts = list(parts) + ([] if small is None else [small])
        k = len(self.parts)
        self.in_specs = [ANY] * k
        self.out_specs = [ANY] * k
        dtype = lambda a: a.dtype if kind == "scatter" else F32
        self.out_shape = [jax.ShapeDtypeStruct((4,) + a.shape[-2:], dtype(a)) for a in parts]
        if small is not None:
            self.out_shape.append(jax.ShapeDtypeStruct((8,) + small.shape, small.dtype))
        self.scratch = [pltpu.SemaphoreType.DMA((self.per * n + 7,)), pltpu.SemaphoreType.DMA((self.per * n + 7,)),
                        pltpu.SemaphoreType.DMA]

    def _copies(self, p, out, send_sems, recv_sems, local_sem, start):
        x, y, c, chips = _place()
        me = 2 * x + y
        n = self.n
        if len(self.parts) > n:
            _gather_small(p[n], out[n], send_sems, recv_sems, self.per * n, local_sem, start)
        for w in range(n):
            if self.kind == "exchange":
                ends = [(p[w].at[:, 1 - c], out[w], out[w], (x, y, 1 - c))]
            else:
                ends = [(p[w].at[2 * cx + cy], out[w].at[me], out[w].at[2 * cx + cy], (cx, cy, c)) for cx, cy in chips]
            for j, (src, dst_there, dst_here, to) in enumerate(ends):
                sems = (send_sems.at[self.per * w + j], recv_sems.at[self.per * w + j])
                if start:
                    _start_remote(src, dst_there, *sems, to)
                else:
                    cp = _remote(src, dst_here, *sems, (x, y, c))
                    cp.wait_recv()
                    cp.wait_send()

    def carried_by(self, body, n_in, n_out, n_steps, step_index=lambda: pl.program_id(0)):
        k = len(self.parts)

        def carrier(*refs):
            ins, mine = refs[:n_in], refs[n_in:n_in + k]
            outs, theirs = refs[n_in + k:n_in + k + n_out], refs[n_in + k + n_out:n_in + 2 * k + n_out]
            sems = refs[n_in + 2 * k + n_out:]

            @pl.when(step_index() == 0)
            def _():
                self._copies(mine, theirs, *sems, start=True)

            body(*ins, *outs)

            @pl.when(step_index() == n_steps - 1)
            def _():
                self._copies(mine, theirs, *sems, start=False)

        return carrier


def _share_halves(halves, small):
    n = len(halves)

    def body(*refs):
        hv = refs[:n]
        small_ref = refs[n]
        out = refs[n + 1:2 * n + 1]
        gathered = refs[2 * n + 1]
        send_sems, recv_sems, local_sem = refs[2 * n + 2:]
        x, y, c, _ = _place()
        sends = [_start_remote(hv[w], out[w], send_sems.at[w], recv_sems.at[w], (x, y, 1 - c)) for w in range(n)]
        _gather_small(small_ref, gathered, send_sems, recv_sems, n, local_sem, True)
        for w in range(n):
            _remote(hv[w], out[w], send_sems.at[w], recv_sems.at[w], (x, y, c)).wait_recv()
        for cp in sends:
            cp.wait_send()
        _gather_small(small_ref, gathered, send_sems, recv_sems, n, local_sem, False)

    return pl.pallas_call(
        body, name="share_halves",
        in_specs=[ANY] * (n + 1), out_specs=[ANY] * (n + 1),
        out_shape=[jax.ShapeDtypeStruct(a.shape, F32) for a in halves] + [jax.ShapeDtypeStruct((8,) + small.shape, F32)],
        scratch_shapes=[pltpu.SemaphoreType.DMA((n + 7,)), pltpu.SemaphoreType.DMA((n + 7,)),
                        pltpu.SemaphoreType.DMA],
        compiler_params=pltpu.CompilerParams(has_side_effects=True),
    )(*halves, small)


SMALL_ROWS = 80


def _pack_small(vecs, ws, bs, sink, extra=None):
    ws = jnp.zeros((64, 1024), F32) if ws is None else ws.reshape(64, 1024)
    bs = jnp.zeros((1, 512), F32) if bs is None else bs.reshape(1, 512)
    sink = jnp.zeros((1, 16), F32) if sink is None else sink.reshape(1, 16)
    extra = jnp.zeros((1, 1024), F32) if extra is None else extra.reshape(1, 1024)
    top = jnp.concatenate(
        [v.reshape(1, 1024) for v in vecs]
        + [jnp.pad(bs, ((0, 0), (0, 512))), jnp.pad(sink, ((0, 0), (0, 1008))), extra], axis=0)
    return jnp.concatenate([top, ws, jnp.zeros((8, 1024), F32)], axis=0)


def _unpack_small(p):
    vecs = [p[k] for k in range(5)]
    return vecs, p[8:72].reshape(4, 128, 128), p[5, :512].reshape(4, 128), p[6, :16]


def kernel(x, norm_0, w_in_0, a_v_norm_0, a_spatial_w_0, a_spatial_b_0, b_group_w_0, b_scale_0, w_out_0, norm_1, w_in_1, sink_1, w_out_1, final_norm, loss_target, m_norm_0, m_w_in_0, m_a_v_norm_0, m_a_spatial_w_0, m_a_spatial_b_0, m_b_group_w_0, m_b_scale_0, m_w_out_0, m_norm_1, m_w_in_1, m_sink_1, m_w_out_1, m_final_norm, v_norm_0, v_w_in_0, v_a_v_norm_0, v_a_spatial_w_0, v_a_spatial_b_0, v_b_group_w_0, v_b_scale_0, v_w_out_0, v_norm_1, v_w_in_1, v_sink_1, v_w_out_1, v_final_norm):
    s = x.shape[1]
    xs = x.reshape(s, D_MODEL)
    target = loss_target.reshape(s, D_MODEL)

    cuts = [_Sharded("cols", (1024, 5120)), _Sharded("rows", (2048, 1024)), _Sharded("cols", (1024, 2560)),
            _Sharded("rows", (1024, 1024)), _Sharded("mid", (4, 256, 256))]
    big_w = [w_in_0, w_out_0, w_in_1, w_out_1, b_group_w_0]
    big_m = [m_w_in_0, m_w_out_0, m_w_in_1, m_w_out_1, m_b_group_w_0]
    big_v = [v_w_in_0, v_w_out_0, v_w_in_1, v_w_out_1, v_b_group_w_0]
    where = jnp.stack([2 * lax.axis_index("x") + lax.axis_index("y"), lax.axis_index("c")]).astype(jnp.int32)
    placed = [_place_shard(where, w, cut, f"place_shard{k}") for k, (w, cut) in enumerate(zip(big_w, cuts))]
    cx, cy = lax.axis_index("x"), lax.axis_index("y")
    order = jnp.stack([2 * cx + cy, 2 * (1 - cx) + cy, 2 * cx + 1 - cy, 2 * (1 - cx) + 1 - cy]).astype(jnp.int32)

    row = lambda v: v.reshape(1, 1024)
    ws16 = a_spatial_w_0.astype(BF16)
    bsx = jnp.repeat(a_spatial_b_0.T, 256, axis=1)
    band = _band_matrices(TOK)[0]
    rope = _rope_tables(s)

    h0, z0, w_in0 = _in_proj0_own(order, xs, row(norm_0), w_in_0.astype(BF16), _Gather(placed[:1], cuts[:1]))
    z0, w_out0, wg = _in_proj0_rest(order, h0, w_in0, z0, _Gather([placed[1], placed[4]], [cuts[1], cuts[4]]))
    cat, x1, w_in1, w_out1 = _mix0_fwd(xs, z0, w_out0, row(a_v_norm_0), ws16, bsx, wg, row(b_scale_0), band,
                                       _Gather(placed[2:4], cuts[2:4]))
    h1, q, k, v, gate = _in_proj1(x1, row(norm_1), w_in1, rope)
    kpad = jnp.pad(k, ((ATTN_WINDOW, ATTN_WINDOW), (0, 0)))
    vpad = jnp.pad(v, ((ATTN_WINDOW, ATTN_WINDOW), (0, 0)))
    o, lse = _attn_fwd(q, kpad, vpad, sink_1)
    y1, dx2, do, dgate, loss_lanes, g_final, dx2h = _tail(x1, o, gate, target, w_out1, row(final_norm))

    dq, dkpad, dvpad, dsink = _attn_bwd(q, kpad, vpad, sink_1, o, lse, do, rope)
    dz1, dx1, g_norm1, dx1h = _in_proj1_bwd(dq, dkpad, dvpad, dgate, x1, dx2, row(norm_1), w_in1, rope)

    g_w_in1 = _weight_grad(h1, dz1, 2, 2, "grad_w_in1").reshape(4, 2, 512, 640)
    g_w_out1 = _weight_grad(y1, dx2h, 1, 1, "grad_w_out1").reshape(4, 2, 128, 1024)
    g_w_out0 = _weight_grad(cat, dx1h, 1, 1, "grad_w_out0").reshape(4, 2, 256, 1024)
    first = [g_w_out0, g_w_in1, g_w_out1]
    zero = jnp.zeros((1024,), F32)
    small1 = _pack_small([zero, zero, zero, g_norm1, g_final], None, None, dsink[0, :16])[:8]
    dz0, dpn, d_ws, d_bs, d_gv, d_scale, d_wg, *theirs1, small1_all = _mix0_bwd(
        dx1h, z0, w_out0, row(a_v_norm_0), ws16, bsx, wg, row(b_scale_0), band, _Rider("exchange", first, small1))
    parts1 = [_add_sibling(where, g, t, f"add_sibling1_{k}") for k, (g, t) in enumerate(zip(first, theirs1))]
    dz0 = _fill_pooled_grad(dz0, dpn)
    g_w_in0, *slots1 = _weight_grad(h0, dz0, 4, 1, "grad_w_in0", _Rider("scatter", parts1))
    g_w_in0 = g_w_in0.reshape(4, 2, 512, 1280)
    g_wg = d_wg.reshape(2, 2, 4, 64, 256).transpose(2, 0, 1, 3, 4).reshape(4, 2, 128, 256)
    second = [g_w_in0, g_wg]
    theirs2 = _exchange_halves(second, "exchange_halves2")
    parts2 = [_add_sibling(where, g, t, f"add_sibling2_{k}") for k, (g, t) in enumerate(zip(second, theirs2))]
    small2 = _pack_small([zero, d_gv, d_scale, zero, zero], d_ws, d_bs[:, :4].T, None)
    grad_x, g_norm0, *slots2, small2_all = _in_proj0_bwd(
        dz0, xs, dx1, row(norm_0), w_in0, _Rider("scatter", parts2, small2))

    grads = [g_w_in0, g_w_out0, g_w_in1, g_w_out1, g_wg]
    theirs = [theirs2[0], theirs1[0], theirs1[1], theirs1[2], theirs2[1]]
    slots = [slots2[0], slots1[0], slots1[1], slots1[2], slots2[1]]
    n = len(grads)
    reduced = [_sum_chips(where, grads[w], theirs[w], slots[w], f"sum_chips{w}") for w in range(n)]
    small3 = _pack_small([g_norm0, zero, zero, zero, zero], None, None, None, loss_lanes)[:8]
    *from_sibling, small3_all = _share_halves(reduced, small3)

    out_g, out_d, out_m, out_v = {}, {}, {}, {}
    names = ["w_in_0", "w_out_0", "w_in_1", "w_out_1", "b_group_w_0"]
    for w in range(n):
        shape = big_w[w].shape
        two_d = (-1, shape[-1])
        outs = _adamw_halves(where, big_w[w].reshape(two_d), reduced[w], from_sibling[w], big_m[w].reshape(two_d),
                             big_v[w].reshape(two_d), f"adamw{w}")
        out_g[names[w]], out_d[names[w]], out_m[names[w]], out_v[names[w]] = (a.reshape(shape) for a in outs)

    g_small = _sum_small(small1_all, small2_all, small3_all)
    small_names = ["norm_0", "a_v_norm_0", "b_scale_0", "norm_1", "final_norm"]
    pack = lambda vecs, ws_, bs_, sk: _pack_small(vecs, ws_, bs_, sk)
    w_small = pack([norm_0, a_v_norm_0, b_scale_0, norm_1, final_norm], a_spatial_w_0, a_spatial_b_0, sink_1)
    m_small = pack([m_norm_0, m_a_v_norm_0, m_b_scale_0, m_norm_1, m_final_norm], m_a_spatial_w_0,
                   m_a_spatial_b_0, m_sink_1)
    v_small = pack([v_norm_0, v_a_v_norm_0, v_b_scale_0, v_norm_1, v_final_norm], v_a_spatial_w_0,
                   v_a_spatial_b_0, v_sink_1)
    d_small, nm_small, nv_small = _adamw(w_small, g_small, m_small, v_small, "adamw_small")
    for store, packed in ((out_g, g_small), (out_d, d_small), (out_m, nm_small), (out_v, nv_small)):
        vecs, ws_, bs_, sk = _unpack_small(packed)
        for name, vec in zip(small_names, vecs):
            store[name] = vec
        store["a_spatial_w_0"], store["a_spatial_b_0"], store["sink_1"] = ws_, bs_, sk

    loss = jnp.sum(g_small[7])
    order = ["norm_0", "w_in_0", "a_v_norm_0", "a_spatial_w_0", "a_spatial_b_0", "b_group_w_0", "b_scale_0",
             "w_out_0", "norm_1", "w_in_1", "sink_1", "w_out_1", "final_norm"]
    return (loss, grad_x.reshape(1, s, D_MODEL), *[out_g[k] for k in order], *[out_d[k] for k in order],
            *[out_m[k] for k in order], *[out_v[k] for k in order])
```

```python
import functools

import numpy as np
import jax
import jax.numpy as jnp
from jax import lax
from jax.experimental import pallas as pl
from jax.experimental.pallas import tpu as pltpu

F32 = jnp.float32
BF16 = jnp.bfloat16
MESH = pl.DeviceIdType.MESH

D_MODEL = 1024
EPS = 1e-6
NEG_INF = -1e30
CHUNK = 128
POOL_WINDOWS = (2, 4, 8, 16)
HALO = 16
N_HEADS = 16
HEAD_DIM = 64
ATTN_WINDOW = 128
ROPE_THETA = 500000.0
ROT_DIM = 16
ADAM_LR = 0.001
ADAM_B1 = 0.9
ADAM_B2 = 0.999
ADAM_EPS = 1e-08
ADAM_WD = 0.01
ADAM_STEP = 10

TOK = 256
VMEM_LIMIT = 56 * 1024 * 1024


def _params(**kw):
    return pltpu.CompilerParams(vmem_limit_bytes=VMEM_LIMIT, **kw)


def _whole(shape):
    nd = len(shape)
    return pl.BlockSpec(shape, lambda *_: (0,) * nd)


def _rows(t, n):
    return pl.BlockSpec((t, n), lambda i: (i, 0))


ANY = pl.BlockSpec(memory_space=pl.ANY)

_G0 = 0.7978845608028654
_G1 = 0.044715


def _gelu(x):
    return 0.5 * x * (1.0 + jnp.tanh(_G0 * (x + _G1 * x * x * x)))


def _gelu_and_grad(x):
    x2 = x * x
    t = jnp.tanh(_G0 * (x + _G1 * x2 * x))
    half = 0.5 * (1.0 + t)
    return x * half, half + 0.5 * x * (1.0 - t * t) * (_G0 * (1.0 + 3.0 * _G1 * x2))


def _sigmoid(x):
    return 1.0 / (1.0 + jnp.exp(-x))


def _dot(a, b):
    return jnp.dot(a, b, preferred_element_type=F32)


def _dot_nt(a, b):
    return lax.dot_general(a, b, (((1,), (1,)), ((), ())), preferred_element_type=F32)


def _dot_tn(a, b):
    return lax.dot_general(a, b, (((0,), (0,)), ((), ())), preferred_element_type=F32)


def _rms_fwd(x, g):
    r = lax.rsqrt(jnp.mean(x * x, axis=-1, keepdims=True) + EPS)
    xh = x * r
    return r, xh, xh * g


def _rms_bwd(dy, g, r, xh):
    dxh = dy * g
    return r * (dxh - xh * jnp.mean(dxh * xh, axis=-1, keepdims=True))


def _band_matrices(t):
    r = np.arange(t)[:, None]
    j = np.arange(t + 2 * HALO)[None, :]
    fwd, bwd = [], []
    for w in POOL_WINDOWS:
        d = j - r - HALO
        fwd.append((d >= -(w // 2)) & (d < w // 2))
        bwd.append((d >= -(w // 2) + 1) & (d <= w // 2))
    return (jnp.asarray(np.stack(fwd), BF16), jnp.asarray(np.stack(bwd), BF16))


def _window_counts(i, t, s):
    tok = i * t + lax.broadcasted_iota(jnp.int32, (t, 1), 0)
    out = []
    for w in POOL_WINDOWS:
        cnt = jnp.minimum(tok + w // 2, s) - jnp.maximum(tok - w // 2, 0)
        out.append(cnt.astype(F32))
    return out


def _rope_tables(s):
    inv = np.float32(ROPE_THETA) ** (-np.arange(0, ROT_DIM, 2, dtype=np.float32) / np.float32(ROT_DIM))
    ang = np.arange(s, dtype=np.float32)[:, None] * inv.astype(np.float32)[None, :]
    cos, sin = np.cos(ang).astype(np.float32), np.sin(ang).astype(np.float32)
    z8 = np.zeros((s, 8), np.float32)
    z48 = np.zeros((s, HEAD_DIM - ROT_DIM), np.float32)
    c = np.concatenate([cos, cos, np.ones((s, HEAD_DIM - ROT_DIM), np.float32)], axis=1)
    s_lo = np.concatenate([z8, sin, z48], axis=1)
    s_hi = np.concatenate([-sin, z8, z48], axis=1)
    return tuple(jnp.asarray(np.concatenate([a, a], axis=1)) for a in (c, s_lo, s_hi))


def _rope(x, c, s_lo, s_hi):
    n = x.shape[1]
    reps = n // 128
    c, s_lo, s_hi = (jnp.tile(a, (1, reps)) for a in (c, s_lo, s_hi))
    return x * c + pltpu.roll(x, 8, 1) * s_lo + pltpu.roll(x, n - 8, 1) * s_hi


def _rope_t(dx, c, s_lo, s_hi):
    n = dx.shape[1]
    reps = n // 128
    c, s_lo, s_hi = (jnp.tile(a, (1, reps)) for a in (c, s_lo, s_hi))
    return dx * c + pltpu.roll(dx * s_lo, n - 8, 1) + pltpu.roll(dx * s_hi, 8, 1)


def _in_proj0_own(order, x, g0, w_own, rider):
    s = x.shape[0]
    n = w_own.shape[1]

    def body(order_ref, x_ref, g_ref, w_ref, h_ref, z_ref):
        _, _, h = _rms_fwd(x_ref[...], g_ref[...])
        h = h.astype(BF16)
        h_ref[...] = h
        z_ref[...] = _dot(h, w_ref[...]).astype(BF16)

    return pl.pallas_call(
        rider.carried_by(body, 4, 2, s // TOK, late=True), name="in_proj0_own",
        grid_spec=pltpu.PrefetchScalarGridSpec(
            num_scalar_prefetch=1, grid=(s // TOK,),
            in_specs=[pl.BlockSpec((TOK, D_MODEL), lambda i, o: (i, 0)), pl.BlockSpec((1, D_MODEL), lambda i, o: (0, 0)),
                      pl.BlockSpec(w_own.shape, lambda i, o: (0, 0))] + rider.in_specs,
            out_specs=[pl.BlockSpec((TOK, D_MODEL), lambda i, o: (i, 0)),
                       pl.BlockSpec((TOK, n), lambda i, o: (i, o[0]))] + rider.out_specs,
            scratch_shapes=rider.scratch),
        out_shape=[jax.ShapeDtypeStruct((s, D_MODEL), BF16), jax.ShapeDtypeStruct((s, 4 * n), BF16)] + rider.out_shape,
        input_output_aliases={4 + j: 2 + j for j in range(len(rider.fulls))},
        compiler_params=_params(),
    )(order, x, g0, w_own, *rider.fulls)


def _in_proj0_rest(order, h0, w_in0, z0, rider):
    s = h0.shape[0]
    tok = min(s, 4 * TOK)
    n_tiles = s // tok
    n = w_in0.shape[1] // 4

    def body(order_ref, h_ref, w_ref, z_in_ref, z_ref):
        z_ref[...] = _dot(h_ref[...], w_ref[...]).astype(BF16)

    return pl.pallas_call(
        rider.carried_by(body, 4, 1, 3 * n_tiles, lambda: pl.program_id(0) * n_tiles + pl.program_id(1)),
        name="in_proj0_rest",
        grid_spec=pltpu.PrefetchScalarGridSpec(
            num_scalar_prefetch=1, grid=(3, n_tiles),
            in_specs=[pl.BlockSpec((tok, D_MODEL), lambda k, i, o: (i, 0)),
                      pl.BlockSpec((w_in0.shape[0], n), lambda k, i, o: (0, o[1 + k])), ANY] + rider.in_specs,
            out_specs=[pl.BlockSpec((tok, n), lambda k, i, o: (i, o[1 + k]))] + rider.out_specs,
            scratch_shapes=rider.scratch),
        out_shape=[jax.ShapeDtypeStruct(z0.shape, BF16)] + rider.out_shape,
        input_output_aliases={3: 0, **{4 + j: 1 + j for j in range(len(rider.fulls))}},
        compiler_params=_params(),
    )(order, h0, w_in0, z0, *rider.fulls)


def _halo_specs(s, col_block, tok=TOK):
    per = tok // HALO
    last = s // HALO - 1
    prev = pl.BlockSpec((HALO, 1024), lambda i: (jnp.maximum(i * per - 1, 0), col_block))
    nxt = pl.BlockSpec((HALO, 1024), lambda i: (jnp.minimum((i + 1) * per, last), col_block))
    return prev, nxt


def _with_halo(i, n_tiles, prev_ref, cur, next_ref):
    prev = prev_ref[...]
    nxt = next_ref[...]
    prev = jnp.where(i > 0, prev, jnp.zeros_like(prev))
    nxt = jnp.where(i < n_tiles - 1, nxt, jnp.zeros_like(nxt))
    return jnp.concatenate([prev, cur, nxt], axis=0)


def _mixer_a(v1, gv, ws_ref, bsx):
    rv, vh, v2 = _rms_fwd(v1, gv)
    v2 = v2.astype(BF16)
    rows = []
    for c in range(v1.shape[0] // CHUNK):
        cols = [_dot(ws_ref[h], v2[c * CHUNK:(c + 1) * CHUNK, h * 256:(h + 1) * 256]) for h in range(4)]
        rows.append(jnp.concatenate(cols, axis=1) + bsx)
    return rv, vh, v2, jnp.concatenate(rows, axis=0)


def _mixer_b_pooled(bx, halo, band_ref, counts):
    out = []
    for g in range(4):
        win = _dot(band_ref[g], halo[:, g * 256:(g + 1) * 256])
        out.append(win / counts[g] - bx[:, g * 256:(g + 1) * 256])
    return out


def _mix0_fwd(x, z0, w_out0, gv, ws, bsx, wg, scale, band, rider):
    s = x.shape[0]
    n_tiles = s // TOK
    k = len(rider.fulls)

    def body(z_ref, zp_ref, zn_ref, x_ref, wout_ref, gv_ref, ws_ref, bsx_ref, wg_ref, sc_ref, band_ref,
             cat_ref, x1_ref):
        i = pl.program_id(0)
        au = z_ref[:, 0:1024].astype(F32)
        av = z_ref[:, 1024:2048].astype(F32)
        ag = z_ref[:, 2048:3072].astype(F32)
        _, _, _, mixed = _mixer_a(_gelu(av), gv_ref[...], ws_ref, bsx_ref[...])
        ya = (_gelu(au) * mixed * (ag * _sigmoid(ag))).astype(BF16)
        cat_ref[:, 0:1024] = ya
        x1 = x_ref[...] + _dot(ya, wout_ref[0:1024, :])

        bx = z_ref[:, 3072:4096]
        bg = z_ref[:, 4096:5120].astype(F32)
        halo = _with_halo(i, n_tiles, zp_ref, bx, zn_ref)
        ps = _mixer_b_pooled(bx.astype(F32), halo, band_ref, _window_counts(i, TOK, s))
        pw = jnp.concatenate([_dot(ps[g].astype(BF16), wg_ref[g]) for g in range(4)], axis=1)
        yb = (pw * sc_ref[...] * (bg * _sigmoid(bg))).astype(BF16)
        cat_ref[:, 1024:2048] = yb
        x1_ref[...] = x1 + _dot(yb, wout_ref[1024:2048, :])

    prev, nxt = _halo_specs(s, 3)
    return pl.pallas_call(
        rider.carried_by(body, 11, 2, n_tiles), name="mix0_fwd", grid=(n_tiles,),
        in_specs=[_rows(TOK, 5120), prev, nxt, _rows(TOK, D_MODEL), _whole(w_out0.shape), _whole((1, 1024)),
                  _whole(ws.shape), _whole(bsx.shape), _whole(wg.shape), _whole((1, 1024)), _whole(band.shape)]
        + rider.in_specs,
        out_specs=[_rows(TOK, 2048), _rows(TOK, D_MODEL)] + rider.out_specs,
        out_shape=[jax.ShapeDtypeStruct((s, 2048), BF16), jax.ShapeDtypeStruct((s, D_MODEL), F32)] + rider.out_shape,
        input_output_aliases={11 + j: 2 + j for j in range(k)},
        scratch_shapes=rider.scratch,
        compiler_params=_params(),
    )(z0, z0, z0, x, w_out0, gv, ws, bsx, wg, scale, band, *rider.fulls)


def _in_proj1(x1, g1, w_in1, rope):
    s = x1.shape[0]

    def body(x_ref, g_ref, w_ref, c_ref, lo_ref, hi_ref, h_ref, q_ref, k_ref, v_ref, gate_ref):
        halves = [slice(r * TOK, (r + 1) * TOK) for r in range(tok // TOK)]
        hs = [_rms_fwd(x_ref[rows, :], g_ref[...])[2].astype(BF16) for rows in halves]
        for rows, h in zip(halves, hs):
            h_ref[rows, :] = h
        qs = [_dot(h, w_ref[:, 0:1024]) for h in hs]
        kvs = [_dot(h, w_ref[:, 1024:1536]) for h in hs]
        for rows, q, kv in zip(halves, qs, kvs):
            tabs = (c_ref[rows, :], lo_ref[rows, :], hi_ref[rows, :])
            q_ref[rows, :] = (_rope(q, *tabs) * Q_SCALE).astype(BF16)
            k_ref[rows, :] = _rope(kv[:, 0:256], *tabs).astype(BF16)
            v_ref[rows, :] = kv[:, 256:512].astype(BF16)
        for rows, h in zip(halves, hs):
            gate_ref[rows, :] = _dot(h, w_ref[:, 1536:2560]).astype(BF16)

    tok = min(s, 2 * TOK)
    tab = _rows(tok, 128)
    return pl.pallas_call(
        body, name="in_proj1", grid=(s // tok,),
        in_specs=[_rows(tok, D_MODEL), _whole((1, D_MODEL)), _whole(w_in1.shape), tab, tab, tab],
        out_specs=[_rows(tok, 1024), _rows(tok, 1024), _rows(tok, 256), _rows(tok, 256), _rows(tok, 1024)],
        out_shape=[jax.ShapeDtypeStruct((s, 1024), BF16), jax.ShapeDtypeStruct((s, 1024), BF16),
                   jax.ShapeDtypeStruct((s, 256), BF16), jax.ShapeDtypeStruct((s, 256), BF16),
                   jax.ShapeDtypeStruct((s, 1024), BF16)],
        compiler_params=_params(),
    )(x1, g1, w_in1, *rope)


QBLK = 128
KBLK = QBLK + 2 * ATTN_WINDOW
Q_SCALE = HEAD_DIM ** -0.5


def _block_bias(q0, s):
    r = lax.broadcasted_iota(jnp.int32, (QBLK, KBLK), 0)
    c = lax.broadcasted_iota(jnp.int32, (QBLK, KBLK), 1)
    kj = q0 - ATTN_WINDOW + c
    ok = (c >= r) & (c <= r + 2 * ATTN_WINDOW) & (kj >= 0) & (kj < s)
    return jnp.where(ok, 0.0, NEG_INF)


def _pair_operands(t):
    lane = lax.broadcasted_iota(jnp.int32, (1, 128), 1)
    first = lane < HEAD_DIM
    zero = jnp.zeros((KBLK, 128), BF16)
    out = []
    for j in range(2):
        slab = t[:, 128 * j:128 * (j + 1)]
        turned = pltpu.bitcast(pltpu.roll(pltpu.bitcast(slab, jnp.uint32), HEAD_DIM, 1), BF16)
        for own_first in (True, False):
            top = jnp.where(first, slab if own_first else turned, zero)
            bottom = jnp.where(first, zero, turned if own_first else slab)
            out.append(jnp.concatenate([top, bottom], axis=0))
    return out


def _attn_fwd(q, kpad, vpad, sink):
    s = q.shape[0]

    def body(sink_ref, q_ref, k_ref, v_ref, o_ref, lse_ref):
        i = pl.program_id(0)
        lane = lax.broadcasted_iota(jnp.int32, (1, 128), 1)
        for b in range(TOK // QBLK):
            rows = slice(b * QBLK, (b + 1) * QBLK)
            start = pl.multiple_of(i * TOK + b * QBLK, QBLK)
            k_bd = _pair_operands(k_ref[pl.ds(start, KBLK), :])
            v_bd = _pair_operands(v_ref[pl.ds(start, KBLK), :])
            bias = _block_bias(i * TOK + b * QBLK, s)
            pairs = range(N_HEADS // 2)
            sc4 = [_dot_nt(jnp.concatenate([q_ref[rows, 256 * g:256 * g + 128], q_ref[rows, 256 * g + 128:256 * (g + 1)]],
                                           axis=0), k_bd[g]) for g in range(4)]
            sc2 = [sc4[m // 2][(m % 2) * QBLK:(m % 2 + 1) * QBLK] for m in pairs]
            scs = [sc2[h // 2][:, (h % 2) * KBLK:(h % 2 + 1) * KBLK] + bias for h in range(N_HEADS)]
            ms = [jnp.maximum(jnp.max(scs[h], axis=-1, keepdims=True), sink_ref[h]) for h in range(N_HEADS)]
            es = [jnp.exp(scs[h] - ms[h]) for h in range(N_HEADS)]
            dens = [jnp.sum(es[h], axis=-1, keepdims=True) + jnp.exp(sink_ref[h] - ms[h]) for h in range(N_HEADS)]
            first = lane < HEAD_DIM
            e2 = [jnp.concatenate([es[2 * m].astype(BF16), es[2 * m + 1].astype(BF16)], axis=1) for m in pairs]
            o4 = [_dot(jnp.concatenate([e2[2 * g], e2[2 * g + 1]], axis=0), v_bd[g]) for g in range(4)]
            outs = [o4[m // 2][(m % 2) * QBLK:(m % 2 + 1) * QBLK]
                    * jnp.where(first, 1.0 / dens[2 * m], 1.0 / dens[2 * m + 1]) for m in pairs]
            o_ref[rows, :] = jnp.concatenate(outs, axis=1).astype(BF16)
            lse = jnp.zeros((QBLK, 128), F32)
            for h in range(N_HEADS):
                lse = lse + jnp.where(lane == h, ms[h] + jnp.log(dens[h]), 0.0)
            lse_ref[rows, :] = lse

    return pl.pallas_call(
        body, name="attn_fwd", grid=(s // TOK,),
        in_specs=[pl.BlockSpec(memory_space=pltpu.SMEM), _rows(TOK, 1024), _whole(kpad.shape), _whole(vpad.shape)],
        out_specs=[_rows(TOK, 1024), _rows(TOK, 128)],
        out_shape=[jax.ShapeDtypeStruct((s, 1024), BF16), jax.ShapeDtypeStruct((s, 128), F32)],
        compiler_params=_params(),
    )(sink, q, kpad, vpad)


def _tail(x1, o, gate, target, w_out1, gf):
    s = x1.shape[0]

    def body(x1_ref, o_ref, gate_ref, t_ref, w_ref, gf_ref, y1_ref, dx2_ref, do_ref, dgate_ref, loss_ref, gfn_ref,
             dx2h_ref):
        i = pl.program_id(0)

        @pl.when(i == 0)
        def _():
            loss_ref[...] = jnp.zeros_like(loss_ref)
            gfn_ref[...] = jnp.zeros_like(gfn_ref)

        halves = [slice(r * TOK, (r + 1) * TOK) for r in range(tok // TOK)]
        gf = gf_ref[...]
        gs = [gate_ref[rows, :].astype(F32) for rows in halves]
        sgs = [_sigmoid(g) for g in gs]
        sils = [g * sg for g, sg in zip(gs, sgs)]
        os_ = [o_ref[rows, :].astype(F32) for rows in halves]
        y1s = [(o * sil).astype(BF16) for o, sil in zip(os_, sils)]
        for rows, y1 in zip(halves, y1s):
            y1_ref[rows, :] = y1
        x2s = [x1_ref[rows, :] + _dot(y1, w_ref[...]) for rows, y1 in zip(halves, y1s)]
        dx2hs = []
        for rows, x2 in zip(halves, x2s):
            r, xh, out = _rms_fwd(x2, gf)
            diff = out - t_ref[rows, :]
            loss_ref[...] += jnp.sum(diff * diff, axis=0, keepdims=True) * (0.5 / D_MODEL)
            dout = diff * (1.0 / D_MODEL)
            gfn_ref[...] += jnp.sum(dout * xh, axis=0, keepdims=True)
            dx2 = _rms_bwd(dout, gf, r, xh)
            dx2_ref[rows, :] = dx2
            dx2hs.append(dx2.astype(BF16))
            dx2h_ref[rows, :] = dx2hs[-1]
        dy1s = [_dot_nt(dx2h, w_ref[...]) for dx2h in dx2hs]
        for rows, dy1, sil, o, sg, g in zip(halves, dy1s, sils, os_, sgs, gs):
            do_ref[rows, :] = (dy1 * sil).astype(BF16)
            dgate_ref[rows, :] = (dy1 * o * (sg * (1.0 + g * (1.0 - sg)))).astype(BF16)

    tok = min(s, 2 * TOK)
    row = _rows(tok, 1024)
    acc = _whole((1, 1024))
    return pl.pallas_call(
        body, name="tail", grid=(s // tok,),
        in_specs=[row, row, row, row, _whole(w_out1.shape), acc],
        out_specs=[row, row, row, row, acc, acc, row],
        out_shape=[jax.ShapeDtypeStruct((s, 1024), BF16), jax.ShapeDtypeStruct((s, 1024), F32),
                   jax.ShapeDtypeStruct((s, 1024), BF16), jax.ShapeDtypeStruct((s, 1024), BF16),
                   jax.ShapeDtypeStruct((1, 1024), F32), jax.ShapeDtypeStruct((1, 1024), F32),
                   jax.ShapeDtypeStruct((s, 1024), BF16)],
        compiler_params=_params(),
    )(x1, o, gate, target, w_out1, gf)


def _attn_bwd(q, kpad, vpad, sink, o, lse, do, rope):
    s = q.shape[0]
    pad_t = (kpad.shape[1], kpad.shape[0])

    def body(sink_ref, q_ref, k_ref, v_ref, o_ref, lse_ref, do_ref, c_ref, lo_ref, hi_ref,
             dq_ref, dk_ref, dv_ref, ds_ref):
        i = pl.program_id(0)

        @pl.when(i == 0)
        def _():
            dk_ref[...] = jnp.zeros_like(dk_ref)
            dv_ref[...] = jnp.zeros_like(dv_ref)
            ds_ref[...] = jnp.zeros_like(ds_ref)

        lane = lax.broadcasted_iota(jnp.int32, (1, 128), 1)
        dsink = jnp.zeros((1, 128), F32)
        for b in range(TOK // QBLK):
            rows = slice(b * QBLK, (b + 1) * QBLK)
            start = pl.multiple_of(i * TOK + b * QBLK, QBLK)
            k_bd = _pair_operands(k_ref[pl.ds(start, KBLK), :])
            v_bd = _pair_operands(v_ref[pl.ds(start, KBLK), :])
            bias = _block_bias(i * TOK + b * QBLK, s)
            lanes_of = (lane < HEAD_DIM, lane >= HEAD_DIM)
            half = lambda t, j: t[:, j * KBLK:(j + 1) * KBLK]
            dqs, dks, dvs = [], [], []
            for g in range(4):
                pairs = (2 * g, 2 * g + 1)
                qs = {m: q_ref[rows, 128 * m:128 * (m + 1)] for m in pairs}
                dos = {m: do_ref[rows, 128 * m:128 * (m + 1)] for m in pairs}
                lses = {h: lse_ref[rows, h:h + 1] for h in range(4 * g, 4 * g + 4)}
                stacked = lambda parts: jnp.concatenate([parts[m] for m in pairs], axis=0)
                unstack = lambda t: {m: t[j * QBLK:(j + 1) * QBLK] for j, m in enumerate(pairs)}
                sc2 = unstack(_dot_nt(stacked(qs), k_bd[g]))
                ps = {2 * m + j: jnp.exp(half(sc2[m], j) + bias - lses[2 * m + j]) for m in pairs for j in range(2)}
                prods = {m: dos[m].astype(F32) * o_ref[rows, 128 * m:128 * (m + 1)].astype(F32) for m in pairs}
                deltas = {2 * m + j: jnp.sum(jnp.where(lanes_of[j], prods[m], 0.0), axis=-1, keepdims=True)
                          for m in pairs for j in range(2)}
                for h in range(4 * g, 4 * g + 4):
                    dsink = dsink + jnp.where(
                        lane == h, -jnp.sum(jnp.exp(sink_ref[h] - lses[h]) * deltas[h], axis=0, keepdims=True), 0.0)
                dp2 = unstack(_dot_nt(stacked(dos), v_bd[g]))
                ds2 = {m: jnp.concatenate(
                    [(ps[2 * m + j] * (half(dp2[m], j) - deltas[2 * m + j])).astype(BF16) for j in range(2)], axis=1)
                    for m in pairs}
                p2 = {m: jnp.concatenate([ps[2 * m].astype(BF16), ps[2 * m + 1].astype(BF16)], axis=1) for m in pairs}
                diag = lambda t: t[0:HEAD_DIM, 0:KBLK] + t[HEAD_DIM:128, KBLK:2 * KBLK]
                dvs.append(diag(_dot_tn(stacked(dos), stacked(p2))))
                dks.append(diag(_dot_tn(stacked(qs), stacked(ds2))))
                dq2 = unstack(_dot(stacked(ds2), k_bd[g]) * Q_SCALE)
                dqs += [dq2[m] for m in pairs]
            dq = jnp.concatenate(dqs, axis=1)
            dq_ref[rows, :] = _rope_t(dq, c_ref[rows, :], lo_ref[rows, :], hi_ref[rows, :]).astype(BF16)
            dk_ref[:, pl.ds(start, KBLK)] += jnp.concatenate(dks, axis=0)
            dv_ref[:, pl.ds(start, KBLK)] += jnp.concatenate(dvs, axis=0)
        ds_ref[...] += dsink

    row = _rows(TOK, 1024)
    tab = _rows(TOK, 128)
    pad = _whole(kpad.shape)
    return pl.pallas_call(
        body, name="attn_bwd", grid=(s // TOK,),
        in_specs=[pl.BlockSpec(memory_space=pltpu.SMEM), row, pad, pad, row, tab, row, tab, tab, tab],
        out_specs=[row, _whole(pad_t), _whole(pad_t), _whole((1, 128))],
        out_shape=[jax.ShapeDtypeStruct((s, 1024), BF16), jax.ShapeDtypeStruct(pad_t, F32),
                   jax.ShapeDtypeStruct(pad_t, F32), jax.ShapeDtypeStruct((1, 128), F32)],
        compiler_params=_params(),
    )(sink, q, kpad, vpad, o, lse, do, *rope)


def _in_proj1_bwd(dq, dk_t, dv_t, dgate, x1, dx2, g1, w_in1, rope):
    s = x1.shape[0]
    tok = min(s, 2 * TOK)
    n_sub = tok // ATTN_WINDOW

    def body(*refs):
        dq_ref = refs[0]
        dk_refs, dv_refs = refs[1:1 + n_sub], refs[1 + n_sub:1 + 2 * n_sub]
        (dgate_ref, x1_ref, dx2_ref, g_ref, w_ref, c_ref, lo_ref, hi_ref,
         dz_ref, dx1_ref, gn_ref, dx1h_ref) = refs[1 + 2 * n_sub:]

        @pl.when(pl.program_id(0) == 0)
        def _():
            gn_ref[...] = jnp.zeros_like(gn_ref)

        halves = [slice(r * TOK, (r + 1) * TOK) for r in range(tok // TOK)]
        per = TOK // ATTN_WINDOW
        g = g_ref[...]
        for r, rows in enumerate(halves):
            dk = jnp.concatenate([ref[...] for ref in dk_refs[r * per:(r + 1) * per]], axis=1).T
            dv = jnp.concatenate([ref[...] for ref in dv_refs[r * per:(r + 1) * per]], axis=1).T
            dz_ref[rows, 0:1024] = dq_ref[rows, :]
            dz_ref[rows, 1024:1280] = _rope_t(dk, c_ref[rows, :], lo_ref[rows, :], hi_ref[rows, :]).astype(BF16)
            dz_ref[rows, 1280:1536] = dv.astype(BF16)
            dz_ref[rows, 1536:2560] = dgate_ref[rows, :]
        dhs = [_dot_nt(dz_ref[rows, :], w_ref[...]) for rows in halves]
        for rows, dh in zip(halves, dhs):
            r, xh, _ = _rms_fwd(x1_ref[rows, :], g)
            gn_ref[...] += jnp.sum(dh * xh, axis=0, keepdims=True)
            dx1 = dx2_ref[rows, :] + _rms_bwd(dh, g, r, xh)
            dx1_ref[rows, :] = dx1
            dx1h_ref[rows, :] = dx1.astype(BF16)

    row = _rows(tok, 1024)
    subs = [pl.BlockSpec((256, ATTN_WINDOW), lambda i, j=j: (0, n_sub * i + 1 + j)) for j in range(n_sub)]
    tab = _rows(tok, 128)
    acc = _whole((1, 1024))
    return pl.pallas_call(
        body, name="in_proj1_bwd", grid=(s // tok,),
        in_specs=[row] + subs + subs + [row, row, row, acc, _whole(w_in1.shape), tab, tab, tab],
        out_specs=[_rows(tok, 2560), row, acc, row],
        out_shape=[jax.ShapeDtypeStruct((s, 2560), BF16), jax.ShapeDtypeStruct((s, 1024), F32),
                   jax.ShapeDtypeStruct((1, 1024), F32), jax.ShapeDtypeStruct((s, 1024), BF16)],
        compiler_params=_params(),
    )(dq, *[dk_t] * n_sub, *[dv_t] * n_sub, dgate, x1, dx2, g1, w_in1, *rope)


def _mix0_bwd(dx1, z0, w_out0, gv, ws, bsx, wg, scale, band, rider):
    s = dx1.shape[0]
    n_tiles = s // TOK

    def body(dx1h_ref, z_ref, zp_ref, zn_ref, wout_ref, gv_ref, ws_ref, bsx_ref, wg_ref, sc_ref, band_ref,
             dz_ref, dpn_ref, dws_ref, dbs_ref, dgv_ref, dsc_ref, dwg_ref):
        i = pl.program_id(0)
        dz_ref[:, 3072:4096] = jnp.zeros((TOK, 1024), BF16)

        @pl.when(i == 0)
        def _():
            for ref in (dws_ref, dbs_ref, dgv_ref, dsc_ref, dwg_ref):
                ref[...] = jnp.zeros_like(ref)

        dcat = _dot_nt(dx1h_ref[...], wout_ref[...])
        dya = dcat[:, 0:1024]
        dyb = dcat[:, 1024:2048]

        au = z_ref[:, 0:1024].astype(F32)
        av = z_ref[:, 1024:2048].astype(F32)
        ag = z_ref[:, 2048:3072].astype(F32)
        gv = gv_ref[...]
        u, du = _gelu_and_grad(au)
        v1, dv1 = _gelu_and_grad(av)
        rv, vh, v2, mixed = _mixer_a(v1, gv, ws_ref, bsx_ref[...])
        sg = _sigmoid(ag)
        sil = ag * sg
        dz_ref[:, 2048:3072] = (dya * u * mixed * (sg * (1.0 + ag * (1.0 - sg)))).astype(BF16)
        dz_ref[:, 0:1024] = (dya * mixed * sil * du).astype(BF16)
        dmixed = dya * u * sil
        lane = lax.broadcasted_iota(jnp.int32, (1, 128), 1)
        dm16 = dmixed.astype(BF16)
        dv2_rows = []
        for c in range(TOK // CHUNK):
            rows = slice(c * CHUNK, (c + 1) * CHUNK)
            cols_out = []
            for h in range(4):
                cols = slice(h * 256, (h + 1) * 256)
                dws_ref[h] += _dot_nt(dm16[rows, cols], v2[rows, cols])
                dbs_ref[...] += jnp.where(lane == h, jnp.sum(dmixed[rows, cols], axis=-1, keepdims=True), 0.0)
                cols_out.append(_dot_tn(ws_ref[h], dm16[rows, cols]))
            dv2_rows.append(jnp.concatenate(cols_out, axis=1))
        dv2 = jnp.concatenate(dv2_rows, axis=0)
        dgv_ref[...] += jnp.sum(dv2 * vh, axis=0, keepdims=True)
        dz_ref[:, 1024:2048] = (_rms_bwd(dv2, gv, rv, vh) * dv1).astype(BF16)

        bx = z_ref[:, 3072:4096]
        bg = z_ref[:, 4096:5120].astype(F32)
        counts = _window_counts(i, TOK, s)
        halo = _with_halo(i, n_tiles, zp_ref, bx, zn_ref)
        ps = [p.astype(BF16) for p in _mixer_b_pooled(bx.astype(F32), halo, band_ref, counts)]
        pw = jnp.concatenate([_dot(ps[g], wg_ref[g]) for g in range(4)], axis=1)
        sgb = _sigmoid(bg)
        sc = sc_ref[...]
        dz_ref[:, 4096:5120] = (dyb * pw * sc * (sgb * (1.0 + bg * (1.0 - sgb)))).astype(BF16)
        dys = dyb * (bg * sgb)
        dsc_ref[...] += jnp.sum(dys * pw, axis=0, keepdims=True)
        dpw = (dys * sc).astype(BF16)
        for g in range(4):
            cols = slice(g * 256, (g + 1) * 256)
            dwg_ref[g] += _dot_tn(ps[g], dpw[:, cols])
            dpn_ref[:, cols] = (_dot_nt(dpw[:, cols], wg_ref[g]) / counts[g]).astype(BF16)

    prev, nxt = _halo_specs(s, 3)
    row = _rows(TOK, 1024)
    vec = _whole((1, 1024))
    return pl.pallas_call(
        rider.carried_by(body, 11, 7, n_tiles), name="mix0_bwd", grid=(n_tiles,),
        in_specs=[row, _rows(TOK, 5120), prev, nxt, _whole(w_out0.shape), vec, _whole(ws.shape),
                  _whole(bsx.shape), _whole(wg.shape), vec, _whole(band.shape)] + rider.in_specs,
        out_specs=[_rows(TOK, 5120), row, _whole((4, 128, 128)), _whole((128, 128)), vec, vec,
                   _whole((4, 256, 256))] + rider.out_specs,
        out_shape=[jax.ShapeDtypeStruct((s, 5120), BF16), jax.ShapeDtypeStruct((s, 1024), BF16),
                   jax.ShapeDtypeStruct((4, 128, 128), F32), jax.ShapeDtypeStruct((128, 128), F32),
                   jax.ShapeDtypeStruct((1, 1024), F32), jax.ShapeDtypeStruct((1, 1024), F32),
                   jax.ShapeDtypeStruct((4, 256, 256), F32)] + rider.out_shape,
        scratch_shapes=rider.scratch,
        compiler_params=_params(),
    )(dx1, z0, z0, z0, w_out0, gv, ws, bsx, wg, scale, band, *rider.parts)


def _fill_pooled_grad(dz0, dpn):
    s = dpn.shape[0]
    tok = min(s, 2 * TOK)
    n_tiles = s // tok
    band_t = _band_matrices(tok)[1]

    def body(dz_in_ref, dpn_ref, dpp_ref, dpx_ref, band_ref, dbx_ref):
        i = pl.program_id(0)
        dpn = dpn_ref[...]
        halo = _with_halo(i, n_tiles, dpp_ref, dpn, dpx_ref)
        counts = _window_counts(i, tok, s)
        for g in range(4):
            cols = slice(g * 256, (g + 1) * 256)
            dbx = _dot(band_ref[g], halo[:, cols]) - dpn[:, cols].astype(F32) * counts[g]
            dbx_ref[:, cols] = dbx.astype(BF16)

    prev, nxt = _halo_specs(s, 0, tok)
    return pl.pallas_call(
        body, name="fill_pooled_grad", grid=(n_tiles,),
        in_specs=[ANY, _rows(tok, 1024), prev, nxt, _whole(band_t.shape)],
        out_specs=pl.BlockSpec((tok, 1024), lambda i: (i, 3)),
        out_shape=jax.ShapeDtypeStruct(dz0.shape, BF16),
        input_output_aliases={0: 0},
        compiler_params=_params(),
    )(dz0, dpn, dpn, dpn, band_t)


def _in_proj0_bwd(dz0, x, dx1, g0, w_in0, rider):
    s = x.shape[0]
    tok = min(s, 2 * TOK)
    n_tiles = s // tok

    def body(dz_ref, x_ref, dx1_ref, g_ref, w_ref, dx_ref, gn_ref):
        i = pl.program_id(0)

        @pl.when(i == 0)
        def _():
            gn_ref[...] = jnp.zeros_like(gn_ref)

        halves = [slice(r * TOK, (r + 1) * TOK) for r in range(tok // TOK)]
        g0v = g_ref[...]
        dhs = [_dot_nt(dz_ref[rows, :], w_ref[...]) for rows in halves]
        for rows, dh in zip(halves, dhs):
            r, xh, _ = _rms_fwd(x_ref[rows, :], g0v)
            gn_ref[...] += jnp.sum(dh * xh, axis=0, keepdims=True)
            dx_ref[rows, :] = dx1_ref[rows, :] + _rms_bwd(dh, g0v, r, xh)

    row = _rows(tok, 1024)
    vec = _whole((1, 1024))
    return pl.pallas_call(
        rider.carried_by(body, 5, 2, n_tiles), name="in_proj0_bwd", grid=(n_tiles,),
        in_specs=[_rows(tok, 5120), row, row, vec, _whole(w_in0.shape)] + rider.in_specs,
        out_specs=[row, vec] + rider.out_specs,
        out_shape=[jax.ShapeDtypeStruct((s, 1024), F32), jax.ShapeDtypeStruct((1, 1024), F32)] + rider.out_shape,
        scratch_shapes=rider.scratch,
        compiler_params=_params(),
    )(dz0, x, dx1, g0, w_in0, *rider.parts)


def _weight_grad(a, b, n_blocks, split, name, rider=None):
    s, k = a.shape
    n = b.shape[1]
    tn = n // n_blocks
    w = tn // split
    ts = min(s, 2048)

    def body(a_ref, b_ref, o_ref):
        @pl.when(pl.program_id(1) == 0)
        def _():
            o_ref[...] = jnp.zeros_like(o_ref)

        res = _dot_tn(a_ref[...], b_ref[...])
        for q in range(split):
            o_ref[q] += res[:, q * w:(q + 1) * w]

    in_specs = [pl.BlockSpec((ts, k), lambda j, t: (t, 0)), pl.BlockSpec((ts, tn), lambda j, t: (t, j))]
    out_spec = pl.BlockSpec((split, k, w), lambda j, t: (j, 0, 0))
    out_shape = jax.ShapeDtypeStruct((n_blocks * split, k, w), F32)
    if rider is None:
        return pl.pallas_call(body, name=name, grid=(n_blocks, s // ts), in_specs=in_specs, out_specs=out_spec,
                              out_shape=out_shape, compiler_params=_params())(a, b)
    steps = s // ts
    return pl.pallas_call(
        rider.carried_by(body, 2, 1, n_blocks * steps, lambda: pl.program_id(0) * steps + pl.program_id(1)),
        name=name, grid=(n_blocks, steps), in_specs=in_specs + rider.in_specs,
        out_specs=[out_spec] + rider.out_specs, out_shape=[out_shape] + rider.out_shape,
        scratch_shapes=rider.scratch, compiler_params=_params(),
    )(a, b, *rider.parts)


def _row_tile(rows, cols):
    t = rows
    while t * cols * 4 > (1 << 21) and t % 32 == 0:
        t //= 2
    return t


def _add_sibling(where, g, theirs, name):
    _, _, rows, cols = g.shape
    t = _row_tile(rows, cols)

    def body(where_ref, g_ref, t_ref, o_ref):
        o_ref[...] = (g_ref[...] + t_ref[...]).astype(BF16)

    spec = pl.BlockSpec((None, t, cols), lambda s, i, p: (s, i, 0))
    return pl.pallas_call(
        body, name=name, out_shape=jax.ShapeDtypeStruct((4, rows, cols), BF16),
        grid_spec=pltpu.PrefetchScalarGridSpec(
            num_scalar_prefetch=1, grid=(4, rows // t),
            in_specs=[pl.BlockSpec((None, None, t, cols), lambda s, i, p: (s, p[1], i, 0)), spec], out_specs=spec),
        compiler_params=_params())(where, g, theirs)


def _sum_chips(where, g, theirs, slots, name):
    _, _, rows, cols = g.shape
    t = _row_tile(rows, cols)

    def body(where_ref, g_ref, t_ref, s_ref, o_ref):
        me = where_ref[0]
        own = g_ref[...] + t_ref[...]
        acc = jnp.where(me == 0, own, s_ref[0].astype(F32))
        for k in range(1, 4):
            acc = acc + jnp.where(me == k, own, s_ref[k].astype(F32))
        o_ref[...] = acc

    return pl.pallas_call(
        body, name=name, out_shape=jax.ShapeDtypeStruct((rows, cols), F32),
        grid_spec=pltpu.PrefetchScalarGridSpec(
            num_scalar_prefetch=1, grid=(rows // t,),
            in_specs=[pl.BlockSpec((None, None, t, cols), lambda i, p: (p[0], p[1], i, 0)),
                      pl.BlockSpec((None, t, cols), lambda i, p: (p[0], i, 0)),
                      pl.BlockSpec((4, t, cols), lambda i, p: (0, i, 0))],
            out_specs=pl.BlockSpec((t, cols), lambda i, p: (i, 0))),
        compiler_params=_params())(where, g, theirs, slots)


def _adamw_halves(where, w, own, theirs, m, v, name):
    rows, cols = own.shape
    t = _row_tile(rows, cols)
    per = rows // t

    def body(where_ref, w_ref, own_ref, th_ref, m_ref, v_ref, g_ref, d_ref, nm_ref, nv_ref):
        g = jnp.where(pl.program_id(0) == where_ref[1], own_ref[...], th_ref[...])
        g_ref[...] = g
        m2 = ADAM_B1 * m_ref[...] + (1.0 - ADAM_B1) * g
        v2 = ADAM_B2 * v_ref[...] + (1.0 - ADAM_B2) * (g * g)
        m_hat = m2 / (1.0 - ADAM_B1 ** ADAM_STEP)
        v_hat = v2 / (1.0 - ADAM_B2 ** ADAM_STEP)
        d_ref[...] = -ADAM_LR * (m_hat / (jnp.sqrt(v_hat) + ADAM_EPS) + ADAM_WD * w_ref[...])
        nm_ref[...] = m2
        nv_ref[...] = v2

    full = pl.BlockSpec((t, cols), lambda h, i, p: (h * per + i, 0))
    half = pl.BlockSpec((t, cols), lambda h, i, p: (i, 0))
    shp = jax.ShapeDtypeStruct(w.shape, F32)
    return pl.pallas_call(
        body, name=name, out_shape=[shp] * 4,
        grid_spec=pltpu.PrefetchScalarGridSpec(
            num_scalar_prefetch=1, grid=(2, per), in_specs=[full, half, half, full, full], out_specs=[full] * 4),
        compiler_params=_params())(where, w, own, theirs, m, v)


def _place_shard(where, w, cut, name):
    if cut.kind == "cols":
        r, n = cut.full_shape
        blk, grid = (256, n // 4), (r // 256,)
        src_map, dst_map = (lambda i, p: (i, 0)), (lambda i, p: (i, p[0]))
    elif cut.kind == "rows":
        r, n = cut.full_shape
        per = r // 4 // 256
        blk, grid = (256, n), (per,)
        src_map, dst_map = (lambda i, p: (i, 0)), (lambda i, p: (p[0] * per + i, 0))
    else:
        g, r, n = cut.full_shape
        blk, grid = (g, r // 4, n), (1,)
        src_map, dst_map = (lambda i, p: (0, 0, 0)), (lambda i, p: (0, p[0], 0))

    def body(where_ref, w_ref, o_ref):
        o_ref[...] = w_ref[...].astype(BF16)

    return pl.pallas_call(
        body, name=name, out_shape=jax.ShapeDtypeStruct(cut.full_shape, BF16),
        grid_spec=pltpu.PrefetchScalarGridSpec(
            num_scalar_prefetch=1, grid=grid, in_specs=[pl.BlockSpec(blk, src_map)],
            out_specs=pl.BlockSpec(blk, dst_map)),
        compiler_params=_params())(where, w)


def _sum_small(first, second, third):
    rows = second.shape[1]

    def body(a_ref, b_ref, c_ref, o_ref):
        top = a_ref[0] + c_ref[0] + b_ref[0, 0:8]
        rest = b_ref[0, 8:rows]
        for k in range(1, 8):
            top = top + (a_ref[k] + c_ref[k] + b_ref[k, 0:8])
            rest = rest + b_ref[k, 8:rows]
        o_ref[0:8] = top
        o_ref[8:rows] = rest

    return pl.pallas_call(
        body, name="sum_small", in_specs=[_whole(first.shape), _whole(second.shape), _whole(third.shape)],
        out_specs=_whole(second.shape[1:]), out_shape=jax.ShapeDtypeStruct(second.shape[1:], F32),
        compiler_params=_params())(first, second, third)


def _adamw(w, g, m, v, name):
    rows, cols = w.shape
    t = _row_tile(rows, cols)

    def body(w_ref, g_ref, m_ref, v_ref, d_ref, nm_ref, nv_ref):
        g = g_ref[...]
        m2 = ADAM_B1 * m_ref[...] + (1.0 - ADAM_B1) * g
        v2 = ADAM_B2 * v_ref[...] + (1.0 - ADAM_B2) * (g * g)
        m_hat = m2 / (1.0 - ADAM_B1 ** ADAM_STEP)
        v_hat = v2 / (1.0 - ADAM_B2 ** ADAM_STEP)
        d_ref[...] = -ADAM_LR * (m_hat / (jnp.sqrt(v_hat) + ADAM_EPS) + ADAM_WD * w_ref[...])
        nm_ref[...] = m2
        nv_ref[...] = v2

    spec = pl.BlockSpec((t, cols), lambda i: (i, 0))
    shp = jax.ShapeDtypeStruct(w.shape, F32)
    return pl.pallas_call(body, name=name, grid=(rows // t,), in_specs=[spec] * 4, out_specs=[spec] * 3,
                          out_shape=[shp] * 3, compiler_params=_params())(w, g, m, v)


def _place():
    x, y, c = lax.axis_index("x"), lax.axis_index("y"), lax.axis_index("c")
    chips = [(1 - x, y), (x, 1 - y), (1 - x, 1 - y)]
    return x, y, c, chips


class _Sharded:
    def __init__(self, kind, full_shape):
        self.kind = kind
        self.full_shape = full_shape

    def in_full(self, ref, s, h):
        if self.kind == "cols":
            r, n = self.full_shape
            return ref.at[pl.ds(h * (r // 2), r // 2), pl.ds(pl.multiple_of(s * (n // 4), 128), n // 4)]
        if self.kind == "rows":
            r, _ = self.full_shape
            return ref.at[pl.ds(pl.multiple_of(s * (r // 4) + h * (r // 8), 8), r // 8), :]
        g, r, _ = self.full_shape
        return ref.at[pl.ds(h * (g // 2), g // 2), pl.ds(pl.multiple_of(s * (r // 4), 16), r // 4), :]


def _remote(src, dst, send_sem, recv_sem, to):
    return pltpu.make_async_remote_copy(src_ref=src, dst_ref=dst, send_sem=send_sem, recv_sem=recv_sem,
                                        device_id=to, device_id_type=MESH)


def _start_remote(src, dst, send_sem, recv_sem, to):
    cp = _remote(src, dst, send_sem, recv_sem, to)
    cp.start()
    return cp


class _Gather:
    def __init__(self, fulls, cuts):
        n = len(fulls)
        self.fulls, self.cuts = list(fulls), list(cuts)
        self.in_specs = [ANY] * n
        self.out_specs = [ANY] * n
        self.out_shape = [jax.ShapeDtypeStruct(cut.full_shape, BF16) for cut in cuts]
        self.scratch = [pltpu.SemaphoreType.DMA((6 * n,)), pltpu.SemaphoreType.DMA((6 * n,))]

    def _step(self, step, src, out, send_sems, recv_sems):
        n, cuts = len(self.fulls), self.cuts
        x, y, c, chips = _place()
        me = 2 * x + y

        def ends(w, s, h, from_src):
            dst = cuts[w].in_full(out[w], s, h)
            return (cuts[w].in_full(src[w], s, h) if from_src else dst), dst

        for w in range(n):
            for j, chip in enumerate(chips):
                s = 2 * chip[0] + chip[1]
                k, k2 = 3 * w + j, 3 * n + 3 * w + j
                if step == "send":
                    _start_remote(*ends(w, me, c, True), send_sems.at[k], recv_sems.at[k], (*chip, c))
                elif step == "pass_on":
                    _remote(*ends(w, s, c, False), send_sems.at[k], recv_sems.at[k], (x, y, c)).wait_recv()
                    _start_remote(*ends(w, s, c, False), send_sems.at[k2], recv_sems.at[k2], (x, y, 1 - c))
                else:
                    _remote(*ends(w, s, 1 - c, False), send_sems.at[k2], recv_sems.at[k2], (x, y, c)).wait_recv()
                    _remote(*ends(w, me, c, True), send_sems.at[k], recv_sems.at[k], (x, y, c)).wait_send()
                    _remote(*ends(w, s, c, False), send_sems.at[k2], recv_sems.at[k2], (x, y, c)).wait_send()

    def carried_by(self, body, n_in, n_out, n_steps, step_index=lambda: pl.program_id(0), late=False):
        k = len(self.fulls)

        def carrier(*refs):
            ins, src = refs[:n_in], refs[n_in:n_in + k]
            outs, out = refs[n_in + k:n_in + k + n_out], refs[n_in + k + n_out:n_in + 2 * k + n_out]
            sems = refs[n_in + 2 * k + n_out:]

            def at_step(step, at):
                @pl.when(step_index() == at)
                def _():
                    self._step(step, src, out, *sems)

            at_step("send", 0)
            if not late:
                at_step("pass_on", 3 * n_steps // 4)
            body(*ins, *outs)
            if late:
                at_step("pass_on", n_steps - 1)
            at_step("finish", n_steps - 1)

        return carrier


def _exchange_halves(grads, name):
    n = len(grads)

    def body(*refs):
        g = refs[:n]
        theirs = refs[n:2 * n]
        send_sems, recv_sems = refs[2 * n:]
        x, y, c, _ = _place()
        sends = [_start_remote(g[w].at[:, 1 - c], theirs[w], send_sems.at[w], recv_sems.at[w], (x, y, 1 - c))
                 for w in range(n)]
        for w in range(n):
            _remote(g[w].at[:, 1 - c], theirs[w], send_sems.at[w], recv_sems.at[w], (x, y, c)).wait_recv()
        for cp in sends:
            cp.wait_send()

    return pl.pallas_call(
        body, name=name,
        in_specs=[ANY] * n, out_specs=[ANY] * n,
        out_shape=[jax.ShapeDtypeStruct((4,) + g.shape[2:], F32) for g in grads],
        scratch_shapes=[pltpu.SemaphoreType.DMA((n,)), pltpu.SemaphoreType.DMA((n,))],
        compiler_params=pltpu.CompilerParams(has_side_effects=True),
    )(*grads)


def _gather_small(small_ref, gathered, send_sems, recv_sems, first_sem, local_sem, start):
    x, y, c, _ = _place()
    me = 4 * x + 2 * y + c
    flips = [(fx, fy, fc) for fx in range(2) for fy in range(2) for fc in range(2)][1:]
    own = pltpu.make_async_copy(small_ref, gathered.at[me], local_sem)
    if start:
        own.start()
    else:
        own.wait()
    for k, (fx, fy, fc) in enumerate(flips):
        peer = (x + fx - 2 * x * fx, y + fy - 2 * y * fy, c + fc - 2 * c * fc)
        sems = (send_sems.at[first_sem + k], recv_sems.at[first_sem + k])
        if start:
            _start_remote(small_ref, gathered.at[me], *sems, peer)
        else:
            cp = _remote(small_ref, gathered.at[4 * peer[0] + 2 * peer[1] + peer[2]], *sems, (x, y, c))
            cp.wait_recv()
            cp.wait_send()


class _Rider:
    def __init__(self, kind, parts, small=None):
        n = len(parts)
        self.kind, self.n = kind, n
        self.per = 3 if kind == "scatter" else 1
        self.parts = list(parts) + ([] if small is None else [small])
        k = len(self.parts)
        self.in_specs = [ANY] * k
        self.out_specs = [ANY] * k
        dtype = lambda a: a.dtype if kind == "scatter" else F32
        self.out_shape = [jax.ShapeDtypeStruct((4,) + a.shape[-2:], dtype(a)) for a in parts]
        if small is not None:
            self.out_shape.append(jax.ShapeDtypeStruct((8,) + small.shape, small.dtype))
        self.scratch = [pltpu.SemaphoreType.DMA((self.per * n + 7,)), pltpu.SemaphoreType.DMA((self.per * n + 7,)),
                        pltpu.SemaphoreType.DMA]

    def _copies(self, p, out, send_sems, recv_sems, local_sem, start):
        x, y, c, chips = _place()
        me = 2 * x + y
        n = self.n
        if len(self.parts) > n:
            _gather_small(p[n], out[n], send_sems, recv_sems, self.per * n, local_sem, start)
        for w in range(n):
            if self.kind == "exchange":
                ends = [(p[w].at[:, 1 - c], out[w], out[w], (x, y, 1 - c))]
            else:
                ends = [(p[w].at[2 * cx + cy], out[w].at[me], out[w].at[2 * cx + cy], (cx, cy, c)) for cx, cy in chips]
            for j, (src, dst_there, dst_here, to) in enumerate(ends):
                sems = (send_sems.at[self.per * w + j], recv_sems.at[self.per * w + j])
                if start:
                    _start_remote(src, dst_there, *sems, to)
                else:
                    cp = _remote(src, dst_here, *sems, (x, y, c))
                    cp.wait_recv()
                    cp.wait_send()

    def carried_by(self, body, n_in, n_out, n_steps, step_index=lambda: pl.program_id(0)):
        k = len(self.parts)

        def carrier(*refs):
            ins, mine = refs[:n_in], refs[n_in:n_in + k]
            outs, theirs = refs[n_in + k:n_in + k + n_out], refs[n_in + k + n_out:n_in + 2 * k + n_out]
            sems = refs[n_in + 2 * k + n_out:]

            @pl.when(step_index() == 0)
            def _():
                self._copies(mine, theirs, *sems, start=True)

            body(*ins, *outs)

            @pl.when(step_index() == n_steps - 1)
            def _():
                self._copies(mine, theirs, *sems, start=False)

        return carrier


def _share_halves(halves, small):
    n = len(halves)

    def body(*refs):
        hv = refs[:n]
        small_ref = refs[n]
        out = refs[n + 1:2 * n + 1]
        gathered = refs[2 * n + 1]
        send_sems, recv_sems, local_sem = refs[2 * n + 2:]
        x, y, c, _ = _place()
        sends = [_start_remote(hv[w], out[w], send_sems.at[w], recv_sems.at[w], (x, y, 1 - c)) for w in range(n)]
        _gather_small(small_ref, gathered, send_sems, recv_sems, n, local_sem, True)
        for w in range(n):
            _remote(hv[w], out[w], send_sems.at[w], recv_sems.at[w], (x, y, c)).wait_recv()
        for cp in sends:
            cp.wait_send()
        _gather_small(small_ref, gathered, send_sems, recv_sems, n, local_sem, False)

    return pl.pallas_call(
        body, name="share_halves",
        in_specs=[ANY] * (n + 1), out_specs=[ANY] * (n + 1),
        out_shape=[jax.ShapeDtypeStruct(a.shape, F32) for a in halves] + [jax.ShapeDtypeStruct((8,) + small.shape, F32)],
        scratch_shapes=[pltpu.SemaphoreType.DMA((n + 7,)), pltpu.SemaphoreType.DMA((n + 7,)),
                        pltpu.SemaphoreType.DMA],
        compiler_params=pltpu.CompilerParams(has_side_effects=True),
    )(*halves, small)


SMALL_ROWS = 80


def _pack_small(vecs, ws, bs, sink, extra=None):
    ws = jnp.zeros((64, 1024), F32) if ws is None else ws.reshape(64, 1024)
    bs = jnp.zeros((1, 512), F32) if bs is None else bs.reshape(1, 512)
    sink = jnp.zeros((1, 16), F32) if sink is None else sink.reshape(1, 16)
    extra = jnp.zeros((1, 1024), F32) if extra is None else extra.reshape(1, 1024)
    top = jnp.concatenate(
        [v.reshape(1, 1024) for v in vecs]
        + [jnp.pad(bs, ((0, 0), (0, 512))), jnp.pad(sink, ((0, 0), (0, 1008))), extra], axis=0)
    return jnp.concatenate([top, ws, jnp.zeros((8, 1024), F32)], axis=0)


def _unpack_small(p):
    vecs = [p[k] for k in range(5)]
    return vecs, p[8:72].reshape(4, 128, 128), p[5, :512].reshape(4, 128), p[6, :16]


def kernel(x, norm_0, w_in_0, a_v_norm_0, a_spatial_w_0, a_spatial_b_0, b_group_w_0, b_scale_0, w_out_0, norm_1, w_in_1, sink_1, w_out_1, final_norm, loss_target, m_norm_0, m_w_in_0, m_a_v_norm_0, m_a_spatial_w_0, m_a_spatial_b_0, m_b_group_w_0, m_b_scale_0, m_w_out_0, m_norm_1, m_w_in_1, m_sink_1, m_w_out_1, m_final_norm, v_norm_0, v_w_in_0, v_a_v_norm_0, v_a_spatial_w_0, v_a_spatial_b_0, v_b_group_w_0, v_b_scale_0, v_w_out_0, v_norm_1, v_w_in_1, v_sink_1, v_w_out_1, v_final_norm):
    s = x.shape[1]
    xs = x.reshape(s, D_MODEL)
    target = loss_target.reshape(s, D_MODEL)

    cuts = [_Sharded("cols", (1024, 5120)), _Sharded("rows", (2048, 1024)), _Sharded("cols", (1024, 2560)),
            _Sharded("rows", (1024, 1024)), _Sharded("mid", (4, 256, 256))]
    big_w = [w_in_0, w_out_0, w_in_1, w_out_1, b_group_w_0]
    big_m = [m_w_in_0, m_w_out_0, m_w_in_1, m_w_out_1, m_b_group_w_0]
    big_v = [v_w_in_0, v_w_out_0, v_w_in_1, v_w_out_1, v_b_group_w_0]
    where = jnp.stack([2 * lax.axis_index("x") + lax.axis_index("y"), lax.axis_index("c")]).astype(jnp.int32)
    placed = [_place_shard(where, w, cut, f"place_shard{k}") for k, (w, cut) in enumerate(zip(big_w, cuts))]
    cx, cy = lax.axis_index("x"), lax.axis_index("y")
    order = jnp.stack([2 * cx + cy, 2 * (1 - cx) + cy, 2 * cx + 1 - cy, 2 * (1 - cx) + 1 - cy]).astype(jnp.int32)

    row = lambda v: v.reshape(1, 1024)
    ws16 = a_spatial_w_0.astype(BF16)
    bsx = jnp.repeat(a_spatial_b_0.T, 256, axis=1)
    band = _band_matrices(TOK)[0]
    rope = _rope_tables(s)

    h0, z0, w_in0 = _in_proj0_own(order, xs, row(norm_0), w_in_0.astype(BF16), _Gather(placed[:1], cuts[:1]))
    z0, w_out0, wg = _in_proj0_rest(order, h0, w_in0, z0, _Gather([placed[1], placed[4]], [cuts[1], cuts[4]]))
    cat, x1, w_in1, w_out1 = _mix0_fwd(xs, z0, w_out0, row(a_v_norm_0), ws16, bsx, wg, row(b_scale_0), band,
                                       _Gather(placed[2:4], cuts[2:4]))
    h1, q, k, v, gate = _in_proj1(x1, row(norm_1), w_in1, rope)
    kpad = jnp.pad(k, ((ATTN_WINDOW, ATTN_WINDOW), (0, 0)))
    vpad = jnp.pad(v, ((ATTN_WINDOW, ATTN_WINDOW), (0, 0)))
    o, lse = _attn_fwd(q, kpad, vpad, sink_1)
    y1, dx2, do, dgate, loss_lanes, g_final, dx2h = _tail(x1, o, gate, target, w_out1, row(final_norm))

    dq, dkpad, dvpad, dsink = _attn_bwd(q, kpad, vpad, sink_1, o, lse, do, rope)
    dz1, dx1, g_norm1, dx1h = _in_proj1_bwd(dq, dkpad, dvpad, dgate, x1, dx2, row(norm_1), w_in1, rope)

    g_w_in1 = _weight_grad(h1, dz1, 2, 2, "grad_w_in1").reshape(4, 2, 512, 640)
    g_w_out1 = _weight_grad(y1, dx2h, 1, 1, "grad_w_out1").reshape(4, 2, 128, 1024)
    g_w_out0 = _weight_grad(cat, dx1h, 1, 1, "grad_w_out0").reshape(4, 2, 256, 1024)
    first = [g_w_out0, g_w_in1, g_w_out1]
    zero = jnp.zeros((1024,), F32)
    small1 = _pack_small([zero, zero, zero, g_norm1, g_final], None, None, dsink[0, :16])[:8]
    dz0, dpn, d_ws, d_bs, d_gv, d_scale, d_wg, *theirs1, small1_all = _mix0_bwd(
        dx1h, z0, w_out0, row(a_v_norm_0), ws16, bsx, wg, row(b_scale_0), band, _Rider("exchange", first, small1))
    parts1 = [_add_sibling(where, g, t, f"add_sibling1_{k}") for k, (g, t) in enumerate(zip(first, theirs1))]
    dz0 = _fill_pooled_grad(dz0, dpn)
    g_w_in0, *slots1 = _weight_grad(h0, dz0, 4, 1, "grad_w_in0", _Rider("scatter", parts1))
    g_w_in0 = g_w_in0.reshape(4, 2, 512, 1280)
    g_wg = d_wg.reshape(2, 2, 4, 64, 256).transpose(2, 0, 1, 3, 4).reshape(4, 2, 128, 256)
    second = [g_w_in0, g_wg]
    theirs2 = _exchange_halves(second, "exchange_halves2")
    parts2 = [_add_sibling(where, g, t, f"add_sibling2_{k}") for k, (g, t) in enumerate(zip(second, theirs2))]
    small2 = _pack_small([zero, d_gv, d_scale, zero, zero], d_ws, d_bs[:, :4].T, None)
    grad_x, g_norm0, *slots2, small2_all = _in_proj0_bwd(
        dz0, xs, dx1, row(norm_0), w_in0, _Rider("scatter", parts2, small2))

    grads = [g_w_in0, g_w_out0, g_w_in1, g_w_out1, g_wg]
    theirs = [theirs2[0], theirs1[0], theirs1[1], theirs1[2], theirs2[1]]
    slots = [slots2[0], slots1[0], slots1[1], slots1[2], slots2[1]]
    n = len(grads)
    reduced = [_sum_chips(where, grads[w], theirs[w], slots[w], f"sum_chips{w}") for w in range(n)]
    small3 = _pack_small([g_norm0, zero, zero, zero, zero], None, None, None, loss_lanes)[:8]
    *from_sibling, small3_all = _share_halves(reduced, small3)

    out_g, out_d, out_m, out_v = {}, {}, {}, {}
    names = ["w_in_0", "w_out_0", "w_in_1", "w_out_1", "b_group_w_0"]
    for w in range(n):
        shape = big_w[w].shape
        two_d = (-1, shape[-1])
        outs = _adamw_halves(where, big_w[w].reshape(two_d), reduced[w], from_sibling[w], big_m[w].reshape(two_d),
                             big_v[w].reshape(two_d), f"adamw{w}")
        out_g[names[w]], out_d[names[w]], out_m[names[w]], out_v[names[w]] = (a.reshape(shape) for a in outs)

    g_small = _sum_small(small1_all, small2_all, small3_all)
    small_names = ["norm_0", "a_v_norm_0", "b_scale_0", "norm_1", "final_norm"]
    pack = lambda vecs, ws_, bs_, sk: _pack_small(vecs, ws_, bs_, sk)
    w_small = pack([norm_0, a_v_norm_0, b_scale_0, norm_1, final_norm], a_spatial_w_0, a_spatial_b_0, sink_1)
    m_small = pack([m_norm_0, m_a_v_norm_0, m_b_scale_0, m_norm_1, m_final_norm], m_a_spatial_w_0,
                   m_a_spatial_b_0, m_sink_1)
    v_small = pack([v_norm_0, v_a_v_norm_0, v_b_scale_0, v_norm_1, v_final_norm], v_a_spatial_w_0,
                   v_a_spatial_b_0, v_sink_1)
    d_small, nm_small, nv_small = _adamw(w_small, g_small, m_small, v_small, "adamw_small")
    for store, packed in ((out_g, g_small), (out_d, d_small), (out_m, nm_small), (out_v, nv_small)):
        vecs, ws_, bs_, sk = _unpack_small(packed)
        for name, vec in zip(small_names, vecs):
            store[name] = vec
        store["a_spatial_w_0"], store["a_spatial_b_0"], store["sink_1"] = ws_, bs_, sk

    loss = jnp.sum(g_small[7])
    order = ["norm_0", "w_in_0", "a_v_norm_0", "a_spatial_w_0", "a_spatial_b_0", "b_group_w_0", "b_scale_0",
             "w_out_0", "norm_1", "w_in_1", "sink_1", "w_out_1", "final_norm"]
    return (loss, grad_x.reshape(1, s, D_MODEL), *[out_g[k] for k in order], *[out_d[k] for k in order],
            *[out_m[k] for k in order], *[out_v[k] for k in order])
```

```python
import numpy as np
import jax
import jax.numpy as jnp
from jax import lax
from jax.experimental import pallas as pl
from jax.experimental.pallas import tpu as pltpu

F32 = jnp.float32
BF16 = jnp.bfloat16
MESH = pl.DeviceIdType.MESH

D_MODEL = 1024
EPS = 1e-6
NEG_INF = -1e30
CHUNK = 128
POOL_WINDOWS = (2, 4, 8, 16)
HALO = 16
N_HEADS = 16
HEAD_DIM = 64
ATTN_WINDOW = 128
ROPE_THETA = 500000.0
ROT_DIM = 16
ADAM_LR = 0.001
ADAM_B1 = 0.9
ADAM_B2 = 0.999
ADAM_EPS = 1e-08
ADAM_WD = 0.01
ADAM_STEP = 10

TOK = 256
VMEM_LIMIT = 56 * 1024 * 1024


def _params(**kw):
    return pltpu.CompilerParams(vmem_limit_bytes=VMEM_LIMIT, **kw)


def _whole(shape):
    nd = len(shape)
    return pl.BlockSpec(shape, lambda *_: (0,) * nd)


def _rows(t, n):
    return pl.BlockSpec((t, n), lambda i: (i, 0))


ANY = pl.BlockSpec(memory_space=pl.ANY)

_G0 = 0.7978845608028654
_G1 = 0.044715


def _gelu(x):
    return 0.5 * x * (1.0 + jnp.tanh(_G0 * (x + _G1 * x * x * x)))


def _gelu_and_grad(x):
    x2 = x * x
    t = jnp.tanh(_G0 * (x + _G1 * x2 * x))
    half = 0.5 * (1.0 + t)
    return x * half, half + 0.5 * x * (1.0 - t * t) * (_G0 * (1.0 + 3.0 * _G1 * x2))


def _sigmoid(x):
    return 1.0 / (1.0 + jnp.exp(-x))


def _dot(a, b):
    return jnp.dot(a, b, preferred_element_type=F32)


def _dot_nt(a, b):
    return lax.dot_general(a, b, (((1,), (1,)), ((), ())), preferred_element_type=F32)


def _dot_tn(a, b):
    return lax.dot_general(a, b, (((0,), (0,)), ((), ())), preferred_element_type=F32)


def _rms_fwd(x, g):
    r = lax.rsqrt(jnp.mean(x * x, axis=-1, keepdims=True) + EPS)
    xh = x * r
    return r, xh, xh * g


def _rms_bwd(dy, g, r, xh):
    dxh = dy * g
    return r * (dxh - xh * jnp.mean(dxh * xh, axis=-1, keepdims=True))


def _band_matrices(t):
    r = np.arange(t)[:, None]
    j = np.arange(t + 2 * HALO)[None, :]
    fwd, bwd = [], []
    for w in POOL_WINDOWS:
        d = j - r - HALO
        fwd.append((d >= -(w // 2)) & (d < w // 2))
        bwd.append((d >= -(w // 2) + 1) & (d <= w // 2))
    return (jnp.asarray(np.stack(fwd), BF16), jnp.asarray(np.stack(bwd), BF16))


def _window_counts(i, t, s):
    tok = i * t + lax.broadcasted_iota(jnp.int32, (t, 1), 0)
    out = []
    for w in POOL_WINDOWS:
        cnt = jnp.minimum(tok + w // 2, s) - jnp.maximum(tok - w // 2, 0)
        out.append(cnt.astype(F32))
    return out


def _rope_tables(s):
    inv = np.float32(ROPE_THETA) ** (-np.arange(0, ROT_DIM, 2, dtype=np.float32) / np.float32(ROT_DIM))
    ang = np.arange(s, dtype=np.float32)[:, None] * inv.astype(np.float32)[None, :]
    cos, sin = np.cos(ang).astype(np.float32), np.sin(ang).astype(np.float32)
    z8 = np.zeros((s, 8), np.float32)
    z48 = np.zeros((s, HEAD_DIM - ROT_DIM), np.float32)
    c = np.concatenate([cos, cos, np.ones((s, HEAD_DIM - ROT_DIM), np.float32)], axis=1)
    s_lo = np.concatenate([z8, sin, z48], axis=1)
    s_hi = np.concatenate([-sin, z8, z48], axis=1)
    return tuple(jnp.asarray(np.concatenate([a, a], axis=1)) for a in (c, s_lo, s_hi))


def _rope(x, c, s_lo, s_hi):
    n = x.shape[1]
    reps = n // 128
    c, s_lo, s_hi = (jnp.tile(a, (1, reps)) for a in (c, s_lo, s_hi))
    return x * c + pltpu.roll(x, 8, 1) * s_lo + pltpu.roll(x, n - 8, 1) * s_hi


def _rope_t(dx, c, s_lo, s_hi):
    n = dx.shape[1]
    reps = n // 128
    c, s_lo, s_hi = (jnp.tile(a, (1, reps)) for a in (c, s_lo, s_hi))
    return dx * c + pltpu.roll(dx * s_lo, n - 8, 1) + pltpu.roll(dx * s_hi, 8, 1)


def _in_proj0_own(order, x, g0, w_own, rider):
    s = x.shape[0]
    n = w_own.shape[1]

    def body(order_ref, x_ref, g_ref, w_ref, h_ref, z_ref):
        _, _, h = _rms_fwd(x_ref[...], g_ref[...])
        h = h.astype(BF16)
        h_ref[...] = h
        z_ref[...] = _dot(h, w_ref[...]).astype(BF16)

    return pl.pallas_call(
        rider.carried_by(body, 4, 2, s // TOK, late=True), name="in_proj0_own",
        grid_spec=pltpu.PrefetchScalarGridSpec(
            num_scalar_prefetch=1, grid=(s // TOK,),
            in_specs=[pl.BlockSpec((TOK, D_MODEL), lambda i, o: (i, 0)), pl.BlockSpec((1, D_MODEL), lambda i, o: (0, 0)),
                      pl.BlockSpec(w_own.shape, lambda i, o: (0, 0))] + rider.in_specs,
            out_specs=[pl.BlockSpec((TOK, D_MODEL), lambda i, o: (i, 0)),
                       pl.BlockSpec((TOK, n), lambda i, o: (i, o[0]))] + rider.out_specs,
            scratch_shapes=rider.scratch),
        out_shape=[jax.ShapeDtypeStruct((s, D_MODEL), BF16), jax.ShapeDtypeStruct((s, 4 * n), BF16)] + rider.out_shape,
        input_output_aliases={4 + j: 2 + j for j in range(len(rider.fulls))},
        compiler_params=_params(),
    )(order, x, g0, w_own, *rider.fulls)


def _in_proj0_rest(order, h0, w_in0, z0, rider):
    s = h0.shape[0]
    tok = min(s, 4 * TOK)
    n_tiles = s // tok
    n = w_in0.shape[1] // 4

    def body(order_ref, h_ref, w_ref, z_in_ref, z_ref):
        z_ref[...] = _dot(h_ref[...], w_ref[...]).astype(BF16)

    return pl.pallas_call(
        rider.carried_by(body, 4, 1, 3 * n_tiles, lambda: pl.program_id(0) * n_tiles + pl.program_id(1)),
        name="in_proj0_rest",
        grid_spec=pltpu.PrefetchScalarGridSpec(
            num_scalar_prefetch=1, grid=(3, n_tiles),
            in_specs=[pl.BlockSpec((tok, D_MODEL), lambda k, i, o: (i, 0)),
                      pl.BlockSpec((w_in0.shape[0], n), lambda k, i, o: (0, o[1 + k])), ANY] + rider.in_specs,
            out_specs=[pl.BlockSpec((tok, n), lambda k, i, o: (i, o[1 + k]))] + rider.out_specs,
            scratch_shapes=rider.scratch),
        out_shape=[jax.ShapeDtypeStruct(z0.shape, BF16)] + rider.out_shape,
        input_output_aliases={3: 0, **{4 + j: 1 + j for j in range(len(rider.fulls))}},
        compiler_params=_params(),
    )(order, h0, w_in0, z0, *rider.fulls)


def _halo_specs(s, col_block, tok=TOK):
    per = tok // HALO
    last = s // HALO - 1
    prev = pl.BlockSpec((HALO, 1024), lambda i: (jnp.maximum(i * per - 1, 0), col_block))
    nxt = pl.BlockSpec((HALO, 1024), lambda i: (jnp.minimum((i + 1) * per, last), col_block))
    return prev, nxt


def _with_halo(i, n_tiles, prev_ref, cur, next_ref):
    prev = prev_ref[...]
    nxt = next_ref[...]
    prev = jnp.where(i > 0, prev, jnp.zeros_like(prev))
    nxt = jnp.where(i < n_tiles - 1, nxt, jnp.zeros_like(nxt))
    return jnp.concatenate([prev, cur, nxt], axis=0)


def _mixer_a(v1, gv, ws_ref, bsx):
    rv, vh, v2 = _rms_fwd(v1, gv)
    v2 = v2.astype(BF16)
    rows = []
    for c in range(v1.shape[0] // CHUNK):
        cols = [_dot(ws_ref[h], v2[c * CHUNK:(c + 1) * CHUNK, h * 256:(h + 1) * 256]) for h in range(4)]
        rows.append(jnp.concatenate(cols, axis=1) + bsx)
    return rv, vh, v2, jnp.concatenate(rows, axis=0)


def _mixer_b_pooled(bx, halo, band_ref, counts):
    out = []
    for g in range(4):
        win = _dot(band_ref[g], halo[:, g * 256:(g + 1) * 256])
        out.append(win / counts[g] - bx[:, g * 256:(g + 1) * 256])
    return out


def _mix0_fwd(x, z0, w_out0, gv, ws, bsx, wg, scale, band, rider):
    s = x.shape[0]
    tok = min(s, 2 * TOK)
    n_tiles = s // tok
    k = len(rider.fulls)

    def body(z_ref, zp_ref, zn_ref, x_ref, wout_ref, gv_ref, ws_ref, bsx_ref, wg_ref, sc_ref, band_ref,
             cat_ref, x1_ref):
        i = pl.program_id(0)
        halves = [slice(r * TOK, (r + 1) * TOK) for r in range(tok // TOK)]
        mixeds = [_mixer_a(_gelu(z_ref[rows, 1024:2048].astype(F32)), gv_ref[...], ws_ref, bsx_ref[...])[3]
                  for rows in halves]
        x1s = []
        for rows, mixed in zip(halves, mixeds):
            au = z_ref[rows, 0:1024].astype(F32)
            ag = z_ref[rows, 2048:3072].astype(F32)
            ya = (_gelu(au) * mixed * (ag * _sigmoid(ag))).astype(BF16)
            cat_ref[rows, 0:1024] = ya
            x1s.append(x_ref[rows, :] + _dot(ya, wout_ref[0:1024, :]))

        bxs = [z_ref[rows, 3072:4096] for rows in halves]
        ybs = []
        for r, rows in enumerate(halves):
            prev = bxs[r - 1][TOK - HALO:] if r > 0 else jnp.where(i > 0, zp_ref[...], jnp.zeros_like(zp_ref))
            nxt = (bxs[r + 1][:HALO] if r + 1 < len(halves)
                   else jnp.where(i < n_tiles - 1, zn_ref[...], jnp.zeros_like(zn_ref)))
            halo = jnp.concatenate([prev, bxs[r], nxt], axis=0)
            ps = _mixer_b_pooled(bxs[r].astype(F32), halo, band_ref, _window_counts(i * len(halves) + r, TOK, s))
            pw = jnp.concatenate([_dot(ps[g].astype(BF16), wg_ref[g]) for g in range(4)], axis=1)
            bg = z_ref[rows, 4096:5120].astype(F32)
            ybs.append((pw * sc_ref[...] * (bg * _sigmoid(bg))).astype(BF16))
            cat_ref[rows, 1024:2048] = ybs[-1]
        for rows, x1, yb in zip(halves, x1s, ybs):
            x1_ref[rows, :] = x1 + _dot(yb, wout_ref[1024:2048, :])

    prev, nxt = _halo_specs(s, 3, tok)
    return pl.pallas_call(
        rider.carried_by(body, 11, 2, n_tiles), name="mix0_fwd", grid=(n_tiles,),
        in_specs=[_rows(tok, 5120), prev, nxt, _rows(tok, D_MODEL), _whole(w_out0.shape), _whole((1, 1024)),
                  _whole(ws.shape), _whole(bsx.shape), _whole(wg.shape), _whole((1, 1024)), _whole(band.shape)]
        + rider.in_specs,
        out_specs=[_rows(tok, 2048), _rows(tok, D_MODEL)] + rider.out_specs,
        out_shape=[jax.ShapeDtypeStruct((s, 2048), BF16), jax.ShapeDtypeStruct((s, D_MODEL), F32)] + rider.out_shape,
        input_output_aliases={11 + j: 2 + j for j in range(k)},
        scratch_shapes=rider.scratch,
        compiler_params=_params(),
    )(z0, z0, z0, x, w_out0, gv, ws, bsx, wg, scale, band, *rider.fulls)


def _in_proj1(x1, g1, w_in1, rope):
    s = x1.shape[0]

    def body(x_ref, g_ref, w_ref, c_ref, lo_ref, hi_ref, h_ref, q_ref, k_ref, v_ref, gate_ref):
        halves = [slice(r * TOK, (r + 1) * TOK) for r in range(tok // TOK)]
        hs = [_rms_fwd(x_ref[rows, :], g_ref[...])[2].astype(BF16) for rows in halves]
        for rows, h in zip(halves, hs):
            h_ref[rows, :] = h
        qs = [_dot(h, w_ref[:, 0:1024]) for h in hs]
        kvs = [_dot(h, w_ref[:, 1024:1536]) for h in hs]
        for rows, q, kv in zip(halves, qs, kvs):
            tabs = (c_ref[rows, :], lo_ref[rows, :], hi_ref[rows, :])
            q_ref[rows, :] = (_rope(q, *tabs) * Q_SCALE).astype(BF16)
            k_ref[rows, :] = _rope(kv[:, 0:256], *tabs).astype(BF16)
            v_ref[rows, :] = kv[:, 256:512].astype(BF16)
        for rows, h in zip(halves, hs):
            gate_ref[rows, :] = _dot(h, w_ref[:, 1536:2560]).astype(BF16)

    tok = min(s, 2 * TOK)
    tab = _rows(tok, 128)
    return pl.pallas_call(
        body, name="in_proj1", grid=(s // tok,),
        in_specs=[_rows(tok, D_MODEL), _whole((1, D_MODEL)), _whole(w_in1.shape), tab, tab, tab],
        out_specs=[_rows(tok, 1024), _rows(tok, 1024), _rows(tok, 256), _rows(tok, 256), _rows(tok, 1024)],
        out_shape=[jax.ShapeDtypeStruct((s, 1024), BF16), jax.ShapeDtypeStruct((s, 1024), BF16),
                   jax.ShapeDtypeStruct((s, 256), BF16), jax.ShapeDtypeStruct((s, 256), BF16),
                   jax.ShapeDtypeStruct((s, 1024), BF16)],
        compiler_params=_params(),
    )(x1, g1, w_in1, *rope)


QBLK = 128
KBLK = QBLK + 2 * ATTN_WINDOW
Q_SCALE = HEAD_DIM ** -0.5


def _block_bias(q0, s):
    r = lax.broadcasted_iota(jnp.int32, (QBLK, KBLK), 0)
    c = lax.broadcasted_iota(jnp.int32, (QBLK, KBLK), 1)
    kj = q0 - ATTN_WINDOW + c
    ok = (c >= r) & (c <= r + 2 * ATTN_WINDOW) & (kj >= 0) & (kj < s)
    return jnp.where(ok, 0.0, NEG_INF)


def _pair_operands(t):
    lane = lax.broadcasted_iota(jnp.int32, (1, 128), 1)
    first = lane < HEAD_DIM
    zero = jnp.zeros((KBLK, 128), BF16)
    out = []
    for j in range(2):
        slab = t[:, 128 * j:128 * (j + 1)]
        turned = pltpu.bitcast(pltpu.roll(pltpu.bitcast(slab, jnp.uint32), HEAD_DIM, 1), BF16)
        for own_first in (True, False):
            top = jnp.where(first, slab if own_first else turned, zero)
            bottom = jnp.where(first, zero, turned if own_first else slab)
            out.append(jnp.concatenate([top, bottom], axis=0))
    return out


def _attn_fwd(q, kpad, vpad, sink):
    s = q.shape[0]

    def body(sink_ref, q_ref, k_ref, v_ref, o_ref, lse_ref):
        i = pl.program_id(0)
        lane = lax.broadcasted_iota(jnp.int32, (1, 128), 1)
        for b in range(TOK // QBLK):
            rows = slice(b * QBLK, (b + 1) * QBLK)
            start = pl.multiple_of(i * TOK + b * QBLK, QBLK)
            k_bd = _pair_operands(k_ref[pl.ds(start, KBLK), :])
            v_bd = _pair_operands(v_ref[pl.ds(start, KBLK), :])
            bias = _block_bias(i * TOK + b * QBLK, s)
            pairs = range(N_HEADS // 2)
            sc4 = [_dot_nt(jnp.concatenate([q_ref[rows, 256 * g:256 * g + 128], q_ref[rows, 256 * g + 128:256 * (g + 1)]],
                                           axis=0), k_bd[g]) for g in range(4)]
            sc2 = [sc4[m // 2][(m % 2) * QBLK:(m % 2 + 1) * QBLK] for m in pairs]
            scs = [sc2[h // 2][:, (h % 2) * KBLK:(h % 2 + 1) * KBLK] + bias for h in range(N_HEADS)]
            ms = [jnp.maximum(jnp.max(scs[h], axis=-1, keepdims=True), sink_ref[h]) for h in range(N_HEADS)]
            es = [jnp.exp(scs[h] - ms[h]) for h in range(N_HEADS)]
            dens = [jnp.sum(es[h], axis=-1, keepdims=True) + jnp.exp(sink_ref[h] - ms[h]) for h in range(N_HEADS)]
            first = lane < HEAD_DIM
            e2 = [jnp.concatenate([es[2 * m].astype(BF16), es[2 * m + 1].astype(BF16)], axis=1) for m in pairs]
            o4 = [_dot(jnp.concatenate([e2[2 * g], e2[2 * g + 1]], axis=0), v_bd[g]) for g in range(4)]
            outs = [o4[m // 2][(m % 2) * QBLK:(m % 2 + 1) * QBLK]
                    * jnp.where(first, 1.0 / dens[2 * m], 1.0 / dens[2 * m + 1]) for m in pairs]
            o_ref[rows, :] = jnp.concatenate(outs, axis=1).astype(BF16)
            lse = jnp.zeros((QBLK, 128), F32)
            for h in range(N_HEADS):
                lse = lse + jnp.where(lane == h, ms[h] + jnp.log(dens[h]), 0.0)
            lse_ref[rows, :] = lse

    return pl.pallas_call(
        body, name="attn_fwd", grid=(s // TOK,),
        in_specs=[pl.BlockSpec(memory_space=pltpu.SMEM), _rows(TOK, 1024), _whole(kpad.shape), _whole(vpad.shape)],
        out_specs=[_rows(TOK, 1024), _rows(TOK, 128)],
        out_shape=[jax.ShapeDtypeStruct((s, 1024), BF16), jax.ShapeDtypeStruct((s, 128), F32)],
        compiler_params=_params(),
    )(sink, q, kpad, vpad)


def _tail(x1, o, gate, target, w_out1, gf):
    s = x1.shape[0]

    def body(x1_ref, o_ref, gate_ref, t_ref, w_ref, gf_ref, y1_ref, dx2_ref, do_ref, dgate_ref, loss_ref, gfn_ref,
             dx2h_ref):
        i = pl.program_id(0)

        @pl.when(i == 0)
        def _():
            loss_ref[...] = jnp.zeros_like(loss_ref)
            gfn_ref[...] = jnp.zeros_like(gfn_ref)

        halves = [slice(r * TOK, (r + 1) * TOK) for r in range(tok // TOK)]
        gf = gf_ref[...]
        gs = [gate_ref[rows, :].astype(F32) for rows in halves]
        sgs = [_sigmoid(g) for g in gs]
        sils = [g * sg for g, sg in zip(gs, sgs)]
        os_ = [o_ref[rows, :].astype(F32) for rows in halves]
        y1s = [(o * sil).astype(BF16) for o, sil in zip(os_, sils)]
        for rows, y1 in zip(halves, y1s):
            y1_ref[rows, :] = y1
        x2s = [x1_ref[rows, :] + _dot(y1, w_ref[...]) for rows, y1 in zip(halves, y1s)]
        dx2hs = []
        for rows, x2 in zip(halves, x2s):
            r, xh, out = _rms_fwd(x2, gf)
            diff = out - t_ref[rows, :]
            loss_ref[...] += jnp.sum(diff * diff, axis=0, keepdims=True) * (0.5 / D_MODEL)
            dout = diff * (1.0 / D_MODEL)
            gfn_ref[...] += jnp.sum(dout * xh, axis=0, keepdims=True)
            dx2 = _rms_bwd(dout, gf, r, xh)
            dx2_ref[rows, :] = dx2
            dx2hs.append(dx2.astype(BF16))
            dx2h_ref[rows, :] = dx2hs[-1]
        dy1s = [_dot_nt(dx2h, w_ref[...]) for dx2h in dx2hs]
        for rows, dy1, sil, o, sg, g in zip(halves, dy1s, sils, os_, sgs, gs):
            do_ref[rows, :] = (dy1 * sil).astype(BF16)
            dgate_ref[rows, :] = (dy1 * o * (sg * (1.0 + g * (1.0 - sg)))).astype(BF16)

    tok = min(s, 2 * TOK)
    row = _rows(tok, 1024)
    acc = _whole((1, 1024))
    return pl.pallas_call(
        body, name="tail", grid=(s // tok,),
        in_specs=[row, row, row, row, _whole(w_out1.shape), acc],
        out_specs=[row, row, row, row, acc, acc, row],
        out_shape=[jax.ShapeDtypeStruct((s, 1024), BF16), jax.ShapeDtypeStruct((s, 1024), F32),
                   jax.ShapeDtypeStruct((s, 1024), BF16), jax.ShapeDtypeStruct((s, 1024), BF16),
                   jax.ShapeDtypeStruct((1, 1024), F32), jax.ShapeDtypeStruct((1, 1024), F32),
                   jax.ShapeDtypeStruct((s, 1024), BF16)],
        compiler_params=_params(),
    )(x1, o, gate, target, w_out1, gf)


def _attn_bwd(q, kpad, vpad, sink, o, lse, do, rope):
    s = q.shape[0]
    pad_t = (kpad.shape[1], kpad.shape[0])

    def body(sink_ref, q_ref, k_ref, v_ref, o_ref, lse_ref, do_ref, c_ref, lo_ref, hi_ref,
             dq_ref, dk_ref, dv_ref, ds_ref):
        i = pl.program_id(0)

        @pl.when(i == 0)
        def _():
            dk_ref[...] = jnp.zeros_like(dk_ref)
            dv_ref[...] = jnp.zeros_like(dv_ref)
            ds_ref[...] = jnp.zeros_like(ds_ref)

        lane = lax.broadcasted_iota(jnp.int32, (1, 128), 1)
        dsink = jnp.zeros((1, 128), F32)
        for b in range(TOK // QBLK):
            rows = slice(b * QBLK, (b + 1) * QBLK)
            start = pl.multiple_of(i * TOK + b * QBLK, QBLK)
            k_bd = _pair_operands(k_ref[pl.ds(start, KBLK), :])
            v_bd = _pair_operands(v_ref[pl.ds(start, KBLK), :])
            bias = _block_bias(i * TOK + b * QBLK, s)
            lanes_of = (lane < HEAD_DIM, lane >= HEAD_DIM)
            half = lambda t, j: t[:, j * KBLK:(j + 1) * KBLK]
            dqs, dks, dvs = [], [], []
            for g in range(4):
                pairs = (2 * g, 2 * g + 1)
                qs = {m: q_ref[rows, 128 * m:128 * (m + 1)] for m in pairs}
                dos = {m: do_ref[rows, 128 * m:128 * (m + 1)] for m in pairs}
                lses = {h: lse_ref[rows, h:h + 1] for h in range(4 * g, 4 * g + 4)}
                stacked = lambda parts: jnp.concatenate([parts[m] for m in pairs], axis=0)
                unstack = lambda t: {m: t[j * QBLK:(j + 1) * QBLK] for j, m in enumerate(pairs)}
                sc2 = unstack(_dot_nt(stacked(qs), k_bd[g]))
                ps = {2 * m + j: jnp.exp(half(sc2[m], j) + bias - lses[2 * m + j]) for m in pairs for j in range(2)}
                prods = {m: dos[m].astype(F32) * o_ref[rows, 128 * m:128 * (m + 1)].astype(F32) for m in pairs}
                deltas = {2 * m + j: jnp.sum(jnp.where(lanes_of[j], prods[m], 0.0), axis=-1, keepdims=True)
                          for m in pairs for j in range(2)}
                for h in range(4 * g, 4 * g + 4):
                    dsink = dsink + jnp.where(
                        lane == h, -jnp.sum(jnp.exp(sink_ref[h] - lses[h]) * deltas[h], axis=0, keepdims=True), 0.0)
                dp2 = unstack(_dot_nt(stacked(dos), v_bd[g]))
                ds2 = {m: jnp.concatenate(
                    [(ps[2 * m + j] * (half(dp2[m], j) - deltas[2 * m + j])).astype(BF16) for j in range(2)], axis=1)
                    for m in pairs}
                p2 = {m: jnp.concatenate([ps[2 * m].astype(BF16), ps[2 * m + 1].astype(BF16)], axis=1) for m in pairs}
                diag = lambda t: t[0:HEAD_DIM, 0:KBLK] + t[HEAD_DIM:128, KBLK:2 * KBLK]
                dvs.append(diag(_dot_tn(stacked(dos), stacked(p2))))
                dks.append(diag(_dot_tn(stacked(qs), stacked(ds2))))
                dq2 = unstack(_dot(stacked(ds2), k_bd[g]) * Q_SCALE)
                dqs += [dq2[m] for m in pairs]
            dq = jnp.concatenate(dqs, axis=1)
            dq_ref[rows, :] = _rope_t(dq, c_ref[rows, :], lo_ref[rows, :], hi_ref[rows, :]).astype(BF16)
            dk_ref[:, pl.ds(start, KBLK)] += jnp.concatenate(dks, axis=0)
            dv_ref[:, pl.ds(start, KBLK)] += jnp.concatenate(dvs, axis=0)
        ds_ref[...] += dsink

    row = _rows(TOK, 1024)
    tab = _rows(TOK, 128)
    pad = _whole(kpad.shape)
    return pl.pallas_call(
        body, name="attn_bwd", grid=(s // TOK,),
        in_specs=[pl.BlockSpec(memory_space=pltpu.SMEM), row, pad, pad, row, tab, row, tab, tab, tab],
        out_specs=[row, _whole(pad_t), _whole(pad_t), _whole((1, 128))],
        out_shape=[jax.ShapeDtypeStruct((s, 1024), BF16), jax.ShapeDtypeStruct(pad_t, F32),
                   jax.ShapeDtypeStruct(pad_t, F32), jax.ShapeDtypeStruct((1, 128), F32)],
        compiler_params=_params(),
    )(sink, q, kpad, vpad, o, lse, do, *rope)


def _in_proj1_bwd(dq, dk_t, dv_t, dgate, x1, dx2, g1, w_in1, rope):
    s = x1.shape[0]
    tok = min(s, 2 * TOK)
    n_sub = tok // ATTN_WINDOW

    def body(*refs):
        dq_ref = refs[0]
        dk_refs, dv_refs = refs[1:1 + n_sub], refs[1 + n_sub:1 + 2 * n_sub]
        (dgate_ref, x1_ref, dx2_ref, g_ref, w_ref, c_ref, lo_ref, hi_ref,
         dz_ref, dx1_ref, gn_ref, dx1h_ref) = refs[1 + 2 * n_sub:]

        @pl.when(pl.program_id(0) == 0)
        def _():
            gn_ref[...] = jnp.zeros_like(gn_ref)

        halves = [slice(r * TOK, (r + 1) * TOK) for r in range(tok // TOK)]
        per = TOK // ATTN_WINDOW
        g = g_ref[...]
        for r, rows in enumerate(halves):
            dk = jnp.concatenate([ref[...] for ref in dk_refs[r * per:(r + 1) * per]], axis=1).T
            dv = jnp.concatenate([ref[...] for ref in dv_refs[r * per:(r + 1) * per]], axis=1).T
            dz_ref[rows, 0:1024] = dq_ref[rows, :]
            dz_ref[rows, 1024:1280] = _rope_t(dk, c_ref[rows, :], lo_ref[rows, :], hi_ref[rows, :]).astype(BF16)
            dz_ref[rows, 1280:1536] = dv.astype(BF16)
            dz_ref[rows, 1536:2560] = dgate_ref[rows, :]
        dhs = [_dot_nt(dz_ref[rows, :], w_ref[...]) for rows in halves]
        for rows, dh in zip(halves, dhs):
            r, xh, _ = _rms_fwd(x1_ref[rows, :], g)
            gn_ref[...] += jnp.sum(dh * xh, axis=0, keepdims=True)
            dx1 = dx2_ref[rows, :] + _rms_bwd(dh, g, r, xh)
            dx1_ref[rows, :] = dx1
            dx1h_ref[rows, :] = dx1.astype(BF16)

    row = _rows(tok, 1024)
    subs = [pl.BlockSpec((256, ATTN_WINDOW), lambda i, j=j: (0, n_sub * i + 1 + j)) for j in range(n_sub)]
    tab = _rows(tok, 128)
    acc = _whole((1, 1024))
    return pl.pallas_call(
        body, name="in_proj1_bwd", grid=(s // tok,),
        in_specs=[row] + subs + subs + [row, row, row, acc, _whole(w_in1.shape), tab, tab, tab],
        out_specs=[_rows(tok, 2560), row, acc, row],
        out_shape=[jax.ShapeDtypeStruct((s, 2560), BF16), jax.ShapeDtypeStruct((s, 1024), F32),
                   jax.ShapeDtypeStruct((1, 1024), F32), jax.ShapeDtypeStruct((s, 1024), BF16)],
        compiler_params=_params(),
    )(dq, *[dk_t] * n_sub, *[dv_t] * n_sub, dgate, x1, dx2, g1, w_in1, *rope)


def _mix0_bwd(dx1, z0, w_out0, gv, ws, bsx, wg, scale, band, rider):
    s = dx1.shape[0]
    n_tiles = s // TOK

    def body(dx1h_ref, z_ref, zp_ref, zn_ref, wout_ref, gv_ref, ws_ref, bsx_ref, wg_ref, sc_ref, band_ref,
             dz_ref, dpn_ref, dws_ref, dbs_ref, dgv_ref, dsc_ref, dwg_ref):
        i = pl.program_id(0)
        dz_ref[:, 3072:4096] = jnp.zeros((TOK, 1024), BF16)

        @pl.when(i == 0)
        def _():
            for ref in (dws_ref, dbs_ref, dgv_ref, dsc_ref, dwg_ref):
                ref[...] = jnp.zeros_like(ref)

        dcat = _dot_nt(dx1h_ref[...], wout_ref[...])
        dya = dcat[:, 0:1024]
        dyb = dcat[:, 1024:2048]

        au = z_ref[:, 0:1024].astype(F32)
        av = z_ref[:, 1024:2048].astype(F32)
        ag = z_ref[:, 2048:3072].astype(F32)
        gv = gv_ref[...]
        u, du = _gelu_and_grad(au)
        v1, dv1 = _gelu_and_grad(av)
        rv, vh, v2, mixed = _mixer_a(v1, gv, ws_ref, bsx_ref[...])
        sg = _sigmoid(ag)
        sil = ag * sg
        dz_ref[:, 2048:3072] = (dya * u * mixed * (sg * (1.0 + ag * (1.0 - sg)))).astype(BF16)
        dz_ref[:, 0:1024] = (dya * mixed * sil * du).astype(BF16)
        dmixed = dya * u * sil
        lane = lax.broadcasted_iota(jnp.int32, (1, 128), 1)
        dm16 = dmixed.astype(BF16)
        dv2_rows = []
        for c in range(TOK // CHUNK):
            rows = slice(c * CHUNK, (c + 1) * CHUNK)
            cols_out = []
            for h in range(4):
                cols = slice(h * 256, (h + 1) * 256)
                dws_ref[h] += _dot_nt(dm16[rows, cols], v2[rows, cols])
                dbs_ref[...] += jnp.where(lane == h, jnp.sum(dmixed[rows, cols], axis=-1, keepdims=True), 0.0)
                cols_out.append(_dot_tn(ws_ref[h], dm16[rows, cols]))
            dv2_rows.append(jnp.concatenate(cols_out, axis=1))
        dv2 = jnp.concatenate(dv2_rows, axis=0)
        dgv_ref[...] += jnp.sum(dv2 * vh, axis=0, keepdims=True)
        dz_ref[:, 1024:2048] = (_rms_bwd(dv2, gv, rv, vh) * dv1).astype(BF16)

        bx = z_ref[:, 3072:4096]
        bg = z_ref[:, 4096:5120].astype(F32)
        counts = _window_counts(i, TOK, s)
        halo = _with_halo(i, n_tiles, zp_ref, bx, zn_ref)
        ps = [p.astype(BF16) for p in _mixer_b_pooled(bx.astype(F32), halo, band_ref, counts)]
        pw = jnp.concatenate([_dot(ps[g], wg_ref[g]) for g in range(4)], axis=1)
        sgb = _sigmoid(bg)
        sc = sc_ref[...]
        dz_ref[:, 4096:5120] = (dyb * pw * sc * (sgb * (1.0 + bg * (1.0 - sgb)))).astype(BF16)
        dys = dyb * (bg * sgb)
        dsc_ref[...] += jnp.sum(dys * pw, axis=0, keepdims=True)
        dpw = (dys * sc).astype(BF16)
        for g in range(4):
            cols = slice(g * 256, (g + 1) * 256)
            dwg_ref[g] += _dot_tn(ps[g], dpw[:, cols])
            dpn_ref[:, cols] = (_dot_nt(dpw[:, cols], wg_ref[g]) / counts[g]).astype(BF16)

    prev, nxt = _halo_specs(s, 3)
    row = _rows(TOK, 1024)
    vec = _whole((1, 1024))
    return pl.pallas_call(
        rider.carried_by(body, 11, 7, n_tiles), name="mix0_bwd", grid=(n_tiles,),
        in_specs=[row, _rows(TOK, 5120), prev, nxt, _whole(w_out0.shape), vec, _whole(ws.shape),
                  _whole(bsx.shape), _whole(wg.shape), vec, _whole(band.shape)] + rider.in_specs,
        out_specs=[_rows(TOK, 5120), row, _whole((4, 128, 128)), _whole((128, 128)), vec, vec,
                   _whole((4, 256, 256))] + rider.out_specs,
        out_shape=[jax.ShapeDtypeStruct((s, 5120), BF16), jax.ShapeDtypeStruct((s, 1024), BF16),
                   jax.ShapeDtypeStruct((4, 128, 128), F32), jax.ShapeDtypeStruct((128, 128), F32),
                   jax.ShapeDtypeStruct((1, 1024), F32), jax.ShapeDtypeStruct((1, 1024), F32),
                   jax.ShapeDtypeStruct((4, 256, 256), F32)] + rider.out_shape,
        scratch_shapes=rider.scratch,
        compiler_params=_params(),
    )(dx1, z0, z0, z0, w_out0, gv, ws, bsx, wg, scale, band, *rider.parts)


def _fill_pooled_grad(dz0, dpn):
    s = dpn.shape[0]
    tok = min(s, 2 * TOK)
    n_tiles = s // tok
    band_t = _band_matrices(tok)[1]

    def body(dz_in_ref, dpn_ref, dpp_ref, dpx_ref, band_ref, dbx_ref):
        i = pl.program_id(0)
        dpn = dpn_ref[...]
        halo = _with_halo(i, n_tiles, dpp_ref, dpn, dpx_ref)
        counts = _window_counts(i, tok, s)
        for g in range(4):
            cols = slice(g * 256, (g + 1) * 256)
            dbx = _dot(band_ref[g], halo[:, cols]) - dpn[:, cols].astype(F32) * counts[g]
            dbx_ref[:, cols] = dbx.astype(BF16)

    prev, nxt = _halo_specs(s, 0, tok)
    return pl.pallas_call(
        body, name="fill_pooled_grad", grid=(n_tiles,),
        in_specs=[ANY, _rows(tok, 1024), prev, nxt, _whole(band_t.shape)],
        out_specs=pl.BlockSpec((tok, 1024), lambda i: (i, 3)),
        out_shape=jax.ShapeDtypeStruct(dz0.shape, BF16),
        input_output_aliases={0: 0},
        compiler_params=_params(),
    )(dz0, dpn, dpn, dpn, band_t)


def _in_proj0_bwd(dz0, x, dx1, g0, w_in0, rider):
    s = x.shape[0]
    tok = min(s, 2 * TOK)
    n_tiles = s // tok

    def body(dz_ref, x_ref, dx1_ref, g_ref, w_ref, dx_ref, gn_ref):
        i = pl.program_id(0)

        @pl.when(i == 0)
        def _():
            gn_ref[...] = jnp.zeros_like(gn_ref)

        halves = [slice(r * TOK, (r + 1) * TOK) for r in range(tok // TOK)]
        g0v = g_ref[...]
        dhs = [_dot_nt(dz_ref[rows, :], w_ref[...]) for rows in halves]
        for rows, dh in zip(halves, dhs):
            r, xh, _ = _rms_fwd(x_ref[rows, :], g0v)
            gn_ref[...] += jnp.sum(dh * xh, axis=0, keepdims=True)
            dx_ref[rows, :] = dx1_ref[rows, :] + _rms_bwd(dh, g0v, r, xh)

    row = _rows(tok, 1024)
    vec = _whole((1, 1024))
    return pl.pallas_call(
        rider.carried_by(body, 5, 2, n_tiles), name="in_proj0_bwd", grid=(n_tiles,),
        in_specs=[_rows(tok, 5120), row, row, vec, _whole(w_in0.shape)] + rider.in_specs,
        out_specs=[row, vec] + rider.out_specs,
        out_shape=[jax.ShapeDtypeStruct((s, 1024), F32), jax.ShapeDtypeStruct((1, 1024), F32)] + rider.out_shape,
        scratch_shapes=rider.scratch,
        compiler_params=_params(),
    )(dz0, x, dx1, g0, w_in0, *rider.parts)


def _weight_grad(a, b, n_blocks, split, name, rider=None):
    s, k = a.shape
    n = b.shape[1]
    tn = n // n_blocks
    w = tn // split
    ts = min(s, 2048)

    def body(a_ref, b_ref, o_ref):
        @pl.when(pl.program_id(1) == 0)
        def _():
            o_ref[...] = jnp.zeros_like(o_ref)

        res = _dot_tn(a_ref[...], b_ref[...])
        for q in range(split):
            o_ref[q] += res[:, q * w:(q + 1) * w]

    in_specs = [pl.BlockSpec((ts, k), lambda j, t: (t, 0)), pl.BlockSpec((ts, tn), lambda j, t: (t, j))]
    out_spec = pl.BlockSpec((split, k, w), lambda j, t: (j, 0, 0))
    out_shape = jax.ShapeDtypeStruct((n_blocks * split, k, w), F32)
    if rider is None:
        return pl.pallas_call(body, name=name, grid=(n_blocks, s // ts), in_specs=in_specs, out_specs=out_spec,
                              out_shape=out_shape, compiler_params=_params())(a, b)
    steps = s // ts
    return pl.pallas_call(
        rider.carried_by(body, 2, 1, n_blocks * steps, lambda: pl.program_id(0) * steps + pl.program_id(1)),
        name=name, grid=(n_blocks, steps), in_specs=in_specs + rider.in_specs,
        out_specs=[out_spec] + rider.out_specs, out_shape=[out_shape] + rider.out_shape,
        scratch_shapes=rider.scratch, compiler_params=_params(),
    )(a, b, *rider.parts)


def _row_tile(rows, cols):
    t = rows
    while t * cols * 4 > (1 << 21) and t % 32 == 0:
        t //= 2
    return t


def _add_sibling(where, g, theirs, name):
    _, _, rows, cols = g.shape
    t = _row_tile(rows, cols)

    def body(where_ref, g_ref, t_ref, o_ref):
        o_ref[...] = (g_ref[...] + t_ref[...]).astype(BF16)

    spec = pl.BlockSpec((None, t, cols), lambda s, i, p: (s, i, 0))
    return pl.pallas_call(
        body, name=name, out_shape=jax.ShapeDtypeStruct((4, rows, cols), BF16),
        grid_spec=pltpu.PrefetchScalarGridSpec(
            num_scalar_prefetch=1, grid=(4, rows // t),
            in_specs=[pl.BlockSpec((None, None, t, cols), lambda s, i, p: (s, p[1], i, 0)), spec], out_specs=spec),
        compiler_params=_params())(where, g, theirs)


def _sum_chips(where, g, theirs, slots, name):
    _, _, rows, cols = g.shape
    t = _row_tile(rows, cols)

    def body(where_ref, g_ref, t_ref, s_ref, o_ref):
        me = where_ref[0]
        own = g_ref[...] + t_ref[...]
        acc = jnp.where(me == 0, own, s_ref[0].astype(F32))
        for k in range(1, 4):
            acc = acc + jnp.where(me == k, own, s_ref[k].astype(F32))
        o_ref[...] = acc

    return pl.pallas_call(
        body, name=name, out_shape=jax.ShapeDtypeStruct((rows, cols), F32),
        grid_spec=pltpu.PrefetchScalarGridSpec(
            num_scalar_prefetch=1, grid=(rows // t,),
            in_specs=[pl.BlockSpec((None, None, t, cols), lambda i, p: (p[0], p[1], i, 0)),
                      pl.BlockSpec((None, t, cols), lambda i, p: (p[0], i, 0)),
                      pl.BlockSpec((4, t, cols), lambda i, p: (0, i, 0))],
            out_specs=pl.BlockSpec((t, cols), lambda i, p: (i, 0))),
        compiler_params=_params())(where, g, theirs, slots)


def _adamw_halves(where, w, own, theirs, m, v, name):
    rows, cols = own.shape
    t = _row_tile(rows, cols)
    per = rows // t

    def body(where_ref, w_ref, own_ref, th_ref, m_ref, v_ref, g_ref, d_ref, nm_ref, nv_ref):
        g = jnp.where(pl.program_id(0) == where_ref[1], own_ref[...], th_ref[...])
        g_ref[...] = g
        m2 = ADAM_B1 * m_ref[...] + (1.0 - ADAM_B1) * g
        v2 = ADAM_B2 * v_ref[...] + (1.0 - ADAM_B2) * (g * g)
        m_hat = m2 / (1.0 - ADAM_B1 ** ADAM_STEP)
        v_hat = v2 / (1.0 - ADAM_B2 ** ADAM_STEP)
        d_ref[...] = -ADAM_LR * (m_hat / (jnp.sqrt(v_hat) + ADAM_EPS) + ADAM_WD * w_ref[...])
        nm_ref[...] = m2
        nv_ref[...] = v2

    full = pl.BlockSpec((t, cols), lambda h, i, p: (h * per + i, 0))
    half = pl.BlockSpec((t, cols), lambda h, i, p: (i, 0))
    shp = jax.ShapeDtypeStruct(w.shape, F32)
    return pl.pallas_call(
        body, name=name, out_shape=[shp] * 4,
        grid_spec=pltpu.PrefetchScalarGridSpec(
            num_scalar_prefetch=1, grid=(2, per), in_specs=[full, half, half, full, full], out_specs=[full] * 4),
        compiler_params=_params())(where, w, own, theirs, m, v)


def _place_shard(where, w, cut, name):
    if cut.kind == "cols":
        r, n = cut.full_shape
        blk, grid = (256, n // 4), (r // 256,)
        src_map, dst_map = (lambda i, p: (i, 0)), (lambda i, p: (i, p[0]))
    elif cut.kind == "rows":
        r, n = cut.full_shape
        per = r // 4 // 256
        blk, grid = (256, n), (per,)
        src_map, dst_map = (lambda i, p: (i, 0)), (lambda i, p: (p[0] * per + i, 0))
    else:
        g, r, n = cut.full_shape
        blk, grid = (g, r // 4, n), (1,)
        src_map, dst_map = (lambda i, p: (0, 0, 0)), (lambda i, p: (0, p[0], 0))

    def body(where_ref, w_ref, o_ref):
        o_ref[...] = w_ref[...].astype(BF16)

    return pl.pallas_call(
        body, name=name, out_shape=jax.ShapeDtypeStruct(cut.full_shape, BF16),
        grid_spec=pltpu.PrefetchScalarGridSpec(
            num_scalar_prefetch=1, grid=grid, in_specs=[pl.BlockSpec(blk, src_map)],
            out_specs=pl.BlockSpec(blk, dst_map)),
        compiler_params=_params())(where, w)


def _sum_small(first, second, third):
    rows = second.shape[1]

    def body(a_ref, b_ref, c_ref, o_ref):
        top = a_ref[0] + c_ref[0] + b_ref[0, 0:8]
        rest = b_ref[0, 8:rows]
        for k in range(1, 8):
            top = top + (a_ref[k] + c_ref[k] + b_ref[k, 0:8])
            rest = rest + b_ref[k, 8:rows]
        o_ref[0:8] = top
        o_ref[8:rows] = rest

    return pl.pallas_call(
        body, name="sum_small", in_specs=[_whole(first.shape), _whole(second.shape), _whole(third.shape)],
        out_specs=_whole(second.shape[1:]), out_shape=jax.ShapeDtypeStruct(second.shape[1:], F32),
        compiler_params=_params())(first, second, third)


def _adamw(w, g, m, v, name):
    rows, cols = w.shape
    t = _row_tile(rows, cols)

    def body(w_ref, g_ref, m_ref, v_ref, d_ref, nm_ref, nv_ref):
        g = g_ref[...]
        m2 = ADAM_B1 * m_ref[...] + (1.0 - ADAM_B1) * g
        v2 = ADAM_B2 * v_ref[...] + (1.0 - ADAM_B2) * (g * g)
        m_hat = m2 / (1.0 - ADAM_B1 ** ADAM_STEP)
        v_hat = v2 / (1.0 - ADAM_B2 ** ADAM_STEP)
        d_ref[...] = -ADAM_LR * (m_hat / (jnp.sqrt(v_hat) + ADAM_EPS) + ADAM_WD * w_ref[...])
        nm_ref[...] = m2
        nv_ref[...] = v2

    spec = pl.BlockSpec((t, cols), lambda i: (i, 0))
    shp = jax.ShapeDtypeStruct(w.shape, F32)
    return pl.pallas_call(body, name=name, grid=(rows // t,), in_specs=[spec] * 4, out_specs=[spec] * 3,
                          out_shape=[shp] * 3, compiler_params=_params())(w, g, m, v)


def _place():
    x, y, c = lax.axis_index("x"), lax.axis_index("y"), lax.axis_index("c")
    chips = [(1 - x, y), (x, 1 - y), (1 - x, 1 - y)]
    return x, y, c, chips


class _Sharded:
    def __init__(self, kind, full_shape):
        self.kind = kind
        self.full_shape = full_shape

    def in_full(self, ref, s, h):
        if self.kind == "cols":
            r, n = self.full_shape
            return ref.at[pl.ds(h * (r // 2), r // 2), pl.ds(pl.multiple_of(s * (n // 4), 128), n // 4)]
        if self.kind == "rows":
            r, _ = self.full_shape
            return ref.at[pl.ds(pl.multiple_of(s * (r // 4) + h * (r // 8), 8), r // 8), :]
        g, r, _ = self.full_shape
        return ref.at[pl.ds(h * (g // 2), g // 2), pl.ds(pl.multiple_of(s * (r // 4), 16), r // 4), :]


def _remote(src, dst, send_sem, recv_sem, to):
    return pltpu.make_async_remote_copy(src_ref=src, dst_ref=dst, send_sem=send_sem, recv_sem=recv_sem,
                                        device_id=to, device_id_type=MESH)


def _start_remote(src, dst, send_sem, recv_sem, to):
    cp = _remote(src, dst, send_sem, recv_sem, to)
    cp.start()
    return cp


class _Gather:
    def __init__(self, fulls, cuts):
        n = len(fulls)
        self.fulls, self.cuts = list(fulls), list(cuts)
        self.in_specs = [ANY] * n
        self.out_specs = [ANY] * n
        self.out_shape = [jax.ShapeDtypeStruct(cut.full_shape, BF16) for cut in cuts]
        self.scratch = [pltpu.SemaphoreType.DMA((6 * n,)), pltpu.SemaphoreType.DMA((6 * n,))]

    def _step(self, step, src, out, send_sems, recv_sems):
        n, cuts = len(self.fulls), self.cuts
        x, y, c, chips = _place()
        me = 2 * x + y

        def ends(w, s, h, from_src):
            dst = cuts[w].in_full(out[w], s, h)
            return (cuts[w].in_full(src[w], s, h) if from_src else dst), dst

        for w in range(n):
            for j, chip in enumerate(chips):
                s = 2 * chip[0] + chip[1]
                k, k2 = 3 * w + j, 3 * n + 3 * w + j
                if step == "send":
                    _start_remote(*ends(w, me, c, True), send_sems.at[k], recv_sems.at[k], (*chip, c))
                elif step == "pass_on":
                    _remote(*ends(w, s, c, False), send_sems.at[k], recv_sems.at[k], (x, y, c)).wait_recv()
                    _start_remote(*ends(w, s, c, False), send_sems.at[k2], recv_sems.at[k2], (x, y, 1 - c))
                else:
                    _remote(*ends(w, s, 1 - c, False), send_sems.at[k2], recv_sems.at[k2], (x, y, c)).wait_recv()
                    _remote(*ends(w, me, c, True), send_sems.at[k], recv_sems.at[k], (x, y, c)).wait_send()
                    _remote(*ends(w, s, c, False), send_sems.at[k2], recv_sems.at[k2], (x, y, c)).wait_send()

    def carried_by(self, body, n_in, n_out, n_steps, step_index=lambda: pl.program_id(0), late=False):
        k = len(self.fulls)

        def carrier(*refs):
            ins, src = refs[:n_in], refs[n_in:n_in + k]
            outs, out = refs[n_in + k:n_in + k + n_out], refs[n_in + k + n_out:n_in + 2 * k + n_out]
            sems = refs[n_in + 2 * k + n_out:]

            def at_step(step, at):
                @pl.when(step_index() == at)
                def _():
                    self._step(step, src, out, *sems)

            at_step("send", 0)
            if not late:
                at_step("pass_on", 3 * n_steps // 4)
            body(*ins, *outs)
            if late:
                at_step("pass_on", n_steps - 1)
            at_step("finish", n_steps - 1)

        return carrier


def _exchange_halves(grads, name):
    n = len(grads)

    def body(*refs):
        g = refs[:n]
        theirs = refs[n:2 * n]
        send_sems, recv_sems = refs[2 * n:]
        x, y, c, _ = _place()
        sends = [_start_remote(g[w].at[:, 1 - c], theirs[w], send_sems.at[w], recv_sems.at[w], (x, y, 1 - c))
                 for w in range(n)]
        for w in range(n):
            _remote(g[w].at[:, 1 - c], theirs[w], send_sems.at[w], recv_sems.at[w], (x, y, c)).wait_recv()
        for cp in sends:
            cp.wait_send()

    return pl.pallas_call(
        body, name=name,
        in_specs=[ANY] * n, out_specs=[ANY] * n,
        out_shape=[jax.ShapeDtypeStruct((4,) + g.shape[2:], F32) for g in grads],
        scratch_shapes=[pltpu.SemaphoreType.DMA((n,)), pltpu.SemaphoreType.DMA((n,))],
        compiler_params=pltpu.CompilerParams(has_side_effects=True),
    )(*grads)


def _gather_small(small_ref, gathered, send_sems, recv_sems, first_sem, local_sem, start):
    x, y, c, _ = _place()
    me = 4 * x + 2 * y + c
    flips = [(fx, fy, fc) for fx in range(2) for fy in range(2) for fc in range(2)][1:]
    own = pltpu.make_async_copy(small_ref, gathered.at[me], local_sem)
    if start:
        own.start()
    else:
        own.wait()
    for k, (fx, fy, fc) in enumerate(flips):
        peer = (x + fx - 2 * x * fx, y + fy - 2 * y * fy, c + fc - 2 * c * fc)
        sems = (send_sems.at[first_sem + k], recv_sems.at[first_sem + k])
        if start:
            _start_remote(small_ref, gathered.at[me], *sems, peer)
        else:
            cp = _remote(small_ref, gathered.at[4 * peer[0] + 2 * peer[1] + peer[2]], *sems, (x, y, c))
            cp.wait_recv()
            cp.wait_send()


class _Rider:
    def __init__(self, kind, parts, small=None):
        n = len(parts)
        self.kind, self.n = kind, n
        self.per = 3 if kind == "scatter" else 1
        self.parts = list(parts) + ([] if small is None else [small])
        k = len(self.parts)
        self.in_specs = [ANY] * k
        self.out_specs = [ANY] * k
        dtype = lambda a: a.dtype if kind == "scatter" else F32
        self.out_shape = [jax.ShapeDtypeStruct((4,) + a.shape[-2:], dtype(a)) for a in parts]
        if small is not None:
            self.out_shape.append(jax.ShapeDtypeStruct((8,) + small.shape, small.dtype))
        self.scratch = [pltpu.SemaphoreType.DMA((self.per * n + 7,)), pltpu.SemaphoreType.DMA((self.per * n + 7,)),
                        pltpu.SemaphoreType.DMA]

    def _copies(self, p, out, send_sems, recv_sems, local_sem, start):
        x, y, c, chips = _place()
        me = 2 * x + y
        n = self.n
        if len(self.parts) > n:
            _gather_small(p[n], out[n], send_sems, recv_sems, self.per * n, local_sem, start)
        for w in range(n):
            if self.kind == "exchange":
                ends = [(p[w].at[:, 1 - c], out[w], out[w], (x, y, 1 - c))]
            else:
                ends = [(p[w].at[2 * cx + cy], out[w].at[me], out[w].at[2 * cx + cy], (cx, cy, c)) for cx, cy in chips]
            for j, (src, dst_there, dst_here, to) in enumerate(ends):
                sems = (send_sems.at[self.per * w + j], recv_sems.at[self.per * w + j])
                if start:
                    _start_remote(src, dst_there, *sems, to)
                else:
                    cp = _remote(src, dst_here, *sems, (x, y, c))
                    cp.wait_recv()
                    cp.wait_send()

    def carried_by(self, body, n_in, n_out, n_steps, step_index=lambda: pl.program_id(0)):
        k = len(self.parts)

        def carrier(*refs):
            ins, mine = refs[:n_in], refs[n_in:n_in + k]
            outs, theirs = refs[n_in + k:n_in + k + n_out], refs[n_in + k + n_out:n_in + 2 * k + n_out]
            sems = refs[n_in + 2 * k + n_out:]

            @pl.when(step_index() == 0)
            def _():
                self._copies(mine, theirs, *sems, start=True)

            body(*ins, *outs)

            @pl.when(step_index() == n_steps - 1)
            def _():
                self._copies(mine, theirs, *sems, start=False)

        return carrier


def _share_halves(halves, small):
    n = len(halves)

    def body(*refs):
        hv = refs[:n]
        small_ref = refs[n]
        out = refs[n + 1:2 * n + 1]
        gathered = refs[2 * n + 1]
        send_sems, recv_sems, local_sem = refs[2 * n + 2:]
        x, y, c, _ = _place()
        sends = [_start_remote(hv[w], out[w], send_sems.at[w], recv_sems.at[w], (x, y, 1 - c)) for w in range(n)]
        _gather_small(small_ref, gathered, send_sems, recv_sems, n, local_sem, True)
        for w in range(n):
            _remote(hv[w], out[w], send_sems.at[w], recv_sems.at[w], (x, y, c)).wait_recv()
        for cp in sends:
            cp.wait_send()
        _gather_small(small_ref, gathered, send_sems, recv_sems, n, local_sem, False)

    return pl.pallas_call(
        body, name="share_halves",
        in_specs=[ANY] * (n + 1), out_specs=[ANY] * (n + 1),
        out_shape=[jax.ShapeDtypeStruct(a.shape, F32) for a in halves] + [jax.ShapeDtypeStruct((8,) + small.shape, F32)],
        scratch_shapes=[pltpu.SemaphoreType.DMA((n + 7,)), pltpu.SemaphoreType.DMA((n + 7,)),
                        pltpu.SemaphoreType.DMA],
        compiler_params=pltpu.CompilerParams(has_side_effects=True),
    )(*halves, small)


def _pack_small(vecs, ws, bs, sink, extra=None):
    ws = jnp.zeros((64, 1024), F32) if ws is None else ws.reshape(64, 1024)
    bs = jnp.zeros((1, 512), F32) if bs is None else bs.reshape(1, 512)
    sink = jnp.zeros((1, 16), F32) if sink is None else sink.reshape(1, 16)
    extra = jnp.zeros((1, 1024), F32) if extra is None else extra.reshape(1, 1024)
    top = jnp.concatenate(
        [v.reshape(1, 1024) for v in vecs]
        + [jnp.pad(bs, ((0, 0), (0, 512))), jnp.pad(sink, ((0, 0), (0, 1008))), extra], axis=0)
    return jnp.concatenate([top, ws, jnp.zeros((8, 1024), F32)], axis=0)


def _unpack_small(p):
    vecs = [p[k] for k in range(5)]
    return vecs, p[8:72].reshape(4, 128, 128), p[5, :512].reshape(4, 128), p[6, :16]


def kernel(x, norm_0, w_in_0, a_v_norm_0, a_spatial_w_0, a_spatial_b_0, b_group_w_0, b_scale_0, w_out_0, norm_1, w_in_1, sink_1, w_out_1, final_norm, loss_target, m_norm_0, m_w_in_0, m_a_v_norm_0, m_a_spatial_w_0, m_a_spatial_b_0, m_b_group_w_0, m_b_scale_0, m_w_out_0, m_norm_1, m_w_in_1, m_sink_1, m_w_out_1, m_final_norm, v_norm_0, v_w_in_0, v_a_v_norm_0, v_a_spatial_w_0, v_a_spatial_b_0, v_b_group_w_0, v_b_scale_0, v_w_out_0, v_norm_1, v_w_in_1, v_sink_1, v_w_out_1, v_final_norm):
    s = x.shape[1]
    xs = x.reshape(s, D_MODEL)
    target = loss_target.reshape(s, D_MODEL)

    cuts = [_Sharded("cols", (1024, 5120)), _Sharded("rows", (2048, 1024)), _Sharded("cols", (1024, 2560)),
            _Sharded("rows", (1024, 1024)), _Sharded("mid", (4, 256, 256))]
    big_w = [w_in_0, w_out_0, w_in_1, w_out_1, b_group_w_0]
    big_m = [m_w_in_0, m_w_out_0, m_w_in_1, m_w_out_1, m_b_group_w_0]
    big_v = [v_w_in_0, v_w_out_0, v_w_in_1, v_w_out_1, v_b_group_w_0]
    where = jnp.stack([2 * lax.axis_index("x") + lax.axis_index("y"), lax.axis_index("c")]).astype(jnp.int32)
    placed = [_place_shard(where, w, cut, f"place_shard{k}") for k, (w, cut) in enumerate(zip(big_w, cuts))]
    cx, cy = lax.axis_index("x"), lax.axis_index("y")
    order = jnp.stack([2 * cx + cy, 2 * (1 - cx) + cy, 2 * cx + 1 - cy, 2 * (1 - cx) + 1 - cy]).astype(jnp.int32)

    row = lambda v: v.reshape(1, 1024)
    ws16 = a_spatial_w_0.astype(BF16)
    bsx = jnp.repeat(a_spatial_b_0.T, 256, axis=1)
    band = _band_matrices(TOK)[0]
    rope = _rope_tables(s)

    h0, z0, w_in0 = _in_proj0_own(order, xs, row(norm_0), w_in_0.astype(BF16), _Gather(placed[:1], cuts[:1]))
    z0, w_out0, wg = _in_proj0_rest(order, h0, w_in0, z0, _Gather([placed[1], placed[4]], [cuts[1], cuts[4]]))
    cat, x1, w_in1, w_out1 = _mix0_fwd(xs, z0, w_out0, row(a_v_norm_0), ws16, bsx, wg, row(b_scale_0), band,
                                       _Gather(placed[2:4], cuts[2:4]))
    h1, q, k, v, gate = _in_proj1(x1, row(norm_1), w_in1, rope)
    kpad = jnp.pad(k, ((ATTN_WINDOW, ATTN_WINDOW), (0, 0)))
    vpad = jnp.pad(v, ((ATTN_WINDOW, ATTN_WINDOW), (0, 0)))
    o, lse = _attn_fwd(q, kpad, vpad, sink_1)
    y1, dx2, do, dgate, loss_lanes, g_final, dx2h = _tail(x1, o, gate, target, w_out1, row(final_norm))

    dq, dkpad, dvpad, dsink = _attn_bwd(q, kpad, vpad, sink_1, o, lse, do, rope)
    dz1, dx1, g_norm1, dx1h = _in_proj1_bwd(dq, dkpad, dvpad, dgate, x1, dx2, row(norm_1), w_in1, rope)

    g_w_in1 = _weight_grad(h1, dz1, 2, 2, "grad_w_in1").reshape(4, 2, 512, 640)
    g_w_out1 = _weight_grad(y1, dx2h, 1, 1, "grad_w_out1").reshape(4, 2, 128, 1024)
    g_w_out0 = _weight_grad(cat, dx1h, 1, 1, "grad_w_out0").reshape(4, 2, 256, 1024)
    first = [g_w_out0, g_w_in1, g_w_out1]
    zero = jnp.zeros((1024,), F32)
    small1 = _pack_small([zero, zero, zero, g_norm1, g_final], None, None, dsink[0, :16])[:8]
    dz0, dpn, d_ws, d_bs, d_gv, d_scale, d_wg, *theirs1, small1_all = _mix0_bwd(
        dx1h, z0, w_out0, row(a_v_norm_0), ws16, bsx, wg, row(b_scale_0), band, _Rider("exchange", first, small1))
    parts1 = [_add_sibling(where, g, t, f"add_sibling1_{k}") for k, (g, t) in enumerate(zip(first, theirs1))]
    dz0 = _fill_pooled_grad(dz0, dpn)
    g_w_in0, *slots1 = _weight_grad(h0, dz0, 4, 1, "grad_w_in0", _Rider("scatter", parts1))
    g_w_in0 = g_w_in0.reshape(4, 2, 512, 1280)
    g_wg = d_wg.reshape(2, 2, 4, 64, 256).transpose(2, 0, 1, 3, 4).reshape(4, 2, 128, 256)
    second = [g_w_in0, g_wg]
    theirs2 = _exchange_halves(second, "exchange_halves2")
    parts2 = [_add_sibling(where, g, t, f"add_sibling2_{k}") for k, (g, t) in enumerate(zip(second, theirs2))]
    small2 = _pack_small([zero, d_gv, d_scale, zero, zero], d_ws, d_bs[:, :4].T, None)
    grad_x, g_norm0, *slots2, small2_all = _in_proj0_bwd(
        dz0, xs, dx1, row(norm_0), w_in0, _Rider("scatter", parts2, small2))

    grads = [g_w_in0, g_w_out0, g_w_in1, g_w_out1, g_wg]
    theirs = [theirs2[0], theirs1[0], theirs1[1], theirs1[2], theirs2[1]]
    slots = [slots2[0], slots1[0], slots1[1], slots1[2], slots2[1]]
    n = len(grads)
    reduced = [_sum_chips(where, grads[w], theirs[w], slots[w], f"sum_chips{w}") for w in range(n)]
    small3 = _pack_small([g_norm0, zero, zero, zero, zero], None, None, None, loss_lanes)[:8]
    *from_sibling, small3_all = _share_halves(reduced, small3)

    out_g, out_d, out_m, out_v = {}, {}, {}, {}
    names = ["w_in_0", "w_out_0", "w_in_1", "w_out_1", "b_group_w_0"]
    for w in range(n):
        shape = big_w[w].shape
        two_d = (-1, shape[-1])
        outs = _adamw_halves(where, big_w[w].reshape(two_d), reduced[w], from_sibling[w], big_m[w].reshape(two_d),
                             big_v[w].reshape(two_d), f"adamw{w}")
        out_g[names[w]], out_d[names[w]], out_m[names[w]], out_v[names[w]] = (a.reshape(shape) for a in outs)

    g_small = _sum_small(small1_all, small2_all, small3_all)
    small_names = ["norm_0", "a_v_norm_0", "b_scale_0", "norm_1", "final_norm"]
    pack = lambda vecs, ws_, bs_, sk: _pack_small(vecs, ws_, bs_, sk)
    w_small = pack([norm_0, a_v_norm_0, b_scale_0, norm_1, final_norm], a_spatial_w_0, a_spatial_b_0, sink_1)
    m_small = pack([m_norm_0, m_a_v_norm_0, m_b_scale_0, m_norm_1, m_final_norm], m_a_spatial_w_0,
                   m_a_spatial_b_0, m_sink_1)
    v_small = pack([v_norm_0, v_a_v_norm_0, v_b_scale_0, v_norm_1, v_final_norm], v_a_spatial_w_0,
                   v_a_spatial_b_0, v_sink_1)
    d_small, nm_small, nv_small = _adamw(w_small, g_small, m_small, v_small, "adamw_small")
    for store, packed in ((out_g, g_small), (out_d, d_small), (out_m, nm_small), (out_v, nv_small)):
        vecs, ws_, bs_, sk = _unpack_small(packed)
        for name, vec in zip(small_names, vecs):
            store[name] = vec
        store["a_spatial_w_0"], store["a_spatial_b_0"], store["sink_1"] = ws_, bs_, sk

    loss = jnp.sum(g_small[7])
    order = ["norm_0", "w_in_0", "a_v_norm_0", "a_spatial_w_0", "a_spatial_b_0", "b_group_w_0", "b_scale_0",
             "w_out_0", "norm_1", "w_in_1", "sink_1", "w_out_1", "final_norm"]
    return (loss, grad_x.reshape(1, s, D_MODEL), *[out_g[k] for k in order], *[out_d[k] for k in order],
            *[out_m[k] for k in order], *[out_v[k] for k in order])
```

```python
import numpy as np
import jax
import jax.numpy as jnp
from jax import lax
from jax.experimental import pallas as pl
from jax.experimental.pallas import tpu as pltpu

F32 = jnp.float32
BF16 = jnp.bfloat16
MESH = pl.DeviceIdType.MESH

D_MODEL = 1024
EPS = 1e-6
NEG_INF = -1e30
CHUNK = 128
POOL_WINDOWS = (2, 4, 8, 16)
HALO = 16
N_HEADS = 16
HEAD_DIM = 64
ATTN_WINDOW = 128
ROPE_THETA = 500000.0
ROT_DIM = 16
ADAM_LR = 0.001
ADAM_B1 = 0.9
ADAM_B2 = 0.999
ADAM_EPS = 1e-08
ADAM_WD = 0.01
ADAM_STEP = 10

TOK = 256
VMEM_LIMIT = 56 * 1024 * 1024


def _params(**kw):
    return pltpu.CompilerParams(vmem_limit_bytes=VMEM_LIMIT, **kw)


def _whole(shape):
    nd = len(shape)
    return pl.BlockSpec(shape, lambda *_: (0,) * nd)


def _rows(t, n):
    return pl.BlockSpec((t, n), lambda i: (i, 0))


ANY = pl.BlockSpec(memory_space=pl.ANY)

_G0 = 0.7978845608028654
_G1 = 0.044715


def _gelu(x):
    return 0.5 * x * (1.0 + jnp.tanh(_G0 * (x + _G1 * x * x * x)))


def _gelu_and_grad(x):
    x2 = x * x
    t = jnp.tanh(_G0 * (x + _G1 * x2 * x))
    half = 0.5 * (1.0 + t)
    return x * half, half + 0.5 * x * (1.0 - t * t) * (_G0 * (1.0 + 3.0 * _G1 * x2))


def _sigmoid(x):
    return 1.0 / (1.0 + jnp.exp(-x))


def _dot(a, b):
    return jnp.dot(a, b, preferred_element_type=F32)


def _dot_nt(a, b):
    return lax.dot_general(a, b, (((1,), (1,)), ((), ())), preferred_element_type=F32)


def _dot_tn(a, b):
    return lax.dot_general(a, b, (((0,), (0,)), ((), ())), preferred_element_type=F32)


def _rms_fwd(x, g):
    r = lax.rsqrt(jnp.mean(x * x, axis=-1, keepdims=True) + EPS)
    xh = x * r
    return r, xh, xh * g


def _rms_bwd(dy, g, r, xh):
    dxh = dy * g
    return r * (dxh - xh * jnp.mean(dxh * xh, axis=-1, keepdims=True))


def _band_matrices(t):
    r = np.arange(t)[:, None]
    j = np.arange(t + 2 * HALO)[None, :]
    fwd, bwd = [], []
    for w in POOL_WINDOWS:
        d = j - r - HALO
        fwd.append((d >= -(w // 2)) & (d < w // 2))
        bwd.append((d >= -(w // 2) + 1) & (d <= w // 2))
    return (jnp.asarray(np.stack(fwd), BF16), jnp.asarray(np.stack(bwd), BF16))


def _window_counts(i, t, s):
    tok = i * t + lax.broadcasted_iota(jnp.int32, (t, 1), 0)
    out = []
    for w in POOL_WINDOWS:
        cnt = jnp.minimum(tok + w // 2, s) - jnp.maximum(tok - w // 2, 0)
        out.append(cnt.astype(F32))
    return out


def _rope_tables(s):
    inv = np.float32(ROPE_THETA) ** (-np.arange(0, ROT_DIM, 2, dtype=np.float32) / np.float32(ROT_DIM))
    ang = np.arange(s, dtype=np.float32)[:, None] * inv.astype(np.float32)[None, :]
    cos, sin = np.cos(ang).astype(np.float32), np.sin(ang).astype(np.float32)
    z8 = np.zeros((s, 8), np.float32)
    z48 = np.zeros((s, HEAD_DIM - ROT_DIM), np.float32)
    c = np.concatenate([cos, cos, np.ones((s, HEAD_DIM - ROT_DIM), np.float32)], axis=1)
    s_lo = np.concatenate([z8, sin, z48], axis=1)
    s_hi = np.concatenate([-sin, z8, z48], axis=1)
    return tuple(jnp.asarray(np.concatenate([a, a], axis=1)) for a in (c, s_lo, s_hi))


def _rope(x, c, s_lo, s_hi):
    n = x.shape[1]
    reps = n // 128
    c, s_lo, s_hi = (jnp.tile(a, (1, reps)) for a in (c, s_lo, s_hi))
    return x * c + pltpu.roll(x, 8, 1) * s_lo + pltpu.roll(x, n - 8, 1) * s_hi


def _rope_t(dx, c, s_lo, s_hi):
    n = dx.shape[1]
    reps = n // 128
    c, s_lo, s_hi = (jnp.tile(a, (1, reps)) for a in (c, s_lo, s_hi))
    return dx * c + pltpu.roll(dx * s_lo, n - 8, 1) + pltpu.roll(dx * s_hi, 8, 1)


def _in_proj0_own(order, x, g0, w_own, rider):
    s = x.shape[0]
    n = w_own.shape[1]

    def body(order_ref, x_ref, g_ref, w_ref, h_ref, z_ref):
        _, _, h = _rms_fwd(x_ref[...], g_ref[...])
        h = h.astype(BF16)
        h_ref[...] = h
        z_ref[...] = _dot(h, w_ref[...]).astype(BF16)

    return pl.pallas_call(
        rider.carried_by(body, 4, 2, s // TOK, late=True), name="in_proj0_own",
        grid_spec=pltpu.PrefetchScalarGridSpec(
            num_scalar_prefetch=1, grid=(s // TOK,),
            in_specs=[pl.BlockSpec((TOK, D_MODEL), lambda i, o: (i, 0)), pl.BlockSpec((1, D_MODEL), lambda i, o: (0, 0)),
                      pl.BlockSpec(w_own.shape, lambda i, o: (0, 0))] + rider.in_specs,
            out_specs=[pl.BlockSpec((TOK, D_MODEL), lambda i, o: (i, 0)),
                       pl.BlockSpec((TOK, n), lambda i, o: (i, o[0]))] + rider.out_specs,
            scratch_shapes=rider.scratch),
        out_shape=[jax.ShapeDtypeStruct((s, D_MODEL), BF16), jax.ShapeDtypeStruct((s, 4 * n), BF16)] + rider.out_shape,
        input_output_aliases={4 + j: 2 + j for j in range(len(rider.fulls))},
        compiler_params=_params(),
    )(order, x, g0, w_own, *rider.fulls)


def _in_proj0_rest(order, h0, w_in0, z0, rider):
    s = h0.shape[0]
    tok = min(s, 4 * TOK)
    n_tiles = s // tok
    n = w_in0.shape[1] // 4

    def body(order_ref, h_ref, w_ref, z_in_ref, z_ref):
        z_ref[...] = _dot(h_ref[...], w_ref[...]).astype(BF16)

    return pl.pallas_call(
        rider.carried_by(body, 4, 1, 3 * n_tiles, lambda: pl.program_id(0) * n_tiles + pl.program_id(1)),
        name="in_proj0_rest",
        grid_spec=pltpu.PrefetchScalarGridSpec(
            num_scalar_prefetch=1, grid=(3, n_tiles),
            in_specs=[pl.BlockSpec((tok, D_MODEL), lambda k, i, o: (i, 0)),
                      pl.BlockSpec((w_in0.shape[0], n), lambda k, i, o: (0, o[1 + k])), ANY] + rider.in_specs,
            out_specs=[pl.BlockSpec((tok, n), lambda k, i, o: (i, o[1 + k]))] + rider.out_specs,
            scratch_shapes=rider.scratch),
        out_shape=[jax.ShapeDtypeStruct(z0.shape, BF16)] + rider.out_shape,
        input_output_aliases={3: 0, **{4 + j: 1 + j for j in range(len(rider.fulls))}},
        compiler_params=_params(),
    )(order, h0, w_in0, z0, *rider.fulls)


def _halo_specs(s, col_block, tok=TOK):
    per = tok // HALO
    last = s // HALO - 1
    prev = pl.BlockSpec((HALO, 1024), lambda i: (jnp.maximum(i * per - 1, 0), col_block))
    nxt = pl.BlockSpec((HALO, 1024), lambda i: (jnp.minimum((i + 1) * per, last), col_block))
    return prev, nxt


def _with_halo(i, n_tiles, prev_ref, cur, next_ref):
    prev = prev_ref[...]
    nxt = next_ref[...]
    prev = jnp.where(i > 0, prev, jnp.zeros_like(prev))
    nxt = jnp.where(i < n_tiles - 1, nxt, jnp.zeros_like(nxt))
    return jnp.concatenate([prev, cur, nxt], axis=0)


def _mixer_a(v1, gv, ws_ref, bsx):
    rv, vh, v2 = _rms_fwd(v1, gv)
    v2 = v2.astype(BF16)
    rows = []
    for c in range(v1.shape[0] // CHUNK):
        cols = [_dot(ws_ref[h], v2[c * CHUNK:(c + 1) * CHUNK, h * 256:(h + 1) * 256]) for h in range(4)]
        rows.append(jnp.concatenate(cols, axis=1) + bsx)
    return rv, vh, v2, jnp.concatenate(rows, axis=0)


def _mixer_b_pooled(bx, halo, band_ref, counts):
    out = []
    for g in range(4):
        win = _dot(band_ref[g], halo[:, g * 256:(g + 1) * 256])
        out.append(win / counts[g] - bx[:, g * 256:(g + 1) * 256])
    return out


def _mix0_fwd(x, z0, w_out0, gv, ws, bsx, wg, scale, band, rider):
    s = x.shape[0]
    tok = min(s, 2 * TOK)
    n_tiles = s // tok
    k = len(rider.fulls)

    def body(z_ref, zp_ref, zn_ref, x_ref, wout_ref, gv_ref, ws_ref, bsx_ref, wg_ref, sc_ref, band_ref,
             cat_ref, x1_ref):
        i = pl.program_id(0)
        halves = [slice(r * TOK, (r + 1) * TOK) for r in range(tok // TOK)]
        mixeds = [_mixer_a(_gelu(z_ref[rows, 1024:2048].astype(F32)), gv_ref[...], ws_ref, bsx_ref[...])[3]
                  for rows in halves]
        x1s = []
        for rows, mixed in zip(halves, mixeds):
            au = z_ref[rows, 0:1024].astype(F32)
            ag = z_ref[rows, 2048:3072].astype(F32)
            ya = (_gelu(au) * mixed * (ag * _sigmoid(ag))).astype(BF16)
            cat_ref[rows, 0:1024] = ya
            x1s.append(x_ref[rows, :] + _dot(ya, wout_ref[0:1024, :]))

        bxs = [z_ref[rows, 3072:4096] for rows in halves]
        ybs = []
        for r, rows in enumerate(halves):
            prev = bxs[r - 1][TOK - HALO:] if r > 0 else jnp.where(i > 0, zp_ref[...], jnp.zeros_like(zp_ref))
            nxt = (bxs[r + 1][:HALO] if r + 1 < len(halves)
                   else jnp.where(i < n_tiles - 1, zn_ref[...], jnp.zeros_like(zn_ref)))
            halo = jnp.concatenate([prev, bxs[r], nxt], axis=0)
            ps = _mixer_b_pooled(bxs[r].astype(F32), halo, band_ref, _window_counts(i * len(halves) + r, TOK, s))
            pw = jnp.concatenate([_dot(ps[g].astype(BF16), wg_ref[g]) for g in range(4)], axis=1)
            bg = z_ref[rows, 4096:5120].astype(F32)
            ybs.append((pw * sc_ref[...] * (bg * _sigmoid(bg))).astype(BF16))
            cat_ref[rows, 1024:2048] = ybs[-1]
        for rows, x1, yb in zip(halves, x1s, ybs):
            x1_ref[rows, :] = x1 + _dot(yb, wout_ref[1024:2048, :])

    prev, nxt = _halo_specs(s, 3, tok)
    return pl.pallas_call(
        rider.carried_by(body, 11, 2, n_tiles), name="mix0_fwd", grid=(n_tiles,),
        in_specs=[_rows(tok, 5120), prev, nxt, _rows(tok, D_MODEL), _whole(w_out0.shape), _whole((1, 1024)),
                  _whole(ws.shape), _whole(bsx.shape), _whole(wg.shape), _whole((1, 1024)), _whole(band.shape)]
        + rider.in_specs,
        out_specs=[_rows(tok, 2048), _rows(tok, D_MODEL)] + rider.out_specs,
        out_shape=[jax.ShapeDtypeStruct((s, 2048), BF16), jax.ShapeDtypeStruct((s, D_MODEL), F32)] + rider.out_shape,
        input_output_aliases={11 + j: 2 + j for j in range(k)},
        scratch_shapes=rider.scratch,
        compiler_params=_params(),
    )(z0, z0, z0, x, w_out0, gv, ws, bsx, wg, scale, band, *rider.fulls)


def _in_proj1(x1, g1, w_in1, rope):
    s = x1.shape[0]

    def body(x_ref, g_ref, w_ref, c_ref, lo_ref, hi_ref, h_ref, q_ref, k_ref, v_ref, gate_ref):
        halves = [slice(r * TOK, (r + 1) * TOK) for r in range(tok // TOK)]
        hs = [_rms_fwd(x_ref[rows, :], g_ref[...])[2].astype(BF16) for rows in halves]
        for rows, h in zip(halves, hs):
            h_ref[rows, :] = h
        qs = [_dot(h, w_ref[:, 0:1024]) for h in hs]
        kvs = [_dot(h, w_ref[:, 1024:1536]) for h in hs]
        for rows, q, kv in zip(halves, qs, kvs):
            tabs = (c_ref[rows, :], lo_ref[rows, :], hi_ref[rows, :])
            q_ref[rows, :] = (_rope(q, *tabs) * Q_SCALE).astype(BF16)
            k_ref[rows, :] = _rope(kv[:, 0:256], *tabs).astype(BF16)
            v_ref[rows, :] = kv[:, 256:512].astype(BF16)
        for rows, h in zip(halves, hs):
            gate_ref[rows, :] = _dot(h, w_ref[:, 1536:2560]).astype(BF16)

    tok = min(s, 2 * TOK)
    tab = _rows(tok, 128)
    return pl.pallas_call(
        body, name="in_proj1", grid=(s // tok,),
        in_specs=[_rows(tok, D_MODEL), _whole((1, D_MODEL)), _whole(w_in1.shape), tab, tab, tab],
        out_specs=[_rows(tok, 1024), _rows(tok, 1024), _rows(tok, 256), _rows(tok, 256), _rows(tok, 1024)],
        out_shape=[jax.ShapeDtypeStruct((s, 1024), BF16), jax.ShapeDtypeStruct((s, 1024), BF16),
                   jax.ShapeDtypeStruct((s, 256), BF16), jax.ShapeDtypeStruct((s, 256), BF16),
                   jax.ShapeDtypeStruct((s, 1024), BF16)],
        compiler_params=_params(),
    )(x1, g1, w_in1, *rope)


QBLK = 128
KBLK = QBLK + 2 * ATTN_WINDOW
Q_SCALE = HEAD_DIM ** -0.5


def _block_bias(q0, s):
    r = lax.broadcasted_iota(jnp.int32, (QBLK, KBLK), 0)
    c = lax.broadcasted_iota(jnp.int32, (QBLK, KBLK), 1)
    kj = q0 - ATTN_WINDOW + c
    ok = (c >= r) & (c <= r + 2 * ATTN_WINDOW) & (kj >= 0) & (kj < s)
    return jnp.where(ok, 0.0, NEG_INF)


def _pair_operands(t):
    lane = lax.broadcasted_iota(jnp.int32, (1, 128), 1)
    first = lane < HEAD_DIM
    zero = jnp.zeros((KBLK, 128), BF16)
    out = []
    for j in range(2):
        slab = t[:, 128 * j:128 * (j + 1)]
        turned = pltpu.bitcast(pltpu.roll(pltpu.bitcast(slab, jnp.uint32), HEAD_DIM, 1), BF16)
        for own_first in (True, False):
            top = jnp.where(first, slab if own_first else turned, zero)
            bottom = jnp.where(first, zero, turned if own_first else slab)
            out.append(jnp.concatenate([top, bottom], axis=0))
    return out


def _attn_fwd(q, kpad, vpad, sink):
    s = q.shape[0]

    def body(sink_ref, q_ref, k_ref, v_ref, o_ref, lse_ref):
        i = pl.program_id(0)
        lane = lax.broadcasted_iota(jnp.int32, (1, 128), 1)
        for b in range(TOK // QBLK):
            rows = slice(b * QBLK, (b + 1) * QBLK)
            start = pl.multiple_of(i * TOK + b * QBLK, QBLK)
            k_bd = _pair_operands(k_ref[pl.ds(start, KBLK), :])
            v_bd = _pair_operands(v_ref[pl.ds(start, KBLK), :])
            bias = _block_bias(i * TOK + b * QBLK, s)
            pairs = range(N_HEADS // 2)
            sc4 = [_dot_nt(jnp.concatenate([q_ref[rows, 256 * g:256 * g + 128], q_ref[rows, 256 * g + 128:256 * (g + 1)]],
                                           axis=0), k_bd[g]) for g in range(4)]
            sc2 = [sc4[m // 2][(m % 2) * QBLK:(m % 2 + 1) * QBLK] for m in pairs]
            scs = [sc2[h // 2][:, (h % 2) * KBLK:(h % 2 + 1) * KBLK] + bias for h in range(N_HEADS)]
            ms = [jnp.maximum(jnp.max(scs[h], axis=-1, keepdims=True), sink_ref[h]) for h in range(N_HEADS)]
            es = [jnp.exp(scs[h] - ms[h]) for h in range(N_HEADS)]
            dens = [jnp.sum(es[h], axis=-1, keepdims=True) + jnp.exp(sink_ref[h] - ms[h]) for h in range(N_HEADS)]
            first = lane < HEAD_DIM
            e2 = [jnp.concatenate([es[2 * m].astype(BF16), es[2 * m + 1].astype(BF16)], axis=1) for m in pairs]
            o4 = [_dot(jnp.concatenate([e2[2 * g], e2[2 * g + 1]], axis=0), v_bd[g]) for g in range(4)]
            outs = [o4[m // 2][(m % 2) * QBLK:(m % 2 + 1) * QBLK]
                    * jnp.where(first, 1.0 / dens[2 * m], 1.0 / dens[2 * m + 1]) for m in pairs]
            o_ref[rows, :] = jnp.concatenate(outs, axis=1).astype(BF16)
            lse = jnp.zeros((QBLK, 128), F32)
            for h in range(N_HEADS):
                lse = lse + jnp.where(lane == h, ms[h] + jnp.log(dens[h]), 0.0)
            lse_ref[rows, :] = lse

    return pl.pallas_call(
        body, name="attn_fwd", grid=(s // TOK,),
        in_specs=[pl.BlockSpec(memory_space=pltpu.SMEM), _rows(TOK, 1024), _whole(kpad.shape), _whole(vpad.shape)],
        out_specs=[_rows(TOK, 1024), _rows(TOK, 128)],
        out_shape=[jax.ShapeDtypeStruct((s, 1024), BF16), jax.ShapeDtypeStruct((s, 128), F32)],
        compiler_params=_params(),
    )(sink, q, kpad, vpad)


def _tail(x1, o, gate, target, w_out1, gf):
    s = x1.shape[0]

    def body(x1_ref, o_ref, gate_ref, t_ref, w_ref, gf_ref, y1_ref, dx2_ref, do_ref, dgate_ref, loss_ref, gfn_ref,
             dx2h_ref):
        i = pl.program_id(0)

        @pl.when(i == 0)
        def _():
            loss_ref[...] = jnp.zeros_like(loss_ref)
            gfn_ref[...] = jnp.zeros_like(gfn_ref)

        halves = [slice(r * TOK, (r + 1) * TOK) for r in range(tok // TOK)]
        gf = gf_ref[...]
        gs = [gate_ref[rows, :].astype(F32) for rows in halves]
        sgs = [_sigmoid(g) for g in gs]
        sils = [g * sg for g, sg in zip(gs, sgs)]
        os_ = [o_ref[rows, :].astype(F32) for rows in halves]
        y1s = [(o * sil).astype(BF16) for o, sil in zip(os_, sils)]
        for rows, y1 in zip(halves, y1s):
            y1_ref[rows, :] = y1
        x2s = [x1_ref[rows, :] + _dot(y1, w_ref[...]) for rows, y1 in zip(halves, y1s)]
        dx2hs = []
        for rows, x2 in zip(halves, x2s):
            r, xh, out = _rms_fwd(x2, gf)
            diff = out - t_ref[rows, :]
            loss_ref[...] += jnp.sum(diff * diff, axis=0, keepdims=True) * (0.5 / D_MODEL)
            dout = diff * (1.0 / D_MODEL)
            gfn_ref[...] += jnp.sum(dout * xh, axis=0, keepdims=True)
            dx2 = _rms_bwd(dout, gf, r, xh)
            dx2_ref[rows, :] = dx2
            dx2hs.append(dx2.astype(BF16))
            dx2h_ref[rows, :] = dx2hs[-1]
        dy1s = [_dot_nt(dx2h, w_ref[...]) for dx2h in dx2hs]
        for rows, dy1, sil, o, sg, g in zip(halves, dy1s, sils, os_, sgs, gs):
            do_ref[rows, :] = (dy1 * sil).astype(BF16)
            dgate_ref[rows, :] = (dy1 * o * (sg * (1.0 + g * (1.0 - sg)))).astype(BF16)

    tok = min(s, 2 * TOK)
    row = _rows(tok, 1024)
    acc = _whole((1, 1024))
    return pl.pallas_call(
        body, name="tail", grid=(s // tok,),
        in_specs=[row, row, row, row, _whole(w_out1.shape), acc],
        out_specs=[row, row, row, row, acc, acc, row],
        out_shape=[jax.ShapeDtypeStruct((s, 1024), BF16), jax.ShapeDtypeStruct((s, 1024), F32),
                   jax.ShapeDtypeStruct((s, 1024), BF16), jax.ShapeDtypeStruct((s, 1024), BF16),
                   jax.ShapeDtypeStruct((1, 1024), F32), jax.ShapeDtypeStruct((1, 1024), F32),
                   jax.ShapeDtypeStruct((s, 1024), BF16)],
        compiler_params=_params(),
    )(x1, o, gate, target, w_out1, gf)


def _attn_bwd(q, kpad, vpad, sink, o, lse, do, rope):
    s = q.shape[0]
    pad_t = (kpad.shape[1], kpad.shape[0])

    def body(sink_ref, q_ref, k_ref, v_ref, o_ref, lse_ref, do_ref, c_ref, lo_ref, hi_ref,
             dq_ref, dk_ref, dv_ref, ds_ref):
        i = pl.program_id(0)

        @pl.when(i == 0)
        def _():
            dk_ref[...] = jnp.zeros_like(dk_ref)
            dv_ref[...] = jnp.zeros_like(dv_ref)
            ds_ref[...] = jnp.zeros_like(ds_ref)

        lane = lax.broadcasted_iota(jnp.int32, (1, 128), 1)
        dsink = jnp.zeros((1, 128), F32)
        for b in range(TOK // QBLK):
            rows = slice(b * QBLK, (b + 1) * QBLK)
            start = pl.multiple_of(i * TOK + b * QBLK, QBLK)
            k_bd = _pair_operands(k_ref[pl.ds(start, KBLK), :])
            v_bd = _pair_operands(v_ref[pl.ds(start, KBLK), :])
            bias = _block_bias(i * TOK + b * QBLK, s)
            lanes_of = (lane < HEAD_DIM, lane >= HEAD_DIM)
            half = lambda t, j: t[:, j * KBLK:(j + 1) * KBLK]
            dqs, dks, dvs = [], [], []
            for g in range(4):
                pairs = (2 * g, 2 * g + 1)
                qs = {m: q_ref[rows, 128 * m:128 * (m + 1)] for m in pairs}
                dos = {m: do_ref[rows, 128 * m:128 * (m + 1)] for m in pairs}
                lses = {h: lse_ref[rows, h:h + 1] for h in range(4 * g, 4 * g + 4)}
                stacked = lambda parts: jnp.concatenate([parts[m] for m in pairs], axis=0)
                unstack = lambda t: {m: t[j * QBLK:(j + 1) * QBLK] for j, m in enumerate(pairs)}
                sc2 = unstack(_dot_nt(stacked(qs), k_bd[g]))
                ps = {2 * m + j: jnp.exp(half(sc2[m], j) + bias - lses[2 * m + j]) for m in pairs for j in range(2)}
                prods = {m: dos[m].astype(F32) * o_ref[rows, 128 * m:128 * (m + 1)].astype(F32) for m in pairs}
                deltas = {2 * m + j: jnp.sum(jnp.where(lanes_of[j], prods[m], 0.0), axis=-1, keepdims=True)
                          for m in pairs for j in range(2)}
                for h in range(4 * g, 4 * g + 4):
                    dsink = dsink + jnp.where(
                        lane == h, -jnp.sum(jnp.exp(sink_ref[h] - lses[h]) * deltas[h], axis=0, keepdims=True), 0.0)
                dp2 = unstack(_dot_nt(stacked(dos), v_bd[g]))
                ds2 = {m: jnp.concatenate(
                    [(ps[2 * m + j] * (half(dp2[m], j) - deltas[2 * m + j])).astype(BF16) for j in range(2)], axis=1)
                    for m in pairs}
                p2 = {m: jnp.concatenate([ps[2 * m].astype(BF16), ps[2 * m + 1].astype(BF16)], axis=1) for m in pairs}
                diag = lambda t: t[0:HEAD_DIM, 0:KBLK] + t[HEAD_DIM:128, KBLK:2 * KBLK]
                dvs.append(diag(_dot_tn(stacked(dos), stacked(p2))))
                dks.append(diag(_dot_tn(stacked(qs), stacked(ds2))))
                dq2 = unstack(_dot(stacked(ds2), k_bd[g]) * Q_SCALE)
                dqs += [dq2[m] for m in pairs]
            dq = jnp.concatenate(dqs, axis=1)
            dq_ref[rows, :] = _rope_t(dq, c_ref[rows, :], lo_ref[rows, :], hi_ref[rows, :]).astype(BF16)
            dk_ref[:, pl.ds(start, KBLK)] += jnp.concatenate(dks, axis=0)
            dv_ref[:, pl.ds(start, KBLK)] += jnp.concatenate(dvs, axis=0)
        ds_ref[...] += dsink

    row = _rows(TOK, 1024)
    tab = _rows(TOK, 128)
    pad = _whole(kpad.shape)
    return pl.pallas_call(
        body, name="attn_bwd", grid=(s // TOK,),
        in_specs=[pl.BlockSpec(memory_space=pltpu.SMEM), row, pad, pad, row, tab, row, tab, tab, tab],
        out_specs=[row, _whole(pad_t), _whole(pad_t), _whole((1, 128))],
        out_shape=[jax.ShapeDtypeStruct((s, 1024), BF16), jax.ShapeDtypeStruct(pad_t, F32),
                   jax.ShapeDtypeStruct(pad_t, F32), jax.ShapeDtypeStruct((1, 128), F32)],
        compiler_params=_params(),
    )(sink, q, kpad, vpad, o, lse, do, *rope)


def _in_proj1_bwd(dq, dk_t, dv_t, dgate, x1, dx2, g1, w_in1, rope):
    s = x1.shape[0]
    tok = min(s, 2 * TOK)
    n_sub = tok // ATTN_WINDOW

    def body(*refs):
        dq_ref = refs[0]
        dk_refs, dv_refs = refs[1:1 + n_sub], refs[1 + n_sub:1 + 2 * n_sub]
        (dgate_ref, x1_ref, dx2_ref, g_ref, w_ref, c_ref, lo_ref, hi_ref,
         dz_ref, dx1_ref, gn_ref, dx1h_ref) = refs[1 + 2 * n_sub:]

        @pl.when(pl.program_id(0) == 0)
        def _():
            gn_ref[...] = jnp.zeros_like(gn_ref)

        halves = [slice(r * TOK, (r + 1) * TOK) for r in range(tok // TOK)]
        per = TOK // ATTN_WINDOW
        g = g_ref[...]
        for r, rows in enumerate(halves):
            dk = jnp.concatenate([ref[...] for ref in dk_refs[r * per:(r + 1) * per]], axis=1).T
            dv = jnp.concatenate([ref[...] for ref in dv_refs[r * per:(r + 1) * per]], axis=1).T
            dz_ref[rows, 0:1024] = dq_ref[rows, :]
            dz_ref[rows, 1024:1280] = _rope_t(dk, c_ref[rows, :], lo_ref[rows, :], hi_ref[rows, :]).astype(BF16)
            dz_ref[rows, 1280:1536] = dv.astype(BF16)
            dz_ref[rows, 1536:2560] = dgate_ref[rows, :]
        dhs = [_dot_nt(dz_ref[rows, :], w_ref[...]) for rows in halves]
        for rows, dh in zip(halves, dhs):
            r, xh, _ = _rms_fwd(x1_ref[rows, :], g)
            gn_ref[...] += jnp.sum(dh * xh, axis=0, keepdims=True)
            dx1 = dx2_ref[rows, :] + _rms_bwd(dh, g, r, xh)
            dx1_ref[rows, :] = dx1
            dx1h_ref[rows, :] = dx1.astype(BF16)

    row = _rows(tok, 1024)
    subs = [pl.BlockSpec((256, ATTN_WINDOW), lambda i, j=j: (0, n_sub * i + 1 + j)) for j in range(n_sub)]
    tab = _rows(tok, 128)
    acc = _whole((1, 1024))
    return pl.pallas_call(
        body, name="in_proj1_bwd", grid=(s // tok,),
        in_specs=[row] + subs + subs + [row, row, row, acc, _whole(w_in1.shape), tab, tab, tab],
        out_specs=[_rows(tok, 2560), row, acc, row],
        out_shape=[jax.ShapeDtypeStruct((s, 2560), BF16), jax.ShapeDtypeStruct((s, 1024), F32),
                   jax.ShapeDtypeStruct((1, 1024), F32), jax.ShapeDtypeStruct((s, 1024), BF16)],
        compiler_params=_params(),
    )(dq, *[dk_t] * n_sub, *[dv_t] * n_sub, dgate, x1, dx2, g1, w_in1, *rope)


def _mix0_bwd(dx1, z0, w_out0, gv, ws, bsx, wg, scale, band, rider):
    s = dx1.shape[0]
    n_tiles = s // TOK

    def body(dx1h_ref, z_ref, zp_ref, zn_ref, wout_ref, gv_ref, ws_ref, bsx_ref, wg_ref, sc_ref, band_ref,
             dz_ref, dpn_ref, dws_ref, dbs_ref, dgv_ref, dsc_ref, dwg_ref):
        i = pl.program_id(0)
        dz_ref[:, 3072:4096] = jnp.zeros((TOK, 1024), BF16)

        @pl.when(i == 0)
        def _():
            for ref in (dws_ref, dbs_ref, dgv_ref, dsc_ref, dwg_ref):
                ref[...] = jnp.zeros_like(ref)

        dcat = _dot_nt(dx1h_ref[...], wout_ref[...])
        dya = dcat[:, 0:1024]
        dyb = dcat[:, 1024:2048]

        au = z_ref[:, 0:1024].astype(F32)
        av = z_ref[:, 1024:2048].astype(F32)
        ag = z_ref[:, 2048:3072].astype(F32)
        gv = gv_ref[...]
        u, du = _gelu_and_grad(au)
        v1, dv1 = _gelu_and_grad(av)
        rv, vh, v2, mixed = _mixer_a(v1, gv, ws_ref, bsx_ref[...])
        sg = _sigmoid(ag)
        sil = ag * sg
        dz_ref[:, 2048:3072] = (dya * u * mixed * (sg * (1.0 + ag * (1.0 - sg)))).astype(BF16)
        dz_ref[:, 0:1024] = (dya * mixed * sil * du).astype(BF16)
        dmixed = dya * u * sil
        lane = lax.broadcasted_iota(jnp.int32, (1, 128), 1)
        dm16 = dmixed.astype(BF16)
        dv2_rows = []
        for c in range(TOK // CHUNK):
            rows = slice(c * CHUNK, (c + 1) * CHUNK)
            cols_out = []
            for h in range(4):
                cols = slice(h * 256, (h + 1) * 256)
                dws_ref[h] += _dot_nt(dm16[rows, cols], v2[rows, cols])
                dbs_ref[...] += jnp.where(lane == h, jnp.sum(dmixed[rows, cols], axis=-1, keepdims=True), 0.0)
                cols_out.append(_dot_tn(ws_ref[h], dm16[rows, cols]))
            dv2_rows.append(jnp.concatenate(cols_out, axis=1))
        dv2 = jnp.concatenate(dv2_rows, axis=0)
        dgv_ref[...] += jnp.sum(dv2 * vh, axis=0, keepdims=True)
        dz_ref[:, 1024:2048] = (_rms_bwd(dv2, gv, rv, vh) * dv1).astype(BF16)

        bx = z_ref[:, 3072:4096]
        bg = z_ref[:, 4096:5120].astype(F32)
        counts = _window_counts(i, TOK, s)
        halo = _with_halo(i, n_tiles, zp_ref, bx, zn_ref)
        ps = [p.astype(BF16) for p in _mixer_b_pooled(bx.astype(F32), halo, band_ref, counts)]
        pw = jnp.concatenate([_dot(ps[g], wg_ref[g]) for g in range(4)], axis=1)
        sgb = _sigmoid(bg)
        sc = sc_ref[...]
        dz_ref[:, 4096:5120] = (dyb * pw * sc * (sgb * (1.0 + bg * (1.0 - sgb)))).astype(BF16)
        dys = dyb * (bg * sgb)
        dsc_ref[...] += jnp.sum(dys * pw, axis=0, keepdims=True)
        dpw = (dys * sc).astype(BF16)
        for g in range(4):
            cols = slice(g * 256, (g + 1) * 256)
            dwg_ref[g] += _dot_tn(ps[g], dpw[:, cols])
            dpn_ref[:, cols] = (_dot_nt(dpw[:, cols], wg_ref[g]) / counts[g]).astype(BF16)

    prev, nxt = _halo_specs(s, 3)
    row = _rows(TOK, 1024)
    vec = _whole((1, 1024))
    return pl.pallas_call(
        rider.carried_by(body, 11, 7, n_tiles), name="mix0_bwd", grid=(n_tiles,),
        in_specs=[row, _rows(TOK, 5120), prev, nxt, _whole(w_out0.shape), vec, _whole(ws.shape),
                  _whole(bsx.shape), _whole(wg.shape), vec, _whole(band.shape)] + rider.in_specs,
        out_specs=[_rows(TOK, 5120), row, _whole((4, 128, 128)), _whole((128, 128)), vec, vec,
                   _whole((4, 256, 256))] + rider.out_specs,
        out_shape=[jax.ShapeDtypeStruct((s, 5120), BF16), jax.ShapeDtypeStruct((s, 1024), BF16),
                   jax.ShapeDtypeStruct((4, 128, 128), F32), jax.ShapeDtypeStruct((128, 128), F32),
                   jax.ShapeDtypeStruct((1, 1024), F32), jax.ShapeDtypeStruct((1, 1024), F32),
                   jax.ShapeDtypeStruct((4, 256, 256), F32)] + rider.out_shape,
        scratch_shapes=rider.scratch,
        compiler_params=_params(),
    )(dx1, z0, z0, z0, w_out0, gv, ws, bsx, wg, scale, band, *rider.parts)


def _fill_pooled_grad(dz0, dpn):
    s = dpn.shape[0]
    tok = min(s, 2 * TOK)
    n_tiles = s // tok
    band_t = _band_matrices(tok)[1]

    def body(dz_in_ref, dpn_ref, dpp_ref, dpx_ref, band_ref, dbx_ref):
        i = pl.program_id(0)
        dpn = dpn_ref[...]
        halo = _with_halo(i, n_tiles, dpp_ref, dpn, dpx_ref)
        counts = _window_counts(i, tok, s)
        for g in range(4):
            cols = slice(g * 256, (g + 1) * 256)
            dbx = _dot(band_ref[g], halo[:, cols]) - dpn[:, cols].astype(F32) * counts[g]
            dbx_ref[:, cols] = dbx.astype(BF16)

    prev, nxt = _halo_specs(s, 0, tok)
    return pl.pallas_call(
        body, name="fill_pooled_grad", grid=(n_tiles,),
        in_specs=[ANY, _rows(tok, 1024), prev, nxt, _whole(band_t.shape)],
        out_specs=pl.BlockSpec((tok, 1024), lambda i: (i, 3)),
        out_shape=jax.ShapeDtypeStruct(dz0.shape, BF16),
        input_output_aliases={0: 0},
        compiler_params=_params(),
    )(dz0, dpn, dpn, dpn, band_t)


def _in_proj0_bwd(dz0, x, dx1, g0, w_in0, rider):
    s = x.shape[0]
    tok = min(s, 2 * TOK)
    n_tiles = s // tok

    def body(dz_ref, x_ref, dx1_ref, g_ref, w_ref, dx_ref, gn_ref):
        i = pl.program_id(0)

        @pl.when(i == 0)
        def _():
            gn_ref[...] = jnp.zeros_like(gn_ref)

        halves = [slice(r * TOK, (r + 1) * TOK) for r in range(tok // TOK)]
        g0v = g_ref[...]
        dhs = [_dot_nt(dz_ref[rows, :], w_ref[...]) for rows in halves]
        for rows, dh in zip(halves, dhs):
            r, xh, _ = _rms_fwd(x_ref[rows, :], g0v)
            gn_ref[...] += jnp.sum(dh * xh, axis=0, keepdims=True)
            dx_ref[rows, :] = dx1_ref[rows, :] + _rms_bwd(dh, g0v, r, xh)

    row = _rows(tok, 1024)
    vec = _whole((1, 1024))
    return pl.pallas_call(
        rider.carried_by(body, 5, 2, n_tiles), name="in_proj0_bwd", grid=(n_tiles,),
        in_specs=[_rows(tok, 5120), row, row, vec, _whole(w_in0.shape)] + rider.in_specs,
        out_specs=[row, vec] + rider.out_specs,
        out_shape=[jax.ShapeDtypeStruct((s, 1024), F32), jax.ShapeDtypeStruct((1, 1024), F32)] + rider.out_shape,
        scratch_shapes=rider.scratch,
        compiler_params=_params(),
    )(dz0, x, dx1, g0, w_in0, *rider.parts)


def _weight_grad(a, b, n_blocks, split, name, rider=None):
    s, k = a.shape
    n = b.shape[1]
    tn = n // n_blocks
    w = tn // split
    ts = min(s, 2048)

    def body(a_ref, b_ref, o_ref):
        @pl.when(pl.program_id(1) == 0)
        def _():
            o_ref[...] = jnp.zeros_like(o_ref)

        res = _dot_tn(a_ref[...], b_ref[...])
        for q in range(split):
            o_ref[q] += res[:, q * w:(q + 1) * w]

    in_specs = [pl.BlockSpec((ts, k), lambda j, t: (t, 0)), pl.BlockSpec((ts, tn), lambda j, t: (t, j))]
    out_spec = pl.BlockSpec((split, k, w), lambda j, t: (j, 0, 0))
    out_shape = jax.ShapeDtypeStruct((n_blocks * split, k, w), F32)
    if rider is None:
        return pl.pallas_call(body, name=name, grid=(n_blocks, s // ts), in_specs=in_specs, out_specs=out_spec,
                              out_shape=out_shape, compiler_params=_params())(a, b)
    steps = s // ts
    return pl.pallas_call(
        rider.carried_by(body, 2, 1, n_blocks * steps, lambda: pl.program_id(0) * steps + pl.program_id(1)),
        name=name, grid=(n_blocks, steps), in_specs=in_specs + rider.in_specs,
        out_specs=[out_spec] + rider.out_specs, out_shape=[out_shape] + rider.out_shape,
        scratch_shapes=rider.scratch, compiler_params=_params(),
    )(a, b, *rider.parts)


def _row_tile(rows, cols):
    t = rows
    while t * cols * 4 > (1 << 21) and t % 32 == 0:
        t //= 2
    return t


def _add_sibling(where, g, theirs, name):
    _, _, rows, cols = g.shape
    t = _row_tile(rows, cols)

    def body(where_ref, g_ref, t_ref, o_ref):
        o_ref[...] = (g_ref[...] + t_ref[...]).astype(BF16)

    spec = pl.BlockSpec((None, t, cols), lambda s, i, p: (s, i, 0))
    return pl.pallas_call(
        body, name=name, out_shape=jax.ShapeDtypeStruct((4, rows, cols), BF16),
        grid_spec=pltpu.PrefetchScalarGridSpec(
            num_scalar_prefetch=1, grid=(4, rows // t),
            in_specs=[pl.BlockSpec((None, None, t, cols), lambda s, i, p: (s, p[1], i, 0)), spec], out_specs=spec),
        compiler_params=_params())(where, g, theirs)


def _sum_chips(where, g, theirs, slots, name):
    _, _, rows, cols = g.shape
    t = _row_tile(rows, cols)

    def body(where_ref, g_ref, t_ref, s_ref, o_ref):
        me = where_ref[0]
        own = g_ref[...] + t_ref[...]
        acc = jnp.where(me == 0, own, s_ref[0].astype(F32))
        for k in range(1, 4):
            acc = acc + jnp.where(me == k, own, s_ref[k].astype(F32))
        o_ref[...] = acc

    return pl.pallas_call(
        body, name=name, out_shape=jax.ShapeDtypeStruct((rows, cols), F32),
        grid_spec=pltpu.PrefetchScalarGridSpec(
            num_scalar_prefetch=1, grid=(rows // t,),
            in_specs=[pl.BlockSpec((None, None, t, cols), lambda i, p: (p[0], p[1], i, 0)),
                      pl.BlockSpec((None, t, cols), lambda i, p: (p[0], i, 0)),
                      pl.BlockSpec((4, t, cols), lambda i, p: (0, i, 0))],
            out_specs=pl.BlockSpec((t, cols), lambda i, p: (i, 0))),
        compiler_params=_params())(where, g, theirs, slots)


def _adamw_halves(where, w, own, theirs, m, v, name):
    rows, cols = own.shape
    t = _row_tile(rows, cols)
    per = rows // t

    def body(where_ref, w_ref, own_ref, th_ref, m_ref, v_ref, g_ref, d_ref, nm_ref, nv_ref):
        g = jnp.where(pl.program_id(0) == where_ref[1], own_ref[...], th_ref[...])
        g_ref[...] = g
        m2 = ADAM_B1 * m_ref[...] + (1.0 - ADAM_B1) * g
        v2 = ADAM_B2 * v_ref[...] + (1.0 - ADAM_B2) * (g * g)
        m_hat = m2 / (1.0 - ADAM_B1 ** ADAM_STEP)
        v_hat = v2 / (1.0 - ADAM_B2 ** ADAM_STEP)
        d_ref[...] = -ADAM_LR * (m_hat / (jnp.sqrt(v_hat) + ADAM_EPS) + ADAM_WD * w_ref[...])
        nm_ref[...] = m2
        nv_ref[...] = v2

    full = pl.BlockSpec((t, cols), lambda h, i, p: (h * per + i, 0))
    half = pl.BlockSpec((t, cols), lambda h, i, p: (i, 0))
    shp = jax.ShapeDtypeStruct(w.shape, F32)
    return pl.pallas_call(
        body, name=name, out_shape=[shp] * 4,
        grid_spec=pltpu.PrefetchScalarGridSpec(
            num_scalar_prefetch=1, grid=(2, per), in_specs=[full, half, half, full, full], out_specs=[full] * 4),
        compiler_params=_params())(where, w, own, theirs, m, v)


def _place_shard(where, w, cut, name):
    if cut.kind == "cols":
        r, n = cut.full_shape
        blk, grid = (256, n // 4), (r // 256,)
        src_map, dst_map = (lambda i, p: (i, 0)), (lambda i, p: (i, p[0]))
    elif cut.kind == "rows":
        r, n = cut.full_shape
        per = r // 4 // 256
        blk, grid = (256, n), (per,)
        src_map, dst_map = (lambda i, p: (i, 0)), (lambda i, p: (p[0] * per + i, 0))
    else:
        g, r, n = cut.full_shape
        blk, grid = (g, r // 4, n), (1,)
        src_map, dst_map = (lambda i, p: (0, 0, 0)), (lambda i, p: (0, p[0], 0))

    def body(where_ref, w_ref, o_ref):
        o_ref[...] = w_ref[...].astype(BF16)

    return pl.pallas_call(
        body, name=name, out_shape=jax.ShapeDtypeStruct(cut.full_shape, BF16),
        grid_spec=pltpu.PrefetchScalarGridSpec(
            num_scalar_prefetch=1, grid=grid, in_specs=[pl.BlockSpec(blk, src_map)],
            out_specs=pl.BlockSpec(blk, dst_map)),
        compiler_params=_params())(where, w)


def _sum_small(first, second, third, handed_on):
    rows = second.shape[1]

    def body(a_ref, b_ref, c_ref, handed_ref, o_ref, handed_out_ref):
        del handed_ref, handed_out_ref
        top = a_ref[0] + c_ref[0] + b_ref[0, 0:8]
        rest = b_ref[0, 8:rows]
        for k in range(1, 8):
            top = top + (a_ref[k] + c_ref[k] + b_ref[k, 0:8])
            rest = rest + b_ref[k, 8:rows]
        o_ref[0:8] = top
        o_ref[8:rows] = rest

    return pl.pallas_call(
        body, name="sum_small", in_specs=[_whole(first.shape), _whole(second.shape), _whole(third.shape), ANY],
        out_specs=[_whole(second.shape[1:]), ANY],
        out_shape=[jax.ShapeDtypeStruct(second.shape[1:], F32), jax.ShapeDtypeStruct(handed_on.shape, handed_on.dtype)],
        input_output_aliases={3: 1},
        compiler_params=_params())(first, second, third, handed_on)


def _adamw(w, g, m, v, name):
    rows, cols = w.shape
    t = _row_tile(rows, cols)

    def body(w_ref, g_ref, m_ref, v_ref, d_ref, nm_ref, nv_ref):
        g = g_ref[...]
        m2 = ADAM_B1 * m_ref[...] + (1.0 - ADAM_B1) * g
        v2 = ADAM_B2 * v_ref[...] + (1.0 - ADAM_B2) * (g * g)
        m_hat = m2 / (1.0 - ADAM_B1 ** ADAM_STEP)
        v_hat = v2 / (1.0 - ADAM_B2 ** ADAM_STEP)
        d_ref[...] = -ADAM_LR * (m_hat / (jnp.sqrt(v_hat) + ADAM_EPS) + ADAM_WD * w_ref[...])
        nm_ref[...] = m2
        nv_ref[...] = v2

    spec = pl.BlockSpec((t, cols), lambda i: (i, 0))
    shp = jax.ShapeDtypeStruct(w.shape, F32)
    return pl.pallas_call(body, name=name, grid=(rows // t,), in_specs=[spec] * 4, out_specs=[spec] * 3,
                          out_shape=[shp] * 3, compiler_params=_params())(w, g, m, v)


def _place():
    x, y, c = lax.axis_index("x"), lax.axis_index("y"), lax.axis_index("c")
    chips = [(1 - x, y), (x, 1 - y), (1 - x, 1 - y)]
    return x, y, c, chips


class _Sharded:
    def __init__(self, kind, full_shape):
        self.kind = kind
        self.full_shape = full_shape

    def in_full(self, ref, s, h):
        if self.kind == "cols":
            r, n = self.full_shape
            return ref.at[pl.ds(h * (r // 2), r // 2), pl.ds(pl.multiple_of(s * (n // 4), 128), n // 4)]
        if self.kind == "rows":
            r, _ = self.full_shape
            return ref.at[pl.ds(pl.multiple_of(s * (r // 4) + h * (r // 8), 8), r // 8), :]
        g, r, _ = self.full_shape
        return ref.at[pl.ds(h * (g // 2), g // 2), pl.ds(pl.multiple_of(s * (r // 4), 16), r // 4), :]


def _remote(src, dst, send_sem, recv_sem, to):
    return pltpu.make_async_remote_copy(src_ref=src, dst_ref=dst, send_sem=send_sem, recv_sem=recv_sem,
                                        device_id=to, device_id_type=MESH)


def _start_remote(src, dst, send_sem, recv_sem, to):
    cp = _remote(src, dst, send_sem, recv_sem, to)
    cp.start()
    return cp


class _Gather:
    def __init__(self, fulls, cuts):
        n = len(fulls)
        self.fulls, self.cuts = list(fulls), list(cuts)
        self.in_specs = [ANY] * n
        self.out_specs = [ANY] * n
        self.out_shape = [jax.ShapeDtypeStruct(cut.full_shape, BF16) for cut in cuts]
        self.scratch = [pltpu.SemaphoreType.DMA((6 * n,)), pltpu.SemaphoreType.DMA((6 * n,))]

    def _step(self, step, src, out, send_sems, recv_sems):
        n, cuts = len(self.fulls), self.cuts
        x, y, c, chips = _place()
        me = 2 * x + y

        def ends(w, s, h, from_src):
            dst = cuts[w].in_full(out[w], s, h)
            return (cuts[w].in_full(src[w], s, h) if from_src else dst), dst

        for w in range(n):
            for j, chip in enumerate(chips):
                s = 2 * chip[0] + chip[1]
                k, k2 = 3 * w + j, 3 * n + 3 * w + j
                if step == "send":
                    _start_remote(*ends(w, me, c, True), send_sems.at[k], recv_sems.at[k], (*chip, c))
                elif step == "pass_on":
                    _remote(*ends(w, s, c, False), send_sems.at[k], recv_sems.at[k], (x, y, c)).wait_recv()
                    _start_remote(*ends(w, s, c, False), send_sems.at[k2], recv_sems.at[k2], (x, y, 1 - c))
                else:
                    _remote(*ends(w, s, 1 - c, False), send_sems.at[k2], recv_sems.at[k2], (x, y, c)).wait_recv()
                    _remote(*ends(w, me, c, True), send_sems.at[k], recv_sems.at[k], (x, y, c)).wait_send()
                    _remote(*ends(w, s, c, False), send_sems.at[k2], recv_sems.at[k2], (x, y, c)).wait_send()

    def carried_by(self, body, n_in, n_out, n_steps, step_index=lambda: pl.program_id(0), late=False):
        k = len(self.fulls)

        def carrier(*refs):
            ins, src = refs[:n_in], refs[n_in:n_in + k]
            outs, out = refs[n_in + k:n_in + k + n_out], refs[n_in + k + n_out:n_in + 2 * k + n_out]
            sems = refs[n_in + 2 * k + n_out:]

            def at_step(step, at):
                @pl.when(step_index() == at)
                def _():
                    self._step(step, src, out, *sems)

            at_step("send", 0)
            if not late:
                at_step("pass_on", 3 * n_steps // 4)
            body(*ins, *outs)
            if late:
                at_step("pass_on", n_steps - 1)
            at_step("finish", n_steps - 1)

        return carrier


def _exchange_halves(grads, name):
    n = len(grads)

    def body(*refs):
        g = refs[:n]
        theirs = refs[n:2 * n]
        send_sems, recv_sems = refs[2 * n:]
        x, y, c, _ = _place()
        sends = [_start_remote(g[w].at[:, 1 - c], theirs[w], send_sems.at[w], recv_sems.at[w], (x, y, 1 - c))
                 for w in range(n)]
        for w in range(n):
            _remote(g[w].at[:, 1 - c], theirs[w], send_sems.at[w], recv_sems.at[w], (x, y, c)).wait_recv()
        for cp in sends:
            cp.wait_send()

    return pl.pallas_call(
        body, name=name,
        in_specs=[ANY] * n, out_specs=[ANY] * n,
        out_shape=[jax.ShapeDtypeStruct((4,) + g.shape[2:], F32) for g in grads],
        scratch_shapes=[pltpu.SemaphoreType.DMA((n,)), pltpu.SemaphoreType.DMA((n,))],
        compiler_params=pltpu.CompilerParams(has_side_effects=True),
    )(*grads)


def _gather_small(small_ref, gathered, send_sems, recv_sems, first_sem, local_sem, start):
    x, y, c, _ = _place()
    me = 4 * x + 2 * y + c
    flips = [(fx, fy, fc) for fx in range(2) for fy in range(2) for fc in range(2)][1:]
    own = pltpu.make_async_copy(small_ref, gathered.at[me], local_sem)
    if start:
        own.start()
    else:
        own.wait()
    for k, (fx, fy, fc) in enumerate(flips):
        peer = (x + fx - 2 * x * fx, y + fy - 2 * y * fy, c + fc - 2 * c * fc)
        sems = (send_sems.at[first_sem + k], recv_sems.at[first_sem + k])
        if start:
            _start_remote(small_ref, gathered.at[me], *sems, peer)
        else:
            cp = _remote(small_ref, gathered.at[4 * peer[0] + 2 * peer[1] + peer[2]], *sems, (x, y, c))
            cp.wait_recv()
            cp.wait_send()


class _Rider:
    def __init__(self, kind, parts, small=None):
        n = len(parts)
        self.kind, self.n = kind, n
        self.per = 3 if kind == "scatter" else 1
        self.parts = list(parts) + ([] if small is None else [small])
        k = len(self.parts)
        self.in_specs = [ANY] * k
        self.out_specs = [ANY] * k
        dtype = lambda a: a.dtype if kind == "scatter" else F32
        self.out_shape = [jax.ShapeDtypeStruct((4,) + a.shape[-2:], dtype(a)) for a in parts]
        if small is not None:
            self.out_shape.append(jax.ShapeDtypeStruct((8,) + small.shape, small.dtype))
        self.scratch = [pltpu.SemaphoreType.DMA((self.per * n + 7,)), pltpu.SemaphoreType.DMA((self.per * n + 7,)),
                        pltpu.SemaphoreType.DMA]

    def _copies(self, p, out, send_sems, recv_sems, local_sem, start):
        x, y, c, chips = _place()
        me = 2 * x + y
        n = self.n
        if len(self.parts) > n:
            _gather_small(p[n], out[n], send_sems, recv_sems, self.per * n, local_sem, start)
        for w in range(n):
            if self.kind == "exchange":
                ends = [(p[w].at[:, 1 - c], out[w], out[w], (x, y, 1 - c))]
            else:
                ends = [(p[w].at[2 * cx + cy], out[w].at[me], out[w].at[2 * cx + cy], (cx, cy, c)) for cx, cy in chips]
            for j, (src, dst_there, dst_here, to) in enumerate(ends):
                sems = (send_sems.at[self.per * w + j], recv_sems.at[self.per * w + j])
                if start:
                    _start_remote(src, dst_there, *sems, to)
                else:
                    cp = _remote(src, dst_here, *sems, (x, y, c))
                    cp.wait_recv()
                    cp.wait_send()

    def carried_by(self, body, n_in, n_out, n_steps, step_index=lambda: pl.program_id(0)):
        k = len(self.parts)

        def carrier(*refs):
            ins, mine = refs[:n_in], refs[n_in:n_in + k]
            outs, theirs = refs[n_in + k:n_in + k + n_out], refs[n_in + k + n_out:n_in + 2 * k + n_out]
            sems = refs[n_in + 2 * k + n_out:]

            @pl.when(step_index() == 0)
            def _():
                self._copies(mine, theirs, *sems, start=True)

            body(*ins, *outs)

            @pl.when(step_index() == n_steps - 1)
            def _():
                self._copies(mine, theirs, *sems, start=False)

        return carrier


def _share_halves(halves, small):
    n = len(halves)

    def body(*refs):
        hv = refs[:n]
        small_ref = refs[n]
        out = refs[n + 1:2 * n + 1]
        gathered = refs[2 * n + 1]
        send_sems, recv_sems, local_sem = refs[2 * n + 2:]
        x, y, c, _ = _place()
        sends = [_start_remote(hv[w], out[w], send_sems.at[w], recv_sems.at[w], (x, y, 1 - c)) for w in range(n)]
        _gather_small(small_ref, gathered, send_sems, recv_sems, n, local_sem, True)
        for w in range(n):
            _remote(hv[w], out[w], send_sems.at[w], recv_sems.at[w], (x, y, c)).wait_recv()
        for cp in sends:
            cp.wait_send()
        _gather_small(small_ref, gathered, send_sems, recv_sems, n, local_sem, False)

    return pl.pallas_call(
        body, name="share_halves",
        in_specs=[ANY] * (n + 1), out_specs=[ANY] * (n + 1),
        out_shape=[jax.ShapeDtypeStruct(a.shape, F32) for a in halves] + [jax.ShapeDtypeStruct((8,) + small.shape, F32)],
        scratch_shapes=[pltpu.SemaphoreType.DMA((n + 7,)), pltpu.SemaphoreType.DMA((n + 7,)),
                        pltpu.SemaphoreType.DMA],
        compiler_params=pltpu.CompilerParams(has_side_effects=True),
    )(*halves, small)


def _pack_small(vecs, ws, bs, sink, extra=None):
    ws = jnp.zeros((64, 1024), F32) if ws is None else ws.reshape(64, 1024)
    bs = jnp.zeros((1, 512), F32) if bs is None else bs.reshape(1, 512)
    sink = jnp.zeros((1, 16), F32) if sink is None else sink.reshape(1, 16)
    extra = jnp.zeros((1, 1024), F32) if extra is None else extra.reshape(1, 1024)
    top = jnp.concatenate(
        [v.reshape(1, 1024) for v in vecs]
        + [jnp.pad(bs, ((0, 0), (0, 512))), jnp.pad(sink, ((0, 0), (0, 1008))), extra], axis=0)
    return jnp.concatenate([top, ws, jnp.zeros((8, 1024), F32)], axis=0)


def _unpack_small(p):
    vecs = [p[k] for k in range(5)]
    return vecs, p[8:72].reshape(4, 128, 128), p[5, :512].reshape(4, 128), p[6, :16]


def kernel(x, norm_0, w_in_0, a_v_norm_0, a_spatial_w_0, a_spatial_b_0, b_group_w_0, b_scale_0, w_out_0, norm_1, w_in_1, sink_1, w_out_1, final_norm, loss_target, m_norm_0, m_w_in_0, m_a_v_norm_0, m_a_spatial_w_0, m_a_spatial_b_0, m_b_group_w_0, m_b_scale_0, m_w_out_0, m_norm_1, m_w_in_1, m_sink_1, m_w_out_1, m_final_norm, v_norm_0, v_w_in_0, v_a_v_norm_0, v_a_spatial_w_0, v_a_spatial_b_0, v_b_group_w_0, v_b_scale_0, v_w_out_0, v_norm_1, v_w_in_1, v_sink_1, v_w_out_1, v_final_norm):
    s = x.shape[1]
    xs = x.reshape(s, D_MODEL)
    target = loss_target.reshape(s, D_MODEL)

    cuts = [_Sharded("cols", (1024, 5120)), _Sharded("rows", (2048, 1024)), _Sharded("cols", (1024, 2560)),
            _Sharded("rows", (1024, 1024)), _Sharded("mid", (4, 256, 256))]
    big_w = [w_in_0, w_out_0, w_in_1, w_out_1, b_group_w_0]
    big_m = [m_w_in_0, m_w_out_0, m_w_in_1, m_w_out_1, m_b_group_w_0]
    big_v = [v_w_in_0, v_w_out_0, v_w_in_1, v_w_out_1, v_b_group_w_0]
    where = jnp.stack([2 * lax.axis_index("x") + lax.axis_index("y"), lax.axis_index("c")]).astype(jnp.int32)
    placed = [_place_shard(where, w, cut, f"place_shard{k}") for k, (w, cut) in enumerate(zip(big_w, cuts))]
    cx, cy = lax.axis_index("x"), lax.axis_index("y")
    order = jnp.stack([2 * cx + cy, 2 * (1 - cx) + cy, 2 * cx + 1 - cy, 2 * (1 - cx) + 1 - cy]).astype(jnp.int32)

    row = lambda v: v.reshape(1, 1024)
    ws16 = a_spatial_w_0.astype(BF16)
    bsx = jnp.repeat(a_spatial_b_0.T, 256, axis=1)
    band = _band_matrices(TOK)[0]
    rope = _rope_tables(s)

    h0, z0, w_in0 = _in_proj0_own(order, xs, row(norm_0), w_in_0.astype(BF16), _Gather(placed[:1], cuts[:1]))
    z0, w_out0, wg = _in_proj0_rest(order, h0, w_in0, z0, _Gather([placed[1], placed[4]], [cuts[1], cuts[4]]))
    cat, x1, w_in1, w_out1 = _mix0_fwd(xs, z0, w_out0, row(a_v_norm_0), ws16, bsx, wg, row(b_scale_0), band,
                                       _Gather(placed[2:4], cuts[2:4]))
    h1, q, k, v, gate = _in_proj1(x1, row(norm_1), w_in1, rope)
    kpad = jnp.pad(k, ((ATTN_WINDOW, ATTN_WINDOW), (0, 0)))
    vpad = jnp.pad(v, ((ATTN_WINDOW, ATTN_WINDOW), (0, 0)))
    o, lse = _attn_fwd(q, kpad, vpad, sink_1)
    y1, dx2, do, dgate, loss_lanes, g_final, dx2h = _tail(x1, o, gate, target, w_out1, row(final_norm))

    dq, dkpad, dvpad, dsink = _attn_bwd(q, kpad, vpad, sink_1, o, lse, do, rope)
    dz1, dx1, g_norm1, dx1h = _in_proj1_bwd(dq, dkpad, dvpad, dgate, x1, dx2, row(norm_1), w_in1, rope)

    g_w_in1 = _weight_grad(h1, dz1, 2, 2, "grad_w_in1").reshape(4, 2, 512, 640)
    g_w_out1 = _weight_grad(y1, dx2h, 1, 1, "grad_w_out1").reshape(4, 2, 128, 1024)
    g_w_out0 = _weight_grad(cat, dx1h, 1, 1, "grad_w_out0").reshape(4, 2, 256, 1024)
    first = [g_w_out0, g_w_in1, g_w_out1]
    zero = jnp.zeros((1024,), F32)
    small1 = _pack_small([zero, zero, zero, g_norm1, g_final], None, None, dsink[0, :16])[:8]
    dz0, dpn, d_ws, d_bs, d_gv, d_scale, d_wg, *theirs1, small1_all = _mix0_bwd(
        dx1h, z0, w_out0, row(a_v_norm_0), ws16, bsx, wg, row(b_scale_0), band, _Rider("exchange", first, small1))
    parts1 = [_add_sibling(where, g, t, f"add_sibling1_{k}") for k, (g, t) in enumerate(zip(first, theirs1))]
    dz0 = _fill_pooled_grad(dz0, dpn)
    g_w_in0, *slots1 = _weight_grad(h0, dz0, 4, 1, "grad_w_in0", _Rider("scatter", parts1))
    g_w_in0 = g_w_in0.reshape(4, 2, 512, 1280)
    g_wg = d_wg.reshape(2, 2, 4, 64, 256).transpose(2, 0, 1, 3, 4).reshape(4, 2, 128, 256)
    second = [g_w_in0, g_wg]
    theirs2 = _exchange_halves(second, "exchange_halves2")
    parts2 = [_add_sibling(where, g, t, f"add_sibling2_{k}") for k, (g, t) in enumerate(zip(second, theirs2))]
    small2 = _pack_small([zero, d_gv, d_scale, zero, zero], d_ws, d_bs[:, :4].T, None)
    grad_x, g_norm0, *slots2, small2_all = _in_proj0_bwd(
        dz0, xs, dx1, row(norm_0), w_in0, _Rider("scatter", parts2, small2))

    grads = [g_w_in0, g_w_out0, g_w_in1, g_w_out1, g_wg]
    theirs = [theirs2[0], theirs1[0], theirs1[1], theirs1[2], theirs2[1]]
    slots = [slots2[0], slots1[0], slots1[1], slots1[2], slots2[1]]
    n = len(grads)
    reduced = [_sum_chips(where, grads[w], theirs[w], slots[w], f"sum_chips{w}") for w in range(n)]
    small3 = _pack_small([g_norm0, zero, zero, zero, zero], None, None, None, loss_lanes)[:8]
    *from_sibling, small3_all = _share_halves(reduced, small3)

    out_g, out_d, out_m, out_v = {}, {}, {}, {}
    names = ["w_in_0", "w_out_0", "w_in_1", "w_out_1", "b_group_w_0"]
    for w in range(n):
        shape = big_w[w].shape
        two_d = (-1, shape[-1])
        outs = _adamw_halves(where, big_w[w].reshape(two_d), reduced[w], from_sibling[w], big_m[w].reshape(two_d),
                             big_v[w].reshape(two_d), f"adamw{w}")
        out_g[names[w]], out_d[names[w]], out_m[names[w]], out_v[names[w]] = (a.reshape(shape) for a in outs)

    g_small, grad_x = _sum_small(small1_all, small2_all, small3_all, grad_x.reshape(1, s, D_MODEL))
    small_names = ["norm_0", "a_v_norm_0", "b_scale_0", "norm_1", "final_norm"]
    pack = lambda vecs, ws_, bs_, sk: _pack_small(vecs, ws_, bs_, sk)
    w_small = pack([norm_0, a_v_norm_0, b_scale_0, norm_1, final_norm], a_spatial_w_0, a_spatial_b_0, sink_1)
    m_small = pack([m_norm_0, m_a_v_norm_0, m_b_scale_0, m_norm_1, m_final_norm], m_a_spatial_w_0,
                   m_a_spatial_b_0, m_sink_1)
    v_small = pack([v_norm_0, v_a_v_norm_0, v_b_scale_0, v_norm_1, v_final_norm], v_a_spatial_w_0,
                   v_a_spatial_b_0, v_sink_1)
    d_small, nm_small, nv_small = _adamw(w_small, g_small, m_small, v_small, "adamw_small")
    for store, packed in ((out_g, g_small), (out_d, d_small), (out_m, nm_small), (out_v, nv_small)):
        vecs, ws_, bs_, sk = _unpack_small(packed)
        for name, vec in zip(small_names, vecs):
            store[name] = vec
        store["a_spatial_w_0"], store["a_spatial_b_0"], store["sink_1"] = ws_, bs_, sk

    loss = jnp.sum(g_small[7])
    order = ["norm_0", "w_in_0", "a_v_norm_0", "a_spatial_w_0", "a_spatial_b_0", "b_group_w_0", "b_scale_0",
             "w_out_0", "norm_1", "w_in_1", "sink_1", "w_out_1", "final_norm"]
    return (loss, grad_x, *[out_g[k] for k in order], *[out_d[k] for k in order],
            *[out_m[k] for k in order], *[out_v[k] for k in order])
```

```python
import numpy as np
import jax
import jax.numpy as jnp
from jax import lax
from jax.experimental import pallas as pl
from jax.experimental.pallas import tpu as pltpu

F32 = jnp.float32
BF16 = jnp.bfloat16
MESH = pl.DeviceIdType.MESH

D_MODEL = 1024
EPS = 1e-6
NEG_INF = -1e30
CHUNK = 128
POOL_WINDOWS = (2, 4, 8, 16)
HALO = 16
N_HEADS = 16
HEAD_DIM = 64
ATTN_WINDOW = 128
ROPE_THETA = 500000.0
ROT_DIM = 16
ADAM_LR = 0.001
ADAM_B1 = 0.9
ADAM_B2 = 0.999
ADAM_EPS = 1e-08
ADAM_WD = 0.01
ADAM_STEP = 10

TOK = 256
VMEM_LIMIT = 56 * 1024 * 1024


def _params(**kw):
    return pltpu.CompilerParams(vmem_limit_bytes=VMEM_LIMIT, **kw)


def _whole(shape):
    nd = len(shape)
    return pl.BlockSpec(shape, lambda *_: (0,) * nd)


def _rows(t, n):
    return pl.BlockSpec((t, n), lambda i: (i, 0))


ANY = pl.BlockSpec(memory_space=pl.ANY)

_G0 = 0.7978845608028654
_G1 = 0.044715


_G2 = _G0 * _G1


def _gelu(x):
    t = jnp.tanh(x * (_G0 + _G2 * (x * x)))
    return x * (0.5 + 0.5 * t)


def _gelu_and_grad(x):
    x2 = x * x
    t = jnp.tanh(x * (_G0 + _G2 * x2))
    half = 0.5 + 0.5 * t
    return x * half, half + (x * (1.0 - t * t)) * (0.5 * _G0 + 1.5 * _G2 * x2)


def _sigmoid(x):
    return 1.0 / (1.0 + jnp.exp(-x))


def _dot(a, b):
    return jnp.dot(a, b, preferred_element_type=F32)


def _dot_nt(a, b):
    return lax.dot_general(a, b, (((1,), (1,)), ((), ())), preferred_element_type=F32)


def _dot_tn(a, b):
    return lax.dot_general(a, b, (((0,), (0,)), ((), ())), preferred_element_type=F32)


def _rms_fwd(x, g):
    r = lax.rsqrt(jnp.mean(x * x, axis=-1, keepdims=True) + EPS)
    xh = x * r
    return r, xh, xh * g


def _rms_bwd(dy, g, r, xh):
    dxh = dy * g
    return r * (dxh - xh * jnp.mean(dxh * xh, axis=-1, keepdims=True))


def _band_matrices(t):
    r = np.arange(t)[:, None]
    j = np.arange(t + 2 * HALO)[None, :]
    fwd, bwd = [], []
    for w in POOL_WINDOWS:
        d = j - r - HALO
        fwd.append((d >= -(w // 2)) & (d < w // 2))
        bwd.append((d >= -(w // 2) + 1) & (d <= w // 2))
    return (jnp.asarray(np.stack(fwd), BF16), jnp.asarray(np.stack(bwd), BF16))


def _window_counts(i, t, s):
    tok = i * t + lax.broadcasted_iota(jnp.int32, (t, 1), 0)
    out = []
    for w in POOL_WINDOWS:
        cnt = jnp.minimum(tok + w // 2, s) - jnp.maximum(tok - w // 2, 0)
        out.append(cnt.astype(F32))
    return out


def _rope_tables(s):
    inv = np.float32(ROPE_THETA) ** (-np.arange(0, ROT_DIM, 2, dtype=np.float32) / np.float32(ROT_DIM))
    ang = np.arange(s, dtype=np.float32)[:, None] * inv.astype(np.float32)[None, :]
    cos, sin = np.cos(ang).astype(np.float32), np.sin(ang).astype(np.float32)
    z8 = np.zeros((s, 8), np.float32)
    z48 = np.zeros((s, HEAD_DIM - ROT_DIM), np.float32)
    c = np.concatenate([cos, cos, np.ones((s, HEAD_DIM - ROT_DIM), np.float32)], axis=1)
    s_lo = np.concatenate([z8, sin, z48], axis=1)
    s_hi = np.concatenate([-sin, z8, z48], axis=1)
    return tuple(jnp.asarray(np.concatenate([a, a], axis=1)) for a in (c, s_lo, s_hi))


def _rope(x, c, s_lo, s_hi):
    n = x.shape[1]
    reps = n // 128
    c, s_lo, s_hi = (jnp.tile(a, (1, reps)) for a in (c, s_lo, s_hi))
    return x * c + pltpu.roll(x, 8, 1) * s_lo + pltpu.roll(x, n - 8, 1) * s_hi


def _rope_t(dx, c, s_lo, s_hi):
    n = dx.shape[1]
    reps = n // 128
    c, s_lo, s_hi = (jnp.tile(a, (1, reps)) for a in (c, s_lo, s_hi))
    return dx * c + pltpu.roll(dx * s_lo, n - 8, 1) + pltpu.roll(dx * s_hi, 8, 1)


def _in_proj0_own(order, x, g0, w_own, rider):
    s = x.shape[0]
    n = w_own.shape[1]

    def body(order_ref, x_ref, g_ref, w_ref, h_ref, z_ref):
        _, _, h = _rms_fwd(x_ref[...], g_ref[...])
        h = h.astype(BF16)
        h_ref[...] = h
        z_ref[...] = _dot(h, w_ref[...]).astype(BF16)

    return pl.pallas_call(
        rider.carried_by(body, 4, 2, s // TOK, late=True), name="in_proj0_own",
        grid_spec=pltpu.PrefetchScalarGridSpec(
            num_scalar_prefetch=1, grid=(s // TOK,),
            in_specs=[pl.BlockSpec((TOK, D_MODEL), lambda i, o: (i, 0)), pl.BlockSpec((1, D_MODEL), lambda i, o: (0, 0)),
                      pl.BlockSpec(w_own.shape, lambda i, o: (0, 0))] + rider.in_specs,
            out_specs=[pl.BlockSpec((TOK, D_MODEL), lambda i, o: (i, 0)),
                       pl.BlockSpec((TOK, n), lambda i, o: (i, o[0]))] + rider.out_specs,
            scratch_shapes=rider.scratch),
        out_shape=[jax.ShapeDtypeStruct((s, D_MODEL), BF16), jax.ShapeDtypeStruct((s, 4 * n), BF16)] + rider.out_shape,
        input_output_aliases={4 + j: 2 + j for j in range(len(rider.fulls))},
        compiler_params=_params(),
    )(order, x, g0, w_own, *rider.fulls)


def _in_proj0_rest(order, h0, w_in0, z0, rider):
    s = h0.shape[0]
    tok = min(s, 4 * TOK)
    n_tiles = s // tok
    n = w_in0.shape[1] // 4

    def body(order_ref, h_ref, w_ref, z_in_ref, z_ref):
        z_ref[...] = _dot(h_ref[...], w_ref[...]).astype(BF16)

    return pl.pallas_call(
        rider.carried_by(body, 4, 1, 3 * n_tiles, lambda: pl.program_id(0) * n_tiles + pl.program_id(1)),
        name="in_proj0_rest",
        grid_spec=pltpu.PrefetchScalarGridSpec(
            num_scalar_prefetch=1, grid=(3, n_tiles),
            in_specs=[pl.BlockSpec((tok, D_MODEL), lambda k, i, o: (i, 0)),
                      pl.BlockSpec((w_in0.shape[0], n), lambda k, i, o: (0, o[1 + k])), ANY] + rider.in_specs,
            out_specs=[pl.BlockSpec((tok, n), lambda k, i, o: (i, o[1 + k]))] + rider.out_specs,
            scratch_shapes=rider.scratch),
        out_shape=[jax.ShapeDtypeStruct(z0.shape, BF16)] + rider.out_shape,
        input_output_aliases={3: 0, **{4 + j: 1 + j for j in range(len(rider.fulls))}},
        compiler_params=_params(),
    )(order, h0, w_in0, z0, *rider.fulls)


def _halo_specs(s, col_block, tok=TOK):
    per = tok // HALO
    last = s // HALO - 1
    prev = pl.BlockSpec((HALO, 1024), lambda i: (jnp.maximum(i * per - 1, 0), col_block))
    nxt = pl.BlockSpec((HALO, 1024), lambda i: (jnp.minimum((i + 1) * per, last), col_block))
    return prev, nxt


def _with_halo(i, n_tiles, prev_ref, cur, next_ref):
    prev = prev_ref[...]
    nxt = next_ref[...]
    prev = jnp.where(i > 0, prev, jnp.zeros_like(prev))
    nxt = jnp.where(i < n_tiles - 1, nxt, jnp.zeros_like(nxt))
    return jnp.concatenate([prev, cur, nxt], axis=0)


def _mixer_a(v1, gv, ws_ref, bsx):
    rv, vh, v2 = _rms_fwd(v1, gv)
    v2 = v2.astype(BF16)
    rows = []
    for c in range(v1.shape[0] // CHUNK):
        cols = [_dot(ws_ref[h], v2[c * CHUNK:(c + 1) * CHUNK, h * 256:(h + 1) * 256]) for h in range(4)]
        rows.append(jnp.concatenate(cols, axis=1) + bsx)
    return rv, vh, v2, jnp.concatenate(rows, axis=0)


def _mixer_b_pooled(bx, halo, band_ref, counts):
    out = []
    for g in range(4):
        win = _dot(band_ref[g], halo[:, g * 256:(g + 1) * 256])
        out.append(win / counts[g] - bx[:, g * 256:(g + 1) * 256])
    return out


def _mix0_fwd(x, z0, w_out0, gv, ws, bsx, wg, scale, band, rider):
    s = x.shape[0]
    tok = min(s, 2 * TOK)
    n_tiles = s // tok
    k = len(rider.fulls)

    def body(z_ref, zp_ref, zn_ref, x_ref, wout_ref, gv_ref, ws_ref, bsx_ref, wg_ref, sc_ref, band_ref,
             cat_ref, x1_ref):
        i = pl.program_id(0)
        halves = [slice(r * TOK, (r + 1) * TOK) for r in range(tok // TOK)]
        mixeds = [_mixer_a(_gelu(z_ref[rows, 1024:2048].astype(F32)), gv_ref[...], ws_ref, bsx_ref[...])[3]
                  for rows in halves]
        x1s = []
        for rows, mixed in zip(halves, mixeds):
            au = z_ref[rows, 0:1024].astype(F32)
            ag = z_ref[rows, 2048:3072].astype(F32)
            ya = (_gelu(au) * mixed * (ag * _sigmoid(ag))).astype(BF16)
            cat_ref[rows, 0:1024] = ya
            x1s.append(x_ref[rows, :] + _dot(ya, wout_ref[0:1024, :]))

        bxs = [z_ref[rows, 3072:4096] for rows in halves]
        ybs = []
        for r, rows in enumerate(halves):
            prev = bxs[r - 1][TOK - HALO:] if r > 0 else jnp.where(i > 0, zp_ref[...], jnp.zeros_like(zp_ref))
            nxt = (bxs[r + 1][:HALO] if r + 1 < len(halves)
                   else jnp.where(i < n_tiles - 1, zn_ref[...], jnp.zeros_like(zn_ref)))
            halo = jnp.concatenate([prev, bxs[r], nxt], axis=0)
            ps = _mixer_b_pooled(bxs[r].astype(F32), halo, band_ref, _window_counts(i * len(halves) + r, TOK, s))
            pw = jnp.concatenate([_dot(ps[g].astype(BF16), wg_ref[g]) for g in range(4)], axis=1)
            bg = z_ref[rows, 4096:5120].astype(F32)
            ybs.append((pw * sc_ref[...] * (bg * _sigmoid(bg))).astype(BF16))
            cat_ref[rows, 1024:2048] = ybs[-1]
        for rows, x1, yb in zip(halves, x1s, ybs):
            x1_ref[rows, :] = x1 + _dot(yb, wout_ref[1024:2048, :])

    prev, nxt = _halo_specs(s, 3, tok)
    return pl.pallas_call(
        rider.carried_by(body, 11, 2, n_tiles), name="mix0_fwd", grid=(n_tiles,),
        in_specs=[_rows(tok, 5120), prev, nxt, _rows(tok, D_MODEL), _whole(w_out0.shape), _whole((1, 1024)),
                  _whole(ws.shape), _whole(bsx.shape), _whole(wg.shape), _whole((1, 1024)), _whole(band.shape)]
        + rider.in_specs,
        out_specs=[_rows(tok, 2048), _rows(tok, D_MODEL)] + rider.out_specs,
        out_shape=[jax.ShapeDtypeStruct((s, 2048), BF16), jax.ShapeDtypeStruct((s, D_MODEL), F32)] + rider.out_shape,
        input_output_aliases={11 + j: 2 + j for j in range(k)},
        scratch_shapes=rider.scratch,
        compiler_params=_params(),
    )(z0, z0, z0, x, w_out0, gv, ws, bsx, wg, scale, band, *rider.fulls)


def _in_proj1(x1, g1, w_in1, rope):
    s = x1.shape[0]

    def body(x_ref, g_ref, w_ref, c_ref, lo_ref, hi_ref, h_ref, q_ref, k_ref, v_ref, gate_ref):
        halves = [slice(r * TOK, (r + 1) * TOK) for r in range(tok // TOK)]
        hs = [_rms_fwd(x_ref[rows, :], g_ref[...])[2].astype(BF16) for rows in halves]
        for rows, h in zip(halves, hs):
            h_ref[rows, :] = h
        qs = [_dot(h, w_ref[:, 0:1024]) for h in hs]
        kvs = [_dot(h, w_ref[:, 1024:1536]) for h in hs]
        for rows, q, kv in zip(halves, qs, kvs):
            tabs = (c_ref[rows, :], lo_ref[rows, :], hi_ref[rows, :])
            q_ref[rows, :] = (_rope(q, *tabs) * Q_SCALE).astype(BF16)
            k_ref[rows, :] = _rope(kv[:, 0:256], *tabs).astype(BF16)
            v_ref[rows, :] = kv[:, 256:512].astype(BF16)
        for rows, h in zip(halves, hs):
            gate_ref[rows, :] = _dot(h, w_ref[:, 1536:2560]).astype(BF16)

    tok = min(s, 2 * TOK)
    tab = _rows(tok, 128)
    return pl.pallas_call(
        body, name="in_proj1", grid=(s // tok,),
        in_specs=[_rows(tok, D_MODEL), _whole((1, D_MODEL)), _whole(w_in1.shape), tab, tab, tab],
        out_specs=[_rows(tok, 1024), _rows(tok, 1024), _rows(tok, 256), _rows(tok, 256), _rows(tok, 1024)],
        out_shape=[jax.ShapeDtypeStruct((s, 1024), BF16), jax.ShapeDtypeStruct((s, 1024), BF16),
                   jax.ShapeDtypeStruct((s, 256), BF16), jax.ShapeDtypeStruct((s, 256), BF16),
                   jax.ShapeDtypeStruct((s, 1024), BF16)],
        compiler_params=_params(),
    )(x1, g1, w_in1, *rope)


QBLK = 128
KBLK = QBLK + 2 * ATTN_WINDOW
Q_SCALE = HEAD_DIM ** -0.5


def _block_bias(q0, s):
    r = lax.broadcasted_iota(jnp.int32, (QBLK, KBLK), 0)
    c = lax.broadcasted_iota(jnp.int32, (QBLK, KBLK), 1)
    kj = q0 - ATTN_WINDOW + c
    ok = (c >= r) & (c <= r + 2 * ATTN_WINDOW) & (kj >= 0) & (kj < s)
    return jnp.where(ok, 0.0, NEG_INF)


def _pair_operands(t):
    lane = lax.broadcasted_iota(jnp.int32, (1, 128), 1)
    first = lane < HEAD_DIM
    zero = jnp.zeros((KBLK, 128), BF16)
    out = []
    for j in range(2):
        slab = t[:, 128 * j:128 * (j + 1)]
        turned = pltpu.bitcast(pltpu.roll(pltpu.bitcast(slab, jnp.uint32), HEAD_DIM, 1), BF16)
        for own_first in (True, False):
            top = jnp.where(first, slab if own_first else turned, zero)
            bottom = jnp.where(first, zero, turned if own_first else slab)
            out.append(jnp.concatenate([top, bottom], axis=0))
    return out


def _attn_fwd(q, kpad, vpad, sink):
    s = q.shape[0]

    def body(sink_ref, q_ref, k_ref, v_ref, o_ref, lse_ref):
        i = pl.program_id(0)
        lane = lax.broadcasted_iota(jnp.int32, (1, 128), 1)
        for b in range(TOK // QBLK):
            rows = slice(b * QBLK, (b + 1) * QBLK)
            start = pl.multiple_of(i * TOK + b * QBLK, QBLK)
            k_bd = _pair_operands(k_ref[pl.ds(start, KBLK), :])
            v_bd = _pair_operands(v_ref[pl.ds(start, KBLK), :])
            bias = _block_bias(i * TOK + b * QBLK, s)
            pairs = range(N_HEADS // 2)
            sc4 = [_dot_nt(jnp.concatenate([q_ref[rows, 256 * g:256 * g + 128], q_ref[rows, 256 * g + 128:256 * (g + 1)]],
                                           axis=0), k_bd[g]) for g in range(4)]
            sc2 = [sc4[m // 2][(m % 2) * QBLK:(m % 2 + 1) * QBLK] for m in pairs]
            scs = [sc2[h // 2][:, (h % 2) * KBLK:(h % 2 + 1) * KBLK] + bias for h in range(N_HEADS)]
            ms = [jnp.maximum(jnp.max(scs[h], axis=-1, keepdims=True), sink_ref[h]) for h in range(N_HEADS)]
            es = [jnp.exp(scs[h] - ms[h]) for h in range(N_HEADS)]
            dens = [jnp.sum(es[h], axis=-1, keepdims=True) + jnp.exp(sink_ref[h] - ms[h]) for h in range(N_HEADS)]
            first = lane < HEAD_DIM
            e2 = [jnp.concatenate([es[2 * m].astype(BF16), es[2 * m + 1].astype(BF16)], axis=1) for m in pairs]
            o4 = [_dot(jnp.concatenate([e2[2 * g], e2[2 * g + 1]], axis=0), v_bd[g]) for g in range(4)]
            outs = [o4[m // 2][(m % 2) * QBLK:(m % 2 + 1) * QBLK]
                    * jnp.where(first, 1.0 / dens[2 * m], 1.0 / dens[2 * m + 1]) for m in pairs]
            o_ref[rows, :] = jnp.concatenate(outs, axis=1).astype(BF16)
            lse = jnp.zeros((QBLK, 128), F32)
            for h in range(N_HEADS):
                lse = lse + jnp.where(lane == h, ms[h] + jnp.log(dens[h]), 0.0)
            lse_ref[rows, :] = lse

    return pl.pallas_call(
        body, name="attn_fwd", grid=(s // TOK,),
        in_specs=[pl.BlockSpec(memory_space=pltpu.SMEM), _rows(TOK, 1024), _whole(kpad.shape), _whole(vpad.shape)],
        out_specs=[_rows(TOK, 1024), _rows(TOK, 128)],
        out_shape=[jax.ShapeDtypeStruct((s, 1024), BF16), jax.ShapeDtypeStruct((s, 128), F32)],
        compiler_params=_params(),
    )(sink, q, kpad, vpad)


def _tail(x1, o, gate, target, w_out1, gf):
    s = x1.shape[0]

    def body(x1_ref, o_ref, gate_ref, t_ref, w_ref, gf_ref, y1_ref, dx2_ref, do_ref, dgate_ref, loss_ref, gfn_ref,
             dx2h_ref):
        i = pl.program_id(0)

        @pl.when(i == 0)
        def _():
            loss_ref[...] = jnp.zeros_like(loss_ref)
            gfn_ref[...] = jnp.zeros_like(gfn_ref)

        halves = [slice(r * TOK, (r + 1) * TOK) for r in range(tok // TOK)]
        gf = gf_ref[...]
        gs = [gate_ref[rows, :].astype(F32) for rows in halves]
        sgs = [_sigmoid(g) for g in gs]
        sils = [g * sg for g, sg in zip(gs, sgs)]
        os_ = [o_ref[rows, :].astype(F32) for rows in halves]
        y1s = [(o * sil).astype(BF16) for o, sil in zip(os_, sils)]
        for rows, y1 in zip(halves, y1s):
            y1_ref[rows, :] = y1
        x2s = [x1_ref[rows, :] + _dot(y1, w_ref[...]) for rows, y1 in zip(halves, y1s)]
        dx2hs = []
        for rows, x2 in zip(halves, x2s):
            r, xh, out = _rms_fwd(x2, gf)
            diff = out - t_ref[rows, :]
            loss_ref[...] += jnp.sum(diff * diff, axis=0, keepdims=True) * (0.5 / D_MODEL)
            dout = diff * (1.0 / D_MODEL)
            gfn_ref[...] += jnp.sum(dout * xh, axis=0, keepdims=True)
            dx2 = _rms_bwd(dout, gf, r, xh)
            dx2_ref[rows, :] = dx2
            dx2hs.append(dx2.astype(BF16))
            dx2h_ref[rows, :] = dx2hs[-1]
        dy1s = [_dot_nt(dx2h, w_ref[...]) for dx2h in dx2hs]
        for rows, dy1, sil, o, sg, g in zip(halves, dy1s, sils, os_, sgs, gs):
            do_ref[rows, :] = (dy1 * sil).astype(BF16)
            dgate_ref[rows, :] = (dy1 * o * (sg * (1.0 + g * (1.0 - sg)))).astype(BF16)

    tok = min(s, 2 * TOK)
    row = _rows(tok, 1024)
    acc = _whole((1, 1024))
    return pl.pallas_call(
        body, name="tail", grid=(s // tok,),
        in_specs=[row, row, row, row, _whole(w_out1.shape), acc],
        out_specs=[row, row, row, row, acc, acc, row],
        out_shape=[jax.ShapeDtypeStruct((s, 1024), BF16), jax.ShapeDtypeStruct((s, 1024), F32),
                   jax.ShapeDtypeStruct((s, 1024), BF16), jax.ShapeDtypeStruct((s, 1024), BF16),
                   jax.ShapeDtypeStruct((1, 1024), F32), jax.ShapeDtypeStruct((1, 1024), F32),
                   jax.ShapeDtypeStruct((s, 1024), BF16)],
        compiler_params=_params(),
    )(x1, o, gate, target, w_out1, gf)


def _attn_bwd(q, kpad, vpad, sink, o, lse, do, rope):
    s = q.shape[0]
    pad_t = (kpad.shape[1], kpad.shape[0])

    def body(sink_ref, q_ref, k_ref, v_ref, o_ref, lse_ref, do_ref, c_ref, lo_ref, hi_ref,
             dq_ref, dk_ref, dv_ref, ds_ref):
        i = pl.program_id(0)

        @pl.when(i == 0)
        def _():
            dk_ref[...] = jnp.zeros_like(dk_ref)
            dv_ref[...] = jnp.zeros_like(dv_ref)
            ds_ref[...] = jnp.zeros_like(ds_ref)

        lane = lax.broadcasted_iota(jnp.int32, (1, 128), 1)
        dsink = jnp.zeros((1, 128), F32)
        for b in range(TOK // QBLK):
            rows = slice(b * QBLK, (b + 1) * QBLK)
            start = pl.multiple_of(i * TOK + b * QBLK, QBLK)
            k_bd = _pair_operands(k_ref[pl.ds(start, KBLK), :])
            v_bd = _pair_operands(v_ref[pl.ds(start, KBLK), :])
            bias = _block_bias(i * TOK + b * QBLK, s)
            lanes_of = (lane < HEAD_DIM, lane >= HEAD_DIM)
            half = lambda t, j: t[:, j * KBLK:(j + 1) * KBLK]
            dqs, dks, dvs = [], [], []
            for g in range(4):
                pairs = (2 * g, 2 * g + 1)
                qs = {m: q_ref[rows, 128 * m:128 * (m + 1)] for m in pairs}
                dos = {m: do_ref[rows, 128 * m:128 * (m + 1)] for m in pairs}
                lses = {h: lse_ref[rows, h:h + 1] for h in range(4 * g, 4 * g + 4)}
                stacked = lambda parts: jnp.concatenate([parts[m] for m in pairs], axis=0)
                unstack = lambda t: {m: t[j * QBLK:(j + 1) * QBLK] for j, m in enumerate(pairs)}
                sc2 = unstack(_dot_nt(stacked(qs), k_bd[g]))
                ps = {2 * m + j: jnp.exp(half(sc2[m], j) + bias - lses[2 * m + j]) for m in pairs for j in range(2)}
                prods = {m: dos[m].astype(F32) * o_ref[rows, 128 * m:128 * (m + 1)].astype(F32) for m in pairs}
                deltas = {2 * m + j: jnp.sum(jnp.where(lanes_of[j], prods[m], 0.0), axis=-1, keepdims=True)
                          for m in pairs for j in range(2)}
                for h in range(4 * g, 4 * g + 4):
                    dsink = dsink + jnp.where(
                        lane == h, -jnp.sum(jnp.exp(sink_ref[h] - lses[h]) * deltas[h], axis=0, keepdims=True), 0.0)
                dp2 = unstack(_dot_nt(stacked(dos), v_bd[g]))
                ds2 = {m: jnp.concatenate(
                    [(ps[2 * m + j] * (half(dp2[m], j) - deltas[2 * m + j])).astype(BF16) for j in range(2)], axis=1)
                    for m in pairs}
                p2 = {m: jnp.concatenate([ps[2 * m].astype(BF16), ps[2 * m + 1].astype(BF16)], axis=1) for m in pairs}
                diag = lambda t: t[0:HEAD_DIM, 0:KBLK] + t[HEAD_DIM:128, KBLK:2 * KBLK]
                dvs.append(diag(_dot_tn(stacked(dos), stacked(p2))))
                dks.append(diag(_dot_tn(stacked(qs), stacked(ds2))))
                dq2 = unstack(_dot(stacked(ds2), k_bd[g]) * Q_SCALE)
                dqs += [dq2[m] for m in pairs]
            dq = jnp.concatenate(dqs, axis=1)
            dq_ref[rows, :] = _rope_t(dq, c_ref[rows, :], lo_ref[rows, :], hi_ref[rows, :]).astype(BF16)
            dk_ref[:, pl.ds(start, KBLK)] += jnp.concatenate(dks, axis=0)
            dv_ref[:, pl.ds(start, KBLK)] += jnp.concatenate(dvs, axis=0)
        ds_ref[...] += dsink

    row = _rows(TOK, 1024)
    tab = _rows(TOK, 128)
    pad = _whole(kpad.shape)
    return pl.pallas_call(
        body, name="attn_bwd", grid=(s // TOK,),
        in_specs=[pl.BlockSpec(memory_space=pltpu.SMEM), row, pad, pad, row, tab, row, tab, tab, tab],
        out_specs=[row, _whole(pad_t), _whole(pad_t), _whole((1, 128))],
        out_shape=[jax.ShapeDtypeStruct((s, 1024), BF16), jax.ShapeDtypeStruct(pad_t, F32),
                   jax.ShapeDtypeStruct(pad_t, F32), jax.ShapeDtypeStruct((1, 128), F32)],
        compiler_params=_params(),
    )(sink, q, kpad, vpad, o, lse, do, *rope)


def _in_proj1_bwd(dq, dk_t, dv_t, dgate, x1, dx2, g1, w_in1, rope):
    s = x1.shape[0]
    tok = min(s, 2 * TOK)
    n_sub = tok // ATTN_WINDOW

    def body(*refs):
        dq_ref = refs[0]
        dk_refs, dv_refs = refs[1:1 + n_sub], refs[1 + n_sub:1 + 2 * n_sub]
        (dgate_ref, x1_ref, dx2_ref, g_ref, w_ref, c_ref, lo_ref, hi_ref,
         dz_ref, dx1_ref, gn_ref, dx1h_ref) = refs[1 + 2 * n_sub:]

        @pl.when(pl.program_id(0) == 0)
        def _():
            gn_ref[...] = jnp.zeros_like(gn_ref)

        halves = [slice(r * TOK, (r + 1) * TOK) for r in range(tok // TOK)]
        per = TOK // ATTN_WINDOW
        g = g_ref[...]
        for r, rows in enumerate(halves):
            dk = jnp.concatenate([ref[...] for ref in dk_refs[r * per:(r + 1) * per]], axis=1).T
            dv = jnp.concatenate([ref[...] for ref in dv_refs[r * per:(r + 1) * per]], axis=1).T
            dz_ref[rows, 0:1024] = dq_ref[rows, :]
            dz_ref[rows, 1024:1280] = _rope_t(dk, c_ref[rows, :], lo_ref[rows, :], hi_ref[rows, :]).astype(BF16)
            dz_ref[rows, 1280:1536] = dv.astype(BF16)
            dz_ref[rows, 1536:2560] = dgate_ref[rows, :]
        dhs = [_dot_nt(dz_ref[rows, :], w_ref[...]) for rows in halves]
        for rows, dh in zip(halves, dhs):
            r, xh, _ = _rms_fwd(x1_ref[rows, :], g)
            gn_ref[...] += jnp.sum(dh * xh, axis=0, keepdims=True)
            dx1 = dx2_ref[rows, :] + _rms_bwd(dh, g, r, xh)
            dx1_ref[rows, :] = dx1
            dx1h_ref[rows, :] = dx1.astype(BF16)

    row = _rows(tok, 1024)
    subs = [pl.BlockSpec((256, ATTN_WINDOW), lambda i, j=j: (0, n_sub * i + 1 + j)) for j in range(n_sub)]
    tab = _rows(tok, 128)
    acc = _whole((1, 1024))
    return pl.pallas_call(
        body, name="in_proj1_bwd", grid=(s // tok,),
        in_specs=[row] + subs + subs + [row, row, row, acc, _whole(w_in1.shape), tab, tab, tab],
        out_specs=[_rows(tok, 2560), row, acc, row],
        out_shape=[jax.ShapeDtypeStruct((s, 2560), BF16), jax.ShapeDtypeStruct((s, 1024), F32),
                   jax.ShapeDtypeStruct((1, 1024), F32), jax.ShapeDtypeStruct((s, 1024), BF16)],
        compiler_params=_params(),
    )(dq, *[dk_t] * n_sub, *[dv_t] * n_sub, dgate, x1, dx2, g1, w_in1, *rope)


def _mix0_bwd(dx1, z0, w_out0, gv, ws, bsx, wg, scale, band, rider):
    s = dx1.shape[0]
    n_tiles = s // TOK

    def body(dx1h_ref, z_ref, zp_ref, zn_ref, wout_ref, gv_ref, ws_ref, bsx_ref, wg_ref, sc_ref, band_ref,
             dz_ref, dpn_ref, dws_ref, dbs_ref, dgv_ref, dsc_ref, dwg_ref):
        i = pl.program_id(0)
        dz_ref[:, 3072:4096] = jnp.zeros((TOK, 1024), BF16)

        @pl.when(i == 0)
        def _():
            for ref in (dws_ref, dbs_ref, dgv_ref, dsc_ref, dwg_ref):
                ref[...] = jnp.zeros_like(ref)

        dcat = _dot_nt(dx1h_ref[...], wout_ref[...])
        dya = dcat[:, 0:1024]
        dyb = dcat[:, 1024:2048]

        au = z_ref[:, 0:1024].astype(F32)
        av = z_ref[:, 1024:2048].astype(F32)
        ag = z_ref[:, 2048:3072].astype(F32)
        gv = gv_ref[...]
        u, du = _gelu_and_grad(au)
        v1, dv1 = _gelu_and_grad(av)
        rv, vh, v2, mixed = _mixer_a(v1, gv, ws_ref, bsx_ref[...])
        sg = _sigmoid(ag)
        sil = ag * sg
        dz_ref[:, 2048:3072] = (dya * u * mixed * (sg * (1.0 + ag * (1.0 - sg)))).astype(BF16)
        dz_ref[:, 0:1024] = (dya * mixed * sil * du).astype(BF16)
        dmixed = dya * u * sil
        lane = lax.broadcasted_iota(jnp.int32, (1, 128), 1)
        dm16 = dmixed.astype(BF16)
        dv2_rows = []
        for c in range(TOK // CHUNK):
            rows = slice(c * CHUNK, (c + 1) * CHUNK)
            cols_out = []
            for h in range(4):
                cols = slice(h * 256, (h + 1) * 256)
                dws_ref[h] += _dot_nt(dm16[rows, cols], v2[rows, cols])
                dbs_ref[...] += jnp.where(lane == h, jnp.sum(dmixed[rows, cols], axis=-1, keepdims=True), 0.0)
                cols_out.append(_dot_tn(ws_ref[h], dm16[rows, cols]))
            dv2_rows.append(jnp.concatenate(cols_out, axis=1))
        dv2 = jnp.concatenate(dv2_rows, axis=0)
        dgv_ref[...] += jnp.sum(dv2 * vh, axis=0, keepdims=True)
        dz_ref[:, 1024:2048] = (_rms_bwd(dv2, gv, rv, vh) * dv1).astype(BF16)

        bx = z_ref[:, 3072:4096]
        bg = z_ref[:, 4096:5120].astype(F32)
        counts = _window_counts(i, TOK, s)
        halo = _with_halo(i, n_tiles, zp_ref, bx, zn_ref)
        ps = [p.astype(BF16) for p in _mixer_b_pooled(bx.astype(F32), halo, band_ref, counts)]
        pw = jnp.concatenate([_dot(ps[g], wg_ref[g]) for g in range(4)], axis=1)
        sgb = _sigmoid(bg)
        sc = sc_ref[...]
        dz_ref[:, 4096:5120] = (dyb * pw * sc * (sgb * (1.0 + bg * (1.0 - sgb)))).astype(BF16)
        dys = dyb * (bg * sgb)
        dsc_ref[...] += jnp.sum(dys * pw, axis=0, keepdims=True)
        dpw = (dys * sc).astype(BF16)
        for g in range(4):
            cols = slice(g * 256, (g + 1) * 256)
            dwg_ref[g] += _dot_tn(ps[g], dpw[:, cols])
            dpn_ref[:, cols] = (_dot_nt(dpw[:, cols], wg_ref[g]) / counts[g]).astype(BF16)

    prev, nxt = _halo_specs(s, 3)
    row = _rows(TOK, 1024)
    vec = _whole((1, 1024))
    return pl.pallas_call(
        rider.carried_by(body, 11, 7, n_tiles), name="mix0_bwd", grid=(n_tiles,),
        in_specs=[row, _rows(TOK, 5120), prev, nxt, _whole(w_out0.shape), vec, _whole(ws.shape),
                  _whole(bsx.shape), _whole(wg.shape), vec, _whole(band.shape)] + rider.in_specs,
        out_specs=[_rows(TOK, 5120), row, _whole((4, 128, 128)), _whole((128, 128)), vec, vec,
                   _whole((4, 256, 256))] + rider.out_specs,
        out_shape=[jax.ShapeDtypeStruct((s, 5120), BF16), jax.ShapeDtypeStruct((s, 1024), BF16),
                   jax.ShapeDtypeStruct((4, 128, 128), F32), jax.ShapeDtypeStruct((128, 128), F32),
                   jax.ShapeDtypeStruct((1, 1024), F32), jax.ShapeDtypeStruct((1, 1024), F32),
                   jax.ShapeDtypeStruct((4, 256, 256), F32)] + rider.out_shape,
        scratch_shapes=rider.scratch,
        compiler_params=_params(),
    )(dx1, z0, z0, z0, w_out0, gv, ws, bsx, wg, scale, band, *rider.parts)


def _fill_pooled_grad(dz0, dpn):
    s = dpn.shape[0]
    tok = min(s, 2 * TOK)
    n_tiles = s // tok
    band_t = _band_matrices(tok)[1]

    def body(dz_in_ref, dpn_ref, dpp_ref, dpx_ref, band_ref, dbx_ref):
        i = pl.program_id(0)
        dpn = dpn_ref[...]
        halo = _with_halo(i, n_tiles, dpp_ref, dpn, dpx_ref)
        counts = _window_counts(i, tok, s)
        for g in range(4):
            cols = slice(g * 256, (g + 1) * 256)
            dbx = _dot(band_ref[g], halo[:, cols]) - dpn[:, cols].astype(F32) * counts[g]
            dbx_ref[:, cols] = dbx.astype(BF16)

    prev, nxt = _halo_specs(s, 0, tok)
    return pl.pallas_call(
        body, name="fill_pooled_grad", grid=(n_tiles,),
        in_specs=[ANY, _rows(tok, 1024), prev, nxt, _whole(band_t.shape)],
        out_specs=pl.BlockSpec((tok, 1024), lambda i: (i, 3)),
        out_shape=jax.ShapeDtypeStruct(dz0.shape, BF16),
        input_output_aliases={0: 0},
        compiler_params=_params(),
    )(dz0, dpn, dpn, dpn, band_t)


def _in_proj0_bwd(dz0, x, dx1, g0, w_in0, rider):
    s = x.shape[0]
    tok = min(s, 2 * TOK)
    n_tiles = s // tok

    def body(dz_ref, x_ref, dx1_ref, g_ref, w_ref, dx_ref, gn_ref):
        i = pl.program_id(0)

        @pl.when(i == 0)
        def _():
            gn_ref[...] = jnp.zeros_like(gn_ref)

        halves = [slice(r * TOK, (r + 1) * TOK) for r in range(tok // TOK)]
        g0v = g_ref[...]
        dhs = [_dot_nt(dz_ref[rows, :], w_ref[...]) for rows in halves]
        for rows, dh in zip(halves, dhs):
            r, xh, _ = _rms_fwd(x_ref[rows, :], g0v)
            gn_ref[...] += jnp.sum(dh * xh, axis=0, keepdims=True)
            dx_ref[rows, :] = dx1_ref[rows, :] + _rms_bwd(dh, g0v, r, xh)

    row = _rows(tok, 1024)
    vec = _whole((1, 1024))
    return pl.pallas_call(
        rider.carried_by(body, 5, 2, n_tiles), name="in_proj0_bwd", grid=(n_tiles,),
        in_specs=[_rows(tok, 5120), row, row, vec, _whole(w_in0.shape)] + rider.in_specs,
        out_specs=[row, vec] + rider.out_specs,
        out_shape=[jax.ShapeDtypeStruct((s, 1024), F32), jax.ShapeDtypeStruct((1, 1024), F32)] + rider.out_shape,
        scratch_shapes=rider.scratch,
        compiler_params=_params(),
    )(dz0, x, dx1, g0, w_in0, *rider.parts)


def _weight_grad(a, b, n_blocks, split, name, rider=None):
    s, k = a.shape
    n = b.shape[1]
    tn = n // n_blocks
    w = tn // split
    ts = min(s, 2048)

    def body(a_ref, b_ref, o_ref):
        @pl.when(pl.program_id(1) == 0)
        def _():
            o_ref[...] = jnp.zeros_like(o_ref)

        res = _dot_tn(a_ref[...], b_ref[...])
        for q in range(split):
            o_ref[q] += res[:, q * w:(q + 1) * w]

    in_specs = [pl.BlockSpec((ts, k), lambda j, t: (t, 0)), pl.BlockSpec((ts, tn), lambda j, t: (t, j))]
    out_spec = pl.BlockSpec((split, k, w), lambda j, t: (j, 0, 0))
    out_shape = jax.ShapeDtypeStruct((n_blocks * split, k, w), F32)
    if rider is None:
        return pl.pallas_call(body, name=name, grid=(n_blocks, s // ts), in_specs=in_specs, out_specs=out_spec,
                              out_shape=out_shape, compiler_params=_params())(a, b)
    steps = s // ts
    return pl.pallas_call(
        rider.carried_by(body, 2, 1, n_blocks * steps, lambda: pl.program_id(0) * steps + pl.program_id(1)),
        name=name, grid=(n_blocks, steps), in_specs=in_specs + rider.in_specs,
        out_specs=[out_spec] + rider.out_specs, out_shape=[out_shape] + rider.out_shape,
        scratch_shapes=rider.scratch, compiler_params=_params(),
    )(a, b, *rider.parts)


def _row_tile(rows, cols):
    t = rows
    while t * cols * 4 > (1 << 21) and t % 32 == 0:
        t //= 2
    return t


def _add_sibling(where, g, theirs, name):
    _, _, rows, cols = g.shape
    t = _row_tile(rows, cols)

    def body(where_ref, g_ref, t_ref, o_ref):
        o_ref[...] = (g_ref[...] + t_ref[...]).astype(BF16)

    spec = pl.BlockSpec((None, t, cols), lambda s, i, p: (s, i, 0))
    return pl.pallas_call(
        body, name=name, out_shape=jax.ShapeDtypeStruct((4, rows, cols), BF16),
        grid_spec=pltpu.PrefetchScalarGridSpec(
            num_scalar_prefetch=1, grid=(4, rows // t),
            in_specs=[pl.BlockSpec((None, None, t, cols), lambda s, i, p: (s, p[1], i, 0)), spec], out_specs=spec),
        compiler_params=_params())(where, g, theirs)


def _sum_chips(where, g, theirs, slots, name):
    _, _, rows, cols = g.shape
    t = _row_tile(rows, cols)

    def body(where_ref, g_ref, t_ref, s_ref, o_ref):
        me = where_ref[0]
        own = g_ref[...] + t_ref[...]
        acc = jnp.where(me == 0, own, s_ref[0].astype(F32))
        for k in range(1, 4):
            acc = acc + jnp.where(me == k, own, s_ref[k].astype(F32))
        o_ref[...] = acc

    return pl.pallas_call(
        body, name=name, out_shape=jax.ShapeDtypeStruct((rows, cols), F32),
        grid_spec=pltpu.PrefetchScalarGridSpec(
            num_scalar_prefetch=1, grid=(rows // t,),
            in_specs=[pl.BlockSpec((None, None, t, cols), lambda i, p: (p[0], p[1], i, 0)),
                      pl.BlockSpec((None, t, cols), lambda i, p: (p[0], i, 0)),
                      pl.BlockSpec((4, t, cols), lambda i, p: (0, i, 0))],
            out_specs=pl.BlockSpec((t, cols), lambda i, p: (i, 0))),
        compiler_params=_params())(where, g, theirs, slots)


def _adamw_halves(where, w, own, theirs, m, v, name):
    rows, cols = own.shape
    t = _row_tile(rows, cols)
    per = rows // t

    def body(where_ref, w_ref, own_ref, th_ref, m_ref, v_ref, g_ref, d_ref, nm_ref, nv_ref):
        g = jnp.where(pl.program_id(0) == where_ref[1], own_ref[...], th_ref[...])
        g_ref[...] = g
        m2 = ADAM_B1 * m_ref[...] + (1.0 - ADAM_B1) * g
        v2 = ADAM_B2 * v_ref[...] + (1.0 - ADAM_B2) * (g * g)
        m_hat = m2 / (1.0 - ADAM_B1 ** ADAM_STEP)
        v_hat = v2 / (1.0 - ADAM_B2 ** ADAM_STEP)
        d_ref[...] = -ADAM_LR * (m_hat / (jnp.sqrt(v_hat) + ADAM_EPS) + ADAM_WD * w_ref[...])
        nm_ref[...] = m2
        nv_ref[...] = v2

    full = pl.BlockSpec((t, cols), lambda h, i, p: (h * per + i, 0))
    half = pl.BlockSpec((t, cols), lambda h, i, p: (i, 0))
    shp = jax.ShapeDtypeStruct(w.shape, F32)
    return pl.pallas_call(
        body, name=name, out_shape=[shp] * 4,
        grid_spec=pltpu.PrefetchScalarGridSpec(
            num_scalar_prefetch=1, grid=(2, per), in_specs=[full, half, half, full, full], out_specs=[full] * 4),
        compiler_params=_params())(where, w, own, theirs, m, v)


def _place_shard(where, w, cut, name):
    if cut.kind == "cols":
        r, n = cut.full_shape
        blk, grid = (256, n // 4), (r // 256,)
        src_map, dst_map = (lambda i, p: (i, 0)), (lambda i, p: (i, p[0]))
    elif cut.kind == "rows":
        r, n = cut.full_shape
        per = r // 4 // 256
        blk, grid = (256, n), (per,)
        src_map, dst_map = (lambda i, p: (i, 0)), (lambda i, p: (p[0] * per + i, 0))
    else:
        g, r, n = cut.full_shape
        blk, grid = (g, r // 4, n), (1,)
        src_map, dst_map = (lambda i, p: (0, 0, 0)), (lambda i, p: (0, p[0], 0))

    def body(where_ref, w_ref, o_ref):
        o_ref[...] = w_ref[...].astype(BF16)

    return pl.pallas_call(
        body, name=name, out_shape=jax.ShapeDtypeStruct(cut.full_shape, BF16),
        grid_spec=pltpu.PrefetchScalarGridSpec(
            num_scalar_prefetch=1, grid=grid, in_specs=[pl.BlockSpec(blk, src_map)],
            out_specs=pl.BlockSpec(blk, dst_map)),
        compiler_params=_params())(where, w)


def _sum_small(first, second, third):
    rows = second.shape[1]

    def body(a_ref, b_ref, c_ref, o_ref):
        top = a_ref[0] + c_ref[0] + b_ref[0, 0:8]
        rest = b_ref[0, 8:rows]
        for k in range(1, 8):
            top = top + (a_ref[k] + c_ref[k] + b_ref[k, 0:8])
            rest = rest + b_ref[k, 8:rows]
        o_ref[0:8] = top
        o_ref[8:rows] = rest

    return pl.pallas_call(
        body, name="sum_small", in_specs=[_whole(first.shape), _whole(second.shape), _whole(third.shape)],
        out_specs=_whole(second.shape[1:]), out_shape=jax.ShapeDtypeStruct(second.shape[1:], F32),
        compiler_params=_params())(first, second, third)


def _adamw(w, g, m, v, name):
    rows, cols = w.shape
    t = _row_tile(rows, cols)

    def body(w_ref, g_ref, m_ref, v_ref, d_ref, nm_ref, nv_ref):
        g = g_ref[...]
        m2 = ADAM_B1 * m_ref[...] + (1.0 - ADAM_B1) * g
        v2 = ADAM_B2 * v_ref[...] + (1.0 - ADAM_B2) * (g * g)
        m_hat = m2 / (1.0 - ADAM_B1 ** ADAM_STEP)
        v_hat = v2 / (1.0 - ADAM_B2 ** ADAM_STEP)
        d_ref[...] = -ADAM_LR * (m_hat / (jnp.sqrt(v_hat) + ADAM_EPS) + ADAM_WD * w_ref[...])
        nm_ref[...] = m2
        nv_ref[...] = v2

    spec = pl.BlockSpec((t, cols), lambda i: (i, 0))
    shp = jax.ShapeDtypeStruct(w.shape, F32)
    return pl.pallas_call(body, name=name, grid=(rows // t,), in_specs=[spec] * 4, out_specs=[spec] * 3,
                          out_shape=[shp] * 3, compiler_params=_params())(w, g, m, v)


def _place():
    x, y, c = lax.axis_index("x"), lax.axis_index("y"), lax.axis_index("c")
    chips = [(1 - x, y), (x, 1 - y), (1 - x, 1 - y)]
    return x, y, c, chips


class _Sharded:
    def __init__(self, kind, full_shape):
        self.kind = kind
        self.full_shape = full_shape

    def in_full(self, ref, s, h):
        if self.kind == "cols":
            r, n = self.full_shape
            return ref.at[pl.ds(h * (r // 2), r // 2), pl.ds(pl.multiple_of(s * (n // 4), 128), n // 4)]
        if self.kind == "rows":
            r, _ = self.full_shape
            return ref.at[pl.ds(pl.multiple_of(s * (r // 4) + h * (r // 8), 8), r // 8), :]
        g, r, _ = self.full_shape
        return ref.at[pl.ds(h * (g // 2), g // 2), pl.ds(pl.multiple_of(s * (r // 4), 16), r // 4), :]


def _remote(src, dst, send_sem, recv_sem, to):
    return pltpu.make_async_remote_copy(src_ref=src, dst_ref=dst, send_sem=send_sem, recv_sem=recv_sem,
                                        device_id=to, device_id_type=MESH)


def _start_remote(src, dst, send_sem, recv_sem, to):
    cp = _remote(src, dst, send_sem, recv_sem, to)
    cp.start()
    return cp


class _Gather:
    def __init__(self, fulls, cuts):
        n = len(fulls)
        self.fulls, self.cuts = list(fulls), list(cuts)
        self.in_specs = [ANY] * n
        self.out_specs = [ANY] * n
        self.out_shape = [jax.ShapeDtypeStruct(cut.full_shape, BF16) for cut in cuts]
        self.scratch = [pltpu.SemaphoreType.DMA((6 * n,)), pltpu.SemaphoreType.DMA((6 * n,))]

    def _step(self, step, src, out, send_sems, recv_sems):
        n, cuts = len(self.fulls), self.cuts
        x, y, c, chips = _place()
        me = 2 * x + y

        def ends(w, s, h, from_src):
            dst = cuts[w].in_full(out[w], s, h)
            return (cuts[w].in_full(src[w], s, h) if from_src else dst), dst

        for w in range(n):
            for j, chip in enumerate(chips):
                s = 2 * chip[0] + chip[1]
                k, k2 = 3 * w + j, 3 * n + 3 * w + j
                if step == "send":
                    _start_remote(*ends(w, me, c, True), send_sems.at[k], recv_sems.at[k], (*chip, c))
                elif step == "pass_on":
                    _remote(*ends(w, s, c, False), send_sems.at[k], recv_sems.at[k], (x, y, c)).wait_recv()
                    _start_remote(*ends(w, s, c, False), send_sems.at[k2], recv_sems.at[k2], (x, y, 1 - c))
                else:
                    _remote(*ends(w, s, 1 - c, False), send_sems.at[k2], recv_sems.at[k2], (x, y, c)).wait_recv()
                    _remote(*ends(w, me, c, True), send_sems.at[k], recv_sems.at[k], (x, y, c)).wait_send()
                    _remote(*ends(w, s, c, False), send_sems.at[k2], recv_sems.at[k2], (x, y, c)).wait_send()

    def carried_by(self, body, n_in, n_out, n_steps, step_index=lambda: pl.program_id(0), late=False):
        k = len(self.fulls)

        def carrier(*refs):
            ins, src = refs[:n_in], refs[n_in:n_in + k]
            outs, out = refs[n_in + k:n_in + k + n_out], refs[n_in + k + n_out:n_in + 2 * k + n_out]
            sems = refs[n_in + 2 * k + n_out:]

            def at_step(step, at):
                @pl.when(step_index() == at)
                def _():
                    self._step(step, src, out, *sems)

            at_step("send", 0)
            if not late:
                at_step("pass_on", 3 * n_steps // 4)
            body(*ins, *outs)
            if late:
                at_step("pass_on", n_steps - 1)
            at_step("finish", n_steps - 1)

        return carrier


def _exchange_halves(grads, name):
    n = len(grads)

    def body(*refs):
        g = refs[:n]
        theirs = refs[n:2 * n]
        send_sems, recv_sems = refs[2 * n:]
        x, y, c, _ = _place()
        sends = [_start_remote(g[w].at[:, 1 - c], theirs[w], send_sems.at[w], recv_sems.at[w], (x, y, 1 - c))
                 for w in range(n)]
        for w in range(n):
            _remote(g[w].at[:, 1 - c], theirs[w], send_sems.at[w], recv_sems.at[w], (x, y, c)).wait_recv()
        for cp in sends:
            cp.wait_send()

    return pl.pallas_call(
        body, name=name,
        in_specs=[ANY] * n, out_specs=[ANY] * n,
        out_shape=[jax.ShapeDtypeStruct((4,) + g.shape[2:], F32) for g in grads],
        scratch_shapes=[pltpu.SemaphoreType.DMA((n,)), pltpu.SemaphoreType.DMA((n,))],
        compiler_params=pltpu.CompilerParams(has_side_effects=True),
    )(*grads)


def _gather_small(small_ref, gathered, send_sems, recv_sems, first_sem, local_sem, start):
    x, y, c, _ = _place()
    me = 4 * x + 2 * y + c
    flips = [(fx, fy, fc) for fx in range(2) for fy in range(2) for fc in range(2)][1:]
    own = pltpu.make_async_copy(small_ref, gathered.at[me], local_sem)
    if start:
        own.start()
    else:
        own.wait()
    for k, (fx, fy, fc) in enumerate(flips):
        peer = (x + fx - 2 * x * fx, y + fy - 2 * y * fy, c + fc - 2 * c * fc)
        sems = (send_sems.at[first_sem + k], recv_sems.at[first_sem + k])
        if start:
            _start_remote(small_ref, gathered.at[me], *sems, peer)
        else:
            cp = _remote(small_ref, gathered.at[4 * peer[0] + 2 * peer[1] + peer[2]], *sems, (x, y, c))
            cp.wait_recv()
            cp.wait_send()


class _Rider:
    def __init__(self, kind, parts, small=None):
        n = len(parts)
        self.kind, self.n = kind, n
        self.per = 3 if kind == "scatter" else 1
        self.parts = list(parts) + ([] if small is None else [small])
        k = len(self.parts)
        self.in_specs = [ANY] * k
        self.out_specs = [ANY] * k
        dtype = lambda a: a.dtype if kind == "scatter" else F32
        self.out_shape = [jax.ShapeDtypeStruct((4,) + a.shape[-2:], dtype(a)) for a in parts]
        if small is not None:
            self.out_shape.append(jax.ShapeDtypeStruct((8,) + small.shape, small.dtype))
        self.scratch = [pltpu.SemaphoreType.DMA((self.per * n + 7,)), pltpu.SemaphoreType.DMA((self.per * n + 7,)),
                        pltpu.SemaphoreType.DMA]

    def _copies(self, p, out, send_sems, recv_sems, local_sem, start):
        x, y, c, chips = _place()
        me = 2 * x + y
        n = self.n
        if len(self.parts) > n:
            _gather_small(p[n], out[n], send_sems, recv_sems, self.per * n, local_sem, start)
        for w in range(n):
            if self.kind == "exchange":
                ends = [(p[w].at[:, 1 - c], out[w], out[w], (x, y, 1 - c))]
            else:
                ends = [(p[w].at[2 * cx + cy], out[w].at[me], out[w].at[2 * cx + cy], (cx, cy, c)) for cx, cy in chips]
            for j, (src, dst_there, dst_here, to) in enumerate(ends):
                sems = (send_sems.at[self.per * w + j], recv_sems.at[self.per * w + j])
                if start:
                    _start_remote(src, dst_there, *sems, to)
                else:
                    cp = _remote(src, dst_here, *sems, (x, y, c))
                    cp.wait_recv()
                    cp.wait_send()

    def carried_by(self, body, n_in, n_out, n_steps, step_index=lambda: pl.program_id(0)):
        k = len(self.parts)

        def carrier(*refs):
            ins, mine = refs[:n_in], refs[n_in:n_in + k]
            outs, theirs = refs[n_in + k:n_in + k + n_out], refs[n_in + k + n_out:n_in + 2 * k + n_out]
            sems = refs[n_in + 2 * k + n_out:]

            @pl.when(step_index() == 0)
            def _():
                self._copies(mine, theirs, *sems, start=True)

            body(*ins, *outs)

            @pl.when(step_index() == n_steps - 1)
            def _():
                self._copies(mine, theirs, *sems, start=False)

        return carrier


def _share_halves(halves, small):
    n = len(halves)

    def body(*refs):
        hv = refs[:n]
        small_ref = refs[n]
        out = refs[n + 1:2 * n + 1]
        gathered = refs[2 * n + 1]
        send_sems, recv_sems, local_sem = refs[2 * n + 2:]
        x, y, c, _ = _place()
        sends = [_start_remote(hv[w], out[w], send_sems.at[w], recv_sems.at[w], (x, y, 1 - c)) for w in range(n)]
        _gather_small(small_ref, gathered, send_sems, recv_sems, n, local_sem, True)
        for w in range(n):
            _remote(hv[w], out[w], send_sems.at[w], recv_sems.at[w], (x, y, c)).wait_recv()
        for cp in sends:
            cp.wait_send()
        _gather_small(small_ref, gathered, send_sems, recv_sems, n, local_sem, False)

    return pl.pallas_call(
        body, name="share_halves",
        in_specs=[ANY] * (n + 1), out_specs=[ANY] * (n + 1),
        out_shape=[jax.ShapeDtypeStruct(a.shape, F32) for a in halves] + [jax.ShapeDtypeStruct((8,) + small.shape, F32)],
        scratch_shapes=[pltpu.SemaphoreType.DMA((n + 7,)), pltpu.SemaphoreType.DMA((n + 7,)),
                        pltpu.SemaphoreType.DMA],
        compiler_params=pltpu.CompilerParams(has_side_effects=True),
    )(*halves, small)


def _pack_small(vecs, ws, bs, sink, extra=None):
    ws = jnp.zeros((64, 1024), F32) if ws is None else ws.reshape(64, 1024)
    bs = jnp.zeros((1, 512), F32) if bs is None else bs.reshape(1, 512)
    sink = jnp.zeros((1, 16), F32) if sink is None else sink.reshape(1, 16)
    extra = jnp.zeros((1, 1024), F32) if extra is None else extra.reshape(1, 1024)
    top = jnp.concatenate(
        [v.reshape(1, 1024) for v in vecs]
        + [jnp.pad(bs, ((0, 0), (0, 512))), jnp.pad(sink, ((0, 0), (0, 1008))), extra], axis=0)
    return jnp.concatenate([top, ws, jnp.zeros((8, 1024), F32)], axis=0)


def _unpack_small(p):
    vecs = [p[k] for k in range(5)]
    return vecs, p[8:72].reshape(4, 128, 128), p[5, :512].reshape(4, 128), p[6, :16]


def kernel(x, norm_0, w_in_0, a_v_norm_0, a_spatial_w_0, a_spatial_b_0, b_group_w_0, b_scale_0, w_out_0, norm_1, w_in_1, sink_1, w_out_1, final_norm, loss_target, m_norm_0, m_w_in_0, m_a_v_norm_0, m_a_spatial_w_0, m_a_spatial_b_0, m_b_group_w_0, m_b_scale_0, m_w_out_0, m_norm_1, m_w_in_1, m_sink_1, m_w_out_1, m_final_norm, v_norm_0, v_w_in_0, v_a_v_norm_0, v_a_spatial_w_0, v_a_spatial_b_0, v_b_group_w_0, v_b_scale_0, v_w_out_0, v_norm_1, v_w_in_1, v_sink_1, v_w_out_1, v_final_norm):
    s = x.shape[1]
    xs = x.reshape(s, D_MODEL)
    target = loss_target.reshape(s, D_MODEL)

    cuts = [_Sharded("cols", (1024, 5120)), _Sharded("rows", (2048, 1024)), _Sharded("cols", (1024, 2560)),
            _Sharded("rows", (1024, 1024)), _Sharded("mid", (4, 256, 256))]
    big_w = [w_in_0, w_out_0, w_in_1, w_out_1, b_group_w_0]
    big_m = [m_w_in_0, m_w_out_0, m_w_in_1, m_w_out_1, m_b_group_w_0]
    big_v = [v_w_in_0, v_w_out_0, v_w_in_1, v_w_out_1, v_b_group_w_0]
    where = jnp.stack([2 * lax.axis_index("x") + lax.axis_index("y"), lax.axis_index("c")]).astype(jnp.int32)
    placed = [_place_shard(where, w, cut, f"place_shard{k}") for k, (w, cut) in enumerate(zip(big_w, cuts))]
    cx, cy = lax.axis_index("x"), lax.axis_index("y")
    order = jnp.stack([2 * cx + cy, 2 * (1 - cx) + cy, 2 * cx + 1 - cy, 2 * (1 - cx) + 1 - cy]).astype(jnp.int32)

    row = lambda v: v.reshape(1, 1024)
    ws16 = a_spatial_w_0.astype(BF16)
    bsx = jnp.repeat(a_spatial_b_0.T, 256, axis=1)
    band = _band_matrices(TOK)[0]
    rope = _rope_tables(s)

    h0, z0, w_in0 = _in_proj0_own(order, xs, row(norm_0), w_in_0.astype(BF16), _Gather(placed[:1], cuts[:1]))
    z0, w_out0, wg = _in_proj0_rest(order, h0, w_in0, z0, _Gather([placed[1], placed[4]], [cuts[1], cuts[4]]))
    cat, x1, w_in1, w_out1 = _mix0_fwd(xs, z0, w_out0, row(a_v_norm_0), ws16, bsx, wg, row(b_scale_0), band,
                                       _Gather(placed[2:4], cuts[2:4]))
    h1, q, k, v, gate = _in_proj1(x1, row(norm_1), w_in1, rope)
    kpad = jnp.pad(k, ((ATTN_WINDOW, ATTN_WINDOW), (0, 0)))
    vpad = jnp.pad(v, ((ATTN_WINDOW, ATTN_WINDOW), (0, 0)))
    o, lse = _attn_fwd(q, kpad, vpad, sink_1)
    y1, dx2, do, dgate, loss_lanes, g_final, dx2h = _tail(x1, o, gate, target, w_out1, row(final_norm))

    dq, dkpad, dvpad, dsink = _attn_bwd(q, kpad, vpad, sink_1, o, lse, do, rope)
    dz1, dx1, g_norm1, dx1h = _in_proj1_bwd(dq, dkpad, dvpad, dgate, x1, dx2, row(norm_1), w_in1, rope)

    g_w_in1 = _weight_grad(h1, dz1, 2, 2, "grad_w_in1").reshape(4, 2, 512, 640)
    g_w_out1 = _weight_grad(y1, dx2h, 1, 1, "grad_w_out1").reshape(4, 2, 128, 1024)
    g_w_out0 = _weight_grad(cat, dx1h, 1, 1, "grad_w_out0").reshape(4, 2, 256, 1024)
    first = [g_w_out0, g_w_in1, g_w_out1]
    zero = jnp.zeros((1024,), F32)
    small1 = _pack_small([zero, zero, zero, g_norm1, g_final], None, None, dsink[0, :16])[:8]
    dz0, dpn, d_ws, d_bs, d_gv, d_scale, d_wg, *theirs1, small1_all = _mix0_bwd(
        dx1h, z0, w_out0, row(a_v_norm_0), ws16, bsx, wg, row(b_scale_0), band, _Rider("exchange", first, small1))
    parts1 = [_add_sibling(where, g, t, f"add_sibling1_{k}") for k, (g, t) in enumerate(zip(first, theirs1))]
    dz0 = _fill_pooled_grad(dz0, dpn)
    g_w_in0, *slots1 = _weight_grad(h0, dz0, 4, 1, "grad_w_in0", _Rider("scatter", parts1))
    g_w_in0 = g_w_in0.reshape(4, 2, 512, 1280)
    g_wg = d_wg.reshape(2, 2, 4, 64, 256).transpose(2, 0, 1, 3, 4).reshape(4, 2, 128, 256)
    second = [g_w_in0, g_wg]
    theirs2 = _exchange_halves(second, "exchange_halves2")
    parts2 = [_add_sibling(where, g, t, f"add_sibling2_{k}") for k, (g, t) in enumerate(zip(second, theirs2))]
    small2 = _pack_small([zero, d_gv, d_scale, zero, zero], d_ws, d_bs[:, :4].T, None)
    grad_x, g_norm0, *slots2, small2_all = _in_proj0_bwd(
        dz0, xs, dx1, row(norm_0), w_in0, _Rider("scatter", parts2, small2))

    grads = [g_w_in0, g_w_out0, g_w_in1, g_w_out1, g_wg]
    theirs = [theirs2[0], theirs1[0], theirs1[1], theirs1[2], theirs2[1]]
    slots = [slots2[0], slots1[0], slots1[1], slots1[2], slots2[1]]
    n = len(grads)
    reduced = [_sum_chips(where, grads[w], theirs[w], slots[w], f"sum_chips{w}") for w in range(n)]
    small3 = _pack_small([g_norm0, zero, zero, zero, zero], None, None, None, loss_lanes)[:8]
    *from_sibling, small3_all = _share_halves(reduced, small3)

    out_g, out_d, out_m, out_v = {}, {}, {}, {}
    names = ["w_in_0", "w_out_0", "w_in_1", "w_out_1", "b_group_w_0"]
    for w in range(n):
        shape = big_w[w].shape
        two_d = (-1, shape[-1])
        outs = _adamw_halves(where, big_w[w].reshape(two_d), reduced[w], from_sibling[w], big_m[w].reshape(two_d),
                             big_v[w].reshape(two_d), f"adamw{w}")
        out_g[names[w]], out_d[names[w]], out_m[names[w]], out_v[names[w]] = (a.reshape(shape) for a in outs)

    g_small = _sum_small(small1_all, small2_all, small3_all)
    small_names = ["norm_0", "a_v_norm_0", "b_scale_0", "norm_1", "final_norm"]
    pack = lambda vecs, ws_, bs_, sk: _pack_small(vecs, ws_, bs_, sk)
    w_small = pack([norm_0, a_v_norm_0, b_scale_0, norm_1, final_norm], a_spatial_w_0, a_spatial_b_0, sink_1)
    m_small = pack([m_norm_0, m_a_v_norm_0, m_b_scale_0, m_norm_1, m_final_norm], m_a_spatial_w_0,
                   m_a_spatial_b_0, m_sink_1)
    v_small = pack([v_norm_0, v_a_v_norm_0, v_b_scale_0, v_norm_1, v_final_norm], v_a_spatial_w_0,
                   v_a_spatial_b_0, v_sink_1)
    d_small, nm_small, nv_small = _adamw(w_small, g_small, m_small, v_small, "adamw_small")
    for store, packed in ((out_g, g_small), (out_d, d_small), (out_m, nm_small), (out_v, nv_small)):
        vecs, ws_, bs_, sk = _unpack_small(packed)
        for name, vec in zip(small_names, vecs):
            store[name] = vec
        store["a_spatial_w_0"], store["a_spatial_b_0"], store["sink_1"] = ws_, bs_, sk

    loss = jnp.sum(g_small[7])
    order = ["norm_0", "w_in_0", "a_v_norm_0", "a_spatial_w_0", "a_spatial_b_0", "b_group_w_0", "b_scale_0",
             "w_out_0", "norm_1", "w_in_1", "sink_1", "w_out_1", "final_norm"]
    return (loss, grad_x.reshape(1, s, D_MODEL), *[out_g[k] for k in order], *[out_d[k] for k in order],
            *[out_m[k] for k in order], *[out_v[k] for k in order])
```
